```python
import math
import jax, jax.numpy as jnp
from jax import lax
import numpy as np


D_MODEL = 1024
BATCH = 8
SEQ = 8192
DEPTH = 4

N_MIXERS = 2
HEAD_DIM = 64
A_Q_HEADS = D_MODEL // HEAD_DIM
A_KV_HEADS = A_Q_HEADS // 4
A_GROUP = A_Q_HEADS // A_KV_HEADS
WINDOW = 128
B_HEADS = D_MODEL // HEAD_DIM
BLOCK = 128
REL_BUCKETS = 32
REL_MAX_DIST = 128
D_FF = 2816
EPS = 1e-6
FORGET_BIAS_INIT = 2.0
N_A_LAYERS = (DEPTH + 1) // 2
N_B_LAYERS = DEPTH // 2
A_IN = (A_Q_HEADS + 2 * A_KV_HEADS) * HEAD_DIM
B_IN = 3 * B_HEADS * HEAD_DIM + B_HEADS

kernel_name = 'hybrid_swa_sink_fox_macaron_adaln'


def rmsnorm(x, g):
    xf = x.astype(jnp.float32)
    y = xf * lax.rsqrt(jnp.mean(xf * xf, axis=-1, keepdims=True) + EPS) * g.astype(jnp.float32)
    return y.astype(x.dtype)


def modulate(x, g, shift, scale):
    return rmsnorm(x, g) * (1 + scale[:, None, :]) + shift[:, None, :]


def swiglu(h, w13, w2):
    gate, up = jnp.split(h @ w13, 2, axis=-1)
    return (jax.nn.silu(gate) * up) @ w2


def rel_bucket(dist):
    n = jnp.maximum(dist, 0)
    max_exact = REL_BUCKETS // 2
    nf = jnp.maximum(n, 1).astype(jnp.float32)
    large = max_exact + (jnp.log(nf / max_exact) / math.log(REL_MAX_DIST / max_exact)
                         * (REL_BUCKETS - max_exact)).astype(jnp.int32)
    large = jnp.minimum(large, REL_BUCKETS - 1)
    return jnp.where(n < max_exact, n, large)


def swa_mixer(h, w_in, w_out, q_g, k_g, sink, rel_bias):
    B, S, _ = h.shape
    nblk = S // BLOCK
    proj = h @ w_in
    q, k, v = jnp.split(proj, [A_Q_HEADS * HEAD_DIM, (A_Q_HEADS + A_KV_HEADS) * HEAD_DIM], axis=-1)
    q = rmsnorm(q.reshape(B, S, A_KV_HEADS, A_GROUP, HEAD_DIM), q_g) * (HEAD_DIM ** -0.5)
    k = rmsnorm(k.reshape(B, S, A_KV_HEADS, HEAD_DIM), k_g)
    v = v.reshape(B, S, A_KV_HEADS, HEAD_DIM)
    pad = ((0, 0), (BLOCK, 0), (0, 0), (0, 0))
    kp = jnp.pad(k, pad)
    vp = jnp.pad(v, pad)
    qi = jnp.arange(BLOCK)[:, None] + BLOCK
    kj = jnp.arange(2 * BLOCK)[None, :]
    dist = qi - kj
    band = (dist >= 0) & (dist < WINDOW)
    bias = jnp.transpose(rel_bias[rel_bucket(dist)], (2, 0, 1)).astype(jnp.float32)
    bias = bias.reshape(A_KV_HEADS, A_GROUP, BLOCK, 2 * BLOCK)
    sink_f = sink.astype(jnp.float32).reshape(A_KV_HEADS, A_GROUP)[None, :, :, None]

    def block(i):
        start = i * BLOCK
        qb = lax.dynamic_slice_in_dim(q, start, BLOCK, axis=1)
        kb = lax.dynamic_slice_in_dim(kp, start, 2 * BLOCK, axis=1)
        vb = lax.dynamic_slice_in_dim(vp, start, 2 * BLOCK, axis=1)
        s = jnp.einsum('bqhgd,bkhd->bhgqk', qb, kb).astype(jnp.float32) + bias
        valid = band & (start - BLOCK + kj >= 0)
        s = jnp.where(valid, s, -jnp.inf)
        m = jnp.maximum(jnp.max(s, axis=-1), sink_f)
        p = jnp.exp(s - m[..., None])
        denom = jnp.sum(p, axis=-1) + jnp.exp(sink_f - m)
        p = p / denom[..., None]
        return jnp.einsum('bhgqk,bkhd->bqhgd', p.astype(vb.dtype), vb)

    out = lax.map(block, jnp.arange(nblk))
    out = jnp.moveaxis(out, 0, 1).reshape(B, S, A_Q_HEADS * HEAD_DIM)
    return out @ w_out


def fox_mixer(h, w_in, w_out, b_f, q_g, k_g):
    B, S, _ = h.shape
    nblk = S // BLOCK
    HD = B_HEADS * HEAD_DIM
    proj = h @ w_in
    q, k, v, fl = jnp.split(proj, [HD, 2 * HD, 3 * HD], axis=-1)
    q = rmsnorm(q.reshape(B, S, B_HEADS, HEAD_DIM), q_g) * (HEAD_DIM ** -0.5)
    k = rmsnorm(k.reshape(B, S, B_HEADS, HEAD_DIM), k_g)
    v = v.reshape(B, S, B_HEADS, HEAD_DIM)
    log_f = jax.nn.log_sigmoid(fl.astype(jnp.float32) + b_f.astype(jnp.float32))
    F = jnp.transpose(jnp.cumsum(log_f, axis=1), (0, 2, 1))
    kpos = jnp.arange(S)

    def block(i):
        start = i * BLOCK
        qb = lax.dynamic_slice_in_dim(q, start, BLOCK, axis=1)
        Fq = lax.dynamic_slice_in_dim(F, start, BLOCK, axis=2)
        s = jnp.einsum('bqhd,bkhd->bhqk', qb, k).astype(jnp.float32)
        s = s + Fq[..., None] - F[:, :, None, :]
        qpos = start + jnp.arange(BLOCK)
        s = jnp.where(kpos[None, :] <= qpos[:, None], s, -jnp.inf)
        p = jax.nn.softmax(s, axis=-1)
        return jnp.einsum('bhqk,bkhd->bqhd', p.astype(v.dtype), v)

    out = lax.map(block, jnp.arange(nblk))
    out = jnp.moveaxis(out, 0, 1).reshape(B, S, HD)
    return out @ w_out


def _fwd_setup_inputs(seed: int = 0) -> dict:
    key = jax.random.key(seed)
    ks = jax.random.split(key, 20)
    D = D_MODEL

    def nrm(k, shape, s):
        return jax.random.normal(k, shape, jnp.float32) * s

    return {
        'x': nrm(ks[0], (BATCH, SEQ, D), 1.0),
        'c': nrm(ks[1], (BATCH, D), 1.0),
        'ada_w': nrm(ks[2], (DEPTH, D, 9 * D), 0.5 * D ** -0.5),
        'ada_b': nrm(ks[3], (DEPTH, 9 * D), 0.02),
        'norm_g': 1.0 + nrm(ks[4], (DEPTH, 3, D), 0.02),
        'ffn_w13': nrm(ks[5], (DEPTH, 2, D, 2 * D_FF), D ** -0.5),
        'ffn_w2': nrm(ks[6], (DEPTH, 2, D_FF, D), D_FF ** -0.5),
        'rel_bias': nrm(ks[7], (REL_BUCKETS, A_Q_HEADS), 0.5),
        'swa_w_in': nrm(ks[8], (N_A_LAYERS, D, A_IN), D ** -0.5),
        'swa_w_out': nrm(ks[9], (N_A_LAYERS, A_Q_HEADS * HEAD_DIM, D), (A_Q_HEADS * HEAD_DIM) ** -0.5),
        'swa_q_g': 1.0 + nrm(ks[10], (N_A_LAYERS, HEAD_DIM), 0.02),
        'swa_k_g': 1.0 + nrm(ks[11], (N_A_LAYERS, HEAD_DIM), 0.02),
        'swa_sink': nrm(ks[12], (N_A_LAYERS, A_Q_HEADS), 0.5),
        'fox_w_in': nrm(ks[13], (N_B_LAYERS, D, B_IN), D ** -0.5),
        'fox_w_out': nrm(ks[14], (N_B_LAYERS, B_HEADS * HEAD_DIM, D), (B_HEADS * HEAD_DIM) ** -0.5),
        'fox_b_f': FORGET_BIAS_INIT + nrm(ks[15], (N_B_LAYERS, B_HEADS), 0.1),
        'fox_q_g': 1.0 + nrm(ks[16], (N_B_LAYERS, HEAD_DIM), 0.02),
        'fox_k_g': 1.0 + nrm(ks[17], (N_B_LAYERS, HEAD_DIM), 0.02),
    }


def _fwd_reference(x, c, ada_w, ada_b, norm_g, ffn_w13, ffn_w2, rel_bias,
              swa_w_in, swa_w_out, swa_q_g, swa_k_g, swa_sink,
              fox_w_in, fox_w_out, fox_b_f, fox_q_g, fox_k_g):
    B = x.shape[0]
    c_act = jax.nn.silu(c)
    for layer in range(DEPTH):
        mod = (c_act @ ada_w[layer] + ada_b[layer]).reshape(B, 3, 3, D_MODEL)
        h = modulate(x, norm_g[layer, 0], mod[:, 0, 0], mod[:, 0, 1])
        x = x + 0.5 * mod[:, 0, 2][:, None, :] * swiglu(h, ffn_w13[layer, 0], ffn_w2[layer, 0])
        h = modulate(x, norm_g[layer, 1], mod[:, 1, 0], mod[:, 1, 1])
        j = layer // N_MIXERS
        if layer % N_MIXERS == 0:
            y = swa_mixer(h, swa_w_in[j], swa_w_out[j], swa_q_g[j], swa_k_g[j], swa_sink[j], rel_bias)
        else:
            y = fox_mixer(h, fox_w_in[j], fox_w_out[j], fox_b_f[j], fox_q_g[j], fox_k_g[j])
        x = x + mod[:, 1, 2][:, None, :] * y
        h = modulate(x, norm_g[layer, 2], mod[:, 2, 0], mod[:, 2, 1])
        x = x + 0.5 * mod[:, 2, 2][:, None, :] * swiglu(h, ffn_w13[layer, 1], ffn_w2[layer, 1])
    return x


import jax as _jax
import jax.numpy as _jnp

TWIN_FORMAT = 'train_step'
FWD_PARAMS = ['x', 'c', 'ada_w', 'ada_b', 'norm_g', 'ffn_w13', 'ffn_w2', 'rel_bias', 'swa_w_in', 'swa_w_out', 'swa_q_g', 'swa_k_g', 'swa_sink', 'fox_w_in', 'fox_w_out', 'fox_b_f', 'fox_q_g', 'fox_k_g']
TWIN_WEIGHTS = ['ada_w', 'ada_b', 'norm_g', 'ffn_w13', 'ffn_w2', 'rel_bias', 'swa_w_in', 'swa_w_out', 'swa_q_g', 'swa_k_g', 'swa_sink', 'fox_w_in', 'fox_w_out', 'fox_b_f', 'fox_q_g', 'fox_k_g']
TWIN_DIFF_INPUT = 'x'
TWIN_INPUTS = ['x', 'c', 'ada_w', 'ada_b', 'norm_g', 'ffn_w13', 'ffn_w2', 'rel_bias', 'swa_w_in', 'swa_w_out', 'swa_q_g', 'swa_k_g', 'swa_sink', 'fox_w_in', 'fox_w_out', 'fox_b_f', 'fox_q_g', 'fox_k_g', 'loss_target', 'm_ada_w', 'm_ada_b', 'm_norm_g', 'm_ffn_w13', 'm_ffn_w2', 'm_rel_bias', 'm_swa_w_in', 'm_swa_w_out', 'm_swa_q_g', 'm_swa_k_g', 'm_swa_sink', 'm_fox_w_in', 'm_fox_w_out', 'm_fox_b_f', 'm_fox_q_g', 'm_fox_k_g', 'v_ada_w', 'v_ada_b', 'v_norm_g', 'v_ffn_w13', 'v_ffn_w2', 'v_rel_bias', 'v_swa_w_in', 'v_swa_w_out', 'v_swa_q_g', 'v_swa_k_g', 'v_swa_sink', 'v_fox_w_in', 'v_fox_w_out', 'v_fox_b_f', 'v_fox_q_g', 'v_fox_k_g']
TWIN_OUTPUTS = ['loss', 'grad_x', 'grad_ada_w', 'grad_ada_b', 'grad_norm_g', 'grad_ffn_w13', 'grad_ffn_w2', 'grad_rel_bias', 'grad_swa_w_in', 'grad_swa_w_out', 'grad_swa_q_g', 'grad_swa_k_g', 'grad_swa_sink', 'grad_fox_w_in', 'grad_fox_w_out', 'grad_fox_b_f', 'grad_fox_q_g', 'grad_fox_k_g', 'delta_ada_w', 'delta_ada_b', 'delta_norm_g', 'delta_ffn_w13', 'delta_ffn_w2', 'delta_rel_bias', 'delta_swa_w_in', 'delta_swa_w_out', 'delta_swa_q_g', 'delta_swa_k_g', 'delta_swa_sink', 'delta_fox_w_in', 'delta_fox_w_out', 'delta_fox_b_f', 'delta_fox_q_g', 'delta_fox_k_g', 'new_m_ada_w', 'new_m_ada_b', 'new_m_norm_g', 'new_m_ffn_w13', 'new_m_ffn_w2', 'new_m_rel_bias', 'new_m_swa_w_in', 'new_m_swa_w_out', 'new_m_swa_q_g', 'new_m_swa_k_g', 'new_m_swa_sink', 'new_m_fox_w_in', 'new_m_fox_w_out', 'new_m_fox_b_f', 'new_m_fox_q_g', 'new_m_fox_k_g', 'new_v_ada_w', 'new_v_ada_b', 'new_v_norm_g', 'new_v_ffn_w13', 'new_v_ffn_w2', 'new_v_rel_bias', 'new_v_swa_w_in', 'new_v_swa_w_out', 'new_v_swa_q_g', 'new_v_swa_k_g', 'new_v_swa_sink', 'new_v_fox_w_in', 'new_v_fox_w_out', 'new_v_fox_b_f', 'new_v_fox_q_g', 'new_v_fox_k_g']
TWIN_LEAF_KINDS = {'loss': 'loss', 'grad_x': 'grad_x', 'grad_ada_w': 'grad_w', 'grad_ada_b': 'grad_w', 'grad_norm_g': 'grad_w', 'grad_ffn_w13': 'grad_w', 'grad_ffn_w2': 'grad_w', 'grad_rel_bias': 'grad_w', 'grad_swa_w_in': 'grad_w', 'grad_swa_w_out': 'grad_w', 'grad_swa_q_g': 'grad_w', 'grad_swa_k_g': 'grad_w', 'grad_swa_sink': 'grad_w', 'grad_fox_w_in': 'grad_w', 'grad_fox_w_out': 'grad_w', 'grad_fox_b_f': 'grad_w', 'grad_fox_q_g': 'grad_w', 'grad_fox_k_g': 'grad_w', 'delta_ada_w': 'delta_w', 'delta_ada_b': 'delta_w', 'delta_norm_g': 'delta_w', 'delta_ffn_w13': 'delta_w', 'delta_ffn_w2': 'delta_w', 'delta_rel_bias': 'delta_w', 'delta_swa_w_in': 'delta_w', 'delta_swa_w_out': 'delta_w', 'delta_swa_q_g': 'delta_w', 'delta_swa_k_g': 'delta_w', 'delta_swa_sink': 'delta_w', 'delta_fox_w_in': 'delta_w', 'delta_fox_w_out': 'delta_w', 'delta_fox_b_f': 'delta_w', 'delta_fox_q_g': 'delta_w', 'delta_fox_k_g': 'delta_w', 'new_m_ada_w': 'new_m', 'new_m_ada_b': 'new_m', 'new_m_norm_g': 'new_m', 'new_m_ffn_w13': 'new_m', 'new_m_ffn_w2': 'new_m', 'new_m_rel_bias': 'new_m', 'new_m_swa_w_in': 'new_m', 'new_m_swa_w_out': 'new_m', 'new_m_swa_q_g': 'new_m', 'new_m_swa_k_g': 'new_m', 'new_m_swa_sink': 'new_m', 'new_m_fox_w_in': 'new_m', 'new_m_fox_w_out': 'new_m', 'new_m_fox_b_f': 'new_m', 'new_m_fox_q_g': 'new_m', 'new_m_fox_k_g': 'new_m', 'new_v_ada_w': 'new_v', 'new_v_ada_b': 'new_v', 'new_v_norm_g': 'new_v', 'new_v_ffn_w13': 'new_v', 'new_v_ffn_w2': 'new_v', 'new_v_rel_bias': 'new_v', 'new_v_swa_w_in': 'new_v', 'new_v_swa_w_out': 'new_v', 'new_v_swa_q_g': 'new_v', 'new_v_swa_k_g': 'new_v', 'new_v_swa_sink': 'new_v', 'new_v_fox_w_in': 'new_v', 'new_v_fox_w_out': 'new_v', 'new_v_fox_b_f': 'new_v', 'new_v_fox_q_g': 'new_v', 'new_v_fox_k_g': 'new_v'}


def _forward(args):
    return _fwd_reference(*[args[k] for k in FWD_PARAMS])


def _output_shape():
    out = _jax.eval_shape(lambda: _forward(_fwd_setup_inputs(0)))
    return out.shape, out.dtype

N_MICROBATCH = 1
ADAM_LR = 0.001
ADAM_B1 = 0.9
ADAM_B2 = 0.999
ADAM_EPS = 1e-08
ADAM_WD = 0.01
ADAM_STEP = 10
PER_EXAMPLE_BATCH_AXIS = {'x': 0, 'c': 0, 'loss_target': 0}
SHARED_INPUTS = []
_WEIGHT_DTYPES = {'ada_w': _jnp.float32, 'ada_b': _jnp.float32, 'norm_g': _jnp.float32, 'ffn_w13': _jnp.float32, 'ffn_w2': _jnp.float32, 'rel_bias': _jnp.float32, 'swa_w_in': _jnp.float32, 'swa_w_out': _jnp.float32, 'swa_q_g': _jnp.float32, 'swa_k_g': _jnp.float32, 'swa_sink': _jnp.float32, 'fox_w_in': _jnp.float32, 'fox_w_out': _jnp.float32, 'fox_b_f': _jnp.float32, 'fox_q_g': _jnp.float32, 'fox_k_g': _jnp.float32}
MOMENT_SCALE = {'ada_w': 1.057111e+00, 'ada_b': 2.010247e+00, 'norm_g': 1.425885e+00, 'ffn_w13': 4.762325e-02, 'ffn_w2': 8.534663e-02, 'rel_bias': 1.634695e-01, 'swa_w_in': 8.664714e-01, 'swa_w_out': 8.497328e-01, 'swa_q_g': 9.983220e-01, 'swa_k_g': 9.952393e-01, 'swa_sink': 1.928944e-01, 'fox_w_in': 5.397031e-01, 'fox_w_out': 7.713201e-01, 'fox_b_f': 2.052036e+01, 'fox_q_g': 5.624455e+00, 'fox_k_g': 5.634107e+00}


def _to_microbatches(a, axis):
    t = _jnp.moveaxis(a, axis, 0)
    t = t.reshape((N_MICROBATCH, t.shape[0] // N_MICROBATCH) + t.shape[1:])
    return _jnp.moveaxis(t, 1, axis + 1)


def setup_inputs(seed: int = 0) -> dict:
    inp = _fwd_setup_inputs(seed)
    key = _jax.random.fold_in(_jax.random.key(seed), 7919)
    shape, _ = _output_shape()
    out = dict(inp)
    out["loss_target"] = _jax.random.normal(_jax.random.fold_in(key, 0), shape, _jnp.float32)
    for i, name in enumerate(TWIN_WEIGHTS):
        w = inp[name].astype(_jnp.float32)
        if MOMENT_SCALE is None:
            s = _jnp.sqrt(_jnp.mean(_jnp.square(w)) + 1e-30)
        else:
            s = MOMENT_SCALE[name]
        km, kv = _jax.random.split(_jax.random.fold_in(key, i + 1))
        out[name] = w
        out["m_" + name] = s * _jax.random.normal(km, w.shape, _jnp.float32)
        out["v_" + name] = (s * s) * _jax.random.uniform(kv, w.shape, _jnp.float32, 0.5, 1.5)
    if N_MICROBATCH > 1:
        for name, axis in PER_EXAMPLE_BATCH_AXIS.items():
            out[name] = _to_microbatches(out[name], axis)
    return {'x': out['x'], 'c': out['c'], 'ada_w': out['ada_w'], 'ada_b': out['ada_b'], 'norm_g': out['norm_g'], 'ffn_w13': out['ffn_w13'], 'ffn_w2': out['ffn_w2'], 'rel_bias': out['rel_bias'], 'swa_w_in': out['swa_w_in'], 'swa_w_out': out['swa_w_out'], 'swa_q_g': out['swa_q_g'], 'swa_k_g': out['swa_k_g'], 'swa_sink': out['swa_sink'], 'fox_w_in': out['fox_w_in'], 'fox_w_out': out['fox_w_out'], 'fox_b_f': out['fox_b_f'], 'fox_q_g': out['fox_q_g'], 'fox_k_g': out['fox_k_g'], 'loss_target': out['loss_target'], 'm_ada_w': out['m_ada_w'], 'm_ada_b': out['m_ada_b'], 'm_norm_g': out['m_norm_g'], 'm_ffn_w13': out['m_ffn_w13'], 'm_ffn_w2': out['m_ffn_w2'], 'm_rel_bias': out['m_rel_bias'], 'm_swa_w_in': out['m_swa_w_in'], 'm_swa_w_out': out['m_swa_w_out'], 'm_swa_q_g': out['m_swa_q_g'], 'm_swa_k_g': out['m_swa_k_g'], 'm_swa_sink': out['m_swa_sink'], 'm_fox_w_in': out['m_fox_w_in'], 'm_fox_w_out': out['m_fox_w_out'], 'm_fox_b_f': out['m_fox_b_f'], 'm_fox_q_g': out['m_fox_q_g'], 'm_fox_k_g': out['m_fox_k_g'], 'v_ada_w': out['v_ada_w'], 'v_ada_b': out['v_ada_b'], 'v_norm_g': out['v_norm_g'], 'v_ffn_w13': out['v_ffn_w13'], 'v_ffn_w2': out['v_ffn_w2'], 'v_rel_bias': out['v_rel_bias'], 'v_swa_w_in': out['v_swa_w_in'], 'v_swa_w_out': out['v_swa_w_out'], 'v_swa_q_g': out['v_swa_q_g'], 'v_swa_k_g': out['v_swa_k_g'], 'v_swa_sink': out['v_swa_sink'], 'v_fox_w_in': out['v_fox_w_in'], 'v_fox_w_out': out['v_fox_w_out'], 'v_fox_b_f': out['v_fox_b_f'], 'v_fox_q_g': out['v_fox_q_g'], 'v_fox_k_g': out['v_fox_k_g']}


def _loss(weights, diff, rest, loss_target):
    with _jax.named_scope("forward"):
        args = {**rest, TWIN_DIFF_INPUT: diff, **{k: w.astype(_WEIGHT_DTYPES[k]) for k, w in weights.items()}}
        y = _forward(args)
    with _jax.named_scope("loss_head"):
        err = _jnp.square(y.astype(_jnp.float32) - loss_target)
        return 0.5 * _jnp.sum(_jnp.mean(err, axis=-1)) if err.ndim else 0.5 * err


def _adamw(w, g, m, v):
    m = ADAM_B1 * m + (1.0 - ADAM_B1) * g
    v = ADAM_B2 * v + (1.0 - ADAM_B2) * _jnp.square(g)
    m_hat = m / (1.0 - ADAM_B1 ** ADAM_STEP)
    v_hat = v / (1.0 - ADAM_B2 ** ADAM_STEP)
    delta = -ADAM_LR * (m_hat / (_jnp.sqrt(v_hat) + ADAM_EPS) + ADAM_WD * w)
    return delta, m, v


def reference(x, c, ada_w, ada_b, norm_g, ffn_w13, ffn_w2, rel_bias, swa_w_in, swa_w_out, swa_q_g, swa_k_g, swa_sink, fox_w_in, fox_w_out, fox_b_f, fox_q_g, fox_k_g, loss_target, m_ada_w, m_ada_b, m_norm_g, m_ffn_w13, m_ffn_w2, m_rel_bias, m_swa_w_in, m_swa_w_out, m_swa_q_g, m_swa_k_g, m_swa_sink, m_fox_w_in, m_fox_w_out, m_fox_b_f, m_fox_q_g, m_fox_k_g, v_ada_w, v_ada_b, v_norm_g, v_ffn_w13, v_ffn_w2, v_rel_bias, v_swa_w_in, v_swa_w_out, v_swa_q_g, v_swa_k_g, v_swa_sink, v_fox_w_in, v_fox_w_out, v_fox_b_f, v_fox_q_g, v_fox_k_g):
    given = dict(x=x, c=c, ada_w=ada_w, ada_b=ada_b, norm_g=norm_g, ffn_w13=ffn_w13, ffn_w2=ffn_w2, rel_bias=rel_bias, swa_w_in=swa_w_in, swa_w_out=swa_w_out, swa_q_g=swa_q_g, swa_k_g=swa_k_g, swa_sink=swa_sink, fox_w_in=fox_w_in, fox_w_out=fox_w_out, fox_b_f=fox_b_f, fox_q_g=fox_q_g, fox_k_g=fox_k_g, loss_target=loss_target, m_ada_w=m_ada_w, m_ada_b=m_ada_b, m_norm_g=m_norm_g, m_ffn_w13=m_ffn_w13, m_ffn_w2=m_ffn_w2, m_rel_bias=m_rel_bias, m_swa_w_in=m_swa_w_in, m_swa_w_out=m_swa_w_out, m_swa_q_g=m_swa_q_g, m_swa_k_g=m_swa_k_g, m_swa_sink=m_swa_sink, m_fox_w_in=m_fox_w_in, m_fox_w_out=m_fox_w_out, m_fox_b_f=m_fox_b_f, m_fox_q_g=m_fox_q_g, m_fox_k_g=m_fox_k_g, v_ada_w=v_ada_w, v_ada_b=v_ada_b, v_norm_g=v_norm_g, v_ffn_w13=v_ffn_w13, v_ffn_w2=v_ffn_w2, v_rel_bias=v_rel_bias, v_swa_w_in=v_swa_w_in, v_swa_w_out=v_swa_w_out, v_swa_q_g=v_swa_q_g, v_swa_k_g=v_swa_k_g, v_swa_sink=v_swa_sink, v_fox_w_in=v_fox_w_in, v_fox_w_out=v_fox_w_out, v_fox_b_f=v_fox_b_f, v_fox_q_g=v_fox_q_g, v_fox_k_g=v_fox_k_g)
    weights = {n: given[n] for n in TWIN_WEIGHTS}
    shared = {n: given[n] for n in SHARED_INPUTS}
    per_example = {n: given[n] for n in ['x', 'c']}
    grad_fn = _jax.value_and_grad(_loss, argnums=(0, 1))

    def one_microbatch(ex, loss_target):
        ex = dict(ex)
        diff = ex.pop(TWIN_DIFF_INPUT)
        return grad_fn(weights, diff, {**shared, **ex}, loss_target)

    if N_MICROBATCH == 1:
        loss, (grad_w, grad_x) = one_microbatch(per_example, given["loss_target"])
    else:
        def body(carry, xs):
            loss_sum, grad_sum = carry
            l_k, (gw_k, gx_k) = one_microbatch(xs[0], xs[1])
            with _jax.named_scope("update"):
                return (loss_sum + l_k, _jax.tree.map(_jnp.add, grad_sum, gw_k)), gx_k

        init = (_jnp.zeros((), _jnp.float32), _jax.tree.map(_jnp.zeros_like, weights))
        (loss, grad_w), grad_x = _jax.lax.scan(body, init, (per_example, given["loss_target"]))
    with _jax.named_scope("update"):
        delta_w, new_m, new_v = {}, {}, {}
        for n in TWIN_WEIGHTS:
            delta_w[n], new_m[n], new_v[n] = _adamw(weights[n], grad_w[n], given["m_" + n], given["v_" + n])
    return (loss, grad_x, *[grad_w[n] for n in TWIN_WEIGHTS], *[delta_w[n] for n in TWIN_WEIGHTS],
            *[new_m[n] for n in TWIN_WEIGHTS], *[new_v[n] for n in TWIN_WEIGHTS])
```

```python
import math

import numpy as np
import jax
import jax.numpy as jnp
from jax import lax
from jax.experimental import pallas as pl
from jax.experimental.pallas import tpu as pltpu

F32 = jnp.float32
BF16 = jnp.bfloat16
HEAD_DIM = 64
GROUP = 4
BLOCK = 128
REL_BUCKETS = 32
REL_MAX_DIST = 128
EPS = 1e-6
NEG = -1e30
N_CHIP = 4
N_DEV = 8
LANES = 128
VMEM_LIMIT = 52 * 1024 * 1024
ADAM_LR, ADAM_B1, ADAM_B2, ADAM_EPS, ADAM_WD, ADAM_STEP = 0.001, 0.9, 0.999, 1e-08, 0.01, 10
MESH = pl.DeviceIdType.MESH
ANY = pl.BlockSpec(memory_space=pl.ANY)


def _params(n_axes):
    return pltpu.CompilerParams(dimension_semantics=("arbitrary",) * n_axes, vmem_limit_bytes=VMEM_LIMIT)


def _nn(a, b):
    return jnp.dot(a, b, preferred_element_type=F32)


def _nt(a, b):
    return lax.dot_general(a, b, (((1,), (1,)), ((), ())), preferred_element_type=F32)


def _tn(a, b):
    return lax.dot_general(a, b, (((0,), (0,)), ((), ())), preferred_element_type=F32)


def _sigmoid(z):
    return 1.0 / (1.0 + jnp.exp(-z))


def _row_tile(s):
    return 512 if s >= 2048 else s // 2


def _attn_tile(s):
    return 512 if s >= 2048 else s // 4


def _position():
    x, y, c = lax.axis_index("x"), lax.axis_index("y"), lax.axis_index("c")
    chips = [(1 - x, y), (x, 1 - y), (1 - x, 1 - y)]
    return x, y, c, chips


def all_gather_rows(v, name):
    m_per, n = v.shape

    def body(x_ref, out_ref, send_sems, recv_sems, local_sem):
        x, y, c, chips = _position()
        me, sibling = (x, y, c), (x, y, 1 - c)

        def rows(px, py, pc):
            return out_ref.at[pl.ds((4 * px + 2 * py + pc) * m_per, m_per), :]

        def copy(k, block, to, src=None):
            return pltpu.make_async_remote_copy(
                src_ref=rows(*block) if src is None else src, dst_ref=rows(*block),
                send_sem=send_sems.at[k], recv_sem=recv_sems.at[k], device_id=to, device_id_type=MESH)

        mine = pltpu.make_async_copy(x_ref, rows(*me), local_sem)
        mine.start()
        first = [copy(0, me, sibling, src=x_ref)]
        first += [copy(1 + j, me, (*chip, c), src=x_ref) for j, chip in enumerate(chips)]
        for cp in first:
            cp.start()
        passed = [copy(4 + j, (*chip, c), sibling) for j, chip in enumerate(chips)]
        for j, chip in enumerate(chips):
            copy(1 + j, (*chip, c), me).wait_recv()
            passed[j].start()
        copy(0, sibling, me).wait_recv()
        for j, chip in enumerate(chips):
            copy(4 + j, (*chip, 1 - c), me).wait_recv()
        for cp in first + passed:
            cp.wait_send()
        mine.wait()

    return pl.pallas_call(
        body, name=name,
        out_shape=jax.ShapeDtypeStruct((N_DEV * m_per, n), v.dtype),
        in_specs=[pl.BlockSpec(memory_space=pltpu.VMEM)],
        out_specs=pl.BlockSpec(memory_space=pltpu.VMEM),
        scratch_shapes=[pltpu.SemaphoreType.DMA((7,)), pltpu.SemaphoreType.DMA((7,)), pltpu.SemaphoreType.DMA],
    )(v)


def _slab(full_ref, kind, b, lead):
    how = kind[0]
    if how == "slot":
        return full_ref.at[b, lead]
    if how == "col":
        w = kind[1]
        idx = (lead,) + (slice(None),) * (len(full_ref.shape) - 2) + (pl.ds(pl.multiple_of(b * w, LANES), w),)
        return full_ref.at[idx]
    h = kind[1]
    idx = (lead,) + (slice(None),) * (len(full_ref.shape) - 3) + (pl.ds(pl.multiple_of(b * h, 8), h), slice(None))
    return full_ref.at[idx]


def _full_shape(shard_shape, kind):
    if kind[0] == "slot":
        return (N_CHIP,) + tuple(shard_shape)
    if kind[0] == "col":
        return tuple(shard_shape[:-1]) + (N_CHIP * shard_shape[-1],)
    return tuple(shard_shape[:-2]) + (N_CHIP * shard_shape[-2], shard_shape[-1])


def gather_weights(shards, kinds):
    n = len(shards)

    def body(*refs):
        ins, outs = refs[:n], refs[n:2 * n]
        send_sems, recv_sems, local_sems = refs[2 * n:]
        x, y, c, chips = _position()
        b_me = 2 * x + y
        sibling = (x, y, 1 - c)
        local, sends = [], []
        for t in range(n):
            half = ins[t].shape[0] // 2
            mine = pl.ds(c * half, half)
            whole = pl.ds(0, 2 * half)
            cp = pltpu.make_async_copy(ins[t], _slab(outs[t], kinds[t], b_me, whole), local_sems.at[t])
            cp.start()
            local.append(cp)
            for j, chip in enumerate(chips):
                cp = pltpu.make_async_remote_copy(
                    src_ref=ins[t].at[mine], dst_ref=_slab(outs[t], kinds[t], b_me, mine),
                    send_sem=send_sems.at[6 * t + j], recv_sem=recv_sems.at[6 * t + j],
                    device_id=(*chip, c), device_id_type=MESH)
                cp.start()
                sends.append(cp)
        for t in range(n):
            half = ins[t].shape[0] // 2
            mine = pl.ds(c * half, half)
            for j, chip in enumerate(chips):
                landed = _slab(outs[t], kinds[t], 2 * chip[0] + chip[1], mine)
                pltpu.make_async_remote_copy(
                    src_ref=landed, dst_ref=landed, send_sem=send_sems.at[6 * t + j], recv_sem=recv_sems.at[6 * t + j],
                    device_id=(*chip, c), device_id_type=MESH).wait_recv()
                cp = pltpu.make_async_remote_copy(
                    src_ref=landed, dst_ref=landed, send_sem=send_sems.at[6 * t + 3 + j],
                    recv_sem=recv_sems.at[6 * t + 3 + j], device_id=sibling, device_id_type=MESH)
                cp.start()
                sends.append(cp)
        for t in range(n):
            half = ins[t].shape[0] // 2
            theirs = pl.ds((1 - c) * half, half)
            for j, chip in enumerate(chips):
                landed = _slab(outs[t], kinds[t], 2 * chip[0] + chip[1], theirs)
                pltpu.make_async_remote_copy(
                    src_ref=landed, dst_ref=landed, send_sem=send_sems.at[6 * t + 3 + j],
                    recv_sem=recv_sems.at[6 * t + 3 + j], device_id=sibling, device_id_type=MESH).wait_recv()
        for cp in sends:
            cp.wait_send()
        for cp in local:
            cp.wait()

    return pl.pallas_call(
        body, name="gather_weights",
        out_shape=[jax.ShapeDtypeStruct(_full_shape(s.shape, k), s.dtype) for s, k in zip(shards, kinds)],
        in_specs=[ANY] * n, out_specs=[ANY] * n,
        scratch_shapes=[pltpu.SemaphoreType.DMA((6 * n,)), pltpu.SemaphoreType.DMA((6 * n,)),
                        pltpu.SemaphoreType.DMA((n,))],
    )(*shards)


def scatter_grads(grads, kinds, shard_shapes):
    n = len(grads)

    def body(*refs):
        ins, outs = refs[:n], refs[n:2 * n]
        send_sems, recv_sems, local_sems = refs[2 * n:]
        x, y, c, chips = _position()
        b_me = 2 * x + y
        local, sends = [], []
        for t in range(n):
            whole = pl.ds(0, ins[t].shape[1] if kinds[t][0] == "slot" else ins[t].shape[0])
            cp = pltpu.make_async_copy(_slab(ins[t], kinds[t], b_me, whole), outs[t].at[3], local_sems.at[t])
            cp.start()
            local.append(cp)
            for j, chip in enumerate(chips):
                cp = pltpu.make_async_remote_copy(
                    src_ref=_slab(ins[t], kinds[t], 2 * chip[0] + chip[1], whole), dst_ref=outs[t].at[j],
                    send_sem=send_sems.at[3 * t + j], recv_sem=recv_sems.at[3 * t + j],
                    device_id=(*chip, c), device_id_type=MESH)
                cp.start()
                sends.append(cp)
        for t in range(n):
            for j, chip in enumerate(chips):
                pltpu.make_async_remote_copy(
                    src_ref=outs[t].at[j], dst_ref=outs[t].at[j], send_sem=send_sems.at[3 * t + j],
                    recv_sem=recv_sems.at[3 * t + j], device_id=(*chip, c), device_id_type=MESH).wait_recv()
        for cp in sends:
            cp.wait_send()
        for cp in local:
            cp.wait()

    return pl.pallas_call(
        body, name="scatter_grads",
        out_shape=[jax.ShapeDtypeStruct((N_CHIP,) + tuple(s), g.dtype) for g, s in zip(grads, shard_shapes)],
        in_specs=[ANY] * n, out_specs=[ANY] * n,
        scratch_shapes=[pltpu.SemaphoreType.DMA((3 * n,)), pltpu.SemaphoreType.DMA((3 * n,)),
                        pltpu.SemaphoreType.DMA((n,))],
    )(*grads)


def swap_with_sibling(parts):
    n = len(parts)

    def body(*refs):
        ins, outs = refs[:n], refs[n:2 * n]
        send_sems, recv_sems = refs[2 * n:]
        x, y, c, _ = _position()
        cps = []
        for t in range(n):
            cp = pltpu.make_async_remote_copy(
                src_ref=ins[t], dst_ref=outs[t], send_sem=send_sems.at[t], recv_sem=recv_sems.at[t],
                device_id=(x, y, 1 - c), device_id_type=MESH)
            cp.start()
            cps.append(cp)
        for cp in cps:
            cp.wait_recv()
        for cp in cps:
            cp.wait_send()

    return pl.pallas_call(
        body, name="swap_with_sibling",
        out_shape=[jax.ShapeDtypeStruct(p.shape, p.dtype) for p in parts],
        in_specs=[ANY] * n, out_specs=[ANY] * n,
        scratch_shapes=[pltpu.SemaphoreType.DMA((n,)), pltpu.SemaphoreType.DMA((n,))],
    )(*parts)


def _modulated(xv, mv_ref):
    g, shift, scale = mv_ref[0:1, :], mv_ref[1:2, :], mv_ref[2:3, :]
    r = lax.rsqrt(jnp.mean(xv * xv, axis=-1, keepdims=True) + EPS)
    xhat = xv * r
    xn = xhat * g
    return xn * (1.0 + scale) + shift, xhat, xn, r, g, scale


def modmm(x, mv, mv_idx, w, w_spec, n_tiles, out_shape, out_spec, name, want_h):
    s, d = x.shape
    tm = _row_tile(s)

    def body(x_ref, mv_ref, w_ref, *rest):
        if want_h:
            out_ref, h_ref, h_scr = rest
        else:
            out_ref, h_scr = rest

        @pl.when(pl.program_id(1) == 0)
        def _():
            h = _modulated(x_ref[...], mv_ref)[0].astype(BF16)
            h_scr[...] = h
            if want_h:
                h_ref[...] = h

        out_ref[...] = _nn(h_scr[...], w_ref[...]).astype(out_ref.dtype)

    out_shapes = [out_shape]
    out_specs = [out_spec]
    if want_h:
        out_shapes.append(jax.ShapeDtypeStruct((s, d), BF16))
        out_specs.append(pl.BlockSpec((tm, d), lambda i, n: (i, 0)))
    res = pl.pallas_call(
        body, name=name, grid=(s // tm, n_tiles),
        in_specs=[pl.BlockSpec((tm, d), lambda i, n: (i, 0)),
                  pl.BlockSpec((None, None, 8, d), lambda i, n: (*mv_idx, 0, 0)), w_spec],
        out_specs=out_specs, out_shape=out_shapes,
        scratch_shapes=[pltpu.VMEM((tm, d), BF16)],
        compiler_params=_params(2),
    )(x, mv, w)
    return res if want_h else (res[0], None)


def resmm(x, mv, mv_idx, coef, lhs, lhs_spec, w, w_spec, k_tiles, tk, name, ffn):
    s, d = x.shape
    tm = _row_tile(s)
    kdim = k_tiles * tk

    def body(x_ref, mv_ref, lhs_ref, w_ref, xo_ref, y_ref, *rest):
        if ffn:
            u_ref, acc = rest
        else:
            (acc,) = rest
        k = pl.program_id(1)

        @pl.when(k == 0)
        def _():
            acc[...] = jnp.zeros_like(acc)

        if ffn:
            ag = lhs_ref[0].astype(F32)
            au = lhs_ref[1].astype(F32)
            left = (ag * _sigmoid(ag) * au).astype(BF16)
            u_ref[...] = left
        else:
            left = lhs_ref[...]
        acc[...] += _nn(left, w_ref[...])

        @pl.when(k == k_tiles - 1)
        def _():
            y = acc[...]
            y_ref[...] = y.astype(BF16)
            xo_ref[...] = x_ref[...] + (coef * mv_ref[3:4, :]) * y

    row = pl.BlockSpec((tm, d), lambda i, k: (i, 0))
    out_shapes = [jax.ShapeDtypeStruct((s, d), F32), jax.ShapeDtypeStruct((s, d), BF16)]
    out_specs = [row, row]
    if ffn:
        out_shapes.append(jax.ShapeDtypeStruct((s, kdim), BF16))
        out_specs.append(pl.BlockSpec((tm, tk), lambda i, k: (i, k)))
    return pl.pallas_call(
        body, name=name, grid=(s // tm, k_tiles),
        in_specs=[row, pl.BlockSpec((None, None, 8, d), lambda i, k: (*mv_idx, 0, 0)), lhs_spec, w_spec],
        out_specs=out_specs, out_shape=out_shapes,
        scratch_shapes=[pltpu.VMEM((tm, d), F32)],
        compiler_params=_params(2),
    )(x, mv, lhs, w)


def resmm_bwd(dxo, y, mv, mv_idx, coef, w, w_spec, k_tiles, tk, name, a=None):
    s, d = dxo.shape
    tm = _row_tile(s)
    kdim = k_tiles * tk
    ffn = a is not None

    def body(dxo_ref, y_ref, mv_ref, w_ref, *rest):
        if ffn:
            a_ref, dy_ref, dl_ref, dgate_ref, dy_scr = rest
        else:
            dy_ref, dl_ref, dgate_ref, dy_scr = rest
        i, k = pl.program_id(0), pl.program_id(1)

        @pl.when((i == 0) & (k == 0))
        def _():
            dgate_ref[...] = jnp.zeros_like(dgate_ref)

        @pl.when(k == 0)
        def _():
            dxv = dxo_ref[...]
            dy = ((coef * mv_ref[3:4, :]) * dxv).astype(BF16)
            dy_scr[...] = dy
            dy_ref[...] = dy
            dgate_ref[0:1, :] += jnp.sum(coef * dxv * y_ref[...].astype(F32), axis=0, keepdims=True)

        dl = _nt(dy_scr[...], w_ref[...])
        if ffn:
            ag = a_ref[0].astype(F32)
            au = a_ref[1].astype(F32)
            sg = _sigmoid(ag)
            dl_ref[0] = (dl * au * (sg * (1.0 + ag * (1.0 - sg)))).astype(BF16)
            dl_ref[1] = (dl * (ag * sg)).astype(BF16)
        else:
            dl_ref[...] = dl.astype(BF16)

    row = pl.BlockSpec((tm, d), lambda i, k: (i, 0))
    in_specs = [row, row, pl.BlockSpec((None, None, 8, d), lambda i, k: (*mv_idx, 0, 0)), w_spec]
    ops = [dxo, y, mv, w]
    if ffn:
        in_specs.append(pl.BlockSpec((2, tm, tk), lambda i, k: (0, i, k)))
        ops.append(a)
        dl_shape = jax.ShapeDtypeStruct((2, s, kdim), BF16)
        dl_spec = pl.BlockSpec((2, tm, tk), lambda i, k: (0, i, k))
    else:
        dl_shape = jax.ShapeDtypeStruct((s, kdim), BF16)
        dl_spec = pl.BlockSpec((tm, tk), lambda i, k: (i, k))
    return pl.pallas_call(
        body, name=name, grid=(s // tm, k_tiles),
        in_specs=in_specs,
        out_specs=[row, dl_spec, pl.BlockSpec((8, d), lambda i, k: (0, 0))],
        out_shape=[jax.ShapeDtypeStruct((s, d), BF16), dl_shape, jax.ShapeDtypeStruct((8, d), F32)],
        scratch_shapes=[pltpu.VMEM((tm, d), BF16)],
        compiler_params=_params(2),
    )(*ops)


def modmm_bwd(dl, dl_spec, w, w_spec, n_tiles, x, dxo, mv, mv_idx, name):
    s, d = x.shape
    tm = _row_tile(s)

    def body(dl_ref, w_ref, x_ref, dxo_ref, mv_ref, dx_ref, red_ref, acc):
        i, n = pl.program_id(0), pl.program_id(1)

        @pl.when((i == 0) & (n == 0))
        def _():
            red_ref[...] = jnp.zeros_like(red_ref)

        @pl.when(n == 0)
        def _():
            acc[...] = jnp.zeros_like(acc)

        acc[...] += _nt(dl_ref[...], w_ref[...])

        @pl.when(n == n_tiles - 1)
        def _():
            dh = acc[...]
            _, xhat, xn, r, g, scale = _modulated(x_ref[...], mv_ref)
            dxn = dh * (1.0 + scale)
            red_ref[0:1, :] += jnp.sum(dxn * xhat, axis=0, keepdims=True)
            red_ref[1:2, :] += jnp.sum(dh, axis=0, keepdims=True)
            red_ref[2:3, :] += jnp.sum(dh * xn, axis=0, keepdims=True)
            gd = dxn * g
            dx_ref[...] = dxo_ref[...] + r * (gd - xhat * jnp.mean(gd * xhat, axis=-1, keepdims=True))

    row = pl.BlockSpec((tm, d), lambda i, n: (i, 0))
    return pl.pallas_call(
        body, name=name, grid=(s // tm, n_tiles),
        in_specs=[dl_spec, w_spec, row, row, pl.BlockSpec((None, None, 8, d), lambda i, n: (*mv_idx, 0, 0))],
        out_specs=[row, pl.BlockSpec((8, d), lambda i, n: (0, 0))],
        out_shape=[jax.ShapeDtypeStruct((s, d), F32), jax.ShapeDtypeStruct((8, d), F32)],
        scratch_shapes=[pltpu.VMEM((tm, d), F32)],
        compiler_params=_params(2),
    )(dl, w, x, dxo, mv)


def weight_grad(a, a_spec, b, b_spec, grid_mn, s, bm, bn, dest, out_spec, name):
    tk = _row_tile(s)
    k_tiles = s // tk

    def body(a_ref, b_ref, dest_ref, out_ref, acc):
        k = pl.program_id(2)

        @pl.when(k == 0)
        def _():
            acc[...] = jnp.zeros_like(acc)

        acc[...] += _tn(a_ref[...], b_ref[...])

        @pl.when(k == k_tiles - 1)
        def _():
            out_ref[...] = acc[...].astype(out_ref.dtype)

    return pl.pallas_call(
        body, name=name, grid=(*grid_mn, k_tiles),
        in_specs=[a_spec, b_spec, ANY], out_specs=out_spec,
        out_shape=jax.ShapeDtypeStruct(dest.shape, dest.dtype),
        input_output_aliases={2: 0},
        scratch_shapes=[pltpu.VMEM((bm, bn), F32)],
        compiler_params=_params(3),
    )(a, b, dest)


def _head_mean(v):
    lane = lax.broadcasted_iota(jnp.int32, v.shape, 1)
    lo = jnp.sum(jnp.where(lane < HEAD_DIM, v, 0.0), axis=-1, keepdims=True)
    hi = jnp.sum(v, axis=-1, keepdims=True) - lo
    return jnp.where(lane < HEAD_DIM, lo, hi) * (1.0 / HEAD_DIM)


def qknorm_fwd(proj, gains, name):
    s = proj.shape[0]
    nqk = gains.shape[1]
    tm = _row_tile(s)

    def body(p_ref, g_ref, o_ref):
        xv = p_ref[...].astype(F32)
        r = lax.rsqrt(_head_mean(xv * xv) + EPS)
        o_ref[...] = (xv * r * g_ref[...]).astype(BF16)

    blk = pl.BlockSpec((tm, LANES), lambda i, c: (i, c))
    return pl.pallas_call(
        body, name=name, grid=(s // tm, nqk // LANES),
        in_specs=[blk, pl.BlockSpec((1, LANES), lambda i, c: (0, c))],
        out_specs=blk, out_shape=jax.ShapeDtypeStruct((s, nqk), BF16),
        compiler_params=_params(2),
    )(proj, gains)


def qknorm_bwd(proj, gains, d, d_spec, n_cols, name, extra=None):
    s = proj.shape[0]
    nqk = gains.shape[1] // LANES
    n_in = n_cols // LANES
    n_out = n_in + (1 if extra is not None else 0)
    tm = _row_tile(s)

    def body(p_ref, g_ref, d_ref, *rest):
        if extra is not None:
            e_ref, o_ref, dg_ref = rest
        else:
            o_ref, dg_ref = rest
        c, i = pl.program_id(0), pl.program_id(1)

        @pl.when(i == 0)
        def _():
            dg_ref[...] = jnp.zeros_like(dg_ref)

        @pl.when(c < nqk)
        def _():
            xv = p_ref[...].astype(F32)
            r = lax.rsqrt(_head_mean(xv * xv) + EPS)
            xhat = xv * r
            dv = d_ref[...]
            gd = dv * g_ref[...]
            o_ref[...] = (r * (gd - xhat * _head_mean(gd * xhat))).astype(BF16)
            dg_ref[0:1, :] += jnp.sum(dv * xhat, axis=0, keepdims=True)

        @pl.when((c >= nqk) & (c < n_in))
        def _():
            o_ref[...] = d_ref[...].astype(BF16)

        if extra is not None:
            @pl.when(c == n_in)
            def _():
                o_ref[...] = e_ref[...]

    in_specs = [pl.BlockSpec((tm, LANES), lambda c, i: (i, jnp.minimum(c, n_in - 1))),
                pl.BlockSpec((1, LANES), lambda c, i: (0, jnp.minimum(c, nqk - 1))), d_spec]
    ops = [proj, gains, d]
    if extra is not None:
        in_specs.append(pl.BlockSpec((tm, LANES), lambda c, i: (i, 0)))
        ops.append(extra)
    return pl.pallas_call(
        body, name=name, grid=(n_out, s // tm),
        in_specs=in_specs,
        out_specs=[pl.BlockSpec((tm, LANES), lambda c, i: (i, c)), pl.BlockSpec((8, LANES), lambda c, i: (0, c))],
        out_shape=[jax.ShapeDtypeStruct((s, n_out * LANES), BF16), jax.ShapeDtypeStruct((8, n_out * LANES), F32)],
        compiler_params=_params(2),
    )(*ops)


def _swa_mask(first):
    qi = lax.broadcasted_iota(jnp.int32, (BLOCK, 2 * BLOCK), 0) + BLOCK
    kj = lax.broadcasted_iota(jnp.int32, (BLOCK, 2 * BLOCK), 1)
    dist = qi - kj
    return (dist >= 0) & (dist < BLOCK) & ((kj >= BLOCK) | jnp.logical_not(first))


def swa_fwd(qkn, proj, bias, sink, d, name):
    s = qkn.shape[0]
    hq = d // HEAD_DIM
    hkv = hq // GROUP
    kw = hkv * HEAD_DIM
    nblk = s // BLOCK
    kcol = d // kw

    def body(q_ref, kc_ref, kp_ref, vc_ref, vp_ref, bias_ref, sink_ref, o_ref, lse_ref):
        mask = _swa_mask(pl.program_id(0) == 0)
        lse_ref[...] = jnp.zeros_like(lse_ref)
        for kvh in range(hkv):
            cols = slice(kvh * HEAD_DIM, (kvh + 1) * HEAD_DIM)
            k2 = jnp.concatenate([kp_ref[:, cols], kc_ref[:, cols]], axis=0)
            v2 = jnp.concatenate([vp_ref[:, cols], vc_ref[:, cols]], axis=0)
            for g in range(GROUP):
                h = kvh * GROUP + g
                hc = slice(h * HEAD_DIM, (h + 1) * HEAD_DIM)
                sc = jnp.where(mask, _nt(q_ref[:, hc], k2) + bias_ref[h], NEG)
                sk = sink_ref[0, h]
                m = jnp.maximum(jnp.max(sc, axis=-1, keepdims=True), sk)
                p = jnp.exp(sc - m)
                denom = jnp.sum(p, axis=-1, keepdims=True) + jnp.exp(sk - m)
                o_ref[:, hc] = (_nn(p.astype(BF16), v2) / denom).astype(BF16)
                lse_ref[:, h:h + 1] = m + jnp.log(denom)

    prev = lambda i: jnp.maximum(i - 1, 0)
    return pl.pallas_call(
        body, name=name, grid=(nblk,),
        in_specs=[pl.BlockSpec((BLOCK, d), lambda i: (i, 0)),
                  pl.BlockSpec((BLOCK, kw), lambda i: (i, kcol)),
                  pl.BlockSpec((BLOCK, kw), lambda i: (prev(i), kcol)),
                  pl.BlockSpec((BLOCK, kw), lambda i: (i, kcol + 1)),
                  pl.BlockSpec((BLOCK, kw), lambda i: (prev(i), kcol + 1)),
                  pl.BlockSpec((hq, BLOCK, 2 * BLOCK), lambda i: (0, 0, 0)),
                  pl.BlockSpec(memory_space=pltpu.SMEM)],
        out_specs=[pl.BlockSpec((BLOCK, d), lambda i: (i, 0)), pl.BlockSpec((BLOCK, LANES), lambda i: (i, 0))],
        out_shape=[jax.ShapeDtypeStruct((s, d), BF16), jax.ShapeDtypeStruct((s, LANES), F32)],
        compiler_params=_params(1),
    )(qkn, qkn, qkn, proj, proj, bias, sink)


def swa_bwd(qkn, proj, bias, sink, do, o, lse, d, name):
    s = qkn.shape[0]
    hq = d // HEAD_DIM
    hkv = hq // GROUP
    kw = hkv * HEAD_DIM
    nblk = s // BLOCK
    kcol = d // kw
    wide = d + 2 * kw

    def body(q_ref, kc_ref, kp_ref, vc_ref, vp_ref, bias_ref, sink_ref, do_ref, o_ref, lse_ref,
             out_ref, dbias_ref, dsink_ref, carry, fresh):
        i = pl.program_id(0)

        @pl.when(i == 0)
        def _():
            dbias_ref[...] = jnp.zeros_like(dbias_ref)
            dsink_ref[...] = jnp.zeros_like(dsink_ref)
            carry[...] = jnp.zeros_like(carry)

        @pl.when(i == nblk)
        def _():
            fresh[...] = jnp.zeros_like(fresh)

        @pl.when(i < nblk)
        def _():
            mask = _swa_mask(i == 0)
            for kvh in range(hkv):
                cols = slice(kvh * HEAD_DIM, (kvh + 1) * HEAD_DIM)
                k2 = jnp.concatenate([kp_ref[:, cols], kc_ref[:, cols]], axis=0)
                v2 = jnp.concatenate([vp_ref[:, cols], vc_ref[:, cols]], axis=0)
                dk2 = jnp.zeros((2 * BLOCK, HEAD_DIM), F32)
                dv2 = jnp.zeros((2 * BLOCK, HEAD_DIM), F32)
                for g in range(GROUP):
                    h = kvh * GROUP + g
                    hc = slice(h * HEAD_DIM, (h + 1) * HEAD_DIM)
                    q = q_ref[:, hc]
                    dov = do_ref[:, hc]
                    lse_h = lse_ref[:, h:h + 1]
                    sc = jnp.where(mask, _nt(q, k2) + bias_ref[h], NEG)
                    p = jnp.exp(sc - lse_h)
                    delta = jnp.sum(dov.astype(F32) * o_ref[:, hc].astype(F32), axis=-1, keepdims=True)
                    ds = p * (_nt(dov, v2) - delta)
                    dbias_ref[h] += ds
                    dsink_ref[0:1, h:h + 1] += jnp.sum(-jnp.exp(sink_ref[0, h] - lse_h) * delta, axis=0, keepdims=True)
                    dsb = ds.astype(BF16)
                    fresh[0, :, hc] = _nn(dsb, k2)
                    dk2 += _tn(dsb, q)
                    dv2 += _tn(p.astype(BF16), dov)
                kc_cols = slice(d + kvh * HEAD_DIM, d + (kvh + 1) * HEAD_DIM)
                vc_cols = slice(d + kw + kvh * HEAD_DIM, d + kw + (kvh + 1) * HEAD_DIM)
                fresh[0, :, kc_cols] = dk2[BLOCK:]
                fresh[0, :, vc_cols] = dv2[BLOCK:]
                fresh[1, :, kc_cols] = dk2[:BLOCK]
                fresh[1, :, vc_cols] = dv2[:BLOCK]

        lane = lax.broadcasted_iota(jnp.int32, (BLOCK, wide), 1)
        out_ref[...] = carry[...] + jnp.where(lane >= d, fresh[1], 0.0)

        @pl.when(i < nblk)
        def _():
            carry[...] = fresh[0]

    cur = lambda i: jnp.minimum(i, nblk - 1)
    prev = lambda i: jnp.maximum(jnp.minimum(i, nblk - 1) - 1, 0)
    return pl.pallas_call(
        body, name=name, grid=(nblk + 1,),
        in_specs=[pl.BlockSpec((BLOCK, d), lambda i: (cur(i), 0)),
                  pl.BlockSpec((BLOCK, kw), lambda i: (cur(i), kcol)),
                  pl.BlockSpec((BLOCK, kw), lambda i: (prev(i), kcol)),
                  pl.BlockSpec((BLOCK, kw), lambda i: (cur(i), kcol + 1)),
                  pl.BlockSpec((BLOCK, kw), lambda i: (prev(i), kcol + 1)),
                  pl.BlockSpec((hq, BLOCK, 2 * BLOCK), lambda i: (0, 0, 0)),
                  pl.BlockSpec(memory_space=pltpu.SMEM),
                  pl.BlockSpec((BLOCK, d), lambda i: (cur(i), 0)),
                  pl.BlockSpec((BLOCK, d), lambda i: (cur(i), 0)),
                  pl.BlockSpec((BLOCK, LANES), lambda i: (cur(i), 0))],
        out_specs=[pl.BlockSpec((BLOCK, wide), lambda i: (jnp.maximum(i - 1, 0), 0)),
                   pl.BlockSpec((hq, BLOCK, 2 * BLOCK), lambda i: (0, 0, 0)),
                   pl.BlockSpec((8, LANES), lambda i: (0, 0))],
        out_shape=[jax.ShapeDtypeStruct((s, wide), F32), jax.ShapeDtypeStruct((hq, BLOCK, 2 * BLOCK), F32),
                   jax.ShapeDtypeStruct((8, LANES), F32)],
        scratch_shapes=[pltpu.VMEM((BLOCK, wide), F32), pltpu.VMEM((2, BLOCK, wide), F32)],
        compiler_params=_params(1),
    )(qkn, qkn, qkn, proj, proj, bias, sink, do, o, lse)


def _rel_bucket_table():
    qi = np.arange(BLOCK)[:, None] + BLOCK
    kj = np.arange(2 * BLOCK)[None, :]
    n = np.maximum(qi - kj, 0)
    max_exact = REL_BUCKETS // 2
    nf = np.maximum(n, 1).astype(np.float32)
    large = max_exact + (np.log(nf / max_exact) / math.log(REL_MAX_DIST / max_exact)
                         * (REL_BUCKETS - max_exact)).astype(np.int32)
    large = np.minimum(large, REL_BUCKETS - 1)
    return np.where(n < max_exact, n, large).astype(np.int32)


def rel_bias_table(rel_bias, bucket):
    hq = rel_bias.shape[1]

    def body(rb_ref, bucket_ref, out_ref):
        tbl = bucket_ref[...]

        def per_head(h, carry):
            def per_bucket(b, acc):
                return jnp.where(tbl == b, rb_ref[b, h], acc)

            out_ref[h] = lax.fori_loop(0, REL_BUCKETS, per_bucket, jnp.zeros(tbl.shape, F32))
            return carry

        lax.fori_loop(0, hq, per_head, 0)

    return pl.pallas_call(
        body, name="rel_bias_table",
        in_specs=[pl.BlockSpec(memory_space=pltpu.SMEM), pl.BlockSpec(memory_space=pltpu.VMEM)],
        out_specs=pl.BlockSpec(memory_space=pltpu.VMEM),
        out_shape=jax.ShapeDtypeStruct((hq,) + tuple(bucket.shape), F32),
    )(rel_bias, bucket)


def rel_bias_grad(dbias, bucket):
    n_layers, hq = dbias.shape[:2]

    def body(db_ref, bucket_ref, out_ref):
        tbl = bucket_ref[...]

        def per_head(h, carry):
            dsum = db_ref[0, h]
            for a in range(1, n_layers):
                dsum = dsum + db_ref[a, h]

            def per_bucket(b, carry2):
                out_ref[b, h] = jnp.sum(jnp.where(tbl == b, dsum, 0.0))
                return carry2

            return lax.fori_loop(0, REL_BUCKETS, per_bucket, carry)

        lax.fori_loop(0, hq, per_head, 0)

    return pl.pallas_call(
        body, name="rel_bias_grad",
        in_specs=[pl.BlockSpec(memory_space=pltpu.VMEM), pl.BlockSpec(memory_space=pltpu.VMEM)],
        out_specs=pl.BlockSpec(memory_space=pltpu.SMEM),
        out_shape=jax.ShapeDtypeStruct((REL_BUCKETS, hq), F32),
    )(dbias, bucket)


def _split3(v):
    hi = v.astype(BF16)
    r1 = v - hi.astype(F32)
    mid = r1.astype(BF16)
    lo = (r1 - mid.astype(F32)).astype(BF16)
    return hi, mid, lo


def _tri_sum(tri, v):
    hi, mid, lo = _split3(v)
    return _nn(tri, hi) + _nn(tri, mid) + _nn(tri, lo)


def fox_gates(fl, b_f, name):
    s = fl.shape[0]
    t = _row_tile(s)

    def body(fl_ref, b_ref, f_ref, carry):
        @pl.when(pl.program_id(0) == 0)
        def _():
            carry[...] = jnp.zeros_like(carry)

        z = fl_ref[...] + b_ref[...]
        logf = jnp.minimum(z, 0.0) - jnp.log(1.0 + jnp.exp(-jnp.abs(z)))
        r = lax.broadcasted_iota(jnp.int32, (t, t), 0)
        cidx = lax.broadcasted_iota(jnp.int32, (t, t), 1)
        tri = jnp.where(cidx <= r, 1.0, 0.0).astype(BF16)
        f = _tri_sum(tri, logf) + carry[0:1, :]
        f_ref[...] = f
        carry[0:1, :] = f_ref[t - 1:t, :]

    blk = pl.BlockSpec((t, LANES), lambda i: (i, 0))
    return pl.pallas_call(
        body, name=name, grid=(s // t,),
        in_specs=[blk, pl.BlockSpec((1, LANES), lambda i: (0, 0))],
        out_specs=blk, out_shape=jax.ShapeDtypeStruct((s, LANES), F32),
        scratch_shapes=[pltpu.VMEM((8, LANES), F32)],
        compiler_params=_params(1),
    )(fl, b_f)


def fox_gates_bwd(fl, b_f, df_query, df_key, name):
    s = fl.shape[0]
    t = _row_tile(s)
    nb = s // t

    def body(fl_ref, b_ref, dfq_ref, dfk_ref, dfl_ref, db_ref, carry):
        @pl.when(pl.program_id(0) == 0)
        def _():
            carry[...] = jnp.zeros_like(carry)
            db_ref[...] = jnp.zeros_like(db_ref)

        dfv = dfq_ref[...] + dfk_ref[...]
        r = lax.broadcasted_iota(jnp.int32, (t, t), 0)
        cidx = lax.broadcasted_iota(jnp.int32, (t, t), 1)
        tri = jnp.where(cidx >= r, 1.0, 0.0).astype(BF16)
        dlog = _tri_sum(tri, dfv) + carry[0:1, :]
        carry[0:1, :] += jnp.sum(dfv, axis=0, keepdims=True)
        z = fl_ref[...] + b_ref[...]
        dz = dlog * (1.0 - _sigmoid(z))
        dfl_ref[...] = dz.astype(BF16)
        db_ref[0:1, :] += jnp.sum(dz, axis=0, keepdims=True)

    rev = pl.BlockSpec((t, LANES), lambda i: (nb - 1 - i, 0))
    return pl.pallas_call(
        body, name=name, grid=(nb,),
        in_specs=[rev, pl.BlockSpec((1, LANES), lambda i: (0, 0)), rev, rev],
        out_specs=[rev, pl.BlockSpec((8, LANES), lambda i: (0, 0))],
        out_shape=[jax.ShapeDtypeStruct((s, LANES), BF16), jax.ShapeDtypeStruct((8, LANES), F32)],
        scratch_shapes=[pltpu.VMEM((8, LANES), F32)],
        compiler_params=_params(1),
    )(fl, b_f, df_query, df_key)


def fox_fwd(qkn, proj, f_col, f_row, d, name):
    s = qkn.shape[0]
    t = _attn_tile(s)
    n_pairs = d // LANES
    nt = s // t

    def body(q_ref, k_ref, v_ref, fq_ref, fk_ref, o_ref, o32_ref, lse_ref, m_scr, l_scr, acc):
        i, j = pl.program_id(1), pl.program_id(2)

        @pl.when(j == 0)
        def _():
            m_scr[...] = jnp.full_like(m_scr, NEG)
            l_scr[...] = jnp.zeros_like(l_scr)
            acc[...] = jnp.zeros_like(acc)

        @pl.when(j <= i)
        def _():
            row = lax.broadcasted_iota(jnp.int32, (t, t), 0)
            col = lax.broadcasted_iota(jnp.int32, (t, t), 1)
            visible = (col <= row) | (j < i)
            for hh in range(2):
                hc = slice(hh * HEAD_DIM, (hh + 1) * HEAD_DIM)
                sc = _nt(q_ref[:, hc], k_ref[:, hc]) + fq_ref[hh] - fk_ref[hh]
                sc = jnp.where(visible, sc, NEG)
                m_prev = m_scr[hh]
                m_new = jnp.maximum(m_prev, jnp.max(sc, axis=-1, keepdims=True))
                alpha = jnp.exp(m_prev - m_new)
                p = jnp.exp(sc - m_new)
                l_scr[hh] = alpha * l_scr[hh] + jnp.sum(p, axis=-1, keepdims=True)
                acc[hh] = alpha * acc[hh] + _nn(p.astype(BF16), v_ref[:, hc])
                m_scr[hh] = m_new

        @pl.when(j == i)
        def _():
            for hh in range(2):
                hc = slice(hh * HEAD_DIM, (hh + 1) * HEAD_DIM)
                ov = acc[hh] / l_scr[hh]
                o_ref[:, hc] = ov.astype(BF16)
                o32_ref[:, hc] = ov
                lse_ref[hh] = m_scr[hh] + jnp.log(l_scr[hh])

    kv = lambda j, i: jnp.minimum(j, i)
    return pl.pallas_call(
        body, name=name, grid=(n_pairs, nt, nt),
        in_specs=[pl.BlockSpec((t, LANES), lambda p, i, j: (i, p)),
                  pl.BlockSpec((t, LANES), lambda p, i, j: (kv(j, i), n_pairs + p)),
                  pl.BlockSpec((t, LANES), lambda p, i, j: (kv(j, i), 2 * n_pairs + p)),
                  pl.BlockSpec((2, t, 1), lambda p, i, j: (p, i, 0)),
                  pl.BlockSpec((2, 1, t), lambda p, i, j: (p, 0, kv(j, i)))],
        out_specs=[pl.BlockSpec((t, LANES), lambda p, i, j: (i, p)),
                   pl.BlockSpec((t, LANES), lambda p, i, j: (i, p)),
                   pl.BlockSpec((2, t, 1), lambda p, i, j: (p, i, 0))],
        out_shape=[jax.ShapeDtypeStruct((s, d), BF16), jax.ShapeDtypeStruct((s, d), F32),
                   jax.ShapeDtypeStruct((2 * n_pairs, s, 1), F32)],
        scratch_shapes=[pltpu.VMEM((2, t, 1), F32), pltpu.VMEM((2, t, 1), F32), pltpu.VMEM((2, t, HEAD_DIM), F32)],
        compiler_params=_params(3),
    )(qkn, qkn, proj, f_col, f_row)


def fox_bwd(qkn, proj, f_col, f_row, lse_row, do, o, d, name):
    s = qkn.shape[0]
    t = _attn_tile(s)
    n_pairs = d // LANES
    nt = s // t

    def body(q_ref, k_ref, v_ref, fk_ref, fq_ref, lse_ref, do_ref, o_ref, out_ref, df_ref, dfq_ref,
             dq_acc, dkv_acc, df_acc, dfq_acc):
        j, i = pl.program_id(1), pl.program_id(2)

        @pl.when((j == 0) & (i == 0))
        def _():
            dq_acc[...] = jnp.zeros_like(dq_acc)
            dfq_acc[...] = jnp.zeros_like(dfq_acc)

        @pl.when(i == 0)
        def _():
            dkv_acc[...] = jnp.zeros_like(dkv_acc)
            df_acc[...] = jnp.zeros_like(df_acc)

        @pl.when(i >= j)
        def _():
            krow = lax.broadcasted_iota(jnp.int32, (t, t), 0)
            qcol = lax.broadcasted_iota(jnp.int32, (t, t), 1)
            visible = (krow <= qcol) | (i > j)
            ones = jnp.ones((8, HEAD_DIM), BF16)
            for hh in range(2):
                hc = slice(hh * HEAD_DIM, (hh + 1) * HEAD_DIM)
                q, k, v, dov = q_ref[:, hc], k_ref[:, hc], v_ref[:, hc], do_ref[:, hc]
                st = _nt(k, q) + fq_ref[hh] - fk_ref[hh]
                pt = jnp.exp(jnp.where(visible, st, NEG) - lse_ref[hh])
                hi, mid, lo = _split3(dov.astype(F32) * o_ref[:, hc])
                delta = jnp.max(_nt(ones, hi) + _nt(ones, mid) + _nt(ones, lo), axis=0, keepdims=True)
                dst = pt * (_nt(v, dov) - delta)
                dsb = dst.astype(BF16)
                dkv_acc[1, :, hc] += _nn(pt.astype(BF16), dov)
                dkv_acc[0, :, hc] += _nn(dsb, q)
                dq_acc[pl.ds(pl.multiple_of(i * t, t), t), hc] += _tn(dsb, k)
                df_acc[hh] -= jnp.sum(dst, axis=-1, keepdims=True)
                dfq_acc[i, hh] += jnp.sum(dst, axis=0, keepdims=True)

        @pl.when(i == nt - 1)
        def _():
            out_ref[0] = dq_acc[pl.ds(pl.multiple_of(j * t, t), t), :]
            out_ref[1] = dkv_acc[0]
            out_ref[2] = dkv_acc[1]
            df_ref[...] = df_acc[...]
            dfq_ref[...] = dfq_acc[j]

    qi = lambda j, i: jnp.maximum(i, j)
    return pl.pallas_call(
        body, name=name, grid=(n_pairs, nt, nt),
        in_specs=[pl.BlockSpec((t, LANES), lambda p, j, i: (qi(j, i), p)),
                  pl.BlockSpec((t, LANES), lambda p, j, i: (j, n_pairs + p)),
                  pl.BlockSpec((t, LANES), lambda p, j, i: (j, 2 * n_pairs + p)),
                  pl.BlockSpec((2, t, 1), lambda p, j, i: (p, j, 0)),
                  pl.BlockSpec((2, 1, t), lambda p, j, i: (p, 0, qi(j, i))),
                  pl.BlockSpec((2, 1, t), lambda p, j, i: (p, 0, qi(j, i))),
                  pl.BlockSpec((t, LANES), lambda p, j, i: (qi(j, i), p)),
                  pl.BlockSpec((t, LANES), lambda p, j, i: (qi(j, i), p))],
        out_specs=[pl.BlockSpec((3, t, LANES), lambda p, j, i: (0, j, p)),
                   pl.BlockSpec((2, t, 1), lambda p, j, i: (p, j, 0)),
                   pl.BlockSpec((2, 1, t), lambda p, j, i: (p, 0, j))],
        out_shape=[jax.ShapeDtypeStruct((3, s, d), F32), jax.ShapeDtypeStruct((2 * n_pairs, s, 1), F32),
                   jax.ShapeDtypeStruct((2 * n_pairs, 1, s), F32)],
        scratch_shapes=[pltpu.VMEM((s, LANES), F32), pltpu.VMEM((2, t, LANES), F32), pltpu.VMEM((2, t, 1), F32),
                        pltpu.VMEM((nt, 2, 1, t), F32)],
        compiler_params=_params(3),
    )(qkn, qkn, proj, f_col, f_row, lse_row, do, o)


def loss_head(y, target):
    s, d = y.shape
    tm = _row_tile(s)

    def body(y_ref, t_ref, dy_ref, loss_ref):
        @pl.when(pl.program_id(0) == 0)
        def _():
            loss_ref[...] = jnp.zeros_like(loss_ref)

        diff = y_ref[...] - t_ref[...]
        dy_ref[...] = diff * (1.0 / d)
        loss_ref[...] += 0.5 * jnp.sum(jnp.mean(diff * diff, axis=-1, keepdims=True), axis=0, keepdims=True)

    row = pl.BlockSpec((tm, d), lambda i: (i, 0))
    return pl.pallas_call(
        body, name="loss_head", grid=(s // tm,),
        in_specs=[row, row],
        out_specs=[row, pl.BlockSpec((8, LANES), lambda i: (0, 0))],
        out_shape=[jax.ShapeDtypeStruct((s, d), F32), jax.ShapeDtypeStruct((8, LANES), F32)],
        compiler_params=_params(1),
    )(y, target)


def ada_mod(c_all, w, b):
    n_layers, d, cols = w.shape

    def body(c_ref, w_ref, b_ref, o_ref):
        cv = c_ref[...]
        o_ref[...] = _nn(cv * _sigmoid(cv), w_ref[...]) + b_ref[...]

    return pl.pallas_call(
        body, name="ada_mod", grid=(n_layers,),
        in_specs=[pl.BlockSpec((N_DEV, d), lambda l: (0, 0)), pl.BlockSpec((None, d, cols), lambda l: (l, 0, 0)),
                  pl.BlockSpec((None, 1, cols), lambda l: (l, 0, 0))],
        out_specs=pl.BlockSpec((None, N_DEV, cols), lambda l: (l, 0, 0)),
        out_shape=jax.ShapeDtypeStruct((n_layers, N_DEV, cols), F32),
        compiler_params=_params(1),
    )(c_all, w, b)


def ada_grad(c_t, dmod):
    d = c_t.shape[0]
    n_layers, _, cols = dmod.shape
    tn = cols // 2

    def body(c_ref, dm_ref, o_ref):
        cv = c_ref[...]
        o_ref[...] = _nn(cv * _sigmoid(cv), dm_ref[...])

    return pl.pallas_call(
        body, name="ada_grad", grid=(n_layers, 2),
        in_specs=[pl.BlockSpec((d, N_DEV), lambda l, n: (0, 0)), pl.BlockSpec((None, N_DEV, tn), lambda l, n: (l, 0, n))],
        out_specs=pl.BlockSpec((None, d, tn), lambda l, n: (l, 0, n)),
        out_shape=jax.ShapeDtypeStruct((n_layers, d, cols), F32),
        compiler_params=_params(2),
    )(c_t, dmod)


def sum_devices(v):
    def body(v_ref, o_ref):
        acc = v_ref[0]
        for k in range(1, N_DEV):
            acc = acc + v_ref[k]
        o_ref[...] = acc

    return pl.pallas_call(body, name="sum_devices", out_shape=jax.ShapeDtypeStruct(v.shape[1:], F32))(v)


def sum_slots(r):
    _, rows, cols = r.shape
    tm = 256 if rows % 256 == 0 else rows

    def body(r_ref, o_ref):
        o_ref[...] = ((r_ref[3].astype(F32) + r_ref[0].astype(F32)) + r_ref[1].astype(F32)) + r_ref[2].astype(F32)

    return pl.pallas_call(
        body, name="sum_slots", grid=(rows // tm,),
        in_specs=[pl.BlockSpec((N_CHIP, tm, cols), lambda i: (0, i, 0))],
        out_specs=pl.BlockSpec((tm, cols), lambda i: (i, 0)),
        out_shape=jax.ShapeDtypeStruct((rows, cols), F32),
        compiler_params=_params(1),
    )(r)


def adamw(w, m, v, g, g2=None):
    rows, cols = w.shape
    tm = 256 if rows % 256 == 0 else rows
    two = g2 is not None
    c1 = 1.0 - ADAM_B1 ** ADAM_STEP
    c2 = 1.0 - ADAM_B2 ** ADAM_STEP

    def body(w_ref, m_ref, v_ref, g_ref, *rest):
        if two:
            g2_ref, go_ref, d_ref, mo_ref, vo_ref = rest
            gv = g_ref[...] + g2_ref[...]
        else:
            go_ref, d_ref, mo_ref, vo_ref = rest
            gv = g_ref[...]
        mn = ADAM_B1 * m_ref[...] + (1.0 - ADAM_B1) * gv
        vn = ADAM_B2 * v_ref[...] + (1.0 - ADAM_B2) * (gv * gv)
        go_ref[...] = gv
        mo_ref[...] = mn
        vo_ref[...] = vn
        d_ref[...] = -ADAM_LR * ((mn / c1) / (jnp.sqrt(vn / c2) + ADAM_EPS) + ADAM_WD * w_ref[...])

    blk = pl.BlockSpec((tm, cols), lambda i: (i, 0))
    ops = [w, m, v, g] + ([g2] if two else [])
    return pl.pallas_call(
        body, name="adamw", grid=(rows // tm,),
        in_specs=[blk] * len(ops), out_specs=[blk] * 4,
        out_shape=[jax.ShapeDtypeStruct((rows, cols), F32)] * 4,
        compiler_params=_params(1),
    )(*ops)


def _pad_rows(flat):
    n = flat.shape[0]
    rows = -(-n // LANES)
    return jnp.pad(flat, (0, rows * LANES - n)).reshape(rows, LANES)


def _pad_rows8(flat):
    rows = _pad_rows(flat)
    return jnp.pad(rows, ((0, -rows.shape[0] % 8), (0, 0)))


def _col_tiles(n):
    return next(k for k in range(1, n // LANES + 1) if n % (k * LANES) == 0 and n // k <= 1536)


def kernel(x, c, ada_w, ada_b, norm_g, ffn_w13, ffn_w2, rel_bias, swa_w_in, swa_w_out, swa_q_g, swa_k_g, swa_sink, fox_w_in, fox_w_out, fox_b_f, fox_q_g, fox_k_g, loss_target, m_ada_w, m_ada_b, m_norm_g, m_ffn_w13, m_ffn_w2, m_rel_bias, m_swa_w_in, m_swa_w_out, m_swa_q_g, m_swa_k_g, m_swa_sink, m_fox_w_in, m_fox_w_out, m_fox_b_f, m_fox_q_g, m_fox_k_g, v_ada_w, v_ada_b, v_norm_g, v_ffn_w13, v_ffn_w2, v_rel_bias, v_swa_w_in, v_swa_w_out, v_swa_q_g, v_swa_k_g, v_swa_sink, v_fox_w_in, v_fox_w_out, v_fox_b_f, v_fox_q_g, v_fox_k_g):
    ix, iy, ic = lax.axis_index("x"), lax.axis_index("y"), lax.axis_index("c")
    chip = 2 * ix + iy
    dev = 2 * chip + ic
    s, d = x.shape[1:]
    n_layers = ada_w.shape[0]
    n_a, n_b = swa_w_in.shape[0], fox_w_in.shape[0]
    hq = d // HEAD_DIM
    hkv = hq // GROUP
    kw = hkv * HEAD_DIM
    c13 = ffn_w13.shape[-1]
    f = 2 * c13
    r2 = ffn_w2.shape[2]
    cq = norm_g.shape[-1]
    a_in = d + 2 * kw
    fx = fox_w_in.shape[-1]
    b_in = N_CHIP * fx
    b_pad = 3 * d + LANES
    x0 = x[0]

    hello = _pad_rows8(jnp.concatenate([c.reshape(-1), norm_g.reshape(-1)]))
    hello_all = all_gather_rows(hello, "gather_c_norm").reshape(N_DEV, -1)
    c_all = hello_all[:, :d]
    ng = hello_all[::2, d:d + n_layers * 3 * cq].reshape(N_CHIP, n_layers, 3, cq)
    norm_full = jnp.moveaxis(ng, 0, 2).reshape(n_layers, 3, d)

    half_cols = ada_w.shape[-1] // 2
    w_half = lax.dynamic_slice_in_dim(ada_w, ic * half_cols, half_cols, axis=2)
    b_half = lax.dynamic_slice_in_dim(ada_b, dev * half_cols, half_cols, axis=1)[:, None, :]
    mod_part = ada_mod(c_all, w_half, b_half)
    mod_all = all_gather_rows(mod_part.reshape(n_layers * N_DEV, half_cols), "gather_mod")
    mod_all = mod_all.reshape(N_DEV, n_layers, N_DEV, half_cols)
    mod_mine = lax.dynamic_index_in_dim(mod_all, dev, axis=2, keepdims=False)
    mod_mine = jnp.moveaxis(mod_mine, 0, 1).reshape(n_layers, 3, 3, d)
    mv = jnp.concatenate([norm_full[:, :, None, :], mod_mine, jnp.zeros((n_layers, 3, 4, d), F32)], axis=2)

    kinds = [("col", c13), ("row", r2), ("col", swa_w_in.shape[-1]), ("row", swa_w_out.shape[1]), ("slot",),
             ("row", fox_w_out.shape[1])]
    w13, w2, wa_in, wa_out, wb_slots, wb_out = gather_weights(
        [ffn_w13.astype(BF16), ffn_w2.astype(BF16), swa_w_in.astype(BF16), swa_w_out.astype(BF16),
         fox_w_in.astype(BF16), fox_w_out.astype(BF16)], kinds)
    wb_in = jnp.concatenate([wb_slots[b] for b in range(N_CHIP)], axis=-1)
    wb_in = jnp.pad(wb_in, ((0, 0), (0, 0), (0, b_pad - b_in)))

    bucket = jnp.asarray(_rel_bucket_table())
    bias = rel_bias_table(rel_bias, bucket)
    tm = _row_tile(s)
    n13 = 2 * f // c13
    na_t, nb_t = _col_tiles(a_in), _col_tiles(b_pad)
    wa_t, wb_t = a_in // na_t, b_pad // nb_t

    def ffn_forward(xv, l, half, sub):
        a, h = modmm(xv, mv, (l, sub), w13, pl.BlockSpec((None, None, d, c13), lambda i, n: (l, half, 0, n)), n13,
                     jax.ShapeDtypeStruct((2, s, f), BF16),
                     pl.BlockSpec((None, tm, c13), lambda i, n: (n // 2, i, n % 2)), f"ffn_up_{l}_{half}", True)
        xo, y, u = resmm(xv, mv, (l, sub), 0.5, a, pl.BlockSpec((2, tm, c13), lambda i, k: (0, i, k)),
                         w2, pl.BlockSpec((None, None, c13, d), lambda i, k: (l, half, k, 0)), f // c13, c13,
                         f"ffn_down_{l}_{half}", True)
        return xo, dict(x=xv, h=h, a=a, u=u, y=y)

    saved = []
    xv = x0
    for l in range(n_layers):
        j = l // 2
        xv, s0 = ffn_forward(xv, l, 0, 0)
        if l % 2 == 0:
            proj, h = modmm(xv, mv, (l, 1), wa_in, pl.BlockSpec((None, d, wa_t), lambda i, n: (j, 0, n)), na_t,
                            jax.ShapeDtypeStruct((s, a_in), BF16), pl.BlockSpec((tm, wa_t), lambda i, n: (i, n)),
                            f"swa_in_{j}", True)
            gains = jnp.concatenate([jnp.tile(swa_q_g[j] * HEAD_DIM ** -0.5, hq), jnp.tile(swa_k_g[j], hkv)])[None, :]
            qkn = qknorm_fwd(proj, gains, f"swa_qknorm_{j}")
            sink = swa_sink[j][None, :]
            o, lse = swa_fwd(qkn, proj, bias, sink, d, f"swa_attn_{j}")
            s1 = dict(x=xv, h=h, proj=proj, gains=gains, qkn=qkn, sink=sink, o=o, lse=lse)
            w_out, w_out_spec = wa_out, pl.BlockSpec((None, d, d), lambda i, k: (j, 0, 0))
        else:
            proj, h = modmm(xv, mv, (l, 1), wb_in, pl.BlockSpec((None, d, wb_t), lambda i, n: (j, 0, n)), nb_t,
                            jax.ShapeDtypeStruct((s, b_pad), BF16), pl.BlockSpec((tm, wb_t), lambda i, n: (i, n)),
                            f"fox_in_{j}", True)
            fl, _ = modmm(xv, mv, (l, 1), wb_in, pl.BlockSpec((None, d, LANES), lambda i, n: (j, 0, 3 * d // LANES)), 1,
                          jax.ShapeDtypeStruct((s, LANES), F32), pl.BlockSpec((tm, LANES), lambda i, n: (i, 0)),
                          f"fox_gate_in_{j}", False)
            b_f = jnp.pad(fox_b_f[j], (0, LANES - hq))[None, :]
            fcum = fox_gates(fl, b_f, f"fox_gates_{j}")
            f_t = fcum[:, :hq].T
            f_col, f_row = f_t[:, :, None], f_t[:, None, :]
            gains = jnp.concatenate([jnp.tile(fox_q_g[j] * HEAD_DIM ** -0.5, hq), jnp.tile(fox_k_g[j], hq)])[None, :]
            qkn = qknorm_fwd(proj, gains, f"fox_qknorm_{j}")
            o, o32, lse = fox_fwd(qkn, proj, f_col, f_row, d, f"fox_attn_{j}")
            s1 = dict(x=xv, h=h, proj=proj, gains=gains, qkn=qkn, fl=fl, b_f=b_f, f_col=f_col, f_row=f_row, o=o, o32=o32,
                      lse=lse)
            w_out, w_out_spec = wb_out, pl.BlockSpec((None, d, d), lambda i, k: (j, 0, 0))
        xv, y = resmm(xv, mv, (l, 1), 1.0, o, pl.BlockSpec((tm, d), lambda i, k: (i, 0)), w_out, w_out_spec, 1, d,
                      f"mixer_out_{l}", False)
        s1["y"] = y
        xv, s2 = ffn_forward(xv, l, 1, 2)
        saved.append((s0, s1, s2))

    dxv, loss_part = loss_head(xv, loss_target[0])
    loss = lax.psum(loss_part[0, 0], ("x", "y", "c"))

    g13 = lax.empty((n_layers, 2, d, 2 * f), BF16)
    g2 = lax.empty((n_layers, 2, f, d), BF16)
    ga_in = lax.empty((n_a, d, a_in), BF16)
    ga_out = lax.empty((n_a, d, d), BF16)
    gb_in = lax.empty((n_b, d, b_pad), BF16)
    gb_out = lax.empty((n_b, d, d), BF16)
    dmod = [[None] * 3 for _ in range(n_layers)]
    dnorm = [[None] * 3 for _ in range(n_layers)]
    dqk_gain = {}
    dsink, db_f, dbias_tabs = {}, {}, []

    def ffn_backward(dxo, sv, l, half, sub):
        nonlocal g13, g2
        dy, da, dgate = resmm_bwd(dxo, sv["y"], mv, (l, sub), 0.5, w2,
                                  pl.BlockSpec((None, None, c13, d), lambda i, k: (l, half, k, 0)), f // c13, c13,
                                  f"ffn_down_bwd_{l}_{half}", a=sv["a"])
        g2 = weight_grad(sv["u"], pl.BlockSpec((tm, c13), lambda m, n, k: (k, m)), dy,
                         pl.BlockSpec((tm, d), lambda m, n, k: (k, 0)), (f // c13, 1), s, c13, d, g2,
                         pl.BlockSpec((None, None, c13, d), lambda m, n, k: (l, half, m, 0)), f"ffn_w2_grad_{l}_{half}")
        g13 = weight_grad(sv["h"], pl.BlockSpec((tm, d), lambda m, n, k: (k, 0)), da,
                          pl.BlockSpec((None, tm, c13), lambda m, n, k: (n // 2, k, n % 2)), (1, n13), s, d, c13, g13,
                          pl.BlockSpec((None, None, d, c13), lambda m, n, k: (l, half, 0, n)), f"ffn_w13_grad_{l}_{half}")
        dx, red = modmm_bwd(da, pl.BlockSpec((None, tm, c13), lambda i, n: (n // 2, i, n % 2)), w13,
                            pl.BlockSpec((None, None, d, c13), lambda i, n: (l, half, 0, n)), n13, sv["x"], dxo, mv,
                            (l, sub), f"ffn_up_bwd_{l}_{half}")
        dmod[l][sub] = (red[1], red[2], dgate[0])
        dnorm[l][sub] = red[0]
        return dx

    for l in reversed(range(n_layers)):
        j = l // 2
        s0, s1, s2 = saved[l]
        dxv = ffn_backward(dxv, s2, l, 1, 2)
        is_a = l % 2 == 0
        w_out = wa_out if is_a else wb_out
        dy, do, dgate = resmm_bwd(dxv, s1["y"], mv, (l, 1), 1.0, w_out, pl.BlockSpec((None, d, d), lambda i, k: (j, 0, 0)),
                                  1, d, f"mixer_out_bwd_{l}")
        g_out = weight_grad(s1["o"], pl.BlockSpec((tm, d), lambda m, n, k: (k, 0)), dy,
                            pl.BlockSpec((tm, d), lambda m, n, k: (k, 0)), (1, 1), s, d, d, ga_out if is_a else gb_out,
                            pl.BlockSpec((None, d, d), lambda m, n, k: (j, 0, 0)), f"mixer_out_grad_{l}")
        if is_a:
            ga_out = g_out
            d_qkv, dbias_tab, dsk = swa_bwd(s1["qkn"], s1["proj"], bias, s1["sink"], do, s1["o"], s1["lse"], d,
                                            f"swa_attn_bwd_{j}")
            dbias_tabs.append(dbias_tab)
            dsink[j] = dsk[0, :hq]
            dproj, dgain = qknorm_bwd(s1["proj"], s1["gains"], d_qkv, pl.BlockSpec((tm, LANES), lambda c_, i: (i, c_)),
                                      a_in, f"swa_qknorm_bwd_{j}")
            ga_in = weight_grad(s1["h"], pl.BlockSpec((tm, d), lambda m, n, k: (k, 0)), dproj,
                                pl.BlockSpec((tm, wa_t), lambda m, n, k: (k, n)), (1, na_t), s, d, wa_t, ga_in,
                                pl.BlockSpec((None, d, wa_t), lambda m, n, k: (j, 0, n)), f"swa_in_grad_{j}")
            dxv, red = modmm_bwd(dproj, pl.BlockSpec((tm, wa_t), lambda i, n: (i, n)), wa_in,
                                 pl.BlockSpec((None, d, wa_t), lambda i, n: (j, 0, n)), na_t, s1["x"], dxv, mv, (l, 1),
                                 f"swa_in_bwd_{j}")
            dqk_gain[("a", j)] = (dgain[0, :d].reshape(hq, HEAD_DIM).sum(0) * HEAD_DIM ** -0.5,
                                  dgain[0, d:d + kw].reshape(hkv, HEAD_DIM).sum(0))
        else:
            gb_out = g_out
            lse_row = s1["lse"].reshape(hq, 1, s)
            d_qkv, df_col, dfq_row = fox_bwd(s1["qkn"], s1["proj"], s1["f_col"], s1["f_row"], lse_row, do, s1["o32"], d,
                                             f"fox_attn_bwd_{j}")
            lanes_of_heads = lambda a: jnp.pad(a.T, ((0, 0), (0, LANES - hq)))
            dfl, dbf = fox_gates_bwd(s1["fl"], s1["b_f"], lanes_of_heads(dfq_row[:, 0, :]), lanes_of_heads(df_col[:, :, 0]),
                                     f"fox_gates_bwd_{j}")
            db_f[j] = dbf[0, :hq]
            npd = d // LANES
            dproj, dgain = qknorm_bwd(
                s1["proj"], s1["gains"], d_qkv,
                pl.BlockSpec((None, tm, LANES), lambda c_, i: (jnp.minimum(c_ // npd, 2), i, c_ % npd)),
                3 * d, f"fox_qknorm_bwd_{j}", extra=dfl)
            gb_in = weight_grad(s1["h"], pl.BlockSpec((tm, d), lambda m, n, k: (k, 0)), dproj,
                                pl.BlockSpec((tm, wb_t), lambda m, n, k: (k, n)), (1, nb_t), s, d, wb_t, gb_in,
                                pl.BlockSpec((None, d, wb_t), lambda m, n, k: (j, 0, n)), f"fox_in_grad_{j}")
            dxv, red = modmm_bwd(dproj, pl.BlockSpec((tm, wb_t), lambda i, n: (i, n)), wb_in,
                                 pl.BlockSpec((None, d, wb_t), lambda i, n: (j, 0, n)), nb_t, s1["x"], dxv, mv, (l, 1),
                                 f"fox_in_bwd_{j}")
            dqk_gain[("b", j)] = (dgain[0, :d].reshape(hq, HEAD_DIM).sum(0) * HEAD_DIM ** -0.5,
                                  dgain[0, d:2 * d].reshape(hq, HEAD_DIM).sum(0))
        dmod[l][1] = (red[1], red[2], dgate[0])
        dnorm[l][1] = red[0]
        dxv = ffn_backward(dxv, s0, l, 0, 0)
    grad_x = dxv[None]

    drel = rel_bias_grad(jnp.stack(dbias_tabs), bucket)
    dmod_flat = jnp.stack([jnp.stack([jnp.stack(dmod[l][sub]) for sub in range(3)]) for l in range(n_layers)]).reshape(-1)
    dnorm_flat = jnp.stack([jnp.stack(dnorm[l]) for l in range(n_layers)]).reshape(-1)
    pieces = [dmod_flat, dnorm_flat,
              jnp.stack([dqk_gain[("a", j)][0] for j in range(n_a)]).reshape(-1),
              jnp.stack([dqk_gain[("a", j)][1] for j in range(n_a)]).reshape(-1),
              jnp.stack([dqk_gain[("b", j)][0] for j in range(n_b)]).reshape(-1),
              jnp.stack([dqk_gain[("b", j)][1] for j in range(n_b)]).reshape(-1),
              jnp.stack([dsink[j] for j in range(n_a)]).reshape(-1),
              jnp.stack([db_f[j] for j in range(n_b)]).reshape(-1),
              drel.reshape(-1)]
    rows = [_pad_rows(p) for p in pieces]
    starts = np.cumsum([0] + [r.shape[0] for r in rows])
    total = -(-int(starts[-1]) // 8) * 8
    small = jnp.pad(jnp.concatenate(rows), ((0, total - int(starts[-1])), (0, 0)))
    small_all = all_gather_rows(small, "gather_small_grads").reshape(N_DEV, total, LANES)
    small_sum = sum_devices(small_all)

    def piece(k, shape):
        n = int(np.prod(shape))
        return small_sum[int(starts[k]):int(starts[k + 1])].reshape(-1)[:n].reshape(shape)

    g_ada_b = piece(0, (n_layers, 9 * d))
    g_norm = lax.dynamic_slice_in_dim(piece(1, (n_layers, 3, d)), chip * cq, cq, axis=2)
    g_swa_q, g_swa_k = piece(2, (n_a, HEAD_DIM)), piece(3, (n_a, HEAD_DIM))
    g_fox_q, g_fox_k = piece(4, (n_b, HEAD_DIM)), piece(5, (n_b, HEAD_DIM))
    g_sink, g_bf, g_rel = piece(6, (n_a, hq)), piece(7, (n_b, hq)), piece(8, (REL_BUCKETS, hq))

    dmod_all = small_all[:, :int(starts[1])].reshape(N_DEV, -1)[:, :n_layers * 9 * d].reshape(N_DEV, n_layers, 9 * d)
    ada_cols = ada_w.shape[-1]
    dmod_mine = lax.dynamic_slice_in_dim(jnp.moveaxis(dmod_all, 0, 1), chip * ada_cols, ada_cols, axis=2)
    g_ada_w = ada_grad(c_all.T, dmod_mine)

    gb_in_slots = jnp.stack([gb_in[:, :, b * fx:(b + 1) * fx] for b in range(N_CHIP)])
    grads = [g13, g2, ga_in, ga_out, gb_in_slots, gb_out]
    shard_shapes = [ffn_w13.shape, ffn_w2.shape, swa_w_in.shape, swa_w_out.shape, fox_w_in.shape, fox_w_out.shape]
    landed = scatter_grads(grads, kinds, shard_shapes)
    parts = [sum_slots(r.reshape(N_CHIP, -1, r.shape[-1])) for r in landed]
    others = swap_with_sibling(parts)

    def update(w, m, v, g, g2=None):
        w2d = w.reshape(-1, w.shape[-1])
        outs = adamw(w2d, m.reshape(w2d.shape), v.reshape(w2d.shape), g.reshape(w2d.shape) if g2 is None else g, g2)
        return [t.reshape(w.shape) for t in outs]

    big = [(ffn_w13, m_ffn_w13, v_ffn_w13), (ffn_w2, m_ffn_w2, v_ffn_w2), (swa_w_in, m_swa_w_in, v_swa_w_in),
           (swa_w_out, m_swa_w_out, v_swa_w_out), (fox_w_in, m_fox_w_in, v_fox_w_in), (fox_w_out, m_fox_w_out, v_fox_w_out)]
    big_out = [update(w, m, v, p, q) for (w, m, v), p, q in zip(big, parts, others)]
    r_ada_w = update(ada_w, m_ada_w, v_ada_w, g_ada_w)
    r_ada_b = update(ada_b, m_ada_b, v_ada_b, g_ada_b)
    r_norm = update(norm_g, m_norm_g, v_norm_g, g_norm)
    r_rel = update(rel_bias, m_rel_bias, v_rel_bias, g_rel)
    r_swa_q = update(swa_q_g, m_swa_q_g, v_swa_q_g, g_swa_q)
    r_swa_k = update(swa_k_g, m_swa_k_g, v_swa_k_g, g_swa_k)
    r_sink = update(swa_sink, m_swa_sink, v_swa_sink, g_sink)
    r_bf = update(fox_b_f, m_fox_b_f, v_fox_b_f, g_bf)
    r_fox_q = update(fox_q_g, m_fox_q_g, v_fox_q_g, g_fox_q)
    r_fox_k = update(fox_k_g, m_fox_k_g, v_fox_k_g, g_fox_k)
    per_weight = [r_ada_w, r_ada_b, r_norm, big_out[0], big_out[1], r_rel, big_out[2], big_out[3], r_swa_q, r_swa_k,
                  r_sink, big_out[4], big_out[5], r_bf, r_fox_q, r_fox_k]
    return (loss, grad_x, *[r[0] for r in per_weight], *[r[1] for r in per_weight],
            *[r[2] for r in per_weight], *[r[3] for r in per_weight])
```

```python
import math

import numpy as np
import jax
import jax.numpy as jnp
from jax import lax
from jax.experimental import pallas as pl
from jax.experimental.pallas import tpu as pltpu

F32 = jnp.float32
BF16 = jnp.bfloat16
HEAD_DIM = 64
GROUP = 4
BLOCK = 128
REL_BUCKETS = 32
REL_MAX_DIST = 128
EPS = 1e-6
NEG = -1e30
N_CHIP = 4
N_DEV = 8
LANES = 128
VMEM_LIMIT = 52 * 1024 * 1024
ADAM_LR, ADAM_B1, ADAM_B2, ADAM_EPS, ADAM_WD, ADAM_STEP = 0.001, 0.9, 0.999, 1e-08, 0.01, 10
MESH = pl.DeviceIdType.MESH
ANY = pl.BlockSpec(memory_space=pl.ANY)


def _params(n_axes):
    return pltpu.CompilerParams(dimension_semantics=("arbitrary",) * n_axes, vmem_limit_bytes=VMEM_LIMIT)


def _nn(a, b):
    return jnp.dot(a, b, preferred_element_type=F32)


def _nt(a, b):
    return lax.dot_general(a, b, (((1,), (1,)), ((), ())), preferred_element_type=F32)


def _tn(a, b):
    return lax.dot_general(a, b, (((0,), (0,)), ((), ())), preferred_element_type=F32)


def _sigmoid(z):
    return 1.0 / (1.0 + jnp.exp(-z))


def _row_tile(s):
    return 512 if s >= 2048 else s // 2


def _attn_tile(s):
    return 512 if s >= 2048 else s // 4


def _position():
    x, y, c = lax.axis_index("x"), lax.axis_index("y"), lax.axis_index("c")
    chips = [(1 - x, y), (x, 1 - y), (1 - x, 1 - y)]
    return x, y, c, chips


def all_gather_rows(v, name):
    m_per, n = v.shape

    def body(x_ref, out_ref, send_sems, recv_sems, local_sem):
        x, y, c, chips = _position()
        me, sibling = (x, y, c), (x, y, 1 - c)

        def rows(px, py, pc):
            return out_ref.at[pl.ds((4 * px + 2 * py + pc) * m_per, m_per), :]

        def copy(k, block, to, src=None):
            return pltpu.make_async_remote_copy(
                src_ref=rows(*block) if src is None else src, dst_ref=rows(*block),
                send_sem=send_sems.at[k], recv_sem=recv_sems.at[k], device_id=to, device_id_type=MESH)

        mine = pltpu.make_async_copy(x_ref, rows(*me), local_sem)
        mine.start()
        first = [copy(0, me, sibling, src=x_ref)]
        first += [copy(1 + j, me, (*chip, c), src=x_ref) for j, chip in enumerate(chips)]
        for cp in first:
            cp.start()
        passed = [copy(4 + j, (*chip, c), sibling) for j, chip in enumerate(chips)]
        for j, chip in enumerate(chips):
            copy(1 + j, (*chip, c), me).wait_recv()
            passed[j].start()
        copy(0, sibling, me).wait_recv()
        for j, chip in enumerate(chips):
            copy(4 + j, (*chip, 1 - c), me).wait_recv()
        for cp in first + passed:
            cp.wait_send()
        mine.wait()

    return pl.pallas_call(
        body, name=name,
        out_shape=jax.ShapeDtypeStruct((N_DEV * m_per, n), v.dtype),
        in_specs=[pl.BlockSpec(memory_space=pltpu.VMEM)],
        out_specs=pl.BlockSpec(memory_space=pltpu.VMEM),
        scratch_shapes=[pltpu.SemaphoreType.DMA((7,)), pltpu.SemaphoreType.DMA((7,)), pltpu.SemaphoreType.DMA],
    )(v)


def _slab(full_ref, kind, b, lead):
    how = kind[0]
    if how == "slot":
        return full_ref.at[b, lead]
    if how == "col":
        w = kind[1]
        idx = (lead,) + (slice(None),) * (len(full_ref.shape) - 2) + (pl.ds(pl.multiple_of(b * w, LANES), w),)
        return full_ref.at[idx]
    h = kind[1]
    idx = (lead,) + (slice(None),) * (len(full_ref.shape) - 3) + (pl.ds(pl.multiple_of(b * h, 8), h), slice(None))
    return full_ref.at[idx]


def _full_shape(shard_shape, kind):
    if kind[0] == "slot":
        return (N_CHIP,) + tuple(shard_shape)
    if kind[0] == "col":
        return tuple(shard_shape[:-1]) + (N_CHIP * shard_shape[-1],)
    return tuple(shard_shape[:-2]) + (N_CHIP * shard_shape[-2], shard_shape[-1])


def gather_weights(shards, kinds):
    n = len(shards)

    def body(*refs):
        ins, outs = refs[:n], refs[n:2 * n]
        send_sems, recv_sems, local_sems = refs[2 * n:]
        x, y, c, chips = _position()
        b_me = 2 * x + y
        sibling = (x, y, 1 - c)
        local, sends = [], []
        for t in range(n):
            half = ins[t].shape[0] // 2
            mine = pl.ds(c * half, half)
            whole = pl.ds(0, 2 * half)
            cp = pltpu.make_async_copy(ins[t], _slab(outs[t], kinds[t], b_me, whole), local_sems.at[t])
            cp.start()
            local.append(cp)
            for j, chip in enumerate(chips):
                cp = pltpu.make_async_remote_copy(
                    src_ref=ins[t].at[mine], dst_ref=_slab(outs[t], kinds[t], b_me, mine),
                    send_sem=send_sems.at[6 * t + j], recv_sem=recv_sems.at[6 * t + j],
                    device_id=(*chip, c), device_id_type=MESH)
                cp.start()
                sends.append(cp)
        for t in range(n):
            half = ins[t].shape[0] // 2
            mine = pl.ds(c * half, half)
            for j, chip in enumerate(chips):
                landed = _slab(outs[t], kinds[t], 2 * chip[0] + chip[1], mine)
                pltpu.make_async_remote_copy(
                    src_ref=landed, dst_ref=landed, send_sem=send_sems.at[6 * t + j], recv_sem=recv_sems.at[6 * t + j],
                    device_id=(*chip, c), device_id_type=MESH).wait_recv()
                cp = pltpu.make_async_remote_copy(
                    src_ref=landed, dst_ref=landed, send_sem=send_sems.at[6 * t + 3 + j],
                    recv_sem=recv_sems.at[6 * t + 3 + j], device_id=sibling, device_id_type=MESH)
                cp.start()
                sends.append(cp)
        for t in range(n):
            half = ins[t].shape[0] // 2
            theirs = pl.ds((1 - c) * half, half)
            for j, chip in enumerate(chips):
                landed = _slab(outs[t], kinds[t], 2 * chip[0] + chip[1], theirs)
                pltpu.make_async_remote_copy(
                    src_ref=landed, dst_ref=landed, send_sem=send_sems.at[6 * t + 3 + j],
                    recv_sem=recv_sems.at[6 * t + 3 + j], device_id=sibling, device_id_type=MESH).wait_recv()
        for cp in sends:
            cp.wait_send()
        for cp in local:
            cp.wait()

    return pl.pallas_call(
        body, name="gather_weights",
        out_shape=[jax.ShapeDtypeStruct(_full_shape(s.shape, k), s.dtype) for s, k in zip(shards, kinds)],
        in_specs=[ANY] * n, out_specs=[ANY] * n,
        scratch_shapes=[pltpu.SemaphoreType.DMA((6 * n,)), pltpu.SemaphoreType.DMA((6 * n,)),
                        pltpu.SemaphoreType.DMA((n,))],
    )(*shards)


def scatter_grads(grads, kinds, shard_shapes):
    n = len(grads)

    def body(*refs):
        ins, outs = refs[:n], refs[n:2 * n]
        send_sems, recv_sems, local_sems = refs[2 * n:]
        x, y, c, chips = _position()
        b_me = 2 * x + y
        local, sends = [], []
        for t in range(n):
            whole = pl.ds(0, ins[t].shape[1] if kinds[t][0] == "slot" else ins[t].shape[0])
            cp = pltpu.make_async_copy(_slab(ins[t], kinds[t], b_me, whole), outs[t].at[3], local_sems.at[t])
            cp.start()
            local.append(cp)
            for j, chip in enumerate(chips):
                cp = pltpu.make_async_remote_copy(
                    src_ref=_slab(ins[t], kinds[t], 2 * chip[0] + chip[1], whole), dst_ref=outs[t].at[j],
                    send_sem=send_sems.at[3 * t + j], recv_sem=recv_sems.at[3 * t + j],
                    device_id=(*chip, c), device_id_type=MESH)
                cp.start()
                sends.append(cp)
        for t in range(n):
            for j, chip in enumerate(chips):
                pltpu.make_async_remote_copy(
                    src_ref=outs[t].at[j], dst_ref=outs[t].at[j], send_sem=send_sems.at[3 * t + j],
                    recv_sem=recv_sems.at[3 * t + j], device_id=(*chip, c), device_id_type=MESH).wait_recv()
        for cp in sends:
            cp.wait_send()
        for cp in local:
            cp.wait()

    return pl.pallas_call(
        body, name="scatter_grads",
        out_shape=[jax.ShapeDtypeStruct((N_CHIP,) + tuple(s), g.dtype) for g, s in zip(grads, shard_shapes)],
        in_specs=[ANY] * n, out_specs=[ANY] * n,
        scratch_shapes=[pltpu.SemaphoreType.DMA((3 * n,)), pltpu.SemaphoreType.DMA((3 * n,)),
                        pltpu.SemaphoreType.DMA((n,))],
    )(*grads)


def swap_with_sibling(parts):
    n = len(parts)

    def body(*refs):
        ins, outs = refs[:n], refs[n:2 * n]
        send_sems, recv_sems = refs[2 * n:]
        x, y, c, _ = _position()
        cps = []
        for t in range(n):
            cp = pltpu.make_async_remote_copy(
                src_ref=ins[t], dst_ref=outs[t], send_sem=send_sems.at[t], recv_sem=recv_sems.at[t],
                device_id=(x, y, 1 - c), device_id_type=MESH)
            cp.start()
            cps.append(cp)
        for cp in cps:
            cp.wait_recv()
        for cp in cps:
            cp.wait_send()

    return pl.pallas_call(
        body, name="swap_with_sibling",
        out_shape=[jax.ShapeDtypeStruct(p.shape, p.dtype) for p in parts],
        in_specs=[ANY] * n, out_specs=[ANY] * n,
        scratch_shapes=[pltpu.SemaphoreType.DMA((n,)), pltpu.SemaphoreType.DMA((n,))],
    )(*parts)


def _modulated(xv, mv_ref):
    g, shift, scale = mv_ref[0:1, :], mv_ref[1:2, :], mv_ref[2:3, :]
    r = lax.rsqrt(jnp.mean(xv * xv, axis=-1, keepdims=True) + EPS)
    xhat = xv * r
    xn = xhat * g
    return xn * (1.0 + scale) + shift, xhat, xn, r, g, scale


def modmm(x, mv, mv_idx, w, w_spec, n_tiles, out_shape, out_spec, name, want_h):
    s, d = x.shape
    tm = _row_tile(s)

    def body(x_ref, mv_ref, w_ref, *rest):
        if want_h:
            out_ref, h_ref, h_scr = rest
        else:
            out_ref, h_scr = rest

        @pl.when(pl.program_id(1) == 0)
        def _():
            h = _modulated(x_ref[...], mv_ref)[0].astype(BF16)
            h_scr[...] = h
            if want_h:
                h_ref[...] = h

        out_ref[...] = _nn(h_scr[...], w_ref[...]).astype(out_ref.dtype)

    out_shapes = [out_shape]
    out_specs = [out_spec]
    if want_h:
        out_shapes.append(jax.ShapeDtypeStruct((s, d), BF16))
        out_specs.append(pl.BlockSpec((tm, d), lambda i, n: (i, 0)))
    res = pl.pallas_call(
        body, name=name, grid=(s // tm, n_tiles),
        in_specs=[pl.BlockSpec((tm, d), lambda i, n: (i, 0)),
                  pl.BlockSpec((None, None, 8, d), lambda i, n: (*mv_idx, 0, 0)), w_spec],
        out_specs=out_specs, out_shape=out_shapes,
        scratch_shapes=[pltpu.VMEM((tm, d), BF16)],
        compiler_params=_params(2),
    )(x, mv, w)
    return res if want_h else (res[0], None)


def resmm(x, mv, mv_idx, coef, lhs, lhs_spec, w, w_spec, k_tiles, tk, name, ffn):
    s, d = x.shape
    tm = _row_tile(s)
    kdim = k_tiles * tk

    def body(x_ref, mv_ref, lhs_ref, w_ref, xo_ref, y_ref, *rest):
        if ffn:
            u_ref, acc = rest
        else:
            (acc,) = rest
        k = pl.program_id(1)

        @pl.when(k == 0)
        def _():
            acc[...] = jnp.zeros_like(acc)

        if ffn:
            ag = lhs_ref[0].astype(F32)
            au = lhs_ref[1].astype(F32)
            left = (ag * _sigmoid(ag) * au).astype(BF16)
            u_ref[...] = left
        else:
            left = lhs_ref[...]
        acc[...] += _nn(left, w_ref[...])

        @pl.when(k == k_tiles - 1)
        def _():
            y = acc[...]
            y_ref[...] = y.astype(BF16)
            xo_ref[...] = x_ref[...] + (coef * mv_ref[3:4, :]) * y

    row = pl.BlockSpec((tm, d), lambda i, k: (i, 0))
    out_shapes = [jax.ShapeDtypeStruct((s, d), F32), jax.ShapeDtypeStruct((s, d), BF16)]
    out_specs = [row, row]
    if ffn:
        out_shapes.append(jax.ShapeDtypeStruct((s, kdim), BF16))
        out_specs.append(pl.BlockSpec((tm, tk), lambda i, k: (i, k)))
    return pl.pallas_call(
        body, name=name, grid=(s // tm, k_tiles),
        in_specs=[row, pl.BlockSpec((None, None, 8, d), lambda i, k: (*mv_idx, 0, 0)), lhs_spec, w_spec],
        out_specs=out_specs, out_shape=out_shapes,
        scratch_shapes=[pltpu.VMEM((tm, d), F32)],
        compiler_params=_params(2),
    )(x, mv, lhs, w)


def resmm_bwd(dxo, y, mv, mv_idx, coef, w, w_spec, k_tiles, tk, name, a=None):
    s, d = dxo.shape
    tm = _row_tile(s)
    kdim = k_tiles * tk
    ffn = a is not None

    def body(dxo_ref, y_ref, mv_ref, w_ref, *rest):
        if ffn:
            a_ref, dy_ref, dl_ref, dgate_ref, dy_scr = rest
        else:
            dy_ref, dl_ref, dgate_ref, dy_scr = rest
        i, k = pl.program_id(0), pl.program_id(1)

        @pl.when((i == 0) & (k == 0))
        def _():
            dgate_ref[...] = jnp.zeros_like(dgate_ref)

        @pl.when(k == 0)
        def _():
            dxv = dxo_ref[...]
            dy = ((coef * mv_ref[3:4, :]) * dxv).astype(BF16)
            dy_scr[...] = dy
            dy_ref[...] = dy
            dgate_ref[0:1, :] += jnp.sum(coef * dxv * y_ref[...].astype(F32), axis=0, keepdims=True)

        dl = _nt(dy_scr[...], w_ref[...])
        if ffn:
            ag = a_ref[0].astype(F32)
            au = a_ref[1].astype(F32)
            sg = _sigmoid(ag)
            dl_ref[0] = (dl * au * (sg * (1.0 + ag * (1.0 - sg)))).astype(BF16)
            dl_ref[1] = (dl * (ag * sg)).astype(BF16)
        else:
            dl_ref[...] = dl.astype(BF16)

    row = pl.BlockSpec((tm, d), lambda i, k: (i, 0))
    in_specs = [row, row, pl.BlockSpec((None, None, 8, d), lambda i, k: (*mv_idx, 0, 0)), w_spec]
    ops = [dxo, y, mv, w]
    if ffn:
        in_specs.append(pl.BlockSpec((2, tm, tk), lambda i, k: (0, i, k)))
        ops.append(a)
        dl_shape = jax.ShapeDtypeStruct((2, s, kdim), BF16)
        dl_spec = pl.BlockSpec((2, tm, tk), lambda i, k: (0, i, k))
    else:
        dl_shape = jax.ShapeDtypeStruct((s, kdim), BF16)
        dl_spec = pl.BlockSpec((tm, tk), lambda i, k: (i, k))
    return pl.pallas_call(
        body, name=name, grid=(s // tm, k_tiles),
        in_specs=in_specs,
        out_specs=[row, dl_spec, pl.BlockSpec((8, d), lambda i, k: (0, 0))],
        out_shape=[jax.ShapeDtypeStruct((s, d), BF16), dl_shape, jax.ShapeDtypeStruct((8, d), F32)],
        scratch_shapes=[pltpu.VMEM((tm, d), BF16)],
        compiler_params=_params(2),
    )(*ops)


def modmm_bwd(dl, dl_spec, w, w_spec, n_tiles, x, dxo, mv, mv_idx, name, more=None):
    s, d = x.shape
    tm = _row_tile(s)

    def body(dl_ref, w_ref, x_ref, dxo_ref, mv_ref, *rest):
        if more is not None:
            dl2_ref, w2_ref, dx_ref, red_ref, acc = rest
        else:
            dx_ref, red_ref, acc = rest
        i, n = pl.program_id(0), pl.program_id(1)

        @pl.when((i == 0) & (n == 0))
        def _():
            red_ref[...] = jnp.zeros_like(red_ref)

        @pl.when(n == 0)
        def _():
            if more is not None:
                acc[...] = _nt(dl2_ref[...], w2_ref[...])
            else:
                acc[...] = jnp.zeros_like(acc)

        acc[...] += _nt(dl_ref[...], w_ref[...])

        @pl.when(n == n_tiles - 1)
        def _():
            dh = acc[...]
            _, xhat, xn, r, g, scale = _modulated(x_ref[...], mv_ref)
            dxn = dh * (1.0 + scale)
            red_ref[0:1, :] += jnp.sum(dxn * xhat, axis=0, keepdims=True)
            red_ref[1:2, :] += jnp.sum(dh, axis=0, keepdims=True)
            red_ref[2:3, :] += jnp.sum(dh * xn, axis=0, keepdims=True)
            gd = dxn * g
            dx_ref[...] = dxo_ref[...] + r * (gd - xhat * jnp.mean(gd * xhat, axis=-1, keepdims=True))

    row = pl.BlockSpec((tm, d), lambda i, n: (i, 0))
    in_specs = [dl_spec, w_spec, row, row, pl.BlockSpec((None, None, 8, d), lambda i, n: (*mv_idx, 0, 0))]
    ops = [dl, w, x, dxo, mv]
    if more is not None:
        in_specs += [more[1], more[2]]
        ops += [more[0], w]
    return pl.pallas_call(
        body, name=name, grid=(s // tm, n_tiles),
        in_specs=in_specs,
        out_specs=[row, pl.BlockSpec((8, d), lambda i, n: (0, 0))],
        out_shape=[jax.ShapeDtypeStruct((s, d), F32), jax.ShapeDtypeStruct((8, d), F32)],
        scratch_shapes=[pltpu.VMEM((tm, d), F32)],
        compiler_params=_params(2),
    )(*ops)


def weight_grad(a, a_spec, b, b_spec, grid_mn, s, bm, bn, dest, out_spec, name):
    tk = _row_tile(s)
    k_tiles = s // tk

    def body(a_ref, b_ref, dest_ref, out_ref, acc):
        k = pl.program_id(2)

        @pl.when(k == 0)
        def _():
            acc[...] = jnp.zeros_like(acc)

        acc[...] += _tn(a_ref[...], b_ref[...])

        @pl.when(k == k_tiles - 1)
        def _():
            out_ref[...] = acc[...].astype(out_ref.dtype)

    return pl.pallas_call(
        body, name=name, grid=(*grid_mn, k_tiles),
        in_specs=[a_spec, b_spec, ANY], out_specs=out_spec,
        out_shape=jax.ShapeDtypeStruct(dest.shape, dest.dtype),
        input_output_aliases={2: 0},
        scratch_shapes=[pltpu.VMEM((bm, bn), F32)],
        compiler_params=_params(3),
    )(a, b, dest)


def _head_mean(v):
    lane = lax.broadcasted_iota(jnp.int32, v.shape, 1)
    lo = jnp.sum(jnp.where(lane < HEAD_DIM, v, 0.0), axis=-1, keepdims=True)
    hi = jnp.sum(v, axis=-1, keepdims=True) - lo
    return jnp.where(lane < HEAD_DIM, lo, hi) * (1.0 / HEAD_DIM)


def qknorm_fwd(proj, gains, width, name):
    s = proj.shape[0]
    nqk = gains.shape[1]
    tm = _row_tile(s)

    def body(p_ref, g_ref, o_ref):
        for cc in range(width // LANES):
            sl = slice(cc * LANES, (cc + 1) * LANES)
            xv = p_ref[:, sl].astype(F32)
            r = lax.rsqrt(_head_mean(xv * xv) + EPS)
            o_ref[:, sl] = (xv * r * g_ref[:, sl]).astype(BF16)

    blk = pl.BlockSpec((tm, width), lambda i, c: (i, c))
    return pl.pallas_call(
        body, name=name, grid=(s // tm, nqk // width),
        in_specs=[blk, pl.BlockSpec((1, width), lambda i, c: (0, c))],
        out_specs=blk, out_shape=jax.ShapeDtypeStruct((s, nqk), BF16),
        compiler_params=_params(2),
    )(proj, gains)


def qknorm_bwd(proj, gains, d, d_spec, n_cols, width, name):
    s = proj.shape[0]
    nqk = gains.shape[1] // width
    n_blocks = n_cols // width
    tm = _row_tile(s)

    def body(p_ref, g_ref, d_ref, o_ref, dg_ref):
        c, i = pl.program_id(0), pl.program_id(1)

        @pl.when(i == 0)
        def _():
            dg_ref[...] = jnp.zeros_like(dg_ref)

        @pl.when(c < nqk)
        def _():
            for cc in range(width // LANES):
                sl = slice(cc * LANES, (cc + 1) * LANES)
                xv = p_ref[:, sl].astype(F32)
                r = lax.rsqrt(_head_mean(xv * xv) + EPS)
                xhat = xv * r
                dv = d_ref[:, sl]
                gd = dv * g_ref[:, sl]
                o_ref[:, sl] = (r * (gd - xhat * _head_mean(gd * xhat))).astype(BF16)
                dg_ref[0:1, sl] += jnp.sum(dv * xhat, axis=0, keepdims=True)

        @pl.when(c >= nqk)
        def _():
            o_ref[...] = d_ref[...].astype(BF16)

    return pl.pallas_call(
        body, name=name, grid=(n_blocks, s // tm),
        in_specs=[pl.BlockSpec((tm, width), lambda c, i: (i, c)),
                  pl.BlockSpec((1, width), lambda c, i: (0, jnp.minimum(c, nqk - 1))), d_spec],
        out_specs=[pl.BlockSpec((tm, width), lambda c, i: (i, c)), pl.BlockSpec((8, width), lambda c, i: (0, c))],
        out_shape=[jax.ShapeDtypeStruct((s, n_cols), BF16), jax.ShapeDtypeStruct((8, n_cols), F32)],
        compiler_params=_params(2),
    )(proj, gains, d)


def _swa_mask(first):
    qi = lax.broadcasted_iota(jnp.int32, (BLOCK, 2 * BLOCK), 0) + BLOCK
    kj = lax.broadcasted_iota(jnp.int32, (BLOCK, 2 * BLOCK), 1)
    dist = qi - kj
    return (dist >= 0) & (dist < BLOCK) & ((kj >= BLOCK) | jnp.logical_not(first))


def swa_fwd(qkn, proj, bias, sink, d, name):
    s = qkn.shape[0]
    hq = d // HEAD_DIM
    hkv = hq // GROUP
    kw = hkv * HEAD_DIM
    nblk = s // BLOCK
    kcol = d // kw

    def body(q_ref, kc_ref, kp_ref, vc_ref, vp_ref, bias_ref, sink_ref, o_ref, lse_ref):
        mask = _swa_mask(pl.program_id(0) == 0)
        lse_ref[...] = jnp.zeros_like(lse_ref)
        for kvh in range(hkv):
            cols = slice(kvh * HEAD_DIM, (kvh + 1) * HEAD_DIM)
            k2 = jnp.concatenate([kp_ref[:, cols], kc_ref[:, cols]], axis=0)
            v2 = jnp.concatenate([vp_ref[:, cols], vc_ref[:, cols]], axis=0)
            for g in range(GROUP):
                h = kvh * GROUP + g
                hc = slice(h * HEAD_DIM, (h + 1) * HEAD_DIM)
                sc = jnp.where(mask, _nt(q_ref[:, hc], k2) + bias_ref[h], NEG)
                sk = sink_ref[0, h]
                m = jnp.maximum(jnp.max(sc, axis=-1, keepdims=True), sk)
                p = jnp.exp(sc - m)
                denom = jnp.sum(p, axis=-1, keepdims=True) + jnp.exp(sk - m)
                o_ref[:, hc] = (_nn(p.astype(BF16), v2) / denom).astype(BF16)
                lse_ref[:, h:h + 1] = m + jnp.log(denom)

    prev = lambda i: jnp.maximum(i - 1, 0)
    return pl.pallas_call(
        body, name=name, grid=(nblk,),
        in_specs=[pl.BlockSpec((BLOCK, d), lambda i: (i, 0)),
                  pl.BlockSpec((BLOCK, kw), lambda i: (i, kcol)),
                  pl.BlockSpec((BLOCK, kw), lambda i: (prev(i), kcol)),
                  pl.BlockSpec((BLOCK, kw), lambda i: (i, kcol + 1)),
                  pl.BlockSpec((BLOCK, kw), lambda i: (prev(i), kcol + 1)),
                  pl.BlockSpec((hq, BLOCK, 2 * BLOCK), lambda i: (0, 0, 0)),
                  pl.BlockSpec(memory_space=pltpu.SMEM)],
        out_specs=[pl.BlockSpec((BLOCK, d), lambda i: (i, 0)), pl.BlockSpec((BLOCK, LANES), lambda i: (i, 0))],
        out_shape=[jax.ShapeDtypeStruct((s, d), BF16), jax.ShapeDtypeStruct((s, LANES), F32)],
        compiler_params=_params(1),
    )(qkn, qkn, qkn, proj, proj, bias, sink)


def swa_bwd(qkn, proj, bias, sink, do, o, lse, d, name):
    s = qkn.shape[0]
    hq = d // HEAD_DIM
    hkv = hq // GROUP
    kw = hkv * HEAD_DIM
    nblk = s // BLOCK
    kcol = d // kw
    wide = d + 2 * kw

    def body(q_ref, kc_ref, kp_ref, vc_ref, vp_ref, bias_ref, sink_ref, do_ref, o_ref, lse_ref,
             out_ref, dbias_ref, dsink_ref, carry, fresh):
        i = pl.program_id(0)

        @pl.when(i == 0)
        def _():
            dbias_ref[...] = jnp.zeros_like(dbias_ref)
            dsink_ref[...] = jnp.zeros_like(dsink_ref)
            carry[...] = jnp.zeros_like(carry)

        @pl.when(i == nblk)
        def _():
            fresh[...] = jnp.zeros_like(fresh)

        @pl.when(i < nblk)
        def _():
            mask = _swa_mask(i == 0)
            for kvh in range(hkv):
                cols = slice(kvh * HEAD_DIM, (kvh + 1) * HEAD_DIM)
                k2 = jnp.concatenate([kp_ref[:, cols], kc_ref[:, cols]], axis=0)
                v2 = jnp.concatenate([vp_ref[:, cols], vc_ref[:, cols]], axis=0)
                dk2 = jnp.zeros((2 * BLOCK, HEAD_DIM), F32)
                dv2 = jnp.zeros((2 * BLOCK, HEAD_DIM), F32)
                for g in range(GROUP):
                    h = kvh * GROUP + g
                    hc = slice(h * HEAD_DIM, (h + 1) * HEAD_DIM)
                    q = q_ref[:, hc]
                    dov = do_ref[:, hc]
                    lse_h = lse_ref[:, h:h + 1]
                    sc = jnp.where(mask, _nt(q, k2) + bias_ref[h], NEG)
                    p = jnp.exp(sc - lse_h)
                    delta = jnp.sum(dov.astype(F32) * o_ref[:, hc].astype(F32), axis=-1, keepdims=True)
                    ds = p * (_nt(dov, v2) - delta)
                    dbias_ref[h] += ds
                    dsink_ref[0:1, h:h + 1] += jnp.sum(-jnp.exp(sink_ref[0, h] - lse_h) * delta, axis=0, keepdims=True)
                    dsb = ds.astype(BF16)
                    fresh[0, :, hc] = _nn(dsb, k2)
                    dk2 += _tn(dsb, q)
                    dv2 += _tn(p.astype(BF16), dov)
                kc_cols = slice(d + kvh * HEAD_DIM, d + (kvh + 1) * HEAD_DIM)
                vc_cols = slice(d + kw + kvh * HEAD_DIM, d + kw + (kvh + 1) * HEAD_DIM)
                fresh[0, :, kc_cols] = dk2[BLOCK:]
                fresh[0, :, vc_cols] = dv2[BLOCK:]
                fresh[1, :, kc_cols] = dk2[:BLOCK]
                fresh[1, :, vc_cols] = dv2[:BLOCK]

        lane = lax.broadcasted_iota(jnp.int32, (BLOCK, wide), 1)
        out_ref[...] = carry[...] + jnp.where(lane >= d, fresh[1], 0.0)

        @pl.when(i < nblk)
        def _():
            carry[...] = fresh[0]

    cur = lambda i: jnp.minimum(i, nblk - 1)
    prev = lambda i: jnp.maximum(jnp.minimum(i, nblk - 1) - 1, 0)
    return pl.pallas_call(
        body, name=name, grid=(nblk + 1,),
        in_specs=[pl.BlockSpec((BLOCK, d), lambda i: (cur(i), 0)),
                  pl.BlockSpec((BLOCK, kw), lambda i: (cur(i), kcol)),
                  pl.BlockSpec((BLOCK, kw), lambda i: (prev(i), kcol)),
                  pl.BlockSpec((BLOCK, kw), lambda i: (cur(i), kcol + 1)),
                  pl.BlockSpec((BLOCK, kw), lambda i: (prev(i), kcol + 1)),
                  pl.BlockSpec((hq, BLOCK, 2 * BLOCK), lambda i: (0, 0, 0)),
                  pl.BlockSpec(memory_space=pltpu.SMEM),
                  pl.BlockSpec((BLOCK, d), lambda i: (cur(i), 0)),
                  pl.BlockSpec((BLOCK, d), lambda i: (cur(i), 0)),
                  pl.BlockSpec((BLOCK, LANES), lambda i: (cur(i), 0))],
        out_specs=[pl.BlockSpec((BLOCK, wide), lambda i: (jnp.maximum(i - 1, 0), 0)),
                   pl.BlockSpec((hq, BLOCK, 2 * BLOCK), lambda i: (0, 0, 0)),
                   pl.BlockSpec((8, LANES), lambda i: (0, 0))],
        out_shape=[jax.ShapeDtypeStruct((s, wide), F32), jax.ShapeDtypeStruct((hq, BLOCK, 2 * BLOCK), F32),
                   jax.ShapeDtypeStruct((8, LANES), F32)],
        scratch_shapes=[pltpu.VMEM((BLOCK, wide), F32), pltpu.VMEM((2, BLOCK, wide), F32)],
        compiler_params=_params(1),
    )(qkn, qkn, qkn, proj, proj, bias, sink, do, o, lse)


def _rel_bucket_table():
    qi = np.arange(BLOCK)[:, None] + BLOCK
    kj = np.arange(2 * BLOCK)[None, :]
    n = np.maximum(qi - kj, 0)
    max_exact = REL_BUCKETS // 2
    nf = np.maximum(n, 1).astype(np.float32)
    large = max_exact + (np.log(nf / max_exact) / math.log(REL_MAX_DIST / max_exact)
                         * (REL_BUCKETS - max_exact)).astype(np.int32)
    large = np.minimum(large, REL_BUCKETS - 1)
    return np.where(n < max_exact, n, large).astype(np.int32)


def rel_bias_table(rel_bias, bucket):
    hq = rel_bias.shape[1]

    def body(rb_ref, bucket_ref, out_ref):
        tbl = bucket_ref[...]

        def per_head(h, carry):
            def per_bucket(b, acc):
                return jnp.where(tbl == b, rb_ref[b, h], acc)

            out_ref[h] = lax.fori_loop(0, REL_BUCKETS, per_bucket, jnp.zeros(tbl.shape, F32))
            return carry

        lax.fori_loop(0, hq, per_head, 0)

    return pl.pallas_call(
        body, name="rel_bias_table",
        in_specs=[pl.BlockSpec(memory_space=pltpu.SMEM), pl.BlockSpec(memory_space=pltpu.VMEM)],
        out_specs=pl.BlockSpec(memory_space=pltpu.VMEM),
        out_shape=jax.ShapeDtypeStruct((hq,) + tuple(bucket.shape), F32),
    )(rel_bias, bucket)


def rel_bias_grad(dbias, bucket):
    n_layers, hq = dbias.shape[:2]

    def body(db_ref, bucket_ref, out_ref):
        tbl = bucket_ref[...]

        def per_head(h, carry):
            dsum = db_ref[0, h]
            for a in range(1, n_layers):
                dsum = dsum + db_ref[a, h]

            def per_bucket(b, carry2):
                out_ref[b, h] = jnp.sum(jnp.where(tbl == b, dsum, 0.0))
                return carry2

            return lax.fori_loop(0, REL_BUCKETS, per_bucket, carry)

        lax.fori_loop(0, hq, per_head, 0)

    return pl.pallas_call(
        body, name="rel_bias_grad",
        in_specs=[pl.BlockSpec(memory_space=pltpu.VMEM), pl.BlockSpec(memory_space=pltpu.VMEM)],
        out_specs=pl.BlockSpec(memory_space=pltpu.SMEM),
        out_shape=jax.ShapeDtypeStruct((REL_BUCKETS, hq), F32),
    )(dbias, bucket)


def _split3(v):
    hi = v.astype(BF16)
    r1 = v - hi.astype(F32)
    mid = r1.astype(BF16)
    lo = (r1 - mid.astype(F32)).astype(BF16)
    return hi, mid, lo


def _tri_sum(tri, v):
    hi, mid, lo = _split3(v)
    return _nn(tri, hi) + _nn(tri, mid) + _nn(tri, lo)


def fox_gates(fl, b_f, name):
    s = fl.shape[0]
    t = _row_tile(s)

    def body(fl_ref, b_ref, f_ref, carry):
        @pl.when(pl.program_id(0) == 0)
        def _():
            carry[...] = jnp.zeros_like(carry)

        z = fl_ref[...] + b_ref[...]
        logf = jnp.minimum(z, 0.0) - jnp.log(1.0 + jnp.exp(-jnp.abs(z)))
        r = lax.broadcasted_iota(jnp.int32, (t, t), 0)
        cidx = lax.broadcasted_iota(jnp.int32, (t, t), 1)
        tri = jnp.where(cidx <= r, 1.0, 0.0).astype(BF16)
        f = _tri_sum(tri, logf) + carry[0:1, :]
        f_ref[...] = f
        carry[0:1, :] = f_ref[t - 1:t, :]

    blk = pl.BlockSpec((t, LANES), lambda i: (i, 0))
    return pl.pallas_call(
        body, name=name, grid=(s // t,),
        in_specs=[blk, pl.BlockSpec((1, LANES), lambda i: (0, 0))],
        out_specs=blk, out_shape=jax.ShapeDtypeStruct((s, LANES), F32),
        scratch_shapes=[pltpu.VMEM((8, LANES), F32)],
        compiler_params=_params(1),
    )(fl, b_f)


def fox_gates_bwd(fl, b_f, df_query, df_key, name):
    s = fl.shape[0]
    t = _row_tile(s)
    nb = s // t

    def body(fl_ref, b_ref, dfq_ref, dfk_ref, dfl_ref, db_ref, carry):
        @pl.when(pl.program_id(0) == 0)
        def _():
            carry[...] = jnp.zeros_like(carry)
            db_ref[...] = jnp.zeros_like(db_ref)

        dfv = dfq_ref[...] + dfk_ref[...]
        r = lax.broadcasted_iota(jnp.int32, (t, t), 0)
        cidx = lax.broadcasted_iota(jnp.int32, (t, t), 1)
        tri = jnp.where(cidx >= r, 1.0, 0.0).astype(BF16)
        dlog = _tri_sum(tri, dfv) + carry[0:1, :]
        carry[0:1, :] += jnp.sum(dfv, axis=0, keepdims=True)
        z = fl_ref[...] + b_ref[...]
        dz = dlog * (1.0 - _sigmoid(z))
        dfl_ref[...] = dz.astype(BF16)
        db_ref[0:1, :] += jnp.sum(dz, axis=0, keepdims=True)

    rev = pl.BlockSpec((t, LANES), lambda i: (nb - 1 - i, 0))
    return pl.pallas_call(
        body, name=name, grid=(nb,),
        in_specs=[rev, pl.BlockSpec((1, LANES), lambda i: (0, 0)), rev, rev],
        out_specs=[rev, pl.BlockSpec((8, LANES), lambda i: (0, 0))],
        out_shape=[jax.ShapeDtypeStruct((s, LANES), BF16), jax.ShapeDtypeStruct((8, LANES), F32)],
        scratch_shapes=[pltpu.VMEM((8, LANES), F32)],
        compiler_params=_params(1),
    )(fl, b_f, df_query, df_key)


def fox_fwd(qkn, proj, f_col, f_row, d, name):
    s = qkn.shape[0]
    t = _attn_tile(s)
    n_pairs = d // LANES
    nt = s // t

    def body(q_ref, k_ref, v_ref, fq_ref, fk_ref, o_ref, o32_ref, lse_ref, m_scr, l_scr, acc):
        i, j = pl.program_id(1), pl.program_id(2)

        @pl.when(j == 0)
        def _():
            m_scr[...] = jnp.full_like(m_scr, NEG)
            l_scr[...] = jnp.zeros_like(l_scr)
            acc[...] = jnp.zeros_like(acc)

        @pl.when(j <= i)
        def _():
            krow = lax.broadcasted_iota(jnp.int32, (t, t), 0)
            qcol = lax.broadcasted_iota(jnp.int32, (t, t), 1)
            visible = (krow <= qcol) | (j < i)
            for hh in range(2):
                hc = slice(hh * HEAD_DIM, (hh + 1) * HEAD_DIM)
                st = _nt(k_ref[:, hc], q_ref[:, hc]) + fq_ref[hh] - fk_ref[hh]
                st = jnp.where(visible, st, NEG)
                m_prev = m_scr[hh]
                m_new = jnp.maximum(m_prev, jnp.max(st, axis=0, keepdims=True))
                alpha = jnp.exp(m_prev - m_new)
                pt = jnp.exp(st - m_new)
                l_scr[hh] = alpha * l_scr[hh] + jnp.sum(pt, axis=0, keepdims=True)
                acc[hc, :] = alpha * acc[hc, :] + _tn(v_ref[:, hc], pt.astype(BF16))
                m_scr[hh] = m_new

        @pl.when(j == i)
        def _():
            l_full = jnp.concatenate([jnp.broadcast_to(l_scr[hh], (HEAD_DIM, t)) for hh in range(2)], axis=0)
            ov = (acc[...] / l_full).T
            o_ref[...] = ov.astype(BF16)
            o32_ref[...] = ov
            lse_ref[...] = m_scr[...] + jnp.log(l_scr[...])

    kv = lambda j, i: jnp.minimum(j, i)
    return pl.pallas_call(
        body, name=name, grid=(n_pairs, nt, nt),
        in_specs=[pl.BlockSpec((t, LANES), lambda p, i, j: (i, p)),
                  pl.BlockSpec((t, LANES), lambda p, i, j: (kv(j, i), n_pairs + p)),
                  pl.BlockSpec((t, LANES), lambda p, i, j: (kv(j, i), 2 * n_pairs + p)),
                  pl.BlockSpec((2, 1, t), lambda p, i, j: (p, 0, i)),
                  pl.BlockSpec((2, t, 1), lambda p, i, j: (p, kv(j, i), 0))],
        out_specs=[pl.BlockSpec((t, LANES), lambda p, i, j: (i, p)),
                   pl.BlockSpec((t, LANES), lambda p, i, j: (i, p)),
                   pl.BlockSpec((2, 1, t), lambda p, i, j: (p, 0, i))],
        out_shape=[jax.ShapeDtypeStruct((s, d), BF16), jax.ShapeDtypeStruct((s, d), F32),
                   jax.ShapeDtypeStruct((2 * n_pairs, 1, s), F32)],
        scratch_shapes=[pltpu.VMEM((2, 1, t), F32), pltpu.VMEM((2, 1, t), F32), pltpu.VMEM((2 * HEAD_DIM, t), F32)],
        compiler_params=_params(3),
    )(qkn, qkn, proj, f_row, f_col)


def fox_bwd(qkn, proj, f_col, f_row, lse_row, do, o, d, name):
    s = qkn.shape[0]
    t = _attn_tile(s)
    n_pairs = d // LANES
    nt = s // t

    def body(q_ref, k_ref, v_ref, fk_ref, fq_ref, lse_ref, do_ref, o_ref, out_ref, df_ref, dfq_ref,
             dq_acc, dkv_acc, df_acc, dfq_acc):
        j, i = pl.program_id(1), pl.program_id(2)

        @pl.when((j == 0) & (i == 0))
        def _():
            dq_acc[...] = jnp.zeros_like(dq_acc)
            dfq_acc[...] = jnp.zeros_like(dfq_acc)

        @pl.when(i == 0)
        def _():
            dkv_acc[...] = jnp.zeros_like(dkv_acc)
            df_acc[...] = jnp.zeros_like(df_acc)

        @pl.when(i >= j)
        def _():
            krow = lax.broadcasted_iota(jnp.int32, (t, t), 0)
            qcol = lax.broadcasted_iota(jnp.int32, (t, t), 1)
            visible = (krow <= qcol) | (i > j)
            ones = jnp.ones((8, HEAD_DIM), BF16)
            for hh in range(2):
                hc = slice(hh * HEAD_DIM, (hh + 1) * HEAD_DIM)
                q, k, v, dov = q_ref[:, hc], k_ref[:, hc], v_ref[:, hc], do_ref[:, hc]
                st = _nt(k, q) + fq_ref[hh] - fk_ref[hh]
                pt = jnp.exp(jnp.where(visible, st, NEG) - lse_ref[hh])
                hi, mid, lo = _split3(dov.astype(F32) * o_ref[:, hc])
                delta = jnp.max(_nt(ones, hi) + _nt(ones, mid) + _nt(ones, lo), axis=0, keepdims=True)
                dst = pt * (_nt(v, dov) - delta)
                dsb = dst.astype(BF16)
                dkv_acc[1, :, hc] += _nn(pt.astype(BF16), dov)
                dkv_acc[0, :, hc] += _nn(dsb, q)
                dq_acc[pl.ds(pl.multiple_of(i * t, t), t), hc] += _tn(dsb, k)
                df_acc[hh] -= jnp.sum(dst, axis=-1, keepdims=True)
                dfq_acc[i, hh] += jnp.sum(dst, axis=0, keepdims=True)

        @pl.when(i == nt - 1)
        def _():
            out_ref[0] = dq_acc[pl.ds(pl.multiple_of(j * t, t), t), :]
            out_ref[1] = dkv_acc[0]
            out_ref[2] = dkv_acc[1]
            df_ref[...] = df_acc[...]
            dfq_ref[...] = dfq_acc[j]

    qi = lambda j, i: jnp.maximum(i, j)
    return pl.pallas_call(
        body, name=name, grid=(n_pairs, nt, nt),
        in_specs=[pl.BlockSpec((t, LANES), lambda p, j, i: (qi(j, i), p)),
                  pl.BlockSpec((t, LANES), lambda p, j, i: (j, n_pairs + p)),
                  pl.BlockSpec((t, LANES), lambda p, j, i: (j, 2 * n_pairs + p)),
                  pl.BlockSpec((2, t, 1), lambda p, j, i: (p, j, 0)),
                  pl.BlockSpec((2, 1, t), lambda p, j, i: (p, 0, qi(j, i))),
                  pl.BlockSpec((2, 1, t), lambda p, j, i: (p, 0, qi(j, i))),
                  pl.BlockSpec((t, LANES), lambda p, j, i: (qi(j, i), p)),
                  pl.BlockSpec((t, LANES), lambda p, j, i: (qi(j, i), p))],
        out_specs=[pl.BlockSpec((3, t, LANES), lambda p, j, i: (0, j, p)),
                   pl.BlockSpec((2, t, 1), lambda p, j, i: (p, j, 0)),
                   pl.BlockSpec((2, 1, t), lambda p, j, i: (p, 0, j))],
        out_shape=[jax.ShapeDtypeStruct((3, s, d), F32), jax.ShapeDtypeStruct((2 * n_pairs, s, 1), F32),
                   jax.ShapeDtypeStruct((2 * n_pairs, 1, s), F32)],
        scratch_shapes=[pltpu.VMEM((s, LANES), F32), pltpu.VMEM((2, t, LANES), F32), pltpu.VMEM((2, t, 1), F32),
                        pltpu.VMEM((nt, 2, 1, t), F32)],
        compiler_params=_params(3),
    )(qkn, qkn, proj, f_col, f_row, lse_row, do, o)


def loss_head(y, target):
    s, d = y.shape
    tm = _row_tile(s)

    def body(y_ref, t_ref, dy_ref, loss_ref):
        @pl.when(pl.program_id(0) == 0)
        def _():
            loss_ref[...] = jnp.zeros_like(loss_ref)

        diff = y_ref[...] - t_ref[...]
        dy_ref[...] = diff * (1.0 / d)
        loss_ref[...] += 0.5 * jnp.sum(jnp.mean(diff * diff, axis=-1, keepdims=True), axis=0, keepdims=True)

    row = pl.BlockSpec((tm, d), lambda i: (i, 0))
    return pl.pallas_call(
        body, name="loss_head", grid=(s // tm,),
        in_specs=[row, row],
        out_specs=[row, pl.BlockSpec((8, LANES), lambda i: (0, 0))],
        out_shape=[jax.ShapeDtypeStruct((s, d), F32), jax.ShapeDtypeStruct((8, LANES), F32)],
        compiler_params=_params(1),
    )(y, target)


def ada_mod(c_all, w, b):
    n_layers, d, cols = w.shape

    def body(c_ref, w_ref, b_ref, o_ref):
        cv = c_ref[...]
        o_ref[...] = _nn(cv * _sigmoid(cv), w_ref[...]) + b_ref[...]

    return pl.pallas_call(
        body, name="ada_mod", grid=(n_layers,),
        in_specs=[pl.BlockSpec((N_DEV, d), lambda l: (0, 0)), pl.BlockSpec((None, d, cols), lambda l: (l, 0, 0)),
                  pl.BlockSpec((None, 1, cols), lambda l: (l, 0, 0))],
        out_specs=pl.BlockSpec((None, N_DEV, cols), lambda l: (l, 0, 0)),
        out_shape=jax.ShapeDtypeStruct((n_layers, N_DEV, cols), F32),
        compiler_params=_params(1),
    )(c_all, w, b)


def ada_grad(c_t, dmod):
    d = c_t.shape[0]
    n_layers, _, cols = dmod.shape
    tn = cols // 2

    def body(c_ref, dm_ref, o_ref):
        cv = c_ref[...]
        o_ref[...] = _nn(cv * _sigmoid(cv), dm_ref[...])

    return pl.pallas_call(
        body, name="ada_grad", grid=(n_layers, 2),
        in_specs=[pl.BlockSpec((d, N_DEV), lambda l, n: (0, 0)), pl.BlockSpec((None, N_DEV, tn), lambda l, n: (l, 0, n))],
        out_specs=pl.BlockSpec((None, d, tn), lambda l, n: (l, 0, n)),
        out_shape=jax.ShapeDtypeStruct((n_layers, d, cols), F32),
        compiler_params=_params(2),
    )(c_t, dmod)


def sum_devices(v):
    def body(v_ref, o_ref):
        acc = v_ref[0]
        for k in range(1, N_DEV):
            acc = acc + v_ref[k]
        o_ref[...] = acc

    return pl.pallas_call(body, name="sum_devices", out_shape=jax.ShapeDtypeStruct(v.shape[1:], F32))(v)


def sum_slots(r):
    _, rows, cols = r.shape
    tm = 256 if rows % 256 == 0 else rows

    def body(r_ref, o_ref):
        o_ref[...] = ((r_ref[3].astype(F32) + r_ref[0].astype(F32)) + r_ref[1].astype(F32)) + r_ref[2].astype(F32)

    return pl.pallas_call(
        body, name="sum_slots", grid=(rows // tm,),
        in_specs=[pl.BlockSpec((N_CHIP, tm, cols), lambda i: (0, i, 0))],
        out_specs=pl.BlockSpec((tm, cols), lambda i: (i, 0)),
        out_shape=jax.ShapeDtypeStruct((rows, cols), F32),
        compiler_params=_params(1),
    )(r)


def adamw(w, m, v, g, g2=None):
    rows, cols = w.shape
    tm = 256 if rows % 256 == 0 else rows
    two = g2 is not None
    c1 = 1.0 - ADAM_B1 ** ADAM_STEP
    c2 = 1.0 - ADAM_B2 ** ADAM_STEP

    def body(w_ref, m_ref, v_ref, g_ref, *rest):
        if two:
            g2_ref, go_ref, d_ref, mo_ref, vo_ref = rest
            gv = g_ref[...] + g2_ref[...]
        else:
            go_ref, d_ref, mo_ref, vo_ref = rest
            gv = g_ref[...]
        mn = ADAM_B1 * m_ref[...] + (1.0 - ADAM_B1) * gv
        vn = ADAM_B2 * v_ref[...] + (1.0 - ADAM_B2) * (gv * gv)
        go_ref[...] = gv
        mo_ref[...] = mn
        vo_ref[...] = vn
        d_ref[...] = -ADAM_LR * ((mn / c1) / (jnp.sqrt(vn / c2) + ADAM_EPS) + ADAM_WD * w_ref[...])

    blk = pl.BlockSpec((tm, cols), lambda i: (i, 0))
    ops = [w, m, v, g] + ([g2] if two else [])
    return pl.pallas_call(
        body, name="adamw", grid=(rows // tm,),
        in_specs=[blk] * len(ops), out_specs=[blk] * 4,
        out_shape=[jax.ShapeDtypeStruct((rows, cols), F32)] * 4,
        compiler_params=_params(1),
    )(*ops)


def _pad_rows(flat):
    n = flat.shape[0]
    rows = -(-n // LANES)
    return jnp.pad(flat, (0, rows * LANES - n)).reshape(rows, LANES)


def _pad_rows8(flat):
    rows = _pad_rows(flat)
    return jnp.pad(rows, ((0, -rows.shape[0] % 8), (0, 0)))


def _col_tiles(n):
    return next(k for k in range(1, n // LANES + 1) if n % (k * LANES) == 0 and n // k <= 1536)


def kernel(x, c, ada_w, ada_b, norm_g, ffn_w13, ffn_w2, rel_bias, swa_w_in, swa_w_out, swa_q_g, swa_k_g, swa_sink, fox_w_in, fox_w_out, fox_b_f, fox_q_g, fox_k_g, loss_target, m_ada_w, m_ada_b, m_norm_g, m_ffn_w13, m_ffn_w2, m_rel_bias, m_swa_w_in, m_swa_w_out, m_swa_q_g, m_swa_k_g, m_swa_sink, m_fox_w_in, m_fox_w_out, m_fox_b_f, m_fox_q_g, m_fox_k_g, v_ada_w, v_ada_b, v_norm_g, v_ffn_w13, v_ffn_w2, v_rel_bias, v_swa_w_in, v_swa_w_out, v_swa_q_g, v_swa_k_g, v_swa_sink, v_fox_w_in, v_fox_w_out, v_fox_b_f, v_fox_q_g, v_fox_k_g):
    ix, iy, ic = lax.axis_index("x"), lax.axis_index("y"), lax.axis_index("c")
    chip = 2 * ix + iy
    dev = 2 * chip + ic
    s, d = x.shape[1:]
    n_layers = ada_w.shape[0]
    n_a, n_b = swa_w_in.shape[0], fox_w_in.shape[0]
    hq = d // HEAD_DIM
    hkv = hq // GROUP
    kw = hkv * HEAD_DIM
    c13 = ffn_w13.shape[-1]
    f = 2 * c13
    r2 = ffn_w2.shape[2]
    cq = norm_g.shape[-1]
    a_in = d + 2 * kw
    fx = fox_w_in.shape[-1]
    b_in = N_CHIP * fx
    b_pad = 3 * d + LANES
    x0 = x[0]

    hello = _pad_rows8(jnp.concatenate([c.reshape(-1), norm_g.reshape(-1)]))
    hello_all = all_gather_rows(hello, "gather_c_norm").reshape(N_DEV, -1)
    c_all = hello_all[:, :d]
    ng = hello_all[::2, d:d + n_layers * 3 * cq].reshape(N_CHIP, n_layers, 3, cq)
    norm_full = jnp.moveaxis(ng, 0, 2).reshape(n_layers, 3, d)

    half_cols = ada_w.shape[-1] // 2
    w_half = lax.dynamic_slice_in_dim(ada_w, ic * half_cols, half_cols, axis=2)
    b_half = lax.dynamic_slice_in_dim(ada_b, dev * half_cols, half_cols, axis=1)[:, None, :]
    mod_part = ada_mod(c_all, w_half, b_half)
    mod_all = all_gather_rows(mod_part.reshape(n_layers * N_DEV, half_cols), "gather_mod")
    mod_all = mod_all.reshape(N_DEV, n_layers, N_DEV, half_cols)
    mod_mine = lax.dynamic_index_in_dim(mod_all, dev, axis=2, keepdims=False)
    mod_mine = jnp.moveaxis(mod_mine, 0, 1).reshape(n_layers, 3, 3, d)
    mv = jnp.concatenate([norm_full[:, :, None, :], mod_mine, jnp.zeros((n_layers, 3, 4, d), F32)], axis=2)

    kinds = [("col", c13), ("row", r2), ("col", swa_w_in.shape[-1]), ("row", swa_w_out.shape[1]), ("slot",),
             ("row", fox_w_out.shape[1])]
    w13, w2, wa_in, wa_out, wb_slots, wb_out = gather_weights(
        [ffn_w13.astype(BF16), ffn_w2.astype(BF16), swa_w_in.astype(BF16), swa_w_out.astype(BF16),
         fox_w_in.astype(BF16), fox_w_out.astype(BF16)], kinds)
    wb_in = jnp.concatenate([wb_slots[b] for b in range(N_CHIP)], axis=-1)
    wb_in = jnp.pad(wb_in, ((0, 0), (0, 0), (0, b_pad - b_in)))

    bucket = jnp.asarray(_rel_bucket_table())
    bias = rel_bias_table(rel_bias, bucket)
    tm = _row_tile(s)
    n13 = 2 * f // c13
    na_t, nb_t = _col_tiles(a_in), _col_tiles(3 * d)
    wa_t, wb_t = a_in // na_t, 3 * d // nb_t
    gate_blk = 3 * d // LANES

    def ffn_forward(xv, l, half, sub):
        a, h = modmm(xv, mv, (l, sub), w13, pl.BlockSpec((None, None, d, c13), lambda i, n: (l, half, 0, n)), n13,
                     jax.ShapeDtypeStruct((2, s, f), BF16),
                     pl.BlockSpec((None, tm, c13), lambda i, n: (n // 2, i, n % 2)), f"ffn_up_{l}_{half}", True)
        xo, y, u = resmm(xv, mv, (l, sub), 0.5, a, pl.BlockSpec((2, tm, c13), lambda i, k: (0, i, k)),
                         w2, pl.BlockSpec((None, None, c13, d), lambda i, k: (l, half, k, 0)), f // c13, c13,
                         f"ffn_down_{l}_{half}", True)
        return xo, dict(x=xv, h=h, a=a, u=u, y=y)

    saved = []
    xv = x0
    for l in range(n_layers):
        j = l // 2
        xv, s0 = ffn_forward(xv, l, 0, 0)
        if l % 2 == 0:
            proj, h = modmm(xv, mv, (l, 1), wa_in, pl.BlockSpec((None, d, wa_t), lambda i, n: (j, 0, n)), na_t,
                            jax.ShapeDtypeStruct((s, a_in), BF16), pl.BlockSpec((tm, wa_t), lambda i, n: (i, n)),
                            f"swa_in_{j}", True)
            gains = jnp.concatenate([jnp.tile(swa_q_g[j] * HEAD_DIM ** -0.5, hq), jnp.tile(swa_k_g[j], hkv)])[None, :]
            qkn = qknorm_fwd(proj, gains, kw, f"swa_qknorm_{j}")
            sink = swa_sink[j][None, :]
            o, lse = swa_fwd(qkn, proj, bias, sink, d, f"swa_attn_{j}")
            s1 = dict(x=xv, h=h, proj=proj, gains=gains, qkn=qkn, sink=sink, o=o, lse=lse)
            w_out, w_out_spec = wa_out, pl.BlockSpec((None, d, d), lambda i, k: (j, 0, 0))
        else:
            proj, h = modmm(xv, mv, (l, 1), wb_in, pl.BlockSpec((None, d, wb_t), lambda i, n: (j, 0, n)), nb_t,
                            jax.ShapeDtypeStruct((s, 3 * d), BF16), pl.BlockSpec((tm, wb_t), lambda i, n: (i, n)),
                            f"fox_in_{j}", True)
            fl, _ = modmm(xv, mv, (l, 1), wb_in, pl.BlockSpec((None, d, LANES), lambda i, n: (j, 0, gate_blk)), 1,
                          jax.ShapeDtypeStruct((s, LANES), F32), pl.BlockSpec((tm, LANES), lambda i, n: (i, 0)),
                          f"fox_gate_in_{j}", False)
            b_f = jnp.pad(fox_b_f[j], (0, LANES - hq))[None, :]
            fcum = fox_gates(fl, b_f, f"fox_gates_{j}")
            f_t = fcum[:, :hq].T
            f_col, f_row = f_t[:, :, None], f_t[:, None, :]
            gains = jnp.concatenate([jnp.tile(fox_q_g[j] * HEAD_DIM ** -0.5, hq), jnp.tile(fox_k_g[j], hq)])[None, :]
            qkn = qknorm_fwd(proj, gains, d, f"fox_qknorm_{j}")
            o, o32, lse = fox_fwd(qkn, proj, f_col, f_row, d, f"fox_attn_{j}")
            s1 = dict(x=xv, h=h, proj=proj, gains=gains, qkn=qkn, fl=fl, b_f=b_f, f_col=f_col, f_row=f_row, o=o, o32=o32,
                      lse=lse)
            w_out, w_out_spec = wb_out, pl.BlockSpec((None, d, d), lambda i, k: (j, 0, 0))
        xv, y = resmm(xv, mv, (l, 1), 1.0, o, pl.BlockSpec((tm, d), lambda i, k: (i, 0)), w_out, w_out_spec, 1, d,
                      f"mixer_out_{l}", False)
        s1["y"] = y
        xv, s2 = ffn_forward(xv, l, 1, 2)
        saved.append((s0, s1, s2))

    dxv, loss_part = loss_head(xv, loss_target[0])
    loss = lax.psum(loss_part[0, 0], ("x", "y", "c"))

    g13 = lax.empty((n_layers, 2, d, 2 * f), BF16)
    g2 = lax.empty((n_layers, 2, f, d), BF16)
    ga_in = lax.empty((n_a, d, a_in), BF16)
    ga_out = lax.empty((n_a, d, d), BF16)
    gb_in = lax.empty((n_b, d, b_pad), BF16)
    gb_out = lax.empty((n_b, d, d), BF16)
    dmod = [[None] * 3 for _ in range(n_layers)]
    dnorm = [[None] * 3 for _ in range(n_layers)]
    dqk_gain = {}
    dsink, db_f, dbias_tabs = {}, {}, []

    def ffn_backward(dxo, sv, l, half, sub):
        nonlocal g13, g2
        dy, da, dgate = resmm_bwd(dxo, sv["y"], mv, (l, sub), 0.5, w2,
                                  pl.BlockSpec((None, None, c13, d), lambda i, k: (l, half, k, 0)), f // c13, c13,
                                  f"ffn_down_bwd_{l}_{half}", a=sv["a"])
        g2 = weight_grad(sv["u"], pl.BlockSpec((tm, c13), lambda m, n, k: (k, m)), dy,
                         pl.BlockSpec((tm, d), lambda m, n, k: (k, 0)), (f // c13, 1), s, c13, d, g2,
                         pl.BlockSpec((None, None, c13, d), lambda m, n, k: (l, half, m, 0)), f"ffn_w2_grad_{l}_{half}")
        g13 = weight_grad(sv["h"], pl.BlockSpec((tm, d), lambda m, n, k: (k, 0)), da,
                          pl.BlockSpec((None, tm, c13), lambda m, n, k: (n // 2, k, n % 2)), (1, n13), s, d, c13, g13,
                          pl.BlockSpec((None, None, d, c13), lambda m, n, k: (l, half, 0, n)), f"ffn_w13_grad_{l}_{half}")
        dx, red = modmm_bwd(da, pl.BlockSpec((None, tm, c13), lambda i, n: (n // 2, i, n % 2)), w13,
                            pl.BlockSpec((None, None, d, c13), lambda i, n: (l, half, 0, n)), n13, sv["x"], dxo, mv,
                            (l, sub), f"ffn_up_bwd_{l}_{half}")
        dmod[l][sub] = (red[1], red[2], dgate[0])
        dnorm[l][sub] = red[0]
        return dx

    for l in reversed(range(n_layers)):
        j = l // 2
        s0, s1, s2 = saved[l]
        dxv = ffn_backward(dxv, s2, l, 1, 2)
        is_a = l % 2 == 0
        w_out = wa_out if is_a else wb_out
        dy, do, dgate = resmm_bwd(dxv, s1["y"], mv, (l, 1), 1.0, w_out, pl.BlockSpec((None, d, d), lambda i, k: (j, 0, 0)),
                                  1, d, f"mixer_out_bwd_{l}")
        g_out = weight_grad(s1["o"], pl.BlockSpec((tm, d), lambda m, n, k: (k, 0)), dy,
                            pl.BlockSpec((tm, d), lambda m, n, k: (k, 0)), (1, 1), s, d, d, ga_out if is_a else gb_out,
                            pl.BlockSpec((None, d, d), lambda m, n, k: (j, 0, 0)), f"mixer_out_grad_{l}")
        if is_a:
            ga_out = g_out
            d_qkv, dbias_tab, dsk = swa_bwd(s1["qkn"], s1["proj"], bias, s1["sink"], do, s1["o"], s1["lse"], d,
                                            f"swa_attn_bwd_{j}")
            dbias_tabs.append(dbias_tab)
            dsink[j] = dsk[0, :hq]
            dproj, dgain = qknorm_bwd(s1["proj"], s1["gains"], d_qkv, pl.BlockSpec((tm, kw), lambda c_, i: (i, c_)),
                                      a_in, kw, f"swa_qknorm_bwd_{j}")
            ga_in = weight_grad(s1["h"], pl.BlockSpec((tm, d), lambda m, n, k: (k, 0)), dproj,
                                pl.BlockSpec((tm, wa_t), lambda m, n, k: (k, n)), (1, na_t), s, d, wa_t, ga_in,
                                pl.BlockSpec((None, d, wa_t), lambda m, n, k: (j, 0, n)), f"swa_in_grad_{j}")
            dxv, red = modmm_bwd(dproj, pl.BlockSpec((tm, wa_t), lambda i, n: (i, n)), wa_in,
                                 pl.BlockSpec((None, d, wa_t), lambda i, n: (j, 0, n)), na_t, s1["x"], dxv, mv, (l, 1),
                                 f"swa_in_bwd_{j}")
            dqk_gain[("a", j)] = (dgain[0, :d].reshape(hq, HEAD_DIM).sum(0) * HEAD_DIM ** -0.5,
                                  dgain[0, d:d + kw].reshape(hkv, HEAD_DIM).sum(0))
        else:
            gb_out = g_out
            lse_row = s1["lse"].reshape(hq, 1, s)
            d_qkv, df_col, dfq_row = fox_bwd(s1["qkn"], s1["proj"], s1["f_col"], s1["f_row"], lse_row, do, s1["o32"], d,
                                             f"fox_attn_bwd_{j}")
            lanes_of_heads = lambda a: jnp.pad(a.T, ((0, 0), (0, LANES - hq)))
            dfl, dbf = fox_gates_bwd(s1["fl"], s1["b_f"], lanes_of_heads(dfq_row[:, 0, :]), lanes_of_heads(df_col[:, :, 0]),
                                     f"fox_gates_bwd_{j}")
            db_f[j] = dbf[0, :hq]
            dproj, dgain = qknorm_bwd(s1["proj"], s1["gains"], d_qkv, pl.BlockSpec((None, tm, d), lambda c_, i: (c_, i, 0)),
                                      3 * d, d, f"fox_qknorm_bwd_{j}")
            gb_in = weight_grad(s1["h"], pl.BlockSpec((tm, d), lambda m, n, k: (k, 0)), dproj,
                                pl.BlockSpec((tm, wb_t), lambda m, n, k: (k, n)), (1, nb_t), s, d, wb_t, gb_in,
                                pl.BlockSpec((None, d, wb_t), lambda m, n, k: (j, 0, n)), f"fox_in_grad_{j}")
            gb_in = weight_grad(s1["h"], pl.BlockSpec((tm, d), lambda m, n, k: (k, 0)), dfl,
                                pl.BlockSpec((tm, LANES), lambda m, n, k: (k, 0)), (1, 1), s, d, LANES, gb_in,
                                pl.BlockSpec((None, d, LANES), lambda m, n, k: (j, 0, gate_blk)), f"fox_gate_in_grad_{j}")
            dxv, red = modmm_bwd(dproj, pl.BlockSpec((tm, wb_t), lambda i, n: (i, n)), wb_in,
                                 pl.BlockSpec((None, d, wb_t), lambda i, n: (j, 0, n)), nb_t, s1["x"], dxv, mv, (l, 1),
                                 f"fox_in_bwd_{j}",
                                 more=(dfl, pl.BlockSpec((tm, LANES), lambda i, n: (i, 0)),
                                       pl.BlockSpec((None, d, LANES), lambda i, n: (j, 0, gate_blk))))
            dqk_gain[("b", j)] = (dgain[0, :d].reshape(hq, HEAD_DIM).sum(0) * HEAD_DIM ** -0.5,
                                  dgain[0, d:2 * d].reshape(hq, HEAD_DIM).sum(0))
        dmod[l][1] = (red[1], red[2], dgate[0])
        dnorm[l][1] = red[0]
        dxv = ffn_backward(dxv, s0, l, 0, 0)
    grad_x = dxv[None]

    drel = rel_bias_grad(jnp.stack(dbias_tabs), bucket)
    dmod_flat = jnp.stack([jnp.stack([jnp.stack(dmod[l][sub]) for sub in range(3)]) for l in range(n_layers)]).reshape(-1)
    dnorm_flat = jnp.stack([jnp.stack(dnorm[l]) for l in range(n_layers)]).reshape(-1)
    pieces = [dmod_flat, dnorm_flat,
              jnp.stack([dqk_gain[("a", j)][0] for j in range(n_a)]).reshape(-1),
              jnp.stack([dqk_gain[("a", j)][1] for j in range(n_a)]).reshape(-1),
              jnp.stack([dqk_gain[("b", j)][0] for j in range(n_b)]).reshape(-1),
              jnp.stack([dqk_gain[("b", j)][1] for j in range(n_b)]).reshape(-1),
              jnp.stack([dsink[j] for j in range(n_a)]).reshape(-1),
              jnp.stack([db_f[j] for j in range(n_b)]).reshape(-1),
              drel.reshape(-1)]
    rows = [_pad_rows(p) for p in pieces]
    starts = np.cumsum([0] + [r.shape[0] for r in rows])
    total = -(-int(starts[-1]) // 8) * 8
    small = jnp.pad(jnp.concatenate(rows), ((0, total - int(starts[-1])), (0, 0)))
    small_all = all_gather_rows(small, "gather_small_grads").reshape(N_DEV, total, LANES)
    small_sum = sum_devices(small_all)

    def piece(k, shape):
        n = int(np.prod(shape))
        return small_sum[int(starts[k]):int(starts[k + 1])].reshape(-1)[:n].reshape(shape)

    g_ada_b = piece(0, (n_layers, 9 * d))
    g_norm = lax.dynamic_slice_in_dim(piece(1, (n_layers, 3, d)), chip * cq, cq, axis=2)
    g_swa_q, g_swa_k = piece(2, (n_a, HEAD_DIM)), piece(3, (n_a, HEAD_DIM))
    g_fox_q, g_fox_k = piece(4, (n_b, HEAD_DIM)), piece(5, (n_b, HEAD_DIM))
    g_sink, g_bf, g_rel = piece(6, (n_a, hq)), piece(7, (n_b, hq)), piece(8, (REL_BUCKETS, hq))

    dmod_all = small_all[:, :int(starts[1])].reshape(N_DEV, -1)[:, :n_layers * 9 * d].reshape(N_DEV, n_layers, 9 * d)
    ada_cols = ada_w.shape[-1]
    dmod_mine = lax.dynamic_slice_in_dim(jnp.moveaxis(dmod_all, 0, 1), chip * ada_cols, ada_cols, axis=2)
    g_ada_w = ada_grad(c_all.T, dmod_mine)

    gb_in_slots = jnp.stack([gb_in[:, :, b * fx:(b + 1) * fx] for b in range(N_CHIP)])
    grads = [g13, g2, ga_in, ga_out, gb_in_slots, gb_out]
    shard_shapes = [ffn_w13.shape, ffn_w2.shape, swa_w_in.shape, swa_w_out.shape, fox_w_in.shape, fox_w_out.shape]
    landed = scatter_grads(grads, kinds, shard_shapes)
    parts = [sum_slots(r.reshape(N_CHIP, -1, r.shape[-1])) for r in landed]
    others = swap_with_sibling(parts)

    def update(w, m, v, g, g2=None):
        w2d = w.reshape(-1, w.shape[-1])
        outs = adamw(w2d, m.reshape(w2d.shape), v.reshape(w2d.shape), g.reshape(w2d.shape) if g2 is None else g, g2)
        return [t.reshape(w.shape) for t in outs]

    big = [(ffn_w13, m_ffn_w13, v_ffn_w13), (ffn_w2, m_ffn_w2, v_ffn_w2), (swa_w_in, m_swa_w_in, v_swa_w_in),
           (swa_w_out, m_swa_w_out, v_swa_w_out), (fox_w_in, m_fox_w_in, v_fox_w_in), (fox_w_out, m_fox_w_out, v_fox_w_out)]
    big_out = [update(w, m, v, p, q) for (w, m, v), p, q in zip(big, parts, others)]
    r_ada_w = update(ada_w, m_ada_w, v_ada_w, g_ada_w)
    r_ada_b = update(ada_b, m_ada_b, v_ada_b, g_ada_b)
    r_norm = update(norm_g, m_norm_g, v_norm_g, g_norm)
    r_rel = update(rel_bias, m_rel_bias, v_rel_bias, g_rel)
    r_swa_q = update(swa_q_g, m_swa_q_g, v_swa_q_g, g_swa_q)
    r_swa_k = update(swa_k_g, m_swa_k_g, v_swa_k_g, g_swa_k)
    r_sink = update(swa_sink, m_swa_sink, v_swa_sink, g_sink)
    r_bf = update(fox_b_f, m_fox_b_f, v_fox_b_f, g_bf)
    r_fox_q = update(fox_q_g, m_fox_q_g, v_fox_q_g, g_fox_q)
    r_fox_k = update(fox_k_g, m_fox_k_g, v_fox_k_g, g_fox_k)
    per_weight = [r_ada_w, r_ada_b, r_norm, big_out[0], big_out[1], r_rel, big_out[2], big_out[3], r_swa_q, r_swa_k,
                  r_sink, big_out[4], big_out[5], r_bf, r_fox_q, r_fox_k]
    return (loss, grad_x, *[r[0] for r in per_weight], *[r[1] for r in per_weight],
            *[r[2] for r in per_weight], *[r[3] for r in per_weight])
```

```python
import math

import numpy as np
import jax
import jax.numpy as jnp
from jax import lax
from jax.experimental import pallas as pl
from jax.experimental.pallas import tpu as pltpu

F32 = jnp.float32
BF16 = jnp.bfloat16
HEAD_DIM = 64
GROUP = 4
BLOCK = 128
REL_BUCKETS = 32
REL_MAX_DIST = 128
EPS = 1e-6
NEG = -1e30
N_CHIP = 4
N_DEV = 8
LANES = 128
VMEM_LIMIT = 52 * 1024 * 1024
ADAM_LR, ADAM_B1, ADAM_B2, ADAM_EPS, ADAM_WD, ADAM_STEP = 0.001, 0.9, 0.999, 1e-08, 0.01, 10
MESH = pl.DeviceIdType.MESH
ANY = pl.BlockSpec(memory_space=pl.ANY)


def _params(n_axes):
    return pltpu.CompilerParams(dimension_semantics=("arbitrary",) * n_axes, vmem_limit_bytes=VMEM_LIMIT)


def _nn(a, b):
    return jnp.dot(a, b, preferred_element_type=F32)


def _nt(a, b):
    return lax.dot_general(a, b, (((1,), (1,)), ((), ())), preferred_element_type=F32)


def _tn(a, b):
    return lax.dot_general(a, b, (((0,), (0,)), ((), ())), preferred_element_type=F32)


def _sigmoid(z):
    return 1.0 / (1.0 + jnp.exp(-z))


def _sigmoid_fast(z):
    return pl.reciprocal(1.0 + jnp.exp(-z), approx=True)


def _row_tile(s):
    return 512 if s >= 2048 else s // 2


def _wide_tile(s):
    return 1024 if s >= 2048 else s // 2


def _attn_tile(s):
    return 512 if s >= 2048 else s // 4


def _position():
    x, y, c = lax.axis_index("x"), lax.axis_index("y"), lax.axis_index("c")
    chips = [(1 - x, y), (x, 1 - y), (1 - x, 1 - y)]
    return x, y, c, chips


def all_gather_rows(v, name):
    m_per, n = v.shape

    def body(x_ref, out_ref, send_sems, recv_sems, local_sem):
        x, y, c, chips = _position()
        me, sibling = (x, y, c), (x, y, 1 - c)

        def rows(px, py, pc):
            return out_ref.at[pl.ds((4 * px + 2 * py + pc) * m_per, m_per), :]

        def copy(k, block, to, src=None):
            return pltpu.make_async_remote_copy(
                src_ref=rows(*block) if src is None else src, dst_ref=rows(*block),
                send_sem=send_sems.at[k], recv_sem=recv_sems.at[k], device_id=to, device_id_type=MESH)

        mine = pltpu.make_async_copy(x_ref, rows(*me), local_sem)
        mine.start()
        first = [copy(0, me, sibling, src=x_ref)]
        first += [copy(1 + j, me, (*chip, c), src=x_ref) for j, chip in enumerate(chips)]
        for cp in first:
            cp.start()
        passed = [copy(4 + j, (*chip, c), sibling) for j, chip in enumerate(chips)]
        for j, chip in enumerate(chips):
            copy(1 + j, (*chip, c), me).wait_recv()
            passed[j].start()
        copy(0, sibling, me).wait_recv()
        for j, chip in enumerate(chips):
            copy(4 + j, (*chip, 1 - c), me).wait_recv()
        for cp in first + passed:
            cp.wait_send()
        mine.wait()

    return pl.pallas_call(
        body, name=name,
        out_shape=jax.ShapeDtypeStruct((N_DEV * m_per, n), v.dtype),
        in_specs=[pl.BlockSpec(memory_space=pltpu.VMEM)],
        out_specs=pl.BlockSpec(memory_space=pltpu.VMEM),
        scratch_shapes=[pltpu.SemaphoreType.DMA((7,)), pltpu.SemaphoreType.DMA((7,)), pltpu.SemaphoreType.DMA],
    )(v)


def _slab(full_ref, kind, b, lead):
    how = kind[0]
    if how == "slot":
        return full_ref.at[b, lead]
    if how == "col":
        w = kind[1]
        idx = (lead,) + (slice(None),) * (len(full_ref.shape) - 2) + (pl.ds(pl.multiple_of(b * w, LANES), w),)
        return full_ref.at[idx]
    h = kind[1]
    idx = (lead,) + (slice(None),) * (len(full_ref.shape) - 3) + (pl.ds(pl.multiple_of(b * h, 8), h), slice(None))
    return full_ref.at[idx]


def _full_shape(shard_shape, kind):
    if kind[0] == "slot":
        return (N_CHIP,) + tuple(shard_shape)
    if kind[0] == "col":
        return tuple(shard_shape[:-1]) + (N_CHIP * shard_shape[-1],)
    return tuple(shard_shape[:-2]) + (N_CHIP * shard_shape[-2], shard_shape[-1])


def gather_weights(shards, kinds):
    n = len(shards)

    def body(*refs):
        ins, outs = refs[:n], refs[n:2 * n]
        send_sems, recv_sems, local_sems = refs[2 * n:]
        x, y, c, chips = _position()
        b_me = 2 * x + y
        sibling = (x, y, 1 - c)
        local, sends = [], []
        for t in range(n):
            half = ins[t].shape[0] // 2
            mine = pl.ds(c * half, half)
            whole = pl.ds(0, 2 * half)
            cp = pltpu.make_async_copy(ins[t], _slab(outs[t], kinds[t], b_me, whole), local_sems.at[t])
            cp.start()
            local.append(cp)
            for j, chip in enumerate(chips):
                cp = pltpu.make_async_remote_copy(
                    src_ref=ins[t].at[mine], dst_ref=_slab(outs[t], kinds[t], b_me, mine),
                    send_sem=send_sems.at[6 * t + j], recv_sem=recv_sems.at[6 * t + j],
                    device_id=(*chip, c), device_id_type=MESH)
                cp.start()
                sends.append(cp)
        for t in range(n):
            half = ins[t].shape[0] // 2
            mine = pl.ds(c * half, half)
            for j, chip in enumerate(chips):
                landed = _slab(outs[t], kinds[t], 2 * chip[0] + chip[1], mine)
                pltpu.make_async_remote_copy(
                    src_ref=landed, dst_ref=landed, send_sem=send_sems.at[6 * t + j], recv_sem=recv_sems.at[6 * t + j],
                    device_id=(*chip, c), device_id_type=MESH).wait_recv()
                cp = pltpu.make_async_remote_copy(
                    src_ref=landed, dst_ref=landed, send_sem=send_sems.at[6 * t + 3 + j],
                    recv_sem=recv_sems.at[6 * t + 3 + j], device_id=sibling, device_id_type=MESH)
                cp.start()
                sends.append(cp)
        for t in range(n):
            half = ins[t].shape[0] // 2
            theirs = pl.ds((1 - c) * half, half)
            for j, chip in enumerate(chips):
                landed = _slab(outs[t], kinds[t], 2 * chip[0] + chip[1], theirs)
                pltpu.make_async_remote_copy(
                    src_ref=landed, dst_ref=landed, send_sem=send_sems.at[6 * t + 3 + j],
                    recv_sem=recv_sems.at[6 * t + 3 + j], device_id=sibling, device_id_type=MESH).wait_recv()
        for cp in sends:
            cp.wait_send()
        for cp in local:
            cp.wait()

    return pl.pallas_call(
        body, name="gather_weights",
        out_shape=[jax.ShapeDtypeStruct(_full_shape(s.shape, k), s.dtype) for s, k in zip(shards, kinds)],
        in_specs=[ANY] * n, out_specs=[ANY] * n,
        scratch_shapes=[pltpu.SemaphoreType.DMA((6 * n,)), pltpu.SemaphoreType.DMA((6 * n,)),
                        pltpu.SemaphoreType.DMA((n,))],
    )(*shards)


def scatter_grads(grads, kinds, shard_shapes):
    n = len(grads)

    def body(*refs):
        ins, outs = refs[:n], refs[n:2 * n]
        send_sems, recv_sems, local_sems = refs[2 * n:]
        x, y, c, chips = _position()
        b_me = 2 * x + y
        local, sends = [], []
        for t in range(n):
            whole = pl.ds(0, ins[t].shape[1] if kinds[t][0] == "slot" else ins[t].shape[0])
            cp = pltpu.make_async_copy(_slab(ins[t], kinds[t], b_me, whole), outs[t].at[3], local_sems.at[t])
            cp.start()
            local.append(cp)
            for j, chip in enumerate(chips):
                cp = pltpu.make_async_remote_copy(
                    src_ref=_slab(ins[t], kinds[t], 2 * chip[0] + chip[1], whole), dst_ref=outs[t].at[j],
                    send_sem=send_sems.at[3 * t + j], recv_sem=recv_sems.at[3 * t + j],
                    device_id=(*chip, c), device_id_type=MESH)
                cp.start()
                sends.append(cp)
        for t in range(n):
            for j, chip in enumerate(chips):
                pltpu.make_async_remote_copy(
                    src_ref=outs[t].at[j], dst_ref=outs[t].at[j], send_sem=send_sems.at[3 * t + j],
                    recv_sem=recv_sems.at[3 * t + j], device_id=(*chip, c), device_id_type=MESH).wait_recv()
        for cp in sends:
            cp.wait_send()
        for cp in local:
            cp.wait()

    return pl.pallas_call(
        body, name="scatter_grads",
        out_shape=[jax.ShapeDtypeStruct((N_CHIP,) + tuple(s), g.dtype) for g, s in zip(grads, shard_shapes)],
        in_specs=[ANY] * n, out_specs=[ANY] * n,
        scratch_shapes=[pltpu.SemaphoreType.DMA((3 * n,)), pltpu.SemaphoreType.DMA((3 * n,)),
                        pltpu.SemaphoreType.DMA((n,))],
    )(*grads)


def swap_with_sibling(parts):
    n = len(parts)

    def body(*refs):
        ins, outs = refs[:n], refs[n:2 * n]
        send_sems, recv_sems = refs[2 * n:]
        x, y, c, _ = _position()
        cps = []
        for t in range(n):
            cp = pltpu.make_async_remote_copy(
                src_ref=ins[t], dst_ref=outs[t], send_sem=send_sems.at[t], recv_sem=recv_sems.at[t],
                device_id=(x, y, 1 - c), device_id_type=MESH)
            cp.start()
            cps.append(cp)
        for cp in cps:
            cp.wait_recv()
        for cp in cps:
            cp.wait_send()

    return pl.pallas_call(
        body, name="swap_with_sibling",
        out_shape=[jax.ShapeDtypeStruct(p.shape, p.dtype) for p in parts],
        in_specs=[ANY] * n, out_specs=[ANY] * n,
        scratch_shapes=[pltpu.SemaphoreType.DMA((n,)), pltpu.SemaphoreType.DMA((n,))],
    )(*parts)


def _modulated(xv, mv_ref):
    g, shift, scale = mv_ref[0:1, :], mv_ref[1:2, :], mv_ref[2:3, :]
    r = lax.rsqrt(jnp.mean(xv * xv, axis=-1, keepdims=True) + EPS)
    xhat = xv * r
    xn = xhat * g
    return xn * (1.0 + scale) + shift, xhat, xn, r, g, scale


def modmm(x, mv, mv_idx, w, w_spec, n_tiles, out_shape, out_spec, name, want_h):
    s, d = x.shape
    tm = _wide_tile(s)

    def body(x_ref, mv_ref, w_ref, *rest):
        if want_h:
            out_ref, h_ref, h_scr = rest
        else:
            out_ref, h_scr = rest

        @pl.when(pl.program_id(1) == 0)
        def _():
            h = _modulated(x_ref[...], mv_ref)[0].astype(BF16)
            h_scr[...] = h
            if want_h:
                h_ref[...] = h

        out_ref[...] = _nn(h_scr[...], w_ref[...]).astype(out_ref.dtype)

    out_shapes = [out_shape]
    out_specs = [out_spec]
    if want_h:
        out_shapes.append(jax.ShapeDtypeStruct((s, d), BF16))
        out_specs.append(pl.BlockSpec((tm, d), lambda i, n: (i, 0)))
    res = pl.pallas_call(
        body, name=name, grid=(s // tm, n_tiles),
        in_specs=[pl.BlockSpec((tm, d), lambda i, n: (i, 0)),
                  pl.BlockSpec((None, None, 8, d), lambda i, n: (*mv_idx, 0, 0)), w_spec],
        out_specs=out_specs, out_shape=out_shapes,
        scratch_shapes=[pltpu.VMEM((tm, d), BF16)],
        compiler_params=_params(2),
    )(x, mv, w)
    return res if want_h else (res[0], None)


def resmm(x, mv, mv_idx, coef, lhs, lhs_spec, w, w_spec, k_tiles, tk, name, ffn):
    s, d = x.shape
    tm = _row_tile(s)
    kdim = k_tiles * tk

    def body(x_ref, mv_ref, lhs_ref, w_ref, xo_ref, y_ref, *rest):
        if ffn:
            u_ref, acc = rest
        else:
            (acc,) = rest
        k = pl.program_id(1)

        @pl.when(k == 0)
        def _():
            acc[...] = jnp.zeros_like(acc)

        if ffn:
            ag = lhs_ref[0].astype(F32)
            au = lhs_ref[1].astype(F32)
            left = (ag * _sigmoid_fast(ag) * au).astype(BF16)
            u_ref[...] = left
        else:
            left = lhs_ref[...]
        acc[...] += _nn(left, w_ref[...])

        @pl.when(k == k_tiles - 1)
        def _():
            y = acc[...]
            y_ref[...] = y.astype(BF16)
            xo_ref[...] = x_ref[...] + (coef * mv_ref[3:4, :]) * y

    row = pl.BlockSpec((tm, d), lambda i, k: (i, 0))
    out_shapes = [jax.ShapeDtypeStruct((s, d), F32), jax.ShapeDtypeStruct((s, d), BF16)]
    out_specs = [row, row]
    if ffn:
        out_shapes.append(jax.ShapeDtypeStruct((s, kdim), BF16))
        out_specs.append(pl.BlockSpec((tm, tk), lambda i, k: (i, k)))
    return pl.pallas_call(
        body, name=name, grid=(s // tm, k_tiles),
        in_specs=[row, pl.BlockSpec((None, None, 8, d), lambda i, k: (*mv_idx, 0, 0)), lhs_spec, w_spec],
        out_specs=out_specs, out_shape=out_shapes,
        scratch_shapes=[pltpu.VMEM((tm, d), F32)],
        compiler_params=_params(2),
    )(x, mv, lhs, w)


def resmm_bwd(dxo, y, mv, mv_idx, coef, w, w_spec, k_tiles, tk, name, a=None):
    s, d = dxo.shape
    tm = _row_tile(s)
    kdim = k_tiles * tk
    ffn = a is not None

    def body(dxo_ref, y_ref, mv_ref, w_ref, *rest):
        if ffn:
            a_ref, dy_ref, dl_ref, dgate_ref, dy_scr = rest
        else:
            dy_ref, dl_ref, dgate_ref, dy_scr = rest
        i, k = pl.program_id(0), pl.program_id(1)

        @pl.when((i == 0) & (k == 0))
        def _():
            dgate_ref[...] = jnp.zeros_like(dgate_ref)

        @pl.when(k == 0)
        def _():
            dxv = dxo_ref[...]
            dy = ((coef * mv_ref[3:4, :]) * dxv).astype(BF16)
            dy_scr[...] = dy
            dy_ref[...] = dy
            dgate_ref[0:1, :] += jnp.sum(coef * dxv * y_ref[...].astype(F32), axis=0, keepdims=True)

        dl = _nt(dy_scr[...], w_ref[...])
        if ffn:
            ag = a_ref[0].astype(F32)
            au = a_ref[1].astype(F32)
            sg = _sigmoid_fast(ag)
            dl_ref[0] =(dl * au * (sg * (1.0 + ag * (1.0 - sg)))).astype(BF16)
            dl_ref[1] = (dl * (ag * sg)).astype(BF16)
        else:
            dl_ref[...] = dl.astype(BF16)

    row = pl.BlockSpec((tm, d), lambda i, k: (i, 0))
    in_specs = [row, row, pl.BlockSpec((None, None, 8, d), lambda i, k: (*mv_idx, 0, 0)), w_spec]
    ops = [dxo, y, mv, w]
    if ffn:
        in_specs.append(pl.BlockSpec((2, tm, tk), lambda i, k: (0, i, k)))
        ops.append(a)
        dl_shape = jax.ShapeDtypeStruct((2, s, kdim), BF16)
        dl_spec = pl.BlockSpec((2, tm, tk), lambda i, k: (0, i, k))
    else:
        dl_shape = jax.ShapeDtypeStruct((s, kdim), BF16)
        dl_spec = pl.BlockSpec((tm, tk), lambda i, k: (i, k))
    return pl.pallas_call(
        body, name=name, grid=(s // tm, k_tiles),
        in_specs=in_specs,
        out_specs=[row, dl_spec, pl.BlockSpec((8, d), lambda i, k: (0, 0))],
        out_shape=[jax.ShapeDtypeStruct((s, d), BF16), dl_shape, jax.ShapeDtypeStruct((8, d), F32)],
        scratch_shapes=[pltpu.VMEM((tm, d), BF16)],
        compiler_params=_params(2),
    )(*ops)


def modmm_bwd(dl, dl_spec, w, w_spec, n_tiles, x, dxo, mv, mv_idx, name, more=None):
    s, d = x.shape
    tm = _row_tile(s)

    def body(dl_ref, w_ref, x_ref, dxo_ref, mv_ref, *rest):
        if more is not None:
            dl2_ref, w2_ref, dx_ref, red_ref, acc = rest
        else:
            dx_ref, red_ref, acc = rest
        i, n = pl.program_id(0), pl.program_id(1)

        @pl.when((i == 0) & (n == 0))
        def _():
            red_ref[...] = jnp.zeros_like(red_ref)

        @pl.when(n == 0)
        def _():
            if more is not None:
                acc[...] = _nt(dl2_ref[...], w2_ref[...])
            else:
                acc[...] = jnp.zeros_like(acc)

        acc[...] += _nt(dl_ref[...], w_ref[...])

        @pl.when(n == n_tiles - 1)
        def _():
            dh = acc[...]
            _, xhat, xn, r, g, scale = _modulated(x_ref[...], mv_ref)
            dxn = dh * (1.0 + scale)
            red_ref[0:1, :] += jnp.sum(dxn * xhat, axis=0, keepdims=True)
            red_ref[1:2, :] += jnp.sum(dh, axis=0, keepdims=True)
            red_ref[2:3, :] += jnp.sum(dh * xn, axis=0, keepdims=True)
            gd = dxn * g
            dx_ref[...] = dxo_ref[...] + r * (gd - xhat * jnp.mean(gd * xhat, axis=-1, keepdims=True))

    row = pl.BlockSpec((tm, d), lambda i, n: (i, 0))
    in_specs = [dl_spec, w_spec, row, row, pl.BlockSpec((None, None, 8, d), lambda i, n: (*mv_idx, 0, 0))]
    ops = [dl, w, x, dxo, mv]
    if more is not None:
        in_specs += [more[1], more[2]]
        ops += [more[0], w]
    return pl.pallas_call(
        body, name=name, grid=(s // tm, n_tiles),
        in_specs=in_specs,
        out_specs=[row, pl.BlockSpec((8, d), lambda i, n: (0, 0))],
        out_shape=[jax.ShapeDtypeStruct((s, d), F32), jax.ShapeDtypeStruct((8, d), F32)],
        scratch_shapes=[pltpu.VMEM((tm, d), F32)],
        compiler_params=_params(2),
    )(*ops)


def weight_grad(a, a_spec, b, b_spec, grid_mn, s, bm, bn, dest, out_spec, name):
    tk = _wide_tile(s)
    k_tiles = s // tk

    def body(a_ref, b_ref, dest_ref, out_ref, acc):
        k = pl.program_id(2)

        @pl.when(k == 0)
        def _():
            acc[...] = jnp.zeros_like(acc)

        acc[...] += _tn(a_ref[...], b_ref[...])

        @pl.when(k == k_tiles - 1)
        def _():
            out_ref[...] = acc[...].astype(out_ref.dtype)

    return pl.pallas_call(
        body, name=name, grid=(*grid_mn, k_tiles),
        in_specs=[a_spec, b_spec, ANY], out_specs=out_spec,
        out_shape=jax.ShapeDtypeStruct(dest.shape, dest.dtype),
        input_output_aliases={2: 0},
        scratch_shapes=[pltpu.VMEM((bm, bn), F32)],
        compiler_params=_params(3),
    )(a, b, dest)


def _head_mean(v):
    lane = lax.broadcasted_iota(jnp.int32, v.shape, 1)
    lo = jnp.sum(jnp.where(lane < HEAD_DIM, v, 0.0), axis=-1, keepdims=True)
    hi = jnp.sum(v, axis=-1, keepdims=True) - lo
    return jnp.where(lane < HEAD_DIM, lo, hi) * (1.0 / HEAD_DIM)


def qknorm_fwd(proj, gains, width, name):
    s = proj.shape[0]
    nqk = gains.shape[1]
    tm = _row_tile(s)

    def body(p_ref, g_ref, o_ref):
        for cc in range(width // LANES):
            sl = slice(cc * LANES, (cc + 1) * LANES)
            xv = p_ref[:, sl].astype(F32)
            r = lax.rsqrt(_head_mean(xv * xv) + EPS)
            o_ref[:, sl] = (xv * r * g_ref[:, sl]).astype(BF16)

    blk = pl.BlockSpec((tm, width), lambda i, c: (i, c))
    return pl.pallas_call(
        body, name=name, grid=(s // tm, nqk // width),
        in_specs=[blk, pl.BlockSpec((1, width), lambda i, c: (0, c))],
        out_specs=blk, out_shape=jax.ShapeDtypeStruct((s, nqk), BF16),
        compiler_params=_params(2),
    )(proj, gains)


def qknorm_bwd(proj, gains, d, d_spec, n_cols, width, name):
    s = proj.shape[0]
    nqk = gains.shape[1] // width
    n_blocks = n_cols // width
    tm = _row_tile(s)

    def body(p_ref, g_ref, d_ref, o_ref, dg_ref):
        c, i = pl.program_id(0), pl.program_id(1)

        @pl.when(i == 0)
        def _():
            dg_ref[...] = jnp.zeros_like(dg_ref)

        @pl.when(c < nqk)
        def _():
            for cc in range(width // LANES):
                sl = slice(cc * LANES, (cc + 1) * LANES)
                xv = p_ref[:, sl].astype(F32)
                r = lax.rsqrt(_head_mean(xv * xv) + EPS)
                xhat = xv * r
                dv = d_ref[:, sl]
                gd = dv * g_ref[:, sl]
                o_ref[:, sl] = (r * (gd - xhat * _head_mean(gd * xhat))).astype(BF16)
                dg_ref[0:1, sl] += jnp.sum(dv * xhat, axis=0, keepdims=True)

        @pl.when(c >= nqk)
        def _():
            o_ref[...] = d_ref[...].astype(BF16)

    return pl.pallas_call(
        body, name=name, grid=(n_blocks, s // tm),
        in_specs=[pl.BlockSpec((tm, width), lambda c, i: (i, c)),
                  pl.BlockSpec((1, width), lambda c, i: (0, jnp.minimum(c, nqk - 1))), d_spec],
        out_specs=[pl.BlockSpec((tm, width), lambda c, i: (i, c)), pl.BlockSpec((8, width), lambda c, i: (0, c))],
        out_shape=[jax.ShapeDtypeStruct((s, n_cols), BF16), jax.ShapeDtypeStruct((8, n_cols), F32)],
        compiler_params=_params(2),
    )(proj, gains, d)


def _swa_mask(first):
    qi = lax.broadcasted_iota(jnp.int32, (BLOCK, 2 * BLOCK), 0) + BLOCK
    kj = lax.broadcasted_iota(jnp.int32, (BLOCK, 2 * BLOCK), 1)
    dist = qi - kj
    return (dist >= 0) & (dist < BLOCK) & ((kj >= BLOCK) | jnp.logical_not(first))


def swa_fwd(qkn, proj, bias, sink, d, name):
    s = qkn.shape[0]
    hq = d // HEAD_DIM
    hkv = hq // GROUP
    kw = hkv * HEAD_DIM
    nblk = s // BLOCK
    kcol = d // kw

    def body(q_ref, kc_ref, kp_ref, vc_ref, vp_ref, bias_ref, sink_ref, o_ref, lse_ref):
        mask = _swa_mask(pl.program_id(0) == 0)
        lse_ref[...] = jnp.zeros_like(lse_ref)
        for kvh in range(hkv):
            cols = slice(kvh * HEAD_DIM, (kvh + 1) * HEAD_DIM)
            k2 = jnp.concatenate([kp_ref[:, cols], kc_ref[:, cols]], axis=0)
            v2 = jnp.concatenate([vp_ref[:, cols], vc_ref[:, cols]], axis=0)
            for g in range(GROUP):
                h = kvh * GROUP + g
                hc = slice(h * HEAD_DIM, (h + 1) * HEAD_DIM)
                sc = jnp.where(mask, _nt(q_ref[:, hc], k2) + bias_ref[h], NEG)
                sk = sink_ref[0, h]
                m = jnp.maximum(jnp.max(sc, axis=-1, keepdims=True), sk)
                p = jnp.exp(sc - m)
                denom = jnp.sum(p, axis=-1, keepdims=True) + jnp.exp(sk - m)
                o_ref[:, hc] = (_nn(p.astype(BF16), v2) / denom).astype(BF16)
                lse_ref[:, h:h + 1] = m + jnp.log(denom)

    prev = lambda i: jnp.maximum(i - 1, 0)
    return pl.pallas_call(
        body, name=name, grid=(nblk,),
        in_specs=[pl.BlockSpec((BLOCK, d), lambda i: (i, 0)),
                  pl.BlockSpec((BLOCK, kw), lambda i: (i, kcol)),
                  pl.BlockSpec((BLOCK, kw), lambda i: (prev(i), kcol)),
                  pl.BlockSpec((BLOCK, kw), lambda i: (i, kcol + 1)),
                  pl.BlockSpec((BLOCK, kw), lambda i: (prev(i), kcol + 1)),
                  pl.BlockSpec((hq, BLOCK, 2 * BLOCK), lambda i: (0, 0, 0)),
                  pl.BlockSpec(memory_space=pltpu.SMEM)],
        out_specs=[pl.BlockSpec((BLOCK, d), lambda i: (i, 0)), pl.BlockSpec((BLOCK, LANES), lambda i: (i, 0))],
        out_shape=[jax.ShapeDtypeStruct((s, d), BF16), jax.ShapeDtypeStruct((s, LANES), F32)],
        compiler_params=_params(1),
    )(qkn, qkn, qkn, proj, proj, bias, sink)


def swa_bwd(qkn, proj, bias, sink, do, o, lse, d, name):
    s = qkn.shape[0]
    hq = d // HEAD_DIM
    hkv = hq // GROUP
    kw = hkv * HEAD_DIM
    nblk = s // BLOCK
    kcol = d // kw
    wide = d + 2 * kw

    def body(q_ref, kc_ref, kp_ref, vc_ref, vp_ref, bias_ref, sink_ref, do_ref, o_ref, lse_ref,
             out_ref, dbias_ref, dsink_ref, carry, fresh):
        i = pl.program_id(0)

        @pl.when(i == 0)
        def _():
            dbias_ref[...] = jnp.zeros_like(dbias_ref)
            dsink_ref[...] = jnp.zeros_like(dsink_ref)
            carry[...] = jnp.zeros_like(carry)

        @pl.when(i == nblk)
        def _():
            fresh[...] = jnp.zeros_like(fresh)

        @pl.when(i < nblk)
        def _():
            mask = _swa_mask(i == 0)
            for kvh in range(hkv):
                cols = slice(kvh * HEAD_DIM, (kvh + 1) * HEAD_DIM)
                k2 = jnp.concatenate([kp_ref[:, cols], kc_ref[:, cols]], axis=0)
                v2 = jnp.concatenate([vp_ref[:, cols], vc_ref[:, cols]], axis=0)
                dk2 = jnp.zeros((2 * BLOCK, HEAD_DIM), F32)
                dv2 = jnp.zeros((2 * BLOCK, HEAD_DIM), F32)
                for g in range(GROUP):
                    h = kvh * GROUP + g
                    hc = slice(h * HEAD_DIM, (h + 1) * HEAD_DIM)
                    q = q_ref[:, hc]
                    dov = do_ref[:, hc]
                    lse_h = lse_ref[:, h:h + 1]
                    sc = jnp.where(mask, _nt(q, k2) + bias_ref[h], NEG)
                    p = jnp.exp(sc - lse_h)
                    delta = jnp.sum(dov.astype(F32) * o_ref[:, hc].astype(F32), axis=-1, keepdims=True)
                    ds = p * (_nt(dov, v2) - delta)
                    dbias_ref[h] += ds
                    dsink_ref[0:1, h:h + 1] += jnp.sum(-jnp.exp(sink_ref[0, h] - lse_h) * delta, axis=0, keepdims=True)
                    dsb = ds.astype(BF16)
                    fresh[0, :, hc] = _nn(dsb, k2)
                    dk2 += _tn(dsb, q)
                    dv2 += _tn(p.astype(BF16), dov)
                kc_cols = slice(d + kvh * HEAD_DIM, d + (kvh + 1) * HEAD_DIM)
                vc_cols = slice(d + kw + kvh * HEAD_DIM, d + kw + (kvh + 1) * HEAD_DIM)
                fresh[0, :, kc_cols] = dk2[BLOCK:]
                fresh[0, :, vc_cols] = dv2[BLOCK:]
                fresh[1, :, kc_cols] = dk2[:BLOCK]
                fresh[1, :, vc_cols] = dv2[:BLOCK]

        lane = lax.broadcasted_iota(jnp.int32, (BLOCK, wide), 1)
        out_ref[...] = carry[...] + jnp.where(lane >= d, fresh[1], 0.0)

        @pl.when(i < nblk)
        def _():
            carry[...] = fresh[0]

    cur = lambda i: jnp.minimum(i, nblk - 1)
    prev = lambda i: jnp.maximum(jnp.minimum(i, nblk - 1) - 1, 0)
    return pl.pallas_call(
        body, name=name, grid=(nblk + 1,),
        in_specs=[pl.BlockSpec((BLOCK, d), lambda i: (cur(i), 0)),
                  pl.BlockSpec((BLOCK, kw), lambda i: (cur(i), kcol)),
                  pl.BlockSpec((BLOCK, kw), lambda i: (prev(i), kcol)),
                  pl.BlockSpec((BLOCK, kw), lambda i: (cur(i), kcol + 1)),
                  pl.BlockSpec((BLOCK, kw), lambda i: (prev(i), kcol + 1)),
                  pl.BlockSpec((hq, BLOCK, 2 * BLOCK), lambda i: (0, 0, 0)),
                  pl.BlockSpec(memory_space=pltpu.SMEM),
                  pl.BlockSpec((BLOCK, d), lambda i: (cur(i), 0)),
                  pl.BlockSpec((BLOCK, d), lambda i: (cur(i), 0)),
                  pl.BlockSpec((BLOCK, LANES), lambda i: (cur(i), 0))],
        out_specs=[pl.BlockSpec((BLOCK, wide), lambda i: (jnp.maximum(i - 1, 0), 0)),
                   pl.BlockSpec((hq, BLOCK, 2 * BLOCK), lambda i: (0, 0, 0)),
                   pl.BlockSpec((8, LANES), lambda i: (0, 0))],
        out_shape=[jax.ShapeDtypeStruct((s, wide), F32), jax.ShapeDtypeStruct((hq, BLOCK, 2 * BLOCK), F32),
                   jax.ShapeDtypeStruct((8, LANES), F32)],
        scratch_shapes=[pltpu.VMEM((BLOCK, wide), F32), pltpu.VMEM((2, BLOCK, wide), F32)],
        compiler_params=_params(1),
    )(qkn, qkn, qkn, proj, proj, bias, sink, do, o, lse)


def _rel_bucket_table():
    qi = np.arange(BLOCK)[:, None] + BLOCK
    kj = np.arange(2 * BLOCK)[None, :]
    n = np.maximum(qi - kj, 0)
    max_exact = REL_BUCKETS // 2
    nf = np.maximum(n, 1).astype(np.float32)
    large = max_exact + (np.log(nf / max_exact) / math.log(REL_MAX_DIST / max_exact)
                         * (REL_BUCKETS - max_exact)).astype(np.int32)
    large = np.minimum(large, REL_BUCKETS - 1)
    return np.where(n < max_exact, n, large).astype(np.int32)


def rel_bias_table(rel_bias, bucket):
    hq = rel_bias.shape[1]

    def body(rb_ref, bucket_ref, out_ref):
        tbl = bucket_ref[...]

        def per_head(h, carry):
            def per_bucket(b, acc):
                return jnp.where(tbl == b, rb_ref[b, h], acc)

            out_ref[h] = lax.fori_loop(0, REL_BUCKETS, per_bucket, jnp.zeros(tbl.shape, F32))
            return carry

        lax.fori_loop(0, hq, per_head, 0)

    return pl.pallas_call(
        body, name="rel_bias_table",
        in_specs=[pl.BlockSpec(memory_space=pltpu.SMEM), pl.BlockSpec(memory_space=pltpu.VMEM)],
        out_specs=pl.BlockSpec(memory_space=pltpu.VMEM),
        out_shape=jax.ShapeDtypeStruct((hq,) + tuple(bucket.shape), F32),
    )(rel_bias, bucket)


def rel_bias_grad(dbias, bucket):
    n_layers, hq = dbias.shape[:2]

    def body(db_ref, bucket_ref, out_ref):
        tbl = bucket_ref[...]

        def per_head(h, carry):
            dsum = db_ref[0, h]
            for a in range(1, n_layers):
                dsum = dsum + db_ref[a, h]

            def per_bucket(b, carry2):
                out_ref[b, h] = jnp.sum(jnp.where(tbl == b, dsum, 0.0))
                return carry2

            return lax.fori_loop(0, REL_BUCKETS, per_bucket, carry)

        lax.fori_loop(0, hq, per_head, 0)

    return pl.pallas_call(
        body, name="rel_bias_grad",
        in_specs=[pl.BlockSpec(memory_space=pltpu.VMEM), pl.BlockSpec(memory_space=pltpu.VMEM)],
        out_specs=pl.BlockSpec(memory_space=pltpu.SMEM),
        out_shape=jax.ShapeDtypeStruct((REL_BUCKETS, hq), F32),
    )(dbias, bucket)


def _split3(v):
    hi = v.astype(BF16)
    r1 = v - hi.astype(F32)
    mid = r1.astype(BF16)
    lo = (r1 - mid.astype(F32)).astype(BF16)
    return hi, mid, lo


def _tri_sum(tri, v):
    hi, mid, lo = _split3(v)
    return _nn(tri, hi) + _nn(tri, mid) + _nn(tri, lo)


def fox_gates(fl, b_f, name):
    s = fl.shape[0]
    t = _row_tile(s)

    def body(fl_ref, b_ref, f_ref, carry):
        @pl.when(pl.program_id(0) == 0)
        def _():
            carry[...] = jnp.zeros_like(carry)

        z = fl_ref[...] + b_ref[...]
        logf = jnp.minimum(z, 0.0) - jnp.log(1.0 + jnp.exp(-jnp.abs(z)))
        r = lax.broadcasted_iota(jnp.int32, (t, t), 0)
        cidx = lax.broadcasted_iota(jnp.int32, (t, t), 1)
        tri = jnp.where(cidx <= r, 1.0, 0.0).astype(BF16)
        f = _tri_sum(tri, logf) + carry[0:1, :]
        f_ref[...] = f
        carry[0:1, :] = f_ref[t - 1:t, :]

    blk = pl.BlockSpec((t, LANES), lambda i: (i, 0))
    return pl.pallas_call(
        body, name=name, grid=(s // t,),
        in_specs=[blk, pl.BlockSpec((1, LANES), lambda i: (0, 0))],
        out_specs=blk, out_shape=jax.ShapeDtypeStruct((s, LANES), F32),
        scratch_shapes=[pltpu.VMEM((8, LANES), F32)],
        compiler_params=_params(1),
    )(fl, b_f)


def fox_gates_bwd(fl, b_f, df_query, df_key, name):
    s = fl.shape[0]
    t = _row_tile(s)
    nb = s // t

    def body(fl_ref, b_ref, dfq_ref, dfk_ref, dfl_ref, db_ref, carry):
        @pl.when(pl.program_id(0) == 0)
        def _():
            carry[...] = jnp.zeros_like(carry)
            db_ref[...] = jnp.zeros_like(db_ref)

        dfv = dfq_ref[...] + dfk_ref[...]
        r = lax.broadcasted_iota(jnp.int32, (t, t), 0)
        cidx = lax.broadcasted_iota(jnp.int32, (t, t), 1)
        tri = jnp.where(cidx >= r, 1.0, 0.0).astype(BF16)
        dlog = _tri_sum(tri, dfv) + carry[0:1, :]
        carry[0:1, :] += jnp.sum(dfv, axis=0, keepdims=True)
        z = fl_ref[...] + b_ref[...]
        dz = dlog * (1.0 - _sigmoid(z))
        dfl_ref[...] = dz.astype(BF16)
        db_ref[0:1, :] += jnp.sum(dz, axis=0, keepdims=True)

    rev = pl.BlockSpec((t, LANES), lambda i: (nb - 1 - i, 0))
    return pl.pallas_call(
        body, name=name, grid=(nb,),
        in_specs=[rev, pl.BlockSpec((1, LANES), lambda i: (0, 0)), rev, rev],
        out_specs=[rev, pl.BlockSpec((8, LANES), lambda i: (0, 0))],
        out_shape=[jax.ShapeDtypeStruct((s, LANES), BF16), jax.ShapeDtypeStruct((8, LANES), F32)],
        scratch_shapes=[pltpu.VMEM((8, LANES), F32)],
        compiler_params=_params(1),
    )(fl, b_f, df_query, df_key)


def fox_fwd(qkn, proj, f_col, f_row, d, name):
    s = qkn.shape[0]
    t = _attn_tile(s)
    n_pairs = d // LANES
    nt = s // t

    def body(q_ref, k_ref, v_ref, fq_ref, fk_ref, o_ref, o32_ref, lse_ref, m_scr, l_scr, acc):
        i, j = pl.program_id(1), pl.program_id(2)

        @pl.when(j == 0)
        def _():
            m_scr[...] = jnp.full_like(m_scr, NEG)
            l_scr[...] = jnp.zeros_like(l_scr)
            acc[...] = jnp.zeros_like(acc)

        @pl.when(j <= i)
        def _():
            krow = lax.broadcasted_iota(jnp.int32, (t, t), 0)
            qcol = lax.broadcasted_iota(jnp.int32, (t, t), 1)
            visible = (krow <= qcol) | (j < i)
            for hh in range(2):
                hc = slice(hh * HEAD_DIM, (hh + 1) * HEAD_DIM)
                st = _nt(k_ref[:, hc], q_ref[:, hc]) + fq_ref[hh] - fk_ref[hh]
                st = jnp.where(visible, st, NEG)
                m_prev = m_scr[hh]
                m_new = jnp.maximum(m_prev, jnp.max(st, axis=0, keepdims=True))
                alpha = jnp.exp(m_prev - m_new)
                pt = jnp.exp(st - m_new)
                l_scr[hh] = alpha * l_scr[hh] + jnp.sum(pt, axis=0, keepdims=True)
                acc[hc, :] = alpha * acc[hc, :] + _tn(v_ref[:, hc], pt.astype(BF16))
                m_scr[hh] = m_new

        @pl.when(j == i)
        def _():
            l_full = jnp.concatenate([jnp.broadcast_to(l_scr[hh], (HEAD_DIM, t)) for hh in range(2)], axis=0)
            ov = (acc[...] / l_full).T
            o_ref[...] = ov.astype(BF16)
            o32_ref[...] = ov
            lse_ref[...] = m_scr[...] + jnp.log(l_scr[...])

    kv = lambda j, i: jnp.minimum(j, i)
    return pl.pallas_call(
        body, name=name, grid=(n_pairs, nt, nt),
        in_specs=[pl.BlockSpec((t, LANES), lambda p, i, j: (i, p)),
                  pl.BlockSpec((t, LANES), lambda p, i, j: (kv(j, i), n_pairs + p)),
                  pl.BlockSpec((t, LANES), lambda p, i, j: (kv(j, i), 2 * n_pairs + p)),
                  pl.BlockSpec((2, 1, t), lambda p, i, j: (p, 0, i)),
                  pl.BlockSpec((2, t, 1), lambda p, i, j: (p, kv(j, i), 0))],
        out_specs=[pl.BlockSpec((t, LANES), lambda p, i, j: (i, p)),
                   pl.BlockSpec((t, LANES), lambda p, i, j: (i, p)),
                   pl.BlockSpec((2, 1, t), lambda p, i, j: (p, 0, i))],
        out_shape=[jax.ShapeDtypeStruct((s, d), BF16), jax.ShapeDtypeStruct((s, d), F32),
                   jax.ShapeDtypeStruct((2 * n_pairs, 1, s), F32)],
        scratch_shapes=[pltpu.VMEM((2, 1, t), F32), pltpu.VMEM((2, 1, t), F32), pltpu.VMEM((2 * HEAD_DIM, t), F32)],
        compiler_params=_params(3),
    )(qkn, qkn, proj, f_row, f_col)


def fox_bwd(qkn, proj, f_col, f_row, lse_row, do, o, d, name):
    s = qkn.shape[0]
    t = _attn_tile(s)
    n_pairs = d // LANES
    nt = s // t

    def body(q_ref, k_ref, v_ref, fk_ref, fq_ref, lse_ref, do_ref, o_ref, out_ref, df_ref, dfq_ref,
             dq_acc, dkv_acc, df_acc, dfq_acc):
        j, i = pl.program_id(1), pl.program_id(2)

        @pl.when((j == 0) & (i == 0))
        def _():
            dq_acc[...] = jnp.zeros_like(dq_acc)
            dfq_acc[...] = jnp.zeros_like(dfq_acc)

        @pl.when(i == 0)
        def _():
            dkv_acc[...] = jnp.zeros_like(dkv_acc)
            df_acc[...] = jnp.zeros_like(df_acc)

        @pl.when(i >= j)
        def _():
            krow = lax.broadcasted_iota(jnp.int32, (t, t), 0)
            qcol = lax.broadcasted_iota(jnp.int32, (t, t), 1)
            visible = (krow <= qcol) | (i > j)
            ones = jnp.ones((8, HEAD_DIM), BF16)
            for hh in range(2):
                hc = slice(hh * HEAD_DIM, (hh + 1) * HEAD_DIM)
                q, k, v, dov = q_ref[:, hc], k_ref[:, hc], v_ref[:, hc], do_ref[:, hc]
                st = _nt(k, q) + fq_ref[hh] - fk_ref[hh]
                pt = jnp.exp(jnp.where(visible, st, NEG) - lse_ref[hh])
                hi, mid, lo = _split3(dov.astype(F32) * o_ref[:, hc])
                delta = jnp.max(_nt(ones, hi) + _nt(ones, mid) + _nt(ones, lo), axis=0, keepdims=True)
                dst = pt * (_nt(v, dov) - delta)
                dsb = dst.astype(BF16)
                dkv_acc[1, :, hc] += _nn(pt.astype(BF16), dov)
                dkv_acc[0, :, hc] += _nn(dsb, q)
                dq_acc[pl.ds(pl.multiple_of(i * t, t), t), hc] += _tn(dsb, k)
                df_acc[hh] -= jnp.sum(dst, axis=-1, keepdims=True)
                dfq_acc[i, hh] += jnp.sum(dst, axis=0, keepdims=True)

        @pl.when(i == nt - 1)
        def _():
            out_ref[0] = dq_acc[pl.ds(pl.multiple_of(j * t, t), t), :]
            out_ref[1] = dkv_acc[0]
            out_ref[2] = dkv_acc[1]
            df_ref[...] = df_acc[...]
            dfq_ref[...] = dfq_acc[j]

    qi = lambda j, i: jnp.maximum(i, j)
    return pl.pallas_call(
        body, name=name, grid=(n_pairs, nt, nt),
        in_specs=[pl.BlockSpec((t, LANES), lambda p, j, i: (qi(j, i), p)),
                  pl.BlockSpec((t, LANES), lambda p, j, i: (j, n_pairs + p)),
                  pl.BlockSpec((t, LANES), lambda p, j, i: (j, 2 * n_pairs + p)),
                  pl.BlockSpec((2, t, 1), lambda p, j, i: (p, j, 0)),
                  pl.BlockSpec((2, 1, t), lambda p, j, i: (p, 0, qi(j, i))),
                  pl.BlockSpec((2, 1, t), lambda p, j, i: (p, 0, qi(j, i))),
                  pl.BlockSpec((t, LANES), lambda p, j, i: (qi(j, i), p)),
                  pl.BlockSpec((t, LANES), lambda p, j, i: (qi(j, i), p))],
        out_specs=[pl.BlockSpec((3, t, LANES), lambda p, j, i: (0, j, p)),
                   pl.BlockSpec((2, t, 1), lambda p, j, i: (p, j, 0)),
                   pl.BlockSpec((2, 1, t), lambda p, j, i: (p, 0, j))],
        out_shape=[jax.ShapeDtypeStruct((3, s, d), F32), jax.ShapeDtypeStruct((2 * n_pairs, s, 1), F32),
                   jax.ShapeDtypeStruct((2 * n_pairs, 1, s), F32)],
        scratch_shapes=[pltpu.VMEM((s, LANES), F32), pltpu.VMEM((2, t, LANES), F32), pltpu.VMEM((2, t, 1), F32),
                        pltpu.VMEM((nt, 2, 1, t), F32)],
        compiler_params=_params(3),
    )(qkn, qkn, proj, f_col, f_row, lse_row, do, o)


def loss_head(y, target):
    s, d = y.shape
    tm = _row_tile(s)

    def body(y_ref, t_ref, dy_ref, loss_ref):
        @pl.when(pl.program_id(0) == 0)
        def _():
            loss_ref[...] = jnp.zeros_like(loss_ref)

        diff = y_ref[...] - t_ref[...]
        dy_ref[...] = diff * (1.0 / d)
        loss_ref[...] += 0.5 * jnp.sum(jnp.mean(diff * diff, axis=-1, keepdims=True), axis=0, keepdims=True)

    row = pl.BlockSpec((tm, d), lambda i: (i, 0))
    return pl.pallas_call(
        body, name="loss_head", grid=(s // tm,),
        in_specs=[row, row],
        out_specs=[row, pl.BlockSpec((8, LANES), lambda i: (0, 0))],
        out_shape=[jax.ShapeDtypeStruct((s, d), F32), jax.ShapeDtypeStruct((8, LANES), F32)],
        compiler_params=_params(1),
    )(y, target)


def ada_mod(c_all, w, b):
    n_layers, d, cols = w.shape

    def body(c_ref, w_ref, b_ref, o_ref):
        cv = c_ref[...]
        o_ref[...] = _nn(cv * _sigmoid(cv), w_ref[...]) + b_ref[...]

    return pl.pallas_call(
        body, name="ada_mod", grid=(n_layers,),
        in_specs=[pl.BlockSpec((N_DEV, d), lambda l: (0, 0)), pl.BlockSpec((None, d, cols), lambda l: (l, 0, 0)),
                  pl.BlockSpec((None, 1, cols), lambda l: (l, 0, 0))],
        out_specs=pl.BlockSpec((None, N_DEV, cols), lambda l: (l, 0, 0)),
        out_shape=jax.ShapeDtypeStruct((n_layers, N_DEV, cols), F32),
        compiler_params=_params(1),
    )(c_all, w, b)


def ada_grad(c_t, dmod):
    d = c_t.shape[0]
    n_layers, _, cols = dmod.shape
    tn = cols // 2

    def body(c_ref, dm_ref, o_ref):
        cv = c_ref[...]
        o_ref[...] = _nn(cv * _sigmoid(cv), dm_ref[...])

    return pl.pallas_call(
        body, name="ada_grad", grid=(n_layers, 2),
        in_specs=[pl.BlockSpec((d, N_DEV), lambda l, n: (0, 0)), pl.BlockSpec((None, N_DEV, tn), lambda l, n: (l, 0, n))],
        out_specs=pl.BlockSpec((None, d, tn), lambda l, n: (l, 0, n)),
        out_shape=jax.ShapeDtypeStruct((n_layers, d, cols), F32),
        compiler_params=_params(2),
    )(c_t, dmod)


def sum_devices(v):
    def body(v_ref, o_ref):
        acc = v_ref[0]
        for k in range(1, N_DEV):
            acc = acc + v_ref[k]
        o_ref[...] = acc

    return pl.pallas_call(body, name="sum_devices", out_shape=jax.ShapeDtypeStruct(v.shape[1:], F32))(v)


def sum_slots(r):
    _, rows, cols = r.shape
    tm = 256 if rows % 256 == 0 else rows

    def body(r_ref, o_ref):
        o_ref[...] = ((r_ref[3].astype(F32) + r_ref[0].astype(F32)) + r_ref[1].astype(F32)) + r_ref[2].astype(F32)

    return pl.pallas_call(
        body, name="sum_slots", grid=(rows // tm,),
        in_specs=[pl.BlockSpec((N_CHIP, tm, cols), lambda i: (0, i, 0))],
        out_specs=pl.BlockSpec((tm, cols), lambda i: (i, 0)),
        out_shape=jax.ShapeDtypeStruct((rows, cols), F32),
        compiler_params=_params(1),
    )(r)


def adamw(w, m, v, g, g2=None):
    rows, cols = w.shape
    tm = 256 if rows % 256 == 0 else rows
    two = g2 is not None
    c1 = 1.0 - ADAM_B1 ** ADAM_STEP
    c2 = 1.0 - ADAM_B2 ** ADAM_STEP

    def body(w_ref, m_ref, v_ref, g_ref, *rest):
        if two:
            g2_ref, go_ref, d_ref, mo_ref, vo_ref = rest
            gv = g_ref[...] + g2_ref[...]
        else:
            go_ref, d_ref, mo_ref, vo_ref = rest
            gv = g_ref[...]
        mn = ADAM_B1 * m_ref[...] + (1.0 - ADAM_B1) * gv
        vn = ADAM_B2 * v_ref[...] + (1.0 - ADAM_B2) * (gv * gv)
        go_ref[...] = gv
        mo_ref[...] = mn
        vo_ref[...] = vn
        d_ref[...] = -ADAM_LR * ((mn / c1) / (jnp.sqrt(vn / c2) + ADAM_EPS) + ADAM_WD * w_ref[...])

    blk = pl.BlockSpec((tm, cols), lambda i: (i, 0))
    ops = [w, m, v, g] + ([g2] if two else [])
    return pl.pallas_call(
        body, name="adamw", grid=(rows // tm,),
        in_specs=[blk] * len(ops), out_specs=[blk] * 4,
        out_shape=[jax.ShapeDtypeStruct((rows, cols), F32)] * 4,
        compiler_params=_params(1),
    )(*ops)


def _pad_rows(flat):
    n = flat.shape[0]
    rows = -(-n // LANES)
    return jnp.pad(flat, (0, rows * LANES - n)).reshape(rows, LANES)


def _pad_rows8(flat):
    rows = _pad_rows(flat)
    return jnp.pad(rows, ((0, -rows.shape[0] % 8), (0, 0)))


def _col_tiles(n):
    return next(k for k in range(1, n // LANES + 1) if n % (k * LANES) == 0 and n // k <= 1536)


def kernel(x, c, ada_w, ada_b, norm_g, ffn_w13, ffn_w2, rel_bias, swa_w_in, swa_w_out, swa_q_g, swa_k_g, swa_sink, fox_w_in, fox_w_out, fox_b_f, fox_q_g, fox_k_g, loss_target, m_ada_w, m_ada_b, m_norm_g, m_ffn_w13, m_ffn_w2, m_rel_bias, m_swa_w_in, m_swa_w_out, m_swa_q_g, m_swa_k_g, m_swa_sink, m_fox_w_in, m_fox_w_out, m_fox_b_f, m_fox_q_g, m_fox_k_g, v_ada_w, v_ada_b, v_norm_g, v_ffn_w13, v_ffn_w2, v_rel_bias, v_swa_w_in, v_swa_w_out, v_swa_q_g, v_swa_k_g, v_swa_sink, v_fox_w_in, v_fox_w_out, v_fox_b_f, v_fox_q_g, v_fox_k_g):
    ix, iy, ic = lax.axis_index("x"), lax.axis_index("y"), lax.axis_index("c")
    chip = 2 * ix + iy
    dev = 2 * chip + ic
    s, d = x.shape[1:]
    n_layers = ada_w.shape[0]
    n_a, n_b = swa_w_in.shape[0], fox_w_in.shape[0]
    hq = d // HEAD_DIM
    hkv = hq // GROUP
    kw = hkv * HEAD_DIM
    c13 = ffn_w13.shape[-1]
    f = 2 * c13
    r2 = ffn_w2.shape[2]
    cq = norm_g.shape[-1]
    a_in = d + 2 * kw
    fx = fox_w_in.shape[-1]
    b_in = N_CHIP * fx
    b_pad = 3 * d + LANES
    x0 = x[0]

    hello = _pad_rows8(jnp.concatenate([c.reshape(-1), norm_g.reshape(-1)]))
    hello_all = all_gather_rows(hello, "gather_c_norm").reshape(N_DEV, -1)
    c_all = hello_all[:, :d]
    ng = hello_all[::2, d:d + n_layers * 3 * cq].reshape(N_CHIP, n_layers, 3, cq)
    norm_full = jnp.moveaxis(ng, 0, 2).reshape(n_layers, 3, d)

    half_cols = ada_w.shape[-1] // 2
    w_half = lax.dynamic_slice_in_dim(ada_w, ic * half_cols, half_cols, axis=2)
    b_half = lax.dynamic_slice_in_dim(ada_b, dev * half_cols, half_cols, axis=1)[:, None, :]
    mod_part = ada_mod(c_all, w_half, b_half)
    mod_all = all_gather_rows(mod_part.reshape(n_layers * N_DEV, half_cols), "gather_mod")
    mod_all = mod_all.reshape(N_DEV, n_layers, N_DEV, half_cols)
    mod_mine = lax.dynamic_index_in_dim(mod_all, dev, axis=2, keepdims=False)
    mod_mine = jnp.moveaxis(mod_mine, 0, 1).reshape(n_layers, 3, 3, d)
    mv = jnp.concatenate([norm_full[:, :, None, :], mod_mine, jnp.zeros((n_layers, 3, 4, d), F32)], axis=2)

    kinds = [("col", c13), ("row", r2), ("col", swa_w_in.shape[-1]), ("row", swa_w_out.shape[1]), ("slot",),
             ("row", fox_w_out.shape[1])]
    w13, w2, wa_in, wa_out, wb_slots, wb_out = gather_weights(
        [ffn_w13.astype(BF16), ffn_w2.astype(BF16), swa_w_in.astype(BF16), swa_w_out.astype(BF16),
         fox_w_in.astype(BF16), fox_w_out.astype(BF16)], kinds)
    wb_in = jnp.concatenate([wb_slots[b] for b in range(N_CHIP)], axis=-1)
    wb_in = jnp.pad(wb_in, ((0, 0), (0, 0), (0, b_pad - b_in)))

    bucket = jnp.asarray(_rel_bucket_table())
    bias = rel_bias_table(rel_bias, bucket)
    tm, tw = _row_tile(s), _wide_tile(s)
    n13 = 2 * f // c13
    na_t, nb_t = _col_tiles(a_in), _col_tiles(3 * d)
    wa_t, wb_t = a_in // na_t, 3 * d // nb_t
    gate_blk = 3 * d // LANES

    def ffn_forward(xv, l, half, sub):
        a, h = modmm(xv, mv, (l, sub), w13, pl.BlockSpec((None, None, d, c13), lambda i, n: (l, half, 0, n)), n13,
                     jax.ShapeDtypeStruct((2, s, f), BF16),
                     pl.BlockSpec((None, tw, c13), lambda i, n: (n // 2, i, n % 2)), f"ffn_up_{l}_{half}", True)
        xo, y, u = resmm(xv, mv, (l, sub), 0.5, a, pl.BlockSpec((2, tm, c13), lambda i, k: (0, i, k)),
                         w2, pl.BlockSpec((None, None, c13, d), lambda i, k: (l, half, k, 0)), f // c13, c13,
                         f"ffn_down_{l}_{half}", True)
        return xo, dict(x=xv, h=h, a=a, u=u, y=y)

    saved = []
    xv = x0
    for l in range(n_layers):
        j = l // 2
        xv, s0 = ffn_forward(xv, l, 0, 0)
        if l % 2 == 0:
            proj, h = modmm(xv, mv, (l, 1), wa_in, pl.BlockSpec((None, d, wa_t), lambda i, n: (j, 0, n)), na_t,
                            jax.ShapeDtypeStruct((s, a_in), BF16), pl.BlockSpec((tw, wa_t), lambda i, n: (i, n)),
                            f"swa_in_{j}", True)
            gains = jnp.concatenate([jnp.tile(swa_q_g[j] * HEAD_DIM ** -0.5, hq), jnp.tile(swa_k_g[j], hkv)])[None, :]
            qkn = qknorm_fwd(proj, gains, kw, f"swa_qknorm_{j}")
            sink = swa_sink[j][None, :]
            o, lse = swa_fwd(qkn, proj, bias, sink, d, f"swa_attn_{j}")
            s1 = dict(x=xv, h=h, proj=proj, gains=gains, qkn=qkn, sink=sink, o=o, lse=lse)
            w_out, w_out_spec = wa_out, pl.BlockSpec((None, d, d), lambda i, k: (j, 0, 0))
        else:
            proj, h = modmm(xv, mv, (l, 1), wb_in, pl.BlockSpec((None, d, wb_t), lambda i, n: (j, 0, n)), nb_t,
                            jax.ShapeDtypeStruct((s, 3 * d), BF16), pl.BlockSpec((tw, wb_t), lambda i, n: (i, n)),
                            f"fox_in_{j}", True)
            fl, _ = modmm(xv, mv, (l, 1), wb_in, pl.BlockSpec((None, d, LANES), lambda i, n: (j, 0, gate_blk)), 1,
                          jax.ShapeDtypeStruct((s, LANES), F32), pl.BlockSpec((tw, LANES), lambda i, n: (i, 0)),
                          f"fox_gate_in_{j}", False)
            b_f = jnp.pad(fox_b_f[j], (0, LANES - hq))[None, :]
            fcum = fox_gates(fl, b_f, f"fox_gates_{j}")
            f_t = fcum[:, :hq].T
            f_col, f_row = f_t[:, :, None], f_t[:, None, :]
            gains = jnp.concatenate([jnp.tile(fox_q_g[j] * HEAD_DIM ** -0.5, hq), jnp.tile(fox_k_g[j], hq)])[None, :]
            qkn = qknorm_fwd(proj, gains, d, f"fox_qknorm_{j}")
            o, o32, lse = fox_fwd(qkn, proj, f_col, f_row, d, f"fox_attn_{j}")
            s1 = dict(x=xv, h=h, proj=proj, gains=gains, qkn=qkn, fl=fl, b_f=b_f, f_col=f_col, f_row=f_row, o=o, o32=o32,
                      lse=lse)
            w_out, w_out_spec = wb_out, pl.BlockSpec((None, d, d), lambda i, k: (j, 0, 0))
        xv, y = resmm(xv, mv, (l, 1), 1.0, o, pl.BlockSpec((tm, d), lambda i, k: (i, 0)), w_out, w_out_spec, 1, d,
                      f"mixer_out_{l}", False)
        s1["y"] = y
        xv, s2 = ffn_forward(xv, l, 1, 2)
        saved.append((s0, s1, s2))

    dxv, loss_part = loss_head(xv, loss_target[0])
    loss = lax.psum(loss_part[0, 0], ("x", "y", "c"))

    g13 = lax.empty((n_layers, 2, d, 2 * f), BF16)
    g2 = lax.empty((n_layers, 2, f, d), BF16)
    ga_in = lax.empty((n_a, d, a_in), BF16)
    ga_out = lax.empty((n_a, d, d), BF16)
    gb_in = lax.empty((n_b, d, b_pad), BF16)
    gb_out = lax.empty((n_b, d, d), BF16)
    dmod = [[None] * 3 for _ in range(n_layers)]
    dnorm = [[None] * 3 for _ in range(n_layers)]
    dqk_gain = {}
    dsink, db_f, dbias_tabs = {}, {}, []

    def ffn_backward(dxo, sv, l, half, sub):
        nonlocal g13, g2
        dy, da, dgate = resmm_bwd(dxo, sv["y"], mv, (l, sub), 0.5, w2,
                                  pl.BlockSpec((None, None, c13, d), lambda i, k: (l, half, k, 0)), f // c13, c13,
                                  f"ffn_down_bwd_{l}_{half}", a=sv["a"])
        g2 = weight_grad(sv["u"], pl.BlockSpec((tw, c13), lambda m, n, k: (k, m)), dy,
                         pl.BlockSpec((tw, d), lambda m, n, k: (k, 0)), (f // c13, 1), s, c13, d, g2,
                         pl.BlockSpec((None, None, c13, d), lambda m, n, k: (l, half, m, 0)), f"ffn_w2_grad_{l}_{half}")
        g13 = weight_grad(sv["h"], pl.BlockSpec((tw, d), lambda m, n, k: (k, 0)), da,
                          pl.BlockSpec((None, tw, c13), lambda m, n, k: (n // 2, k, n % 2)), (1, n13), s, d, c13, g13,
                          pl.BlockSpec((None, None, d, c13), lambda m, n, k: (l, half, 0, n)), f"ffn_w13_grad_{l}_{half}")
        dx, red = modmm_bwd(da, pl.BlockSpec((None, tm, c13), lambda i, n: (n // 2, i, n % 2)), w13,
                            pl.BlockSpec((None, None, d, c13), lambda i, n: (l, half, 0, n)), n13, sv["x"], dxo, mv,
                            (l, sub), f"ffn_up_bwd_{l}_{half}")
        dmod[l][sub] = (red[1], red[2], dgate[0])
        dnorm[l][sub] = red[0]
        return dx

    for l in reversed(range(n_layers)):
        j = l // 2
        s0, s1, s2 = saved[l]
        dxv = ffn_backward(dxv, s2, l, 1, 2)
        is_a = l % 2 == 0
        w_out = wa_out if is_a else wb_out
        dy, do, dgate = resmm_bwd(dxv, s1["y"], mv, (l, 1), 1.0, w_out, pl.BlockSpec((None, d, d), lambda i, k: (j, 0, 0)),
                                  1, d, f"mixer_out_bwd_{l}")
        g_out = weight_grad(s1["o"], pl.BlockSpec((tw, d), lambda m, n, k: (k, 0)), dy,
                            pl.BlockSpec((tw, d), lambda m, n, k: (k, 0)), (1, 1), s, d, d, ga_out if is_a else gb_out,
                            pl.BlockSpec((None, d, d), lambda m, n, k: (j, 0, 0)), f"mixer_out_grad_{l}")
        if is_a:
            ga_out = g_out
            d_qkv, dbias_tab, dsk = swa_bwd(s1["qkn"], s1["proj"], bias, s1["sink"], do, s1["o"], s1["lse"], d,
                                            f"swa_attn_bwd_{j}")
            dbias_tabs.append(dbias_tab)
            dsink[j] = dsk[0, :hq]
            dproj, dgain = qknorm_bwd(s1["proj"], s1["gains"], d_qkv, pl.BlockSpec((tm, kw), lambda c_, i: (i, c_)),
                                      a_in, kw, f"swa_qknorm_bwd_{j}")
            ga_in = weight_grad(s1["h"], pl.BlockSpec((tw, d), lambda m, n, k: (k, 0)), dproj,
                                pl.BlockSpec((tw, wa_t), lambda m, n, k: (k, n)), (1, na_t), s, d, wa_t, ga_in,
                                pl.BlockSpec((None, d, wa_t), lambda m, n, k: (j, 0, n)), f"swa_in_grad_{j}")
            dxv, red = modmm_bwd(dproj, pl.BlockSpec((tm, wa_t), lambda i, n: (i, n)), wa_in,
                                 pl.BlockSpec((None, d, wa_t), lambda i, n: (j, 0, n)), na_t, s1["x"], dxv, mv, (l, 1),
                                 f"swa_in_bwd_{j}")
            dqk_gain[("a", j)] = (dgain[0, :d].reshape(hq, HEAD_DIM).sum(0) * HEAD_DIM ** -0.5,
                                  dgain[0, d:d + kw].reshape(hkv, HEAD_DIM).sum(0))
        else:
            gb_out = g_out
            lse_row = s1["lse"].reshape(hq, 1, s)
            d_qkv, df_col, dfq_row = fox_bwd(s1["qkn"], s1["proj"], s1["f_col"], s1["f_row"], lse_row, do, s1["o32"], d,
                                             f"fox_attn_bwd_{j}")
            lanes_of_heads = lambda a: jnp.pad(a.T, ((0, 0), (0, LANES - hq)))
            dfl, dbf = fox_gates_bwd(s1["fl"], s1["b_f"], lanes_of_heads(dfq_row[:, 0, :]), lanes_of_heads(df_col[:, :, 0]),
                                     f"fox_gates_bwd_{j}")
            db_f[j] = dbf[0, :hq]
            dproj, dgain = qknorm_bwd(s1["proj"], s1["gains"], d_qkv, pl.BlockSpec((None, tm, d), lambda c_, i: (c_, i, 0)),
                                      3 * d, d, f"fox_qknorm_bwd_{j}")
            gb_in = weight_grad(s1["h"], pl.BlockSpec((tw, d), lambda m, n, k: (k, 0)), dproj,
                                pl.BlockSpec((tw, wb_t), lambda m, n, k: (k, n)), (1, nb_t), s, d, wb_t, gb_in,
                                pl.BlockSpec((None, d, wb_t), lambda m, n, k: (j, 0, n)), f"fox_in_grad_{j}")
            gb_in = weight_grad(s1["h"], pl.BlockSpec((tw, d), lambda m, n, k: (k, 0)), dfl,
                                pl.BlockSpec((tw, LANES), lambda m, n, k: (k, 0)), (1, 1), s, d, LANES, gb_in,
                                pl.BlockSpec((None, d, LANES), lambda m, n, k: (j, 0, gate_blk)), f"fox_gate_in_grad_{j}")
            dxv, red = modmm_bwd(dproj, pl.BlockSpec((tm, wb_t), lambda i, n: (i, n)), wb_in,
                                 pl.BlockSpec((None, d, wb_t), lambda i, n: (j, 0, n)), nb_t, s1["x"], dxv, mv, (l, 1),
                                 f"fox_in_bwd_{j}",
                                 more=(dfl, pl.BlockSpec((tm, LANES), lambda i, n: (i, 0)),
                                       pl.BlockSpec((None, d, LANES), lambda i, n: (j, 0, gate_blk))))
            dqk_gain[("b", j)] = (dgain[0, :d].reshape(hq, HEAD_DIM).sum(0) * HEAD_DIM ** -0.5,
                                  dgain[0, d:2 * d].reshape(hq, HEAD_DIM).sum(0))
        dmod[l][1] = (red[1], red[2], dgate[0])
        dnorm[l][1] = red[0]
        dxv = ffn_backward(dxv, s0, l, 0, 0)
    grad_x = dxv[None]

    drel = rel_bias_grad(jnp.stack(dbias_tabs), bucket)
    dmod_flat = jnp.stack([jnp.stack([jnp.stack(dmod[l][sub]) for sub in range(3)]) for l in range(n_layers)]).reshape(-1)
    dnorm_flat = jnp.stack([jnp.stack(dnorm[l]) for l in range(n_layers)]).reshape(-1)
    pieces = [dmod_flat, dnorm_flat,
              jnp.stack([dqk_gain[("a", j)][0] for j in range(n_a)]).reshape(-1),
              jnp.stack([dqk_gain[("a", j)][1] for j in range(n_a)]).reshape(-1),
              jnp.stack([dqk_gain[("b", j)][0] for j in range(n_b)]).reshape(-1),
              jnp.stack([dqk_gain[("b", j)][1] for j in range(n_b)]).reshape(-1),
              jnp.stack([dsink[j] for j in range(n_a)]).reshape(-1),
              jnp.stack([db_f[j] for j in range(n_b)]).reshape(-1),
              drel.reshape(-1)]
    rows = [_pad_rows(p) for p in pieces]
    starts = np.cumsum([0] + [r.shape[0] for r in rows])
    total = -(-int(starts[-1]) // 8) * 8
    small = jnp.pad(jnp.concatenate(rows), ((0, total - int(starts[-1])), (0, 0)))
    small_all = all_gather_rows(small, "gather_small_grads").reshape(N_DEV, total, LANES)
    small_sum = sum_devices(small_all)

    def piece(k, shape):
        n = int(np.prod(shape))
        return small_sum[int(starts[k]):int(starts[k + 1])].reshape(-1)[:n].reshape(shape)

    g_ada_b = piece(0, (n_layers, 9 * d))
    g_norm = lax.dynamic_slice_in_dim(piece(1, (n_layers, 3, d)), chip * cq, cq, axis=2)
    g_swa_q, g_swa_k = piece(2, (n_a, HEAD_DIM)), piece(3, (n_a, HEAD_DIM))
    g_fox_q, g_fox_k = piece(4, (n_b, HEAD_DIM)), piece(5, (n_b, HEAD_DIM))
    g_sink, g_bf, g_rel = piece(6, (n_a, hq)), piece(7, (n_b, hq)), piece(8, (REL_BUCKETS, hq))

    dmod_all = small_all[:, :int(starts[1])].reshape(N_DEV, -1)[:, :n_layers * 9 * d].reshape(N_DEV, n_layers, 9 * d)
    ada_cols = ada_w.shape[-1]
    dmod_mine = lax.dynamic_slice_in_dim(jnp.moveaxis(dmod_all, 0, 1), chip * ada_cols, ada_cols, axis=2)
    g_ada_w = ada_grad(c_all.T, dmod_mine)

    gb_in_slots = jnp.stack([gb_in[:, :, b * fx:(b + 1) * fx] for b in range(N_CHIP)])
    grads = [g13, g2, ga_in, ga_out, gb_in_slots, gb_out]
    shard_shapes = [ffn_w13.shape, ffn_w2.shape, swa_w_in.shape, swa_w_out.shape, fox_w_in.shape, fox_w_out.shape]
    landed = scatter_grads(grads, kinds, shard_shapes)
    parts = [sum_slots(r.reshape(N_CHIP, -1, r.shape[-1])) for r in landed]
    others = swap_with_sibling(parts)

    def update(w, m, v, g, g2=None):
        w2d = w.reshape(-1, w.shape[-1])
        outs = adamw(w2d, m.reshape(w2d.shape), v.reshape(w2d.shape), g.reshape(w2d.shape) if g2 is None else g, g2)
        return [t.reshape(w.shape) for t in outs]

    big = [(ffn_w13, m_ffn_w13, v_ffn_w13), (ffn_w2, m_ffn_w2, v_ffn_w2), (swa_w_in, m_swa_w_in, v_swa_w_in),
           (swa_w_out, m_swa_w_out, v_swa_w_out), (fox_w_in, m_fox_w_in, v_fox_w_in), (fox_w_out, m_fox_w_out, v_fox_w_out)]
    big_out = [update(w, m, v, p, q) for (w, m, v), p, q in zip(big, parts, others)]
    r_ada_w = update(ada_w, m_ada_w, v_ada_w, g_ada_w)
    r_ada_b = update(ada_b, m_ada_b, v_ada_b, g_ada_b)
    r_norm = update(norm_g, m_norm_g, v_norm_g, g_norm)
    r_rel = update(rel_bias, m_rel_bias, v_rel_bias, g_rel)
    r_swa_q = update(swa_q_g, m_swa_q_g, v_swa_q_g, g_swa_q)
    r_swa_k = update(swa_k_g, m_swa_k_g, v_swa_k_g, g_swa_k)
    r_sink = update(swa_sink, m_swa_sink, v_swa_sink, g_sink)
    r_bf = update(fox_b_f, m_fox_b_f, v_fox_b_f, g_bf)
    r_fox_q = update(fox_q_g, m_fox_q_g, v_fox_q_g, g_fox_q)
    r_fox_k = update(fox_k_g, m_fox_k_g, v_fox_k_g, g_fox_k)
    per_weight = [r_ada_w, r_ada_b, r_norm, big_out[0], big_out[1], r_rel, big_out[2], big_out[3], r_swa_q, r_swa_k,
                  r_sink, big_out[4], big_out[5], r_bf, r_fox_q, r_fox_k]
    return (loss, grad_x, *[r[0] for r in per_weight], *[r[1] for r in per_weight],
            *[r[2] for r in per_weight], *[r[3] for r in per_weight])
```

```python
import math

import numpy as np
import jax
import jax.numpy as jnp
from jax import lax
from jax.experimental import pallas as pl
from jax.experimental.pallas import tpu as pltpu

F32 = jnp.float32
BF16 = jnp.bfloat16
HEAD_DIM = 64
GROUP = 4
BLOCK = 128
REL_BUCKETS = 32
REL_MAX_DIST = 128
EPS = 1e-6
NEG = -1e30
N_CHIP = 4
N_DEV = 8
LANES = 128
VMEM_LIMIT = 52 * 1024 * 1024
ADAM_LR, ADAM_B1, ADAM_B2, ADAM_EPS, ADAM_WD, ADAM_STEP = 0.001, 0.9, 0.999, 1e-08, 0.01, 10
MESH = pl.DeviceIdType.MESH
ANY = pl.BlockSpec(memory_space=pl.ANY)


def _params(n_axes):
    return pltpu.CompilerParams(dimension_semantics=("arbitrary",) * n_axes, vmem_limit_bytes=VMEM_LIMIT)


def _nn(a, b):
    return jnp.dot(a, b, preferred_element_type=F32)


def _nt(a, b):
    return lax.dot_general(a, b, (((1,), (1,)), ((), ())), preferred_element_type=F32)


def _tn(a, b):
    return lax.dot_general(a, b, (((0,), (0,)), ((), ())), preferred_element_type=F32)


def _sigmoid(z):
    return 1.0 / (1.0 + jnp.exp(-z))


def _sigmoid_fast(z):
    return pl.reciprocal(1.0 + jnp.exp(-z), approx=True)


def _row_tile(s):
    return 512 if s >= 2048 else s // 2


def _wide_tile(s):
    return 1024 if s >= 2048 else s // 2


def _attn_tile(s):
    return 512 if s >= 2048 else s // 4


def _position():
    x, y, c = lax.axis_index("x"), lax.axis_index("y"), lax.axis_index("c")
    chips = [(1 - x, y), (x, 1 - y), (1 - x, 1 - y)]
    return x, y, c, chips


def all_gather_rows(v, name):
    m_per, n = v.shape

    def body(x_ref, out_ref, send_sems, recv_sems, local_sem):
        x, y, c, chips = _position()
        me, sibling = (x, y, c), (x, y, 1 - c)

        def rows(px, py, pc):
            return out_ref.at[pl.ds((4 * px + 2 * py + pc) * m_per, m_per), :]

        def copy(k, block, to, src=None):
            return pltpu.make_async_remote_copy(
                src_ref=rows(*block) if src is None else src, dst_ref=rows(*block),
                send_sem=send_sems.at[k], recv_sem=recv_sems.at[k], device_id=to, device_id_type=MESH)

        mine = pltpu.make_async_copy(x_ref, rows(*me), local_sem)
        mine.start()
        first = [copy(0, me, sibling, src=x_ref)]
        first += [copy(1 + j, me, (*chip, c), src=x_ref) for j, chip in enumerate(chips)]
        for cp in first:
            cp.start()
        passed = [copy(4 + j, (*chip, c), sibling) for j, chip in enumerate(chips)]
        for j, chip in enumerate(chips):
            copy(1 + j, (*chip, c), me).wait_recv()
            passed[j].start()
        copy(0, sibling, me).wait_recv()
        for j, chip in enumerate(chips):
            copy(4 + j, (*chip, 1 - c), me).wait_recv()
        for cp in first + passed:
            cp.wait_send()
        mine.wait()

    return pl.pallas_call(
        body, name=name,
        out_shape=jax.ShapeDtypeStruct((N_DEV * m_per, n), v.dtype),
        in_specs=[pl.BlockSpec(memory_space=pltpu.VMEM)],
        out_specs=pl.BlockSpec(memory_space=pltpu.VMEM),
        scratch_shapes=[pltpu.SemaphoreType.DMA((7,)), pltpu.SemaphoreType.DMA((7,)), pltpu.SemaphoreType.DMA],
    )(v)


def _slab(full_ref, kind, b, lead):
    how = kind[0]
    if how == "slot":
        return full_ref.at[b, lead]
    if how == "col":
        w = kind[1]
        idx = (lead,) + (slice(None),) * (len(full_ref.shape) - 2) + (pl.ds(pl.multiple_of(b * w, LANES), w),)
        return full_ref.at[idx]
    h = kind[1]
    idx = (lead,) + (slice(None),) * (len(full_ref.shape) - 3) + (pl.ds(pl.multiple_of(b * h, 8), h), slice(None))
    return full_ref.at[idx]


def _full_shape(shard_shape, kind):
    if kind[0] == "slot":
        return (N_CHIP,) + tuple(shard_shape)
    if kind[0] == "col":
        return tuple(shard_shape[:-1]) + (N_CHIP * shard_shape[-1],)
    return tuple(shard_shape[:-2]) + (N_CHIP * shard_shape[-2], shard_shape[-1])


def gather_weights(shards, kinds):
    n = len(shards)

    def body(*refs):
        ins, outs = refs[:n], refs[n:2 * n]
        send_sems, recv_sems, local_sems = refs[2 * n:]
        x, y, c, chips = _position()
        b_me = 2 * x + y
        sibling = (x, y, 1 - c)
        local, sends = [], []
        for t in range(n):
            half = ins[t].shape[0] // 2
            mine = pl.ds(c * half, half)
            whole = pl.ds(0, 2 * half)
            cp = pltpu.make_async_copy(ins[t], _slab(outs[t], kinds[t], b_me, whole), local_sems.at[t])
            cp.start()
            local.append(cp)
            for j, chip in enumerate(chips):
                cp = pltpu.make_async_remote_copy(
                    src_ref=ins[t].at[mine], dst_ref=_slab(outs[t], kinds[t], b_me, mine),
                    send_sem=send_sems.at[6 * t + j], recv_sem=recv_sems.at[6 * t + j],
                    device_id=(*chip, c), device_id_type=MESH)
                cp.start()
                sends.append(cp)
        for t in range(n):
            half = ins[t].shape[0] // 2
            mine = pl.ds(c * half, half)
            for j, chip in enumerate(chips):
                landed = _slab(outs[t], kinds[t], 2 * chip[0] + chip[1], mine)
                pltpu.make_async_remote_copy(
                    src_ref=landed, dst_ref=landed, send_sem=send_sems.at[6 * t + j], recv_sem=recv_sems.at[6 * t + j],
                    device_id=(*chip, c), device_id_type=MESH).wait_recv()
                cp = pltpu.make_async_remote_copy(
                    src_ref=landed, dst_ref=landed, send_sem=send_sems.at[6 * t + 3 + j],
                    recv_sem=recv_sems.at[6 * t + 3 + j], device_id=sibling, device_id_type=MESH)
                cp.start()
                sends.append(cp)
        for t in range(n):
            half = ins[t].shape[0] // 2
            theirs = pl.ds((1 - c) * half, half)
            for j, chip in enumerate(chips):
                landed = _slab(outs[t], kinds[t], 2 * chip[0] + chip[1], theirs)
                pltpu.make_async_remote_copy(
                    src_ref=landed, dst_ref=landed, send_sem=send_sems.at[6 * t + 3 + j],
                    recv_sem=recv_sems.at[6 * t + 3 + j], device_id=sibling, device_id_type=MESH).wait_recv()
        for cp in sends:
            cp.wait_send()
        for cp in local:
            cp.wait()

    return pl.pallas_call(
        body, name="gather_weights",
        out_shape=[jax.ShapeDtypeStruct(_full_shape(s.shape, k), s.dtype) for s, k in zip(shards, kinds)],
        in_specs=[ANY] * n, out_specs=[ANY] * n,
        scratch_shapes=[pltpu.SemaphoreType.DMA((6 * n,)), pltpu.SemaphoreType.DMA((6 * n,)),
                        pltpu.SemaphoreType.DMA((n,))],
    )(*shards)


def scatter_grads(grads, kinds, shard_shapes):
    n = len(grads)

    def body(*refs):
        ins, outs = refs[:n], refs[n:2 * n]
        send_sems, recv_sems, local_sems = refs[2 * n:]
        x, y, c, chips = _position()
        b_me = 2 * x + y
        local, sends = [], []
        for t in range(n):
            whole = pl.ds(0, ins[t].shape[1] if kinds[t][0] == "slot" else ins[t].shape[0])
            cp = pltpu.make_async_copy(_slab(ins[t], kinds[t], b_me, whole), outs[t].at[3], local_sems.at[t])
            cp.start()
            local.append(cp)
            for j, chip in enumerate(chips):
                cp = pltpu.make_async_remote_copy(
                    src_ref=_slab(ins[t], kinds[t], 2 * chip[0] + chip[1], whole), dst_ref=outs[t].at[j],
                    send_sem=send_sems.at[3 * t + j], recv_sem=recv_sems.at[3 * t + j],
                    device_id=(*chip, c), device_id_type=MESH)
                cp.start()
                sends.append(cp)
        for t in range(n):
            for j, chip in enumerate(chips):
                pltpu.make_async_remote_copy(
                    src_ref=outs[t].at[j], dst_ref=outs[t].at[j], send_sem=send_sems.at[3 * t + j],
                    recv_sem=recv_sems.at[3 * t + j], device_id=(*chip, c), device_id_type=MESH).wait_recv()
        for cp in sends:
            cp.wait_send()
        for cp in local:
            cp.wait()

    return pl.pallas_call(
        body, name="scatter_grads",
        out_shape=[jax.ShapeDtypeStruct((N_CHIP,) + tuple(s), g.dtype) for g, s in zip(grads, shard_shapes)],
        in_specs=[ANY] * n, out_specs=[ANY] * n,
        scratch_shapes=[pltpu.SemaphoreType.DMA((3 * n,)), pltpu.SemaphoreType.DMA((3 * n,)),
                        pltpu.SemaphoreType.DMA((n,))],
    )(*grads)


def _whole(ref, kind):
    return pl.ds(0, ref.shape[1] if kind[0] == "slot" else ref.shape[0])


def place_shards(srcs, dests, kinds, into_slab, name):
    n = len(srcs)

    def body(*refs):
        ins, outs, sems = refs[:n], refs[2 * n:3 * n], refs[3 * n]
        x, y, _, _ = _position()
        b_me = 2 * x + y
        cps = []
        for t in range(n):
            if into_slab:
                cp = pltpu.make_async_copy(ins[t], _slab(outs[t], kinds[t], b_me, _whole(outs[t], kinds[t])), sems.at[t])
            else:
                cp = pltpu.make_async_copy(_slab(ins[t], kinds[t], b_me, _whole(ins[t], kinds[t])), outs[t].at[3], sems.at[t])
            cp.start()
            cps.append(cp)
        for cp in cps:
            cp.wait()

    return pl.pallas_call(
        body, name=name,
        out_shape=[jax.ShapeDtypeStruct(v.shape, v.dtype) for v in dests],
        in_specs=[ANY] * (2 * n), out_specs=[ANY] * n,
        input_output_aliases={n + t: t for t in range(n)},
        scratch_shapes=[pltpu.SemaphoreType.DMA((n,))],
    )(*srcs, *dests)


def _exchange_copies(srcs, lands, kinds, gather, send_sems, recv_sems):
    x, y, c, chips = _position()
    b_me = 2 * x + y
    out = []
    for t in range(len(srcs)):
        for j, chip in enumerate(chips):
            b_j = 2 * chip[0] + chip[1]
            if gather:
                src = srcs[t]
                sent_to = _slab(lands[t], kinds[t], b_me, _whole(lands[t], kinds[t]))
                arrives = _slab(lands[t], kinds[t], b_j, _whole(lands[t], kinds[t]))
            else:
                src = _slab(srcs[t], kinds[t], b_j, _whole(srcs[t], kinds[t]))
                sent_to = arrives = lands[t].at[j]
            k = 3 * t + j
            send = pltpu.make_async_remote_copy(src_ref=src, dst_ref=sent_to, send_sem=send_sems.at[k],
                                                recv_sem=recv_sems.at[k], device_id=(*chip, c), device_id_type=MESH)
            recv = pltpu.make_async_remote_copy(src_ref=src, dst_ref=arrives, send_sem=send_sems.at[k],
                                                recv_sem=recv_sems.at[k], device_id=(*chip, c), device_id_type=MESH)
            out.append((send, recv))
    return out


def exchange_start(srcs, lands, kinds, gather, name):
    n = len(srcs)
    hbm = pl.BlockSpec(memory_space=pltpu.HBM)

    def body(*refs):
        ins, lnd = refs[:n], refs[n:2 * n]
        send_sems, recv_sems = refs[2 * n], refs[2 * n + 1]
        token = refs[-1]
        for send, _ in _exchange_copies(ins, lnd, kinds, gather, send_sems, recv_sems):
            send.start()
        token[...] = jnp.zeros_like(token)

    ops = [pltpu.with_memory_space_constraint(v, pltpu.HBM) for v in (*srcs, *lands)]
    res = pl.pallas_call(
        body, name=name,
        out_shape=(pltpu.SemaphoreType.DMA((3 * n,)), pltpu.SemaphoreType.DMA((3 * n,)),
                   *[pltpu.HBM(v.shape, v.dtype) for v in ops], jax.ShapeDtypeStruct((8, LANES), F32)),
        in_specs=[hbm] * (2 * n),
        out_specs=(pl.BlockSpec(memory_space=pltpu.SEMAPHORE), pl.BlockSpec(memory_space=pltpu.SEMAPHORE),
                   *[hbm] * (2 * n), pl.BlockSpec(memory_space=pltpu.VMEM)),
        input_output_aliases={t: 2 + t for t in range(2 * n)},
        compiler_params=pltpu.CompilerParams(has_side_effects=pltpu.SideEffectType.DATAFLOW_SIDE_EFFECTING),
    )(*ops)
    return res[0], res[1], list(res[2:2 + n]), list(res[2 + n:2 + 2 * n]), res[-1]


def exchange_wait(send_sems, recv_sems, srcs, lands, after, kinds, gather, name):
    n = len(srcs)
    hbm = pl.BlockSpec(memory_space=pltpu.HBM)

    def body(*refs):
        ins, lnd = refs[:n], refs[n:2 * n]
        ssem, rsem = refs[2 * n], refs[2 * n + 1]
        for send, recv in _exchange_copies(ins, lnd, kinds, gather, ssem, rsem):
            send.wait_send()
            recv.wait_recv()

    res = pl.pallas_call(
        body, name=name,
        out_shape=tuple(pltpu.HBM(v.shape, v.dtype) for v in (*srcs, *lands)),
        in_specs=[hbm] * (2 * n) + [pl.BlockSpec(memory_space=pltpu.SEMAPHORE)] * 2 + [ANY],
        out_specs=tuple([hbm] * (2 * n)),
        input_output_aliases={t: t for t in range(2 * n)},
        compiler_params=pltpu.CompilerParams(has_side_effects=pltpu.SideEffectType.DATAFLOW_SIDE_EFFECTING),
    )(*srcs, *lands, send_sems, recv_sems, after)
    return list(res[n:])


def swap_with_sibling(parts):
    n = len(parts)

    def body(*refs):
        ins, outs = refs[:n], refs[n:2 * n]
        send_sems, recv_sems = refs[2 * n:]
        x, y, c, _ = _position()
        cps = []
        for t in range(n):
            cp = pltpu.make_async_remote_copy(
                src_ref=ins[t], dst_ref=outs[t], send_sem=send_sems.at[t], recv_sem=recv_sems.at[t],
                device_id=(x, y, 1 - c), device_id_type=MESH)
            cp.start()
            cps.append(cp)
        for cp in cps:
            cp.wait_recv()
        for cp in cps:
            cp.wait_send()

    return pl.pallas_call(
        body, name="swap_with_sibling",
        out_shape=[jax.ShapeDtypeStruct(p.shape, p.dtype) for p in parts],
        in_specs=[ANY] * n, out_specs=[ANY] * n,
        scratch_shapes=[pltpu.SemaphoreType.DMA((n,)), pltpu.SemaphoreType.DMA((n,))],
    )(*parts)


def _modulated(xv, mv_ref):
    g, shift, scale = mv_ref[0:1, :], mv_ref[1:2, :], mv_ref[2:3, :]
    r = lax.rsqrt(jnp.mean(xv * xv, axis=-1, keepdims=True) + EPS)
    xhat = xv * r
    xn = xhat * g
    return xn * (1.0 + scale) + shift, xhat, xn, r, g, scale


def modmm(x, mv, mv_idx, w, w_spec, n_tiles, out_shape, out_spec, name, want_h):
    s, d = x.shape
    tm = _wide_tile(s)

    def body(x_ref, mv_ref, w_ref, *rest):
        if want_h:
            out_ref, h_ref, h_scr = rest
        else:
            out_ref, h_scr = rest

        @pl.when(pl.program_id(1) == 0)
        def _():
            h = _modulated(x_ref[...], mv_ref)[0].astype(BF16)
            h_scr[...] = h
            if want_h:
                h_ref[...] = h

        out_ref[...] = _nn(h_scr[...], w_ref[...]).astype(out_ref.dtype)

    out_shapes = [out_shape]
    out_specs = [out_spec]
    if want_h:
        out_shapes.append(jax.ShapeDtypeStruct((s, d), BF16))
        out_specs.append(pl.BlockSpec((tm, d), lambda i, n: (i, 0)))
    res = pl.pallas_call(
        body, name=name, grid=(s // tm, n_tiles),
        in_specs=[pl.BlockSpec((tm, d), lambda i, n: (i, 0)),
                  pl.BlockSpec((None, None, 8, d), lambda i, n: (*mv_idx, 0, 0)), w_spec],
        out_specs=out_specs, out_shape=out_shapes,
        scratch_shapes=[pltpu.VMEM((tm, d), BF16)],
        compiler_params=_params(2),
    )(x, mv, w)
    return res if want_h else (res[0], None)


def resmm(x, mv, mv_idx, coef, lhs, lhs_spec, w, w_spec, k_tiles, tk, name, ffn):
    s, d = x.shape
    tm = _row_tile(s)
    kdim = k_tiles * tk

    def body(x_ref, mv_ref, lhs_ref, w_ref, xo_ref, y_ref, *rest):
        if ffn:
            u_ref, acc = rest
        else:
            (acc,) = rest
        k = pl.program_id(1)

        @pl.when(k == 0)
        def _():
            acc[...] = jnp.zeros_like(acc)

        if ffn:
            ag = lhs_ref[0].astype(F32)
            au = lhs_ref[1].astype(F32)
            left = (ag * _sigmoid_fast(ag) * au).astype(BF16)
            u_ref[...] = left
        else:
            left = lhs_ref[...]
        acc[...] += _nn(left, w_ref[...])

        @pl.when(k == k_tiles - 1)
        def _():
            y = acc[...]
            y_ref[...] = y.astype(BF16)
            xo_ref[...] = x_ref[...] + (coef * mv_ref[3:4, :]) * y

    row = pl.BlockSpec((tm, d), lambda i, k: (i, 0))
    out_shapes = [jax.ShapeDtypeStruct((s, d), F32), jax.ShapeDtypeStruct((s, d), BF16)]
    out_specs = [row, row]
    if ffn:
        out_shapes.append(jax.ShapeDtypeStruct((s, kdim), BF16))
        out_specs.append(pl.BlockSpec((tm, tk), lambda i, k: (i, k)))
    return pl.pallas_call(
        body, name=name, grid=(s // tm, k_tiles),
        in_specs=[row, pl.BlockSpec((None, None, 8, d), lambda i, k: (*mv_idx, 0, 0)), lhs_spec, w_spec],
        out_specs=out_specs, out_shape=out_shapes,
        scratch_shapes=[pltpu.VMEM((tm, d), F32)],
        compiler_params=_params(2),
    )(x, mv, lhs, w)


def resmm_bwd(dxo, y, mv, mv_idx, coef, w, w_spec, k_tiles, tk, name, a=None):
    s, d = dxo.shape
    tm = _row_tile(s)
    kdim = k_tiles * tk
    ffn = a is not None

    def body(dxo_ref, y_ref, mv_ref, w_ref, *rest):
        if ffn:
            a_ref, dy_ref, dl_ref, dgate_ref, dy_scr = rest
        else:
            dy_ref, dl_ref, dgate_ref, dy_scr = rest
        i, k = pl.program_id(0), pl.program_id(1)

        @pl.when((i == 0) & (k == 0))
        def _():
            dgate_ref[...] = jnp.zeros_like(dgate_ref)

        @pl.when(k == 0)
        def _():
            dxv = dxo_ref[...]
            dy = ((coef * mv_ref[3:4, :]) * dxv).astype(BF16)
            dy_scr[...] = dy
            dy_ref[...] = dy
            dgate_ref[0:1, :] += jnp.sum(coef * dxv * y_ref[...].astype(F32), axis=0, keepdims=True)

        dl = _nt(dy_scr[...], w_ref[...])
        if ffn:
            ag = a_ref[0].astype(F32)
            au = a_ref[1].astype(F32)
            sg = _sigmoid_fast(ag)
            dl_ref[0] =(dl * au * (sg * (1.0 + ag * (1.0 - sg)))).astype(BF16)
            dl_ref[1] = (dl * (ag * sg)).astype(BF16)
        else:
            dl_ref[...] = dl.astype(BF16)

    row = pl.BlockSpec((tm, d), lambda i, k: (i, 0))
    in_specs = [row, row, pl.BlockSpec((None, None, 8, d), lambda i, k: (*mv_idx, 0, 0)), w_spec]
    ops = [dxo, y, mv, w]
    if ffn:
        in_specs.append(pl.BlockSpec((2, tm, tk), lambda i, k: (0, i, k)))
        ops.append(a)
        dl_shape = jax.ShapeDtypeStruct((2, s, kdim), BF16)
        dl_spec = pl.BlockSpec((2, tm, tk), lambda i, k: (0, i, k))
    else:
        dl_shape = jax.ShapeDtypeStruct((s, kdim), BF16)
        dl_spec = pl.BlockSpec((tm, tk), lambda i, k: (i, k))
    return pl.pallas_call(
        body, name=name, grid=(s // tm, k_tiles),
        in_specs=in_specs,
        out_specs=[row, dl_spec, pl.BlockSpec((8, d), lambda i, k: (0, 0))],
        out_shape=[jax.ShapeDtypeStruct((s, d), BF16), dl_shape, jax.ShapeDtypeStruct((8, d), F32)],
        scratch_shapes=[pltpu.VMEM((tm, d), BF16)],
        compiler_params=_params(2),
    )(*ops)


def modmm_bwd(dl, dl_spec, w, w_spec, n_tiles, x, dxo, mv, mv_idx, name, more=None):
    s, d = x.shape
    tm = _row_tile(s)

    def body(dl_ref, w_ref, x_ref, dxo_ref, mv_ref, *rest):
        if more is not None:
            dl2_ref, w2_ref, dx_ref, red_ref, acc = rest
        else:
            dx_ref, red_ref, acc = rest
        i, n = pl.program_id(0), pl.program_id(1)

        @pl.when((i == 0) & (n == 0))
        def _():
            red_ref[...] = jnp.zeros_like(red_ref)

        @pl.when(n == 0)
        def _():
            if more is not None:
                acc[...] = _nt(dl2_ref[...], w2_ref[...])
            else:
                acc[...] = jnp.zeros_like(acc)

        acc[...] += _nt(dl_ref[...], w_ref[...])

        @pl.when(n == n_tiles - 1)
        def _():
            dh = acc[...]
            _, xhat, xn, r, g, scale = _modulated(x_ref[...], mv_ref)
            dxn = dh * (1.0 + scale)
            red_ref[0:1, :] += jnp.sum(dxn * xhat, axis=0, keepdims=True)
            red_ref[1:2, :] += jnp.sum(dh, axis=0, keepdims=True)
            red_ref[2:3, :] += jnp.sum(dh * xn, axis=0, keepdims=True)
            gd = dxn * g
            dx_ref[...] = dxo_ref[...] + r * (gd - xhat * jnp.mean(gd * xhat, axis=-1, keepdims=True))

    row = pl.BlockSpec((tm, d), lambda i, n: (i, 0))
    in_specs = [dl_spec, w_spec, row, row, pl.BlockSpec((None, None, 8, d), lambda i, n: (*mv_idx, 0, 0))]
    ops = [dl, w, x, dxo, mv]
    if more is not None:
        in_specs += [more[1], more[2]]
        ops += [more[0], w]
    return pl.pallas_call(
        body, name=name, grid=(s // tm, n_tiles),
        in_specs=in_specs,
        out_specs=[row, pl.BlockSpec((8, d), lambda i, n: (0, 0))],
        out_shape=[jax.ShapeDtypeStruct((s, d), F32), jax.ShapeDtypeStruct((8, d), F32)],
        scratch_shapes=[pltpu.VMEM((tm, d), F32)],
        compiler_params=_params(2),
    )(*ops)


def weight_grad(a, a_spec, b, b_spec, grid_mn, s, bm, bn, dest, out_spec, name):
    tk = _wide_tile(s)
    k_tiles = s // tk

    def body(a_ref, b_ref, dest_ref, out_ref, acc):
        k = pl.program_id(2)

        @pl.when(k == 0)
        def _():
            acc[...] = jnp.zeros_like(acc)

        acc[...] += _tn(a_ref[...], b_ref[...])

        @pl.when(k == k_tiles - 1)
        def _():
            out_ref[...] = acc[...].astype(out_ref.dtype)

    return pl.pallas_call(
        body, name=name, grid=(*grid_mn, k_tiles),
        in_specs=[a_spec, b_spec, ANY], out_specs=out_spec,
        out_shape=jax.ShapeDtypeStruct(dest.shape, dest.dtype),
        input_output_aliases={2: 0},
        scratch_shapes=[pltpu.VMEM((bm, bn), F32)],
        compiler_params=_params(3),
    )(a, b, dest)


def _head_mean(v):
    lane = lax.broadcasted_iota(jnp.int32, v.shape, 1)
    lo = jnp.sum(jnp.where(lane < HEAD_DIM, v, 0.0), axis=-1, keepdims=True)
    hi = jnp.sum(v, axis=-1, keepdims=True) - lo
    return jnp.where(lane < HEAD_DIM, lo, hi) * (1.0 / HEAD_DIM)


def qknorm_fwd(proj, gains, width, name):
    s = proj.shape[0]
    nqk = gains.shape[1]
    tm = _row_tile(s)

    def body(p_ref, g_ref, o_ref):
        for cc in range(width // LANES):
            sl = slice(cc * LANES, (cc + 1) * LANES)
            xv = p_ref[:, sl].astype(F32)
            r = lax.rsqrt(_head_mean(xv * xv) + EPS)
            o_ref[:, sl] = (xv * r * g_ref[:, sl]).astype(BF16)

    blk = pl.BlockSpec((tm, width), lambda i, c: (i, c))
    return pl.pallas_call(
        body, name=name, grid=(s // tm, nqk // width),
        in_specs=[blk, pl.BlockSpec((1, width), lambda i, c: (0, c))],
        out_specs=blk, out_shape=jax.ShapeDtypeStruct((s, nqk), BF16),
        compiler_params=_params(2),
    )(proj, gains)


def qknorm_bwd(proj, gains, d, d_spec, n_cols, width, name):
    s = proj.shape[0]
    nqk = gains.shape[1] // width
    n_blocks = n_cols // width
    tm = _row_tile(s)

    def body(p_ref, g_ref, d_ref, o_ref, dg_ref):
        c, i = pl.program_id(0), pl.program_id(1)

        @pl.when(i == 0)
        def _():
            dg_ref[...] = jnp.zeros_like(dg_ref)

        @pl.when(c < nqk)
        def _():
            for cc in range(width // LANES):
                sl = slice(cc * LANES, (cc + 1) * LANES)
                xv = p_ref[:, sl].astype(F32)
                r = lax.rsqrt(_head_mean(xv * xv) + EPS)
                xhat = xv * r
                dv = d_ref[:, sl]
                gd = dv * g_ref[:, sl]
                o_ref[:, sl] = (r * (gd - xhat * _head_mean(gd * xhat))).astype(BF16)
                dg_ref[0:1, sl] += jnp.sum(dv * xhat, axis=0, keepdims=True)

        @pl.when(c >= nqk)
        def _():
            o_ref[...] = d_ref[...].astype(BF16)

    return pl.pallas_call(
        body, name=name, grid=(n_blocks, s // tm),
        in_specs=[pl.BlockSpec((tm, width), lambda c, i: (i, c)),
                  pl.BlockSpec((1, width), lambda c, i: (0, jnp.minimum(c, nqk - 1))), d_spec],
        out_specs=[pl.BlockSpec((tm, width), lambda c, i: (i, c)), pl.BlockSpec((8, width), lambda c, i: (0, c))],
        out_shape=[jax.ShapeDtypeStruct((s, n_cols), BF16), jax.ShapeDtypeStruct((8, n_cols), F32)],
        compiler_params=_params(2),
    )(proj, gains, d)


def _swa_mask(first):
    qi = lax.broadcasted_iota(jnp.int32, (BLOCK, 2 * BLOCK), 0) + BLOCK
    kj = lax.broadcasted_iota(jnp.int32, (BLOCK, 2 * BLOCK), 1)
    dist = qi - kj
    return (dist >= 0) & (dist < BLOCK) & ((kj >= BLOCK) | jnp.logical_not(first))


def swa_fwd(qkn, proj, bias, sink, d, name):
    s = qkn.shape[0]
    hq = d // HEAD_DIM
    hkv = hq // GROUP
    kw = hkv * HEAD_DIM
    nblk = s // BLOCK
    kcol = d // kw

    def body(q_ref, kc_ref, kp_ref, vc_ref, vp_ref, bias_ref, sink_ref, o_ref, lse_ref):
        mask = _swa_mask(pl.program_id(0) == 0)
        lse_ref[...] = jnp.zeros_like(lse_ref)
        for kvh in range(hkv):
            cols = slice(kvh * HEAD_DIM, (kvh + 1) * HEAD_DIM)
            k2 = jnp.concatenate([kp_ref[:, cols], kc_ref[:, cols]], axis=0)
            v2 = jnp.concatenate([vp_ref[:, cols], vc_ref[:, cols]], axis=0)
            for g in range(GROUP):
                h = kvh * GROUP + g
                hc = slice(h * HEAD_DIM, (h + 1) * HEAD_DIM)
                sc = jnp.where(mask, _nt(q_ref[:, hc], k2) + bias_ref[h], NEG)
                sk = sink_ref[0, h]
                m = jnp.maximum(jnp.max(sc, axis=-1, keepdims=True), sk)
                p = jnp.exp(sc - m)
                denom = jnp.sum(p, axis=-1, keepdims=True) + jnp.exp(sk - m)
                o_ref[:, hc] = (_nn(p.astype(BF16), v2) / denom).astype(BF16)
                lse_ref[:, h:h + 1] = m + jnp.log(denom)

    prev = lambda i: jnp.maximum(i - 1, 0)
    return pl.pallas_call(
        body, name=name, grid=(nblk,),
        in_specs=[pl.BlockSpec((BLOCK, d), lambda i: (i, 0)),
                  pl.BlockSpec((BLOCK, kw), lambda i: (i, kcol)),
                  pl.BlockSpec((BLOCK, kw), lambda i: (prev(i), kcol)),
                  pl.BlockSpec((BLOCK, kw), lambda i: (i, kcol + 1)),
                  pl.BlockSpec((BLOCK, kw), lambda i: (prev(i), kcol + 1)),
                  pl.BlockSpec((hq, BLOCK, 2 * BLOCK), lambda i: (0, 0, 0)),
                  pl.BlockSpec(memory_space=pltpu.SMEM)],
        out_specs=[pl.BlockSpec((BLOCK, d), lambda i: (i, 0)), pl.BlockSpec((BLOCK, LANES), lambda i: (i, 0))],
        out_shape=[jax.ShapeDtypeStruct((s, d), BF16), jax.ShapeDtypeStruct((s, LANES), F32)],
        compiler_params=_params(1),
    )(qkn, qkn, qkn, proj, proj, bias, sink)


def swa_bwd(qkn, proj, bias, sink, do, o, lse, d, name):
    s = qkn.shape[0]
    hq = d // HEAD_DIM
    hkv = hq // GROUP
    kw = hkv * HEAD_DIM
    nblk = s // BLOCK
    kcol = d // kw
    wide = d + 2 * kw

    def body(q_ref, kc_ref, kp_ref, vc_ref, vp_ref, bias_ref, sink_ref, do_ref, o_ref, lse_ref,
             out_ref, dbias_ref, dsink_ref, carry, fresh):
        i = pl.program_id(0)

        @pl.when(i == 0)
        def _():
            dbias_ref[...] = jnp.zeros_like(dbias_ref)
            dsink_ref[...] = jnp.zeros_like(dsink_ref)
            carry[...] = jnp.zeros_like(carry)

        @pl.when(i == nblk)
        def _():
            fresh[...] = jnp.zeros_like(fresh)

        @pl.when(i < nblk)
        def _():
            mask = _swa_mask(i == 0)
            for kvh in range(hkv):
                cols = slice(kvh * HEAD_DIM, (kvh + 1) * HEAD_DIM)
                k2 = jnp.concatenate([kp_ref[:, cols], kc_ref[:, cols]], axis=0)
                v2 = jnp.concatenate([vp_ref[:, cols], vc_ref[:, cols]], axis=0)
                dk2 = jnp.zeros((2 * BLOCK, HEAD_DIM), F32)
                dv2 = jnp.zeros((2 * BLOCK, HEAD_DIM), F32)
                for g in range(GROUP):
                    h = kvh * GROUP + g
                    hc = slice(h * HEAD_DIM, (h + 1) * HEAD_DIM)
                    q = q_ref[:, hc]
                    dov = do_ref[:, hc]
                    lse_h = lse_ref[:, h:h + 1]
                    sc = jnp.where(mask, _nt(q, k2) + bias_ref[h], NEG)
                    p = jnp.exp(sc - lse_h)
                    delta = jnp.sum(dov.astype(F32) * o_ref[:, hc].astype(F32), axis=-1, keepdims=True)
                    ds = p * (_nt(dov, v2) - delta)
                    dbias_ref[h] += ds
                    dsink_ref[0:1, h:h + 1] += jnp.sum(-jnp.exp(sink_ref[0, h] - lse_h) * delta, axis=0, keepdims=True)
                    dsb = ds.astype(BF16)
                    fresh[0, :, hc] = _nn(dsb, k2)
                    dk2 += _tn(dsb, q)
                    dv2 += _tn(p.astype(BF16), dov)
                kc_cols = slice(d + kvh * HEAD_DIM, d + (kvh + 1) * HEAD_DIM)
                vc_cols = slice(d + kw + kvh * HEAD_DIM, d + kw + (kvh + 1) * HEAD_DIM)
                fresh[0, :, kc_cols] = dk2[BLOCK:]
                fresh[0, :, vc_cols] = dv2[BLOCK:]
                fresh[1, :, kc_cols] = dk2[:BLOCK]
                fresh[1, :, vc_cols] = dv2[:BLOCK]

        lane = lax.broadcasted_iota(jnp.int32, (BLOCK, wide), 1)
        out_ref[...] = carry[...] + jnp.where(lane >= d, fresh[1], 0.0)

        @pl.when(i < nblk)
        def _():
            carry[...] = fresh[0]

    cur = lambda i: jnp.minimum(i, nblk - 1)
    prev = lambda i: jnp.maximum(jnp.minimum(i, nblk - 1) - 1, 0)
    return pl.pallas_call(
        body, name=name, grid=(nblk + 1,),
        in_specs=[pl.BlockSpec((BLOCK, d), lambda i: (cur(i), 0)),
                  pl.BlockSpec((BLOCK, kw), lambda i: (cur(i), kcol)),
                  pl.BlockSpec((BLOCK, kw), lambda i: (prev(i), kcol)),
                  pl.BlockSpec((BLOCK, kw), lambda i: (cur(i), kcol + 1)),
                  pl.BlockSpec((BLOCK, kw), lambda i: (prev(i), kcol + 1)),
                  pl.BlockSpec((hq, BLOCK, 2 * BLOCK), lambda i: (0, 0, 0)),
                  pl.BlockSpec(memory_space=pltpu.SMEM),
                  pl.BlockSpec((BLOCK, d), lambda i: (cur(i), 0)),
                  pl.BlockSpec((BLOCK, d), lambda i: (cur(i), 0)),
                  pl.BlockSpec((BLOCK, LANES), lambda i: (cur(i), 0))],
        out_specs=[pl.BlockSpec((BLOCK, wide), lambda i: (jnp.maximum(i - 1, 0), 0)),
                   pl.BlockSpec((hq, BLOCK, 2 * BLOCK), lambda i: (0, 0, 0)),
                   pl.BlockSpec((8, LANES), lambda i: (0, 0))],
        out_shape=[jax.ShapeDtypeStruct((s, wide), F32), jax.ShapeDtypeStruct((hq, BLOCK, 2 * BLOCK), F32),
                   jax.ShapeDtypeStruct((8, LANES), F32)],
        scratch_shapes=[pltpu.VMEM((BLOCK, wide), F32), pltpu.VMEM((2, BLOCK, wide), F32)],
        compiler_params=_params(1),
    )(qkn, qkn, qkn, proj, proj, bias, sink, do, o, lse)


def _rel_bucket_table():
    qi = np.arange(BLOCK)[:, None] + BLOCK
    kj = np.arange(2 * BLOCK)[None, :]
    n = np.maximum(qi - kj, 0)
    max_exact = REL_BUCKETS // 2
    nf = np.maximum(n, 1).astype(np.float32)
    large = max_exact + (np.log(nf / max_exact) / math.log(REL_MAX_DIST / max_exact)
                         * (REL_BUCKETS - max_exact)).astype(np.int32)
    large = np.minimum(large, REL_BUCKETS - 1)
    return np.where(n < max_exact, n, large).astype(np.int32)


def rel_bias_table(rel_bias, bucket):
    hq = rel_bias.shape[1]

    def body(rb_ref, bucket_ref, out_ref):
        tbl = bucket_ref[...]

        def per_head(h, carry):
            def per_bucket(b, acc):
                return jnp.where(tbl == b, rb_ref[b, h], acc)

            out_ref[h] = lax.fori_loop(0, REL_BUCKETS, per_bucket, jnp.zeros(tbl.shape, F32))
            return carry

        lax.fori_loop(0, hq, per_head, 0)

    return pl.pallas_call(
        body, name="rel_bias_table",
        in_specs=[pl.BlockSpec(memory_space=pltpu.SMEM), pl.BlockSpec(memory_space=pltpu.VMEM)],
        out_specs=pl.BlockSpec(memory_space=pltpu.VMEM),
        out_shape=jax.ShapeDtypeStruct((hq,) + tuple(bucket.shape), F32),
    )(rel_bias, bucket)


def rel_bias_grad(dbias, bucket):
    n_layers, hq = dbias.shape[:2]

    def body(db_ref, bucket_ref, out_ref):
        tbl = bucket_ref[...]

        def per_head(h, carry):
            dsum = db_ref[0, h]
            for a in range(1, n_layers):
                dsum = dsum + db_ref[a, h]

            def per_bucket(b, carry2):
                out_ref[b, h] = jnp.sum(jnp.where(tbl == b, dsum, 0.0))
                return carry2

            return lax.fori_loop(0, REL_BUCKETS, per_bucket, carry)

        lax.fori_loop(0, hq, per_head, 0)

    return pl.pallas_call(
        body, name="rel_bias_grad",
        in_specs=[pl.BlockSpec(memory_space=pltpu.VMEM), pl.BlockSpec(memory_space=pltpu.VMEM)],
        out_specs=pl.BlockSpec(memory_space=pltpu.SMEM),
        out_shape=jax.ShapeDtypeStruct((REL_BUCKETS, hq), F32),
    )(dbias, bucket)


def _split3(v):
    hi = v.astype(BF16)
    r1 = v - hi.astype(F32)
    mid = r1.astype(BF16)
    lo = (r1 - mid.astype(F32)).astype(BF16)
    return hi, mid, lo


def _tri_sum(tri, v):
    hi, mid, lo = _split3(v)
    return _nn(tri, hi) + _nn(tri, mid) + _nn(tri, lo)


def fox_gates(fl, b_f, name):
    s = fl.shape[0]
    t = _row_tile(s)

    def body(fl_ref, b_ref, f_ref, carry):
        @pl.when(pl.program_id(0) == 0)
        def _():
            carry[...] = jnp.zeros_like(carry)

        z = fl_ref[...] + b_ref[...]
        logf = jnp.minimum(z, 0.0) - jnp.log(1.0 + jnp.exp(-jnp.abs(z)))
        r = lax.broadcasted_iota(jnp.int32, (t, t), 0)
        cidx = lax.broadcasted_iota(jnp.int32, (t, t), 1)
        tri = jnp.where(cidx <= r, 1.0, 0.0).astype(BF16)
        f = _tri_sum(tri, logf) + carry[0:1, :]
        f_ref[...] = f
        carry[0:1, :] = f_ref[t - 1:t, :]

    blk = pl.BlockSpec((t, LANES), lambda i: (i, 0))
    return pl.pallas_call(
        body, name=name, grid=(s // t,),
        in_specs=[blk, pl.BlockSpec((1, LANES), lambda i: (0, 0))],
        out_specs=blk, out_shape=jax.ShapeDtypeStruct((s, LANES), F32),
        scratch_shapes=[pltpu.VMEM((8, LANES), F32)],
        compiler_params=_params(1),
    )(fl, b_f)


def fox_gates_bwd(fl, b_f, df_query, df_key, name):
    s = fl.shape[0]
    t = _row_tile(s)
    nb = s // t

    def body(fl_ref, b_ref, dfq_ref, dfk_ref, dfl_ref, db_ref, carry):
        @pl.when(pl.program_id(0) == 0)
        def _():
            carry[...] = jnp.zeros_like(carry)
            db_ref[...] = jnp.zeros_like(db_ref)

        dfv = dfq_ref[...] + dfk_ref[...]
        r = lax.broadcasted_iota(jnp.int32, (t, t), 0)
        cidx = lax.broadcasted_iota(jnp.int32, (t, t), 1)
        tri = jnp.where(cidx >= r, 1.0, 0.0).astype(BF16)
        dlog = _tri_sum(tri, dfv) + carry[0:1, :]
        carry[0:1, :] += jnp.sum(dfv, axis=0, keepdims=True)
        z = fl_ref[...] + b_ref[...]
        dz = dlog * (1.0 - _sigmoid(z))
        dfl_ref[...] = dz.astype(BF16)
        db_ref[0:1, :] += jnp.sum(dz, axis=0, keepdims=True)

    rev = pl.BlockSpec((t, LANES), lambda i: (nb - 1 - i, 0))
    return pl.pallas_call(
        body, name=name, grid=(nb,),
        in_specs=[rev, pl.BlockSpec((1, LANES), lambda i: (0, 0)), rev, rev],
        out_specs=[rev, pl.BlockSpec((8, LANES), lambda i: (0, 0))],
        out_shape=[jax.ShapeDtypeStruct((s, LANES), BF16), jax.ShapeDtypeStruct((8, LANES), F32)],
        scratch_shapes=[pltpu.VMEM((8, LANES), F32)],
        compiler_params=_params(1),
    )(fl, b_f, df_query, df_key)


def fox_fwd(qkn, proj, f_col, f_row, d, name):
    s = qkn.shape[0]
    t = _attn_tile(s)
    n_pairs = d // LANES
    nt = s // t

    def body(q_ref, k_ref, v_ref, fq_ref, fk_ref, o_ref, o32_ref, lse_ref, m_scr, l_scr, acc):
        i, j = pl.program_id(1), pl.program_id(2)

        @pl.when(j == 0)
        def _():
            m_scr[...] = jnp.full_like(m_scr, NEG)
            l_scr[...] = jnp.zeros_like(l_scr)
            acc[...] = jnp.zeros_like(acc)

        @pl.when(j <= i)
        def _():
            krow = lax.broadcasted_iota(jnp.int32, (t, t), 0)
            qcol = lax.broadcasted_iota(jnp.int32, (t, t), 1)
            visible = (krow <= qcol) | (j < i)
            for hh in range(2):
                hc = slice(hh * HEAD_DIM, (hh + 1) * HEAD_DIM)
                st = _nt(k_ref[:, hc], q_ref[:, hc]) + fq_ref[hh] - fk_ref[hh]
                st = jnp.where(visible, st, NEG)
                m_prev = m_scr[hh]
                m_new = jnp.maximum(m_prev, jnp.max(st, axis=0, keepdims=True))
                alpha = jnp.exp(m_prev - m_new)
                pt = jnp.exp(st - m_new)
                l_scr[hh] = alpha * l_scr[hh] + jnp.sum(pt, axis=0, keepdims=True)
                acc[hc, :] = alpha * acc[hc, :] + _tn(v_ref[:, hc], pt.astype(BF16))
                m_scr[hh] = m_new

        @pl.when(j == i)
        def _():
            l_full = jnp.concatenate([jnp.broadcast_to(l_scr[hh], (HEAD_DIM, t)) for hh in range(2)], axis=0)
            ov = (acc[...] / l_full).T
            o_ref[...] = ov.astype(BF16)
            o32_ref[...] = ov
            lse_ref[...] = m_scr[...] + jnp.log(l_scr[...])

    kv = lambda j, i: jnp.minimum(j, i)
    return pl.pallas_call(
        body, name=name, grid=(n_pairs, nt, nt),
        in_specs=[pl.BlockSpec((t, LANES), lambda p, i, j: (i, p)),
                  pl.BlockSpec((t, LANES), lambda p, i, j: (kv(j, i), n_pairs + p)),
                  pl.BlockSpec((t, LANES), lambda p, i, j: (kv(j, i), 2 * n_pairs + p)),
                  pl.BlockSpec((2, 1, t), lambda p, i, j: (p, 0, i)),
                  pl.BlockSpec((2, t, 1), lambda p, i, j: (p, kv(j, i), 0))],
        out_specs=[pl.BlockSpec((t, LANES), lambda p, i, j: (i, p)),
                   pl.BlockSpec((t, LANES), lambda p, i, j: (i, p)),
                   pl.BlockSpec((2, 1, t), lambda p, i, j: (p, 0, i))],
        out_shape=[jax.ShapeDtypeStruct((s, d), BF16), jax.ShapeDtypeStruct((s, d), F32),
                   jax.ShapeDtypeStruct((2 * n_pairs, 1, s), F32)],
        scratch_shapes=[pltpu.VMEM((2, 1, t), F32), pltpu.VMEM((2, 1, t), F32), pltpu.VMEM((2 * HEAD_DIM, t), F32)],
        compiler_params=_params(3),
    )(qkn, qkn, proj, f_row, f_col)


def fox_bwd(qkn, proj, f_col, f_row, lse_row, do, o, d, name):
    s = qkn.shape[0]
    t = _attn_tile(s)
    n_pairs = d // LANES
    nt = s // t

    def body(q_ref, k_ref, v_ref, fk_ref, fq_ref, lse_ref, do_ref, o_ref, out_ref, df_ref, dfq_ref,
             dq_acc, dkv_acc, df_acc, dfq_acc):
        j, i = pl.program_id(1), pl.program_id(2)

        @pl.when((j == 0) & (i == 0))
        def _():
            dq_acc[...] = jnp.zeros_like(dq_acc)
            dfq_acc[...] = jnp.zeros_like(dfq_acc)

        @pl.when(i == 0)
        def _():
            dkv_acc[...] = jnp.zeros_like(dkv_acc)
            df_acc[...] = jnp.zeros_like(df_acc)

        @pl.when(i >= j)
        def _():
            krow = lax.broadcasted_iota(jnp.int32, (t, t), 0)
            qcol = lax.broadcasted_iota(jnp.int32, (t, t), 1)
            visible = (krow <= qcol) | (i > j)
            ones = jnp.ones((8, HEAD_DIM), BF16)
            for hh in range(2):
                hc = slice(hh * HEAD_DIM, (hh + 1) * HEAD_DIM)
                q, k, v, dov = q_ref[:, hc], k_ref[:, hc], v_ref[:, hc], do_ref[:, hc]
                st = _nt(k, q) + fq_ref[hh] - fk_ref[hh]
                pt = jnp.exp(jnp.where(visible, st, NEG) - lse_ref[hh])
                hi, mid, lo = _split3(dov.astype(F32) * o_ref[:, hc])
                delta = jnp.max(_nt(ones, hi) + _nt(ones, mid) + _nt(ones, lo), axis=0, keepdims=True)
                dst = pt * (_nt(v, dov) - delta)
                dsb = dst.astype(BF16)
                dkv_acc[1, :, hc] += _nn(pt.astype(BF16), dov)
                dkv_acc[0, :, hc] += _nn(dsb, q)
                dq_acc[pl.ds(pl.multiple_of(i * t, t), t), hc] += _tn(dsb, k)
                df_acc[hh] -= jnp.sum(dst, axis=-1, keepdims=True)
                dfq_acc[i, hh] += jnp.sum(dst, axis=0, keepdims=True)

        @pl.when(i == nt - 1)
        def _():
            out_ref[0] = dq_acc[pl.ds(pl.multiple_of(j * t, t), t), :]
            out_ref[1] = dkv_acc[0]
            out_ref[2] = dkv_acc[1]
            df_ref[...] = df_acc[...]
            dfq_ref[...] = dfq_acc[j]

    qi = lambda j, i: jnp.maximum(i, j)
    return pl.pallas_call(
        body, name=name, grid=(n_pairs, nt, nt),
        in_specs=[pl.BlockSpec((t, LANES), lambda p, j, i: (qi(j, i), p)),
                  pl.BlockSpec((t, LANES), lambda p, j, i: (j, n_pairs + p)),
                  pl.BlockSpec((t, LANES), lambda p, j, i: (j, 2 * n_pairs + p)),
                  pl.BlockSpec((2, t, 1), lambda p, j, i: (p, j, 0)),
                  pl.BlockSpec((2, 1, t), lambda p, j, i: (p, 0, qi(j, i))),
                  pl.BlockSpec((2, 1, t), lambda p, j, i: (p, 0, qi(j, i))),
                  pl.BlockSpec((t, LANES), lambda p, j, i: (qi(j, i), p)),
                  pl.BlockSpec((t, LANES), lambda p, j, i: (qi(j, i), p))],
        out_specs=[pl.BlockSpec((3, t, LANES), lambda p, j, i: (0, j, p)),
                   pl.BlockSpec((2, t, 1), lambda p, j, i: (p, j, 0)),
                   pl.BlockSpec((2, 1, t), lambda p, j, i: (p, 0, j))],
        out_shape=[jax.ShapeDtypeStruct((3, s, d), F32), jax.ShapeDtypeStruct((2 * n_pairs, s, 1), F32),
                   jax.ShapeDtypeStruct((2 * n_pairs, 1, s), F32)],
        scratch_shapes=[pltpu.VMEM((s, LANES), F32), pltpu.VMEM((2, t, LANES), F32), pltpu.VMEM((2, t, 1), F32),
                        pltpu.VMEM((nt, 2, 1, t), F32)],
        compiler_params=_params(3),
    )(qkn, qkn, proj, f_col, f_row, lse_row, do, o)


def loss_head(y, target):
    s, d = y.shape
    tm = _row_tile(s)

    def body(y_ref, t_ref, dy_ref, loss_ref):
        @pl.when(pl.program_id(0) == 0)
        def _():
            loss_ref[...] = jnp.zeros_like(loss_ref)

        diff = y_ref[...] - t_ref[...]
        dy_ref[...] = diff * (1.0 / d)
        loss_ref[...] += 0.5 * jnp.sum(jnp.mean(diff * diff, axis=-1, keepdims=True), axis=0, keepdims=True)

    row = pl.BlockSpec((tm, d), lambda i: (i, 0))
    return pl.pallas_call(
        body, name="loss_head", grid=(s // tm,),
        in_specs=[row, row],
        out_specs=[row, pl.BlockSpec((8, LANES), lambda i: (0, 0))],
        out_shape=[jax.ShapeDtypeStruct((s, d), F32), jax.ShapeDtypeStruct((8, LANES), F32)],
        compiler_params=_params(1),
    )(y, target)


def ada_mod(c_all, w, b):
    n_layers, d, cols = w.shape

    def body(c_ref, w_ref, b_ref, o_ref):
        cv = c_ref[...]
        o_ref[...] = _nn(cv * _sigmoid(cv), w_ref[...]) + b_ref[...]

    return pl.pallas_call(
        body, name="ada_mod", grid=(n_layers,),
        in_specs=[pl.BlockSpec((N_DEV, d), lambda l: (0, 0)), pl.BlockSpec((None, d, cols), lambda l: (l, 0, 0)),
                  pl.BlockSpec((None, 1, cols), lambda l: (l, 0, 0))],
        out_specs=pl.BlockSpec((None, N_DEV, cols), lambda l: (l, 0, 0)),
        out_shape=jax.ShapeDtypeStruct((n_layers, N_DEV, cols), F32),
        compiler_params=_params(1),
    )(c_all, w, b)


def ada_grad(c_t, dmod):
    d = c_t.shape[0]
    n_layers, _, cols = dmod.shape
    tn = cols // 2

    def body(c_ref, dm_ref, o_ref):
        cv = c_ref[...]
        o_ref[...] = _nn(cv * _sigmoid(cv), dm_ref[...])

    return pl.pallas_call(
        body, name="ada_grad", grid=(n_layers, 2),
        in_specs=[pl.BlockSpec((d, N_DEV), lambda l, n: (0, 0)), pl.BlockSpec((None, N_DEV, tn), lambda l, n: (l, 0, n))],
        out_specs=pl.BlockSpec((None, d, tn), lambda l, n: (l, 0, n)),
        out_shape=jax.ShapeDtypeStruct((n_layers, d, cols), F32),
        compiler_params=_params(2),
    )(c_t, dmod)


def sum_devices(v):
    def body(v_ref, o_ref):
        acc = v_ref[0]
        for k in range(1, N_DEV):
            acc = acc + v_ref[k]
        o_ref[...] = acc

    return pl.pallas_call(body, name="sum_devices", out_shape=jax.ShapeDtypeStruct(v.shape[1:], F32))(v)


def sum_slots(r):
    _, rows, cols = r.shape
    tm = 256 if rows % 256 == 0 else rows

    def body(r_ref, o_ref):
        o_ref[...] = ((r_ref[3].astype(F32) + r_ref[0].astype(F32)) + r_ref[1].astype(F32)) + r_ref[2].astype(F32)

    return pl.pallas_call(
        body, name="sum_slots", grid=(rows // tm,),
        in_specs=[pl.BlockSpec((N_CHIP, tm, cols), lambda i: (0, i, 0))],
        out_specs=pl.BlockSpec((tm, cols), lambda i: (i, 0)),
        out_shape=jax.ShapeDtypeStruct((rows, cols), F32),
        compiler_params=_params(1),
    )(r)


def adamw(w, m, v, g, g2=None):
    rows, cols = w.shape
    tm = 256 if rows % 256 == 0 else rows
    two = g2 is not None
    c1 = 1.0 - ADAM_B1 ** ADAM_STEP
    c2 = 1.0 - ADAM_B2 ** ADAM_STEP

    def body(w_ref, m_ref, v_ref, g_ref, *rest):
        if two:
            g2_ref, go_ref, d_ref, mo_ref, vo_ref = rest
            gv = g_ref[...] + g2_ref[...]
        else:
            go_ref, d_ref, mo_ref, vo_ref = rest
            gv = g_ref[...]
        mn = ADAM_B1 * m_ref[...] + (1.0 - ADAM_B1) * gv
        vn = ADAM_B2 * v_ref[...] + (1.0 - ADAM_B2) * (gv * gv)
        go_ref[...] = gv
        mo_ref[...] = mn
        vo_ref[...] = vn
        d_ref[...] = -ADAM_LR * ((mn / c1) / (jnp.sqrt(vn / c2) + ADAM_EPS) + ADAM_WD * w_ref[...])

    blk = pl.BlockSpec((tm, cols), lambda i: (i, 0))
    ops = [w, m, v, g] + ([g2] if two else [])
    return pl.pallas_call(
        body, name="adamw", grid=(rows // tm,),
        in_specs=[blk] * len(ops), out_specs=[blk] * 4,
        out_shape=[jax.ShapeDtypeStruct((rows, cols), F32)] * 4,
        compiler_params=_params(1),
    )(*ops)


def _pad_rows(flat):
    n = flat.shape[0]
    rows = -(-n // LANES)
    return jnp.pad(flat, (0, rows * LANES - n)).reshape(rows, LANES)


def _pad_rows8(flat):
    rows = _pad_rows(flat)
    return jnp.pad(rows, ((0, -rows.shape[0] % 8), (0, 0)))


def _col_tiles(n):
    return next(k for k in range(1, n // LANES + 1) if n % (k * LANES) == 0 and n // k <= 1536)


def kernel(x, c, ada_w, ada_b, norm_g, ffn_w13, ffn_w2, rel_bias, swa_w_in, swa_w_out, swa_q_g, swa_k_g, swa_sink, fox_w_in, fox_w_out, fox_b_f, fox_q_g, fox_k_g, loss_target, m_ada_w, m_ada_b, m_norm_g, m_ffn_w13, m_ffn_w2, m_rel_bias, m_swa_w_in, m_swa_w_out, m_swa_q_g, m_swa_k_g, m_swa_sink, m_fox_w_in, m_fox_w_out, m_fox_b_f, m_fox_q_g, m_fox_k_g, v_ada_w, v_ada_b, v_norm_g, v_ffn_w13, v_ffn_w2, v_rel_bias, v_swa_w_in, v_swa_w_out, v_swa_q_g, v_swa_k_g, v_swa_sink, v_fox_w_in, v_fox_w_out, v_fox_b_f, v_fox_q_g, v_fox_k_g):
    ix, iy, ic = lax.axis_index("x"), lax.axis_index("y"), lax.axis_index("c")
    chip = 2 * ix + iy
    dev = 2 * chip + ic
    s, d = x.shape[1:]
    n_layers = ada_w.shape[0]
    n_a, n_b = swa_w_in.shape[0], fox_w_in.shape[0]
    hq = d // HEAD_DIM
    hkv = hq // GROUP
    kw = hkv * HEAD_DIM
    c13 = ffn_w13.shape[-1]
    f = 2 * c13
    r2 = ffn_w2.shape[2]
    cq = norm_g.shape[-1]
    a_in = d + 2 * kw
    fx = fox_w_in.shape[-1]
    b_in = N_CHIP * fx
    b_pad = 3 * d + LANES
    x0 = x[0]

    hello = _pad_rows8(jnp.concatenate([c.reshape(-1), norm_g.reshape(-1)]))
    hello_all = all_gather_rows(hello, "gather_c_norm").reshape(N_DEV, -1)
    c_all = hello_all[:, :d]
    ng = hello_all[::2, d:d + n_layers * 3 * cq].reshape(N_CHIP, n_layers, 3, cq)
    norm_full = jnp.moveaxis(ng, 0, 2).reshape(n_layers, 3, d)

    half_cols = ada_w.shape[-1] // 2
    w_half = lax.dynamic_slice_in_dim(ada_w, ic * half_cols, half_cols, axis=2)
    b_half = lax.dynamic_slice_in_dim(ada_b, dev * half_cols, half_cols, axis=1)[:, None, :]
    mod_part = ada_mod(c_all, w_half, b_half)
    mod_all = all_gather_rows(mod_part.reshape(n_layers * N_DEV, half_cols), "gather_mod")
    mod_all = mod_all.reshape(N_DEV, n_layers, N_DEV, half_cols)
    mod_mine = lax.dynamic_index_in_dim(mod_all, dev, axis=2, keepdims=False)
    mod_mine = jnp.moveaxis(mod_mine, 0, 1).reshape(n_layers, 3, 3, d)
    mv = jnp.concatenate([norm_full[:, :, None, :], mod_mine, jnp.zeros((n_layers, 3, 4, d), F32)], axis=2)

    n_groups = n_layers // 2
    kinds = [("col", c13), ("row", r2), ("col", swa_w_in.shape[-1]), ("slot",), ("slot",), ("slot",)]

    def group_shards(g, barrier_with=None):
        raw = [ffn_w13[2 * g:2 * g + 2], ffn_w2[2 * g:2 * g + 2], swa_w_in[g], swa_w_out[g], fox_w_in[g], fox_w_out[g]]
        if barrier_with is not None:
            raw = lax.optimization_barrier((barrier_with, raw))[1]
        return [r.astype(BF16) for r in raw]

    def group_weights(fulls):
        w13_g, w2_g, wa_in_g, wa_out_g, wb_slots, wb_out_g = fulls
        wb_in_g = jnp.pad(jnp.concatenate([wb_slots[b] for b in range(N_CHIP)], axis=-1), ((0, 0), (0, b_pad - b_in)))
        return dict(w13=w13_g, w2=w2_g, wa_in=wa_in_g, wa_out=wa_out_g.reshape(d, d), wb_in=wb_in_g,
                    wb_out=wb_out_g.reshape(d, d))

    weights = [group_weights(gather_weights(group_shards(0), kinds))]
    in_flight = []
    for g in range(1, n_groups):
        shards = group_shards(g, barrier_with=weights[0]["w13"])
        fulls = [lax.empty(_full_shape(v.shape, k), BF16) for v, k in zip(shards, kinds)]
        fulls = place_shards(shards, fulls, kinds, True, f"place_weights_{g}")
        in_flight.append(exchange_start(shards, fulls, kinds, True, f"gather_start_{g}"))
    for *_, token in in_flight:
        mv = mv + token[0, 0]

    bucket = jnp.asarray(_rel_bucket_table())
    bias = rel_bias_table(rel_bias, bucket)
    tm, tw = _row_tile(s), _wide_tile(s)
    n13 = 2 * f // c13
    na_t, nb_t = _col_tiles(a_in), _col_tiles(3 * d)
    wa_t, wb_t = a_in // na_t, 3 * d // nb_t
    gate_blk = 3 * d // LANES

    def ffn_forward(xv, l, half, sub):
        wg, li = weights[l // 2], l % 2
        a, h = modmm(xv, mv, (l, sub), wg["w13"], pl.BlockSpec((None, None, d, c13), lambda i, n: (li, half, 0, n)), n13,
                     jax.ShapeDtypeStruct((2, s, f), BF16),
                     pl.BlockSpec((None, tw, c13), lambda i, n: (n // 2, i, n % 2)), f"ffn_up_{l}_{half}", True)
        xo, y, u = resmm(xv, mv, (l, sub), 0.5, a, pl.BlockSpec((2, tm, c13), lambda i, k: (0, i, k)),
                         wg["w2"], pl.BlockSpec((None, None, c13, d), lambda i, k: (li, half, k, 0)), f // c13, c13,
                         f"ffn_down_{l}_{half}", True)
        return xo, dict(x=xv, h=h, a=a, u=u, y=y)

    saved = []
    xv = x0
    for l in range(n_layers):
        j = l // 2
        if l % 2 == 0 and j >= 1:
            send_sems, recv_sems, shards, fulls, _ = in_flight[j - 1]
            weights.append(group_weights(exchange_wait(send_sems, recv_sems, shards, fulls, xv, kinds, True,
                                                       f"gather_wait_{j}")))
        wg = weights[j]
        xv, s0 = ffn_forward(xv, l, 0, 0)
        if l % 2 == 0:
            proj, h = modmm(xv, mv, (l, 1), wg["wa_in"], pl.BlockSpec((d, wa_t), lambda i, n: (0, n)), na_t,
                            jax.ShapeDtypeStruct((s, a_in), BF16), pl.BlockSpec((tw, wa_t), lambda i, n: (i, n)),
                            f"swa_in_{j}", True)
            gains = jnp.concatenate([jnp.tile(swa_q_g[j] * HEAD_DIM ** -0.5, hq), jnp.tile(swa_k_g[j], hkv)])[None, :]
            qkn = qknorm_fwd(proj, gains, kw, f"swa_qknorm_{j}")
            sink = swa_sink[j][None, :]
            o, lse = swa_fwd(qkn, proj, bias, sink, d, f"swa_attn_{j}")
            s1 = dict(x=xv, h=h, proj=proj, gains=gains, qkn=qkn, sink=sink, o=o, lse=lse)
            w_out = wg["wa_out"]
        else:
            proj, h = modmm(xv, mv, (l, 1), wg["wb_in"], pl.BlockSpec((d, wb_t), lambda i, n: (0, n)), nb_t,
                            jax.ShapeDtypeStruct((s, 3 * d), BF16), pl.BlockSpec((tw, wb_t), lambda i, n: (i, n)),
                            f"fox_in_{j}", True)
            fl, _ = modmm(xv, mv, (l, 1), wg["wb_in"], pl.BlockSpec((d, LANES), lambda i, n: (0, gate_blk)), 1,
                          jax.ShapeDtypeStruct((s, LANES), F32), pl.BlockSpec((tw, LANES), lambda i, n: (i, 0)),
                          f"fox_gate_in_{j}", False)
            b_f = jnp.pad(fox_b_f[j], (0, LANES - hq))[None, :]
            fcum = fox_gates(fl, b_f, f"fox_gates_{j}")
            f_t = fcum[:, :hq].T
            f_col, f_row = f_t[:, :, None], f_t[:, None, :]
            gains = jnp.concatenate([jnp.tile(fox_q_g[j] * HEAD_DIM ** -0.5, hq), jnp.tile(fox_k_g[j], hq)])[None, :]
            qkn = qknorm_fwd(proj, gains, d, f"fox_qknorm_{j}")
            o, o32, lse = fox_fwd(qkn, proj, f_col, f_row, d, f"fox_attn_{j}")
            s1 = dict(x=xv, h=h, proj=proj, gains=gains, qkn=qkn, fl=fl, b_f=b_f, f_col=f_col, f_row=f_row, o=o, o32=o32,
                      lse=lse)
            w_out = wg["wb_out"]
        xv, y = resmm(xv, mv, (l, 1), 1.0, o, pl.BlockSpec((tm, d), lambda i, k: (i, 0)), w_out,
                      pl.BlockSpec((d, d), lambda i, k: (0, 0)), 1, d, f"mixer_out_{l}", False)
        s1["y"] = y
        xv, s2 = ffn_forward(xv, l, 1, 2)
        saved.append((s0, s1, s2))

    dxv, loss_part = loss_head(xv, loss_target[0])
    loss = lax.psum(loss_part[0, 0], ("x", "y", "c"))

    grads = [dict(w13=lax.empty((2, 2, d, 2 * f), BF16), w2=lax.empty((2, 2, f, d), BF16),
                  wa_in=lax.empty((d, a_in), BF16), wa_out=lax.empty((d, d), BF16),
                  wb_in=lax.empty((d, b_pad), BF16), wb_out=lax.empty((d, d), BF16)) for _ in range(n_groups)]
    dmod = [[None] * 3 for _ in range(n_layers)]
    dnorm = [[None] * 3 for _ in range(n_layers)]
    dqk_gain = {}
    dsink, db_f, dbias_tabs = {}, {}, []

    def ffn_backward(dxo, sv, l, half, sub):
        wg, gg, li = weights[l // 2], grads[l // 2], l % 2
        dy, da, dgate = resmm_bwd(dxo, sv["y"], mv, (l, sub), 0.5, wg["w2"],
                                  pl.BlockSpec((None, None, c13, d), lambda i, k: (li, half, k, 0)), f // c13, c13,
                                  f"ffn_down_bwd_{l}_{half}", a=sv["a"])
        gg["w2"] = weight_grad(sv["u"], pl.BlockSpec((tw, c13), lambda m, n, k: (k, m)), dy,
                               pl.BlockSpec((tw, d), lambda m, n, k: (k, 0)), (f // c13, 1), s, c13, d, gg["w2"],
                               pl.BlockSpec((None, None, c13, d), lambda m, n, k: (li, half, m, 0)),
                               f"ffn_w2_grad_{l}_{half}")
        gg["w13"] = weight_grad(sv["h"], pl.BlockSpec((tw, d), lambda m, n, k: (k, 0)), da,
                                pl.BlockSpec((None, tw, c13), lambda m, n, k: (n // 2, k, n % 2)), (1, n13), s, d, c13,
                                gg["w13"], pl.BlockSpec((None, None, d, c13), lambda m, n, k: (li, half, 0, n)),
                                f"ffn_w13_grad_{l}_{half}")
        dx, red = modmm_bwd(da, pl.BlockSpec((None, tm, c13), lambda i, n: (n // 2, i, n % 2)), wg["w13"],
                            pl.BlockSpec((None, None, d, c13), lambda i, n: (li, half, 0, n)), n13, sv["x"], dxo, mv,
                            (l, sub), f"ffn_up_bwd_{l}_{half}")
        dmod[l][sub] = (red[1], red[2], dgate[0])
        dnorm[l][sub] = red[0]
        return dx

    def group_slabs(gg):
        gb_slots = jnp.stack([gg["wb_in"][:, b * fx:(b + 1) * fx] for b in range(N_CHIP)])
        return [gg["w13"], gg["w2"], gg["wa_in"], gg["wa_out"].reshape(N_CHIP, d // N_CHIP, d), gb_slots,
                gg["wb_out"].reshape(N_CHIP, d // N_CHIP, d)]

    shard_shapes = [(2,) + ffn_w13.shape[1:], (2,) + ffn_w2.shape[1:], swa_w_in.shape[1:], swa_w_out.shape[1:],
                    fox_w_in.shape[1:], fox_w_out.shape[1:]]
    scatter_in_flight = {}

    for l in reversed(range(n_layers)):
        j = l // 2
        wg, gg = weights[j], grads[j]
        s0, s1, s2 = saved[l]
        dxv = ffn_backward(dxv, s2, l, 1, 2)
        is_a = l % 2 == 0
        w_out = wg["wa_out"] if is_a else wg["wb_out"]
        dy, do, dgate = resmm_bwd(dxv, s1["y"], mv, (l, 1), 1.0, w_out, pl.BlockSpec((d, d), lambda i, k: (0, 0)),
                                  1, d, f"mixer_out_bwd_{l}")
        out_key = "wa_out" if is_a else "wb_out"
        gg[out_key] = weight_grad(s1["o"], pl.BlockSpec((tw, d), lambda m, n, k: (k, 0)), dy,
                                  pl.BlockSpec((tw, d), lambda m, n, k: (k, 0)), (1, 1), s, d, d, gg[out_key],
                                  pl.BlockSpec((d, d), lambda m, n, k: (0, 0)), f"mixer_out_grad_{l}")
        if is_a:
            d_qkv, dbias_tab, dsk = swa_bwd(s1["qkn"], s1["proj"], bias, s1["sink"], do, s1["o"], s1["lse"], d,
                                            f"swa_attn_bwd_{j}")
            dbias_tabs.append(dbias_tab)
            dsink[j] = dsk[0, :hq]
            dproj, dgain = qknorm_bwd(s1["proj"], s1["gains"], d_qkv, pl.BlockSpec((tm, kw), lambda c_, i: (i, c_)),
                                      a_in, kw, f"swa_qknorm_bwd_{j}")
            gg["wa_in"] = weight_grad(s1["h"], pl.BlockSpec((tw, d), lambda m, n, k: (k, 0)), dproj,
                                      pl.BlockSpec((tw, wa_t), lambda m, n, k: (k, n)), (1, na_t), s, d, wa_t, gg["wa_in"],
                                      pl.BlockSpec((d, wa_t), lambda m, n, k: (0, n)), f"swa_in_grad_{j}")
            dxv, red = modmm_bwd(dproj, pl.BlockSpec((tm, wa_t), lambda i, n: (i, n)), wg["wa_in"],
                                 pl.BlockSpec((d, wa_t), lambda i, n: (0, n)), na_t, s1["x"], dxv, mv, (l, 1),
                                 f"swa_in_bwd_{j}")
            dqk_gain[("a", j)] = (dgain[0, :d].reshape(hq, HEAD_DIM).sum(0) * HEAD_DIM ** -0.5,
                                  dgain[0, d:d + kw].reshape(hkv, HEAD_DIM).sum(0))
        else:
            lse_row = s1["lse"].reshape(hq, 1, s)
            d_qkv, df_col, dfq_row = fox_bwd(s1["qkn"], s1["proj"], s1["f_col"], s1["f_row"], lse_row, do, s1["o32"], d,
                                             f"fox_attn_bwd_{j}")
            lanes_of_heads = lambda a: jnp.pad(a.T, ((0, 0), (0, LANES - hq)))
            dfl, dbf = fox_gates_bwd(s1["fl"], s1["b_f"], lanes_of_heads(dfq_row[:, 0, :]), lanes_of_heads(df_col[:, :, 0]),
                                     f"fox_gates_bwd_{j}")
            db_f[j] = dbf[0, :hq]
            dproj, dgain = qknorm_bwd(s1["proj"], s1["gains"], d_qkv, pl.BlockSpec((None, tm, d), lambda c_, i: (c_, i, 0)),
                                      3 * d, d, f"fox_qknorm_bwd_{j}")
            gg["wb_in"] = weight_grad(s1["h"], pl.BlockSpec((tw, d), lambda m, n, k: (k, 0)), dproj,
                                      pl.BlockSpec((tw, wb_t), lambda m, n, k: (k, n)), (1, nb_t), s, d, wb_t, gg["wb_in"],
                                      pl.BlockSpec((d, wb_t), lambda m, n, k: (0, n)), f"fox_in_grad_{j}")
            gg["wb_in"] = weight_grad(s1["h"], pl.BlockSpec((tw, d), lambda m, n, k: (k, 0)), dfl,
                                      pl.BlockSpec((tw, LANES), lambda m, n, k: (k, 0)), (1, 1), s, d, LANES, gg["wb_in"],
                                      pl.BlockSpec((d, LANES), lambda m, n, k: (0, gate_blk)), f"fox_gate_in_grad_{j}")
            dxv, red = modmm_bwd(dproj, pl.BlockSpec((tm, wb_t), lambda i, n: (i, n)), wg["wb_in"],
                                 pl.BlockSpec((d, wb_t), lambda i, n: (0, n)), nb_t, s1["x"], dxv, mv, (l, 1),
                                 f"fox_in_bwd_{j}",
                                 more=(dfl, pl.BlockSpec((tm, LANES), lambda i, n: (i, 0)),
                                       pl.BlockSpec((d, LANES), lambda i, n: (0, gate_blk))))
            dqk_gain[("b", j)] = (dgain[0, :d].reshape(hq, HEAD_DIM).sum(0) * HEAD_DIM ** -0.5,
                                  dgain[0, d:2 * d].reshape(hq, HEAD_DIM).sum(0))
        dmod[l][1] = (red[1], red[2], dgate[0])
        dnorm[l][1] = red[0]
        dxv = ffn_backward(dxv, s0, l, 0, 0)
        if l % 2 == 0 and j >= 1:
            slabs = group_slabs(gg)
            lands = [lax.empty((N_CHIP,) + tuple(shp), BF16) for shp in shard_shapes]
            lands = place_shards(slabs, lands, kinds, False, f"place_own_grads_{j}")
            scatter_in_flight[j] = exchange_start(slabs, lands, kinds, False, f"scatter_start_{j}")
            mv = mv + scatter_in_flight[j][-1][0, 0]
    grad_x = dxv[None]

    drel = rel_bias_grad(jnp.stack(dbias_tabs), bucket)
    dmod_flat = jnp.stack([jnp.stack([jnp.stack(dmod[l][sub]) for sub in range(3)]) for l in range(n_layers)]).reshape(-1)
    dnorm_flat = jnp.stack([jnp.stack(dnorm[l]) for l in range(n_layers)]).reshape(-1)
    pieces = [dmod_flat, dnorm_flat,
              jnp.stack([dqk_gain[("a", j)][0] for j in range(n_a)]).reshape(-1),
              jnp.stack([dqk_gain[("a", j)][1] for j in range(n_a)]).reshape(-1),
              jnp.stack([dqk_gain[("b", j)][0] for j in range(n_b)]).reshape(-1),
              jnp.stack([dqk_gain[("b", j)][1] for j in range(n_b)]).reshape(-1),
              jnp.stack([dsink[j] for j in range(n_a)]).reshape(-1),
              jnp.stack([db_f[j] for j in range(n_b)]).reshape(-1),
              drel.reshape(-1)]
    rows = [_pad_rows(p) for p in pieces]
    starts = np.cumsum([0] + [r.shape[0] for r in rows])
    total = -(-int(starts[-1]) // 8) * 8
    small = jnp.pad(jnp.concatenate(rows), ((0, total - int(starts[-1])), (0, 0)))
    small_all = all_gather_rows(small, "gather_small_grads").reshape(N_DEV, total, LANES)
    small_sum = sum_devices(small_all)

    def piece(k, shape):
        n = int(np.prod(shape))
        return small_sum[int(starts[k]):int(starts[k + 1])].reshape(-1)[:n].reshape(shape)

    g_ada_b = piece(0, (n_layers, 9 * d))
    g_norm = lax.dynamic_slice_in_dim(piece(1, (n_layers, 3, d)), chip * cq, cq, axis=2)
    g_swa_q, g_swa_k = piece(2, (n_a, HEAD_DIM)), piece(3, (n_a, HEAD_DIM))
    g_fox_q, g_fox_k = piece(4, (n_b, HEAD_DIM)), piece(5, (n_b, HEAD_DIM))
    g_sink, g_bf, g_rel = piece(6, (n_a, hq)), piece(7, (n_b, hq)), piece(8, (REL_BUCKETS, hq))

    dmod_all = small_all[:, :int(starts[1])].reshape(N_DEV, -1)[:, :n_layers * 9 * d].reshape(N_DEV, n_layers, 9 * d)
    ada_cols = ada_w.shape[-1]
    dmod_mine = lax.dynamic_slice_in_dim(jnp.moveaxis(dmod_all, 0, 1), chip * ada_cols, ada_cols, axis=2)
    g_ada_w = ada_grad(c_all.T, dmod_mine)

    landed = {0: scatter_grads(group_slabs(grads[0]), kinds, shard_shapes)}
    for g, (send_sems, recv_sems, slabs, lands, _) in scatter_in_flight.items():
        landed[g] = exchange_wait(send_sems, recv_sems, slabs, lands, landed[0][0], kinds, False, f"scatter_wait_{g}")
    parts = [jnp.concatenate([sum_slots(landed[g][t].reshape(N_CHIP, -1, landed[g][t].shape[-1]))
                              for g in range(n_groups)]) for t in range(len(kinds))]
    others = swap_with_sibling(parts)

    def update(w, m, v, g, g2=None):
        w2d = w.reshape(-1, w.shape[-1])
        outs = adamw(w2d, m.reshape(w2d.shape), v.reshape(w2d.shape), g.reshape(w2d.shape) if g2 is None else g, g2)
        return [t.reshape(w.shape) for t in outs]

    big = [(ffn_w13, m_ffn_w13, v_ffn_w13), (ffn_w2, m_ffn_w2, v_ffn_w2), (swa_w_in, m_swa_w_in, v_swa_w_in),
           (swa_w_out, m_swa_w_out, v_swa_w_out), (fox_w_in, m_fox_w_in, v_fox_w_in), (fox_w_out, m_fox_w_out, v_fox_w_out)]
    big_out = [update(w, m, v, p, q) for (w, m, v), p, q in zip(big, parts, others)]
    r_ada_w = update(ada_w, m_ada_w, v_ada_w, g_ada_w)
    r_ada_b = update(ada_b, m_ada_b, v_ada_b, g_ada_b)
    r_norm = update(norm_g, m_norm_g, v_norm_g, g_norm)
    r_rel = update(rel_bias, m_rel_bias, v_rel_bias, g_rel)
    r_swa_q = update(swa_q_g, m_swa_q_g, v_swa_q_g, g_swa_q)
    r_swa_k = update(swa_k_g, m_swa_k_g, v_swa_k_g, g_swa_k)
    r_sink = update(swa_sink, m_swa_sink, v_swa_sink, g_sink)
    r_bf = update(fox_b_f, m_fox_b_f, v_fox_b_f, g_bf)
    r_fox_q = update(fox_q_g, m_fox_q_g, v_fox_q_g, g_fox_q)
    r_fox_k = update(fox_k_g, m_fox_k_g, v_fox_k_g, g_fox_k)
    per_weight = [r_ada_w, r_ada_b, r_norm, big_out[0], big_out[1], r_rel, big_out[2], big_out[3], r_swa_q, r_swa_k,
                  r_sink, big_out[4], big_out[5], r_bf, r_fox_q, r_fox_k]
    return (loss, grad_x, *[r[0] for r in per_weight], *[r[1] for r in per_weight],
            *[r[2] for r in per_weight], *[r[3] for r in per_weight])
```

```python
import math

import numpy as np
import jax
import jax.numpy as jnp
from jax import lax
from jax.experimental import pallas as pl
from jax.experimental.pallas import tpu as pltpu

F32 = jnp.float32
BF16 = jnp.bfloat16
HEAD_DIM = 64
GROUP = 4
BLOCK = 128
REL_BUCKETS = 32
REL_MAX_DIST = 128
EPS = 1e-6
NEG = -1e30
N_CHIP = 4
N_DEV = 8
LANES = 128
VMEM_LIMIT = 52 * 1024 * 1024
ADAM_LR, ADAM_B1, ADAM_B2, ADAM_EPS, ADAM_WD, ADAM_STEP = 0.001, 0.9, 0.999, 1e-08, 0.01, 10
MESH = pl.DeviceIdType.MESH
ANY = pl.BlockSpec(memory_space=pl.ANY)


def _params(n_axes):
    return pltpu.CompilerParams(dimension_semantics=("arbitrary",) * n_axes, vmem_limit_bytes=VMEM_LIMIT)


def _nn(a, b):
    return jnp.dot(a, b, preferred_element_type=F32)


def _nt(a, b):
    return lax.dot_general(a, b, (((1,), (1,)), ((), ())), preferred_element_type=F32)


def _tn(a, b):
    return lax.dot_general(a, b, (((0,), (0,)), ((), ())), preferred_element_type=F32)


def _sigmoid(z):
    return 1.0 / (1.0 + jnp.exp(-z))


def _sigmoid_fast(z):
    return pl.reciprocal(1.0 + jnp.exp(-z), approx=True)


def _row_tile(s):
    return 512 if s >= 2048 else s // 2


def _wide_tile(s):
    return 1024 if s >= 2048 else s // 2


def _attn_tile(s):
    return 512 if s >= 2048 else s // 4


def _position():
    x, y, c = lax.axis_index("x"), lax.axis_index("y"), lax.axis_index("c")
    chips = [(1 - x, y), (x, 1 - y), (1 - x, 1 - y)]
    return x, y, c, chips


def all_gather_rows(v, name):
    m_per, n = v.shape

    def body(x_ref, out_ref, send_sems, recv_sems, local_sem):
        x, y, c, chips = _position()
        me, sibling = (x, y, c), (x, y, 1 - c)

        def rows(px, py, pc):
            return out_ref.at[pl.ds((4 * px + 2 * py + pc) * m_per, m_per), :]

        def copy(k, block, to, src=None):
            return pltpu.make_async_remote_copy(
                src_ref=rows(*block) if src is None else src, dst_ref=rows(*block),
                send_sem=send_sems.at[k], recv_sem=recv_sems.at[k], device_id=to, device_id_type=MESH)

        mine = pltpu.make_async_copy(x_ref, rows(*me), local_sem)
        mine.start()
        first = [copy(0, me, sibling, src=x_ref)]
        first += [copy(1 + j, me, (*chip, c), src=x_ref) for j, chip in enumerate(chips)]
        for cp in first:
            cp.start()
        passed = [copy(4 + j, (*chip, c), sibling) for j, chip in enumerate(chips)]
        for j, chip in enumerate(chips):
            copy(1 + j, (*chip, c), me).wait_recv()
            passed[j].start()
        copy(0, sibling, me).wait_recv()
        for j, chip in enumerate(chips):
            copy(4 + j, (*chip, 1 - c), me).wait_recv()
        for cp in first + passed:
            cp.wait_send()
        mine.wait()

    return pl.pallas_call(
        body, name=name,
        out_shape=jax.ShapeDtypeStruct((N_DEV * m_per, n), v.dtype),
        in_specs=[pl.BlockSpec(memory_space=pltpu.VMEM)],
        out_specs=pl.BlockSpec(memory_space=pltpu.VMEM),
        scratch_shapes=[pltpu.SemaphoreType.DMA((7,)), pltpu.SemaphoreType.DMA((7,)), pltpu.SemaphoreType.DMA],
    )(v)


def _slab(full_ref, kind, b, lead):
    how = kind[0]
    if how == "slot":
        return full_ref.at[b, lead]
    if how == "col":
        w = kind[1]
        idx = (lead,) + (slice(None),) * (len(full_ref.shape) - 2) + (pl.ds(pl.multiple_of(b * w, LANES), w),)
        return full_ref.at[idx]
    h = kind[1]
    idx = (lead,) + (slice(None),) * (len(full_ref.shape) - 3) + (pl.ds(pl.multiple_of(b * h, 8), h), slice(None))
    return full_ref.at[idx]


def _full_shape(shard_shape, kind):
    if kind[0] == "slot":
        return (N_CHIP,) + tuple(shard_shape)
    if kind[0] == "col":
        return tuple(shard_shape[:-1]) + (N_CHIP * shard_shape[-1],)
    return tuple(shard_shape[:-2]) + (N_CHIP * shard_shape[-2], shard_shape[-1])


def _slab_start(shape, kind, b):
    zeros = [0] * len(shape)
    if kind[0] == "slot":
        return [b] + zeros[1:]
    if kind[0] == "col":
        return zeros[:-1] + [b * kind[1]]
    return zeros[:-2] + [b * kind[1], 0]


def place_own(shard, kind, b):
    full = lax.empty(_full_shape(shard.shape, kind), shard.dtype)
    piece = shard[None] if kind[0] == "slot" else shard
    return lax.dynamic_update_slice(full, piece, _slab_start(full.shape, kind, b))


def own_slab(full, kind, b, shard_shape):
    sizes = (1,) + tuple(shard_shape) if kind[0] == "slot" else tuple(shard_shape)
    return lax.dynamic_slice(full, _slab_start(full.shape, kind, b), sizes).reshape(shard_shape)


def gather_weights(shards, fulls, kinds):
    n = len(shards)

    def body(*refs):
        ins, outs = refs[:n], refs[2 * n:3 * n]
        send_sems, recv_sems = refs[3 * n:]
        x, y, c, chips = _position()
        b_me = 2 * x + y
        sibling = (x, y, 1 - c)
        sends = []
        for t in range(n):
            half = ins[t].shape[0] // 2
            mine = pl.ds(c * half, half)
            for j, chip in enumerate(chips):
                cp = pltpu.make_async_remote_copy(
                    src_ref=ins[t].at[mine], dst_ref=_slab(outs[t], kinds[t], b_me, mine),
                    send_sem=send_sems.at[6 * t + j], recv_sem=recv_sems.at[6 * t + j],
                    device_id=(*chip, c), device_id_type=MESH)
                cp.start()
                sends.append(cp)
        for t in range(n):
            half = ins[t].shape[0] // 2
            mine = pl.ds(c * half, half)
            for j, chip in enumerate(chips):
                landed = _slab(outs[t], kinds[t], 2 * chip[0] + chip[1], mine)
                pltpu.make_async_remote_copy(
                    src_ref=landed, dst_ref=landed, send_sem=send_sems.at[6 * t + j], recv_sem=recv_sems.at[6 * t + j],
                    device_id=(*chip, c), device_id_type=MESH).wait_recv()
                cp = pltpu.make_async_remote_copy(
                    src_ref=landed, dst_ref=landed, send_sem=send_sems.at[6 * t + 3 + j],
                    recv_sem=recv_sems.at[6 * t + 3 + j], device_id=sibling, device_id_type=MESH)
                cp.start()
                sends.append(cp)
        for t in range(n):
            half = ins[t].shape[0] // 2
            theirs = pl.ds((1 - c) * half, half)
            for j, chip in enumerate(chips):
                landed = _slab(outs[t], kinds[t], 2 * chip[0] + chip[1], theirs)
                pltpu.make_async_remote_copy(
                    src_ref=landed, dst_ref=landed, send_sem=send_sems.at[6 * t + 3 + j],
                    recv_sem=recv_sems.at[6 * t + 3 + j], device_id=sibling, device_id_type=MESH).wait_recv()
        for cp in sends:
            cp.wait_send()

    return pl.pallas_call(
        body, name="gather_weights",
        out_shape=[jax.ShapeDtypeStruct(v.shape, v.dtype) for v in fulls],
        in_specs=[ANY] * (2 * n), out_specs=[ANY] * n,
        input_output_aliases={n + t: t for t in range(n)},
        scratch_shapes=[pltpu.SemaphoreType.DMA((6 * n,)), pltpu.SemaphoreType.DMA((6 * n,))],
    )(*shards, *fulls)


def scatter_grads(grads, kinds, shard_shapes):
    n = len(grads)

    def body(*refs):
        ins, outs = refs[:n], refs[n:2 * n]
        send_sems, recv_sems = refs[2 * n:]
        x, y, c, chips = _position()
        sends = []
        for t in range(n):
            whole = _whole(ins[t], kinds[t])
            for j, chip in enumerate(chips):
                cp = pltpu.make_async_remote_copy(
                    src_ref=_slab(ins[t], kinds[t], 2 * chip[0] + chip[1], whole), dst_ref=outs[t].at[j],
                    send_sem=send_sems.at[3 * t + j], recv_sem=recv_sems.at[3 * t + j],
                    device_id=(*chip, c), device_id_type=MESH)
                cp.start()
                sends.append(cp)
        for t in range(n):
            for j, chip in enumerate(chips):
                pltpu.make_async_remote_copy(
                    src_ref=outs[t].at[j], dst_ref=outs[t].at[j], send_sem=send_sems.at[3 * t + j],
                    recv_sem=recv_sems.at[3 * t + j], device_id=(*chip, c), device_id_type=MESH).wait_recv()
        for cp in sends:
            cp.wait_send()

    return pl.pallas_call(
        body, name="scatter_grads",
        out_shape=[jax.ShapeDtypeStruct((N_CHIP - 1,) + tuple(s), g.dtype) for g, s in zip(grads, shard_shapes)],
        in_specs=[ANY] * n, out_specs=[ANY] * n,
        scratch_shapes=[pltpu.SemaphoreType.DMA((3 * n,)), pltpu.SemaphoreType.DMA((3 * n,))],
    )(*grads)


def _whole(ref, kind):
    return pl.ds(0, ref.shape[1] if kind[0] == "slot" else ref.shape[0])


def _exchange_copies(srcs, lands, kinds, gather, send_sems, recv_sems):
    x, y, c, chips = _position()
    b_me = 2 * x + y
    out = []
    for t in range(len(srcs)):
        for j, chip in enumerate(chips):
            b_j = 2 * chip[0] + chip[1]
            if gather:
                src = srcs[t]
                sent_to = _slab(lands[t], kinds[t], b_me, _whole(lands[t], kinds[t]))
                arrives = _slab(lands[t], kinds[t], b_j, _whole(lands[t], kinds[t]))
            else:
                src = _slab(srcs[t], kinds[t], b_j, _whole(srcs[t], kinds[t]))
                sent_to = arrives = lands[t].at[j]
            k = 3 * t + j
            send = pltpu.make_async_remote_copy(src_ref=src, dst_ref=sent_to, send_sem=send_sems.at[k],
                                                recv_sem=recv_sems.at[k], device_id=(*chip, c), device_id_type=MESH)
            recv = pltpu.make_async_remote_copy(src_ref=src, dst_ref=arrives, send_sem=send_sems.at[k],
                                                recv_sem=recv_sems.at[k], device_id=(*chip, c), device_id_type=MESH)
            out.append((send, recv))
    return out


def exchange_start(srcs, lands, kinds, gather, name):
    n = len(srcs)
    hbm = pl.BlockSpec(memory_space=pltpu.HBM)

    def body(*refs):
        ins, lnd = refs[:n], refs[n:2 * n]
        send_sems, recv_sems = refs[2 * n], refs[2 * n + 1]
        token = refs[-1]
        for send, _ in _exchange_copies(ins, lnd, kinds, gather, send_sems, recv_sems):
            send.start()
        token[...] = jnp.zeros_like(token)

    ops = [pltpu.with_memory_space_constraint(v, pltpu.HBM) for v in (*srcs, *lands)]
    res = pl.pallas_call(
        body, name=name,
        out_shape=(pltpu.SemaphoreType.DMA((3 * n,)), pltpu.SemaphoreType.DMA((3 * n,)),
                   *[pltpu.HBM(v.shape, v.dtype) for v in ops], jax.ShapeDtypeStruct((8, LANES), F32)),
        in_specs=[hbm] * (2 * n),
        out_specs=(pl.BlockSpec(memory_space=pltpu.SEMAPHORE), pl.BlockSpec(memory_space=pltpu.SEMAPHORE),
                   *[hbm] * (2 * n), pl.BlockSpec(memory_space=pltpu.VMEM)),
        input_output_aliases={t: 2 + t for t in range(2 * n)},
        compiler_params=pltpu.CompilerParams(has_side_effects=pltpu.SideEffectType.DATAFLOW_SIDE_EFFECTING),
    )(*ops)
    return res[0], res[1], list(res[2:2 + n]), list(res[2 + n:2 + 2 * n]), res[-1]


def exchange_wait(send_sems, recv_sems, srcs, lands, after, kinds, gather, name):
    n = len(srcs)
    hbm = pl.BlockSpec(memory_space=pltpu.HBM)

    def body(*refs):
        ins, lnd = refs[:n], refs[n:2 * n]
        ssem, rsem = refs[2 * n], refs[2 * n + 1]
        for send, recv in _exchange_copies(ins, lnd, kinds, gather, ssem, rsem):
            send.wait_send()
            recv.wait_recv()

    res = pl.pallas_call(
        body, name=name,
        out_shape=tuple(pltpu.HBM(v.shape, v.dtype) for v in (*srcs, *lands)),
        in_specs=[hbm] * (2 * n) + [pl.BlockSpec(memory_space=pltpu.SEMAPHORE)] * 2 + [ANY],
        out_specs=tuple([hbm] * (2 * n)),
        input_output_aliases={t: t for t in range(2 * n)},
        compiler_params=pltpu.CompilerParams(has_side_effects=pltpu.SideEffectType.DATAFLOW_SIDE_EFFECTING),
    )(*srcs, *lands, send_sems, recv_sems, after)
    return list(res[:n]), list(res[n:])


def swap_with_sibling(parts):
    n = len(parts)

    def body(*refs):
        ins, outs = refs[:n], refs[n:2 * n]
        send_sems, recv_sems = refs[2 * n:]
        x, y, c, _ = _position()
        cps = []
        for t in range(n):
            cp = pltpu.make_async_remote_copy(
                src_ref=ins[t], dst_ref=outs[t], send_sem=send_sems.at[t], recv_sem=recv_sems.at[t],
                device_id=(x, y, 1 - c), device_id_type=MESH)
            cp.start()
            cps.append(cp)
        for cp in cps:
            cp.wait_recv()
        for cp in cps:
            cp.wait_send()

    return pl.pallas_call(
        body, name="swap_with_sibling",
        out_shape=[jax.ShapeDtypeStruct(p.shape, p.dtype) for p in parts],
        in_specs=[ANY] * n, out_specs=[ANY] * n,
        scratch_shapes=[pltpu.SemaphoreType.DMA((n,)), pltpu.SemaphoreType.DMA((n,))],
    )(*parts)


def _modulated(xv, mv_ref):
    g, shift, scale = mv_ref[0:1, :], mv_ref[1:2, :], mv_ref[2:3, :]
    r = lax.rsqrt(jnp.mean(xv * xv, axis=-1, keepdims=True) + EPS)
    xhat = xv * r
    xn = xhat * g
    return xn * (1.0 + scale) + shift, xhat, xn, r, g, scale


def modmm(x, mv, mv_idx, w, w_spec, n_tiles, out_shape, out_spec, name, want_h):
    s, d = x.shape
    tm = _wide_tile(s)

    def body(x_ref, mv_ref, w_ref, *rest):
        if want_h:
            out_ref, h_ref, h_scr = rest
        else:
            out_ref, h_scr = rest

        @pl.when(pl.program_id(1) == 0)
        def _():
            h = _modulated(x_ref[...], mv_ref)[0].astype(BF16)
            h_scr[...] = h
            if want_h:
                h_ref[...] = h

        out_ref[...] = _nn(h_scr[...], w_ref[...]).astype(out_ref.dtype)

    out_shapes = [out_shape]
    out_specs = [out_spec]
    if want_h:
        out_shapes.append(jax.ShapeDtypeStruct((s, d), BF16))
        out_specs.append(pl.BlockSpec((tm, d), lambda i, n: (i, 0)))
    res = pl.pallas_call(
        body, name=name, grid=(s // tm, n_tiles),
        in_specs=[pl.BlockSpec((tm, d), lambda i, n: (i, 0)),
                  pl.BlockSpec((None, None, 8, d), lambda i, n: (*mv_idx, 0, 0)), w_spec],
        out_specs=out_specs, out_shape=out_shapes,
        scratch_shapes=[pltpu.VMEM((tm, d), BF16)],
        compiler_params=_params(2),
    )(x, mv, w)
    return res if want_h else (res[0], None)


def resmm(x, mv, mv_idx, coef, lhs, lhs_spec, w, w_spec, k_tiles, tk, name, ffn):
    s, d = x.shape
    tm = _row_tile(s)
    kdim = k_tiles * tk

    def body(x_ref, mv_ref, lhs_ref, w_ref, xo_ref, y_ref, *rest):
        if ffn:
            u_ref, acc = rest
        else:
            (acc,) = rest
        k = pl.program_id(1)

        @pl.when(k == 0)
        def _():
            acc[...] = jnp.zeros_like(acc)

        if ffn:
            ag = lhs_ref[0].astype(F32)
            au = lhs_ref[1].astype(F32)
            left = (ag * _sigmoid_fast(ag) * au).astype(BF16)
            u_ref[...] = left
        else:
            left = lhs_ref[...]
        acc[...] += _nn(left, w_ref[...])

        @pl.when(k == k_tiles - 1)
        def _():
            y = acc[...]
            y_ref[...] = y.astype(BF16)
            xo_ref[...] = x_ref[...] + (coef * mv_ref[3:4, :]) * y

    row = pl.BlockSpec((tm, d), lambda i, k: (i, 0))
    out_shapes = [jax.ShapeDtypeStruct((s, d), F32), jax.ShapeDtypeStruct((s, d), BF16)]
    out_specs = [row, row]
    if ffn:
        out_shapes.append(jax.ShapeDtypeStruct((s, kdim), BF16))
        out_specs.append(pl.BlockSpec((tm, tk), lambda i, k: (i, k)))
    return pl.pallas_call(
        body, name=name, grid=(s // tm, k_tiles),
        in_specs=[row, pl.BlockSpec((None, None, 8, d), lambda i, k: (*mv_idx, 0, 0)), lhs_spec, w_spec],
        out_specs=out_specs, out_shape=out_shapes,
        scratch_shapes=[pltpu.VMEM((tm, d), F32)],
        compiler_params=_params(2),
    )(x, mv, lhs, w)


def resmm_bwd(dxo, y, mv, mv_idx, coef, w, w_spec, k_tiles, tk, name, a=None):
    s, d = dxo.shape
    tm = _row_tile(s)
    kdim = k_tiles * tk
    ffn = a is not None

    def body(dxo_ref, y_ref, mv_ref, w_ref, *rest):
        if ffn:
            a_ref, dy_ref, dl_ref, dgate_ref, dy_scr = rest
        else:
            dy_ref, dl_ref, dgate_ref, dy_scr = rest
        i, k = pl.program_id(0), pl.program_id(1)

        @pl.when((i == 0) & (k == 0))
        def _():
            dgate_ref[...] = jnp.zeros_like(dgate_ref)

        @pl.when(k == 0)
        def _():
            dxv = dxo_ref[...]
            dy = ((coef * mv_ref[3:4, :]) * dxv).astype(BF16)
            dy_scr[...] = dy
            dy_ref[...] = dy
            dgate_ref[0:1, :] += jnp.sum(coef * dxv * y_ref[...].astype(F32), axis=0, keepdims=True)

        dl = _nt(dy_scr[...], w_ref[...])
        if ffn:
            ag = a_ref[0].astype(F32)
            au = a_ref[1].astype(F32)
            sg = _sigmoid_fast(ag)
            dl_ref[0] =(dl * au * (sg * (1.0 + ag * (1.0 - sg)))).astype(BF16)
            dl_ref[1] = (dl * (ag * sg)).astype(BF16)
        else:
            dl_ref[...] = dl.astype(BF16)

    row = pl.BlockSpec((tm, d), lambda i, k: (i, 0))
    in_specs = [row, row, pl.BlockSpec((None, None, 8, d), lambda i, k: (*mv_idx, 0, 0)), w_spec]
    ops = [dxo, y, mv, w]
    if ffn:
        in_specs.append(pl.BlockSpec((2, tm, tk), lambda i, k: (0, i, k)))
        ops.append(a)
        dl_shape = jax.ShapeDtypeStruct((2, s, kdim), BF16)
        dl_spec = pl.BlockSpec((2, tm, tk), lambda i, k: (0, i, k))
    else:
        dl_shape = jax.ShapeDtypeStruct((s, kdim), BF16)
        dl_spec = pl.BlockSpec((tm, tk), lambda i, k: (i, k))
    return pl.pallas_call(
        body, name=name, grid=(s // tm, k_tiles),
        in_specs=in_specs,
        out_specs=[row, dl_spec, pl.BlockSpec((8, d), lambda i, k: (0, 0))],
        out_shape=[jax.ShapeDtypeStruct((s, d), BF16), dl_shape, jax.ShapeDtypeStruct((8, d), F32)],
        scratch_shapes=[pltpu.VMEM((tm, d), BF16)],
        compiler_params=_params(2),
    )(*ops)


def modmm_bwd(dl, dl_spec, w, w_spec, n_tiles, x, dxo, mv, mv_idx, name, more=None):
    s, d = x.shape
    tm = _row_tile(s)

    def body(dl_ref, w_ref, x_ref, dxo_ref, mv_ref, *rest):
        if more is not None:
            dl2_ref, w2_ref, dx_ref, red_ref, acc = rest
        else:
            dx_ref, red_ref, acc = rest
        i, n = pl.program_id(0), pl.program_id(1)

        @pl.when((i == 0) & (n == 0))
        def _():
            red_ref[...] = jnp.zeros_like(red_ref)

        @pl.when(n == 0)
        def _():
            if more is not None:
                acc[...] = _nt(dl2_ref[...], w2_ref[...])
            else:
                acc[...] = jnp.zeros_like(acc)

        acc[...] += _nt(dl_ref[...], w_ref[...])

        @pl.when(n == n_tiles - 1)
        def _():
            dh = acc[...]
            _, xhat, xn, r, g, scale = _modulated(x_ref[...], mv_ref)
            dxn = dh * (1.0 + scale)
            red_ref[0:1, :] += jnp.sum(dxn * xhat, axis=0, keepdims=True)
            red_ref[1:2, :] += jnp.sum(dh, axis=0, keepdims=True)
            red_ref[2:3, :] += jnp.sum(dh * xn, axis=0, keepdims=True)
            gd = dxn * g
            dx_ref[...] = dxo_ref[...] + r * (gd - xhat * jnp.mean(gd * xhat, axis=-1, keepdims=True))

    row = pl.BlockSpec((tm, d), lambda i, n: (i, 0))
    in_specs = [dl_spec, w_spec, row, row, pl.BlockSpec((None, None, 8, d), lambda i, n: (*mv_idx, 0, 0))]
    ops = [dl, w, x, dxo, mv]
    if more is not None:
        in_specs += [more[1], more[2]]
        ops += [more[0], w]
    return pl.pallas_call(
        body, name=name, grid=(s // tm, n_tiles),
        in_specs=in_specs,
        out_specs=[row, pl.BlockSpec((8, d), lambda i, n: (0, 0))],
        out_shape=[jax.ShapeDtypeStruct((s, d), F32), jax.ShapeDtypeStruct((8, d), F32)],
        scratch_shapes=[pltpu.VMEM((tm, d), F32)],
        compiler_params=_params(2),
    )(*ops)


def weight_grad(a, a_spec, b, b_spec, grid_mn, s, bm, bn, dest, out_spec, name):
    tk = _wide_tile(s)
    k_tiles = s // tk

    def body(a_ref, b_ref, dest_ref, out_ref, acc):
        k = pl.program_id(2)

        @pl.when(k == 0)
        def _():
            acc[...] = jnp.zeros_like(acc)

        acc[...] += _tn(a_ref[...], b_ref[...])

        @pl.when(k == k_tiles - 1)
        def _():
            out_ref[...] = acc[...].astype(out_ref.dtype)

    return pl.pallas_call(
        body, name=name, grid=(*grid_mn, k_tiles),
        in_specs=[a_spec, b_spec, ANY], out_specs=out_spec,
        out_shape=jax.ShapeDtypeStruct(dest.shape, dest.dtype),
        input_output_aliases={2: 0},
        scratch_shapes=[pltpu.VMEM((bm, bn), F32)],
        compiler_params=_params(3),
    )(a, b, dest)


def _head_mean(v):
    lane = lax.broadcasted_iota(jnp.int32, v.shape, 1)
    lo = jnp.sum(jnp.where(lane < HEAD_DIM, v, 0.0), axis=-1, keepdims=True)
    hi = jnp.sum(v, axis=-1, keepdims=True) - lo
    return jnp.where(lane < HEAD_DIM, lo, hi) * (1.0 / HEAD_DIM)


def qknorm_fwd(proj, gains, width, name):
    s = proj.shape[0]
    nqk = gains.shape[1]
    tm = _row_tile(s)

    def body(p_ref, g_ref, o_ref):
        for cc in range(width // LANES):
            sl = slice(cc * LANES, (cc + 1) * LANES)
            xv = p_ref[:, sl].astype(F32)
            r = lax.rsqrt(_head_mean(xv * xv) + EPS)
            o_ref[:, sl] = (xv * r * g_ref[:, sl]).astype(BF16)

    blk = pl.BlockSpec((tm, width), lambda i, c: (i, c))
    return pl.pallas_call(
        body, name=name, grid=(s // tm, nqk // width),
        in_specs=[blk, pl.BlockSpec((1, width), lambda i, c: (0, c))],
        out_specs=blk, out_shape=jax.ShapeDtypeStruct((s, nqk), BF16),
        compiler_params=_params(2),
    )(proj, gains)


def qknorm_bwd(proj, gains, d, d_spec, n_cols, width, name):
    s = proj.shape[0]
    nqk = gains.shape[1] // width
    n_blocks = n_cols // width
    tm = _row_tile(s)

    def body(p_ref, g_ref, d_ref, o_ref, dg_ref):
        c, i = pl.program_id(0), pl.program_id(1)

        @pl.when(i == 0)
        def _():
            dg_ref[...] = jnp.zeros_like(dg_ref)

        @pl.when(c < nqk)
        def _():
            for cc in range(width // LANES):
                sl = slice(cc * LANES, (cc + 1) * LANES)
                xv = p_ref[:, sl].astype(F32)
                r = lax.rsqrt(_head_mean(xv * xv) + EPS)
                xhat = xv * r
                dv = d_ref[:, sl]
                gd = dv * g_ref[:, sl]
                o_ref[:, sl] = (r * (gd - xhat * _head_mean(gd * xhat))).astype(BF16)
                dg_ref[0:1, sl] += jnp.sum(dv * xhat, axis=0, keepdims=True)

        @pl.when(c >= nqk)
        def _():
            o_ref[...] = d_ref[...].astype(BF16)

    return pl.pallas_call(
        body, name=name, grid=(n_blocks, s // tm),
        in_specs=[pl.BlockSpec((tm, width), lambda c, i: (i, c)),
                  pl.BlockSpec((1, width), lambda c, i: (0, jnp.minimum(c, nqk - 1))), d_spec],
        out_specs=[pl.BlockSpec((tm, width), lambda c, i: (i, c)), pl.BlockSpec((8, width), lambda c, i: (0, c))],
        out_shape=[jax.ShapeDtypeStruct((s, n_cols), BF16), jax.ShapeDtypeStruct((8, n_cols), F32)],
        compiler_params=_params(2),
    )(proj, gains, d)


def _swa_mask(first):
    qi = lax.broadcasted_iota(jnp.int32, (BLOCK, 2 * BLOCK), 0) + BLOCK
    kj = lax.broadcasted_iota(jnp.int32, (BLOCK, 2 * BLOCK), 1)
    dist = qi - kj
    return (dist >= 0) & (dist < BLOCK) & ((kj >= BLOCK) | jnp.logical_not(first))


def swa_fwd(qkn, proj, bias, sink, d, name):
    s = qkn.shape[0]
    hq = d // HEAD_DIM
    hkv = hq // GROUP
    kw = hkv * HEAD_DIM
    nblk = s // BLOCK
    kcol = d // kw

    def body(q_ref, kc_ref, kp_ref, vc_ref, vp_ref, bias_ref, sink_ref, o_ref, lse_ref):
        mask = _swa_mask(pl.program_id(0) == 0)
        lse_ref[...] = jnp.zeros_like(lse_ref)
        for kvh in range(hkv):
            cols = slice(kvh * HEAD_DIM, (kvh + 1) * HEAD_DIM)
            k2 = jnp.concatenate([kp_ref[:, cols], kc_ref[:, cols]], axis=0)
            v2 = jnp.concatenate([vp_ref[:, cols], vc_ref[:, cols]], axis=0)
            for g in range(GROUP):
                h = kvh * GROUP + g
                hc = slice(h * HEAD_DIM, (h + 1) * HEAD_DIM)
                sc = jnp.where(mask, _nt(q_ref[:, hc], k2) + bias_ref[h], NEG)
                sk = sink_ref[0, h]
                m = jnp.maximum(jnp.max(sc, axis=-1, keepdims=True), sk)
                p = jnp.exp(sc - m)
                denom = jnp.sum(p, axis=-1, keepdims=True) + jnp.exp(sk - m)
                o_ref[:, hc] = (_nn(p.astype(BF16), v2) / denom).astype(BF16)
                lse_ref[:, h:h + 1] = m + jnp.log(denom)

    prev = lambda i: jnp.maximum(i - 1, 0)
    return pl.pallas_call(
        body, name=name, grid=(nblk,),
        in_specs=[pl.BlockSpec((BLOCK, d), lambda i: (i, 0)),
                  pl.BlockSpec((BLOCK, kw), lambda i: (i, kcol)),
                  pl.BlockSpec((BLOCK, kw), lambda i: (prev(i), kcol)),
                  pl.BlockSpec((BLOCK, kw), lambda i: (i, kcol + 1)),
                  pl.BlockSpec((BLOCK, kw), lambda i: (prev(i), kcol + 1)),
                  pl.BlockSpec((hq, BLOCK, 2 * BLOCK), lambda i: (0, 0, 0)),
                  pl.BlockSpec(memory_space=pltpu.SMEM)],
        out_specs=[pl.BlockSpec((BLOCK, d), lambda i: (i, 0)), pl.BlockSpec((BLOCK, LANES), lambda i: (i, 0))],
        out_shape=[jax.ShapeDtypeStruct((s, d), BF16), jax.ShapeDtypeStruct((s, LANES), F32)],
        compiler_params=_params(1),
    )(qkn, qkn, qkn, proj, proj, bias, sink)


def swa_bwd(qkn, proj, bias, sink, do, o, lse, d, name):
    s = qkn.shape[0]
    hq = d // HEAD_DIM
    hkv = hq // GROUP
    kw = hkv * HEAD_DIM
    nblk = s // BLOCK
    kcol = d // kw
    wide = d + 2 * kw

    def body(q_ref, kc_ref, kp_ref, vc_ref, vp_ref, bias_ref, sink_ref, do_ref, o_ref, lse_ref,
             out_ref, dbias_ref, dsink_ref, carry, fresh):
        i = pl.program_id(0)

        @pl.when(i == 0)
        def _():
            dbias_ref[...] = jnp.zeros_like(dbias_ref)
            dsink_ref[...] = jnp.zeros_like(dsink_ref)
            carry[...] = jnp.zeros_like(carry)

        @pl.when(i == nblk)
        def _():
            fresh[...] = jnp.zeros_like(fresh)

        @pl.when(i < nblk)
        def _():
            mask = _swa_mask(i == 0)
            for kvh in range(hkv):
                cols = slice(kvh * HEAD_DIM, (kvh + 1) * HEAD_DIM)
                k2 = jnp.concatenate([kp_ref[:, cols], kc_ref[:, cols]], axis=0)
                v2 = jnp.concatenate([vp_ref[:, cols], vc_ref[:, cols]], axis=0)
                dk2 = jnp.zeros((2 * BLOCK, HEAD_DIM), F32)
                dv2 = jnp.zeros((2 * BLOCK, HEAD_DIM), F32)
                for g in range(GROUP):
                    h = kvh * GROUP + g
                    hc = slice(h * HEAD_DIM, (h + 1) * HEAD_DIM)
                    q = q_ref[:, hc]
                    dov = do_ref[:, hc]
                    lse_h = lse_ref[:, h:h + 1]
                    sc = jnp.where(mask, _nt(q, k2) + bias_ref[h], NEG)
                    p = jnp.exp(sc - lse_h)
                    delta = jnp.sum(dov.astype(F32) * o_ref[:, hc].astype(F32), axis=-1, keepdims=True)
                    ds = p * (_nt(dov, v2) - delta)
                    dbias_ref[h] += ds
                    dsink_ref[0:1, h:h + 1] += jnp.sum(-jnp.exp(sink_ref[0, h] - lse_h) * delta, axis=0, keepdims=True)
                    dsb = ds.astype(BF16)
                    fresh[0, :, hc] = _nn(dsb, k2)
                    dk2 += _tn(dsb, q)
                    dv2 += _tn(p.astype(BF16), dov)
                kc_cols = slice(d + kvh * HEAD_DIM, d + (kvh + 1) * HEAD_DIM)
                vc_cols = slice(d + kw + kvh * HEAD_DIM, d + kw + (kvh + 1) * HEAD_DIM)
                fresh[0, :, kc_cols] = dk2[BLOCK:]
                fresh[0, :, vc_cols] = dv2[BLOCK:]
                fresh[1, :, kc_cols] = dk2[:BLOCK]
                fresh[1, :, vc_cols] = dv2[:BLOCK]

        lane = lax.broadcasted_iota(jnp.int32, (BLOCK, wide), 1)
        out_ref[...] = carry[...] + jnp.where(lane >= d, fresh[1], 0.0)

        @pl.when(i < nblk)
        def _():
            carry[...] = fresh[0]

    cur = lambda i: jnp.minimum(i, nblk - 1)
    prev = lambda i: jnp.maximum(jnp.minimum(i, nblk - 1) - 1, 0)
    return pl.pallas_call(
        body, name=name, grid=(nblk + 1,),
        in_specs=[pl.BlockSpec((BLOCK, d), lambda i: (cur(i), 0)),
                  pl.BlockSpec((BLOCK, kw), lambda i: (cur(i), kcol)),
                  pl.BlockSpec((BLOCK, kw), lambda i: (prev(i), kcol)),
                  pl.BlockSpec((BLOCK, kw), lambda i: (cur(i), kcol + 1)),
                  pl.BlockSpec((BLOCK, kw), lambda i: (prev(i), kcol + 1)),
                  pl.BlockSpec((hq, BLOCK, 2 * BLOCK), lambda i: (0, 0, 0)),
                  pl.BlockSpec(memory_space=pltpu.SMEM),
                  pl.BlockSpec((BLOCK, d), lambda i: (cur(i), 0)),
                  pl.BlockSpec((BLOCK, d), lambda i: (cur(i), 0)),
                  pl.BlockSpec((BLOCK, LANES), lambda i: (cur(i), 0))],
        out_specs=[pl.BlockSpec((BLOCK, wide), lambda i: (jnp.maximum(i - 1, 0), 0)),
                   pl.BlockSpec((hq, BLOCK, 2 * BLOCK), lambda i: (0, 0, 0)),
                   pl.BlockSpec((8, LANES), lambda i: (0, 0))],
        out_shape=[jax.ShapeDtypeStruct((s, wide), F32), jax.ShapeDtypeStruct((hq, BLOCK, 2 * BLOCK), F32),
                   jax.ShapeDtypeStruct((8, LANES), F32)],
        scratch_shapes=[pltpu.VMEM((BLOCK, wide), F32), pltpu.VMEM((2, BLOCK, wide), F32)],
        compiler_params=_params(1),
    )(qkn, qkn, qkn, proj, proj, bias, sink, do, o, lse)


def _rel_bucket_table():
    qi = np.arange(BLOCK)[:, None] + BLOCK
    kj = np.arange(2 * BLOCK)[None, :]
    n = np.maximum(qi - kj, 0)
    max_exact = REL_BUCKETS // 2
    nf = np.maximum(n, 1).astype(np.float32)
    large = max_exact + (np.log(nf / max_exact) / math.log(REL_MAX_DIST / max_exact)
                         * (REL_BUCKETS - max_exact)).astype(np.int32)
    large = np.minimum(large, REL_BUCKETS - 1)
    return np.where(n < max_exact, n, large).astype(np.int32)


def rel_bias_table(rel_bias, bucket):
    hq = rel_bias.shape[1]

    def body(rb_ref, bucket_ref, out_ref):
        tbl = bucket_ref[...]

        def per_head(h, carry):
            def per_bucket(b, acc):
                return jnp.where(tbl == b, rb_ref[b, h], acc)

            out_ref[h] = lax.fori_loop(0, REL_BUCKETS, per_bucket, jnp.zeros(tbl.shape, F32))
            return carry

        lax.fori_loop(0, hq, per_head, 0)

    return pl.pallas_call(
        body, name="rel_bias_table",
        in_specs=[pl.BlockSpec(memory_space=pltpu.SMEM), pl.BlockSpec(memory_space=pltpu.VMEM)],
        out_specs=pl.BlockSpec(memory_space=pltpu.VMEM),
        out_shape=jax.ShapeDtypeStruct((hq,) + tuple(bucket.shape), F32),
    )(rel_bias, bucket)


def rel_bias_grad(dbias, bucket):
    n_layers, hq = dbias.shape[:2]

    def body(db_ref, bucket_ref, out_ref):
        tbl = bucket_ref[...]

        def per_head(h, carry):
            dsum = db_ref[0, h]
            for a in range(1, n_layers):
                dsum = dsum + db_ref[a, h]

            def per_bucket(b, carry2):
                out_ref[b, h] = jnp.sum(jnp.where(tbl == b, dsum, 0.0))
                return carry2

            return lax.fori_loop(0, REL_BUCKETS, per_bucket, carry)

        lax.fori_loop(0, hq, per_head, 0)

    return pl.pallas_call(
        body, name="rel_bias_grad",
        in_specs=[pl.BlockSpec(memory_space=pltpu.VMEM), pl.BlockSpec(memory_space=pltpu.VMEM)],
        out_specs=pl.BlockSpec(memory_space=pltpu.SMEM),
        out_shape=jax.ShapeDtypeStruct((REL_BUCKETS, hq), F32),
    )(dbias, bucket)


def _split3(v):
    hi = v.astype(BF16)
    r1 = v - hi.astype(F32)
    mid = r1.astype(BF16)
    lo = (r1 - mid.astype(F32)).astype(BF16)
    return hi, mid, lo


def _tri_sum(tri, v):
    hi, mid, lo = _split3(v)
    return _nn(tri, hi) + _nn(tri, mid) + _nn(tri, lo)


def fox_gates(fl, b_f, name):
    s = fl.shape[0]
    t = _row_tile(s)

    def body(fl_ref, b_ref, f_ref, carry):
        @pl.when(pl.program_id(0) == 0)
        def _():
            carry[...] = jnp.zeros_like(carry)

        z = fl_ref[...] + b_ref[...]
        logf = jnp.minimum(z, 0.0) - jnp.log(1.0 + jnp.exp(-jnp.abs(z)))
        r = lax.broadcasted_iota(jnp.int32, (t, t), 0)
        cidx = lax.broadcasted_iota(jnp.int32, (t, t), 1)
        tri = jnp.where(cidx <= r, 1.0, 0.0).astype(BF16)
        f = _tri_sum(tri, logf) + carry[0:1, :]
        f_ref[...] = f
        carry[0:1, :] = f_ref[t - 1:t, :]

    blk = pl.BlockSpec((t, LANES), lambda i: (i, 0))
    return pl.pallas_call(
        body, name=name, grid=(s // t,),
        in_specs=[blk, pl.BlockSpec((1, LANES), lambda i: (0, 0))],
        out_specs=blk, out_shape=jax.ShapeDtypeStruct((s, LANES), F32),
        scratch_shapes=[pltpu.VMEM((8, LANES), F32)],
        compiler_params=_params(1),
    )(fl, b_f)


def fox_gates_bwd(fl, b_f, df_query, df_key, name):
    s = fl.shape[0]
    t = _row_tile(s)
    nb = s // t

    def body(fl_ref, b_ref, dfq_ref, dfk_ref, dfl_ref, db_ref, carry):
        @pl.when(pl.program_id(0) == 0)
        def _():
            carry[...] = jnp.zeros_like(carry)
            db_ref[...] = jnp.zeros_like(db_ref)

        dfv = dfq_ref[...] + dfk_ref[...]
        r = lax.broadcasted_iota(jnp.int32, (t, t), 0)
        cidx = lax.broadcasted_iota(jnp.int32, (t, t), 1)
        tri = jnp.where(cidx >= r, 1.0, 0.0).astype(BF16)
        dlog = _tri_sum(tri, dfv) + carry[0:1, :]
        carry[0:1, :] += jnp.sum(dfv, axis=0, keepdims=True)
        z = fl_ref[...] + b_ref[...]
        dz = dlog * (1.0 - _sigmoid(z))
        dfl_ref[...] = dz.astype(BF16)
        db_ref[0:1, :] += jnp.sum(dz, axis=0, keepdims=True)

    rev = pl.BlockSpec((t, LANES), lambda i: (nb - 1 - i, 0))
    return pl.pallas_call(
        body, name=name, grid=(nb,),
        in_specs=[rev, pl.BlockSpec((1, LANES), lambda i: (0, 0)), rev, rev],
        out_specs=[rev, pl.BlockSpec((8, LANES), lambda i: (0, 0))],
        out_shape=[jax.ShapeDtypeStruct((s, LANES), BF16), jax.ShapeDtypeStruct((8, LANES), F32)],
        scratch_shapes=[pltpu.VMEM((8, LANES), F32)],
        compiler_params=_params(1),
    )(fl, b_f, df_query, df_key)


def fox_fwd(qkn, proj, f_col, f_row, d, name):
    s = qkn.shape[0]
    t = _attn_tile(s)
    n_pairs = d // LANES
    nt = s // t

    def body(q_ref, k_ref, v_ref, fq_ref, fk_ref, o_ref, o32_ref, lse_ref, m_scr, l_scr, acc):
        i, j = pl.program_id(1), pl.program_id(2)

        @pl.when(j == 0)
        def _():
            m_scr[...] = jnp.full_like(m_scr, NEG)
            l_scr[...] = jnp.zeros_like(l_scr)
            acc[...] = jnp.zeros_like(acc)

        @pl.when(j <= i)
        def _():
            krow = lax.broadcasted_iota(jnp.int32, (t, t), 0)
            qcol = lax.broadcasted_iota(jnp.int32, (t, t), 1)
            visible = (krow <= qcol) | (j < i)
            for hh in range(2):
                hc = slice(hh * HEAD_DIM, (hh + 1) * HEAD_DIM)
                st = _nt(k_ref[:, hc], q_ref[:, hc]) + fq_ref[hh] - fk_ref[hh]
                st = jnp.where(visible, st, NEG)
                m_prev = m_scr[hh]
                m_new = jnp.maximum(m_prev, jnp.max(st, axis=0, keepdims=True))
                alpha = jnp.exp(m_prev - m_new)
                pt = jnp.exp(st - m_new)
                l_scr[hh] = alpha * l_scr[hh] + jnp.sum(pt, axis=0, keepdims=True)
                acc[hc, :] = alpha * acc[hc, :] + _tn(v_ref[:, hc], pt.astype(BF16))
                m_scr[hh] = m_new

        @pl.when(j == i)
        def _():
            l_full = jnp.concatenate([jnp.broadcast_to(l_scr[hh], (HEAD_DIM, t)) for hh in range(2)], axis=0)
            ov = (acc[...] / l_full).T
            o_ref[...] = ov.astype(BF16)
            o32_ref[...] = ov
            lse_ref[...] = m_scr[...] + jnp.log(l_scr[...])

    kv = lambda j, i: jnp.minimum(j, i)
    return pl.pallas_call(
        body, name=name, grid=(n_pairs, nt, nt),
        in_specs=[pl.BlockSpec((t, LANES), lambda p, i, j: (i, p)),
                  pl.BlockSpec((t, LANES), lambda p, i, j: (kv(j, i), n_pairs + p)),
                  pl.BlockSpec((t, LANES), lambda p, i, j: (kv(j, i), 2 * n_pairs + p)),
                  pl.BlockSpec((2, 1, t), lambda p, i, j: (p, 0, i)),
                  pl.BlockSpec((2, t, 1), lambda p, i, j: (p, kv(j, i), 0))],
        out_specs=[pl.BlockSpec((t, LANES), lambda p, i, j: (i, p)),
                   pl.BlockSpec((t, LANES), lambda p, i, j: (i, p)),
                   pl.BlockSpec((2, 1, t), lambda p, i, j: (p, 0, i))],
        out_shape=[jax.ShapeDtypeStruct((s, d), BF16), jax.ShapeDtypeStruct((s, d), F32),
                   jax.ShapeDtypeStruct((2 * n_pairs, 1, s), F32)],
        scratch_shapes=[pltpu.VMEM((2, 1, t), F32), pltpu.VMEM((2, 1, t), F32), pltpu.VMEM((2 * HEAD_DIM, t), F32)],
        compiler_params=_params(3),
    )(qkn, qkn, proj, f_row, f_col)


def fox_bwd(qkn, proj, f_col, f_row, lse_row, do, o, d, name):
    s = qkn.shape[0]
    t = _attn_tile(s)
    n_pairs = d // LANES
    nt = s // t

    def body(q_ref, k_ref, v_ref, fk_ref, fq_ref, lse_ref, do_ref, o_ref, out_ref, df_ref, dfq_ref,
             dq_acc, dkv_acc, df_acc, dfq_acc):
        j, i = pl.program_id(1), pl.program_id(2)

        @pl.when((j == 0) & (i == 0))
        def _():
            dq_acc[...] = jnp.zeros_like(dq_acc)
            dfq_acc[...] = jnp.zeros_like(dfq_acc)

        @pl.when(i == 0)
        def _():
            dkv_acc[...] = jnp.zeros_like(dkv_acc)
            df_acc[...] = jnp.zeros_like(df_acc)

        @pl.when(i >= j)
        def _():
            krow = lax.broadcasted_iota(jnp.int32, (t, t), 0)
            qcol = lax.broadcasted_iota(jnp.int32, (t, t), 1)
            visible = (krow <= qcol) | (i > j)
            ones = jnp.ones((8, HEAD_DIM), BF16)
            for hh in range(2):
                hc = slice(hh * HEAD_DIM, (hh + 1) * HEAD_DIM)
                q, k, v, dov = q_ref[:, hc], k_ref[:, hc], v_ref[:, hc], do_ref[:, hc]
                st = _nt(k, q) + fq_ref[hh] - fk_ref[hh]
                pt = jnp.exp(jnp.where(visible, st, NEG) - lse_ref[hh])
                hi, mid, lo = _split3(dov.astype(F32) * o_ref[:, hc])
                delta = jnp.max(_nt(ones, hi) + _nt(ones, mid) + _nt(ones, lo), axis=0, keepdims=True)
                dst = pt * (_nt(v, dov) - delta)
                dsb = dst.astype(BF16)
                dkv_acc[1, :, hc] += _nn(pt.astype(BF16), dov)
                dkv_acc[0, :, hc] += _nn(dsb, q)
                dq_acc[pl.ds(pl.multiple_of(i * t, t), t), hc] += _tn(dsb, k)
                df_acc[hh] -= jnp.sum(dst, axis=-1, keepdims=True)
                dfq_acc[i, hh] += jnp.sum(dst, axis=0, keepdims=True)

        @pl.when(i == nt - 1)
        def _():
            out_ref[0] = dq_acc[pl.ds(pl.multiple_of(j * t, t), t), :]
            out_ref[1] = dkv_acc[0]
            out_ref[2] = dkv_acc[1]
            df_ref[...] = df_acc[...]
            dfq_ref[...] = dfq_acc[j]

    qi = lambda j, i: jnp.maximum(i, j)
    return pl.pallas_call(
        body, name=name, grid=(n_pairs, nt, nt),
        in_specs=[pl.BlockSpec((t, LANES), lambda p, j, i: (qi(j, i), p)),
                  pl.BlockSpec((t, LANES), lambda p, j, i: (j, n_pairs + p)),
                  pl.BlockSpec((t, LANES), lambda p, j, i: (j, 2 * n_pairs + p)),
                  pl.BlockSpec((2, t, 1), lambda p, j, i: (p, j, 0)),
                  pl.BlockSpec((2, 1, t), lambda p, j, i: (p, 0, qi(j, i))),
                  pl.BlockSpec((2, 1, t), lambda p, j, i: (p, 0, qi(j, i))),
                  pl.BlockSpec((t, LANES), lambda p, j, i: (qi(j, i), p)),
                  pl.BlockSpec((t, LANES), lambda p, j, i: (qi(j, i), p))],
        out_specs=[pl.BlockSpec((3, t, LANES), lambda p, j, i: (0, j, p)),
                   pl.BlockSpec((2, t, 1), lambda p, j, i: (p, j, 0)),
                   pl.BlockSpec((2, 1, t), lambda p, j, i: (p, 0, j))],
        out_shape=[jax.ShapeDtypeStruct((3, s, d), F32), jax.ShapeDtypeStruct((2 * n_pairs, s, 1), F32),
                   jax.ShapeDtypeStruct((2 * n_pairs, 1, s), F32)],
        scratch_shapes=[pltpu.VMEM((s, LANES), F32), pltpu.VMEM((2, t, LANES), F32), pltpu.VMEM((2, t, 1), F32),
                        pltpu.VMEM((nt, 2, 1, t), F32)],
        compiler_params=_params(3),
    )(qkn, qkn, proj, f_col, f_row, lse_row, do, o)


def loss_head(y, target):
    s, d = y.shape
    tm = _row_tile(s)

    def body(y_ref, t_ref, dy_ref, loss_ref):
        @pl.when(pl.program_id(0) == 0)
        def _():
            loss_ref[...] = jnp.zeros_like(loss_ref)

        diff = y_ref[...] - t_ref[...]
        dy_ref[...] = diff * (1.0 / d)
        loss_ref[...] += 0.5 * jnp.sum(jnp.mean(diff * diff, axis=-1, keepdims=True), axis=0, keepdims=True)

    row = pl.BlockSpec((tm, d), lambda i: (i, 0))
    return pl.pallas_call(
        body, name="loss_head", grid=(s // tm,),
        in_specs=[row, row],
        out_specs=[row, pl.BlockSpec((8, LANES), lambda i: (0, 0))],
        out_shape=[jax.ShapeDtypeStruct((s, d), F32), jax.ShapeDtypeStruct((8, LANES), F32)],
        compiler_params=_params(1),
    )(y, target)


def ada_mod(c_all, w, b):
    n_layers, d, cols = w.shape

    def body(c_ref, w_ref, b_ref, o_ref):
        cv = c_ref[...]
        o_ref[...] = _nn(cv * _sigmoid(cv), w_ref[...]) + b_ref[...]

    return pl.pallas_call(
        body, name="ada_mod", grid=(n_layers,),
        in_specs=[pl.BlockSpec((N_DEV, d), lambda l: (0, 0)), pl.BlockSpec((None, d, cols), lambda l: (l, 0, 0)),
                  pl.BlockSpec((None, 1, cols), lambda l: (l, 0, 0))],
        out_specs=pl.BlockSpec((None, N_DEV, cols), lambda l: (l, 0, 0)),
        out_shape=jax.ShapeDtypeStruct((n_layers, N_DEV, cols), F32),
        compiler_params=_params(1),
    )(c_all, w, b)


def ada_grad(c_t, dmod):
    d = c_t.shape[0]
    n_layers, _, cols = dmod.shape
    tn = cols // 2

    def body(c_ref, dm_ref, o_ref):
        cv = c_ref[...]
        o_ref[...] = _nn(cv * _sigmoid(cv), dm_ref[...])

    return pl.pallas_call(
        body, name="ada_grad", grid=(n_layers, 2),
        in_specs=[pl.BlockSpec((d, N_DEV), lambda l, n: (0, 0)), pl.BlockSpec((None, N_DEV, tn), lambda l, n: (l, 0, n))],
        out_specs=pl.BlockSpec((None, d, tn), lambda l, n: (l, 0, n)),
        out_shape=jax.ShapeDtypeStruct((n_layers, d, cols), F32),
        compiler_params=_params(2),
    )(c_t, dmod)


def sum_devices(v):
    def body(v_ref, o_ref):
        acc = v_ref[0]
        for k in range(1, N_DEV):
            acc = acc + v_ref[k]
        o_ref[...] = acc

    return pl.pallas_call(body, name="sum_devices", out_shape=jax.ShapeDtypeStruct(v.shape[1:], F32))(v)


def sum_slots(r, own):
    _, rows, cols = r.shape
    tm = 256 if rows % 256 == 0 else rows

    def body(r_ref, own_ref, o_ref):
        o_ref[...] = ((own_ref[...].astype(F32) + r_ref[0].astype(F32)) + r_ref[1].astype(F32)) + r_ref[2].astype(F32)

    return pl.pallas_call(
        body, name="sum_slots", grid=(rows // tm,),
        in_specs=[pl.BlockSpec((N_CHIP - 1, tm, cols), lambda i: (0, i, 0)), pl.BlockSpec((tm, cols), lambda i: (i, 0))],
        out_specs=pl.BlockSpec((tm, cols), lambda i: (i, 0)),
        out_shape=jax.ShapeDtypeStruct((rows, cols), F32),
        compiler_params=_params(1),
    )(r, own)


def adamw(w, m, v, g, g2=None):
    rows, cols = w.shape
    tm = 256 if rows % 256 == 0 else rows
    two = g2 is not None
    c1 = 1.0 - ADAM_B1 ** ADAM_STEP
    c2 = 1.0 - ADAM_B2 ** ADAM_STEP

    def body(w_ref, m_ref, v_ref, g_ref, *rest):
        if two:
            g2_ref, go_ref, d_ref, mo_ref, vo_ref = rest
            gv = g_ref[...] + g2_ref[...]
        else:
            go_ref, d_ref, mo_ref, vo_ref = rest
            gv = g_ref[...]
        mn = ADAM_B1 * m_ref[...] + (1.0 - ADAM_B1) * gv
        vn = ADAM_B2 * v_ref[...] + (1.0 - ADAM_B2) * (gv * gv)
        go_ref[...] = gv
        mo_ref[...] = mn
        vo_ref[...] = vn
        d_ref[...] = -ADAM_LR * ((mn / c1) / (jnp.sqrt(vn / c2) + ADAM_EPS) + ADAM_WD * w_ref[...])

    blk = pl.BlockSpec((tm, cols), lambda i: (i, 0))
    ops = [w, m, v, g] + ([g2] if two else [])
    return pl.pallas_call(
        body, name="adamw", grid=(rows // tm,),
        in_specs=[blk] * len(ops), out_specs=[blk] * 4,
        out_shape=[jax.ShapeDtypeStruct((rows, cols), F32)] * 4,
        compiler_params=_params(1),
    )(*ops)


def _pad_rows(flat):
    n = flat.shape[0]
    rows = -(-n // LANES)
    return jnp.pad(flat, (0, rows * LANES - n)).reshape(rows, LANES)


def _pad_rows8(flat):
    rows = _pad_rows(flat)
    return jnp.pad(rows, ((0, -rows.shape[0] % 8), (0, 0)))


def _col_tiles(n):
    return next(k for k in range(1, n // LANES + 1) if n % (k * LANES) == 0 and n // k <= 1536)


def kernel(x, c, ada_w, ada_b, norm_g, ffn_w13, ffn_w2, rel_bias, swa_w_in, swa_w_out, swa_q_g, swa_k_g, swa_sink, fox_w_in, fox_w_out, fox_b_f, fox_q_g, fox_k_g, loss_target, m_ada_w, m_ada_b, m_norm_g, m_ffn_w13, m_ffn_w2, m_rel_bias, m_swa_w_in, m_swa_w_out, m_swa_q_g, m_swa_k_g, m_swa_sink, m_fox_w_in, m_fox_w_out, m_fox_b_f, m_fox_q_g, m_fox_k_g, v_ada_w, v_ada_b, v_norm_g, v_ffn_w13, v_ffn_w2, v_rel_bias, v_swa_w_in, v_swa_w_out, v_swa_q_g, v_swa_k_g, v_swa_sink, v_fox_w_in, v_fox_w_out, v_fox_b_f, v_fox_q_g, v_fox_k_g):
    ix, iy, ic = lax.axis_index("x"), lax.axis_index("y"), lax.axis_index("c")
    chip = 2 * ix + iy
    dev = 2 * chip + ic
    s, d = x.shape[1:]
    n_layers = ada_w.shape[0]
    n_a, n_b = swa_w_in.shape[0], fox_w_in.shape[0]
    hq = d // HEAD_DIM
    hkv = hq // GROUP
    kw = hkv * HEAD_DIM
    c13 = ffn_w13.shape[-1]
    f = 2 * c13
    r2 = ffn_w2.shape[2]
    cq = norm_g.shape[-1]
    a_in = d + 2 * kw
    fx = fox_w_in.shape[-1]
    b_in = N_CHIP * fx
    b_pad = 3 * d + LANES
    x0 = x[0]

    hello = _pad_rows8(jnp.concatenate([c.reshape(-1), norm_g.reshape(-1)]))
    hello_all = all_gather_rows(hello, "gather_c_norm").reshape(N_DEV, -1)
    c_all = hello_all[:, :d]
    ng = hello_all[::2, d:d + n_layers * 3 * cq].reshape(N_CHIP, n_layers, 3, cq)
    norm_full = jnp.moveaxis(ng, 0, 2).reshape(n_layers, 3, d)

    half_cols = ada_w.shape[-1] // 2
    w_half = lax.dynamic_slice_in_dim(ada_w, ic * half_cols, half_cols, axis=2)
    b_half = lax.dynamic_slice_in_dim(ada_b, dev * half_cols, half_cols, axis=1)[:, None, :]
    mod_part = ada_mod(c_all, w_half, b_half)
    mod_all = all_gather_rows(mod_part.reshape(n_layers * N_DEV, half_cols), "gather_mod")
    mod_all = mod_all.reshape(N_DEV, n_layers, N_DEV, half_cols)
    mod_mine = lax.dynamic_index_in_dim(mod_all, dev, axis=2, keepdims=False)
    mod_mine = jnp.moveaxis(mod_mine, 0, 1).reshape(n_layers, 3, 3, d)
    mv = jnp.concatenate([norm_full[:, :, None, :], mod_mine, jnp.zeros((n_layers, 3, 4, d), F32)], axis=2)

    n_groups = n_layers // 2
    kinds = [("col", c13), ("row", r2), ("col", swa_w_in.shape[-1]), ("slot",), ("slot",), ("slot",)]

    def group_shards(g, barrier_with=None):
        raw = [ffn_w13[2 * g:2 * g + 2], ffn_w2[2 * g:2 * g + 2], swa_w_in[g], swa_w_out[g], fox_w_in[g], fox_w_out[g]]
        if barrier_with is not None:
            raw = lax.optimization_barrier((barrier_with, raw))[1]
        return [r.astype(BF16) for r in raw]

    def group_weights(fulls):
        w13_g, w2_g, wa_in_g, wa_out_g, wb_slots, wb_out_g = fulls
        wb_in_g = jnp.pad(jnp.concatenate([wb_slots[b] for b in range(N_CHIP)], axis=-1), ((0, 0), (0, b_pad - b_in)))
        return dict(w13=w13_g, w2=w2_g, wa_in=wa_in_g, wa_out=wa_out_g.reshape(d, d), wb_in=wb_in_g,
                    wb_out=wb_out_g.reshape(d, d))

    shards = group_shards(0, barrier_with=mv)
    fulls = [place_own(v, k, chip) for v, k in zip(shards, kinds)]
    weights = [group_weights(gather_weights(shards, fulls, kinds))]
    in_flight = []
    for g in range(1, n_groups):
        shards = group_shards(g, barrier_with=weights[0]["w13"])
        fulls = [place_own(v, k, chip) for v, k in zip(shards, kinds)]
        in_flight.append(exchange_start(shards, fulls, kinds, True, f"gather_start_{g}"))
    for *_, token in in_flight:
        mv = mv + token[0, 0]

    bucket = jnp.asarray(_rel_bucket_table())
    bias = rel_bias_table(rel_bias, bucket)
    tm, tw = _row_tile(s), _wide_tile(s)
    n13 = 2 * f // c13
    na_t, nb_t = _col_tiles(a_in), _col_tiles(3 * d)
    wa_t, wb_t = a_in // na_t, 3 * d // nb_t
    gate_blk = 3 * d // LANES

    def ffn_forward(xv, l, half, sub):
        wg, li = weights[l // 2], l % 2
        a, h = modmm(xv, mv, (l, sub), wg["w13"], pl.BlockSpec((None, None, d, c13), lambda i, n: (li, half, 0, n)), n13,
                     jax.ShapeDtypeStruct((2, s, f), BF16),
                     pl.BlockSpec((None, tw, c13), lambda i, n: (n // 2, i, n % 2)), f"ffn_up_{l}_{half}", True)
        xo, y, u = resmm(xv, mv, (l, sub), 0.5, a, pl.BlockSpec((2, tm, c13), lambda i, k: (0, i, k)),
                         wg["w2"], pl.BlockSpec((None, None, c13, d), lambda i, k: (li, half, k, 0)), f // c13, c13,
                         f"ffn_down_{l}_{half}", True)
        return xo, dict(x=xv, h=h, a=a, u=u, y=y)

    saved = []
    xv = x0
    for l in range(n_layers):
        j = l // 2
        if l % 2 == 0 and j >= 1:
            send_sems, recv_sems, shards, fulls, _ = in_flight[j - 1]
            weights.append(group_weights(exchange_wait(send_sems, recv_sems, shards, fulls, xv, kinds, True,
                                                       f"gather_wait_{j}")[1]))
        wg = weights[j]
        xv, s0 = ffn_forward(xv, l, 0, 0)
        if l % 2 == 0:
            proj, h = modmm(xv, mv, (l, 1), wg["wa_in"], pl.BlockSpec((d, wa_t), lambda i, n: (0, n)), na_t,
                            jax.ShapeDtypeStruct((s, a_in), BF16), pl.BlockSpec((tw, wa_t), lambda i, n: (i, n)),
                            f"swa_in_{j}", True)
            gains = jnp.concatenate([jnp.tile(swa_q_g[j] * HEAD_DIM ** -0.5, hq), jnp.tile(swa_k_g[j], hkv)])[None, :]
            qkn = qknorm_fwd(proj, gains, kw, f"swa_qknorm_{j}")
            sink = swa_sink[j][None, :]
            o, lse = swa_fwd(qkn, proj, bias, sink, d, f"swa_attn_{j}")
            s1 = dict(x=xv, h=h, proj=proj, gains=gains, qkn=qkn, sink=sink, o=o, lse=lse)
            w_out = wg["wa_out"]
        else:
            proj, h = modmm(xv, mv, (l, 1), wg["wb_in"], pl.BlockSpec((d, wb_t), lambda i, n: (0, n)), nb_t,
                            jax.ShapeDtypeStruct((s, 3 * d), BF16), pl.BlockSpec((tw, wb_t), lambda i, n: (i, n)),
                            f"fox_in_{j}", True)
            fl, _ = modmm(xv, mv, (l, 1), wg["wb_in"], pl.BlockSpec((d, LANES), lambda i, n: (0, gate_blk)), 1,
                          jax.ShapeDtypeStruct((s, LANES), F32), pl.BlockSpec((tw, LANES), lambda i, n: (i, 0)),
                          f"fox_gate_in_{j}", False)
            b_f = jnp.pad(fox_b_f[j], (0, LANES - hq))[None, :]
            fcum = fox_gates(fl, b_f, f"fox_gates_{j}")
            f_t = fcum[:, :hq].T
            f_col, f_row = f_t[:, :, None], f_t[:, None, :]
            gains = jnp.concatenate([jnp.tile(fox_q_g[j] * HEAD_DIM ** -0.5, hq), jnp.tile(fox_k_g[j], hq)])[None, :]
            qkn = qknorm_fwd(proj, gains, d, f"fox_qknorm_{j}")
            o, o32, lse = fox_fwd(qkn, proj, f_col, f_row, d, f"fox_attn_{j}")
            s1 = dict(x=xv, h=h, proj=proj, gains=gains, qkn=qkn, fl=fl, b_f=b_f, f_col=f_col, f_row=f_row, o=o, o32=o32,
                      lse=lse)
            w_out = wg["wb_out"]
        xv, y = resmm(xv, mv, (l, 1), 1.0, o, pl.BlockSpec((tm, d), lambda i, k: (i, 0)), w_out,
                      pl.BlockSpec((d, d), lambda i, k: (0, 0)), 1, d, f"mixer_out_{l}", False)
        s1["y"] = y
        xv, s2 = ffn_forward(xv, l, 1, 2)
        saved.append((s0, s1, s2))

    dxv, loss_part = loss_head(xv, loss_target[0])
    loss = lax.psum(loss_part[0, 0], ("x", "y", "c"))

    grads = [dict(w13=lax.empty((2, 2, d, 2 * f), BF16), w2=lax.empty((2, 2, f, d), BF16),
                  wa_in=lax.empty((d, a_in), BF16), wa_out=lax.empty((d, d), BF16),
                  wb_in=lax.empty((d, b_pad), BF16), wb_out=lax.empty((d, d), BF16)) for _ in range(n_groups)]
    dmod = [[None] * 3 for _ in range(n_layers)]
    dnorm = [[None] * 3 for _ in range(n_layers)]
    dqk_gain = {}
    dsink, db_f, dbias_tabs = {}, {}, []

    def ffn_backward(dxo, sv, l, half, sub):
        wg, gg, li = weights[l // 2], grads[l // 2], l % 2
        dy, da, dgate = resmm_bwd(dxo, sv["y"], mv, (l, sub), 0.5, wg["w2"],
                                  pl.BlockSpec((None, None, c13, d), lambda i, k: (li, half, k, 0)), f // c13, c13,
                                  f"ffn_down_bwd_{l}_{half}", a=sv["a"])
        gg["w2"] = weight_grad(sv["u"], pl.BlockSpec((tw, c13), lambda m, n, k: (k, m)), dy,
                               pl.BlockSpec((tw, d), lambda m, n, k: (k, 0)), (f // c13, 1), s, c13, d, gg["w2"],
                               pl.BlockSpec((None, None, c13, d), lambda m, n, k: (li, half, m, 0)),
                               f"ffn_w2_grad_{l}_{half}")
        gg["w13"] = weight_grad(sv["h"], pl.BlockSpec((tw, d), lambda m, n, k: (k, 0)), da,
                                pl.BlockSpec((None, tw, c13), lambda m, n, k: (n // 2, k, n % 2)), (1, n13), s, d, c13,
                                gg["w13"], pl.BlockSpec((None, None, d, c13), lambda m, n, k: (li, half, 0, n)),
                                f"ffn_w13_grad_{l}_{half}")
        dx, red = modmm_bwd(da, pl.BlockSpec((None, tm, c13), lambda i, n: (n // 2, i, n % 2)), wg["w13"],
                            pl.BlockSpec((None, None, d, c13), lambda i, n: (li, half, 0, n)), n13, sv["x"], dxo, mv,
                            (l, sub), f"ffn_up_bwd_{l}_{half}")
        dmod[l][sub] = (red[1], red[2], dgate[0])
        dnorm[l][sub] = red[0]
        return dx

    def group_slabs(gg):
        gb_slots = jnp.stack([gg["wb_in"][:, b * fx:(b + 1) * fx] for b in range(N_CHIP)])
        return [gg["w13"], gg["w2"], gg["wa_in"], gg["wa_out"].reshape(N_CHIP, d // N_CHIP, d), gb_slots,
                gg["wb_out"].reshape(N_CHIP, d // N_CHIP, d)]

    shard_shapes = [(2,) + ffn_w13.shape[1:], (2,) + ffn_w2.shape[1:], swa_w_in.shape[1:], swa_w_out.shape[1:],
                    fox_w_in.shape[1:], fox_w_out.shape[1:]]
    scatter_in_flight = {}

    for l in reversed(range(n_layers)):
        j = l // 2
        wg, gg = weights[j], grads[j]
        s0, s1, s2 = saved[l]
        dxv = ffn_backward(dxv, s2, l, 1, 2)
        is_a = l % 2 == 0
        w_out = wg["wa_out"] if is_a else wg["wb_out"]
        dy, do, dgate = resmm_bwd(dxv, s1["y"], mv, (l, 1), 1.0, w_out, pl.BlockSpec((d, d), lambda i, k: (0, 0)),
                                  1, d, f"mixer_out_bwd_{l}")
        out_key = "wa_out" if is_a else "wb_out"
        gg[out_key] = weight_grad(s1["o"], pl.BlockSpec((tw, d), lambda m, n, k: (k, 0)), dy,
                                  pl.BlockSpec((tw, d), lambda m, n, k: (k, 0)), (1, 1), s, d, d, gg[out_key],
                                  pl.BlockSpec((d, d), lambda m, n, k: (0, 0)), f"mixer_out_grad_{l}")
        if is_a:
            d_qkv, dbias_tab, dsk = swa_bwd(s1["qkn"], s1["proj"], bias, s1["sink"], do, s1["o"], s1["lse"], d,
                                            f"swa_attn_bwd_{j}")
            dbias_tabs.append(dbias_tab)
            dsink[j] = dsk[0, :hq]
            dproj, dgain = qknorm_bwd(s1["proj"], s1["gains"], d_qkv, pl.BlockSpec((tm, kw), lambda c_, i: (i, c_)),
                                      a_in, kw, f"swa_qknorm_bwd_{j}")
            gg["wa_in"] = weight_grad(s1["h"], pl.BlockSpec((tw, d), lambda m, n, k: (k, 0)), dproj,
                                      pl.BlockSpec((tw, wa_t), lambda m, n, k: (k, n)), (1, na_t), s, d, wa_t, gg["wa_in"],
                                      pl.BlockSpec((d, wa_t), lambda m, n, k: (0, n)), f"swa_in_grad_{j}")
            dxv, red = modmm_bwd(dproj, pl.BlockSpec((tm, wa_t), lambda i, n: (i, n)), wg["wa_in"],
                                 pl.BlockSpec((d, wa_t), lambda i, n: (0, n)), na_t, s1["x"], dxv, mv, (l, 1),
                                 f"swa_in_bwd_{j}")
            dqk_gain[("a", j)] = (dgain[0, :d].reshape(hq, HEAD_DIM).sum(0) * HEAD_DIM ** -0.5,
                                  dgain[0, d:d + kw].reshape(hkv, HEAD_DIM).sum(0))
        else:
            lse_row = s1["lse"].reshape(hq, 1, s)
            d_qkv, df_col, dfq_row = fox_bwd(s1["qkn"], s1["proj"], s1["f_col"], s1["f_row"], lse_row, do, s1["o32"], d,
                                             f"fox_attn_bwd_{j}")
            lanes_of_heads = lambda a: jnp.pad(a.T, ((0, 0), (0, LANES - hq)))
            dfl, dbf = fox_gates_bwd(s1["fl"], s1["b_f"], lanes_of_heads(dfq_row[:, 0, :]), lanes_of_heads(df_col[:, :, 0]),
                                     f"fox_gates_bwd_{j}")
            db_f[j] = dbf[0, :hq]
            dproj, dgain = qknorm_bwd(s1["proj"], s1["gains"], d_qkv, pl.BlockSpec((None, tm, d), lambda c_, i: (c_, i, 0)),
                                      3 * d, d, f"fox_qknorm_bwd_{j}")
            gg["wb_in"] = weight_grad(s1["h"], pl.BlockSpec((tw, d), lambda m, n, k: (k, 0)), dproj,
                                      pl.BlockSpec((tw, wb_t), lambda m, n, k: (k, n)), (1, nb_t), s, d, wb_t, gg["wb_in"],
                                      pl.BlockSpec((d, wb_t), lambda m, n, k: (0, n)), f"fox_in_grad_{j}")
            gg["wb_in"] = weight_grad(s1["h"], pl.BlockSpec((tw, d), lambda m, n, k: (k, 0)), dfl,
                                      pl.BlockSpec((tw, LANES), lambda m, n, k: (k, 0)), (1, 1), s, d, LANES, gg["wb_in"],
                                      pl.BlockSpec((d, LANES), lambda m, n, k: (0, gate_blk)), f"fox_gate_in_grad_{j}")
            dxv, red = modmm_bwd(dproj, pl.BlockSpec((tm, wb_t), lambda i, n: (i, n)), wg["wb_in"],
                                 pl.BlockSpec((d, wb_t), lambda i, n: (0, n)), nb_t, s1["x"], dxv, mv, (l, 1),
                                 f"fox_in_bwd_{j}",
                                 more=(dfl, pl.BlockSpec((tm, LANES), lambda i, n: (i, 0)),
                                       pl.BlockSpec((d, LANES), lambda i, n: (0, gate_blk))))
            dqk_gain[("b", j)] = (dgain[0, :d].reshape(hq, HEAD_DIM).sum(0) * HEAD_DIM ** -0.5,
                                  dgain[0, d:2 * d].reshape(hq, HEAD_DIM).sum(0))
        dmod[l][1] = (red[1], red[2], dgate[0])
        dnorm[l][1] = red[0]
        dxv = ffn_backward(dxv, s0, l, 0, 0)
        if l % 2 == 0 and j >= 1:
            slabs = group_slabs(gg)
            lands = [lax.empty((N_CHIP - 1,) + tuple(shp), BF16) for shp in shard_shapes]
            scatter_in_flight[j] = exchange_start(slabs, lands, kinds, False, f"scatter_start_{j}")
            mv = mv + scatter_in_flight[j][-1][0, 0]
    grad_x = dxv[None]

    drel = rel_bias_grad(jnp.stack(dbias_tabs), bucket)
    dmod_flat = jnp.stack([jnp.stack([jnp.stack(dmod[l][sub]) for sub in range(3)]) for l in range(n_layers)]).reshape(-1)
    dnorm_flat = jnp.stack([jnp.stack(dnorm[l]) for l in range(n_layers)]).reshape(-1)
    pieces = [dmod_flat, dnorm_flat,
              jnp.stack([dqk_gain[("a", j)][0] for j in range(n_a)]).reshape(-1),
              jnp.stack([dqk_gain[("a", j)][1] for j in range(n_a)]).reshape(-1),
              jnp.stack([dqk_gain[("b", j)][0] for j in range(n_b)]).reshape(-1),
              jnp.stack([dqk_gain[("b", j)][1] for j in range(n_b)]).reshape(-1),
              jnp.stack([dsink[j] for j in range(n_a)]).reshape(-1),
              jnp.stack([db_f[j] for j in range(n_b)]).reshape(-1),
              drel.reshape(-1)]
    rows = [_pad_rows(p) for p in pieces]
    starts = np.cumsum([0] + [r.shape[0] for r in rows])
    total = -(-int(starts[-1]) // 8) * 8
    small = jnp.pad(jnp.concatenate(rows), ((0, total - int(starts[-1])), (0, 0)))
    small_all = all_gather_rows(small, "gather_small_grads").reshape(N_DEV, total, LANES)
    small_sum = sum_devices(small_all)

    def piece(k, shape):
        n = int(np.prod(shape))
        return small_sum[int(starts[k]):int(starts[k + 1])].reshape(-1)[:n].reshape(shape)

    g_ada_b = piece(0, (n_layers, 9 * d))
    g_norm = lax.dynamic_slice_in_dim(piece(1, (n_layers, 3, d)), chip * cq, cq, axis=2)
    g_swa_q, g_swa_k = piece(2, (n_a, HEAD_DIM)), piece(3, (n_a, HEAD_DIM))
    g_fox_q, g_fox_k = piece(4, (n_b, HEAD_DIM)), piece(5, (n_b, HEAD_DIM))
    g_sink, g_bf, g_rel = piece(6, (n_a, hq)), piece(7, (n_b, hq)), piece(8, (REL_BUCKETS, hq))

    dmod_all = small_all[:, :int(starts[1])].reshape(N_DEV, -1)[:, :n_layers * 9 * d].reshape(N_DEV, n_layers, 9 * d)
    ada_cols = ada_w.shape[-1]
    dmod_mine = lax.dynamic_slice_in_dim(jnp.moveaxis(dmod_all, 0, 1), chip * ada_cols, ada_cols, axis=2)
    g_ada_w = ada_grad(c_all.T, dmod_mine)

    sources = {0: group_slabs(grads[0])}
    landed = {0: scatter_grads(sources[0], kinds, shard_shapes)}
    for g, (send_sems, recv_sems, slabs, lands, _) in scatter_in_flight.items():
        sources[g], landed[g] = exchange_wait(send_sems, recv_sems, slabs, lands, landed[0][0], kinds, False,
                                              f"scatter_wait_{g}")

    def group_sum(g, t):
        cols = shard_shapes[t][-1]
        own = own_slab(sources[g][t], kinds[t], chip, shard_shapes[t])
        return sum_slots(landed[g][t].reshape(N_CHIP - 1, -1, cols), own.reshape(-1, cols))

    parts = [jnp.concatenate([group_sum(g, t) for g in range(n_groups)]) for t in range(len(kinds))]
    others = swap_with_sibling(parts)

    def update(w, m, v, g, g2=None):
        w2d = w.reshape(-1, w.shape[-1])
        outs = adamw(w2d, m.reshape(w2d.shape), v.reshape(w2d.shape), g.reshape(w2d.shape) if g2 is None else g, g2)
        return [t.reshape(w.shape) for t in outs]

    big = [(ffn_w13, m_ffn_w13, v_ffn_w13), (ffn_w2, m_ffn_w2, v_ffn_w2), (swa_w_in, m_swa_w_in, v_swa_w_in),
           (swa_w_out, m_swa_w_out, v_swa_w_out), (fox_w_in, m_fox_w_in, v_fox_w_in), (fox_w_out, m_fox_w_out, v_fox_w_out)]
    big_out = [update(w, m, v, p, q) for (w, m, v), p, q in zip(big, parts, others)]
    r_ada_w = update(ada_w, m_ada_w, v_ada_w, g_ada_w)
    r_ada_b = update(ada_b, m_ada_b, v_ada_b, g_ada_b)
    r_norm = update(norm_g, m_norm_g, v_norm_g, g_norm)
    r_rel = update(rel_bias, m_rel_bias, v_rel_bias, g_rel)
    r_swa_q = update(swa_q_g, m_swa_q_g, v_swa_q_g, g_swa_q)
    r_swa_k = update(swa_k_g, m_swa_k_g, v_swa_k_g, g_swa_k)
    r_sink = update(swa_sink, m_swa_sink, v_swa_sink, g_sink)
    r_bf = update(fox_b_f, m_fox_b_f, v_fox_b_f, g_bf)
    r_fox_q = update(fox_q_g, m_fox_q_g, v_fox_q_g, g_fox_q)
    r_fox_k = update(fox_k_g, m_fox_k_g, v_fox_k_g, g_fox_k)
    per_weight = [r_ada_w, r_ada_b, r_norm, big_out[0], big_out[1], r_rel, big_out[2], big_out[3], r_swa_q, r_swa_k,
                  r_sink, big_out[4], big_out[5], r_bf, r_fox_q, r_fox_k]
    return (loss, grad_x, *[r[0] for r in per_weight], *[r[1] for r in per_weight],
            *[r[2] for r in per_weight], *[r[3] for r in per_weight])
```

```python
import math

import numpy as np
import jax
import jax.numpy as jnp
from jax import lax
from jax.experimental import pallas as pl
from jax.experimental.pallas import tpu as pltpu

F32 = jnp.float32
BF16 = jnp.bfloat16
HEAD_DIM = 64
GROUP = 4
BLOCK = 128
REL_BUCKETS = 32
REL_MAX_DIST = 128
EPS = 1e-6
NEG = -1e30
N_CHIP = 4
N_DEV = 8
LANES = 128
VMEM_LIMIT = 52 * 1024 * 1024
ADAM_LR, ADAM_B1, ADAM_B2, ADAM_EPS, ADAM_WD, ADAM_STEP = 0.001, 0.9, 0.999, 1e-08, 0.01, 10
MESH = pl.DeviceIdType.MESH
ANY = pl.BlockSpec(memory_space=pl.ANY)


def _params(n_axes):
    return pltpu.CompilerParams(dimension_semantics=("arbitrary",) * n_axes, vmem_limit_bytes=VMEM_LIMIT)


def _nn(a, b):
    return jnp.dot(a, b, preferred_element_type=F32)


def _nt(a, b):
    return lax.dot_general(a, b, (((1,), (1,)), ((), ())), preferred_element_type=F32)


def _tn(a, b):
    return lax.dot_general(a, b, (((0,), (0,)), ((), ())), preferred_element_type=F32)


def _sigmoid(z):
    return 1.0 / (1.0 + jnp.exp(-z))


def _sigmoid_fast(z):
    return pl.reciprocal(1.0 + jnp.exp(-z), approx=True)


def _row_tile(s):
    return 512 if s >= 2048 else s // 2


def _wide_tile(s):
    return 1024 if s >= 2048 else s // 2


def _attn_tile(s):
    return 512 if s >= 2048 else s // 4


def _position():
    x, y, c = lax.axis_index("x"), lax.axis_index("y"), lax.axis_index("c")
    chips = [(1 - x, y), (x, 1 - y), (1 - x, 1 - y)]
    return x, y, c, chips


def all_gather_rows(v, name):
    m_per, n = v.shape

    def body(x_ref, out_ref, send_sems, recv_sems, local_sem):
        x, y, c, chips = _position()
        me, sibling = (x, y, c), (x, y, 1 - c)

        def rows(px, py, pc):
            return out_ref.at[pl.ds((4 * px + 2 * py + pc) * m_per, m_per), :]

        def copy(k, block, to, src=None):
            return pltpu.make_async_remote_copy(
                src_ref=rows(*block) if src is None else src, dst_ref=rows(*block),
                send_sem=send_sems.at[k], recv_sem=recv_sems.at[k], device_id=to, device_id_type=MESH)

        mine = pltpu.make_async_copy(x_ref, rows(*me), local_sem)
        mine.start()
        first = [copy(0, me, sibling, src=x_ref)]
        first += [copy(1 + j, me, (*chip, c), src=x_ref) for j, chip in enumerate(chips)]
        for cp in first:
            cp.start()
        passed = [copy(4 + j, (*chip, c), sibling) for j, chip in enumerate(chips)]
        for j, chip in enumerate(chips):
            copy(1 + j, (*chip, c), me).wait_recv()
            passed[j].start()
        copy(0, sibling, me).wait_recv()
        for j, chip in enumerate(chips):
            copy(4 + j, (*chip, 1 - c), me).wait_recv()
        for cp in first + passed:
            cp.wait_send()
        mine.wait()

    return pl.pallas_call(
        body, name=name,
        out_shape=jax.ShapeDtypeStruct((N_DEV * m_per, n), v.dtype),
        in_specs=[pl.BlockSpec(memory_space=pltpu.VMEM)],
        out_specs=pl.BlockSpec(memory_space=pltpu.VMEM),
        scratch_shapes=[pltpu.SemaphoreType.DMA((7,)), pltpu.SemaphoreType.DMA((7,)), pltpu.SemaphoreType.DMA],
    )(v)


def _slab(full_ref, kind, b, lead):
    how = kind[0]
    if how == "slot":
        return full_ref.at[b, lead]
    if how == "col":
        w = kind[1]
        idx = (lead,) + (slice(None),) * (len(full_ref.shape) - 2) + (pl.ds(pl.multiple_of(b * w, LANES), w),)
        return full_ref.at[idx]
    h = kind[1]
    idx = (lead,) + (slice(None),) * (len(full_ref.shape) - 3) + (pl.ds(pl.multiple_of(b * h, 8), h), slice(None))
    return full_ref.at[idx]


def _full_shape(shard_shape, kind):
    if kind[0] == "slot":
        return (N_CHIP,) + tuple(shard_shape)
    if kind[0] == "col":
        return tuple(shard_shape[:-1]) + (N_CHIP * shard_shape[-1],)
    return tuple(shard_shape[:-2]) + (N_CHIP * shard_shape[-2], shard_shape[-1])


def _slab_start(shape, kind, b):
    zeros = [0] * len(shape)
    if kind[0] == "slot":
        return [b] + zeros[1:]
    if kind[0] == "col":
        return zeros[:-1] + [b * kind[1]]
    return zeros[:-2] + [b * kind[1], 0]


def cast_and_place(shard, kind, chip, after, name):
    rows, cols = shard.shape[-2:]
    lead = int(np.prod(shard.shape[:-2]))
    if kind[0] == "col":
        full3, where = (lead, rows, N_CHIP * cols), lambda p, b: (p, 0, b[0])
    elif kind[0] == "row":
        full3, where = (lead, N_CHIP * rows, cols), lambda p, b: (p, b[0], 0)
    else:
        full3, where = (N_CHIP * lead, rows, cols), lambda p, b: (b[0] * lead + p, 0, 0)

    def body(b_ref, s_ref, after_ref, o_ref):
        o_ref[...] = s_ref[...].astype(BF16)

    full = pl.pallas_call(
        body, name=name,
        grid_spec=pltpu.PrefetchScalarGridSpec(
            num_scalar_prefetch=1, grid=(lead,),
            in_specs=[pl.BlockSpec((1, rows, cols), lambda p, b: (p, 0, 0)), ANY],
            out_specs=pl.BlockSpec((1, rows, cols), where)),
        out_shape=jax.ShapeDtypeStruct(full3, BF16),
        compiler_params=_params(1),
    )(jnp.reshape(chip, (1,)).astype(jnp.int32), shard.reshape(lead, rows, cols), after)
    return full.reshape(_full_shape(shard.shape, kind))


def own_slab(full, kind, b, shard_shape):
    sizes = (1,) + tuple(shard_shape) if kind[0] == "slot" else tuple(shard_shape)
    return lax.dynamic_slice(full, _slab_start(full.shape, kind, b), sizes).reshape(shard_shape)


def gather_weights(fulls, kinds):
    n = len(fulls)

    def body(*refs):
        outs = refs[n:2 * n]
        send_sems, recv_sems = refs[2 * n:]
        x, y, c, chips = _position()
        b_me = 2 * x + y
        sibling = (x, y, 1 - c)
        sends = []

        def halves(t):
            lead = outs[t].shape[1] if kinds[t][0] == "slot" else outs[t].shape[0]
            return pl.ds(c * (lead // 2), lead // 2), pl.ds((1 - c) * (lead // 2), lead // 2)

        for t in range(n):
            mine, _ = halves(t)
            own = _slab(outs[t], kinds[t], b_me, mine)
            for j, chip in enumerate(chips):
                cp = pltpu.make_async_remote_copy(
                    src_ref=own, dst_ref=own,
                    send_sem=send_sems.at[6 * t + j], recv_sem=recv_sems.at[6 * t + j],
                    device_id=(*chip, c), device_id_type=MESH)
                cp.start()
                sends.append(cp)
        for t in range(n):
            mine, _ = halves(t)
            for j, chip in enumerate(chips):
                landed = _slab(outs[t], kinds[t], 2 * chip[0] + chip[1], mine)
                pltpu.make_async_remote_copy(
                    src_ref=landed, dst_ref=landed, send_sem=send_sems.at[6 * t + j], recv_sem=recv_sems.at[6 * t + j],
                    device_id=(*chip, c), device_id_type=MESH).wait_recv()
                cp = pltpu.make_async_remote_copy(
                    src_ref=landed, dst_ref=landed, send_sem=send_sems.at[6 * t + 3 + j],
                    recv_sem=recv_sems.at[6 * t + 3 + j], device_id=sibling, device_id_type=MESH)
                cp.start()
                sends.append(cp)
        for t in range(n):
            _, theirs = halves(t)
            for j, chip in enumerate(chips):
                landed = _slab(outs[t], kinds[t], 2 * chip[0] + chip[1], theirs)
                pltpu.make_async_remote_copy(
                    src_ref=landed, dst_ref=landed, send_sem=send_sems.at[6 * t + 3 + j],
                    recv_sem=recv_sems.at[6 * t + 3 + j], device_id=sibling, device_id_type=MESH).wait_recv()
        for cp in sends:
            cp.wait_send()

    return pl.pallas_call(
        body, name="gather_weights",
        out_shape=[jax.ShapeDtypeStruct(v.shape, v.dtype) for v in fulls],
        in_specs=[ANY] * n, out_specs=[ANY] * n,
        input_output_aliases={t: t for t in range(n)},
        scratch_shapes=[pltpu.SemaphoreType.DMA((6 * n,)), pltpu.SemaphoreType.DMA((6 * n,))],
    )(*fulls)


def scatter_grads(grads, kinds, shard_shapes):
    n = len(grads)

    def body(*refs):
        ins, outs = refs[:n], refs[n:2 * n]
        send_sems, recv_sems = refs[2 * n:]
        x, y, c, chips = _position()
        sends = []
        for t in range(n):
            whole = _whole(ins[t], kinds[t])
            for j, chip in enumerate(chips):
                cp = pltpu.make_async_remote_copy(
                    src_ref=_slab(ins[t], kinds[t], 2 * chip[0] + chip[1], whole), dst_ref=outs[t].at[j],
                    send_sem=send_sems.at[3 * t + j], recv_sem=recv_sems.at[3 * t + j],
                    device_id=(*chip, c), device_id_type=MESH)
                cp.start()
                sends.append(cp)
        for t in range(n):
            for j, chip in enumerate(chips):
                pltpu.make_async_remote_copy(
                    src_ref=outs[t].at[j], dst_ref=outs[t].at[j], send_sem=send_sems.at[3 * t + j],
                    recv_sem=recv_sems.at[3 * t + j], device_id=(*chip, c), device_id_type=MESH).wait_recv()
        for cp in sends:
            cp.wait_send()

    return pl.pallas_call(
        body, name="scatter_grads",
        out_shape=[jax.ShapeDtypeStruct((N_CHIP - 1,) + tuple(s), g.dtype) for g, s in zip(grads, shard_shapes)],
        in_specs=[ANY] * n, out_specs=[ANY] * n,
        scratch_shapes=[pltpu.SemaphoreType.DMA((3 * n,)), pltpu.SemaphoreType.DMA((3 * n,))],
    )(*grads)


def _whole(ref, kind):
    return pl.ds(0, ref.shape[1] if kind[0] == "slot" else ref.shape[0])


def _exchange_copies(srcs, lands, kinds, gather, send_sems, recv_sems):
    x, y, c, chips = _position()
    b_me = 2 * x + y
    out = []
    for t in range(len(lands)):
        for j, chip in enumerate(chips):
            b_j = 2 * chip[0] + chip[1]
            if gather:
                src = sent_to = _slab(lands[t], kinds[t], b_me, _whole(lands[t], kinds[t]))
                arrives = _slab(lands[t], kinds[t], b_j, _whole(lands[t], kinds[t]))
            else:
                src = _slab(srcs[t], kinds[t], b_j, _whole(srcs[t], kinds[t]))
                sent_to = arrives = lands[t].at[j]
            k = 3 * t + j
            send = pltpu.make_async_remote_copy(src_ref=src, dst_ref=sent_to, send_sem=send_sems.at[k],
                                                recv_sem=recv_sems.at[k], device_id=(*chip, c), device_id_type=MESH)
            recv = pltpu.make_async_remote_copy(src_ref=src, dst_ref=arrives, send_sem=send_sems.at[k],
                                                recv_sem=recv_sems.at[k], device_id=(*chip, c), device_id_type=MESH)
            out.append((send, recv))
    return out


def exchange_start(srcs, lands, kinds, gather, name):
    ns, nl = len(srcs), len(lands)
    hbm = pl.BlockSpec(memory_space=pltpu.HBM)

    def body(*refs):
        ins, lnd = refs[:ns], refs[ns:ns + nl]
        send_sems, recv_sems = refs[ns + nl], refs[ns + nl + 1]
        token = refs[-1]
        for send, _ in _exchange_copies(ins, lnd, kinds, gather, send_sems, recv_sems):
            send.start()
        token[...] = jnp.zeros_like(token)

    ops = [pltpu.with_memory_space_constraint(v, pltpu.HBM) for v in (*srcs, *lands)]
    res = pl.pallas_call(
        body, name=name,
        out_shape=(pltpu.SemaphoreType.DMA((3 * nl,)), pltpu.SemaphoreType.DMA((3 * nl,)),
                   *[pltpu.HBM(v.shape, v.dtype) for v in ops], jax.ShapeDtypeStruct((8, LANES), F32)),
        in_specs=[hbm] * (ns + nl),
        out_specs=(pl.BlockSpec(memory_space=pltpu.SEMAPHORE), pl.BlockSpec(memory_space=pltpu.SEMAPHORE),
                   *[hbm] * (ns + nl), pl.BlockSpec(memory_space=pltpu.VMEM)),
        input_output_aliases={t: 2 + t for t in range(ns + nl)},
        compiler_params=pltpu.CompilerParams(has_side_effects=pltpu.SideEffectType.DATAFLOW_SIDE_EFFECTING),
    )(*ops)
    return res[0], res[1], list(res[2:2 + ns]), list(res[2 + ns:2 + ns + nl]), res[-1]


def exchange_wait(send_sems, recv_sems, srcs, lands, after, kinds, gather, name):
    ns, nl = len(srcs), len(lands)
    hbm = pl.BlockSpec(memory_space=pltpu.HBM)

    def body(*refs):
        ins, lnd = refs[:ns], refs[ns:ns + nl]
        ssem, rsem = refs[ns + nl], refs[ns + nl + 1]
        for send, recv in _exchange_copies(ins, lnd, kinds, gather, ssem, rsem):
            send.wait_send()
            recv.wait_recv()

    res = pl.pallas_call(
        body, name=name,
        out_shape=tuple(pltpu.HBM(v.shape, v.dtype) for v in (*srcs, *lands)),
        in_specs=[hbm] * (ns + nl) + [pl.BlockSpec(memory_space=pltpu.SEMAPHORE)] * 2 + [ANY],
        out_specs=tuple([hbm] * (ns + nl)),
        input_output_aliases={t: t for t in range(ns + nl)},
        compiler_params=pltpu.CompilerParams(has_side_effects=pltpu.SideEffectType.DATAFLOW_SIDE_EFFECTING),
    )(*srcs, *lands, send_sems, recv_sems, after)
    return list(res[:ns]), list(res[ns:])


def swap_with_sibling(parts):
    n = len(parts)

    def body(*refs):
        ins, outs = refs[:n], refs[n:2 * n]
        send_sems, recv_sems = refs[2 * n:]
        x, y, c, _ = _position()
        cps = []
        for t in range(n):
            cp = pltpu.make_async_remote_copy(
                src_ref=ins[t], dst_ref=outs[t], send_sem=send_sems.at[t], recv_sem=recv_sems.at[t],
                device_id=(x, y, 1 - c), device_id_type=MESH)
            cp.start()
            cps.append(cp)
        for cp in cps:
            cp.wait_recv()
        for cp in cps:
            cp.wait_send()

    return pl.pallas_call(
        body, name="swap_with_sibling",
        out_shape=[jax.ShapeDtypeStruct(p.shape, p.dtype) for p in parts],
        in_specs=[ANY] * n, out_specs=[ANY] * n,
        scratch_shapes=[pltpu.SemaphoreType.DMA((n,)), pltpu.SemaphoreType.DMA((n,))],
    )(*parts)


def _modulated(xv, mv_ref):
    g, shift, scale = mv_ref[0:1, :], mv_ref[1:2, :], mv_ref[2:3, :]
    r = lax.rsqrt(jnp.mean(xv * xv, axis=-1, keepdims=True) + EPS)
    xhat = xv * r
    xn = xhat * g
    return xn * (1.0 + scale) + shift, xhat, xn, r, g, scale


def modmm(x, mv, mv_idx, w, w_spec, n_tiles, out_shape, out_spec, name, want_h):
    s, d = x.shape
    tm = _wide_tile(s)

    def body(x_ref, mv_ref, w_ref, *rest):
        if want_h:
            out_ref, h_ref, h_scr = rest
        else:
            out_ref, h_scr = rest

        @pl.when(pl.program_id(1) == 0)
        def _():
            h = _modulated(x_ref[...], mv_ref)[0].astype(BF16)
            h_scr[...] = h
            if want_h:
                h_ref[...] = h

        out_ref[...] = _nn(h_scr[...], w_ref[...]).astype(out_ref.dtype)

    out_shapes = [out_shape]
    out_specs = [out_spec]
    if want_h:
        out_shapes.append(jax.ShapeDtypeStruct((s, d), BF16))
        out_specs.append(pl.BlockSpec((tm, d), lambda i, n: (i, 0)))
    res = pl.pallas_call(
        body, name=name, grid=(s // tm, n_tiles),
        in_specs=[pl.BlockSpec((tm, d), lambda i, n: (i, 0)),
                  pl.BlockSpec((None, None, 8, d), lambda i, n: (*mv_idx, 0, 0)), w_spec],
        out_specs=out_specs, out_shape=out_shapes,
        scratch_shapes=[pltpu.VMEM((tm, d), BF16)],
        compiler_params=_params(2),
    )(x, mv, w)
    return res if want_h else (res[0], None)


def resmm(x, mv, mv_idx, coef, lhs, lhs_spec, w, w_spec, k_tiles, tk, name, ffn):
    s, d = x.shape
    tm = _row_tile(s)
    kdim = k_tiles * tk

    def body(x_ref, mv_ref, lhs_ref, w_ref, xo_ref, y_ref, *rest):
        if ffn:
            u_ref, acc = rest
        else:
            (acc,) = rest
        k = pl.program_id(1)

        @pl.when(k == 0)
        def _():
            acc[...] = jnp.zeros_like(acc)

        if ffn:
            ag = lhs_ref[0].astype(F32)
            au = lhs_ref[1].astype(F32)
            left = (ag * _sigmoid_fast(ag) * au).astype(BF16)
            u_ref[...] = left
        else:
            left = lhs_ref[...]
        acc[...] += _nn(left, w_ref[...])

        @pl.when(k == k_tiles - 1)
        def _():
            y = acc[...]
            y_ref[...] = y.astype(BF16)
            xo_ref[...] = x_ref[...] + (coef * mv_ref[3:4, :]) * y

    row = pl.BlockSpec((tm, d), lambda i, k: (i, 0))
    out_shapes = [jax.ShapeDtypeStruct((s, d), F32), jax.ShapeDtypeStruct((s, d), BF16)]
    out_specs = [row, row]
    if ffn:
        out_shapes.append(jax.ShapeDtypeStruct((s, kdim), BF16))
        out_specs.append(pl.BlockSpec((tm, tk), lambda i, k: (i, k)))
    return pl.pallas_call(
        body, name=name, grid=(s // tm, k_tiles),
        in_specs=[row, pl.BlockSpec((None, None, 8, d), lambda i, k: (*mv_idx, 0, 0)), lhs_spec, w_spec],
        out_specs=out_specs, out_shape=out_shapes,
        scratch_shapes=[pltpu.VMEM((tm, d), F32)],
        compiler_params=_params(2),
    )(x, mv, lhs, w)


def resmm_bwd(dxo, y, mv, mv_idx, coef, w, w_spec, k_tiles, tk, name, a=None):
    s, d = dxo.shape
    tm = _row_tile(s)
    kdim = k_tiles * tk
    ffn = a is not None

    def body(dxo_ref, y_ref, mv_ref, w_ref, *rest):
        if ffn:
            a_ref, dy_ref, dl_ref, dgate_ref, dy_scr = rest
        else:
            dy_ref, dl_ref, dgate_ref, dy_scr = rest
        i, k = pl.program_id(0), pl.program_id(1)

        @pl.when((i == 0) & (k == 0))
        def _():
            dgate_ref[...] = jnp.zeros_like(dgate_ref)

        @pl.when(k == 0)
        def _():
            dxv = dxo_ref[...]
            dy = ((coef * mv_ref[3:4, :]) * dxv).astype(BF16)
            dy_scr[...] = dy
            dy_ref[...] = dy
            dgate_ref[0:1, :] += jnp.sum(coef * dxv * y_ref[...].astype(F32), axis=0, keepdims=True)

        dl = _nt(dy_scr[...], w_ref[...])
        if ffn:
            ag = a_ref[0].astype(F32)
            au = a_ref[1].astype(F32)
            sg = _sigmoid_fast(ag)
            dl_ref[0] =(dl * au * (sg * (1.0 + ag * (1.0 - sg)))).astype(BF16)
            dl_ref[1] = (dl * (ag * sg)).astype(BF16)
        else:
            dl_ref[...] = dl.astype(BF16)

    row = pl.BlockSpec((tm, d), lambda i, k: (i, 0))
    in_specs = [row, row, pl.BlockSpec((None, None, 8, d), lambda i, k: (*mv_idx, 0, 0)), w_spec]
    ops = [dxo, y, mv, w]
    if ffn:
        in_specs.append(pl.BlockSpec((2, tm, tk), lambda i, k: (0, i, k)))
        ops.append(a)
        dl_shape = jax.ShapeDtypeStruct((2, s, kdim), BF16)
        dl_spec = pl.BlockSpec((2, tm, tk), lambda i, k: (0, i, k))
    else:
        dl_shape = jax.ShapeDtypeStruct((s, kdim), BF16)
        dl_spec = pl.BlockSpec((tm, tk), lambda i, k: (i, k))
    return pl.pallas_call(
        body, name=name, grid=(s // tm, k_tiles),
        in_specs=in_specs,
        out_specs=[row, dl_spec, pl.BlockSpec((8, d), lambda i, k: (0, 0))],
        out_shape=[jax.ShapeDtypeStruct((s, d), BF16), dl_shape, jax.ShapeDtypeStruct((8, d), F32)],
        scratch_shapes=[pltpu.VMEM((tm, d), BF16)],
        compiler_params=_params(2),
    )(*ops)


def modmm_bwd(dl, dl_spec, w, w_spec, n_tiles, x, dxo, mv, mv_idx, name, more=None):
    s, d = x.shape
    tm = _row_tile(s)

    def body(dl_ref, w_ref, x_ref, dxo_ref, mv_ref, *rest):
        if more is not None:
            dl2_ref, w2_ref, dx_ref, red_ref, acc = rest
        else:
            dx_ref, red_ref, acc = rest
        i, n = pl.program_id(0), pl.program_id(1)

        @pl.when((i == 0) & (n == 0))
        def _():
            red_ref[...] = jnp.zeros_like(red_ref)

        @pl.when(n == 0)
        def _():
            if more is not None:
                acc[...] = _nt(dl2_ref[...], w2_ref[...])
            else:
                acc[...] = jnp.zeros_like(acc)

        acc[...] += _nt(dl_ref[...], w_ref[...])

        @pl.when(n == n_tiles - 1)
        def _():
            dh = acc[...]
            _, xhat, xn, r, g, scale = _modulated(x_ref[...], mv_ref)
            dxn = dh * (1.0 + scale)
            red_ref[0:1, :] += jnp.sum(dxn * xhat, axis=0, keepdims=True)
            red_ref[1:2, :] += jnp.sum(dh, axis=0, keepdims=True)
            red_ref[2:3, :] += jnp.sum(dh * xn, axis=0, keepdims=True)
            gd = dxn * g
            dx_ref[...] = dxo_ref[...] + r * (gd - xhat * jnp.mean(gd * xhat, axis=-1, keepdims=True))

    row = pl.BlockSpec((tm, d), lambda i, n: (i, 0))
    in_specs = [dl_spec, w_spec, row, row, pl.BlockSpec((None, None, 8, d), lambda i, n: (*mv_idx, 0, 0))]
    ops = [dl, w, x, dxo, mv]
    if more is not None:
        in_specs += [more[1], more[2]]
        ops += [more[0], w]
    return pl.pallas_call(
        body, name=name, grid=(s // tm, n_tiles),
        in_specs=in_specs,
        out_specs=[row, pl.BlockSpec((8, d), lambda i, n: (0, 0))],
        out_shape=[jax.ShapeDtypeStruct((s, d), F32), jax.ShapeDtypeStruct((8, d), F32)],
        scratch_shapes=[pltpu.VMEM((tm, d), F32)],
        compiler_params=_params(2),
    )(*ops)


def weight_grad(a, a_spec, b, b_spec, grid_mn, s, bm, bn, dest, out_spec, name):
    tk = _wide_tile(s)
    k_tiles = s // tk

    def body(a_ref, b_ref, dest_ref, out_ref, acc):
        k = pl.program_id(2)

        @pl.when(k == 0)
        def _():
            acc[...] = jnp.zeros_like(acc)

        acc[...] += _tn(a_ref[...], b_ref[...])

        @pl.when(k == k_tiles - 1)
        def _():
            out_ref[...] = acc[...].astype(out_ref.dtype)

    return pl.pallas_call(
        body, name=name, grid=(*grid_mn, k_tiles),
        in_specs=[a_spec, b_spec, ANY], out_specs=out_spec,
        out_shape=jax.ShapeDtypeStruct(dest.shape, dest.dtype),
        input_output_aliases={2: 0},
        scratch_shapes=[pltpu.VMEM((bm, bn), F32)],
        compiler_params=_params(3),
    )(a, b, dest)


def _head_mean(v):
    lane = lax.broadcasted_iota(jnp.int32, v.shape, 1)
    lo = jnp.sum(jnp.where(lane < HEAD_DIM, v, 0.0), axis=-1, keepdims=True)
    hi = jnp.sum(v, axis=-1, keepdims=True) - lo
    return jnp.where(lane < HEAD_DIM, lo, hi) * (1.0 / HEAD_DIM)


def qknorm_fwd(proj, gains, width, name):
    s = proj.shape[0]
    nqk = gains.shape[1]
    tm = _row_tile(s)

    def body(p_ref, g_ref, o_ref):
        for cc in range(width // LANES):
            sl = slice(cc * LANES, (cc + 1) * LANES)
            xv = p_ref[:, sl].astype(F32)
            r = lax.rsqrt(_head_mean(xv * xv) + EPS)
            o_ref[:, sl] = (xv * r * g_ref[:, sl]).astype(BF16)

    blk = pl.BlockSpec((tm, width), lambda i, c: (i, c))
    return pl.pallas_call(
        body, name=name, grid=(s // tm, nqk // width),
        in_specs=[blk, pl.BlockSpec((1, width), lambda i, c: (0, c))],
        out_specs=blk, out_shape=jax.ShapeDtypeStruct((s, nqk), BF16),
        compiler_params=_params(2),
    )(proj, gains)


def qknorm_bwd(proj, gains, d, d_spec, n_cols, width, name):
    s = proj.shape[0]
    nqk = gains.shape[1] // width
    n_blocks = n_cols // width
    tm = _row_tile(s)

    def body(p_ref, g_ref, d_ref, o_ref, dg_ref):
        c, i = pl.program_id(0), pl.program_id(1)

        @pl.when(i == 0)
        def _():
            dg_ref[...] = jnp.zeros_like(dg_ref)

        @pl.when(c < nqk)
        def _():
            for cc in range(width // LANES):
                sl = slice(cc * LANES, (cc + 1) * LANES)
                xv = p_ref[:, sl].astype(F32)
                r = lax.rsqrt(_head_mean(xv * xv) + EPS)
                xhat = xv * r
                dv = d_ref[:, sl]
                gd = dv * g_ref[:, sl]
                o_ref[:, sl] = (r * (gd - xhat * _head_mean(gd * xhat))).astype(BF16)
                dg_ref[0:1, sl] += jnp.sum(dv * xhat, axis=0, keepdims=True)

        @pl.when(c >= nqk)
        def _():
            o_ref[...] = d_ref[...].astype(BF16)

    return pl.pallas_call(
        body, name=name, grid=(n_blocks, s // tm),
        in_specs=[pl.BlockSpec((tm, width), lambda c, i: (i, c)),
                  pl.BlockSpec((1, width), lambda c, i: (0, jnp.minimum(c, nqk - 1))), d_spec],
        out_specs=[pl.BlockSpec((tm, width), lambda c, i: (i, c)), pl.BlockSpec((8, width), lambda c, i: (0, c))],
        out_shape=[jax.ShapeDtypeStruct((s, n_cols), BF16), jax.ShapeDtypeStruct((8, n_cols), F32)],
        compiler_params=_params(2),
    )(proj, gains, d)


def _swa_mask(first):
    qi = lax.broadcasted_iota(jnp.int32, (BLOCK, 2 * BLOCK), 0) + BLOCK
    kj = lax.broadcasted_iota(jnp.int32, (BLOCK, 2 * BLOCK), 1)
    dist = qi - kj
    return (dist >= 0) & (dist < BLOCK) & ((kj >= BLOCK) | jnp.logical_not(first))


def swa_fwd(qkn, proj, bias, sink, d, name):
    s = qkn.shape[0]
    hq = d // HEAD_DIM
    hkv = hq // GROUP
    kw = hkv * HEAD_DIM
    nblk = s // BLOCK
    kcol = d // kw

    def body(q_ref, kc_ref, kp_ref, vc_ref, vp_ref, bias_ref, sink_ref, o_ref, lse_ref):
        mask = _swa_mask(pl.program_id(0) == 0)
        lse_ref[...] = jnp.zeros_like(lse_ref)
        for kvh in range(hkv):
            cols = slice(kvh * HEAD_DIM, (kvh + 1) * HEAD_DIM)
            k2 = jnp.concatenate([kp_ref[:, cols], kc_ref[:, cols]], axis=0)
            v2 = jnp.concatenate([vp_ref[:, cols], vc_ref[:, cols]], axis=0)
            for g in range(GROUP):
                h = kvh * GROUP + g
                hc = slice(h * HEAD_DIM, (h + 1) * HEAD_DIM)
                sc = jnp.where(mask, _nt(q_ref[:, hc], k2) + bias_ref[h], NEG)
                sk = sink_ref[0, h]
                m = jnp.maximum(jnp.max(sc, axis=-1, keepdims=True), sk)
                p = jnp.exp(sc - m)
                denom = jnp.sum(p, axis=-1, keepdims=True) + jnp.exp(sk - m)
                o_ref[:, hc] = (_nn(p.astype(BF16), v2) / denom).astype(BF16)
                lse_ref[:, h:h + 1] = m + jnp.log(denom)

    prev = lambda i: jnp.maximum(i - 1, 0)
    return pl.pallas_call(
        body, name=name, grid=(nblk,),
        in_specs=[pl.BlockSpec((BLOCK, d), lambda i: (i, 0)),
                  pl.BlockSpec((BLOCK, kw), lambda i: (i, kcol)),
                  pl.BlockSpec((BLOCK, kw), lambda i: (prev(i), kcol)),
                  pl.BlockSpec((BLOCK, kw), lambda i: (i, kcol + 1)),
                  pl.BlockSpec((BLOCK, kw), lambda i: (prev(i), kcol + 1)),
                  pl.BlockSpec((hq, BLOCK, 2 * BLOCK), lambda i: (0, 0, 0)),
                  pl.BlockSpec(memory_space=pltpu.SMEM)],
        out_specs=[pl.BlockSpec((BLOCK, d), lambda i: (i, 0)), pl.BlockSpec((BLOCK, LANES), lambda i: (i, 0))],
        out_shape=[jax.ShapeDtypeStruct((s, d), BF16), jax.ShapeDtypeStruct((s, LANES), F32)],
        compiler_params=_params(1),
    )(qkn, qkn, qkn, proj, proj, bias, sink)


def swa_bwd(qkn, proj, bias, sink, do, o, lse, d, name):
    s = qkn.shape[0]
    hq = d // HEAD_DIM
    hkv = hq // GROUP
    kw = hkv * HEAD_DIM
    nblk = s // BLOCK
    kcol = d // kw
    wide = d + 2 * kw

    def body(q_ref, kc_ref, kp_ref, vc_ref, vp_ref, bias_ref, sink_ref, do_ref, o_ref, lse_ref,
             out_ref, dbias_ref, dsink_ref, carry, fresh):
        i = pl.program_id(0)

        @pl.when(i == 0)
        def _():
            dbias_ref[...] = jnp.zeros_like(dbias_ref)
            dsink_ref[...] = jnp.zeros_like(dsink_ref)
            carry[...] = jnp.zeros_like(carry)

        @pl.when(i == nblk)
        def _():
            fresh[...] = jnp.zeros_like(fresh)

        @pl.when(i < nblk)
        def _():
            mask = _swa_mask(i == 0)
            for kvh in range(hkv):
                cols = slice(kvh * HEAD_DIM, (kvh + 1) * HEAD_DIM)
                k2 = jnp.concatenate([kp_ref[:, cols], kc_ref[:, cols]], axis=0)
                v2 = jnp.concatenate([vp_ref[:, cols], vc_ref[:, cols]], axis=0)
                dk2 = jnp.zeros((2 * BLOCK, HEAD_DIM), F32)
                dv2 = jnp.zeros((2 * BLOCK, HEAD_DIM), F32)
                for g in range(GROUP):
                    h = kvh * GROUP + g
                    hc = slice(h * HEAD_DIM, (h + 1) * HEAD_DIM)
                    q = q_ref[:, hc]
                    dov = do_ref[:, hc]
                    lse_h = lse_ref[:, h:h + 1]
                    sc = jnp.where(mask, _nt(q, k2) + bias_ref[h], NEG)
                    p = jnp.exp(sc - lse_h)
                    delta = jnp.sum(dov.astype(F32) * o_ref[:, hc].astype(F32), axis=-1, keepdims=True)
                    ds = p * (_nt(dov, v2) - delta)
                    dbias_ref[h] += ds
                    dsink_ref[0:1, h:h + 1] += jnp.sum(-jnp.exp(sink_ref[0, h] - lse_h) * delta, axis=0, keepdims=True)
                    dsb = ds.astype(BF16)
                    fresh[0, :, hc] = _nn(dsb, k2)
                    dk2 += _tn(dsb, q)
                    dv2 += _tn(p.astype(BF16), dov)
                kc_cols = slice(d + kvh * HEAD_DIM, d + (kvh + 1) * HEAD_DIM)
                vc_cols = slice(d + kw + kvh * HEAD_DIM, d + kw + (kvh + 1) * HEAD_DIM)
                fresh[0, :, kc_cols] = dk2[BLOCK:]
                fresh[0, :, vc_cols] = dv2[BLOCK:]
                fresh[1, :, kc_cols] = dk2[:BLOCK]
                fresh[1, :, vc_cols] = dv2[:BLOCK]

        lane = lax.broadcasted_iota(jnp.int32, (BLOCK, wide), 1)
        out_ref[...] = carry[...] + jnp.where(lane >= d, fresh[1], 0.0)

        @pl.when(i < nblk)
        def _():
            carry[...] = fresh[0]

    cur = lambda i: jnp.minimum(i, nblk - 1)
    prev = lambda i: jnp.maximum(jnp.minimum(i, nblk - 1) - 1, 0)
    return pl.pallas_call(
        body, name=name, grid=(nblk + 1,),
        in_specs=[pl.BlockSpec((BLOCK, d), lambda i: (cur(i), 0)),
                  pl.BlockSpec((BLOCK, kw), lambda i: (cur(i), kcol)),
                  pl.BlockSpec((BLOCK, kw), lambda i: (prev(i), kcol)),
                  pl.BlockSpec((BLOCK, kw), lambda i: (cur(i), kcol + 1)),
                  pl.BlockSpec((BLOCK, kw), lambda i: (prev(i), kcol + 1)),
                  pl.BlockSpec((hq, BLOCK, 2 * BLOCK), lambda i: (0, 0, 0)),
                  pl.BlockSpec(memory_space=pltpu.SMEM),
                  pl.BlockSpec((BLOCK, d), lambda i: (cur(i), 0)),
                  pl.BlockSpec((BLOCK, d), lambda i: (cur(i), 0)),
                  pl.BlockSpec((BLOCK, LANES), lambda i: (cur(i), 0))],
        out_specs=[pl.BlockSpec((BLOCK, wide), lambda i: (jnp.maximum(i - 1, 0), 0)),
                   pl.BlockSpec((hq, BLOCK, 2 * BLOCK), lambda i: (0, 0, 0)),
                   pl.BlockSpec((8, LANES), lambda i: (0, 0))],
        out_shape=[jax.ShapeDtypeStruct((s, wide), F32), jax.ShapeDtypeStruct((hq, BLOCK, 2 * BLOCK), F32),
                   jax.ShapeDtypeStruct((8, LANES), F32)],
        scratch_shapes=[pltpu.VMEM((BLOCK, wide), F32), pltpu.VMEM((2, BLOCK, wide), F32)],
        compiler_params=_params(1),
    )(qkn, qkn, qkn, proj, proj, bias, sink, do, o, lse)


def _rel_bucket_table():
    qi = np.arange(BLOCK)[:, None] + BLOCK
    kj = np.arange(2 * BLOCK)[None, :]
    n = np.maximum(qi - kj, 0)
    max_exact = REL_BUCKETS // 2
    nf = np.maximum(n, 1).astype(np.float32)
    large = max_exact + (np.log(nf / max_exact) / math.log(REL_MAX_DIST / max_exact)
                         * (REL_BUCKETS - max_exact)).astype(np.int32)
    large = np.minimum(large, REL_BUCKETS - 1)
    return np.where(n < max_exact, n, large).astype(np.int32)


def rel_bias_table(rel_bias, bucket):
    hq = rel_bias.shape[1]

    def body(rb_ref, bucket_ref, out_ref):
        tbl = bucket_ref[...]

        def per_head(h, carry):
            def per_bucket(b, acc):
                return jnp.where(tbl == b, rb_ref[b, h], acc)

            out_ref[h] = lax.fori_loop(0, REL_BUCKETS, per_bucket, jnp.zeros(tbl.shape, F32))
            return carry

        lax.fori_loop(0, hq, per_head, 0)

    return pl.pallas_call(
        body, name="rel_bias_table",
        in_specs=[pl.BlockSpec(memory_space=pltpu.SMEM), pl.BlockSpec(memory_space=pltpu.VMEM)],
        out_specs=pl.BlockSpec(memory_space=pltpu.VMEM),
        out_shape=jax.ShapeDtypeStruct((hq,) + tuple(bucket.shape), F32),
    )(rel_bias, bucket)


def rel_bias_grad(dbias, bucket):
    n_layers, hq = dbias.shape[:2]

    def body(db_ref, bucket_ref, out_ref):
        tbl = bucket_ref[...]

        def per_head(h, carry):
            dsum = db_ref[0, h]
            for a in range(1, n_layers):
                dsum = dsum + db_ref[a, h]

            def per_bucket(b, carry2):
                out_ref[b, h] = jnp.sum(jnp.where(tbl == b, dsum, 0.0))
                return carry2

            return lax.fori_loop(0, REL_BUCKETS, per_bucket, carry)

        lax.fori_loop(0, hq, per_head, 0)

    return pl.pallas_call(
        body, name="rel_bias_grad",
        in_specs=[pl.BlockSpec(memory_space=pltpu.VMEM), pl.BlockSpec(memory_space=pltpu.VMEM)],
        out_specs=pl.BlockSpec(memory_space=pltpu.SMEM),
        out_shape=jax.ShapeDtypeStruct((REL_BUCKETS, hq), F32),
    )(dbias, bucket)


def _split3(v):
    hi = v.astype(BF16)
    r1 = v - hi.astype(F32)
    mid = r1.astype(BF16)
    lo = (r1 - mid.astype(F32)).astype(BF16)
    return hi, mid, lo


def _tri_sum(tri, v):
    hi, mid, lo = _split3(v)
    return _nn(tri, hi) + _nn(tri, mid) + _nn(tri, lo)


def fox_gates(fl, b_f, name):
    s = fl.shape[0]
    t = _row_tile(s)

    def body(fl_ref, b_ref, f_ref, carry):
        @pl.when(pl.program_id(0) == 0)
        def _():
            carry[...] = jnp.zeros_like(carry)

        z = fl_ref[...] + b_ref[...]
        logf = jnp.minimum(z, 0.0) - jnp.log(1.0 + jnp.exp(-jnp.abs(z)))
        r = lax.broadcasted_iota(jnp.int32, (t, t), 0)
        cidx = lax.broadcasted_iota(jnp.int32, (t, t), 1)
        tri = jnp.where(cidx <= r, 1.0, 0.0).astype(BF16)
        f = _tri_sum(tri, logf) + carry[0:1, :]
        f_ref[...] = f
        carry[0:1, :] = f_ref[t - 1:t, :]

    blk = pl.BlockSpec((t, LANES), lambda i: (i, 0))
    return pl.pallas_call(
        body, name=name, grid=(s // t,),
        in_specs=[blk, pl.BlockSpec((1, LANES), lambda i: (0, 0))],
        out_specs=blk, out_shape=jax.ShapeDtypeStruct((s, LANES), F32),
        scratch_shapes=[pltpu.VMEM((8, LANES), F32)],
        compiler_params=_params(1),
    )(fl, b_f)


def fox_gates_bwd(fl, b_f, df_query, df_key, name):
    s = fl.shape[0]
    t = _row_tile(s)
    nb = s // t

    def body(fl_ref, b_ref, dfq_ref, dfk_ref, dfl_ref, db_ref, carry):
        @pl.when(pl.program_id(0) == 0)
        def _():
            carry[...] = jnp.zeros_like(carry)
            db_ref[...] = jnp.zeros_like(db_ref)

        dfv = dfq_ref[...] + dfk_ref[...]
        r = lax.broadcasted_iota(jnp.int32, (t, t), 0)
        cidx = lax.broadcasted_iota(jnp.int32, (t, t), 1)
        tri = jnp.where(cidx >= r, 1.0, 0.0).astype(BF16)
        dlog = _tri_sum(tri, dfv) + carry[0:1, :]
        carry[0:1, :] += jnp.sum(dfv, axis=0, keepdims=True)
        z = fl_ref[...] + b_ref[...]
        dz = dlog * (1.0 - _sigmoid(z))
        dfl_ref[...] = dz.astype(BF16)
        db_ref[0:1, :] += jnp.sum(dz, axis=0, keepdims=True)

    rev = pl.BlockSpec((t, LANES), lambda i: (nb - 1 - i, 0))
    return pl.pallas_call(
        body, name=name, grid=(nb,),
        in_specs=[rev, pl.BlockSpec((1, LANES), lambda i: (0, 0)), rev, rev],
        out_specs=[rev, pl.BlockSpec((8, LANES), lambda i: (0, 0))],
        out_shape=[jax.ShapeDtypeStruct((s, LANES), BF16), jax.ShapeDtypeStruct((8, LANES), F32)],
        scratch_shapes=[pltpu.VMEM((8, LANES), F32)],
        compiler_params=_params(1),
    )(fl, b_f, df_query, df_key)


def fox_fwd(qkn, proj, f_col, f_row, d, name):
    s = qkn.shape[0]
    t = _attn_tile(s)
    n_pairs = d // LANES
    nt = s // t

    def body(q_ref, k_ref, v_ref, fq_ref, fk_ref, o_ref, o32_ref, lse_ref, m_scr, l_scr, acc):
        i, j = pl.program_id(1), pl.program_id(2)

        @pl.when(j == 0)
        def _():
            m_scr[...] = jnp.full_like(m_scr, NEG)
            l_scr[...] = jnp.zeros_like(l_scr)
            acc[...] = jnp.zeros_like(acc)

        @pl.when(j <= i)
        def _():
            krow = lax.broadcasted_iota(jnp.int32, (t, t), 0)
            qcol = lax.broadcasted_iota(jnp.int32, (t, t), 1)
            visible = (krow <= qcol) | (j < i)
            for hh in range(2):
                hc = slice(hh * HEAD_DIM, (hh + 1) * HEAD_DIM)
                st = _nt(k_ref[:, hc], q_ref[:, hc]) + fq_ref[hh] - fk_ref[hh]
                st = jnp.where(visible, st, NEG)
                m_prev = m_scr[hh]
                m_new = jnp.maximum(m_prev, jnp.max(st, axis=0, keepdims=True))
                alpha = jnp.exp(m_prev - m_new)
                pt = jnp.exp(st - m_new)
                l_scr[hh] = alpha * l_scr[hh] + jnp.sum(pt, axis=0, keepdims=True)
                acc[hc, :] = alpha * acc[hc, :] + _tn(v_ref[:, hc], pt.astype(BF16))
                m_scr[hh] = m_new

        @pl.when(j == i)
        def _():
            l_full = jnp.concatenate([jnp.broadcast_to(l_scr[hh], (HEAD_DIM, t)) for hh in range(2)], axis=0)
            ov = (acc[...] / l_full).T
            o_ref[...] = ov.astype(BF16)
            o32_ref[...] = ov
            lse_ref[...] = m_scr[...] + jnp.log(l_scr[...])

    kv = lambda j, i: jnp.minimum(j, i)
    return pl.pallas_call(
        body, name=name, grid=(n_pairs, nt, nt),
        in_specs=[pl.BlockSpec((t, LANES), lambda p, i, j: (i, p)),
                  pl.BlockSpec((t, LANES), lambda p, i, j: (kv(j, i), n_pairs + p)),
                  pl.BlockSpec((t, LANES), lambda p, i, j: (kv(j, i), 2 * n_pairs + p)),
                  pl.BlockSpec((2, 1, t), lambda p, i, j: (p, 0, i)),
                  pl.BlockSpec((2, t, 1), lambda p, i, j: (p, kv(j, i), 0))],
        out_specs=[pl.BlockSpec((t, LANES), lambda p, i, j: (i, p)),
                   pl.BlockSpec((t, LANES), lambda p, i, j: (i, p)),
                   pl.BlockSpec((2, 1, t), lambda p, i, j: (p, 0, i))],
        out_shape=[jax.ShapeDtypeStruct((s, d), BF16), jax.ShapeDtypeStruct((s, d), F32),
                   jax.ShapeDtypeStruct((2 * n_pairs, 1, s), F32)],
        scratch_shapes=[pltpu.VMEM((2, 1, t), F32), pltpu.VMEM((2, 1, t), F32), pltpu.VMEM((2 * HEAD_DIM, t), F32)],
        compiler_params=_params(3),
    )(qkn, qkn, proj, f_row, f_col)


def fox_bwd(qkn, proj, f_col, f_row, lse_row, do, o, d, name):
    s = qkn.shape[0]
    t = _attn_tile(s)
    n_pairs = d // LANES
    nt = s // t

    def body(q_ref, k_ref, v_ref, fk_ref, fq_ref, lse_ref, do_ref, o_ref, out_ref, df_ref, dfq_ref,
             dq_acc, dkv_acc, df_acc, dfq_acc):
        j, i = pl.program_id(1), pl.program_id(2)

        @pl.when((j == 0) & (i == 0))
        def _():
            dq_acc[...] = jnp.zeros_like(dq_acc)
            dfq_acc[...] = jnp.zeros_like(dfq_acc)

        @pl.when(i == 0)
        def _():
            dkv_acc[...] = jnp.zeros_like(dkv_acc)
            df_acc[...] = jnp.zeros_like(df_acc)

        @pl.when(i >= j)
        def _():
            krow = lax.broadcasted_iota(jnp.int32, (t, t), 0)
            qcol = lax.broadcasted_iota(jnp.int32, (t, t), 1)
            visible = (krow <= qcol) | (i > j)
            ones = jnp.ones((8, HEAD_DIM), BF16)
            for hh in range(2):
                hc = slice(hh * HEAD_DIM, (hh + 1) * HEAD_DIM)
                q, k, v, dov = q_ref[:, hc], k_ref[:, hc], v_ref[:, hc], do_ref[:, hc]
                st = _nt(k, q) + fq_ref[hh] - fk_ref[hh]
                pt = jnp.exp(jnp.where(visible, st, NEG) - lse_ref[hh])
                hi, mid, lo = _split3(dov.astype(F32) * o_ref[:, hc])
                delta = jnp.max(_nt(ones, hi) + _nt(ones, mid) + _nt(ones, lo), axis=0, keepdims=True)
                dst = pt * (_nt(v, dov) - delta)
                dsb = dst.astype(BF16)
                dkv_acc[1, :, hc] += _nn(pt.astype(BF16), dov)
                dkv_acc[0, :, hc] += _nn(dsb, q)
                dq_acc[pl.ds(pl.multiple_of(i * t, t), t), hc] += _tn(dsb, k)
                df_acc[hh] -= jnp.sum(dst, axis=-1, keepdims=True)
                dfq_acc[i, hh] += jnp.sum(dst, axis=0, keepdims=True)

        @pl.when(i == nt - 1)
        def _():
            out_ref[0] = dq_acc[pl.ds(pl.multiple_of(j * t, t), t), :]
            out_ref[1] = dkv_acc[0]
            out_ref[2] = dkv_acc[1]
            df_ref[...] = df_acc[...]
            dfq_ref[...] = dfq_acc[j]

    qi = lambda j, i: jnp.maximum(i, j)
    return pl.pallas_call(
        body, name=name, grid=(n_pairs, nt, nt),
        in_specs=[pl.BlockSpec((t, LANES), lambda p, j, i: (qi(j, i), p)),
                  pl.BlockSpec((t, LANES), lambda p, j, i: (j, n_pairs + p)),
                  pl.BlockSpec((t, LANES), lambda p, j, i: (j, 2 * n_pairs + p)),
                  pl.BlockSpec((2, t, 1), lambda p, j, i: (p, j, 0)),
                  pl.BlockSpec((2, 1, t), lambda p, j, i: (p, 0, qi(j, i))),
                  pl.BlockSpec((2, 1, t), lambda p, j, i: (p, 0, qi(j, i))),
                  pl.BlockSpec((t, LANES), lambda p, j, i: (qi(j, i), p)),
                  pl.BlockSpec((t, LANES), lambda p, j, i: (qi(j, i), p))],
        out_specs=[pl.BlockSpec((3, t, LANES), lambda p, j, i: (0, j, p)),
                   pl.BlockSpec((2, t, 1), lambda p, j, i: (p, j, 0)),
                   pl.BlockSpec((2, 1, t), lambda p, j, i: (p, 0, j))],
        out_shape=[jax.ShapeDtypeStruct((3, s, d), F32), jax.ShapeDtypeStruct((2 * n_pairs, s, 1), F32),
                   jax.ShapeDtypeStruct((2 * n_pairs, 1, s), F32)],
        scratch_shapes=[pltpu.VMEM((s, LANES), F32), pltpu.VMEM((2, t, LANES), F32), pltpu.VMEM((2, t, 1), F32),
                        pltpu.VMEM((nt, 2, 1, t), F32)],
        compiler_params=_params(3),
    )(qkn, qkn, proj, f_col, f_row, lse_row, do, o)


def loss_head(y, target):
    s, d = y.shape
    tm = _row_tile(s)

    def body(y_ref, t_ref, dy_ref, loss_ref):
        @pl.when(pl.program_id(0) == 0)
        def _():
            loss_ref[...] = jnp.zeros_like(loss_ref)

        diff = y_ref[...] - t_ref[...]
        dy_ref[...] = diff * (1.0 / d)
        loss_ref[...] += 0.5 * jnp.sum(jnp.mean(diff * diff, axis=-1, keepdims=True), axis=0, keepdims=True)

    row = pl.BlockSpec((tm, d), lambda i: (i, 0))
    return pl.pallas_call(
        body, name="loss_head", grid=(s // tm,),
        in_specs=[row, row],
        out_specs=[row, pl.BlockSpec((8, LANES), lambda i: (0, 0))],
        out_shape=[jax.ShapeDtypeStruct((s, d), F32), jax.ShapeDtypeStruct((8, LANES), F32)],
        compiler_params=_params(1),
    )(y, target)


def ada_mod(c_all, w, b):
    n_layers, d, cols = w.shape

    def body(c_ref, w_ref, b_ref, o_ref):
        cv = c_ref[...]
        o_ref[...] = _nn(cv * _sigmoid(cv), w_ref[...]) + b_ref[...]

    return pl.pallas_call(
        body, name="ada_mod", grid=(n_layers,),
        in_specs=[pl.BlockSpec((N_DEV, d), lambda l: (0, 0)), pl.BlockSpec((None, d, cols), lambda l: (l, 0, 0)),
                  pl.BlockSpec((None, 1, cols), lambda l: (l, 0, 0))],
        out_specs=pl.BlockSpec((None, N_DEV, cols), lambda l: (l, 0, 0)),
        out_shape=jax.ShapeDtypeStruct((n_layers, N_DEV, cols), F32),
        compiler_params=_params(1),
    )(c_all, w, b)


def ada_grad(c_t, dmod):
    d = c_t.shape[0]
    n_layers, _, cols = dmod.shape
    tn = cols // 2

    def body(c_ref, dm_ref, o_ref):
        cv = c_ref[...]
        o_ref[...] = _nn(cv * _sigmoid(cv), dm_ref[...])

    return pl.pallas_call(
        body, name="ada_grad", grid=(n_layers, 2),
        in_specs=[pl.BlockSpec((d, N_DEV), lambda l, n: (0, 0)), pl.BlockSpec((None, N_DEV, tn), lambda l, n: (l, 0, n))],
        out_specs=pl.BlockSpec((None, d, tn), lambda l, n: (l, 0, n)),
        out_shape=jax.ShapeDtypeStruct((n_layers, d, cols), F32),
        compiler_params=_params(2),
    )(c_t, dmod)


def sum_devices(v):
    def body(v_ref, o_ref):
        acc = v_ref[0]
        for k in range(1, N_DEV):
            acc = acc + v_ref[k]
        o_ref[...] = acc

    return pl.pallas_call(body, name="sum_devices", out_shape=jax.ShapeDtypeStruct(v.shape[1:], F32))(v)


def sum_slots(r, own):
    _, rows, cols = r.shape
    tm = 256 if rows % 256 == 0 else rows

    def body(r_ref, own_ref, o_ref):
        o_ref[...] = ((own_ref[...].astype(F32) + r_ref[0].astype(F32)) + r_ref[1].astype(F32)) + r_ref[2].astype(F32)

    return pl.pallas_call(
        body, name="sum_slots", grid=(rows // tm,),
        in_specs=[pl.BlockSpec((N_CHIP - 1, tm, cols), lambda i: (0, i, 0)), pl.BlockSpec((tm, cols), lambda i: (i, 0))],
        out_specs=pl.BlockSpec((tm, cols), lambda i: (i, 0)),
        out_shape=jax.ShapeDtypeStruct((rows, cols), F32),
        compiler_params=_params(1),
    )(r, own)


def adamw(w, m, v, g, g2=None):
    rows, cols = w.shape
    tm = 256 if rows % 256 == 0 else rows
    two = g2 is not None
    c1 = 1.0 - ADAM_B1 ** ADAM_STEP
    c2 = 1.0 - ADAM_B2 ** ADAM_STEP

    def body(w_ref, m_ref, v_ref, g_ref, *rest):
        if two:
            g2_ref, go_ref, d_ref, mo_ref, vo_ref = rest
            gv = g_ref[...] + g2_ref[...]
        else:
            go_ref, d_ref, mo_ref, vo_ref = rest
            gv = g_ref[...]
        mn = ADAM_B1 * m_ref[...] + (1.0 - ADAM_B1) * gv
        vn = ADAM_B2 * v_ref[...] + (1.0 - ADAM_B2) * (gv * gv)
        go_ref[...] = gv
        mo_ref[...] = mn
        vo_ref[...] = vn
        d_ref[...] = -ADAM_LR * ((mn / c1) / (jnp.sqrt(vn / c2) + ADAM_EPS) + ADAM_WD * w_ref[...])

    blk = pl.BlockSpec((tm, cols), lambda i: (i, 0))
    ops = [w, m, v, g] + ([g2] if two else [])
    return pl.pallas_call(
        body, name="adamw", grid=(rows // tm,),
        in_specs=[blk] * len(ops), out_specs=[blk] * 4,
        out_shape=[jax.ShapeDtypeStruct((rows, cols), F32)] * 4,
        compiler_params=_params(1),
    )(*ops)


def _pad_rows(flat):
    n = flat.shape[0]
    rows = -(-n // LANES)
    return jnp.pad(flat, (0, rows * LANES - n)).reshape(rows, LANES)


def _pad_rows8(flat):
    rows = _pad_rows(flat)
    return jnp.pad(rows, ((0, -rows.shape[0] % 8), (0, 0)))


def _col_tiles(n):
    return next(k for k in range(1, n // LANES + 1) if n % (k * LANES) == 0 and n // k <= 1536)


def kernel(x, c, ada_w, ada_b, norm_g, ffn_w13, ffn_w2, rel_bias, swa_w_in, swa_w_out, swa_q_g, swa_k_g, swa_sink, fox_w_in, fox_w_out, fox_b_f, fox_q_g, fox_k_g, loss_target, m_ada_w, m_ada_b, m_norm_g, m_ffn_w13, m_ffn_w2, m_rel_bias, m_swa_w_in, m_swa_w_out, m_swa_q_g, m_swa_k_g, m_swa_sink, m_fox_w_in, m_fox_w_out, m_fox_b_f, m_fox_q_g, m_fox_k_g, v_ada_w, v_ada_b, v_norm_g, v_ffn_w13, v_ffn_w2, v_rel_bias, v_swa_w_in, v_swa_w_out, v_swa_q_g, v_swa_k_g, v_swa_sink, v_fox_w_in, v_fox_w_out, v_fox_b_f, v_fox_q_g, v_fox_k_g):
    ix, iy, ic = lax.axis_index("x"), lax.axis_index("y"), lax.axis_index("c")
    chip = 2 * ix + iy
    dev = 2 * chip + ic
    s, d = x.shape[1:]
    n_layers = ada_w.shape[0]
    n_a, n_b = swa_w_in.shape[0], fox_w_in.shape[0]
    hq = d // HEAD_DIM
    hkv = hq // GROUP
    kw = hkv * HEAD_DIM
    c13 = ffn_w13.shape[-1]
    f = 2 * c13
    r2 = ffn_w2.shape[2]
    cq = norm_g.shape[-1]
    a_in = d + 2 * kw
    fx = fox_w_in.shape[-1]
    b_in = N_CHIP * fx
    b_pad = 3 * d + LANES
    x0 = x[0]

    hello = _pad_rows8(jnp.concatenate([c.reshape(-1), norm_g.reshape(-1)]))
    hello_all = all_gather_rows(hello, "gather_c_norm").reshape(N_DEV, -1)
    c_all = hello_all[:, :d]
    ng = hello_all[::2, d:d + n_layers * 3 * cq].reshape(N_CHIP, n_layers, 3, cq)
    norm_full = jnp.moveaxis(ng, 0, 2).reshape(n_layers, 3, d)

    half_cols = ada_w.shape[-1] // 2
    w_half = lax.dynamic_slice_in_dim(ada_w, ic * half_cols, half_cols, axis=2)
    b_half = lax.dynamic_slice_in_dim(ada_b, dev * half_cols, half_cols, axis=1)[:, None, :]
    mod_part = ada_mod(c_all, w_half, b_half)
    mod_all = all_gather_rows(mod_part.reshape(n_layers * N_DEV, half_cols), "gather_mod")
    mod_all = mod_all.reshape(N_DEV, n_layers, N_DEV, half_cols)
    mod_mine = lax.dynamic_index_in_dim(mod_all, dev, axis=2, keepdims=False)
    mod_mine = jnp.moveaxis(mod_mine, 0, 1).reshape(n_layers, 3, 3, d)
    mv = jnp.concatenate([norm_full[:, :, None, :], mod_mine, jnp.zeros((n_layers, 3, 4, d), F32)], axis=2)

    n_groups = n_layers // 2
    kinds = [("col", c13), ("row", r2), ("col", swa_w_in.shape[-1]), ("slot",), ("slot",), ("slot",)]

    def group_buffers(g, after):
        raw = [ffn_w13[2 * g:2 * g + 2], ffn_w2[2 * g:2 * g + 2], swa_w_in[g], swa_w_out[g], fox_w_in[g], fox_w_out[g]]
        return [cast_and_place(r, k, chip, after, f"place_weights_{g}_{t}") for t, (r, k) in enumerate(zip(raw, kinds))]

    def group_weights(fulls):
        w13_g, w2_g, wa_in_g, wa_out_g, wb_slots, wb_out_g = fulls
        wb_in_g = jnp.pad(jnp.concatenate([wb_slots[b] for b in range(N_CHIP)], axis=-1), ((0, 0), (0, b_pad - b_in)))
        return dict(w13=w13_g, w2=w2_g, wa_in=wa_in_g, wa_out=wa_out_g.reshape(d, d), wb_in=wb_in_g,
                    wb_out=wb_out_g.reshape(d, d))

    weights = [group_weights(gather_weights(group_buffers(0, mv), kinds))]
    in_flight = []
    for g in range(1, n_groups):
        in_flight.append(exchange_start([], group_buffers(g, weights[0]["w13"]), kinds, True, f"gather_start_{g}"))
    for *_, token in in_flight:
        mv = mv + token[0, 0]

    bucket = jnp.asarray(_rel_bucket_table())
    bias = rel_bias_table(rel_bias, bucket)
    tm, tw = _row_tile(s), _wide_tile(s)
    n13 = 2 * f // c13
    na_t, nb_t = _col_tiles(a_in), _col_tiles(3 * d)
    wa_t, wb_t = a_in // na_t, 3 * d // nb_t
    gate_blk = 3 * d // LANES

    def ffn_forward(xv, l, half, sub):
        wg, li = weights[l // 2], l % 2
        a, h = modmm(xv, mv, (l, sub), wg["w13"], pl.BlockSpec((None, None, d, c13), lambda i, n: (li, half, 0, n)), n13,
                     jax.ShapeDtypeStruct((2, s, f), BF16),
                     pl.BlockSpec((None, tw, c13), lambda i, n: (n // 2, i, n % 2)), f"ffn_up_{l}_{half}", True)
        xo, y, u = resmm(xv, mv, (l, sub), 0.5, a, pl.BlockSpec((2, tm, c13), lambda i, k: (0, i, k)),
                         wg["w2"], pl.BlockSpec((None, None, c13, d), lambda i, k: (li, half, k, 0)), f // c13, c13,
                         f"ffn_down_{l}_{half}", True)
        return xo, dict(x=xv, h=h, a=a, u=u, y=y)

    saved = []
    xv = x0
    for l in range(n_layers):
        j = l // 2
        if l % 2 == 0 and j >= 1:
            send_sems, recv_sems, shards, fulls, _ = in_flight[j - 1]
            weights.append(group_weights(exchange_wait(send_sems, recv_sems, shards, fulls, xv, kinds, True,
                                                       f"gather_wait_{j}")[1]))
        wg = weights[j]
        xv, s0 = ffn_forward(xv, l, 0, 0)
        if l % 2 == 0:
            proj, h = modmm(xv, mv, (l, 1), wg["wa_in"], pl.BlockSpec((d, wa_t), lambda i, n: (0, n)), na_t,
                            jax.ShapeDtypeStruct((s, a_in), BF16), pl.BlockSpec((tw, wa_t), lambda i, n: (i, n)),
                            f"swa_in_{j}", True)
            gains = jnp.concatenate([jnp.tile(swa_q_g[j] * HEAD_DIM ** -0.5, hq), jnp.tile(swa_k_g[j], hkv)])[None, :]
            qkn = qknorm_fwd(proj, gains, kw, f"swa_qknorm_{j}")
            sink = swa_sink[j][None, :]
            o, lse = swa_fwd(qkn, proj, bias, sink, d, f"swa_attn_{j}")
            s1 = dict(x=xv, h=h, proj=proj, gains=gains, qkn=qkn, sink=sink, o=o, lse=lse)
            w_out = wg["wa_out"]
        else:
            proj, h = modmm(xv, mv, (l, 1), wg["wb_in"], pl.BlockSpec((d, wb_t), lambda i, n: (0, n)), nb_t,
                            jax.ShapeDtypeStruct((s, 3 * d), BF16), pl.BlockSpec((tw, wb_t), lambda i, n: (i, n)),
                            f"fox_in_{j}", True)
            fl, _ = modmm(xv, mv, (l, 1), wg["wb_in"], pl.BlockSpec((d, LANES), lambda i, n: (0, gate_blk)), 1,
                          jax.ShapeDtypeStruct((s, LANES), F32), pl.BlockSpec((tw, LANES), lambda i, n: (i, 0)),
                          f"fox_gate_in_{j}", False)
            b_f = jnp.pad(fox_b_f[j], (0, LANES - hq))[None, :]
            fcum = fox_gates(fl, b_f, f"fox_gates_{j}")
            f_t = fcum[:, :hq].T
            f_col, f_row = f_t[:, :, None], f_t[:, None, :]
            gains = jnp.concatenate([jnp.tile(fox_q_g[j] * HEAD_DIM ** -0.5, hq), jnp.tile(fox_k_g[j], hq)])[None, :]
            qkn = qknorm_fwd(proj, gains, d, f"fox_qknorm_{j}")
            o, o32, lse = fox_fwd(qkn, proj, f_col, f_row, d, f"fox_attn_{j}")
            s1 = dict(x=xv, h=h, proj=proj, gains=gains, qkn=qkn, fl=fl, b_f=b_f, f_col=f_col, f_row=f_row, o=o, o32=o32,
                      lse=lse)
            w_out = wg["wb_out"]
        xv, y = resmm(xv, mv, (l, 1), 1.0, o, pl.BlockSpec((tm, d), lambda i, k: (i, 0)), w_out,
                      pl.BlockSpec((d, d), lambda i, k: (0, 0)), 1, d, f"mixer_out_{l}", False)
        s1["y"] = y
        xv, s2 = ffn_forward(xv, l, 1, 2)
        saved.append((s0, s1, s2))

    dxv, loss_part = loss_head(xv, loss_target[0])
    loss = lax.psum(loss_part[0, 0], ("x", "y", "c"))

    grads = [dict(w13=lax.empty((2, 2, d, 2 * f), BF16), w2=lax.empty((2, 2, f, d), BF16),
                  wa_in=lax.empty((d, a_in), BF16), wa_out=lax.empty((d, d), BF16),
                  wb_in=lax.empty((d, b_pad), BF16), wb_out=lax.empty((d, d), BF16)) for _ in range(n_groups)]
    dmod = [[None] * 3 for _ in range(n_layers)]
    dnorm = [[None] * 3 for _ in range(n_layers)]
    dqk_gain = {}
    dsink, db_f, dbias_tabs = {}, {}, []

    def ffn_backward(dxo, sv, l, half, sub):
        wg, gg, li = weights[l // 2], grads[l // 2], l % 2
        dy, da, dgate = resmm_bwd(dxo, sv["y"], mv, (l, sub), 0.5, wg["w2"],
                                  pl.BlockSpec((None, None, c13, d), lambda i, k: (li, half, k, 0)), f // c13, c13,
                                  f"ffn_down_bwd_{l}_{half}", a=sv["a"])
        gg["w2"] = weight_grad(sv["u"], pl.BlockSpec((tw, c13), lambda m, n, k: (k, m)), dy,
                               pl.BlockSpec((tw, d), lambda m, n, k: (k, 0)), (f // c13, 1), s, c13, d, gg["w2"],
                               pl.BlockSpec((None, None, c13, d), lambda m, n, k: (li, half, m, 0)),
                               f"ffn_w2_grad_{l}_{half}")
        gg["w13"] = weight_grad(sv["h"], pl.BlockSpec((tw, d), lambda m, n, k: (k, 0)), da,
                                pl.BlockSpec((None, tw, c13), lambda m, n, k: (n // 2, k, n % 2)), (1, n13), s, d, c13,
                                gg["w13"], pl.BlockSpec((None, None, d, c13), lambda m, n, k: (li, half, 0, n)),
                                f"ffn_w13_grad_{l}_{half}")
        dx, red = modmm_bwd(da, pl.BlockSpec((None, tm, c13), lambda i, n: (n // 2, i, n % 2)), wg["w13"],
                            pl.BlockSpec((None, None, d, c13), lambda i, n: (li, half, 0, n)), n13, sv["x"], dxo, mv,
                            (l, sub), f"ffn_up_bwd_{l}_{half}")
        dmod[l][sub] = (red[1], red[2], dgate[0])
        dnorm[l][sub] = red[0]
        return dx

    def group_slabs(gg):
        gb_slots = jnp.stack([gg["wb_in"][:, b * fx:(b + 1) * fx] for b in range(N_CHIP)])
        return [gg["w13"], gg["w2"], gg["wa_in"], gg["wa_out"].reshape(N_CHIP, d // N_CHIP, d), gb_slots,
                gg["wb_out"].reshape(N_CHIP, d // N_CHIP, d)]

    shard_shapes = [(2,) + ffn_w13.shape[1:], (2,) + ffn_w2.shape[1:], swa_w_in.shape[1:], swa_w_out.shape[1:],
                    fox_w_in.shape[1:], fox_w_out.shape[1:]]
    scatter_in_flight = {}

    for l in reversed(range(n_layers)):
        j = l // 2
        wg, gg = weights[j], grads[j]
        s0, s1, s2 = saved[l]
        dxv = ffn_backward(dxv, s2, l, 1, 2)
        is_a = l % 2 == 0
        w_out = wg["wa_out"] if is_a else wg["wb_out"]
        dy, do, dgate = resmm_bwd(dxv, s1["y"], mv, (l, 1), 1.0, w_out, pl.BlockSpec((d, d), lambda i, k: (0, 0)),
                                  1, d, f"mixer_out_bwd_{l}")
        out_key = "wa_out" if is_a else "wb_out"
        gg[out_key] = weight_grad(s1["o"], pl.BlockSpec((tw, d), lambda m, n, k: (k, 0)), dy,
                                  pl.BlockSpec((tw, d), lambda m, n, k: (k, 0)), (1, 1), s, d, d, gg[out_key],
                                  pl.BlockSpec((d, d), lambda m, n, k: (0, 0)), f"mixer_out_grad_{l}")
        if is_a:
            d_qkv, dbias_tab, dsk = swa_bwd(s1["qkn"], s1["proj"], bias, s1["sink"], do, s1["o"], s1["lse"], d,
                                            f"swa_attn_bwd_{j}")
            dbias_tabs.append(dbias_tab)
            dsink[j] = dsk[0, :hq]
            dproj, dgain = qknorm_bwd(s1["proj"], s1["gains"], d_qkv, pl.BlockSpec((tm, kw), lambda c_, i: (i, c_)),
                                      a_in, kw, f"swa_qknorm_bwd_{j}")
            gg["wa_in"] = weight_grad(s1["h"], pl.BlockSpec((tw, d), lambda m, n, k: (k, 0)), dproj,
                                      pl.BlockSpec((tw, wa_t), lambda m, n, k: (k, n)), (1, na_t), s, d, wa_t, gg["wa_in"],
                                      pl.BlockSpec((d, wa_t), lambda m, n, k: (0, n)), f"swa_in_grad_{j}")
            dxv, red = modmm_bwd(dproj, pl.BlockSpec((tm, wa_t), lambda i, n: (i, n)), wg["wa_in"],
                                 pl.BlockSpec((d, wa_t), lambda i, n: (0, n)), na_t, s1["x"], dxv, mv, (l, 1),
                                 f"swa_in_bwd_{j}")
            dqk_gain[("a", j)] = (dgain[0, :d].reshape(hq, HEAD_DIM).sum(0) * HEAD_DIM ** -0.5,
                                  dgain[0, d:d + kw].reshape(hkv, HEAD_DIM).sum(0))
        else:
            lse_row = s1["lse"].reshape(hq, 1, s)
            d_qkv, df_col, dfq_row = fox_bwd(s1["qkn"], s1["proj"], s1["f_col"], s1["f_row"], lse_row, do, s1["o32"], d,
                                             f"fox_attn_bwd_{j}")
            lanes_of_heads = lambda a: jnp.pad(a.T, ((0, 0), (0, LANES - hq)))
            dfl, dbf = fox_gates_bwd(s1["fl"], s1["b_f"], lanes_of_heads(dfq_row[:, 0, :]), lanes_of_heads(df_col[:, :, 0]),
                                     f"fox_gates_bwd_{j}")
            db_f[j] = dbf[0, :hq]
            dproj, dgain = qknorm_bwd(s1["proj"], s1["gains"], d_qkv, pl.BlockSpec((None, tm, d), lambda c_, i: (c_, i, 0)),
                                      3 * d, d, f"fox_qknorm_bwd_{j}")
            gg["wb_in"] = weight_grad(s1["h"], pl.BlockSpec((tw, d), lambda m, n, k: (k, 0)), dproj,
                                      pl.BlockSpec((tw, wb_t), lambda m, n, k: (k, n)), (1, nb_t), s, d, wb_t, gg["wb_in"],
                                      pl.BlockSpec((d, wb_t), lambda m, n, k: (0, n)), f"fox_in_grad_{j}")
            gg["wb_in"] = weight_grad(s1["h"], pl.BlockSpec((tw, d), lambda m, n, k: (k, 0)), dfl,
                                      pl.BlockSpec((tw, LANES), lambda m, n, k: (k, 0)), (1, 1), s, d, LANES, gg["wb_in"],
                                      pl.BlockSpec((d, LANES), lambda m, n, k: (0, gate_blk)), f"fox_gate_in_grad_{j}")
            dxv, red = modmm_bwd(dproj, pl.BlockSpec((tm, wb_t), lambda i, n: (i, n)), wg["wb_in"],
                                 pl.BlockSpec((d, wb_t), lambda i, n: (0, n)), nb_t, s1["x"], dxv, mv, (l, 1),
                                 f"fox_in_bwd_{j}",
                                 more=(dfl, pl.BlockSpec((tm, LANES), lambda i, n: (i, 0)),
                                       pl.BlockSpec((d, LANES), lambda i, n: (0, gate_blk))))
            dqk_gain[("b", j)] = (dgain[0, :d].reshape(hq, HEAD_DIM).sum(0) * HEAD_DIM ** -0.5,
                                  dgain[0, d:2 * d].reshape(hq, HEAD_DIM).sum(0))
        dmod[l][1] = (red[1], red[2], dgate[0])
        dnorm[l][1] = red[0]
        dxv = ffn_backward(dxv, s0, l, 0, 0)
        if l % 2 == 0 and j >= 1:
            slabs = group_slabs(gg)
            lands = [lax.empty((N_CHIP - 1,) + tuple(shp), BF16) for shp in shard_shapes]
            scatter_in_flight[j] = exchange_start(slabs, lands, kinds, False, f"scatter_start_{j}")
            mv = mv + scatter_in_flight[j][-1][0, 0]
    grad_x = dxv[None]

    drel = rel_bias_grad(jnp.stack(dbias_tabs), bucket)
    dmod_flat = jnp.stack([jnp.stack([jnp.stack(dmod[l][sub]) for sub in range(3)]) for l in range(n_layers)]).reshape(-1)
    dnorm_flat = jnp.stack([jnp.stack(dnorm[l]) for l in range(n_layers)]).reshape(-1)
    pieces = [dmod_flat, dnorm_flat,
              jnp.stack([dqk_gain[("a", j)][0] for j in range(n_a)]).reshape(-1),
              jnp.stack([dqk_gain[("a", j)][1] for j in range(n_a)]).reshape(-1),
              jnp.stack([dqk_gain[("b", j)][0] for j in range(n_b)]).reshape(-1),
              jnp.stack([dqk_gain[("b", j)][1] for j in range(n_b)]).reshape(-1),
              jnp.stack([dsink[j] for j in range(n_a)]).reshape(-1),
              jnp.stack([db_f[j] for j in range(n_b)]).reshape(-1),
              drel.reshape(-1)]
    rows = [_pad_rows(p) for p in pieces]
    starts = np.cumsum([0] + [r.shape[0] for r in rows])
    total = -(-int(starts[-1]) // 8) * 8
    small = jnp.pad(jnp.concatenate(rows), ((0, total - int(starts[-1])), (0, 0)))
    small_all = all_gather_rows(small, "gather_small_grads").reshape(N_DEV, total, LANES)
    small_sum = sum_devices(small_all)

    def piece(k, shape):
        n = int(np.prod(shape))
        return small_sum[int(starts[k]):int(starts[k + 1])].reshape(-1)[:n].reshape(shape)

    g_ada_b = piece(0, (n_layers, 9 * d))
    g_norm = lax.dynamic_slice_in_dim(piece(1, (n_layers, 3, d)), chip * cq, cq, axis=2)
    g_swa_q, g_swa_k = piece(2, (n_a, HEAD_DIM)), piece(3, (n_a, HEAD_DIM))
    g_fox_q, g_fox_k = piece(4, (n_b, HEAD_DIM)), piece(5, (n_b, HEAD_DIM))
    g_sink, g_bf, g_rel = piece(6, (n_a, hq)), piece(7, (n_b, hq)), piece(8, (REL_BUCKETS, hq))

    dmod_all = small_all[:, :int(starts[1])].reshape(N_DEV, -1)[:, :n_layers * 9 * d].reshape(N_DEV, n_layers, 9 * d)
    ada_cols = ada_w.shape[-1]
    dmod_mine = lax.dynamic_slice_in_dim(jnp.moveaxis(dmod_all, 0, 1), chip * ada_cols, ada_cols, axis=2)
    g_ada_w = ada_grad(c_all.T, dmod_mine)

    sources = {0: group_slabs(grads[0])}
    landed = {0: scatter_grads(sources[0], kinds, shard_shapes)}
    for g, (send_sems, recv_sems, slabs, lands, _) in scatter_in_flight.items():
        sources[g], landed[g] = exchange_wait(send_sems, recv_sems, slabs, lands, landed[0][0], kinds, False,
                                              f"scatter_wait_{g}")

    def group_sum(g, t):
        cols = shard_shapes[t][-1]
        own = own_slab(sources[g][t], kinds[t], chip, shard_shapes[t])
        return sum_slots(landed[g][t].reshape(N_CHIP - 1, -1, cols), own.reshape(-1, cols))

    parts = [jnp.concatenate([group_sum(g, t) for g in range(n_groups)]) for t in range(len(kinds))]
    others = swap_with_sibling(parts)

    def update(w, m, v, g, g2=None):
        w2d = w.reshape(-1, w.shape[-1])
        outs = adamw(w2d, m.reshape(w2d.shape), v.reshape(w2d.shape), g.reshape(w2d.shape) if g2 is None else g, g2)
        return [t.reshape(w.shape) for t in outs]

    big = [(ffn_w13, m_ffn_w13, v_ffn_w13), (ffn_w2, m_ffn_w2, v_ffn_w2), (swa_w_in, m_swa_w_in, v_swa_w_in),
           (swa_w_out, m_swa_w_out, v_swa_w_out), (fox_w_in, m_fox_w_in, v_fox_w_in), (fox_w_out, m_fox_w_out, v_fox_w_out)]
    big_out = [update(w, m, v, p, q) for (w, m, v), p, q in zip(big, parts, others)]
    r_ada_w = update(ada_w, m_ada_w, v_ada_w, g_ada_w)
    r_ada_b = update(ada_b, m_ada_b, v_ada_b, g_ada_b)
    r_norm = update(norm_g, m_norm_g, v_norm_g, g_norm)
    r_rel = update(rel_bias, m_rel_bias, v_rel_bias, g_rel)
    r_swa_q = update(swa_q_g, m_swa_q_g, v_swa_q_g, g_swa_q)
    r_swa_k = update(swa_k_g, m_swa_k_g, v_swa_k_g, g_swa_k)
    r_sink = update(swa_sink, m_swa_sink, v_swa_sink, g_sink)
    r_bf = update(fox_b_f, m_fox_b_f, v_fox_b_f, g_bf)
    r_fox_q = update(fox_q_g, m_fox_q_g, v_fox_q_g, g_fox_q)
    r_fox_k = update(fox_k_g, m_fox_k_g, v_fox_k_g, g_fox_k)
    per_weight = [r_ada_w, r_ada_b, r_norm, big_out[0], big_out[1], r_rel, big_out[2], big_out[3], r_swa_q, r_swa_k,
                  r_sink, big_out[4], big_out[5], r_bf, r_fox_q, r_fox_k]
    return (loss, grad_x, *[r[0] for r in per_weight], *[r[1] for r in per_weight],
            *[r[2] for r in per_weight], *[r[3] for r in per_weight])
```

```python
import math

import numpy as np
import jax
import jax.numpy as jnp
from jax import lax
from jax.experimental import pallas as pl
from jax.experimental.pallas import tpu as pltpu

F32 = jnp.float32
BF16 = jnp.bfloat16
HEAD_DIM = 64
GROUP = 4
FOX_HEADS = 4
BLOCK = 128
REL_BUCKETS = 32
REL_MAX_DIST = 128
EPS = 1e-6
NEG = -1e30
N_CHIP = 4
N_DEV = 8
LANES = 128
VMEM_LIMIT = 52 * 1024 * 1024
ADAM_LR, ADAM_B1, ADAM_B2, ADAM_EPS, ADAM_WD, ADAM_STEP = 0.001, 0.9, 0.999, 1e-08, 0.01, 10
MESH = pl.DeviceIdType.MESH
ANY = pl.BlockSpec(memory_space=pl.ANY)


def _params(n_axes):
    return pltpu.CompilerParams(dimension_semantics=("arbitrary",) * n_axes, vmem_limit_bytes=VMEM_LIMIT)


def _nn(a, b):
    return jnp.dot(a, b, preferred_element_type=F32)


def _nt(a, b):
    return lax.dot_general(a, b, (((1,), (1,)), ((), ())), preferred_element_type=F32)


def _tn(a, b):
    return lax.dot_general(a, b, (((0,), (0,)), ((), ())), preferred_element_type=F32)


def _sigmoid(z):
    return 1.0 / (1.0 + jnp.exp(-z))


def _sigmoid_fast(z):
    return pl.reciprocal(1.0 + jnp.exp(-z), approx=True)


def _row_tile(s):
    return 512 if s >= 2048 else s // 2


def _wide_tile(s):
    return 1024 if s >= 2048 else s // 2


def _attn_tile(s):
    return 512 if s >= 2048 else s // 4


def _position():
    x, y, c = lax.axis_index("x"), lax.axis_index("y"), lax.axis_index("c")
    chips = [(1 - x, y), (x, 1 - y), (1 - x, 1 - y)]
    return x, y, c, chips


def all_gather_rows(v, name):
    m_per, n = v.shape

    def body(x_ref, out_ref, send_sems, recv_sems, local_sem):
        x, y, c, chips = _position()
        me, sibling = (x, y, c), (x, y, 1 - c)

        def rows(px, py, pc):
            return out_ref.at[pl.ds((4 * px + 2 * py + pc) * m_per, m_per), :]

        def copy(k, block, to, src=None):
            return pltpu.make_async_remote_copy(
                src_ref=rows(*block) if src is None else src, dst_ref=rows(*block),
                send_sem=send_sems.at[k], recv_sem=recv_sems.at[k], device_id=to, device_id_type=MESH)

        mine = pltpu.make_async_copy(x_ref, rows(*me), local_sem)
        mine.start()
        first = [copy(0, me, sibling, src=x_ref)]
        first += [copy(1 + j, me, (*chip, c), src=x_ref) for j, chip in enumerate(chips)]
        for cp in first:
            cp.start()
        passed = [copy(4 + j, (*chip, c), sibling) for j, chip in enumerate(chips)]
        for j, chip in enumerate(chips):
            copy(1 + j, (*chip, c), me).wait_recv()
            passed[j].start()
        copy(0, sibling, me).wait_recv()
        for j, chip in enumerate(chips):
            copy(4 + j, (*chip, 1 - c), me).wait_recv()
        for cp in first + passed:
            cp.wait_send()
        mine.wait()

    return pl.pallas_call(
        body, name=name,
        out_shape=jax.ShapeDtypeStruct((N_DEV * m_per, n), v.dtype),
        in_specs=[pl.BlockSpec(memory_space=pltpu.VMEM)],
        out_specs=pl.BlockSpec(memory_space=pltpu.VMEM),
        scratch_shapes=[pltpu.SemaphoreType.DMA((7,)), pltpu.SemaphoreType.DMA((7,)), pltpu.SemaphoreType.DMA],
    )(v)


def _slab(full_ref, kind, b, lead):
    how = kind[0]
    if how == "slot":
        return full_ref.at[b, lead]
    if how == "col":
        w = kind[1]
        idx = (lead,) + (slice(None),) * (len(full_ref.shape) - 2) + (pl.ds(pl.multiple_of(b * w, LANES), w),)
        return full_ref.at[idx]
    h = kind[1]
    idx = (lead,) + (slice(None),) * (len(full_ref.shape) - 3) + (pl.ds(pl.multiple_of(b * h, 8), h), slice(None))
    return full_ref.at[idx]


def _full_shape(shard_shape, kind):
    if kind[0] == "slot":
        return (N_CHIP,) + tuple(shard_shape)
    if kind[0] == "col":
        return tuple(shard_shape[:-1]) + (N_CHIP * shard_shape[-1],)
    return tuple(shard_shape[:-2]) + (N_CHIP * shard_shape[-2], shard_shape[-1])


def _slab_start(shape, kind, b):
    zeros = [0] * len(shape)
    if kind[0] == "slot":
        return [b] + zeros[1:]
    if kind[0] == "col":
        return zeros[:-1] + [b * kind[1]]
    return zeros[:-2] + [b * kind[1], 0]


def cast_and_place(shard, kind, chip, after, name):
    rows, cols = shard.shape[-2:]
    lead = int(np.prod(shard.shape[:-2]))
    if kind[0] == "col":
        full3, where = (lead, rows, N_CHIP * cols), lambda p, b: (p, 0, b[0])
    elif kind[0] == "row":
        full3, where = (lead, N_CHIP * rows, cols), lambda p, b: (p, b[0], 0)
    else:
        full3, where = (N_CHIP * lead, rows, cols), lambda p, b: (b[0] * lead + p, 0, 0)

    def body(b_ref, s_ref, after_ref, o_ref):
        o_ref[...] = s_ref[...].astype(BF16)

    full = pl.pallas_call(
        body, name=name,
        grid_spec=pltpu.PrefetchScalarGridSpec(
            num_scalar_prefetch=1, grid=(lead,),
            in_specs=[pl.BlockSpec((1, rows, cols), lambda p, b: (p, 0, 0)), ANY],
            out_specs=pl.BlockSpec((1, rows, cols), where)),
        out_shape=jax.ShapeDtypeStruct(full3, BF16),
        compiler_params=_params(1),
    )(jnp.reshape(chip, (1,)).astype(jnp.int32), shard.reshape(lead, rows, cols), after)
    return full.reshape(_full_shape(shard.shape, kind))


def own_slab(full, kind, b, shard_shape):
    sizes = (1,) + tuple(shard_shape) if kind[0] == "slot" else tuple(shard_shape)
    return lax.dynamic_slice(full, _slab_start(full.shape, kind, b), sizes).reshape(shard_shape)


def gather_weights(fulls, kinds):
    n = len(fulls)

    def body(*refs):
        outs = refs[n:2 * n]
        send_sems, recv_sems = refs[2 * n:]
        x, y, c, chips = _position()
        b_me = 2 * x + y
        sibling = (x, y, 1 - c)
        sends = []

        def halves(t):
            lead = outs[t].shape[1] if kinds[t][0] == "slot" else outs[t].shape[0]
            return pl.ds(c * (lead // 2), lead // 2), pl.ds((1 - c) * (lead // 2), lead // 2)

        for t in range(n):
            mine, _ = halves(t)
            own = _slab(outs[t], kinds[t], b_me, mine)
            for j, chip in enumerate(chips):
                cp = pltpu.make_async_remote_copy(
                    src_ref=own, dst_ref=own,
                    send_sem=send_sems.at[6 * t + j], recv_sem=recv_sems.at[6 * t + j],
                    device_id=(*chip, c), device_id_type=MESH)
                cp.start()
                sends.append(cp)
        for t in range(n):
            mine, _ = halves(t)
            for j, chip in enumerate(chips):
                landed = _slab(outs[t], kinds[t], 2 * chip[0] + chip[1], mine)
                pltpu.make_async_remote_copy(
                    src_ref=landed, dst_ref=landed, send_sem=send_sems.at[6 * t + j], recv_sem=recv_sems.at[6 * t + j],
                    device_id=(*chip, c), device_id_type=MESH).wait_recv()
                cp = pltpu.make_async_remote_copy(
                    src_ref=landed, dst_ref=landed, send_sem=send_sems.at[6 * t + 3 + j],
                    recv_sem=recv_sems.at[6 * t + 3 + j], device_id=sibling, device_id_type=MESH)
                cp.start()
                sends.append(cp)
        for t in range(n):
            _, theirs = halves(t)
            for j, chip in enumerate(chips):
                landed = _slab(outs[t], kinds[t], 2 * chip[0] + chip[1], theirs)
                pltpu.make_async_remote_copy(
                    src_ref=landed, dst_ref=landed, send_sem=send_sems.at[6 * t + 3 + j],
                    recv_sem=recv_sems.at[6 * t + 3 + j], device_id=sibling, device_id_type=MESH).wait_recv()
        for cp in sends:
            cp.wait_send()

    return pl.pallas_call(
        body, name="gather_weights",
        out_shape=[jax.ShapeDtypeStruct(v.shape, v.dtype) for v in fulls],
        in_specs=[ANY] * n, out_specs=[ANY] * n,
        input_output_aliases={t: t for t in range(n)},
        scratch_shapes=[pltpu.SemaphoreType.DMA((6 * n,)), pltpu.SemaphoreType.DMA((6 * n,))],
    )(*fulls)


def scatter_grads(grads, kinds, shard_shapes):
    n = len(grads)

    def body(*refs):
        ins, outs = refs[:n], refs[n:2 * n]
        send_sems, recv_sems = refs[2 * n:]
        x, y, c, chips = _position()
        sends = []
        for t in range(n):
            whole = _whole(ins[t], kinds[t])
            for j, chip in enumerate(chips):
                cp = pltpu.make_async_remote_copy(
                    src_ref=_slab(ins[t], kinds[t], 2 * chip[0] + chip[1], whole), dst_ref=outs[t].at[j],
                    send_sem=send_sems.at[3 * t + j], recv_sem=recv_sems.at[3 * t + j],
                    device_id=(*chip, c), device_id_type=MESH)
                cp.start()
                sends.append(cp)
        for t in range(n):
            for j, chip in enumerate(chips):
                pltpu.make_async_remote_copy(
                    src_ref=outs[t].at[j], dst_ref=outs[t].at[j], send_sem=send_sems.at[3 * t + j],
                    recv_sem=recv_sems.at[3 * t + j], device_id=(*chip, c), device_id_type=MESH).wait_recv()
        for cp in sends:
            cp.wait_send()

    return pl.pallas_call(
        body, name="scatter_grads",
        out_shape=[jax.ShapeDtypeStruct((N_CHIP - 1,) + tuple(s), g.dtype) for g, s in zip(grads, shard_shapes)],
        in_specs=[ANY] * n, out_specs=[ANY] * n,
        scratch_shapes=[pltpu.SemaphoreType.DMA((3 * n,)), pltpu.SemaphoreType.DMA((3 * n,))],
    )(*grads)


def _whole(ref, kind):
    return pl.ds(0, ref.shape[1] if kind[0] == "slot" else ref.shape[0])


def _exchange_copies(srcs, lands, kinds, gather, send_sems, recv_sems):
    x, y, c, chips = _position()
    b_me = 2 * x + y
    out = []
    for t in range(len(lands)):
        for j, chip in enumerate(chips):
            b_j = 2 * chip[0] + chip[1]
            if gather:
                src = sent_to = _slab(lands[t], kinds[t], b_me, _whole(lands[t], kinds[t]))
                arrives = _slab(lands[t], kinds[t], b_j, _whole(lands[t], kinds[t]))
            else:
                src = _slab(srcs[t], kinds[t], b_j, _whole(srcs[t], kinds[t]))
                sent_to = arrives = lands[t].at[j]
            k = 3 * t + j
            send = pltpu.make_async_remote_copy(src_ref=src, dst_ref=sent_to, send_sem=send_sems.at[k],
                                                recv_sem=recv_sems.at[k], device_id=(*chip, c), device_id_type=MESH)
            recv = pltpu.make_async_remote_copy(src_ref=src, dst_ref=arrives, send_sem=send_sems.at[k],
                                                recv_sem=recv_sems.at[k], device_id=(*chip, c), device_id_type=MESH)
            out.append((send, recv))
    return out


def exchange_start(srcs, lands, kinds, gather, name):
    ns, nl = len(srcs), len(lands)
    hbm = pl.BlockSpec(memory_space=pltpu.HBM)

    def body(*refs):
        ins, lnd = refs[:ns], refs[ns:ns + nl]
        send_sems, recv_sems = refs[ns + nl], refs[ns + nl + 1]
        token = refs[-1]
        for send, _ in _exchange_copies(ins, lnd, kinds, gather, send_sems, recv_sems):
            send.start()
        token[...] = jnp.zeros_like(token)

    ops = [pltpu.with_memory_space_constraint(v, pltpu.HBM) for v in (*srcs, *lands)]
    res = pl.pallas_call(
        body, name=name,
        out_shape=(pltpu.SemaphoreType.DMA((3 * nl,)), pltpu.SemaphoreType.DMA((3 * nl,)),
                   *[pltpu.HBM(v.shape, v.dtype) for v in ops], jax.ShapeDtypeStruct((8, LANES), F32)),
        in_specs=[hbm] * (ns + nl),
        out_specs=(pl.BlockSpec(memory_space=pltpu.SEMAPHORE), pl.BlockSpec(memory_space=pltpu.SEMAPHORE),
                   *[hbm] * (ns + nl), pl.BlockSpec(memory_space=pltpu.VMEM)),
        input_output_aliases={t: 2 + t for t in range(ns + nl)},
        compiler_params=pltpu.CompilerParams(has_side_effects=pltpu.SideEffectType.DATAFLOW_SIDE_EFFECTING),
    )(*ops)
    return res[0], res[1], list(res[2:2 + ns]), list(res[2 + ns:2 + ns + nl]), res[-1]


def exchange_wait(send_sems, recv_sems, srcs, lands, after, kinds, gather, name):
    ns, nl = len(srcs), len(lands)
    hbm = pl.BlockSpec(memory_space=pltpu.HBM)

    def body(*refs):
        ins, lnd = refs[:ns], refs[ns:ns + nl]
        ssem, rsem = refs[ns + nl], refs[ns + nl + 1]
        for send, recv in _exchange_copies(ins, lnd, kinds, gather, ssem, rsem):
            send.wait_send()
            recv.wait_recv()

    res = pl.pallas_call(
        body, name=name,
        out_shape=tuple(pltpu.HBM(v.shape, v.dtype) for v in (*srcs, *lands)),
        in_specs=[hbm] * (ns + nl) + [pl.BlockSpec(memory_space=pltpu.SEMAPHORE)] * 2 + [ANY],
        out_specs=tuple([hbm] * (ns + nl)),
        input_output_aliases={t: t for t in range(ns + nl)},
        compiler_params=pltpu.CompilerParams(has_side_effects=pltpu.SideEffectType.DATAFLOW_SIDE_EFFECTING),
    )(*srcs, *lands, send_sems, recv_sems, after)
    return list(res[:ns]), list(res[ns:])


def swap_with_sibling(parts):
    n = len(parts)

    def body(*refs):
        ins, outs = refs[:n], refs[n:2 * n]
        send_sems, recv_sems = refs[2 * n:]
        x, y, c, _ = _position()
        cps = []
        for t in range(n):
            cp = pltpu.make_async_remote_copy(
                src_ref=ins[t], dst_ref=outs[t], send_sem=send_sems.at[t], recv_sem=recv_sems.at[t],
                device_id=(x, y, 1 - c), device_id_type=MESH)
            cp.start()
            cps.append(cp)
        for cp in cps:
            cp.wait_recv()
        for cp in cps:
            cp.wait_send()

    return pl.pallas_call(
        body, name="swap_with_sibling",
        out_shape=[jax.ShapeDtypeStruct(p.shape, p.dtype) for p in parts],
        in_specs=[ANY] * n, out_specs=[ANY] * n,
        scratch_shapes=[pltpu.SemaphoreType.DMA((n,)), pltpu.SemaphoreType.DMA((n,))],
    )(*parts)


def _modulated(xv, mv_ref):
    g, shift, scale = mv_ref[0:1, :], mv_ref[1:2, :], mv_ref[2:3, :]
    r = lax.rsqrt(jnp.mean(xv * xv, axis=-1, keepdims=True) + EPS)
    xhat = xv * r
    xn = xhat * g
    return xn * (1.0 + scale) + shift, xhat, xn, r, g, scale


def modmm(x, mv, mv_idx, w, w_spec, n_tiles, out_shape, out_spec, name, want_h):
    s, d = x.shape
    tm = _wide_tile(s)

    def body(x_ref, mv_ref, w_ref, *rest):
        if want_h:
            out_ref, h_ref, h_scr = rest
        else:
            out_ref, h_scr = rest

        @pl.when(pl.program_id(1) == 0)
        def _():
            h = _modulated(x_ref[...], mv_ref)[0].astype(BF16)
            h_scr[...] = h
            if want_h:
                h_ref[...] = h

        out_ref[...] = _nn(h_scr[...], w_ref[...]).astype(out_ref.dtype)

    out_shapes = [out_shape]
    out_specs = [out_spec]
    if want_h:
        out_shapes.append(jax.ShapeDtypeStruct((s, d), BF16))
        out_specs.append(pl.BlockSpec((tm, d), lambda i, n: (i, 0)))
    res = pl.pallas_call(
        body, name=name, grid=(s // tm, n_tiles),
        in_specs=[pl.BlockSpec((tm, d), lambda i, n: (i, 0)),
                  pl.BlockSpec((None, None, 8, d), lambda i, n: (*mv_idx, 0, 0)), w_spec],
        out_specs=out_specs, out_shape=out_shapes,
        scratch_shapes=[pltpu.VMEM((tm, d), BF16)],
        compiler_params=_params(2),
    )(x, mv, w)
    return res if want_h else (res[0], None)


def resmm(x, mv, mv_idx, coef, lhs, lhs_spec, w, w_spec, k_tiles, tk, name, ffn):
    s, d = x.shape
    tm = _row_tile(s)
    kdim = k_tiles * tk

    def body(x_ref, mv_ref, lhs_ref, w_ref, xo_ref, y_ref, *rest):
        if ffn:
            u_ref, acc = rest
        else:
            (acc,) = rest
        k = pl.program_id(1)

        @pl.when(k == 0)
        def _():
            acc[...] = jnp.zeros_like(acc)

        if ffn:
            ag = lhs_ref[0].astype(F32)
            au = lhs_ref[1].astype(F32)
            left = (ag * _sigmoid_fast(ag) * au).astype(BF16)
            u_ref[...] = left
        else:
            left = lhs_ref[...]
        acc[...] += _nn(left, w_ref[...])

        @pl.when(k == k_tiles - 1)
        def _():
            y = acc[...]
            y_ref[...] = y.astype(BF16)
            xo_ref[...] = x_ref[...] + (coef * mv_ref[3:4, :]) * y

    row = pl.BlockSpec((tm, d), lambda i, k: (i, 0))
    out_shapes = [jax.ShapeDtypeStruct((s, d), F32), jax.ShapeDtypeStruct((s, d), BF16)]
    out_specs = [row, row]
    if ffn:
        out_shapes.append(jax.ShapeDtypeStruct((s, kdim), BF16))
        out_specs.append(pl.BlockSpec((tm, tk), lambda i, k: (i, k)))
    return pl.pallas_call(
        body, name=name, grid=(s // tm, k_tiles),
        in_specs=[row, pl.BlockSpec((None, None, 8, d), lambda i, k: (*mv_idx, 0, 0)), lhs_spec, w_spec],
        out_specs=out_specs, out_shape=out_shapes,
        scratch_shapes=[pltpu.VMEM((tm, d), F32)],
        compiler_params=_params(2),
    )(x, mv, lhs, w)


def resmm_bwd(dxo, y, mv, mv_idx, coef, w, w_spec, k_tiles, tk, name, a=None):
    s, d = dxo.shape
    tm = _row_tile(s)
    kdim = k_tiles * tk
    ffn = a is not None

    def body(dxo_ref, y_ref, mv_ref, w_ref, *rest):
        if ffn:
            a_ref, dy_ref, dl_ref, dgate_ref, dy_scr = rest
        else:
            dy_ref, dl_ref, dgate_ref, dy_scr = rest
        i, k = pl.program_id(0), pl.program_id(1)

        @pl.when((i == 0) & (k == 0))
        def _():
            dgate_ref[...] = jnp.zeros_like(dgate_ref)

        @pl.when(k == 0)
        def _():
            dxv = dxo_ref[...]
            dy = ((coef * mv_ref[3:4, :]) * dxv).astype(BF16)
            dy_scr[...] = dy
            dy_ref[...] = dy
            dgate_ref[0:1, :] += jnp.sum(coef * dxv * y_ref[...].astype(F32), axis=0, keepdims=True)

        dl = _nt(dy_scr[...], w_ref[...])
        if ffn:
            ag = a_ref[0].astype(F32)
            au = a_ref[1].astype(F32)
            sg = _sigmoid_fast(ag)
            dl_ref[0] =(dl * au * (sg * (1.0 + ag * (1.0 - sg)))).astype(BF16)
            dl_ref[1] = (dl * (ag * sg)).astype(BF16)
        else:
            dl_ref[...] = dl.astype(BF16)

    row = pl.BlockSpec((tm, d), lambda i, k: (i, 0))
    in_specs = [row, row, pl.BlockSpec((None, None, 8, d), lambda i, k: (*mv_idx, 0, 0)), w_spec]
    ops = [dxo, y, mv, w]
    if ffn:
        in_specs.append(pl.BlockSpec((2, tm, tk), lambda i, k: (0, i, k)))
        ops.append(a)
        dl_shape = jax.ShapeDtypeStruct((2, s, kdim), BF16)
        dl_spec = pl.BlockSpec((2, tm, tk), lambda i, k: (0, i, k))
    else:
        dl_shape = jax.ShapeDtypeStruct((s, kdim), BF16)
        dl_spec = pl.BlockSpec((tm, tk), lambda i, k: (i, k))
    return pl.pallas_call(
        body, name=name, grid=(s // tm, k_tiles),
        in_specs=in_specs,
        out_specs=[row, dl_spec, pl.BlockSpec((8, d), lambda i, k: (0, 0))],
        out_shape=[jax.ShapeDtypeStruct((s, d), BF16), dl_shape, jax.ShapeDtypeStruct((8, d), F32)],
        scratch_shapes=[pltpu.VMEM((tm, d), BF16)],
        compiler_params=_params(2),
    )(*ops)


def modmm_bwd(dl, dl_spec, w, w_spec, n_tiles, x, dxo, mv, mv_idx, name, more=None):
    s, d = x.shape
    tm = _row_tile(s)

    def body(dl_ref, w_ref, x_ref, dxo_ref, mv_ref, *rest):
        if more is not None:
            dl2_ref, w2_ref, dx_ref, red_ref, acc = rest
        else:
            dx_ref, red_ref, acc = rest
        i, n = pl.program_id(0), pl.program_id(1)

        @pl.when((i == 0) & (n == 0))
        def _():
            red_ref[...] = jnp.zeros_like(red_ref)

        @pl.when(n == 0)
        def _():
            if more is not None:
                acc[...] = _nt(dl2_ref[...], w2_ref[...])
            else:
                acc[...] = jnp.zeros_like(acc)

        acc[...] += _nt(dl_ref[...], w_ref[...])

        @pl.when(n == n_tiles - 1)
        def _():
            dh = acc[...]
            _, xhat, xn, r, g, scale = _modulated(x_ref[...], mv_ref)
            dxn = dh * (1.0 + scale)
            red_ref[0:1, :] += jnp.sum(dxn * xhat, axis=0, keepdims=True)
            red_ref[1:2, :] += jnp.sum(dh, axis=0, keepdims=True)
            red_ref[2:3, :] += jnp.sum(dh * xn, axis=0, keepdims=True)
            gd = dxn * g
            dx_ref[...] = dxo_ref[...] + r * (gd - xhat * jnp.mean(gd * xhat, axis=-1, keepdims=True))

    row = pl.BlockSpec((tm, d), lambda i, n: (i, 0))
    in_specs = [dl_spec, w_spec, row, row, pl.BlockSpec((None, None, 8, d), lambda i, n: (*mv_idx, 0, 0))]
    ops = [dl, w, x, dxo, mv]
    if more is not None:
        in_specs += [more[1], more[2]]
        ops += [more[0], w]
    return pl.pallas_call(
        body, name=name, grid=(s // tm, n_tiles),
        in_specs=in_specs,
        out_specs=[row, pl.BlockSpec((8, d), lambda i, n: (0, 0))],
        out_shape=[jax.ShapeDtypeStruct((s, d), F32), jax.ShapeDtypeStruct((8, d), F32)],
        scratch_shapes=[pltpu.VMEM((tm, d), F32)],
        compiler_params=_params(2),
    )(*ops)


def weight_grad(a, a_spec, b, b_spec, grid_mn, s, bm, bn, dest, out_spec, name):
    tk = _wide_tile(s)
    k_tiles = s // tk

    def body(a_ref, b_ref, dest_ref, out_ref, acc):
        k = pl.program_id(2)

        @pl.when(k == 0)
        def _():
            acc[...] = jnp.zeros_like(acc)

        acc[...] += _tn(a_ref[...], b_ref[...])

        @pl.when(k == k_tiles - 1)
        def _():
            out_ref[...] = acc[...].astype(out_ref.dtype)

    return pl.pallas_call(
        body, name=name, grid=(*grid_mn, k_tiles),
        in_specs=[a_spec, b_spec, ANY], out_specs=out_spec,
        out_shape=jax.ShapeDtypeStruct(dest.shape, dest.dtype),
        input_output_aliases={2: 0},
        scratch_shapes=[pltpu.VMEM((bm, bn), F32)],
        compiler_params=_params(3),
    )(a, b, dest)


def _head_mean(v):
    lane = lax.broadcasted_iota(jnp.int32, v.shape, 1)
    lo = jnp.sum(jnp.where(lane < HEAD_DIM, v, 0.0), axis=-1, keepdims=True)
    hi = jnp.sum(v, axis=-1, keepdims=True) - lo
    return jnp.where(lane < HEAD_DIM, lo, hi) * (1.0 / HEAD_DIM)


def qknorm_fwd(proj, gains, width, name):
    s = proj.shape[0]
    nqk = gains.shape[1]
    tm = _row_tile(s)

    def body(p_ref, g_ref, o_ref):
        for cc in range(width // LANES):
            sl = slice(cc * LANES, (cc + 1) * LANES)
            xv = p_ref[:, sl].astype(F32)
            r = lax.rsqrt(_head_mean(xv * xv) + EPS)
            o_ref[:, sl] = (xv * r * g_ref[:, sl]).astype(BF16)

    blk = pl.BlockSpec((tm, width), lambda i, c: (i, c))
    return pl.pallas_call(
        body, name=name, grid=(s // tm, nqk // width),
        in_specs=[blk, pl.BlockSpec((1, width), lambda i, c: (0, c))],
        out_specs=blk, out_shape=jax.ShapeDtypeStruct((s, nqk), BF16),
        compiler_params=_params(2),
    )(proj, gains)


def qknorm_bwd(proj, gains, d, d_spec, n_cols, width, name):
    s = proj.shape[0]
    nqk = gains.shape[1] // width
    n_blocks = n_cols // width
    tm = _row_tile(s)

    def body(p_ref, g_ref, d_ref, o_ref, dg_ref):
        c, i = pl.program_id(0), pl.program_id(1)

        @pl.when(i == 0)
        def _():
            dg_ref[...] = jnp.zeros_like(dg_ref)

        @pl.when(c < nqk)
        def _():
            for cc in range(width // LANES):
                sl = slice(cc * LANES, (cc + 1) * LANES)
                xv = p_ref[:, sl].astype(F32)
                r = lax.rsqrt(_head_mean(xv * xv) + EPS)
                xhat = xv * r
                dv = d_ref[:, sl]
                gd = dv * g_ref[:, sl]
                o_ref[:, sl] = (r * (gd - xhat * _head_mean(gd * xhat))).astype(BF16)
                dg_ref[0:1, sl] += jnp.sum(dv * xhat, axis=0, keepdims=True)

        @pl.when(c >= nqk)
        def _():
            o_ref[...] = d_ref[...].astype(BF16)

    return pl.pallas_call(
        body, name=name, grid=(n_blocks, s // tm),
        in_specs=[pl.BlockSpec((tm, width), lambda c, i: (i, c)),
                  pl.BlockSpec((1, width), lambda c, i: (0, jnp.minimum(c, nqk - 1))), d_spec],
        out_specs=[pl.BlockSpec((tm, width), lambda c, i: (i, c)), pl.BlockSpec((8, width), lambda c, i: (0, c))],
        out_shape=[jax.ShapeDtypeStruct((s, n_cols), BF16), jax.ShapeDtypeStruct((8, n_cols), F32)],
        compiler_params=_params(2),
    )(proj, gains, d)


def _swa_mask(first):
    qi = lax.broadcasted_iota(jnp.int32, (BLOCK, 2 * BLOCK), 0) + BLOCK
    kj = lax.broadcasted_iota(jnp.int32, (BLOCK, 2 * BLOCK), 1)
    dist = qi - kj
    return (dist >= 0) & (dist < BLOCK) & ((kj >= BLOCK) | jnp.logical_not(first))


def swa_fwd(qkn, proj, bias, sink, d, name):
    s = qkn.shape[0]
    hq = d // HEAD_DIM
    hkv = hq // GROUP
    kw = hkv * HEAD_DIM
    nblk = s // BLOCK
    kcol = d // kw

    def body(q_ref, kc_ref, kp_ref, vc_ref, vp_ref, bias_ref, sink_ref, o_ref, lse_ref):
        mask = _swa_mask(pl.program_id(0) == 0)
        lse_ref[...] = jnp.zeros_like(lse_ref)
        for kvh in range(hkv):
            cols = slice(kvh * HEAD_DIM, (kvh + 1) * HEAD_DIM)
            k2 = jnp.concatenate([kp_ref[:, cols], kc_ref[:, cols]], axis=0)
            v2 = jnp.concatenate([vp_ref[:, cols], vc_ref[:, cols]], axis=0)
            for g in range(GROUP):
                h = kvh * GROUP + g
                hc = slice(h * HEAD_DIM, (h + 1) * HEAD_DIM)
                sc = jnp.where(mask, _nt(q_ref[:, hc], k2) + bias_ref[h], NEG)
                sk = sink_ref[0, h]
                m = jnp.maximum(jnp.max(sc, axis=-1, keepdims=True), sk)
                p = jnp.exp(sc - m)
                denom = jnp.sum(p, axis=-1, keepdims=True) + jnp.exp(sk - m)
                o_ref[:, hc] = (_nn(p.astype(BF16), v2) / denom).astype(BF16)
                lse_ref[:, h:h + 1] = m + jnp.log(denom)

    prev = lambda i: jnp.maximum(i - 1, 0)
    return pl.pallas_call(
        body, name=name, grid=(nblk,),
        in_specs=[pl.BlockSpec((BLOCK, d), lambda i: (i, 0)),
                  pl.BlockSpec((BLOCK, kw), lambda i: (i, kcol)),
                  pl.BlockSpec((BLOCK, kw), lambda i: (prev(i), kcol)),
                  pl.BlockSpec((BLOCK, kw), lambda i: (i, kcol + 1)),
                  pl.BlockSpec((BLOCK, kw), lambda i: (prev(i), kcol + 1)),
                  pl.BlockSpec((hq, BLOCK, 2 * BLOCK), lambda i: (0, 0, 0)),
                  pl.BlockSpec(memory_space=pltpu.SMEM)],
        out_specs=[pl.BlockSpec((BLOCK, d), lambda i: (i, 0)), pl.BlockSpec((BLOCK, LANES), lambda i: (i, 0))],
        out_shape=[jax.ShapeDtypeStruct((s, d), BF16), jax.ShapeDtypeStruct((s, LANES), F32)],
        compiler_params=_params(1),
    )(qkn, qkn, qkn, proj, proj, bias, sink)


def swa_bwd(qkn, proj, bias, sink, do, o, lse, d, name):
    s = qkn.shape[0]
    hq = d // HEAD_DIM
    hkv = hq // GROUP
    kw = hkv * HEAD_DIM
    nblk = s // BLOCK
    kcol = d // kw
    wide = d + 2 * kw

    def body(q_ref, kc_ref, kp_ref, vc_ref, vp_ref, bias_ref, sink_ref, do_ref, o_ref, lse_ref,
             out_ref, dbias_ref, dsink_ref, carry, fresh):
        i = pl.program_id(0)

        @pl.when(i == 0)
        def _():
            dbias_ref[...] = jnp.zeros_like(dbias_ref)
            dsink_ref[...] = jnp.zeros_like(dsink_ref)
            carry[...] = jnp.zeros_like(carry)

        @pl.when(i == nblk)
        def _():
            fresh[...] = jnp.zeros_like(fresh)

        @pl.when(i < nblk)
        def _():
            mask = _swa_mask(i == 0)
            for kvh in range(hkv):
                cols = slice(kvh * HEAD_DIM, (kvh + 1) * HEAD_DIM)
                k2 = jnp.concatenate([kp_ref[:, cols], kc_ref[:, cols]], axis=0)
                v2 = jnp.concatenate([vp_ref[:, cols], vc_ref[:, cols]], axis=0)
                dk2 = jnp.zeros((2 * BLOCK, HEAD_DIM), F32)
                dv2 = jnp.zeros((2 * BLOCK, HEAD_DIM), F32)
                for g in range(GROUP):
                    h = kvh * GROUP + g
                    hc = slice(h * HEAD_DIM, (h + 1) * HEAD_DIM)
                    q = q_ref[:, hc]
                    dov = do_ref[:, hc]
                    lse_h = lse_ref[:, h:h + 1]
                    sc = jnp.where(mask, _nt(q, k2) + bias_ref[h], NEG)
                    p = jnp.exp(sc - lse_h)
                    delta = jnp.sum(dov.astype(F32) * o_ref[:, hc].astype(F32), axis=-1, keepdims=True)
                    ds = p * (_nt(dov, v2) - delta)
                    dbias_ref[h] += ds
                    dsink_ref[0:1, h:h + 1] += jnp.sum(-jnp.exp(sink_ref[0, h] - lse_h) * delta, axis=0, keepdims=True)
                    dsb = ds.astype(BF16)
                    fresh[0, :, hc] = _nn(dsb, k2)
                    dk2 += _tn(dsb, q)
                    dv2 += _tn(p.astype(BF16), dov)
                kc_cols = slice(d + kvh * HEAD_DIM, d + (kvh + 1) * HEAD_DIM)
                vc_cols = slice(d + kw + kvh * HEAD_DIM, d + kw + (kvh + 1) * HEAD_DIM)
                fresh[0, :, kc_cols] = dk2[BLOCK:]
                fresh[0, :, vc_cols] = dv2[BLOCK:]
                fresh[1, :, kc_cols] = dk2[:BLOCK]
                fresh[1, :, vc_cols] = dv2[:BLOCK]

        lane = lax.broadcasted_iota(jnp.int32, (BLOCK, wide), 1)
        out_ref[...] = carry[...] + jnp.where(lane >= d, fresh[1], 0.0)

        @pl.when(i < nblk)
        def _():
            carry[...] = fresh[0]

    cur = lambda i: jnp.minimum(i, nblk - 1)
    prev = lambda i: jnp.maximum(jnp.minimum(i, nblk - 1) - 1, 0)
    return pl.pallas_call(
        body, name=name, grid=(nblk + 1,),
        in_specs=[pl.BlockSpec((BLOCK, d), lambda i: (cur(i), 0)),
                  pl.BlockSpec((BLOCK, kw), lambda i: (cur(i), kcol)),
                  pl.BlockSpec((BLOCK, kw), lambda i: (prev(i), kcol)),
                  pl.BlockSpec((BLOCK, kw), lambda i: (cur(i), kcol + 1)),
                  pl.BlockSpec((BLOCK, kw), lambda i: (prev(i), kcol + 1)),
                  pl.BlockSpec((hq, BLOCK, 2 * BLOCK), lambda i: (0, 0, 0)),
                  pl.BlockSpec(memory_space=pltpu.SMEM),
                  pl.BlockSpec((BLOCK, d), lambda i: (cur(i), 0)),
                  pl.BlockSpec((BLOCK, d), lambda i: (cur(i), 0)),
                  pl.BlockSpec((BLOCK, LANES), lambda i: (cur(i), 0))],
        out_specs=[pl.BlockSpec((BLOCK, wide), lambda i: (jnp.maximum(i - 1, 0), 0)),
                   pl.BlockSpec((hq, BLOCK, 2 * BLOCK), lambda i: (0, 0, 0)),
                   pl.BlockSpec((8, LANES), lambda i: (0, 0))],
        out_shape=[jax.ShapeDtypeStruct((s, wide), F32), jax.ShapeDtypeStruct((hq, BLOCK, 2 * BLOCK), F32),
                   jax.ShapeDtypeStruct((8, LANES), F32)],
        scratch_shapes=[pltpu.VMEM((BLOCK, wide), F32), pltpu.VMEM((2, BLOCK, wide), F32)],
        compiler_params=_params(1),
    )(qkn, qkn, qkn, proj, proj, bias, sink, do, o, lse)


def _rel_bucket_table():
    qi = np.arange(BLOCK)[:, None] + BLOCK
    kj = np.arange(2 * BLOCK)[None, :]
    n = np.maximum(qi - kj, 0)
    max_exact = REL_BUCKETS // 2
    nf = np.maximum(n, 1).astype(np.float32)
    large = max_exact + (np.log(nf / max_exact) / math.log(REL_MAX_DIST / max_exact)
                         * (REL_BUCKETS - max_exact)).astype(np.int32)
    large = np.minimum(large, REL_BUCKETS - 1)
    return np.where(n < max_exact, n, large).astype(np.int32)


def rel_bias_table(rel_bias, bucket):
    hq = rel_bias.shape[1]

    def body(rb_ref, bucket_ref, out_ref):
        tbl = bucket_ref[...]

        def per_head(h, carry):
            def per_bucket(b, acc):
                return jnp.where(tbl == b, rb_ref[b, h], acc)

            out_ref[h] = lax.fori_loop(0, REL_BUCKETS, per_bucket, jnp.zeros(tbl.shape, F32))
            return carry

        lax.fori_loop(0, hq, per_head, 0)

    return pl.pallas_call(
        body, name="rel_bias_table",
        in_specs=[pl.BlockSpec(memory_space=pltpu.SMEM), pl.BlockSpec(memory_space=pltpu.VMEM)],
        out_specs=pl.BlockSpec(memory_space=pltpu.VMEM),
        out_shape=jax.ShapeDtypeStruct((hq,) + tuple(bucket.shape), F32),
    )(rel_bias, bucket)


def rel_bias_grad(dbias, bucket):
    n_layers, hq = dbias.shape[:2]

    def body(db_ref, bucket_ref, out_ref):
        tbl = bucket_ref[...]

        def per_head(h, carry):
            dsum = db_ref[0, h]
            for a in range(1, n_layers):
                dsum = dsum + db_ref[a, h]

            def per_bucket(b, carry2):
                out_ref[b, h] = jnp.sum(jnp.where(tbl == b, dsum, 0.0))
                return carry2

            return lax.fori_loop(0, REL_BUCKETS, per_bucket, carry)

        lax.fori_loop(0, hq, per_head, 0)

    return pl.pallas_call(
        body, name="rel_bias_grad",
        in_specs=[pl.BlockSpec(memory_space=pltpu.VMEM), pl.BlockSpec(memory_space=pltpu.VMEM)],
        out_specs=pl.BlockSpec(memory_space=pltpu.SMEM),
        out_shape=jax.ShapeDtypeStruct((REL_BUCKETS, hq), F32),
    )(dbias, bucket)


def _split3(v):
    hi = v.astype(BF16)
    r1 = v - hi.astype(F32)
    mid = r1.astype(BF16)
    lo = (r1 - mid.astype(F32)).astype(BF16)
    return hi, mid, lo


def _tri_sum(tri, v):
    hi, mid, lo = _split3(v)
    return _nn(tri, hi) + _nn(tri, mid) + _nn(tri, lo)


def fox_gates(fl, b_f, name):
    s = fl.shape[0]
    t = _row_tile(s)

    def body(fl_ref, b_ref, f_ref, carry):
        @pl.when(pl.program_id(0) == 0)
        def _():
            carry[...] = jnp.zeros_like(carry)

        z = fl_ref[...] + b_ref[...]
        logf = jnp.minimum(z, 0.0) - jnp.log(1.0 + jnp.exp(-jnp.abs(z)))
        r = lax.broadcasted_iota(jnp.int32, (t, t), 0)
        cidx = lax.broadcasted_iota(jnp.int32, (t, t), 1)
        tri = jnp.where(cidx <= r, 1.0, 0.0).astype(BF16)
        f = _tri_sum(tri, logf) + carry[0:1, :]
        f_ref[...] = f
        carry[0:1, :] = f_ref[t - 1:t, :]

    blk = pl.BlockSpec((t, LANES), lambda i: (i, 0))
    return pl.pallas_call(
        body, name=name, grid=(s // t,),
        in_specs=[blk, pl.BlockSpec((1, LANES), lambda i: (0, 0))],
        out_specs=blk, out_shape=jax.ShapeDtypeStruct((s, LANES), F32),
        scratch_shapes=[pltpu.VMEM((8, LANES), F32)],
        compiler_params=_params(1),
    )(fl, b_f)


def fox_gates_bwd(fl, b_f, df_query, df_key, name):
    s = fl.shape[0]
    t = _row_tile(s)
    nb = s // t

    def body(fl_ref, b_ref, dfq_ref, dfk_ref, dfl_ref, db_ref, carry):
        @pl.when(pl.program_id(0) == 0)
        def _():
            carry[...] = jnp.zeros_like(carry)
            db_ref[...] = jnp.zeros_like(db_ref)

        dfv = dfq_ref[...] + dfk_ref[...]
        r = lax.broadcasted_iota(jnp.int32, (t, t), 0)
        cidx = lax.broadcasted_iota(jnp.int32, (t, t), 1)
        tri = jnp.where(cidx >= r, 1.0, 0.0).astype(BF16)
        dlog = _tri_sum(tri, dfv) + carry[0:1, :]
        carry[0:1, :] += jnp.sum(dfv, axis=0, keepdims=True)
        z = fl_ref[...] + b_ref[...]
        dz = dlog * (1.0 - _sigmoid(z))
        dfl_ref[...] = dz.astype(BF16)
        db_ref[0:1, :] += jnp.sum(dz, axis=0, keepdims=True)

    rev = pl.BlockSpec((t, LANES), lambda i: (nb - 1 - i, 0))
    return pl.pallas_call(
        body, name=name, grid=(nb,),
        in_specs=[rev, pl.BlockSpec((1, LANES), lambda i: (0, 0)), rev, rev],
        out_specs=[rev, pl.BlockSpec((8, LANES), lambda i: (0, 0))],
        out_shape=[jax.ShapeDtypeStruct((s, LANES), BF16), jax.ShapeDtypeStruct((8, LANES), F32)],
        scratch_shapes=[pltpu.VMEM((8, LANES), F32)],
        compiler_params=_params(1),
    )(fl, b_f, df_query, df_key)


def fox_fwd(qkn, proj, f_col, f_row, d, name):
    s = qkn.shape[0]
    t = _attn_tile(s)
    hs, wide = FOX_HEADS, FOX_HEADS * HEAD_DIM
    n_pairs = d // wide
    nt = s // t

    def body(q_ref, k_ref, v_ref, fq_ref, fk_ref, o_ref, o32_ref, lse_ref, m_scr, l_scr, acc):
        i, j = pl.program_id(1), pl.program_id(2)

        @pl.when(j == 0)
        def _():
            m_scr[...] = jnp.full_like(m_scr, NEG)
            l_scr[...] = jnp.zeros_like(l_scr)
            acc[...] = jnp.zeros_like(acc)

        @pl.when(j <= i)
        def _():
            krow = lax.broadcasted_iota(jnp.int32, (t, t), 0)
            qcol = lax.broadcasted_iota(jnp.int32, (t, t), 1)
            visible = (krow <= qcol) | (j < i)
            for hh in range(hs):
                hc = slice(hh * HEAD_DIM, (hh + 1) * HEAD_DIM)
                st = _nt(k_ref[:, hc], q_ref[:, hc]) + fq_ref[hh] - fk_ref[hh]
                st = jnp.where(visible, st, NEG)
                m_prev = m_scr[hh]
                m_new = jnp.maximum(m_prev, jnp.max(st, axis=0, keepdims=True))
                alpha = jnp.exp(m_prev - m_new)
                pt = jnp.exp(st - m_new)
                l_scr[hh] = alpha * l_scr[hh] + jnp.sum(pt, axis=0, keepdims=True)
                acc[hc, :] = alpha * acc[hc, :] + _tn(v_ref[:, hc], pt.astype(BF16))
                m_scr[hh] = m_new

        @pl.when(j == i)
        def _():
            l_full = jnp.concatenate([jnp.broadcast_to(l_scr[hh], (HEAD_DIM, t)) for hh in range(hs)], axis=0)
            ov = (acc[...] / l_full).T
            o_ref[...] = ov.astype(BF16)
            o32_ref[...] = ov
            lse_ref[...] = m_scr[...] + jnp.log(l_scr[...])

    kv = lambda j, i: jnp.minimum(j, i)
    return pl.pallas_call(
        body, name=name, grid=(n_pairs, nt, nt),
        in_specs=[pl.BlockSpec((t, wide), lambda p, i, j: (i, p)),
                  pl.BlockSpec((t, wide), lambda p, i, j: (kv(j, i), n_pairs + p)),
                  pl.BlockSpec((t, wide), lambda p, i, j: (kv(j, i), 2 * n_pairs + p)),
                  pl.BlockSpec((hs, 1, t), lambda p, i, j: (p, 0, i)),
                  pl.BlockSpec((hs, t, 1), lambda p, i, j: (p, kv(j, i), 0))],
        out_specs=[pl.BlockSpec((t, wide), lambda p, i, j: (i, p)),
                   pl.BlockSpec((t, wide), lambda p, i, j: (i, p)),
                   pl.BlockSpec((hs, 1, t), lambda p, i, j: (p, 0, i))],
        out_shape=[jax.ShapeDtypeStruct((s, d), BF16), jax.ShapeDtypeStruct((s, d), F32),
                   jax.ShapeDtypeStruct((hs * n_pairs, 1, s), F32)],
        scratch_shapes=[pltpu.VMEM((hs, 1, t), F32), pltpu.VMEM((hs, 1, t), F32), pltpu.VMEM((wide, t), F32)],
        compiler_params=_params(3),
    )(qkn, qkn, proj, f_row, f_col)


def fox_bwd(qkn, proj, f_col, f_row, lse_row, do, o, d, name):
    s = qkn.shape[0]
    t = _attn_tile(s)
    hs, wide = FOX_HEADS, FOX_HEADS * HEAD_DIM
    n_pairs = d // wide
    nt = s // t

    def body(q_ref, k_ref, v_ref, fk_ref, fq_ref, lse_ref, do_ref, o_ref, out_ref, df_ref, dfq_ref,
             dq_acc, dkv_acc, df_acc, dfq_acc):
        j, i = pl.program_id(1), pl.program_id(2)

        @pl.when((j == 0) & (i == 0))
        def _():
            dq_acc[...] = jnp.zeros_like(dq_acc)
            dfq_acc[...] = jnp.zeros_like(dfq_acc)

        @pl.when(i == 0)
        def _():
            dkv_acc[...] = jnp.zeros_like(dkv_acc)
            df_acc[...] = jnp.zeros_like(df_acc)

        @pl.when(i >= j)
        def _():
            krow = lax.broadcasted_iota(jnp.int32, (t, t), 0)
            qcol = lax.broadcasted_iota(jnp.int32, (t, t), 1)
            visible = (krow <= qcol) | (i > j)
            ones = jnp.ones((8, HEAD_DIM), BF16)
            for hh in range(hs):
                hc = slice(hh * HEAD_DIM, (hh + 1) * HEAD_DIM)
                q, k, v, dov = q_ref[:, hc], k_ref[:, hc], v_ref[:, hc], do_ref[:, hc]
                st = _nt(k, q) + fq_ref[hh] - fk_ref[hh]
                pt = jnp.exp(jnp.where(visible, st, NEG) - lse_ref[hh])
                hi, mid, lo = _split3(dov.astype(F32) * o_ref[:, hc])
                delta = jnp.max(_nt(ones, hi) + _nt(ones, mid) + _nt(ones, lo), axis=0, keepdims=True)
                dst = pt * (_nt(v, dov) - delta)
                dsb = dst.astype(BF16)
                dkv_acc[1, :, hc] += _nn(pt.astype(BF16), dov)
                dkv_acc[0, :, hc] += _nn(dsb, q)
                dq_acc[pl.ds(pl.multiple_of(i * t, t), t), hc] += _tn(dsb, k)
                df_acc[hh] -= jnp.sum(dst, axis=-1, keepdims=True)
                dfq_acc[i, hh] += jnp.sum(dst, axis=0, keepdims=True)

        @pl.when(i == nt - 1)
        def _():
            out_ref[0] = dq_acc[pl.ds(pl.multiple_of(j * t, t), t), :]
            out_ref[1] = dkv_acc[0]
            out_ref[2] = dkv_acc[1]
            df_ref[...] = df_acc[...]
            dfq_ref[...] = dfq_acc[j]

    qi = lambda j, i: jnp.maximum(i, j)
    return pl.pallas_call(
        body, name=name, grid=(n_pairs, nt, nt),
        in_specs=[pl.BlockSpec((t, wide), lambda p, j, i: (qi(j, i), p)),
                  pl.BlockSpec((t, wide), lambda p, j, i: (j, n_pairs + p)),
                  pl.BlockSpec((t, wide), lambda p, j, i: (j, 2 * n_pairs + p)),
                  pl.BlockSpec((hs, t, 1), lambda p, j, i: (p, j, 0)),
                  pl.BlockSpec((hs, 1, t), lambda p, j, i: (p, 0, qi(j, i))),
                  pl.BlockSpec((hs, 1, t), lambda p, j, i: (p, 0, qi(j, i))),
                  pl.BlockSpec((t, wide), lambda p, j, i: (qi(j, i), p)),
                  pl.BlockSpec((t, wide), lambda p, j, i: (qi(j, i), p))],
        out_specs=[pl.BlockSpec((3, t, wide), lambda p, j, i: (0, j, p)),
                   pl.BlockSpec((hs, t, 1), lambda p, j, i: (p, j, 0)),
                   pl.BlockSpec((hs, 1, t), lambda p, j, i: (p, 0, j))],
        out_shape=[jax.ShapeDtypeStruct((3, s, d), F32), jax.ShapeDtypeStruct((hs * n_pairs, s, 1), F32),
                   jax.ShapeDtypeStruct((hs * n_pairs, 1, s), F32)],
        scratch_shapes=[pltpu.VMEM((s, wide), F32), pltpu.VMEM((2, t, wide), F32), pltpu.VMEM((hs, t, 1), F32),
                        pltpu.VMEM((nt, hs, 1, t), F32)],
        compiler_params=_params(3),
    )(qkn, qkn, proj, f_col, f_row, lse_row, do, o)


def loss_head(y, target):
    s, d = y.shape
    tm = _row_tile(s)

    def body(y_ref, t_ref, dy_ref, loss_ref):
        @pl.when(pl.program_id(0) == 0)
        def _():
            loss_ref[...] = jnp.zeros_like(loss_ref)

        diff = y_ref[...] - t_ref[...]
        dy_ref[...] = diff * (1.0 / d)
        loss_ref[...] += 0.5 * jnp.sum(jnp.mean(diff * diff, axis=-1, keepdims=True), axis=0, keepdims=True)

    row = pl.BlockSpec((tm, d), lambda i: (i, 0))
    return pl.pallas_call(
        body, name="loss_head", grid=(s // tm,),
        in_specs=[row, row],
        out_specs=[row, pl.BlockSpec((8, LANES), lambda i: (0, 0))],
        out_shape=[jax.ShapeDtypeStruct((s, d), F32), jax.ShapeDtypeStruct((8, LANES), F32)],
        compiler_params=_params(1),
    )(y, target)


def ada_mod(c_all, w, b):
    n_layers, d, cols = w.shape

    def body(c_ref, w_ref, b_ref, o_ref):
        cv = c_ref[...]
        o_ref[...] = _nn(cv * _sigmoid(cv), w_ref[...]) + b_ref[...]

    return pl.pallas_call(
        body, name="ada_mod", grid=(n_layers,),
        in_specs=[pl.BlockSpec((N_DEV, d), lambda l: (0, 0)), pl.BlockSpec((None, d, cols), lambda l: (l, 0, 0)),
                  pl.BlockSpec((None, 1, cols), lambda l: (l, 0, 0))],
        out_specs=pl.BlockSpec((None, N_DEV, cols), lambda l: (l, 0, 0)),
        out_shape=jax.ShapeDtypeStruct((n_layers, N_DEV, cols), F32),
        compiler_params=_params(1),
    )(c_all, w, b)


def ada_grad(c_t, dmod):
    d = c_t.shape[0]
    n_layers, _, cols = dmod.shape
    tn = cols // 2

    def body(c_ref, dm_ref, o_ref):
        cv = c_ref[...]
        o_ref[...] = _nn(cv * _sigmoid(cv), dm_ref[...])

    return pl.pallas_call(
        body, name="ada_grad", grid=(n_layers, 2),
        in_specs=[pl.BlockSpec((d, N_DEV), lambda l, n: (0, 0)), pl.BlockSpec((None, N_DEV, tn), lambda l, n: (l, 0, n))],
        out_specs=pl.BlockSpec((None, d, tn), lambda l, n: (l, 0, n)),
        out_shape=jax.ShapeDtypeStruct((n_layers, d, cols), F32),
        compiler_params=_params(2),
    )(c_t, dmod)


def sum_devices(v):
    def body(v_ref, o_ref):
        acc = v_ref[0]
        for k in range(1, N_DEV):
            acc = acc + v_ref[k]
        o_ref[...] = acc

    return pl.pallas_call(body, name="sum_devices", out_shape=jax.ShapeDtypeStruct(v.shape[1:], F32))(v)


def sum_slots(r, own):
    _, rows, cols = r.shape
    tm = 256 if rows % 256 == 0 else rows

    def body(r_ref, own_ref, o_ref):
        o_ref[...] = ((own_ref[...].astype(F32) + r_ref[0].astype(F32)) + r_ref[1].astype(F32)) + r_ref[2].astype(F32)

    return pl.pallas_call(
        body, name="sum_slots", grid=(rows // tm,),
        in_specs=[pl.BlockSpec((N_CHIP - 1, tm, cols), lambda i: (0, i, 0)), pl.BlockSpec((tm, cols), lambda i: (i, 0))],
        out_specs=pl.BlockSpec((tm, cols), lambda i: (i, 0)),
        out_shape=jax.ShapeDtypeStruct((rows, cols), F32),
        compiler_params=_params(1),
    )(r, own)


def adamw(w, m, v, g, g2=None):
    rows, cols = w.shape
    tm = 256 if rows % 256 == 0 else rows
    two = g2 is not None
    c1 = 1.0 - ADAM_B1 ** ADAM_STEP
    c2 = 1.0 - ADAM_B2 ** ADAM_STEP

    def body(w_ref, m_ref, v_ref, g_ref, *rest):
        if two:
            g2_ref, go_ref, d_ref, mo_ref, vo_ref = rest
            gv = g_ref[...] + g2_ref[...]
        else:
            go_ref, d_ref, mo_ref, vo_ref = rest
            gv = g_ref[...]
        mn = ADAM_B1 * m_ref[...] + (1.0 - ADAM_B1) * gv
        vn = ADAM_B2 * v_ref[...] + (1.0 - ADAM_B2) * (gv * gv)
        go_ref[...] = gv
        mo_ref[...] = mn
        vo_ref[...] = vn
        d_ref[...] = -ADAM_LR * ((mn / c1) / (jnp.sqrt(vn / c2) + ADAM_EPS) + ADAM_WD * w_ref[...])

    blk = pl.BlockSpec((tm, cols), lambda i: (i, 0))
    ops = [w, m, v, g] + ([g2] if two else [])
    return pl.pallas_call(
        body, name="adamw", grid=(rows // tm,),
        in_specs=[blk] * len(ops), out_specs=[blk] * 4,
        out_shape=[jax.ShapeDtypeStruct((rows, cols), F32)] * 4,
        compiler_params=_params(1),
    )(*ops)


def _pad_rows(flat):
    n = flat.shape[0]
    rows = -(-n // LANES)
    return jnp.pad(flat, (0, rows * LANES - n)).reshape(rows, LANES)


def _pad_rows8(flat):
    rows = _pad_rows(flat)
    return jnp.pad(rows, ((0, -rows.shape[0] % 8), (0, 0)))


def _col_tiles(n):
    return next(k for k in range(1, n // LANES + 1) if n % (k * LANES) == 0 and n // k <= 1536)


def kernel(x, c, ada_w, ada_b, norm_g, ffn_w13, ffn_w2, rel_bias, swa_w_in, swa_w_out, swa_q_g, swa_k_g, swa_sink, fox_w_in, fox_w_out, fox_b_f, fox_q_g, fox_k_g, loss_target, m_ada_w, m_ada_b, m_norm_g, m_ffn_w13, m_ffn_w2, m_rel_bias, m_swa_w_in, m_swa_w_out, m_swa_q_g, m_swa_k_g, m_swa_sink, m_fox_w_in, m_fox_w_out, m_fox_b_f, m_fox_q_g, m_fox_k_g, v_ada_w, v_ada_b, v_norm_g, v_ffn_w13, v_ffn_w2, v_rel_bias, v_swa_w_in, v_swa_w_out, v_swa_q_g, v_swa_k_g, v_swa_sink, v_fox_w_in, v_fox_w_out, v_fox_b_f, v_fox_q_g, v_fox_k_g):
    ix, iy, ic = lax.axis_index("x"), lax.axis_index("y"), lax.axis_index("c")
    chip = 2 * ix + iy
    dev = 2 * chip + ic
    s, d = x.shape[1:]
    n_layers = ada_w.shape[0]
    n_a, n_b = swa_w_in.shape[0], fox_w_in.shape[0]
    hq = d // HEAD_DIM
    hkv = hq // GROUP
    kw = hkv * HEAD_DIM
    c13 = ffn_w13.shape[-1]
    f = 2 * c13
    r2 = ffn_w2.shape[2]
    cq = norm_g.shape[-1]
    a_in = d + 2 * kw
    fx = fox_w_in.shape[-1]
    b_in = N_CHIP * fx
    b_pad = 3 * d + LANES
    x0 = x[0]

    hello = _pad_rows8(jnp.concatenate([c.reshape(-1), norm_g.reshape(-1)]))
    hello_all = all_gather_rows(hello, "gather_c_norm").reshape(N_DEV, -1)
    c_all = hello_all[:, :d]
    ng = hello_all[::2, d:d + n_layers * 3 * cq].reshape(N_CHIP, n_layers, 3, cq)
    norm_full = jnp.moveaxis(ng, 0, 2).reshape(n_layers, 3, d)

    half_cols = ada_w.shape[-1] // 2
    w_half = lax.dynamic_slice_in_dim(ada_w, ic * half_cols, half_cols, axis=2)
    b_half = lax.dynamic_slice_in_dim(ada_b, dev * half_cols, half_cols, axis=1)[:, None, :]
    mod_part = ada_mod(c_all, w_half, b_half)
    mod_all = all_gather_rows(mod_part.reshape(n_layers * N_DEV, half_cols), "gather_mod")
    mod_all = mod_all.reshape(N_DEV, n_layers, N_DEV, half_cols)
    mod_mine = lax.dynamic_index_in_dim(mod_all, dev, axis=2, keepdims=False)
    mod_mine = jnp.moveaxis(mod_mine, 0, 1).reshape(n_layers, 3, 3, d)
    mv = jnp.concatenate([norm_full[:, :, None, :], mod_mine, jnp.zeros((n_layers, 3, 4, d), F32)], axis=2)

    n_groups = n_layers // 2
    kinds = [("col", c13), ("row", r2), ("col", swa_w_in.shape[-1]), ("slot",), ("slot",), ("slot",)]

    def group_buffers(g, after):
        raw = [ffn_w13[2 * g:2 * g + 2], ffn_w2[2 * g:2 * g + 2], swa_w_in[g], swa_w_out[g], fox_w_in[g], fox_w_out[g]]
        return [cast_and_place(r, k, chip, after, f"place_weights_{g}_{t}") for t, (r, k) in enumerate(zip(raw, kinds))]

    def group_weights(fulls):
        w13_g, w2_g, wa_in_g, wa_out_g, wb_slots, wb_out_g = fulls
        wb_in_g = jnp.pad(jnp.concatenate([wb_slots[b] for b in range(N_CHIP)], axis=-1), ((0, 0), (0, b_pad - b_in)))
        return dict(w13=w13_g, w2=w2_g, wa_in=wa_in_g, wa_out=wa_out_g.reshape(d, d), wb_in=wb_in_g,
                    wb_out=wb_out_g.reshape(d, d))

    weights = [group_weights(gather_weights(group_buffers(0, mv), kinds))]
    in_flight = []
    for g in range(1, n_groups):
        in_flight.append(exchange_start([], group_buffers(g, weights[0]["w13"]), kinds, True, f"gather_start_{g}"))
    for *_, token in in_flight:
        mv = mv + token[0, 0]

    bucket = jnp.asarray(_rel_bucket_table())
    bias = rel_bias_table(rel_bias, bucket)
    tm, tw = _row_tile(s), _wide_tile(s)
    n13 = 2 * f // c13
    na_t, nb_t = _col_tiles(a_in), _col_tiles(3 * d)
    wa_t, wb_t = a_in // na_t, 3 * d // nb_t
    gate_blk = 3 * d // LANES

    def ffn_forward(xv, l, half, sub):
        wg, li = weights[l // 2], l % 2
        a, h = modmm(xv, mv, (l, sub), wg["w13"], pl.BlockSpec((None, None, d, c13), lambda i, n: (li, half, 0, n)), n13,
                     jax.ShapeDtypeStruct((2, s, f), BF16),
                     pl.BlockSpec((None, tw, c13), lambda i, n: (n // 2, i, n % 2)), f"ffn_up_{l}_{half}", True)
        xo, y, u = resmm(xv, mv, (l, sub), 0.5, a, pl.BlockSpec((2, tm, c13), lambda i, k: (0, i, k)),
                         wg["w2"], pl.BlockSpec((None, None, c13, d), lambda i, k: (li, half, k, 0)), f // c13, c13,
                         f"ffn_down_{l}_{half}", True)
        return xo, dict(x=xv, h=h, a=a, u=u, y=y)

    saved = []
    xv = x0
    for l in range(n_layers):
        j = l // 2
        if l % 2 == 0 and j >= 1:
            send_sems, recv_sems, shards, fulls, _ = in_flight[j - 1]
            weights.append(group_weights(exchange_wait(send_sems, recv_sems, shards, fulls, xv, kinds, True,
                                                       f"gather_wait_{j}")[1]))
        wg = weights[j]
        xv, s0 = ffn_forward(xv, l, 0, 0)
        if l % 2 == 0:
            proj, h = modmm(xv, mv, (l, 1), wg["wa_in"], pl.BlockSpec((d, wa_t), lambda i, n: (0, n)), na_t,
                            jax.ShapeDtypeStruct((s, a_in), BF16), pl.BlockSpec((tw, wa_t), lambda i, n: (i, n)),
                            f"swa_in_{j}", True)
            gains = jnp.concatenate([jnp.tile(swa_q_g[j] * HEAD_DIM ** -0.5, hq), jnp.tile(swa_k_g[j], hkv)])[None, :]
            qkn = qknorm_fwd(proj, gains, kw, f"swa_qknorm_{j}")
            sink = swa_sink[j][None, :]
            o, lse = swa_fwd(qkn, proj, bias, sink, d, f"swa_attn_{j}")
            s1 = dict(x=xv, h=h, proj=proj, gains=gains, qkn=qkn, sink=sink, o=o, lse=lse)
            w_out = wg["wa_out"]
        else:
            proj, h = modmm(xv, mv, (l, 1), wg["wb_in"], pl.BlockSpec((d, wb_t), lambda i, n: (0, n)), nb_t,
                            jax.ShapeDtypeStruct((s, 3 * d), BF16), pl.BlockSpec((tw, wb_t), lambda i, n: (i, n)),
                            f"fox_in_{j}", True)
            fl, _ = modmm(xv, mv, (l, 1), wg["wb_in"], pl.BlockSpec((d, LANES), lambda i, n: (0, gate_blk)), 1,
                          jax.ShapeDtypeStruct((s, LANES), F32), pl.BlockSpec((tw, LANES), lambda i, n: (i, 0)),
                          f"fox_gate_in_{j}", False)
            b_f = jnp.pad(fox_b_f[j], (0, LANES - hq))[None, :]
            fcum = fox_gates(fl, b_f, f"fox_gates_{j}")
            f_t = fcum[:, :hq].T
            f_col, f_row = f_t[:, :, None], f_t[:, None, :]
            gains = jnp.concatenate([jnp.tile(fox_q_g[j] * HEAD_DIM ** -0.5, hq), jnp.tile(fox_k_g[j], hq)])[None, :]
            qkn = qknorm_fwd(proj, gains, d, f"fox_qknorm_{j}")
            o, o32, lse = fox_fwd(qkn, proj, f_col, f_row, d, f"fox_attn_{j}")
            s1 = dict(x=xv, h=h, proj=proj, gains=gains, qkn=qkn, fl=fl, b_f=b_f, f_col=f_col, f_row=f_row, o=o, o32=o32,
                      lse=lse)
            w_out = wg["wb_out"]
        xv, y = resmm(xv, mv, (l, 1), 1.0, o, pl.BlockSpec((tm, d), lambda i, k: (i, 0)), w_out,
                      pl.BlockSpec((d, d), lambda i, k: (0, 0)), 1, d, f"mixer_out_{l}", False)
        s1["y"] = y
        xv, s2 = ffn_forward(xv, l, 1, 2)
        saved.append((s0, s1, s2))

    dxv, loss_part = loss_head(xv, loss_target[0])
    loss = lax.psum(loss_part[0, 0], ("x", "y", "c"))

    grads = [dict(w13=lax.empty((2, 2, d, 2 * f), BF16), w2=lax.empty((2, 2, f, d), BF16),
                  wa_in=lax.empty((d, a_in), BF16), wa_out=lax.empty((d, d), BF16),
                  wb_in=lax.empty((d, b_pad), BF16), wb_out=lax.empty((d, d), BF16)) for _ in range(n_groups)]
    dmod = [[None] * 3 for _ in range(n_layers)]
    dnorm = [[None] * 3 for _ in range(n_layers)]
    dqk_gain = {}
    dsink, db_f, dbias_tabs = {}, {}, []

    def ffn_backward(dxo, sv, l, half, sub):
        wg, gg, li = weights[l // 2], grads[l // 2], l % 2
        dy, da, dgate = resmm_bwd(dxo, sv["y"], mv, (l, sub), 0.5, wg["w2"],
                                  pl.BlockSpec((None, None, c13, d), lambda i, k: (li, half, k, 0)), f // c13, c13,
                                  f"ffn_down_bwd_{l}_{half}", a=sv["a"])
        gg["w2"] = weight_grad(sv["u"], pl.BlockSpec((tw, c13), lambda m, n, k: (k, m)), dy,
                               pl.BlockSpec((tw, d), lambda m, n, k: (k, 0)), (f // c13, 1), s, c13, d, gg["w2"],
                               pl.BlockSpec((None, None, c13, d), lambda m, n, k: (li, half, m, 0)),
                               f"ffn_w2_grad_{l}_{half}")
        gg["w13"] = weight_grad(sv["h"], pl.BlockSpec((tw, d), lambda m, n, k: (k, 0)), da,
                                pl.BlockSpec((None, tw, c13), lambda m, n, k: (n // 2, k, n % 2)), (1, n13), s, d, c13,
                                gg["w13"], pl.BlockSpec((None, None, d, c13), lambda m, n, k: (li, half, 0, n)),
                                f"ffn_w13_grad_{l}_{half}")
        dx, red = modmm_bwd(da, pl.BlockSpec((None, tm, c13), lambda i, n: (n // 2, i, n % 2)), wg["w13"],
                            pl.BlockSpec((None, None, d, c13), lambda i, n: (li, half, 0, n)), n13, sv["x"], dxo, mv,
                            (l, sub), f"ffn_up_bwd_{l}_{half}")
        dmod[l][sub] = (red[1], red[2], dgate[0])
        dnorm[l][sub] = red[0]
        return dx

    def group_slabs(gg):
        gb_slots = jnp.stack([gg["wb_in"][:, b * fx:(b + 1) * fx] for b in range(N_CHIP)])
        return [gg["w13"], gg["w2"], gg["wa_in"], gg["wa_out"].reshape(N_CHIP, d // N_CHIP, d), gb_slots,
                gg["wb_out"].reshape(N_CHIP, d // N_CHIP, d)]

    shard_shapes = [(2,) + ffn_w13.shape[1:], (2,) + ffn_w2.shape[1:], swa_w_in.shape[1:], swa_w_out.shape[1:],
                    fox_w_in.shape[1:], fox_w_out.shape[1:]]
    scatter_in_flight = {}

    for l in reversed(range(n_layers)):
        j = l // 2
        wg, gg = weights[j], grads[j]
        s0, s1, s2 = saved[l]
        dxv = ffn_backward(dxv, s2, l, 1, 2)
        is_a = l % 2 == 0
        w_out = wg["wa_out"] if is_a else wg["wb_out"]
        dy, do, dgate = resmm_bwd(dxv, s1["y"], mv, (l, 1), 1.0, w_out, pl.BlockSpec((d, d), lambda i, k: (0, 0)),
                                  1, d, f"mixer_out_bwd_{l}")
        out_key = "wa_out" if is_a else "wb_out"
        gg[out_key] = weight_grad(s1["o"], pl.BlockSpec((tw, d), lambda m, n, k: (k, 0)), dy,
                                  pl.BlockSpec((tw, d), lambda m, n, k: (k, 0)), (1, 1), s, d, d, gg[out_key],
                                  pl.BlockSpec((d, d), lambda m, n, k: (0, 0)), f"mixer_out_grad_{l}")
        if is_a:
            d_qkv, dbias_tab, dsk = swa_bwd(s1["qkn"], s1["proj"], bias, s1["sink"], do, s1["o"], s1["lse"], d,
                                            f"swa_attn_bwd_{j}")
            dbias_tabs.append(dbias_tab)
            dsink[j] = dsk[0, :hq]
            dproj, dgain = qknorm_bwd(s1["proj"], s1["gains"], d_qkv, pl.BlockSpec((tm, kw), lambda c_, i: (i, c_)),
                                      a_in, kw, f"swa_qknorm_bwd_{j}")
            gg["wa_in"] = weight_grad(s1["h"], pl.BlockSpec((tw, d), lambda m, n, k: (k, 0)), dproj,
                                      pl.BlockSpec((tw, wa_t), lambda m, n, k: (k, n)), (1, na_t), s, d, wa_t, gg["wa_in"],
                                      pl.BlockSpec((d, wa_t), lambda m, n, k: (0, n)), f"swa_in_grad_{j}")
            dxv, red = modmm_bwd(dproj, pl.BlockSpec((tm, wa_t), lambda i, n: (i, n)), wg["wa_in"],
                                 pl.BlockSpec((d, wa_t), lambda i, n: (0, n)), na_t, s1["x"], dxv, mv, (l, 1),
                                 f"swa_in_bwd_{j}")
            dqk_gain[("a", j)] = (dgain[0, :d].reshape(hq, HEAD_DIM).sum(0) * HEAD_DIM ** -0.5,
                                  dgain[0, d:d + kw].reshape(hkv, HEAD_DIM).sum(0))
        else:
            lse_row = s1["lse"].reshape(hq, 1, s)
            d_qkv, df_col, dfq_row = fox_bwd(s1["qkn"], s1["proj"], s1["f_col"], s1["f_row"], lse_row, do, s1["o32"], d,
                                             f"fox_attn_bwd_{j}")
            lanes_of_heads = lambda a: jnp.pad(a.T, ((0, 0), (0, LANES - hq)))
            dfl, dbf = fox_gates_bwd(s1["fl"], s1["b_f"], lanes_of_heads(dfq_row[:, 0, :]), lanes_of_heads(df_col[:, :, 0]),
                                     f"fox_gates_bwd_{j}")
            db_f[j] = dbf[0, :hq]
            dproj, dgain = qknorm_bwd(s1["proj"], s1["gains"], d_qkv, pl.BlockSpec((None, tm, d), lambda c_, i: (c_, i, 0)),
                                      3 * d, d, f"fox_qknorm_bwd_{j}")
            gg["wb_in"] = weight_grad(s1["h"], pl.BlockSpec((tw, d), lambda m, n, k: (k, 0)), dproj,
                                      pl.BlockSpec((tw, wb_t), lambda m, n, k: (k, n)), (1, nb_t), s, d, wb_t, gg["wb_in"],
                                      pl.BlockSpec((d, wb_t), lambda m, n, k: (0, n)), f"fox_in_grad_{j}")
            gg["wb_in"] = weight_grad(s1["h"], pl.BlockSpec((tw, d), lambda m, n, k: (k, 0)), dfl,
                                      pl.BlockSpec((tw, LANES), lambda m, n, k: (k, 0)), (1, 1), s, d, LANES, gg["wb_in"],
                                      pl.BlockSpec((d, LANES), lambda m, n, k: (0, gate_blk)), f"fox_gate_in_grad_{j}")
            dxv, red = modmm_bwd(dproj, pl.BlockSpec((tm, wb_t), lambda i, n: (i, n)), wg["wb_in"],
                                 pl.BlockSpec((d, wb_t), lambda i, n: (0, n)), nb_t, s1["x"], dxv, mv, (l, 1),
                                 f"fox_in_bwd_{j}",
                                 more=(dfl, pl.BlockSpec((tm, LANES), lambda i, n: (i, 0)),
                                       pl.BlockSpec((d, LANES), lambda i, n: (0, gate_blk))))
            dqk_gain[("b", j)] = (dgain[0, :d].reshape(hq, HEAD_DIM).sum(0) * HEAD_DIM ** -0.5,
                                  dgain[0, d:2 * d].reshape(hq, HEAD_DIM).sum(0))
        dmod[l][1] = (red[1], red[2], dgate[0])
        dnorm[l][1] = red[0]
        dxv = ffn_backward(dxv, s0, l, 0, 0)
        if l % 2 == 0 and j >= 1:
            slabs = group_slabs(gg)
            lands = [lax.empty((N_CHIP - 1,) + tuple(shp), BF16) for shp in shard_shapes]
            scatter_in_flight[j] = exchange_start(slabs, lands, kinds, False, f"scatter_start_{j}")
            mv = mv + scatter_in_flight[j][-1][0, 0]
    grad_x = dxv[None]

    drel = rel_bias_grad(jnp.stack(dbias_tabs), bucket)
    dmod_flat = jnp.stack([jnp.stack([jnp.stack(dmod[l][sub]) for sub in range(3)]) for l in range(n_layers)]).reshape(-1)
    dnorm_flat = jnp.stack([jnp.stack(dnorm[l]) for l in range(n_layers)]).reshape(-1)
    pieces = [dmod_flat, dnorm_flat,
              jnp.stack([dqk_gain[("a", j)][0] for j in range(n_a)]).reshape(-1),
              jnp.stack([dqk_gain[("a", j)][1] for j in range(n_a)]).reshape(-1),
              jnp.stack([dqk_gain[("b", j)][0] for j in range(n_b)]).reshape(-1),
              jnp.stack([dqk_gain[("b", j)][1] for j in range(n_b)]).reshape(-1),
              jnp.stack([dsink[j] for j in range(n_a)]).reshape(-1),
              jnp.stack([db_f[j] for j in range(n_b)]).reshape(-1),
              drel.reshape(-1)]
    rows = [_pad_rows(p) for p in pieces]
    starts = np.cumsum([0] + [r.shape[0] for r in rows])
    total = -(-int(starts[-1]) // 8) * 8
    small = jnp.pad(jnp.concatenate(rows), ((0, total - int(starts[-1])), (0, 0)))
    small_all = all_gather_rows(small, "gather_small_grads").reshape(N_DEV, total, LANES)
    small_sum = sum_devices(small_all)

    def piece(k, shape):
        n = int(np.prod(shape))
        return small_sum[int(starts[k]):int(starts[k + 1])].reshape(-1)[:n].reshape(shape)

    g_ada_b = piece(0, (n_layers, 9 * d))
    g_norm = lax.dynamic_slice_in_dim(piece(1, (n_layers, 3, d)), chip * cq, cq, axis=2)
    g_swa_q, g_swa_k = piece(2, (n_a, HEAD_DIM)), piece(3, (n_a, HEAD_DIM))
    g_fox_q, g_fox_k = piece(4, (n_b, HEAD_DIM)), piece(5, (n_b, HEAD_DIM))
    g_sink, g_bf, g_rel = piece(6, (n_a, hq)), piece(7, (n_b, hq)), piece(8, (REL_BUCKETS, hq))

    dmod_all = small_all[:, :int(starts[1])].reshape(N_DEV, -1)[:, :n_layers * 9 * d].reshape(N_DEV, n_layers, 9 * d)
    ada_cols = ada_w.shape[-1]
    dmod_mine = lax.dynamic_slice_in_dim(jnp.moveaxis(dmod_all, 0, 1), chip * ada_cols, ada_cols, axis=2)
    g_ada_w = ada_grad(c_all.T, dmod_mine)

    sources = {0: group_slabs(grads[0])}
    landed = {0: scatter_grads(sources[0], kinds, shard_shapes)}
    for g, (send_sems, recv_sems, slabs, lands, _) in scatter_in_flight.items():
        sources[g], landed[g] = exchange_wait(send_sems, recv_sems, slabs, lands, landed[0][0], kinds, False,
                                              f"scatter_wait_{g}")

    def group_sum(g, t):
        cols = shard_shapes[t][-1]
        own = own_slab(sources[g][t], kinds[t], chip, shard_shapes[t])
        return sum_slots(landed[g][t].reshape(N_CHIP - 1, -1, cols), own.reshape(-1, cols))

    parts = [jnp.concatenate([group_sum(g, t) for g in range(n_groups)]) for t in range(len(kinds))]
    others = swap_with_sibling(parts)

    def update(w, m, v, g, g2=None):
        w2d = w.reshape(-1, w.shape[-1])
        outs = adamw(w2d, m.reshape(w2d.shape), v.reshape(w2d.shape), g.reshape(w2d.shape) if g2 is None else g, g2)
        return [t.reshape(w.shape) for t in outs]

    big = [(ffn_w13, m_ffn_w13, v_ffn_w13), (ffn_w2, m_ffn_w2, v_ffn_w2), (swa_w_in, m_swa_w_in, v_swa_w_in),
           (swa_w_out, m_swa_w_out, v_swa_w_out), (fox_w_in, m_fox_w_in, v_fox_w_in), (fox_w_out, m_fox_w_out, v_fox_w_out)]
    big_out = [update(w, m, v, p, q) for (w, m, v), p, q in zip(big, parts, others)]
    r_ada_w = update(ada_w, m_ada_w, v_ada_w, g_ada_w)
    r_ada_b = update(ada_b, m_ada_b, v_ada_b, g_ada_b)
    r_norm = update(norm_g, m_norm_g, v_norm_g, g_norm)
    r_rel = update(rel_bias, m_rel_bias, v_rel_bias, g_rel)
    r_swa_q = update(swa_q_g, m_swa_q_g, v_swa_q_g, g_swa_q)
    r_swa_k = update(swa_k_g, m_swa_k_g, v_swa_k_g, g_swa_k)
    r_sink = update(swa_sink, m_swa_sink, v_swa_sink, g_sink)
    r_bf = update(fox_b_f, m_fox_b_f, v_fox_b_f, g_bf)
    r_fox_q = update(fox_q_g, m_fox_q_g, v_fox_q_g, g_fox_q)
    r_fox_k = update(fox_k_g, m_fox_k_g, v_fox_k_g, g_fox_k)
    per_weight = [r_ada_w, r_ada_b, r_norm, big_out[0], big_out[1], r_rel, big_out[2], big_out[3], r_swa_q, r_swa_k,
                  r_sink, big_out[4], big_out[5], r_bf, r_fox_q, r_fox_k]
    return (loss, grad_x, *[r[0] for r in per_weight], *[r[1] for r in per_weight],
            *[r[2] for r in per_weight], *[r[3] for r in per_weight])
```

```python
import math

import numpy as np
import jax
import jax.numpy as jnp
from jax import lax
from jax.experimental import pallas as pl
from jax.experimental.pallas import tpu as pltpu

F32 = jnp.float32
BF16 = jnp.bfloat16
HEAD_DIM = 64
GROUP = 4
FOX_HEADS = 8
BLOCK = 128
REL_BUCKETS = 32
REL_MAX_DIST = 128
EPS = 1e-6
NEG = -1e30
N_CHIP = 4
N_DEV = 8
LANES = 128
VMEM_LIMIT = 52 * 1024 * 1024
ADAM_LR, ADAM_B1, ADAM_B2, ADAM_EPS, ADAM_WD, ADAM_STEP = 0.001, 0.9, 0.999, 1e-08, 0.01, 10
MESH = pl.DeviceIdType.MESH
ANY = pl.BlockSpec(memory_space=pl.ANY)


def _params(n_axes):
    return pltpu.CompilerParams(dimension_semantics=("arbitrary",) * n_axes, vmem_limit_bytes=VMEM_LIMIT)


def _nn(a, b):
    return jnp.dot(a, b, preferred_element_type=F32)


def _nt(a, b):
    return lax.dot_general(a, b, (((1,), (1,)), ((), ())), preferred_element_type=F32)


def _tn(a, b):
    return lax.dot_general(a, b, (((0,), (0,)), ((), ())), preferred_element_type=F32)


def _sigmoid(z):
    return 1.0 / (1.0 + jnp.exp(-z))


def _sigmoid_fast(z):
    return pl.reciprocal(1.0 + jnp.exp(-z), approx=True)


def _row_tile(s):
    return 512 if s >= 2048 else s // 2


def _wide_tile(s):
    return 1024 if s >= 2048 else s // 2


def _attn_tile(s):
    return 512 if s >= 2048 else s // 4


def _position():
    x, y, c = lax.axis_index("x"), lax.axis_index("y"), lax.axis_index("c")
    chips = [(1 - x, y), (x, 1 - y), (1 - x, 1 - y)]
    return x, y, c, chips


def all_gather_rows(v, name):
    m_per, n = v.shape

    def body(x_ref, out_ref, send_sems, recv_sems, local_sem):
        x, y, c, chips = _position()
        me, sibling = (x, y, c), (x, y, 1 - c)

        def rows(px, py, pc):
            return out_ref.at[pl.ds((4 * px + 2 * py + pc) * m_per, m_per), :]

        def copy(k, block, to, src=None):
            return pltpu.make_async_remote_copy(
                src_ref=rows(*block) if src is None else src, dst_ref=rows(*block),
                send_sem=send_sems.at[k], recv_sem=recv_sems.at[k], device_id=to, device_id_type=MESH)

        mine = pltpu.make_async_copy(x_ref, rows(*me), local_sem)
        mine.start()
        first = [copy(0, me, sibling, src=x_ref)]
        first += [copy(1 + j, me, (*chip, c), src=x_ref) for j, chip in enumerate(chips)]
        for cp in first:
            cp.start()
        passed = [copy(4 + j, (*chip, c), sibling) for j, chip in enumerate(chips)]
        for j, chip in enumerate(chips):
            copy(1 + j, (*chip, c), me).wait_recv()
            passed[j].start()
        copy(0, sibling, me).wait_recv()
        for j, chip in enumerate(chips):
            copy(4 + j, (*chip, 1 - c), me).wait_recv()
        for cp in first + passed:
            cp.wait_send()
        mine.wait()

    return pl.pallas_call(
        body, name=name,
        out_shape=jax.ShapeDtypeStruct((N_DEV * m_per, n), v.dtype),
        in_specs=[pl.BlockSpec(memory_space=pltpu.VMEM)],
        out_specs=pl.BlockSpec(memory_space=pltpu.VMEM),
        scratch_shapes=[pltpu.SemaphoreType.DMA((7,)), pltpu.SemaphoreType.DMA((7,)), pltpu.SemaphoreType.DMA],
    )(v)


def _slab(full_ref, kind, b, lead):
    how = kind[0]
    if how == "slot":
        return full_ref.at[b, lead]
    if how == "col":
        w = kind[1]
        idx = (lead,) + (slice(None),) * (len(full_ref.shape) - 2) + (pl.ds(pl.multiple_of(b * w, LANES), w),)
        return full_ref.at[idx]
    h = kind[1]
    idx = (lead,) + (slice(None),) * (len(full_ref.shape) - 3) + (pl.ds(pl.multiple_of(b * h, 8), h), slice(None))
    return full_ref.at[idx]


def _full_shape(shard_shape, kind):
    if kind[0] == "slot":
        return (N_CHIP,) + tuple(shard_shape)
    if kind[0] == "col":
        return tuple(shard_shape[:-1]) + (N_CHIP * shard_shape[-1],)
    return tuple(shard_shape[:-2]) + (N_CHIP * shard_shape[-2], shard_shape[-1])


def _slab_start(shape, kind, b):
    zeros = [0] * len(shape)
    if kind[0] == "slot":
        return [b] + zeros[1:]
    if kind[0] == "col":
        return zeros[:-1] + [b * kind[1]]
    return zeros[:-2] + [b * kind[1], 0]


def cast_and_place(shard, kind, chip, after, name):
    rows, cols = shard.shape[-2:]
    lead = int(np.prod(shard.shape[:-2]))
    if kind[0] == "col":
        full3, where = (lead, rows, N_CHIP * cols), lambda p, b: (p, 0, b[0])
    elif kind[0] == "row":
        full3, where = (lead, N_CHIP * rows, cols), lambda p, b: (p, b[0], 0)
    else:
        full3, where = (N_CHIP * lead, rows, cols), lambda p, b: (b[0] * lead + p, 0, 0)

    def body(b_ref, s_ref, after_ref, o_ref):
        o_ref[...] = s_ref[...].astype(BF16)

    full = pl.pallas_call(
        body, name=name,
        grid_spec=pltpu.PrefetchScalarGridSpec(
            num_scalar_prefetch=1, grid=(lead,),
            in_specs=[pl.BlockSpec((1, rows, cols), lambda p, b: (p, 0, 0)), ANY],
            out_specs=pl.BlockSpec((1, rows, cols), where)),
        out_shape=jax.ShapeDtypeStruct(full3, BF16),
        compiler_params=_params(1),
    )(jnp.reshape(chip, (1,)).astype(jnp.int32), shard.reshape(lead, rows, cols), after)
    return full.reshape(_full_shape(shard.shape, kind))


def own_slab(full, kind, b, shard_shape):
    sizes = (1,) + tuple(shard_shape) if kind[0] == "slot" else tuple(shard_shape)
    return lax.dynamic_slice(full, _slab_start(full.shape, kind, b), sizes).reshape(shard_shape)


def gather_weights(fulls, kinds):
    n = len(fulls)

    def body(*refs):
        outs = refs[n:2 * n]
        send_sems, recv_sems = refs[2 * n:]
        x, y, c, chips = _position()
        b_me = 2 * x + y
        sibling = (x, y, 1 - c)
        sends = []

        def halves(t):
            lead = outs[t].shape[1] if kinds[t][0] == "slot" else outs[t].shape[0]
            return pl.ds(c * (lead // 2), lead // 2), pl.ds((1 - c) * (lead // 2), lead // 2)

        for t in range(n):
            mine, _ = halves(t)
            own = _slab(outs[t], kinds[t], b_me, mine)
            for j, chip in enumerate(chips):
                cp = pltpu.make_async_remote_copy(
                    src_ref=own, dst_ref=own,
                    send_sem=send_sems.at[6 * t + j], recv_sem=recv_sems.at[6 * t + j],
                    device_id=(*chip, c), device_id_type=MESH)
                cp.start()
                sends.append(cp)
        for t in range(n):
            mine, _ = halves(t)
            for j, chip in enumerate(chips):
                landed = _slab(outs[t], kinds[t], 2 * chip[0] + chip[1], mine)
                pltpu.make_async_remote_copy(
                    src_ref=landed, dst_ref=landed, send_sem=send_sems.at[6 * t + j], recv_sem=recv_sems.at[6 * t + j],
                    device_id=(*chip, c), device_id_type=MESH).wait_recv()
                cp = pltpu.make_async_remote_copy(
                    src_ref=landed, dst_ref=landed, send_sem=send_sems.at[6 * t + 3 + j],
                    recv_sem=recv_sems.at[6 * t + 3 + j], device_id=sibling, device_id_type=MESH)
                cp.start()
                sends.append(cp)
        for t in range(n):
            _, theirs = halves(t)
            for j, chip in enumerate(chips):
                landed = _slab(outs[t], kinds[t], 2 * chip[0] + chip[1], theirs)
                pltpu.make_async_remote_copy(
                    src_ref=landed, dst_ref=landed, send_sem=send_sems.at[6 * t + 3 + j],
                    recv_sem=recv_sems.at[6 * t + 3 + j], device_id=sibling, device_id_type=MESH).wait_recv()
        for cp in sends:
            cp.wait_send()

    return pl.pallas_call(
        body, name="gather_weights",
        out_shape=[jax.ShapeDtypeStruct(v.shape, v.dtype) for v in fulls],
        in_specs=[ANY] * n, out_specs=[ANY] * n,
        input_output_aliases={t: t for t in range(n)},
        scratch_shapes=[pltpu.SemaphoreType.DMA((6 * n,)), pltpu.SemaphoreType.DMA((6 * n,))],
    )(*fulls)


def scatter_grads(grads, kinds, shard_shapes):
    n = len(grads)

    def body(*refs):
        ins, outs = refs[:n], refs[n:2 * n]
        send_sems, recv_sems = refs[2 * n:]
        x, y, c, chips = _position()
        sends = []
        for t in range(n):
            whole = _whole(ins[t], kinds[t])
            for j, chip in enumerate(chips):
                cp = pltpu.make_async_remote_copy(
                    src_ref=_slab(ins[t], kinds[t], 2 * chip[0] + chip[1], whole), dst_ref=outs[t].at[j],
                    send_sem=send_sems.at[3 * t + j], recv_sem=recv_sems.at[3 * t + j],
                    device_id=(*chip, c), device_id_type=MESH)
                cp.start()
                sends.append(cp)
        for t in range(n):
            for j, chip in enumerate(chips):
                pltpu.make_async_remote_copy(
                    src_ref=outs[t].at[j], dst_ref=outs[t].at[j], send_sem=send_sems.at[3 * t + j],
                    recv_sem=recv_sems.at[3 * t + j], device_id=(*chip, c), device_id_type=MESH).wait_recv()
        for cp in sends:
            cp.wait_send()

    return pl.pallas_call(
        body, name="scatter_grads",
        out_shape=[jax.ShapeDtypeStruct((N_CHIP - 1,) + tuple(s), g.dtype) for g, s in zip(grads, shard_shapes)],
        in_specs=[ANY] * n, out_specs=[ANY] * n,
        scratch_shapes=[pltpu.SemaphoreType.DMA((3 * n,)), pltpu.SemaphoreType.DMA((3 * n,))],
    )(*grads)


def _whole(ref, kind):
    return pl.ds(0, ref.shape[1] if kind[0] == "slot" else ref.shape[0])


def _exchange_copies(srcs, lands, kinds, gather, send_sems, recv_sems):
    x, y, c, chips = _position()
    b_me = 2 * x + y
    out = []
    for t in range(len(lands)):
        for j, chip in enumerate(chips):
            b_j = 2 * chip[0] + chip[1]
            if gather:
                src = sent_to = _slab(lands[t], kinds[t], b_me, _whole(lands[t], kinds[t]))
                arrives = _slab(lands[t], kinds[t], b_j, _whole(lands[t], kinds[t]))
            else:
                src = _slab(srcs[t], kinds[t], b_j, _whole(srcs[t], kinds[t]))
                sent_to = arrives = lands[t].at[j]
            k = 3 * t + j
            send = pltpu.make_async_remote_copy(src_ref=src, dst_ref=sent_to, send_sem=send_sems.at[k],
                                                recv_sem=recv_sems.at[k], device_id=(*chip, c), device_id_type=MESH)
            recv = pltpu.make_async_remote_copy(src_ref=src, dst_ref=arrives, send_sem=send_sems.at[k],
                                                recv_sem=recv_sems.at[k], device_id=(*chip, c), device_id_type=MESH)
            out.append((send, recv))
    return out


def exchange_start(srcs, lands, kinds, gather, name):
    ns, nl = len(srcs), len(lands)
    hbm = pl.BlockSpec(memory_space=pltpu.HBM)

    def body(*refs):
        ins, lnd = refs[:ns], refs[ns:ns + nl]
        send_sems, recv_sems = refs[ns + nl], refs[ns + nl + 1]
        token = refs[-1]
        for send, _ in _exchange_copies(ins, lnd, kinds, gather, send_sems, recv_sems):
            send.start()
        token[...] = jnp.zeros_like(token)

    ops = [pltpu.with_memory_space_constraint(v, pltpu.HBM) for v in (*srcs, *lands)]
    res = pl.pallas_call(
        body, name=name,
        out_shape=(pltpu.SemaphoreType.DMA((3 * nl,)), pltpu.SemaphoreType.DMA((3 * nl,)),
                   *[pltpu.HBM(v.shape, v.dtype) for v in ops], jax.ShapeDtypeStruct((8, LANES), F32)),
        in_specs=[hbm] * (ns + nl),
        out_specs=(pl.BlockSpec(memory_space=pltpu.SEMAPHORE), pl.BlockSpec(memory_space=pltpu.SEMAPHORE),
                   *[hbm] * (ns + nl), pl.BlockSpec(memory_space=pltpu.VMEM)),
        input_output_aliases={t: 2 + t for t in range(ns + nl)},
        compiler_params=pltpu.CompilerParams(has_side_effects=pltpu.SideEffectType.DATAFLOW_SIDE_EFFECTING),
    )(*ops)
    return res[0], res[1], list(res[2:2 + ns]), list(res[2 + ns:2 + ns + nl]), res[-1]


def exchange_wait(send_sems, recv_sems, srcs, lands, after, kinds, gather, name):
    ns, nl = len(srcs), len(lands)
    hbm = pl.BlockSpec(memory_space=pltpu.HBM)

    def body(*refs):
        ins, lnd = refs[:ns], refs[ns:ns + nl]
        ssem, rsem = refs[ns + nl], refs[ns + nl + 1]
        for send, recv in _exchange_copies(ins, lnd, kinds, gather, ssem, rsem):
            send.wait_send()
            recv.wait_recv()

    res = pl.pallas_call(
        body, name=name,
        out_shape=tuple(pltpu.HBM(v.shape, v.dtype) for v in (*srcs, *lands)),
        in_specs=[hbm] * (ns + nl) + [pl.BlockSpec(memory_space=pltpu.SEMAPHORE)] * 2 + [ANY],
        out_specs=tuple([hbm] * (ns + nl)),
        input_output_aliases={t: t for t in range(ns + nl)},
        compiler_params=pltpu.CompilerParams(has_side_effects=pltpu.SideEffectType.DATAFLOW_SIDE_EFFECTING),
    )(*srcs, *lands, send_sems, recv_sems, after)
    return list(res[:ns]), list(res[ns:])


def swap_with_sibling(parts):
    n = len(parts)

    def body(*refs):
        ins, outs = refs[:n], refs[n:2 * n]
        send_sems, recv_sems = refs[2 * n:]
        x, y, c, _ = _position()
        cps = []
        for t in range(n):
            cp = pltpu.make_async_remote_copy(
                src_ref=ins[t], dst_ref=outs[t], send_sem=send_sems.at[t], recv_sem=recv_sems.at[t],
                device_id=(x, y, 1 - c), device_id_type=MESH)
            cp.start()
            cps.append(cp)
        for cp in cps:
            cp.wait_recv()
        for cp in cps:
            cp.wait_send()

    return pl.pallas_call(
        body, name="swap_with_sibling",
        out_shape=[jax.ShapeDtypeStruct(p.shape, p.dtype) for p in parts],
        in_specs=[ANY] * n, out_specs=[ANY] * n,
        scratch_shapes=[pltpu.SemaphoreType.DMA((n,)), pltpu.SemaphoreType.DMA((n,))],
    )(*parts)


def _modulated(xv, mv_ref):
    g, shift, scale = mv_ref[0:1, :], mv_ref[1:2, :], mv_ref[2:3, :]
    r = lax.rsqrt(jnp.mean(xv * xv, axis=-1, keepdims=True) + EPS)
    xhat = xv * r
    xn = xhat * g
    return xn * (1.0 + scale) + shift, xhat, xn, r, g, scale


def modmm(x, mv, mv_idx, w, w_spec, n_tiles, out_shape, out_spec, name, want_h):
    s, d = x.shape
    tm = _wide_tile(s)

    def body(x_ref, mv_ref, w_ref, *rest):
        if want_h:
            out_ref, h_ref, h_scr = rest
        else:
            out_ref, h_scr = rest

        @pl.when(pl.program_id(1) == 0)
        def _():
            h = _modulated(x_ref[...], mv_ref)[0].astype(BF16)
            h_scr[...] = h
            if want_h:
                h_ref[...] = h

        out_ref[...] = _nn(h_scr[...], w_ref[...]).astype(out_ref.dtype)

    out_shapes = [out_shape]
    out_specs = [out_spec]
    if want_h:
        out_shapes.append(jax.ShapeDtypeStruct((s, d), BF16))
        out_specs.append(pl.BlockSpec((tm, d), lambda i, n: (i, 0)))
    res = pl.pallas_call(
        body, name=name, grid=(s // tm, n_tiles),
        in_specs=[pl.BlockSpec((tm, d), lambda i, n: (i, 0)),
                  pl.BlockSpec((None, None, 8, d), lambda i, n: (*mv_idx, 0, 0)), w_spec],
        out_specs=out_specs, out_shape=out_shapes,
        scratch_shapes=[pltpu.VMEM((tm, d), BF16)],
        compiler_params=_params(2),
    )(x, mv, w)
    return res if want_h else (res[0], None)


def resmm(x, mv, mv_idx, coef, lhs, lhs_spec, w, w_spec, k_tiles, tk, name, ffn):
    s, d = x.shape
    tm = _row_tile(s)
    kdim = k_tiles * tk

    def body(x_ref, mv_ref, lhs_ref, w_ref, xo_ref, y_ref, *rest):
        if ffn:
            u_ref, acc = rest
        else:
            (acc,) = rest
        k = pl.program_id(1)

        @pl.when(k == 0)
        def _():
            acc[...] = jnp.zeros_like(acc)

        if ffn:
            ag = lhs_ref[0].astype(F32)
            au = lhs_ref[1].astype(F32)
            left = (ag * _sigmoid_fast(ag) * au).astype(BF16)
            u_ref[...] = left
        else:
            left = lhs_ref[...]
        acc[...] += _nn(left, w_ref[...])

        @pl.when(k == k_tiles - 1)
        def _():
            y = acc[...]
            y_ref[...] = y.astype(BF16)
            xo_ref[...] = x_ref[...] + (coef * mv_ref[3:4, :]) * y

    row = pl.BlockSpec((tm, d), lambda i, k: (i, 0))
    out_shapes = [jax.ShapeDtypeStruct((s, d), F32), jax.ShapeDtypeStruct((s, d), BF16)]
    out_specs = [row, row]
    if ffn:
        out_shapes.append(jax.ShapeDtypeStruct((s, kdim), BF16))
        out_specs.append(pl.BlockSpec((tm, tk), lambda i, k: (i, k)))
    return pl.pallas_call(
        body, name=name, grid=(s // tm, k_tiles),
        in_specs=[row, pl.BlockSpec((None, None, 8, d), lambda i, k: (*mv_idx, 0, 0)), lhs_spec, w_spec],
        out_specs=out_specs, out_shape=out_shapes,
        scratch_shapes=[pltpu.VMEM((tm, d), F32)],
        compiler_params=_params(2),
    )(x, mv, lhs, w)


def resmm_bwd(dxo, y, mv, mv_idx, coef, w, w_spec, k_tiles, tk, name, a=None):
    s, d = dxo.shape
    tm = _row_tile(s)
    kdim = k_tiles * tk
    ffn = a is not None

    def body(dxo_ref, y_ref, mv_ref, w_ref, *rest):
        if ffn:
            a_ref, dy_ref, dl_ref, dgate_ref, dy_scr = rest
        else:
            dy_ref, dl_ref, dgate_ref, dy_scr = rest
        i, k = pl.program_id(0), pl.program_id(1)

        @pl.when((i == 0) & (k == 0))
        def _():
            dgate_ref[...] = jnp.zeros_like(dgate_ref)

        @pl.when(k == 0)
        def _():
            dxv = dxo_ref[...]
            dy = ((coef * mv_ref[3:4, :]) * dxv).astype(BF16)
            dy_scr[...] = dy
            dy_ref[...] = dy
            dgate_ref[0:1, :] += jnp.sum(coef * dxv * y_ref[...].astype(F32), axis=0, keepdims=True)

        dl = _nt(dy_scr[...], w_ref[...])
        if ffn:
            ag = a_ref[0].astype(F32)
            au = a_ref[1].astype(F32)
            sg = _sigmoid_fast(ag)
            dl_ref[0] =(dl * au * (sg * (1.0 + ag * (1.0 - sg)))).astype(BF16)
            dl_ref[1] = (dl * (ag * sg)).astype(BF16)
        else:
            dl_ref[...] = dl.astype(BF16)

    row = pl.BlockSpec((tm, d), lambda i, k: (i, 0))
    in_specs = [row, row, pl.BlockSpec((None, None, 8, d), lambda i, k: (*mv_idx, 0, 0)), w_spec]
    ops = [dxo, y, mv, w]
    if ffn:
        in_specs.append(pl.BlockSpec((2, tm, tk), lambda i, k: (0, i, k)))
        ops.append(a)
        dl_shape = jax.ShapeDtypeStruct((2, s, kdim), BF16)
        dl_spec = pl.BlockSpec((2, tm, tk), lambda i, k: (0, i, k))
    else:
        dl_shape = jax.ShapeDtypeStruct((s, kdim), BF16)
        dl_spec = pl.BlockSpec((tm, tk), lambda i, k: (i, k))
    return pl.pallas_call(
        body, name=name, grid=(s // tm, k_tiles),
        in_specs=in_specs,
        out_specs=[row, dl_spec, pl.BlockSpec((8, d), lambda i, k: (0, 0))],
        out_shape=[jax.ShapeDtypeStruct((s, d), BF16), dl_shape, jax.ShapeDtypeStruct((8, d), F32)],
        scratch_shapes=[pltpu.VMEM((tm, d), BF16)],
        compiler_params=_params(2),
    )(*ops)


def modmm_bwd(dl, dl_spec, w, w_spec, n_tiles, x, dxo, mv, mv_idx, name, more=None):
    s, d = x.shape
    tm = _row_tile(s)

    def body(dl_ref, w_ref, x_ref, dxo_ref, mv_ref, *rest):
        if more is not None:
            dl2_ref, w2_ref, dx_ref, red_ref, acc = rest
        else:
            dx_ref, red_ref, acc = rest
        i, n = pl.program_id(0), pl.program_id(1)

        @pl.when((i == 0) & (n == 0))
        def _():
            red_ref[...] = jnp.zeros_like(red_ref)

        @pl.when(n == 0)
        def _():
            if more is not None:
                acc[...] = _nt(dl2_ref[...], w2_ref[...])
            else:
                acc[...] = jnp.zeros_like(acc)

        acc[...] += _nt(dl_ref[...], w_ref[...])

        @pl.when(n == n_tiles - 1)
        def _():
            dh = acc[...]
            _, xhat, xn, r, g, scale = _modulated(x_ref[...], mv_ref)
            dxn = dh * (1.0 + scale)
            red_ref[0:1, :] += jnp.sum(dxn * xhat, axis=0, keepdims=True)
            red_ref[1:2, :] += jnp.sum(dh, axis=0, keepdims=True)
            red_ref[2:3, :] += jnp.sum(dh * xn, axis=0, keepdims=True)
            gd = dxn * g
            dx_ref[...] = dxo_ref[...] + r * (gd - xhat * jnp.mean(gd * xhat, axis=-1, keepdims=True))

    row = pl.BlockSpec((tm, d), lambda i, n: (i, 0))
    in_specs = [dl_spec, w_spec, row, row, pl.BlockSpec((None, None, 8, d), lambda i, n: (*mv_idx, 0, 0))]
    ops = [dl, w, x, dxo, mv]
    if more is not None:
        in_specs += [more[1], more[2]]
        ops += [more[0], w]
    return pl.pallas_call(
        body, name=name, grid=(s // tm, n_tiles),
        in_specs=in_specs,
        out_specs=[row, pl.BlockSpec((8, d), lambda i, n: (0, 0))],
        out_shape=[jax.ShapeDtypeStruct((s, d), F32), jax.ShapeDtypeStruct((8, d), F32)],
        scratch_shapes=[pltpu.VMEM((tm, d), F32)],
        compiler_params=_params(2),
    )(*ops)


def weight_grad(a, a_spec, b, b_spec, grid_mn, s, bm, bn, dest, out_spec, name):
    tk = _wide_tile(s)
    k_tiles = s // tk

    def body(a_ref, b_ref, dest_ref, out_ref, acc):
        k = pl.program_id(2)

        @pl.when(k == 0)
        def _():
            acc[...] = jnp.zeros_like(acc)

        acc[...] += _tn(a_ref[...], b_ref[...])

        @pl.when(k == k_tiles - 1)
        def _():
            out_ref[...] = acc[...].astype(out_ref.dtype)

    return pl.pallas_call(
        body, name=name, grid=(*grid_mn, k_tiles),
        in_specs=[a_spec, b_spec, ANY], out_specs=out_spec,
        out_shape=jax.ShapeDtypeStruct(dest.shape, dest.dtype),
        input_output_aliases={2: 0},
        scratch_shapes=[pltpu.VMEM((bm, bn), F32)],
        compiler_params=_params(3),
    )(a, b, dest)


def _head_mean(v):
    lane = lax.broadcasted_iota(jnp.int32, v.shape, 1)
    lo = jnp.sum(jnp.where(lane < HEAD_DIM, v, 0.0), axis=-1, keepdims=True)
    hi = jnp.sum(v, axis=-1, keepdims=True) - lo
    return jnp.where(lane < HEAD_DIM, lo, hi) * (1.0 / HEAD_DIM)


def qknorm_fwd(proj, gains, width, name):
    s = proj.shape[0]
    nqk = gains.shape[1]
    tm = _row_tile(s)

    def body(p_ref, g_ref, o_ref):
        for cc in range(width // LANES):
            sl = slice(cc * LANES, (cc + 1) * LANES)
            xv = p_ref[:, sl].astype(F32)
            r = lax.rsqrt(_head_mean(xv * xv) + EPS)
            o_ref[:, sl] = (xv * r * g_ref[:, sl]).astype(BF16)

    blk = pl.BlockSpec((tm, width), lambda i, c: (i, c))
    return pl.pallas_call(
        body, name=name, grid=(s // tm, nqk // width),
        in_specs=[blk, pl.BlockSpec((1, width), lambda i, c: (0, c))],
        out_specs=blk, out_shape=jax.ShapeDtypeStruct((s, nqk), BF16),
        compiler_params=_params(2),
    )(proj, gains)


def qknorm_bwd(proj, gains, d, d_spec, n_cols, width, name):
    s = proj.shape[0]
    nqk = gains.shape[1] // width
    n_blocks = n_cols // width
    tm = _row_tile(s)

    def body(p_ref, g_ref, d_ref, o_ref, dg_ref):
        c, i = pl.program_id(0), pl.program_id(1)

        @pl.when(i == 0)
        def _():
            dg_ref[...] = jnp.zeros_like(dg_ref)

        @pl.when(c < nqk)
        def _():
            for cc in range(width // LANES):
                sl = slice(cc * LANES, (cc + 1) * LANES)
                xv = p_ref[:, sl].astype(F32)
                r = lax.rsqrt(_head_mean(xv * xv) + EPS)
                xhat = xv * r
                dv = d_ref[:, sl]
                gd = dv * g_ref[:, sl]
                o_ref[:, sl] = (r * (gd - xhat * _head_mean(gd * xhat))).astype(BF16)
                dg_ref[0:1, sl] += jnp.sum(dv * xhat, axis=0, keepdims=True)

        @pl.when(c >= nqk)
        def _():
            o_ref[...] = d_ref[...].astype(BF16)

    return pl.pallas_call(
        body, name=name, grid=(n_blocks, s // tm),
        in_specs=[pl.BlockSpec((tm, width), lambda c, i: (i, c)),
                  pl.BlockSpec((1, width), lambda c, i: (0, jnp.minimum(c, nqk - 1))), d_spec],
        out_specs=[pl.BlockSpec((tm, width), lambda c, i: (i, c)), pl.BlockSpec((8, width), lambda c, i: (0, c))],
        out_shape=[jax.ShapeDtypeStruct((s, n_cols), BF16), jax.ShapeDtypeStruct((8, n_cols), F32)],
        compiler_params=_params(2),
    )(proj, gains, d)


def _swa_mask(first):
    qi = lax.broadcasted_iota(jnp.int32, (BLOCK, 2 * BLOCK), 0) + BLOCK
    kj = lax.broadcasted_iota(jnp.int32, (BLOCK, 2 * BLOCK), 1)
    dist = qi - kj
    return (dist >= 0) & (dist < BLOCK) & ((kj >= BLOCK) | jnp.logical_not(first))


def swa_fwd(qkn, proj, bias, sink, d, name):
    s = qkn.shape[0]
    hq = d // HEAD_DIM
    hkv = hq // GROUP
    kw = hkv * HEAD_DIM
    nblk = s // BLOCK
    kcol = d // kw

    def body(q_ref, kc_ref, kp_ref, vc_ref, vp_ref, bias_ref, sink_ref, o_ref, lse_ref):
        mask = _swa_mask(pl.program_id(0) == 0)
        lse_ref[...] = jnp.zeros_like(lse_ref)
        for kvh in range(hkv):
            cols = slice(kvh * HEAD_DIM, (kvh + 1) * HEAD_DIM)
            k2 = jnp.concatenate([kp_ref[:, cols], kc_ref[:, cols]], axis=0)
            v2 = jnp.concatenate([vp_ref[:, cols], vc_ref[:, cols]], axis=0)
            for g in range(GROUP):
                h = kvh * GROUP + g
                hc = slice(h * HEAD_DIM, (h + 1) * HEAD_DIM)
                sc = jnp.where(mask, _nt(q_ref[:, hc], k2) + bias_ref[h], NEG)
                sk = sink_ref[0, h]
                m = jnp.maximum(jnp.max(sc, axis=-1, keepdims=True), sk)
                p = jnp.exp(sc - m)
                denom = jnp.sum(p, axis=-1, keepdims=True) + jnp.exp(sk - m)
                o_ref[:, hc] = (_nn(p.astype(BF16), v2) / denom).astype(BF16)
                lse_ref[:, h:h + 1] = m + jnp.log(denom)

    prev = lambda i: jnp.maximum(i - 1, 0)
    return pl.pallas_call(
        body, name=name, grid=(nblk,),
        in_specs=[pl.BlockSpec((BLOCK, d), lambda i: (i, 0)),
                  pl.BlockSpec((BLOCK, kw), lambda i: (i, kcol)),
                  pl.BlockSpec((BLOCK, kw), lambda i: (prev(i), kcol)),
                  pl.BlockSpec((BLOCK, kw), lambda i: (i, kcol + 1)),
                  pl.BlockSpec((BLOCK, kw), lambda i: (prev(i), kcol + 1)),
                  pl.BlockSpec((hq, BLOCK, 2 * BLOCK), lambda i: (0, 0, 0)),
                  pl.BlockSpec(memory_space=pltpu.SMEM)],
        out_specs=[pl.BlockSpec((BLOCK, d), lambda i: (i, 0)), pl.BlockSpec((BLOCK, LANES), lambda i: (i, 0))],
        out_shape=[jax.ShapeDtypeStruct((s, d), BF16), jax.ShapeDtypeStruct((s, LANES), F32)],
        compiler_params=_params(1),
    )(qkn, qkn, qkn, proj, proj, bias, sink)


def swa_bwd(qkn, proj, bias, sink, do, o, lse, d, name):
    s = qkn.shape[0]
    hq = d // HEAD_DIM
    hkv = hq // GROUP
    kw = hkv * HEAD_DIM
    nblk = s // BLOCK
    kcol = d // kw
    wide = d + 2 * kw

    def body(q_ref, kc_ref, kp_ref, vc_ref, vp_ref, bias_ref, sink_ref, do_ref, o_ref, lse_ref,
             out_ref, dbias_ref, dsink_ref, carry, fresh):
        i = pl.program_id(0)

        @pl.when(i == 0)
        def _():
            dbias_ref[...] = jnp.zeros_like(dbias_ref)
            dsink_ref[...] = jnp.zeros_like(dsink_ref)
            carry[...] = jnp.zeros_like(carry)

        @pl.when(i == nblk)
        def _():
            fresh[...] = jnp.zeros_like(fresh)

        @pl.when(i < nblk)
        def _():
            mask = _swa_mask(i == 0)
            for kvh in range(hkv):
                cols = slice(kvh * HEAD_DIM, (kvh + 1) * HEAD_DIM)
                k2 = jnp.concatenate([kp_ref[:, cols], kc_ref[:, cols]], axis=0)
                v2 = jnp.concatenate([vp_ref[:, cols], vc_ref[:, cols]], axis=0)
                dk2 = jnp.zeros((2 * BLOCK, HEAD_DIM), F32)
                dv2 = jnp.zeros((2 * BLOCK, HEAD_DIM), F32)
                for g in range(GROUP):
                    h = kvh * GROUP + g
                    hc = slice(h * HEAD_DIM, (h + 1) * HEAD_DIM)
                    q = q_ref[:, hc]
                    dov = do_ref[:, hc]
                    lse_h = lse_ref[:, h:h + 1]
                    sc = jnp.where(mask, _nt(q, k2) + bias_ref[h], NEG)
                    p = jnp.exp(sc - lse_h)
                    delta = jnp.sum(dov.astype(F32) * o_ref[:, hc].astype(F32), axis=-1, keepdims=True)
                    ds = p * (_nt(dov, v2) - delta)
                    dbias_ref[h] += ds
                    dsink_ref[0:1, h:h + 1] += jnp.sum(-jnp.exp(sink_ref[0, h] - lse_h) * delta, axis=0, keepdims=True)
                    dsb = ds.astype(BF16)
                    fresh[0, :, hc] = _nn(dsb, k2)
                    dk2 += _tn(dsb, q)
                    dv2 += _tn(p.astype(BF16), dov)
                kc_cols = slice(d + kvh * HEAD_DIM, d + (kvh + 1) * HEAD_DIM)
                vc_cols = slice(d + kw + kvh * HEAD_DIM, d + kw + (kvh + 1) * HEAD_DIM)
                fresh[0, :, kc_cols] = dk2[BLOCK:]
                fresh[0, :, vc_cols] = dv2[BLOCK:]
                fresh[1, :, kc_cols] = dk2[:BLOCK]
                fresh[1, :, vc_cols] = dv2[:BLOCK]

        lane = lax.broadcasted_iota(jnp.int32, (BLOCK, wide), 1)
        out_ref[...] = carry[...] + jnp.where(lane >= d, fresh[1], 0.0)

        @pl.when(i < nblk)
        def _():
            carry[...] = fresh[0]

    cur = lambda i: jnp.minimum(i, nblk - 1)
    prev = lambda i: jnp.maximum(jnp.minimum(i, nblk - 1) - 1, 0)
    return pl.pallas_call(
        body, name=name, grid=(nblk + 1,),
        in_specs=[pl.BlockSpec((BLOCK, d), lambda i: (cur(i), 0)),
                  pl.BlockSpec((BLOCK, kw), lambda i: (cur(i), kcol)),
                  pl.BlockSpec((BLOCK, kw), lambda i: (prev(i), kcol)),
                  pl.BlockSpec((BLOCK, kw), lambda i: (cur(i), kcol + 1)),
                  pl.BlockSpec((BLOCK, kw), lambda i: (prev(i), kcol + 1)),
                  pl.BlockSpec((hq, BLOCK, 2 * BLOCK), lambda i: (0, 0, 0)),
                  pl.BlockSpec(memory_space=pltpu.SMEM),
                  pl.BlockSpec((BLOCK, d), lambda i: (cur(i), 0)),
                  pl.BlockSpec((BLOCK, d), lambda i: (cur(i), 0)),
                  pl.BlockSpec((BLOCK, LANES), lambda i: (cur(i), 0))],
        out_specs=[pl.BlockSpec((BLOCK, wide), lambda i: (jnp.maximum(i - 1, 0), 0)),
                   pl.BlockSpec((hq, BLOCK, 2 * BLOCK), lambda i: (0, 0, 0)),
                   pl.BlockSpec((8, LANES), lambda i: (0, 0))],
        out_shape=[jax.ShapeDtypeStruct((s, wide), F32), jax.ShapeDtypeStruct((hq, BLOCK, 2 * BLOCK), F32),
                   jax.ShapeDtypeStruct((8, LANES), F32)],
        scratch_shapes=[pltpu.VMEM((BLOCK, wide), F32), pltpu.VMEM((2, BLOCK, wide), F32)],
        compiler_params=_params(1),
    )(qkn, qkn, qkn, proj, proj, bias, sink, do, o, lse)


def _rel_bucket_table():
    qi = np.arange(BLOCK)[:, None] + BLOCK
    kj = np.arange(2 * BLOCK)[None, :]
    n = np.maximum(qi - kj, 0)
    max_exact = REL_BUCKETS // 2
    nf = np.maximum(n, 1).astype(np.float32)
    large = max_exact + (np.log(nf / max_exact) / math.log(REL_MAX_DIST / max_exact)
                         * (REL_BUCKETS - max_exact)).astype(np.int32)
    large = np.minimum(large, REL_BUCKETS - 1)
    return np.where(n < max_exact, n, large).astype(np.int32)


def rel_bias_table(rel_bias, bucket):
    hq = rel_bias.shape[1]

    def body(rb_ref, bucket_ref, out_ref):
        tbl = bucket_ref[...]

        def per_head(h, carry):
            def per_bucket(b, acc):
                return jnp.where(tbl == b, rb_ref[b, h], acc)

            out_ref[h] = lax.fori_loop(0, REL_BUCKETS, per_bucket, jnp.zeros(tbl.shape, F32))
            return carry

        lax.fori_loop(0, hq, per_head, 0)

    return pl.pallas_call(
        body, name="rel_bias_table",
        in_specs=[pl.BlockSpec(memory_space=pltpu.SMEM), pl.BlockSpec(memory_space=pltpu.VMEM)],
        out_specs=pl.BlockSpec(memory_space=pltpu.VMEM),
        out_shape=jax.ShapeDtypeStruct((hq,) + tuple(bucket.shape), F32),
    )(rel_bias, bucket)


def rel_bias_grad(dbias, bucket):
    n_layers, hq = dbias.shape[:2]

    def body(db_ref, bucket_ref, out_ref):
        tbl = bucket_ref[...]

        def per_head(h, carry):
            dsum = db_ref[0, h]
            for a in range(1, n_layers):
                dsum = dsum + db_ref[a, h]

            def per_bucket(b, carry2):
                out_ref[b, h] = jnp.sum(jnp.where(tbl == b, dsum, 0.0))
                return carry2

            return lax.fori_loop(0, REL_BUCKETS, per_bucket, carry)

        lax.fori_loop(0, hq, per_head, 0)

    return pl.pallas_call(
        body, name="rel_bias_grad",
        in_specs=[pl.BlockSpec(memory_space=pltpu.VMEM), pl.BlockSpec(memory_space=pltpu.VMEM)],
        out_specs=pl.BlockSpec(memory_space=pltpu.SMEM),
        out_shape=jax.ShapeDtypeStruct((REL_BUCKETS, hq), F32),
    )(dbias, bucket)


def _split3(v):
    hi = v.astype(BF16)
    r1 = v - hi.astype(F32)
    mid = r1.astype(BF16)
    lo = (r1 - mid.astype(F32)).astype(BF16)
    return hi, mid, lo


def _tri_sum(tri, v):
    hi, mid, lo = _split3(v)
    return _nn(tri, hi) + _nn(tri, mid) + _nn(tri, lo)


def fox_gates(fl, b_f, name):
    s = fl.shape[0]
    t = _row_tile(s)

    def body(fl_ref, b_ref, f_ref, carry):
        @pl.when(pl.program_id(0) == 0)
        def _():
            carry[...] = jnp.zeros_like(carry)

        z = fl_ref[...] + b_ref[...]
        logf = jnp.minimum(z, 0.0) - jnp.log(1.0 + jnp.exp(-jnp.abs(z)))
        r = lax.broadcasted_iota(jnp.int32, (t, t), 0)
        cidx = lax.broadcasted_iota(jnp.int32, (t, t), 1)
        tri = jnp.where(cidx <= r, 1.0, 0.0).astype(BF16)
        f = _tri_sum(tri, logf) + carry[0:1, :]
        f_ref[...] = f
        carry[0:1, :] = f_ref[t - 1:t, :]

    blk = pl.BlockSpec((t, LANES), lambda i: (i, 0))
    return pl.pallas_call(
        body, name=name, grid=(s // t,),
        in_specs=[blk, pl.BlockSpec((1, LANES), lambda i: (0, 0))],
        out_specs=blk, out_shape=jax.ShapeDtypeStruct((s, LANES), F32),
        scratch_shapes=[pltpu.VMEM((8, LANES), F32)],
        compiler_params=_params(1),
    )(fl, b_f)


def fox_gates_bwd(fl, b_f, df_query, df_key, name):
    s = fl.shape[0]
    t = _row_tile(s)
    nb = s // t

    def body(fl_ref, b_ref, dfq_ref, dfk_ref, dfl_ref, db_ref, carry):
        @pl.when(pl.program_id(0) == 0)
        def _():
            carry[...] = jnp.zeros_like(carry)
            db_ref[...] = jnp.zeros_like(db_ref)

        dfv = dfq_ref[...] + dfk_ref[...]
        r = lax.broadcasted_iota(jnp.int32, (t, t), 0)
        cidx = lax.broadcasted_iota(jnp.int32, (t, t), 1)
        tri = jnp.where(cidx >= r, 1.0, 0.0).astype(BF16)
        dlog = _tri_sum(tri, dfv) + carry[0:1, :]
        carry[0:1, :] += jnp.sum(dfv, axis=0, keepdims=True)
        z = fl_ref[...] + b_ref[...]
        dz = dlog * (1.0 - _sigmoid(z))
        dfl_ref[...] = dz.astype(BF16)
        db_ref[0:1, :] += jnp.sum(dz, axis=0, keepdims=True)

    rev = pl.BlockSpec((t, LANES), lambda i: (nb - 1 - i, 0))
    return pl.pallas_call(
        body, name=name, grid=(nb,),
        in_specs=[rev, pl.BlockSpec((1, LANES), lambda i: (0, 0)), rev, rev],
        out_specs=[rev, pl.BlockSpec((8, LANES), lambda i: (0, 0))],
        out_shape=[jax.ShapeDtypeStruct((s, LANES), BF16), jax.ShapeDtypeStruct((8, LANES), F32)],
        scratch_shapes=[pltpu.VMEM((8, LANES), F32)],
        compiler_params=_params(1),
    )(fl, b_f, df_query, df_key)


def fox_fwd(qkn, proj, f_col, f_row, d, name):
    s = qkn.shape[0]
    t = _attn_tile(s)
    hs, wide = FOX_HEADS, FOX_HEADS * HEAD_DIM
    n_pairs = d // wide
    nt = s // t

    def body(q_ref, k_ref, v_ref, fq_ref, fk_ref, o_ref, o32_ref, lse_ref, m_scr, l_scr, acc):
        i, j = pl.program_id(1), pl.program_id(2)

        @pl.when(j == 0)
        def _():
            m_scr[...] = jnp.full_like(m_scr, NEG)
            l_scr[...] = jnp.zeros_like(l_scr)
            acc[...] = jnp.zeros_like(acc)

        @pl.when(j <= i)
        def _():
            krow = lax.broadcasted_iota(jnp.int32, (t, t), 0)
            qcol = lax.broadcasted_iota(jnp.int32, (t, t), 1)
            visible = (krow <= qcol) | (j < i)
            for hh in range(hs):
                hc = slice(hh * HEAD_DIM, (hh + 1) * HEAD_DIM)
                st = _nt(k_ref[:, hc], q_ref[:, hc]) + fq_ref[hh] - fk_ref[hh]
                st = jnp.where(visible, st, NEG)
                m_prev = m_scr[hh]
                m_new = jnp.maximum(m_prev, jnp.max(st, axis=0, keepdims=True))
                alpha = jnp.exp(m_prev - m_new)
                pt = jnp.exp(st - m_new)
                l_scr[hh] = alpha * l_scr[hh] + jnp.sum(pt, axis=0, keepdims=True)
                acc[hc, :] = alpha * acc[hc, :] + _tn(v_ref[:, hc], pt.astype(BF16))
                m_scr[hh] = m_new

        @pl.when(j == i)
        def _():
            l_full = jnp.concatenate([jnp.broadcast_to(l_scr[hh], (HEAD_DIM, t)) for hh in range(hs)], axis=0)
            ov = (acc[...] / l_full).T
            o_ref[...] = ov.astype(BF16)
            o32_ref[...] = ov
            lse_ref[...] = m_scr[...] + jnp.log(l_scr[...])

    kv = lambda j, i: jnp.minimum(j, i)
    return pl.pallas_call(
        body, name=name, grid=(n_pairs, nt, nt),
        in_specs=[pl.BlockSpec((t, wide), lambda p, i, j: (i, p)),
                  pl.BlockSpec((t, wide), lambda p, i, j: (kv(j, i), n_pairs + p)),
                  pl.BlockSpec((t, wide), lambda p, i, j: (kv(j, i), 2 * n_pairs + p)),
                  pl.BlockSpec((hs, 1, t), lambda p, i, j: (p, 0, i)),
                  pl.BlockSpec((hs, t, 1), lambda p, i, j: (p, kv(j, i), 0))],
        out_specs=[pl.BlockSpec((t, wide), lambda p, i, j: (i, p)),
                   pl.BlockSpec((t, wide), lambda p, i, j: (i, p)),
                   pl.BlockSpec((hs, 1, t), lambda p, i, j: (p, 0, i))],
        out_shape=[jax.ShapeDtypeStruct((s, d), BF16), jax.ShapeDtypeStruct((s, d), F32),
                   jax.ShapeDtypeStruct((hs * n_pairs, 1, s), F32)],
        scratch_shapes=[pltpu.VMEM((hs, 1, t), F32), pltpu.VMEM((hs, 1, t), F32), pltpu.VMEM((wide, t), F32)],
        compiler_params=_params(3),
    )(qkn, qkn, proj, f_row, f_col)


def fox_bwd(qkn, proj, f_col, f_row, lse_row, do, o, d, name):
    s = qkn.shape[0]
    t = _attn_tile(s)
    hs, wide = FOX_HEADS, FOX_HEADS * HEAD_DIM
    n_pairs = d // wide
    nt = s // t

    def body(q_ref, k_ref, v_ref, fk_ref, fq_ref, lse_ref, do_ref, o_ref, out_ref, df_ref, dfq_ref,
             dq_acc, dkv_acc, df_acc, dfq_acc):
        j, i = pl.program_id(1), pl.program_id(2)

        @pl.when((j == 0) & (i == 0))
        def _():
            dq_acc[...] = jnp.zeros_like(dq_acc)
            dfq_acc[...] = jnp.zeros_like(dfq_acc)

        @pl.when(i == 0)
        def _():
            dkv_acc[...] = jnp.zeros_like(dkv_acc)
            df_acc[...] = jnp.zeros_like(df_acc)

        @pl.when(i >= j)
        def _():
            krow = lax.broadcasted_iota(jnp.int32, (t, t), 0)
            qcol = lax.broadcasted_iota(jnp.int32, (t, t), 1)
            visible = (krow <= qcol) | (i > j)
            ones = jnp.ones((8, HEAD_DIM), BF16)
            for hh in range(hs):
                hc = slice(hh * HEAD_DIM, (hh + 1) * HEAD_DIM)
                q, k, v, dov = q_ref[:, hc], k_ref[:, hc], v_ref[:, hc], do_ref[:, hc]
                st = _nt(k, q) + fq_ref[hh] - fk_ref[hh]
                pt = jnp.exp(jnp.where(visible, st, NEG) - lse_ref[hh])
                hi, mid, lo = _split3(dov.astype(F32) * o_ref[:, hc])
                delta = jnp.max(_nt(ones, hi) + _nt(ones, mid) + _nt(ones, lo), axis=0, keepdims=True)
                dst = pt * (_nt(v, dov) - delta)
                dsb = dst.astype(BF16)
                dkv_acc[1, :, hc] += _nn(pt.astype(BF16), dov)
                dkv_acc[0, :, hc] += _nn(dsb, q)
                dq_acc[pl.ds(pl.multiple_of(i * t, t), t), hc] += _tn(dsb, k)
                df_acc[hh] -= jnp.sum(dst, axis=-1, keepdims=True)
                dfq_acc[i, hh] += jnp.sum(dst, axis=0, keepdims=True)

        @pl.when(i == nt - 1)
        def _():
            out_ref[0] = dq_acc[pl.ds(pl.multiple_of(j * t, t), t), :]
            out_ref[1] = dkv_acc[0]
            out_ref[2] = dkv_acc[1]
            df_ref[...] = df_acc[...]
            dfq_ref[...] = dfq_acc[j]

    qi = lambda j, i: jnp.maximum(i, j)
    return pl.pallas_call(
        body, name=name, grid=(n_pairs, nt, nt),
        in_specs=[pl.BlockSpec((t, wide), lambda p, j, i: (qi(j, i), p)),
                  pl.BlockSpec((t, wide), lambda p, j, i: (j, n_pairs + p)),
                  pl.BlockSpec((t, wide), lambda p, j, i: (j, 2 * n_pairs + p)),
                  pl.BlockSpec((hs, t, 1), lambda p, j, i: (p, j, 0)),
                  pl.BlockSpec((hs, 1, t), lambda p, j, i: (p, 0, qi(j, i))),
                  pl.BlockSpec((hs, 1, t), lambda p, j, i: (p, 0, qi(j, i))),
                  pl.BlockSpec((t, wide), lambda p, j, i: (qi(j, i), p)),
                  pl.BlockSpec((t, wide), lambda p, j, i: (qi(j, i), p))],
        out_specs=[pl.BlockSpec((3, t, wide), lambda p, j, i: (0, j, p)),
                   pl.BlockSpec((hs, t, 1), lambda p, j, i: (p, j, 0)),
                   pl.BlockSpec((hs, 1, t), lambda p, j, i: (p, 0, j))],
        out_shape=[jax.ShapeDtypeStruct((3, s, d), F32), jax.ShapeDtypeStruct((hs * n_pairs, s, 1), F32),
                   jax.ShapeDtypeStruct((hs * n_pairs, 1, s), F32)],
        scratch_shapes=[pltpu.VMEM((s, wide), F32), pltpu.VMEM((2, t, wide), F32), pltpu.VMEM((hs, t, 1), F32),
                        pltpu.VMEM((nt, hs, 1, t), F32)],
        compiler_params=_params(3),
    )(qkn, qkn, proj, f_col, f_row, lse_row, do, o)


def loss_head(y, target):
    s, d = y.shape
    tm = _row_tile(s)

    def body(y_ref, t_ref, dy_ref, loss_ref):
        @pl.when(pl.program_id(0) == 0)
        def _():
            loss_ref[...] = jnp.zeros_like(loss_ref)

        diff = y_ref[...] - t_ref[...]
        dy_ref[...] = diff * (1.0 / d)
        loss_ref[...] += 0.5 * jnp.sum(jnp.mean(diff * diff, axis=-1, keepdims=True), axis=0, keepdims=True)

    row = pl.BlockSpec((tm, d), lambda i: (i, 0))
    return pl.pallas_call(
        body, name="loss_head", grid=(s // tm,),
        in_specs=[row, row],
        out_specs=[row, pl.BlockSpec((8, LANES), lambda i: (0, 0))],
        out_shape=[jax.ShapeDtypeStruct((s, d), F32), jax.ShapeDtypeStruct((8, LANES), F32)],
        compiler_params=_params(1),
    )(y, target)


def ada_mod(c_all, w, b):
    n_layers, d, cols = w.shape

    def body(c_ref, w_ref, b_ref, o_ref):
        cv = c_ref[...]
        o_ref[...] = _nn(cv * _sigmoid(cv), w_ref[...]) + b_ref[...]

    return pl.pallas_call(
        body, name="ada_mod", grid=(n_layers,),
        in_specs=[pl.BlockSpec((N_DEV, d), lambda l: (0, 0)), pl.BlockSpec((None, d, cols), lambda l: (l, 0, 0)),
                  pl.BlockSpec((None, 1, cols), lambda l: (l, 0, 0))],
        out_specs=pl.BlockSpec((None, N_DEV, cols), lambda l: (l, 0, 0)),
        out_shape=jax.ShapeDtypeStruct((n_layers, N_DEV, cols), F32),
        compiler_params=_params(1),
    )(c_all, w, b)


def ada_grad(c_t, dmod):
    d = c_t.shape[0]
    n_layers, _, cols = dmod.shape
    tn = cols // 2

    def body(c_ref, dm_ref, o_ref):
        cv = c_ref[...]
        o_ref[...] = _nn(cv * _sigmoid(cv), dm_ref[...])

    return pl.pallas_call(
        body, name="ada_grad", grid=(n_layers, 2),
        in_specs=[pl.BlockSpec((d, N_DEV), lambda l, n: (0, 0)), pl.BlockSpec((None, N_DEV, tn), lambda l, n: (l, 0, n))],
        out_specs=pl.BlockSpec((None, d, tn), lambda l, n: (l, 0, n)),
        out_shape=jax.ShapeDtypeStruct((n_layers, d, cols), F32),
        compiler_params=_params(2),
    )(c_t, dmod)


def sum_devices(v):
    def body(v_ref, o_ref):
        acc = v_ref[0]
        for k in range(1, N_DEV):
            acc = acc + v_ref[k]
        o_ref[...] = acc

    return pl.pallas_call(body, name="sum_devices", out_shape=jax.ShapeDtypeStruct(v.shape[1:], F32))(v)


def sum_slots(r, own):
    _, rows, cols = r.shape
    tm = 256 if rows % 256 == 0 else rows

    def body(r_ref, own_ref, o_ref):
        o_ref[...] = ((own_ref[...].astype(F32) + r_ref[0].astype(F32)) + r_ref[1].astype(F32)) + r_ref[2].astype(F32)

    return pl.pallas_call(
        body, name="sum_slots", grid=(rows // tm,),
        in_specs=[pl.BlockSpec((N_CHIP - 1, tm, cols), lambda i: (0, i, 0)), pl.BlockSpec((tm, cols), lambda i: (i, 0))],
        out_specs=pl.BlockSpec((tm, cols), lambda i: (i, 0)),
        out_shape=jax.ShapeDtypeStruct((rows, cols), F32),
        compiler_params=_params(1),
    )(r, own)


def adamw(w, m, v, g, g2=None):
    rows, cols = w.shape
    tm = 256 if rows % 256 == 0 else rows
    two = g2 is not None
    c1 = 1.0 - ADAM_B1 ** ADAM_STEP
    c2 = 1.0 - ADAM_B2 ** ADAM_STEP

    def body(w_ref, m_ref, v_ref, g_ref, *rest):
        if two:
            g2_ref, go_ref, d_ref, mo_ref, vo_ref = rest
            gv = g_ref[...] + g2_ref[...]
        else:
            go_ref, d_ref, mo_ref, vo_ref = rest
            gv = g_ref[...]
        mn = ADAM_B1 * m_ref[...] + (1.0 - ADAM_B1) * gv
        vn = ADAM_B2 * v_ref[...] + (1.0 - ADAM_B2) * (gv * gv)
        go_ref[...] = gv
        mo_ref[...] = mn
        vo_ref[...] = vn
        d_ref[...] = -ADAM_LR * ((mn / c1) / (jnp.sqrt(vn / c2) + ADAM_EPS) + ADAM_WD * w_ref[...])

    blk = pl.BlockSpec((tm, cols), lambda i: (i, 0))
    ops = [w, m, v, g] + ([g2] if two else [])
    return pl.pallas_call(
        body, name="adamw", grid=(rows // tm,),
        in_specs=[blk] * len(ops), out_specs=[blk] * 4,
        out_shape=[jax.ShapeDtypeStruct((rows, cols), F32)] * 4,
        compiler_params=_params(1),
    )(*ops)


def _pad_rows(flat):
    n = flat.shape[0]
    rows = -(-n // LANES)
    return jnp.pad(flat, (0, rows * LANES - n)).reshape(rows, LANES)


def _pad_rows8(flat):
    rows = _pad_rows(flat)
    return jnp.pad(rows, ((0, -rows.shape[0] % 8), (0, 0)))


def _col_tiles(n):
    return next(k for k in range(1, n // LANES + 1) if n % (k * LANES) == 0 and n // k <= 1536)


def kernel(x, c, ada_w, ada_b, norm_g, ffn_w13, ffn_w2, rel_bias, swa_w_in, swa_w_out, swa_q_g, swa_k_g, swa_sink, fox_w_in, fox_w_out, fox_b_f, fox_q_g, fox_k_g, loss_target, m_ada_w, m_ada_b, m_norm_g, m_ffn_w13, m_ffn_w2, m_rel_bias, m_swa_w_in, m_swa_w_out, m_swa_q_g, m_swa_k_g, m_swa_sink, m_fox_w_in, m_fox_w_out, m_fox_b_f, m_fox_q_g, m_fox_k_g, v_ada_w, v_ada_b, v_norm_g, v_ffn_w13, v_ffn_w2, v_rel_bias, v_swa_w_in, v_swa_w_out, v_swa_q_g, v_swa_k_g, v_swa_sink, v_fox_w_in, v_fox_w_out, v_fox_b_f, v_fox_q_g, v_fox_k_g):
    ix, iy, ic = lax.axis_index("x"), lax.axis_index("y"), lax.axis_index("c")
    chip = 2 * ix + iy
    dev = 2 * chip + ic
    s, d = x.shape[1:]
    n_layers = ada_w.shape[0]
    n_a, n_b = swa_w_in.shape[0], fox_w_in.shape[0]
    hq = d // HEAD_DIM
    hkv = hq // GROUP
    kw = hkv * HEAD_DIM
    c13 = ffn_w13.shape[-1]
    f = 2 * c13
    r2 = ffn_w2.shape[2]
    cq = norm_g.shape[-1]
    a_in = d + 2 * kw
    fx = fox_w_in.shape[-1]
    b_in = N_CHIP * fx
    b_pad = 3 * d + LANES
    x0 = x[0]

    hello = _pad_rows8(jnp.concatenate([c.reshape(-1), norm_g.reshape(-1)]))
    hello_all = all_gather_rows(hello, "gather_c_norm").reshape(N_DEV, -1)
    c_all = hello_all[:, :d]
    ng = hello_all[::2, d:d + n_layers * 3 * cq].reshape(N_CHIP, n_layers, 3, cq)
    norm_full = jnp.moveaxis(ng, 0, 2).reshape(n_layers, 3, d)

    half_cols = ada_w.shape[-1] // 2
    w_half = lax.dynamic_slice_in_dim(ada_w, ic * half_cols, half_cols, axis=2)
    b_half = lax.dynamic_slice_in_dim(ada_b, dev * half_cols, half_cols, axis=1)[:, None, :]
    mod_part = ada_mod(c_all, w_half, b_half)
    mod_all = all_gather_rows(mod_part.reshape(n_layers * N_DEV, half_cols), "gather_mod")
    mod_all = mod_all.reshape(N_DEV, n_layers, N_DEV, half_cols)
    mod_mine = lax.dynamic_index_in_dim(mod_all, dev, axis=2, keepdims=False)
    mod_mine = jnp.moveaxis(mod_mine, 0, 1).reshape(n_layers, 3, 3, d)
    mv = jnp.concatenate([norm_full[:, :, None, :], mod_mine, jnp.zeros((n_layers, 3, 4, d), F32)], axis=2)

    n_groups = n_layers // 2
    kinds = [("col", c13), ("row", r2), ("col", swa_w_in.shape[-1]), ("slot",), ("slot",), ("slot",)]

    def group_buffers(g, after):
        raw = [ffn_w13[2 * g:2 * g + 2], ffn_w2[2 * g:2 * g + 2], swa_w_in[g], swa_w_out[g], fox_w_in[g], fox_w_out[g]]
        return [cast_and_place(r, k, chip, after, f"place_weights_{g}_{t}") for t, (r, k) in enumerate(zip(raw, kinds))]

    def group_weights(fulls):
        w13_g, w2_g, wa_in_g, wa_out_g, wb_slots, wb_out_g = fulls
        wb_in_g = jnp.pad(jnp.concatenate([wb_slots[b] for b in range(N_CHIP)], axis=-1), ((0, 0), (0, b_pad - b_in)))
        return dict(w13=w13_g, w2=w2_g, wa_in=wa_in_g, wa_out=wa_out_g.reshape(d, d), wb_in=wb_in_g,
                    wb_out=wb_out_g.reshape(d, d))

    weights = [group_weights(gather_weights(group_buffers(0, mv), kinds))]
    in_flight = []
    for g in range(1, n_groups):
        in_flight.append(exchange_start([], group_buffers(g, weights[0]["w13"]), kinds, True, f"gather_start_{g}"))
    for *_, token in in_flight:
        mv = mv + token[0, 0]

    bucket = jnp.asarray(_rel_bucket_table())
    bias = rel_bias_table(rel_bias, bucket)
    tm, tw = _row_tile(s), _wide_tile(s)
    n13 = 2 * f // c13
    na_t, nb_t = _col_tiles(a_in), _col_tiles(3 * d)
    wa_t, wb_t = a_in // na_t, 3 * d // nb_t
    gate_blk = 3 * d // LANES

    def ffn_forward(xv, l, half, sub):
        wg, li = weights[l // 2], l % 2
        a, h = modmm(xv, mv, (l, sub), wg["w13"], pl.BlockSpec((None, None, d, c13), lambda i, n: (li, half, 0, n)), n13,
                     jax.ShapeDtypeStruct((2, s, f), BF16),
                     pl.BlockSpec((None, tw, c13), lambda i, n: (n // 2, i, n % 2)), f"ffn_up_{l}_{half}", True)
        xo, y, u = resmm(xv, mv, (l, sub), 0.5, a, pl.BlockSpec((2, tm, c13), lambda i, k: (0, i, k)),
                         wg["w2"], pl.BlockSpec((None, None, c13, d), lambda i, k: (li, half, k, 0)), f // c13, c13,
                         f"ffn_down_{l}_{half}", True)
        return xo, dict(x=xv, h=h, a=a, u=u, y=y)

    saved = []
    xv = x0
    for l in range(n_layers):
        j = l // 2
        if l % 2 == 0 and j >= 1:
            send_sems, recv_sems, shards, fulls, _ = in_flight[j - 1]
            weights.append(group_weights(exchange_wait(send_sems, recv_sems, shards, fulls, xv, kinds, True,
                                                       f"gather_wait_{j}")[1]))
        wg = weights[j]
        xv, s0 = ffn_forward(xv, l, 0, 0)
        if l % 2 == 0:
            proj, h = modmm(xv, mv, (l, 1), wg["wa_in"], pl.BlockSpec((d, wa_t), lambda i, n: (0, n)), na_t,
                            jax.ShapeDtypeStruct((s, a_in), BF16), pl.BlockSpec((tw, wa_t), lambda i, n: (i, n)),
                            f"swa_in_{j}", True)
            gains = jnp.concatenate([jnp.tile(swa_q_g[j] * HEAD_DIM ** -0.5, hq), jnp.tile(swa_k_g[j], hkv)])[None, :]
            qkn = qknorm_fwd(proj, gains, kw, f"swa_qknorm_{j}")
            sink = swa_sink[j][None, :]
            o, lse = swa_fwd(qkn, proj, bias, sink, d, f"swa_attn_{j}")
            s1 = dict(x=xv, h=h, proj=proj, gains=gains, qkn=qkn, sink=sink, o=o, lse=lse)
            w_out = wg["wa_out"]
        else:
            proj, h = modmm(xv, mv, (l, 1), wg["wb_in"], pl.BlockSpec((d, wb_t), lambda i, n: (0, n)), nb_t,
                            jax.ShapeDtypeStruct((s, 3 * d), BF16), pl.BlockSpec((tw, wb_t), lambda i, n: (i, n)),
                            f"fox_in_{j}", True)
            fl, _ = modmm(xv, mv, (l, 1), wg["wb_in"], pl.BlockSpec((d, LANES), lambda i, n: (0, gate_blk)), 1,
                          jax.ShapeDtypeStruct((s, LANES), F32), pl.BlockSpec((tw, LANES), lambda i, n: (i, 0)),
                          f"fox_gate_in_{j}", False)
            b_f = jnp.pad(fox_b_f[j], (0, LANES - hq))[None, :]
            fcum = fox_gates(fl, b_f, f"fox_gates_{j}")
            f_t = fcum[:, :hq].T
            f_col, f_row = f_t[:, :, None], f_t[:, None, :]
            gains = jnp.concatenate([jnp.tile(fox_q_g[j] * HEAD_DIM ** -0.5, hq), jnp.tile(fox_k_g[j], hq)])[None, :]
            qkn = qknorm_fwd(proj, gains, d, f"fox_qknorm_{j}")
            o, o32, lse = fox_fwd(qkn, proj, f_col, f_row, d, f"fox_attn_{j}")
            s1 = dict(x=xv, h=h, proj=proj, gains=gains, qkn=qkn, fl=fl, b_f=b_f, f_col=f_col, f_row=f_row, o=o, o32=o32,
                      lse=lse)
            w_out = wg["wb_out"]
        xv, y = resmm(xv, mv, (l, 1), 1.0, o, pl.BlockSpec((tm, d), lambda i, k: (i, 0)), w_out,
                      pl.BlockSpec((d, d), lambda i, k: (0, 0)), 1, d, f"mixer_out_{l}", False)
        s1["y"] = y
        xv, s2 = ffn_forward(xv, l, 1, 2)
        saved.append((s0, s1, s2))

    dxv, loss_part = loss_head(xv, loss_target[0])
    loss = lax.psum(loss_part[0, 0], ("x", "y", "c"))

    grads = [dict(w13=lax.empty((2, 2, d, 2 * f), BF16), w2=lax.empty((2, 2, f, d), BF16),
                  wa_in=lax.empty((d, a_in), BF16), wa_out=lax.empty((d, d), BF16),
                  wb_in=lax.empty((d, b_pad), BF16), wb_out=lax.empty((d, d), BF16)) for _ in range(n_groups)]
    dmod = [[None] * 3 for _ in range(n_layers)]
    dnorm = [[None] * 3 for _ in range(n_layers)]
    dqk_gain = {}
    dsink, db_f, dbias_tabs = {}, {}, []

    def ffn_backward(dxo, sv, l, half, sub):
        wg, gg, li = weights[l // 2], grads[l // 2], l % 2
        dy, da, dgate = resmm_bwd(dxo, sv["y"], mv, (l, sub), 0.5, wg["w2"],
                                  pl.BlockSpec((None, None, c13, d), lambda i, k: (li, half, k, 0)), f // c13, c13,
                                  f"ffn_down_bwd_{l}_{half}", a=sv["a"])
        gg["w2"] = weight_grad(sv["u"], pl.BlockSpec((tw, c13), lambda m, n, k: (k, m)), dy,
                               pl.BlockSpec((tw, d), lambda m, n, k: (k, 0)), (f // c13, 1), s, c13, d, gg["w2"],
                               pl.BlockSpec((None, None, c13, d), lambda m, n, k: (li, half, m, 0)),
                               f"ffn_w2_grad_{l}_{half}")
        gg["w13"] = weight_grad(sv["h"], pl.BlockSpec((tw, d), lambda m, n, k: (k, 0)), da,
                                pl.BlockSpec((None, tw, c13), lambda m, n, k: (n // 2, k, n % 2)), (1, n13), s, d, c13,
                                gg["w13"], pl.BlockSpec((None, None, d, c13), lambda m, n, k: (li, half, 0, n)),
                                f"ffn_w13_grad_{l}_{half}")
        dx, red = modmm_bwd(da, pl.BlockSpec((None, tm, c13), lambda i, n: (n // 2, i, n % 2)), wg["w13"],
                            pl.BlockSpec((None, None, d, c13), lambda i, n: (li, half, 0, n)), n13, sv["x"], dxo, mv,
                            (l, sub), f"ffn_up_bwd_{l}_{half}")
        dmod[l][sub] = (red[1], red[2], dgate[0])
        dnorm[l][sub] = red[0]
        return dx

    def group_slabs(gg):
        gb_slots = jnp.stack([gg["wb_in"][:, b * fx:(b + 1) * fx] for b in range(N_CHIP)])
        return [gg["w13"], gg["w2"], gg["wa_in"], gg["wa_out"].reshape(N_CHIP, d // N_CHIP, d), gb_slots,
                gg["wb_out"].reshape(N_CHIP, d // N_CHIP, d)]

    shard_shapes = [(2,) + ffn_w13.shape[1:], (2,) + ffn_w2.shape[1:], swa_w_in.shape[1:], swa_w_out.shape[1:],
                    fox_w_in.shape[1:], fox_w_out.shape[1:]]
    scatter_in_flight = {}

    for l in reversed(range(n_layers)):
        j = l // 2
        wg, gg = weights[j], grads[j]
        s0, s1, s2 = saved[l]
        dxv = ffn_backward(dxv, s2, l, 1, 2)
        is_a = l % 2 == 0
        w_out = wg["wa_out"] if is_a else wg["wb_out"]
        dy, do, dgate = resmm_bwd(dxv, s1["y"], mv, (l, 1), 1.0, w_out, pl.BlockSpec((d, d), lambda i, k: (0, 0)),
                                  1, d, f"mixer_out_bwd_{l}")
        out_key = "wa_out" if is_a else "wb_out"
        gg[out_key] = weight_grad(s1["o"], pl.BlockSpec((tw, d), lambda m, n, k: (k, 0)), dy,
                                  pl.BlockSpec((tw, d), lambda m, n, k: (k, 0)), (1, 1), s, d, d, gg[out_key],
                                  pl.BlockSpec((d, d), lambda m, n, k: (0, 0)), f"mixer_out_grad_{l}")
        if is_a:
            d_qkv, dbias_tab, dsk = swa_bwd(s1["qkn"], s1["proj"], bias, s1["sink"], do, s1["o"], s1["lse"], d,
                                            f"swa_attn_bwd_{j}")
            dbias_tabs.append(dbias_tab)
            dsink[j] = dsk[0, :hq]
            dproj, dgain = qknorm_bwd(s1["proj"], s1["gains"], d_qkv, pl.BlockSpec((tm, kw), lambda c_, i: (i, c_)),
                                      a_in, kw, f"swa_qknorm_bwd_{j}")
            gg["wa_in"] = weight_grad(s1["h"], pl.BlockSpec((tw, d), lambda m, n, k: (k, 0)), dproj,
                                      pl.BlockSpec((tw, wa_t), lambda m, n, k: (k, n)), (1, na_t), s, d, wa_t, gg["wa_in"],
                                      pl.BlockSpec((d, wa_t), lambda m, n, k: (0, n)), f"swa_in_grad_{j}")
            dxv, red = modmm_bwd(dproj, pl.BlockSpec((tm, wa_t), lambda i, n: (i, n)), wg["wa_in"],
                                 pl.BlockSpec((d, wa_t), lambda i, n: (0, n)), na_t, s1["x"], dxv, mv, (l, 1),
                                 f"swa_in_bwd_{j}")
            dqk_gain[("a", j)] = (dgain[0, :d].reshape(hq, HEAD_DIM).sum(0) * HEAD_DIM ** -0.5,
                                  dgain[0, d:d + kw].reshape(hkv, HEAD_DIM).sum(0))
        else:
            lse_row = s1["lse"].reshape(hq, 1, s)
            d_qkv, df_col, dfq_row = fox_bwd(s1["qkn"], s1["proj"], s1["f_col"], s1["f_row"], lse_row, do, s1["o32"], d,
                                             f"fox_attn_bwd_{j}")
            lanes_of_heads = lambda a: jnp.pad(a.T, ((0, 0), (0, LANES - hq)))
            dfl, dbf = fox_gates_bwd(s1["fl"], s1["b_f"], lanes_of_heads(dfq_row[:, 0, :]), lanes_of_heads(df_col[:, :, 0]),
                                     f"fox_gates_bwd_{j}")
            db_f[j] = dbf[0, :hq]
            dproj, dgain = qknorm_bwd(s1["proj"], s1["gains"], d_qkv, pl.BlockSpec((None, tm, d), lambda c_, i: (c_, i, 0)),
                                      3 * d, d, f"fox_qknorm_bwd_{j}")
            gg["wb_in"] = weight_grad(s1["h"], pl.BlockSpec((tw, d), lambda m, n, k: (k, 0)), dproj,
                                      pl.BlockSpec((tw, wb_t), lambda m, n, k: (k, n)), (1, nb_t), s, d, wb_t, gg["wb_in"],
                                      pl.BlockSpec((d, wb_t), lambda m, n, k: (0, n)), f"fox_in_grad_{j}")
            gg["wb_in"] = weight_grad(s1["h"], pl.BlockSpec((tw, d), lambda m, n, k: (k, 0)), dfl,
                                      pl.BlockSpec((tw, LANES), lambda m, n, k: (k, 0)), (1, 1), s, d, LANES, gg["wb_in"],
                                      pl.BlockSpec((d, LANES), lambda m, n, k: (0, gate_blk)), f"fox_gate_in_grad_{j}")
            dxv, red = modmm_bwd(dproj, pl.BlockSpec((tm, wb_t), lambda i, n: (i, n)), wg["wb_in"],
                                 pl.BlockSpec((d, wb_t), lambda i, n: (0, n)), nb_t, s1["x"], dxv, mv, (l, 1),
                                 f"fox_in_bwd_{j}",
                                 more=(dfl, pl.BlockSpec((tm, LANES), lambda i, n: (i, 0)),
                                       pl.BlockSpec((d, LANES), lambda i, n: (0, gate_blk))))
            dqk_gain[("b", j)] = (dgain[0, :d].reshape(hq, HEAD_DIM).sum(0) * HEAD_DIM ** -0.5,
                                  dgain[0, d:2 * d].reshape(hq, HEAD_DIM).sum(0))
        dmod[l][1] = (red[1], red[2], dgate[0])
        dnorm[l][1] = red[0]
        dxv = ffn_backward(dxv, s0, l, 0, 0)
        if l % 2 == 0 and j >= 1:
            slabs = group_slabs(gg)
            lands = [lax.empty((N_CHIP - 1,) + tuple(shp), BF16) for shp in shard_shapes]
            scatter_in_flight[j] = exchange_start(slabs, lands, kinds, False, f"scatter_start_{j}")
            mv = mv + scatter_in_flight[j][-1][0, 0]
    grad_x = dxv[None]

    drel = rel_bias_grad(jnp.stack(dbias_tabs), bucket)
    dmod_flat = jnp.stack([jnp.stack([jnp.stack(dmod[l][sub]) for sub in range(3)]) for l in range(n_layers)]).reshape(-1)
    dnorm_flat = jnp.stack([jnp.stack(dnorm[l]) for l in range(n_layers)]).reshape(-1)
    pieces = [dmod_flat, dnorm_flat,
              jnp.stack([dqk_gain[("a", j)][0] for j in range(n_a)]).reshape(-1),
              jnp.stack([dqk_gain[("a", j)][1] for j in range(n_a)]).reshape(-1),
              jnp.stack([dqk_gain[("b", j)][0] for j in range(n_b)]).reshape(-1),
              jnp.stack([dqk_gain[("b", j)][1] for j in range(n_b)]).reshape(-1),
              jnp.stack([dsink[j] for j in range(n_a)]).reshape(-1),
              jnp.stack([db_f[j] for j in range(n_b)]).reshape(-1),
              drel.reshape(-1)]
    rows = [_pad_rows(p) for p in pieces]
    starts = np.cumsum([0] + [r.shape[0] for r in rows])
    total = -(-int(starts[-1]) // 8) * 8
    small = jnp.pad(jnp.concatenate(rows), ((0, total - int(starts[-1])), (0, 0)))
    small_all = all_gather_rows(small, "gather_small_grads").reshape(N_DEV, total, LANES)
    small_sum = sum_devices(small_all)

    def piece(k, shape):
        n = int(np.prod(shape))
        return small_sum[int(starts[k]):int(starts[k + 1])].reshape(-1)[:n].reshape(shape)

    g_ada_b = piece(0, (n_layers, 9 * d))
    g_norm = lax.dynamic_slice_in_dim(piece(1, (n_layers, 3, d)), chip * cq, cq, axis=2)
    g_swa_q, g_swa_k = piece(2, (n_a, HEAD_DIM)), piece(3, (n_a, HEAD_DIM))
    g_fox_q, g_fox_k = piece(4, (n_b, HEAD_DIM)), piece(5, (n_b, HEAD_DIM))
    g_sink, g_bf, g_rel = piece(6, (n_a, hq)), piece(7, (n_b, hq)), piece(8, (REL_BUCKETS, hq))

    dmod_all = small_all[:, :int(starts[1])].reshape(N_DEV, -1)[:, :n_layers * 9 * d].reshape(N_DEV, n_layers, 9 * d)
    ada_cols = ada_w.shape[-1]
    dmod_mine = lax.dynamic_slice_in_dim(jnp.moveaxis(dmod_all, 0, 1), chip * ada_cols, ada_cols, axis=2)
    g_ada_w = ada_grad(c_all.T, dmod_mine)

    sources = {0: group_slabs(grads[0])}
    landed = {0: scatter_grads(sources[0], kinds, shard_shapes)}
    for g, (send_sems, recv_sems, slabs, lands, _) in scatter_in_flight.items():
        sources[g], landed[g] = exchange_wait(send_sems, recv_sems, slabs, lands, landed[0][0], kinds, False,
                                              f"scatter_wait_{g}")

    def group_sum(g, t):
        cols = shard_shapes[t][-1]
        own = own_slab(sources[g][t], kinds[t], chip, shard_shapes[t])
        return sum_slots(landed[g][t].reshape(N_CHIP - 1, -1, cols), own.reshape(-1, cols))

    parts = [jnp.concatenate([group_sum(g, t) for g in range(n_groups)]) for t in range(len(kinds))]
    others = swap_with_sibling(parts)

    def update(w, m, v, g, g2=None):
        w2d = w.reshape(-1, w.shape[-1])
        outs = adamw(w2d, m.reshape(w2d.shape), v.reshape(w2d.shape), g.reshape(w2d.shape) if g2 is None else g, g2)
        return [t.reshape(w.shape) for t in outs]

    big = [(ffn_w13, m_ffn_w13, v_ffn_w13), (ffn_w2, m_ffn_w2, v_ffn_w2), (swa_w_in, m_swa_w_in, v_swa_w_in),
           (swa_w_out, m_swa_w_out, v_swa_w_out), (fox_w_in, m_fox_w_in, v_fox_w_in), (fox_w_out, m_fox_w_out, v_fox_w_out)]
    big_out = [update(w, m, v, p, q) for (w, m, v), p, q in zip(big, parts, others)]
    r_ada_w = update(ada_w, m_ada_w, v_ada_w, g_ada_w)
    r_ada_b = update(ada_b, m_ada_b, v_ada_b, g_ada_b)
    r_norm = update(norm_g, m_norm_g, v_norm_g, g_norm)
    r_rel = update(rel_bias, m_rel_bias, v_rel_bias, g_rel)
    r_swa_q = update(swa_q_g, m_swa_q_g, v_swa_q_g, g_swa_q)
    r_swa_k = update(swa_k_g, m_swa_k_g, v_swa_k_g, g_swa_k)
    r_sink = update(swa_sink, m_swa_sink, v_swa_sink, g_sink)
    r_bf = update(fox_b_f, m_fox_b_f, v_fox_b_f, g_bf)
    r_fox_q = update(fox_q_g, m_fox_q_g, v_fox_q_g, g_fox_q)
    r_fox_k = update(fox_k_g, m_fox_k_g, v_fox_k_g, g_fox_k)
    per_weight = [r_ada_w, r_ada_b, r_norm, big_out[0], big_out[1], r_rel, big_out[2], big_out[3], r_swa_q, r_swa_k,
                  r_sink, big_out[4], big_out[5], r_bf, r_fox_q, r_fox_k]
    return (loss, grad_x, *[r[0] for r in per_weight], *[r[1] for r in per_weight],
            *[r[2] for r in per_weight], *[r[3] for r in per_weight])
```

```python
import math

import numpy as np
import jax
import jax.numpy as jnp
from jax import lax
from jax.experimental import pallas as pl
from jax.experimental.pallas import tpu as pltpu

F32 = jnp.float32
BF16 = jnp.bfloat16
HEAD_DIM = 64
GROUP = 4
FOX_HEADS = 8
BLOCK = 128
REL_BUCKETS = 32
REL_MAX_DIST = 128
EPS = 1e-6
NEG = -1e30
N_CHIP = 4
N_DEV = 8
LANES = 128
VMEM_LIMIT = 52 * 1024 * 1024
ADAM_LR, ADAM_B1, ADAM_B2, ADAM_EPS, ADAM_WD, ADAM_STEP = 0.001, 0.9, 0.999, 1e-08, 0.01, 10
MESH = pl.DeviceIdType.MESH
ANY = pl.BlockSpec(memory_space=pl.ANY)


def _params(n_axes):
    return pltpu.CompilerParams(dimension_semantics=("arbitrary",) * n_axes, vmem_limit_bytes=VMEM_LIMIT)


def _nn(a, b):
    return jnp.dot(a, b, preferred_element_type=F32)


def _nt(a, b):
    return lax.dot_general(a, b, (((1,), (1,)), ((), ())), preferred_element_type=F32)


def _tn(a, b):
    return lax.dot_general(a, b, (((0,), (0,)), ((), ())), preferred_element_type=F32)


def _sigmoid(z):
    return 1.0 / (1.0 + jnp.exp(-z))


def _sigmoid_fast(z):
    return pl.reciprocal(1.0 + jnp.exp(-z), approx=True)


def _row_tile(s):
    return 512 if s >= 2048 else s // 2


def _wide_tile(s):
    return 1024 if s >= 2048 else s // 2


def _attn_tile(s):
    return 512 if s >= 2048 else s // 4


def _position():
    x, y, c = lax.axis_index("x"), lax.axis_index("y"), lax.axis_index("c")
    chips = [(1 - x, y), (x, 1 - y), (1 - x, 1 - y)]
    return x, y, c, chips


def all_gather_rows(v, name):
    m_per, n = v.shape

    def body(x_ref, out_ref, send_sems, recv_sems, local_sem):
        x, y, c, chips = _position()
        me, sibling = (x, y, c), (x, y, 1 - c)

        def rows(px, py, pc):
            return out_ref.at[pl.ds((4 * px + 2 * py + pc) * m_per, m_per), :]

        def copy(k, block, to, src=None):
            return pltpu.make_async_remote_copy(
                src_ref=rows(*block) if src is None else src, dst_ref=rows(*block),
                send_sem=send_sems.at[k], recv_sem=recv_sems.at[k], device_id=to, device_id_type=MESH)

        mine = pltpu.make_async_copy(x_ref, rows(*me), local_sem)
        mine.start()
        first = [copy(0, me, sibling, src=x_ref)]
        first += [copy(1 + j, me, (*chip, c), src=x_ref) for j, chip in enumerate(chips)]
        for cp in first:
            cp.start()
        passed = [copy(4 + j, (*chip, c), sibling) for j, chip in enumerate(chips)]
        for j, chip in enumerate(chips):
            copy(1 + j, (*chip, c), me).wait_recv()
            passed[j].start()
        copy(0, sibling, me).wait_recv()
        for j, chip in enumerate(chips):
            copy(4 + j, (*chip, 1 - c), me).wait_recv()
        for cp in first + passed:
            cp.wait_send()
        mine.wait()

    return pl.pallas_call(
        body, name=name,
        out_shape=jax.ShapeDtypeStruct((N_DEV * m_per, n), v.dtype),
        in_specs=[pl.BlockSpec(memory_space=pltpu.VMEM)],
        out_specs=pl.BlockSpec(memory_space=pltpu.VMEM),
        scratch_shapes=[pltpu.SemaphoreType.DMA((7,)), pltpu.SemaphoreType.DMA((7,)), pltpu.SemaphoreType.DMA],
    )(v)


def _slab(full_ref, kind, b, lead):
    how = kind[0]
    if how == "slot":
        return full_ref.at[b, lead]
    if how == "col":
        w = kind[1]
        idx = (lead,) + (slice(None),) * (len(full_ref.shape) - 2) + (pl.ds(pl.multiple_of(b * w, LANES), w),)
        return full_ref.at[idx]
    h = kind[1]
    idx = (lead,) + (slice(None),) * (len(full_ref.shape) - 3) + (pl.ds(pl.multiple_of(b * h, 8), h), slice(None))
    return full_ref.at[idx]


def _full_shape(shard_shape, kind):
    if kind[0] == "slot":
        return (N_CHIP,) + tuple(shard_shape)
    if kind[0] == "col":
        return tuple(shard_shape[:-1]) + (N_CHIP * shard_shape[-1],)
    return tuple(shard_shape[:-2]) + (N_CHIP * shard_shape[-2], shard_shape[-1])


def _slab_start(shape, kind, b):
    zeros = [0] * len(shape)
    if kind[0] == "slot":
        return [b] + zeros[1:]
    if kind[0] == "col":
        return zeros[:-1] + [b * kind[1]]
    return zeros[:-2] + [b * kind[1], 0]


def cast_and_place(shard, kind, chip, after, name):
    rows, cols = shard.shape[-2:]
    lead = int(np.prod(shard.shape[:-2]))
    if kind[0] == "col":
        full3, where = (lead, rows, N_CHIP * cols), lambda p, b: (p, 0, b[0])
    elif kind[0] == "row":
        full3, where = (lead, N_CHIP * rows, cols), lambda p, b: (p, b[0], 0)
    else:
        full3, where = (N_CHIP * lead, rows, cols), lambda p, b: (b[0] * lead + p, 0, 0)

    def body(b_ref, s_ref, after_ref, o_ref):
        o_ref[...] = s_ref[...].astype(BF16)

    full = pl.pallas_call(
        body, name=name,
        grid_spec=pltpu.PrefetchScalarGridSpec(
            num_scalar_prefetch=1, grid=(lead,),
            in_specs=[pl.BlockSpec((1, rows, cols), lambda p, b: (p, 0, 0)), ANY],
            out_specs=pl.BlockSpec((1, rows, cols), where)),
        out_shape=jax.ShapeDtypeStruct(full3, BF16),
        compiler_params=_params(1),
    )(jnp.reshape(chip, (1,)).astype(jnp.int32), shard.reshape(lead, rows, cols), after)
    return full.reshape(_full_shape(shard.shape, kind))


def own_slab(full, kind, b, shard_shape):
    sizes = (1,) + tuple(shard_shape) if kind[0] == "slot" else tuple(shard_shape)
    return lax.dynamic_slice(full, _slab_start(full.shape, kind, b), sizes).reshape(shard_shape)


def gather_weights(fulls, kinds):
    n = len(fulls)

    def body(*refs):
        outs = refs[n:2 * n]
        send_sems, recv_sems = refs[2 * n:]
        x, y, c, chips = _position()
        b_me = 2 * x + y
        sibling = (x, y, 1 - c)
        sends = []

        def halves(t):
            lead = outs[t].shape[1] if kinds[t][0] == "slot" else outs[t].shape[0]
            return pl.ds(c * (lead // 2), lead // 2), pl.ds((1 - c) * (lead // 2), lead // 2)

        for t in range(n):
            mine, _ = halves(t)
            own = _slab(outs[t], kinds[t], b_me, mine)
            for j, chip in enumerate(chips):
                cp = pltpu.make_async_remote_copy(
                    src_ref=own, dst_ref=own,
                    send_sem=send_sems.at[6 * t + j], recv_sem=recv_sems.at[6 * t + j],
                    device_id=(*chip, c), device_id_type=MESH)
                cp.start()
                sends.append(cp)
        for t in range(n):
            mine, _ = halves(t)
            for j, chip in enumerate(chips):
                landed = _slab(outs[t], kinds[t], 2 * chip[0] + chip[1], mine)
                pltpu.make_async_remote_copy(
                    src_ref=landed, dst_ref=landed, send_sem=send_sems.at[6 * t + j], recv_sem=recv_sems.at[6 * t + j],
                    device_id=(*chip, c), device_id_type=MESH).wait_recv()
                cp = pltpu.make_async_remote_copy(
                    src_ref=landed, dst_ref=landed, send_sem=send_sems.at[6 * t + 3 + j],
                    recv_sem=recv_sems.at[6 * t + 3 + j], device_id=sibling, device_id_type=MESH)
                cp.start()
                sends.append(cp)
        for t in range(n):
            _, theirs = halves(t)
            for j, chip in enumerate(chips):
                landed = _slab(outs[t], kinds[t], 2 * chip[0] + chip[1], theirs)
                pltpu.make_async_remote_copy(
                    src_ref=landed, dst_ref=landed, send_sem=send_sems.at[6 * t + 3 + j],
                    recv_sem=recv_sems.at[6 * t + 3 + j], device_id=sibling, device_id_type=MESH).wait_recv()
        for cp in sends:
            cp.wait_send()

    return pl.pallas_call(
        body, name="gather_weights",
        out_shape=[jax.ShapeDtypeStruct(v.shape, v.dtype) for v in fulls],
        in_specs=[ANY] * n, out_specs=[ANY] * n,
        input_output_aliases={t: t for t in range(n)},
        scratch_shapes=[pltpu.SemaphoreType.DMA((6 * n,)), pltpu.SemaphoreType.DMA((6 * n,))],
    )(*fulls)


def scatter_grads(grads, kinds, shard_shapes):
    n = len(grads)

    def body(*refs):
        ins, outs = refs[:n], refs[n:2 * n]
        send_sems, recv_sems = refs[2 * n:]
        x, y, c, chips = _position()
        sends = []
        for t in range(n):
            whole = _whole(ins[t], kinds[t])
            for j, chip in enumerate(chips):
                cp = pltpu.make_async_remote_copy(
                    src_ref=_slab(ins[t], kinds[t], 2 * chip[0] + chip[1], whole), dst_ref=outs[t].at[j],
                    send_sem=send_sems.at[3 * t + j], recv_sem=recv_sems.at[3 * t + j],
                    device_id=(*chip, c), device_id_type=MESH)
                cp.start()
                sends.append(cp)
        for t in range(n):
            for j, chip in enumerate(chips):
                pltpu.make_async_remote_copy(
                    src_ref=outs[t].at[j], dst_ref=outs[t].at[j], send_sem=send_sems.at[3 * t + j],
                    recv_sem=recv_sems.at[3 * t + j], device_id=(*chip, c), device_id_type=MESH).wait_recv()
        for cp in sends:
            cp.wait_send()

    return pl.pallas_call(
        body, name="scatter_grads",
        out_shape=[jax.ShapeDtypeStruct((N_CHIP - 1,) + tuple(s), g.dtype) for g, s in zip(grads, shard_shapes)],
        in_specs=[ANY] * n, out_specs=[ANY] * n,
        scratch_shapes=[pltpu.SemaphoreType.DMA((3 * n,)), pltpu.SemaphoreType.DMA((3 * n,))],
    )(*grads)


def _whole(ref, kind):
    return pl.ds(0, ref.shape[1] if kind[0] == "slot" else ref.shape[0])


def _exchange_copies(srcs, lands, kinds, gather, send_sems, recv_sems):
    x, y, c, chips = _position()
    b_me = 2 * x + y
    out = []
    for t in range(len(lands)):
        for j, chip in enumerate(chips):
            b_j = 2 * chip[0] + chip[1]
            if gather:
                src = sent_to = _slab(lands[t], kinds[t], b_me, _whole(lands[t], kinds[t]))
                arrives = _slab(lands[t], kinds[t], b_j, _whole(lands[t], kinds[t]))
            else:
                src = _slab(srcs[t], kinds[t], b_j, _whole(srcs[t], kinds[t]))
                sent_to = arrives = lands[t].at[j]
            k = 3 * t + j
            send = pltpu.make_async_remote_copy(src_ref=src, dst_ref=sent_to, send_sem=send_sems.at[k],
                                                recv_sem=recv_sems.at[k], device_id=(*chip, c), device_id_type=MESH)
            recv = pltpu.make_async_remote_copy(src_ref=src, dst_ref=arrives, send_sem=send_sems.at[k],
                                                recv_sem=recv_sems.at[k], device_id=(*chip, c), device_id_type=MESH)
            out.append((send, recv))
    return out


def exchange_start(srcs, lands, kinds, gather, name):
    ns, nl = len(srcs), len(lands)
    hbm = pl.BlockSpec(memory_space=pltpu.HBM)

    def body(*refs):
        ins, lnd = refs[:ns], refs[ns:ns + nl]
        send_sems, recv_sems = refs[ns + nl], refs[ns + nl + 1]
        token = refs[-1]
        for send, _ in _exchange_copies(ins, lnd, kinds, gather, send_sems, recv_sems):
            send.start()
        token[...] = jnp.zeros_like(token)

    ops = [pltpu.with_memory_space_constraint(v, pltpu.HBM) for v in (*srcs, *lands)]
    res = pl.pallas_call(
        body, name=name,
        out_shape=(pltpu.SemaphoreType.DMA((3 * nl,)), pltpu.SemaphoreType.DMA((3 * nl,)),
                   *[pltpu.HBM(v.shape, v.dtype) for v in ops], jax.ShapeDtypeStruct((8, LANES), F32)),
        in_specs=[hbm] * (ns + nl),
        out_specs=(pl.BlockSpec(memory_space=pltpu.SEMAPHORE), pl.BlockSpec(memory_space=pltpu.SEMAPHORE),
                   *[hbm] * (ns + nl), pl.BlockSpec(memory_space=pltpu.VMEM)),
        input_output_aliases={t: 2 + t for t in range(ns + nl)},
        compiler_params=pltpu.CompilerParams(has_side_effects=pltpu.SideEffectType.DATAFLOW_SIDE_EFFECTING),
    )(*ops)
    return res[0], res[1], list(res[2:2 + ns]), list(res[2 + ns:2 + ns + nl]), res[-1]


def exchange_wait(send_sems, recv_sems, srcs, lands, after, kinds, gather, name):
    ns, nl = len(srcs), len(lands)
    hbm = pl.BlockSpec(memory_space=pltpu.HBM)

    def body(*refs):
        ins, lnd = refs[:ns], refs[ns:ns + nl]
        ssem, rsem = refs[ns + nl], refs[ns + nl + 1]
        for send, recv in _exchange_copies(ins, lnd, kinds, gather, ssem, rsem):
            send.wait_send()
            recv.wait_recv()

    res = pl.pallas_call(
        body, name=name,
        out_shape=tuple(pltpu.HBM(v.shape, v.dtype) for v in (*srcs, *lands)),
        in_specs=[hbm] * (ns + nl) + [pl.BlockSpec(memory_space=pltpu.SEMAPHORE)] * 2 + [ANY],
        out_specs=tuple([hbm] * (ns + nl)),
        input_output_aliases={t: t for t in range(ns + nl)},
        compiler_params=pltpu.CompilerParams(has_side_effects=pltpu.SideEffectType.DATAFLOW_SIDE_EFFECTING),
    )(*srcs, *lands, send_sems, recv_sems, after)
    return list(res[:ns]), list(res[ns:])


def swap_with_sibling(parts):
    n = len(parts)

    def body(*refs):
        ins, outs = refs[:n], refs[n:2 * n]
        send_sems, recv_sems = refs[2 * n:]
        x, y, c, _ = _position()
        cps = []
        for t in range(n):
            cp = pltpu.make_async_remote_copy(
                src_ref=ins[t], dst_ref=outs[t], send_sem=send_sems.at[t], recv_sem=recv_sems.at[t],
                device_id=(x, y, 1 - c), device_id_type=MESH)
            cp.start()
            cps.append(cp)
        for cp in cps:
            cp.wait_recv()
        for cp in cps:
            cp.wait_send()

    return pl.pallas_call(
        body, name="swap_with_sibling",
        out_shape=[jax.ShapeDtypeStruct(p.shape, p.dtype) for p in parts],
        in_specs=[ANY] * n, out_specs=[ANY] * n,
        scratch_shapes=[pltpu.SemaphoreType.DMA((n,)), pltpu.SemaphoreType.DMA((n,))],
    )(*parts)


def _modulated(xv, mv_ref):
    g, shift, scale = mv_ref[0:1, :], mv_ref[1:2, :], mv_ref[2:3, :]
    r = lax.rsqrt(jnp.mean(xv * xv, axis=-1, keepdims=True) + EPS)
    xhat = xv * r
    xn = xhat * g
    return xn * (1.0 + scale) + shift, xhat, xn, r, g, scale


def _resident(block_shape, index):
    return pl.BlockSpec(block_shape, lambda i: index, pipeline_mode=pl.Buffered(1))


def modmm(x, mv, mv_idx, w, w_block, w_index, n_cols, chunk, out_dtype, name, want_h):
    s, d = x.shape
    tm = _row_tile(s)

    def body(x_ref, mv_ref, w_ref, out_ref, *rest):
        h = _modulated(x_ref[...], mv_ref)[0].astype(BF16)
        if want_h:
            rest[0][...] = h
        for n in range(n_cols // chunk):
            cols = slice(n * chunk, (n + 1) * chunk)
            out_ref[:, cols] = _nn(h, w_ref[:, cols]).astype(out_dtype)

    row = pl.BlockSpec((tm, d), lambda i: (i, 0))
    out_shapes = [jax.ShapeDtypeStruct((s, n_cols), out_dtype)]
    out_specs = [pl.BlockSpec((tm, n_cols), lambda i: (i, 0))]
    if want_h:
        out_shapes.append(jax.ShapeDtypeStruct((s, d), BF16))
        out_specs.append(row)
    res = pl.pallas_call(
        body, name=name, grid=(s // tm,),
        in_specs=[row, pl.BlockSpec((None, None, 8, d), lambda i: (*mv_idx, 0, 0)), _resident(w_block, w_index)],
        out_specs=out_specs, out_shape=out_shapes,
        compiler_params=_params(1),
    )(x, mv, w)
    return res if want_h else (res[0], None)


def resmm(x, mv, mv_idx, coef, lhs, w, w_block, w_index, kdim, chunk, name, ffn):
    s, d = x.shape
    tm = _row_tile(s)
    lhs_cols = 2 * kdim if ffn else kdim

    def body(x_ref, mv_ref, lhs_ref, w_ref, xo_ref, y_ref, *rest):
        y = jnp.zeros((tm, d), F32)
        for k in range(kdim // chunk):
            rows = slice(k * chunk, (k + 1) * chunk)
            if ffn:
                ag = lhs_ref[:, rows].astype(F32)
                au = lhs_ref[:, kdim + k * chunk:kdim + (k + 1) * chunk].astype(F32)
                left = (ag * _sigmoid_fast(ag) * au).astype(BF16)
                rest[0][:, rows] = left
            else:
                left = lhs_ref[:, rows]
            y = y + _nn(left, w_ref[rows, :])
        y_ref[...] = y.astype(BF16)
        xo_ref[...] = x_ref[...] + (coef * mv_ref[3:4, :]) * y

    row = pl.BlockSpec((tm, d), lambda i: (i, 0))
    out_shapes = [jax.ShapeDtypeStruct((s, d), F32), jax.ShapeDtypeStruct((s, d), BF16)]
    out_specs = [row, row]
    if ffn:
        out_shapes.append(jax.ShapeDtypeStruct((s, kdim), BF16))
        out_specs.append(pl.BlockSpec((tm, kdim), lambda i: (i, 0)))
    return pl.pallas_call(
        body, name=name, grid=(s // tm,),
        in_specs=[row, pl.BlockSpec((None, None, 8, d), lambda i: (*mv_idx, 0, 0)),
                  pl.BlockSpec((tm, lhs_cols), lambda i: (i, 0)), _resident(w_block, w_index)],
        out_specs=out_specs, out_shape=out_shapes,
        compiler_params=_params(1),
    )(x, mv, lhs, w)


def resmm_bwd(dxo, y, mv, mv_idx, coef, w, w_block, w_index, kdim, chunk, name, a=None):
    s, d = dxo.shape
    ffn = a is not None
    tm = _row_tile(s) // 2 if ffn else _row_tile(s)
    dl_cols = 2 * kdim if ffn else kdim

    def body(dxo_ref, y_ref, mv_ref, w_ref, *rest):
        if ffn:
            a_ref, dy_ref, dl_ref, dgate_ref = rest
        else:
            dy_ref, dl_ref, dgate_ref = rest

        @pl.when(pl.program_id(0) == 0)
        def _():
            dgate_ref[...] = jnp.zeros_like(dgate_ref)

        dxv = dxo_ref[...]
        dy = ((coef * mv_ref[3:4, :]) * dxv).astype(BF16)
        dy_ref[...] = dy
        dgate_ref[0:1, :] += jnp.sum(coef * dxv * y_ref[...].astype(F32), axis=0, keepdims=True)
        for k in range(kdim // chunk):
            rows = slice(k * chunk, (k + 1) * chunk)
            dl = _nt(dy, w_ref[rows, :])
            if ffn:
                ups = slice(kdim + k * chunk, kdim + (k + 1) * chunk)
                ag = a_ref[:, rows].astype(F32)
                au = a_ref[:, ups].astype(F32)
                sg = _sigmoid_fast(ag)
                dl_ref[:, rows] = (dl * au * (sg * (1.0 + ag * (1.0 - sg)))).astype(BF16)
                dl_ref[:, ups] = (dl * (ag * sg)).astype(BF16)
            else:
                dl_ref[:, rows] = dl.astype(BF16)

    row = pl.BlockSpec((tm, d), lambda i: (i, 0))
    wide = pl.BlockSpec((tm, dl_cols), lambda i: (i, 0))
    in_specs = [row, row, pl.BlockSpec((None, None, 8, d), lambda i: (*mv_idx, 0, 0)), _resident(w_block, w_index)]
    ops = [dxo, y, mv, w]
    if ffn:
        in_specs.append(wide)
        ops.append(a)
    return pl.pallas_call(
        body, name=name, grid=(s // tm,),
        in_specs=in_specs,
        out_specs=[row, wide, pl.BlockSpec((8, d), lambda i: (0, 0))],
        out_shape=[jax.ShapeDtypeStruct((s, d), BF16), jax.ShapeDtypeStruct((s, dl_cols), BF16),
                   jax.ShapeDtypeStruct((8, d), F32)],
        compiler_params=_params(1),
    )(*ops)


def modmm_bwd(dl, w, w_block, w_index, n_cols, chunk, x, dxo, mv, mv_idx, name, more=None):
    s, d = x.shape
    tm = _row_tile(s)

    def body(dl_ref, w_ref, x_ref, dxo_ref, mv_ref, *rest):
        if more is not None:
            dl2_ref, w2_ref, dx_ref, red_ref = rest
            dh = _nt(dl2_ref[...], w2_ref[...])
        else:
            dx_ref, red_ref = rest
            dh = jnp.zeros((tm, d), F32)

        @pl.when(pl.program_id(0) == 0)
        def _():
            red_ref[...] = jnp.zeros_like(red_ref)

        for n in range(n_cols // chunk):
            cols = slice(n * chunk, (n + 1) * chunk)
            dh = dh + _nt(dl_ref[:, cols], w_ref[:, cols])
        _, xhat, xn, r, g, scale = _modulated(x_ref[...], mv_ref)
        dxn = dh * (1.0 + scale)
        red_ref[0:1, :] += jnp.sum(dxn * xhat, axis=0, keepdims=True)
        red_ref[1:2, :] += jnp.sum(dh, axis=0, keepdims=True)
        red_ref[2:3, :] += jnp.sum(dh * xn, axis=0, keepdims=True)
        gd = dxn * g
        dx_ref[...] = dxo_ref[...] + r * (gd - xhat * jnp.mean(gd * xhat, axis=-1, keepdims=True))

    row = pl.BlockSpec((tm, d), lambda i: (i, 0))
    in_specs = [pl.BlockSpec((tm, n_cols), lambda i: (i, 0)), _resident(w_block, w_index), row, row,
                pl.BlockSpec((None, None, 8, d), lambda i: (*mv_idx, 0, 0))]
    ops = [dl, w, x, dxo, mv]
    if more is not None:
        in_specs += [pl.BlockSpec((tm, more[0].shape[1]), lambda i: (i, 0)), _resident(more[1], more[2])]
        ops += [more[0], w]
    return pl.pallas_call(
        body, name=name, grid=(s // tm,),
        in_specs=in_specs,
        out_specs=[row, pl.BlockSpec((8, d), lambda i: (0, 0))],
        out_shape=[jax.ShapeDtypeStruct((s, d), F32), jax.ShapeDtypeStruct((8, d), F32)],
        compiler_params=_params(1),
    )(*ops)


def weight_grad(a, a_spec, b, b_spec, grid_mn, s, bm, bn, dest, out_spec, name):
    tk = _wide_tile(s)
    k_tiles = s // tk

    def body(a_ref, b_ref, dest_ref, out_ref, acc):
        k = pl.program_id(2)

        @pl.when(k == 0)
        def _():
            acc[...] = jnp.zeros_like(acc)

        acc[...] += _tn(a_ref[...], b_ref[...])

        @pl.when(k == k_tiles - 1)
        def _():
            out_ref[...] = acc[...].astype(out_ref.dtype)

    return pl.pallas_call(
        body, name=name, grid=(*grid_mn, k_tiles),
        in_specs=[a_spec, b_spec, ANY], out_specs=out_spec,
        out_shape=jax.ShapeDtypeStruct(dest.shape, dest.dtype),
        input_output_aliases={2: 0},
        scratch_shapes=[pltpu.VMEM((bm, bn), F32)],
        compiler_params=_params(3),
    )(a, b, dest)


def _head_mean(v):
    lane = lax.broadcasted_iota(jnp.int32, v.shape, 1)
    lo = jnp.sum(jnp.where(lane < HEAD_DIM, v, 0.0), axis=-1, keepdims=True)
    hi = jnp.sum(v, axis=-1, keepdims=True) - lo
    return jnp.where(lane < HEAD_DIM, lo, hi) * (1.0 / HEAD_DIM)


def qknorm_fwd(proj, gains, width, name):
    s = proj.shape[0]
    nqk = gains.shape[1]
    tm = _row_tile(s)

    def body(p_ref, g_ref, o_ref):
        for cc in range(width // LANES):
            sl = slice(cc * LANES, (cc + 1) * LANES)
            xv = p_ref[:, sl].astype(F32)
            r = lax.rsqrt(_head_mean(xv * xv) + EPS)
            o_ref[:, sl] = (xv * r * g_ref[:, sl]).astype(BF16)

    blk = pl.BlockSpec((tm, width), lambda i, c: (i, c))
    return pl.pallas_call(
        body, name=name, grid=(s // tm, nqk // width),
        in_specs=[blk, pl.BlockSpec((1, width), lambda i, c: (0, c))],
        out_specs=blk, out_shape=jax.ShapeDtypeStruct((s, nqk), BF16),
        compiler_params=_params(2),
    )(proj, gains)


def qknorm_bwd(proj, gains, d, d_spec, n_cols, width, name):
    s = proj.shape[0]
    nqk = gains.shape[1] // width
    n_blocks = n_cols // width
    tm = _row_tile(s)

    def body(p_ref, g_ref, d_ref, o_ref, dg_ref):
        c, i = pl.program_id(0), pl.program_id(1)

        @pl.when(i == 0)
        def _():
            dg_ref[...] = jnp.zeros_like(dg_ref)

        @pl.when(c < nqk)
        def _():
            for cc in range(width // LANES):
                sl = slice(cc * LANES, (cc + 1) * LANES)
                xv = p_ref[:, sl].astype(F32)
                r = lax.rsqrt(_head_mean(xv * xv) + EPS)
                xhat = xv * r
                dv = d_ref[:, sl]
                gd = dv * g_ref[:, sl]
                o_ref[:, sl] = (r * (gd - xhat * _head_mean(gd * xhat))).astype(BF16)
                dg_ref[0:1, sl] += jnp.sum(dv * xhat, axis=0, keepdims=True)

        @pl.when(c >= nqk)
        def _():
            o_ref[...] = d_ref[...].astype(BF16)

    return pl.pallas_call(
        body, name=name, grid=(n_blocks, s // tm),
        in_specs=[pl.BlockSpec((tm, width), lambda c, i: (i, c)),
                  pl.BlockSpec((1, width), lambda c, i: (0, jnp.minimum(c, nqk - 1))), d_spec],
        out_specs=[pl.BlockSpec((tm, width), lambda c, i: (i, c)), pl.BlockSpec((8, width), lambda c, i: (0, c))],
        out_shape=[jax.ShapeDtypeStruct((s, n_cols), BF16), jax.ShapeDtypeStruct((8, n_cols), F32)],
        compiler_params=_params(2),
    )(proj, gains, d)


def _swa_mask(first):
    qi = lax.broadcasted_iota(jnp.int32, (BLOCK, 2 * BLOCK), 0) + BLOCK
    kj = lax.broadcasted_iota(jnp.int32, (BLOCK, 2 * BLOCK), 1)
    dist = qi - kj
    return (dist >= 0) & (dist < BLOCK) & ((kj >= BLOCK) | jnp.logical_not(first))


def swa_fwd(qkn, proj, bias, sink, d, name):
    s = qkn.shape[0]
    hq = d // HEAD_DIM
    hkv = hq // GROUP
    kw = hkv * HEAD_DIM
    nblk = s // BLOCK
    kcol = d // kw

    def body(q_ref, kc_ref, kp_ref, vc_ref, vp_ref, bias_ref, sink_ref, o_ref, lse_ref):
        mask = _swa_mask(pl.program_id(0) == 0)
        lse_ref[...] = jnp.zeros_like(lse_ref)
        for kvh in range(hkv):
            cols = slice(kvh * HEAD_DIM, (kvh + 1) * HEAD_DIM)
            k2 = jnp.concatenate([kp_ref[:, cols], kc_ref[:, cols]], axis=0)
            v2 = jnp.concatenate([vp_ref[:, cols], vc_ref[:, cols]], axis=0)
            for g in range(GROUP):
                h = kvh * GROUP + g
                hc = slice(h * HEAD_DIM, (h + 1) * HEAD_DIM)
                sc = jnp.where(mask, _nt(q_ref[:, hc], k2) + bias_ref[h], NEG)
                sk = sink_ref[0, h]
                m = jnp.maximum(jnp.max(sc, axis=-1, keepdims=True), sk)
                p = jnp.exp(sc - m)
                denom = jnp.sum(p, axis=-1, keepdims=True) + jnp.exp(sk - m)
                o_ref[:, hc] = (_nn(p.astype(BF16), v2) / denom).astype(BF16)
                lse_ref[:, h:h + 1] = m + jnp.log(denom)

    prev = lambda i: jnp.maximum(i - 1, 0)
    return pl.pallas_call(
        body, name=name, grid=(nblk,),
        in_specs=[pl.BlockSpec((BLOCK, d), lambda i: (i, 0)),
                  pl.BlockSpec((BLOCK, kw), lambda i: (i, kcol)),
                  pl.BlockSpec((BLOCK, kw), lambda i: (prev(i), kcol)),
                  pl.BlockSpec((BLOCK, kw), lambda i: (i, kcol + 1)),
                  pl.BlockSpec((BLOCK, kw), lambda i: (prev(i), kcol + 1)),
                  pl.BlockSpec((hq, BLOCK, 2 * BLOCK), lambda i: (0, 0, 0)),
                  pl.BlockSpec(memory_space=pltpu.SMEM)],
        out_specs=[pl.BlockSpec((BLOCK, d), lambda i: (i, 0)), pl.BlockSpec((BLOCK, LANES), lambda i: (i, 0))],
        out_shape=[jax.ShapeDtypeStruct((s, d), BF16), jax.ShapeDtypeStruct((s, LANES), F32)],
        compiler_params=_params(1),
    )(qkn, qkn, qkn, proj, proj, bias, sink)


def swa_bwd(qkn, proj, bias, sink, do, o, lse, d, name):
    s = qkn.shape[0]
    hq = d // HEAD_DIM
    hkv = hq // GROUP
    kw = hkv * HEAD_DIM
    nblk = s // BLOCK
    kcol = d // kw
    wide = d + 2 * kw

    def body(q_ref, kc_ref, kp_ref, vc_ref, vp_ref, bias_ref, sink_ref, do_ref, o_ref, lse_ref,
             out_ref, dbias_ref, dsink_ref, carry, fresh):
        i = pl.program_id(0)

        @pl.when(i == 0)
        def _():
            dbias_ref[...] = jnp.zeros_like(dbias_ref)
            dsink_ref[...] = jnp.zeros_like(dsink_ref)
            carry[...] = jnp.zeros_like(carry)

        @pl.when(i == nblk)
        def _():
            fresh[...] = jnp.zeros_like(fresh)

        @pl.when(i < nblk)
        def _():
            mask = _swa_mask(i == 0)
            for kvh in range(hkv):
                cols = slice(kvh * HEAD_DIM, (kvh + 1) * HEAD_DIM)
                k2 = jnp.concatenate([kp_ref[:, cols], kc_ref[:, cols]], axis=0)
                v2 = jnp.concatenate([vp_ref[:, cols], vc_ref[:, cols]], axis=0)
                dk2 = jnp.zeros((2 * BLOCK, HEAD_DIM), F32)
                dv2 = jnp.zeros((2 * BLOCK, HEAD_DIM), F32)
                for g in range(GROUP):
                    h = kvh * GROUP + g
                    hc = slice(h * HEAD_DIM, (h + 1) * HEAD_DIM)
                    q = q_ref[:, hc]
                    dov = do_ref[:, hc]
                    lse_h = lse_ref[:, h:h + 1]
                    sc = jnp.where(mask, _nt(q, k2) + bias_ref[h], NEG)
                    p = jnp.exp(sc - lse_h)
                    delta = jnp.sum(dov.astype(F32) * o_ref[:, hc].astype(F32), axis=-1, keepdims=True)
                    ds = p * (_nt(dov, v2) - delta)
                    dbias_ref[h] += ds
                    dsink_ref[0:1, h:h + 1] += jnp.sum(-jnp.exp(sink_ref[0, h] - lse_h) * delta, axis=0, keepdims=True)
                    dsb = ds.astype(BF16)
                    fresh[0, :, hc] = _nn(dsb, k2)
                    dk2 += _tn(dsb, q)
                    dv2 += _tn(p.astype(BF16), dov)
                kc_cols = slice(d + kvh * HEAD_DIM, d + (kvh + 1) * HEAD_DIM)
                vc_cols = slice(d + kw + kvh * HEAD_DIM, d + kw + (kvh + 1) * HEAD_DIM)
                fresh[0, :, kc_cols] = dk2[BLOCK:]
                fresh[0, :, vc_cols] = dv2[BLOCK:]
                fresh[1, :, kc_cols] = dk2[:BLOCK]
                fresh[1, :, vc_cols] = dv2[:BLOCK]

        lane = lax.broadcasted_iota(jnp.int32, (BLOCK, wide), 1)
        out_ref[...] = carry[...] + jnp.where(lane >= d, fresh[1], 0.0)

        @pl.when(i < nblk)
        def _():
            carry[...] = fresh[0]

    cur = lambda i: jnp.minimum(i, nblk - 1)
    prev = lambda i: jnp.maximum(jnp.minimum(i, nblk - 1) - 1, 0)
    return pl.pallas_call(
        body, name=name, grid=(nblk + 1,),
        in_specs=[pl.BlockSpec((BLOCK, d), lambda i: (cur(i), 0)),
                  pl.BlockSpec((BLOCK, kw), lambda i: (cur(i), kcol)),
                  pl.BlockSpec((BLOCK, kw), lambda i: (prev(i), kcol)),
                  pl.BlockSpec((BLOCK, kw), lambda i: (cur(i), kcol + 1)),
                  pl.BlockSpec((BLOCK, kw), lambda i: (prev(i), kcol + 1)),
                  pl.BlockSpec((hq, BLOCK, 2 * BLOCK), lambda i: (0, 0, 0)),
                  pl.BlockSpec(memory_space=pltpu.SMEM),
                  pl.BlockSpec((BLOCK, d), lambda i: (cur(i), 0)),
                  pl.BlockSpec((BLOCK, d), lambda i: (cur(i), 0)),
                  pl.BlockSpec((BLOCK, LANES), lambda i: (cur(i), 0))],
        out_specs=[pl.BlockSpec((BLOCK, wide), lambda i: (jnp.maximum(i - 1, 0), 0)),
                   pl.BlockSpec((hq, BLOCK, 2 * BLOCK), lambda i: (0, 0, 0)),
                   pl.BlockSpec((8, LANES), lambda i: (0, 0))],
        out_shape=[jax.ShapeDtypeStruct((s, wide), F32), jax.ShapeDtypeStruct((hq, BLOCK, 2 * BLOCK), F32),
                   jax.ShapeDtypeStruct((8, LANES), F32)],
        scratch_shapes=[pltpu.VMEM((BLOCK, wide), F32), pltpu.VMEM((2, BLOCK, wide), F32)],
        compiler_params=_params(1),
    )(qkn, qkn, qkn, proj, proj, bias, sink, do, o, lse)


def _rel_bucket_table():
    qi = np.arange(BLOCK)[:, None] + BLOCK
    kj = np.arange(2 * BLOCK)[None, :]
    n = np.maximum(qi - kj, 0)
    max_exact = REL_BUCKETS // 2
    nf = np.maximum(n, 1).astype(np.float32)
    large = max_exact + (np.log(nf / max_exact) / math.log(REL_MAX_DIST / max_exact)
                         * (REL_BUCKETS - max_exact)).astype(np.int32)
    large = np.minimum(large, REL_BUCKETS - 1)
    return np.where(n < max_exact, n, large).astype(np.int32)


def rel_bias_table(rel_bias, bucket):
    hq = rel_bias.shape[1]

    def body(rb_ref, bucket_ref, out_ref):
        tbl = bucket_ref[...]

        def per_head(h, carry):
            def per_bucket(b, acc):
                return jnp.where(tbl == b, rb_ref[b, h], acc)

            out_ref[h] = lax.fori_loop(0, REL_BUCKETS, per_bucket, jnp.zeros(tbl.shape, F32))
            return carry

        lax.fori_loop(0, hq, per_head, 0)

    return pl.pallas_call(
        body, name="rel_bias_table",
        in_specs=[pl.BlockSpec(memory_space=pltpu.SMEM), pl.BlockSpec(memory_space=pltpu.VMEM)],
        out_specs=pl.BlockSpec(memory_space=pltpu.VMEM),
        out_shape=jax.ShapeDtypeStruct((hq,) + tuple(bucket.shape), F32),
    )(rel_bias, bucket)


def rel_bias_grad(dbias, bucket):
    n_layers, hq = dbias.shape[:2]

    def body(db_ref, bucket_ref, out_ref):
        tbl = bucket_ref[...]

        def per_head(h, carry):
            dsum = db_ref[0, h]
            for a in range(1, n_layers):
                dsum = dsum + db_ref[a, h]

            def per_bucket(b, carry2):
                out_ref[b, h] = jnp.sum(jnp.where(tbl == b, dsum, 0.0))
                return carry2

            return lax.fori_loop(0, REL_BUCKETS, per_bucket, carry)

        lax.fori_loop(0, hq, per_head, 0)

    return pl.pallas_call(
        body, name="rel_bias_grad",
        in_specs=[pl.BlockSpec(memory_space=pltpu.VMEM), pl.BlockSpec(memory_space=pltpu.VMEM)],
        out_specs=pl.BlockSpec(memory_space=pltpu.SMEM),
        out_shape=jax.ShapeDtypeStruct((REL_BUCKETS, hq), F32),
    )(dbias, bucket)


def _split3(v):
    hi = v.astype(BF16)
    r1 = v - hi.astype(F32)
    mid = r1.astype(BF16)
    lo = (r1 - mid.astype(F32)).astype(BF16)
    return hi, mid, lo


def _tri_sum(tri, v):
    hi, mid, lo = _split3(v)
    return _nn(tri, hi) + _nn(tri, mid) + _nn(tri, lo)


def fox_gates(fl, b_f, name):
    s = fl.shape[0]
    t = _row_tile(s)

    def body(fl_ref, b_ref, f_ref, carry):
        @pl.when(pl.program_id(0) == 0)
        def _():
            carry[...] = jnp.zeros_like(carry)

        z = fl_ref[...] + b_ref[...]
        logf = jnp.minimum(z, 0.0) - jnp.log(1.0 + jnp.exp(-jnp.abs(z)))
        r = lax.broadcasted_iota(jnp.int32, (t, t), 0)
        cidx = lax.broadcasted_iota(jnp.int32, (t, t), 1)
        tri = jnp.where(cidx <= r, 1.0, 0.0).astype(BF16)
        f = _tri_sum(tri, logf) + carry[0:1, :]
        f_ref[...] = f
        carry[0:1, :] = f_ref[t - 1:t, :]

    blk = pl.BlockSpec((t, LANES), lambda i: (i, 0))
    return pl.pallas_call(
        body, name=name, grid=(s // t,),
        in_specs=[blk, pl.BlockSpec((1, LANES), lambda i: (0, 0))],
        out_specs=blk, out_shape=jax.ShapeDtypeStruct((s, LANES), F32),
        scratch_shapes=[pltpu.VMEM((8, LANES), F32)],
        compiler_params=_params(1),
    )(fl, b_f)


def fox_gates_bwd(fl, b_f, df_query, df_key, name):
    s = fl.shape[0]
    t = _row_tile(s)
    nb = s // t

    def body(fl_ref, b_ref, dfq_ref, dfk_ref, dfl_ref, db_ref, carry):
        @pl.when(pl.program_id(0) == 0)
        def _():
            carry[...] = jnp.zeros_like(carry)
            db_ref[...] = jnp.zeros_like(db_ref)

        dfv = dfq_ref[...] + dfk_ref[...]
        r = lax.broadcasted_iota(jnp.int32, (t, t), 0)
        cidx = lax.broadcasted_iota(jnp.int32, (t, t), 1)
        tri = jnp.where(cidx >= r, 1.0, 0.0).astype(BF16)
        dlog = _tri_sum(tri, dfv) + carry[0:1, :]
        carry[0:1, :] += jnp.sum(dfv, axis=0, keepdims=True)
        z = fl_ref[...] + b_ref[...]
        dz = dlog * (1.0 - _sigmoid(z))
        dfl_ref[...] = dz.astype(BF16)
        db_ref[0:1, :] += jnp.sum(dz, axis=0, keepdims=True)

    rev = pl.BlockSpec((t, LANES), lambda i: (nb - 1 - i, 0))
    return pl.pallas_call(
        body, name=name, grid=(nb,),
        in_specs=[rev, pl.BlockSpec((1, LANES), lambda i: (0, 0)), rev, rev],
        out_specs=[rev, pl.BlockSpec((8, LANES), lambda i: (0, 0))],
        out_shape=[jax.ShapeDtypeStruct((s, LANES), BF16), jax.ShapeDtypeStruct((8, LANES), F32)],
        scratch_shapes=[pltpu.VMEM((8, LANES), F32)],
        compiler_params=_params(1),
    )(fl, b_f, df_query, df_key)


def fox_fwd(qkn, proj, f_col, f_row, d, name):
    s = qkn.shape[0]
    t = _attn_tile(s)
    hs, wide = FOX_HEADS, FOX_HEADS * HEAD_DIM
    n_pairs = d // wide
    nt = s // t

    def body(q_ref, k_ref, v_ref, fq_ref, fk_ref, o_ref, o32_ref, lse_ref, m_scr, l_scr, acc):
        i, j = pl.program_id(1), pl.program_id(2)

        @pl.when(j == 0)
        def _():
            m_scr[...] = jnp.full_like(m_scr, NEG)
            l_scr[...] = jnp.zeros_like(l_scr)
            acc[...] = jnp.zeros_like(acc)

        @pl.when(j <= i)
        def _():
            krow = lax.broadcasted_iota(jnp.int32, (t, t), 0)
            qcol = lax.broadcasted_iota(jnp.int32, (t, t), 1)
            visible = (krow <= qcol) | (j < i)
            for hh in range(hs):
                hc = slice(hh * HEAD_DIM, (hh + 1) * HEAD_DIM)
                st = _nt(k_ref[:, hc], q_ref[:, hc]) + fq_ref[hh] - fk_ref[hh]
                st = jnp.where(visible, st, NEG)
                m_prev = m_scr[hh]
                m_new = jnp.maximum(m_prev, jnp.max(st, axis=0, keepdims=True))
                alpha = jnp.exp(m_prev - m_new)
                pt = jnp.exp(st - m_new)
                l_scr[hh] = alpha * l_scr[hh] + jnp.sum(pt, axis=0, keepdims=True)
                acc[hc, :] = alpha * acc[hc, :] + _tn(v_ref[:, hc], pt.astype(BF16))
                m_scr[hh] = m_new

        @pl.when(j == i)
        def _():
            l_full = jnp.concatenate([jnp.broadcast_to(l_scr[hh], (HEAD_DIM, t)) for hh in range(hs)], axis=0)
            ov = (acc[...] / l_full).T
            o_ref[...] = ov.astype(BF16)
            o32_ref[...] = ov
            lse_ref[...] = m_scr[...] + jnp.log(l_scr[...])

    kv = lambda j, i: jnp.minimum(j, i)
    return pl.pallas_call(
        body, name=name, grid=(n_pairs, nt, nt),
        in_specs=[pl.BlockSpec((t, wide), lambda p, i, j: (i, p)),
                  pl.BlockSpec((t, wide), lambda p, i, j: (kv(j, i), n_pairs + p)),
                  pl.BlockSpec((t, wide), lambda p, i, j: (kv(j, i), 2 * n_pairs + p)),
                  pl.BlockSpec((hs, 1, t), lambda p, i, j: (p, 0, i)),
                  pl.BlockSpec((hs, t, 1), lambda p, i, j: (p, kv(j, i), 0))],
        out_specs=[pl.BlockSpec((t, wide), lambda p, i, j: (i, p)),
                   pl.BlockSpec((t, wide), lambda p, i, j: (i, p)),
                   pl.BlockSpec((hs, 1, t), lambda p, i, j: (p, 0, i))],
        out_shape=[jax.ShapeDtypeStruct((s, d), BF16), jax.ShapeDtypeStruct((s, d), F32),
                   jax.ShapeDtypeStruct((hs * n_pairs, 1, s), F32)],
        scratch_shapes=[pltpu.VMEM((hs, 1, t), F32), pltpu.VMEM((hs, 1, t), F32), pltpu.VMEM((wide, t), F32)],
        compiler_params=_params(3),
    )(qkn, qkn, proj, f_row, f_col)


def fox_bwd(qkn, proj, f_col, f_row, lse_row, do, o, d, name):
    s = qkn.shape[0]
    t = _attn_tile(s)
    hs, wide = FOX_HEADS, FOX_HEADS * HEAD_DIM
    n_pairs = d // wide
    nt = s // t

    def body(q_ref, k_ref, v_ref, fk_ref, fq_ref, lse_ref, do_ref, o_ref, out_ref, df_ref, dfq_ref,
             dq_acc, dkv_acc, df_acc, dfq_acc):
        j, i = pl.program_id(1), pl.program_id(2)

        @pl.when((j == 0) & (i == 0))
        def _():
            dq_acc[...] = jnp.zeros_like(dq_acc)
            dfq_acc[...] = jnp.zeros_like(dfq_acc)

        @pl.when(i == 0)
        def _():
            dkv_acc[...] = jnp.zeros_like(dkv_acc)
            df_acc[...] = jnp.zeros_like(df_acc)

        @pl.when(i >= j)
        def _():
            krow = lax.broadcasted_iota(jnp.int32, (t, t), 0)
            qcol = lax.broadcasted_iota(jnp.int32, (t, t), 1)
            visible = (krow <= qcol) | (i > j)
            ones = jnp.ones((8, HEAD_DIM), BF16)
            for hh in range(hs):
                hc = slice(hh * HEAD_DIM, (hh + 1) * HEAD_DIM)
                q, k, v, dov = q_ref[:, hc], k_ref[:, hc], v_ref[:, hc], do_ref[:, hc]
                st = _nt(k, q) + fq_ref[hh] - fk_ref[hh]
                pt = jnp.exp(jnp.where(visible, st, NEG) - lse_ref[hh])
                hi, mid, lo = _split3(dov.astype(F32) * o_ref[:, hc])
                delta = jnp.max(_nt(ones, hi) + _nt(ones, mid) + _nt(ones, lo), axis=0, keepdims=True)
                dst = pt * (_nt(v, dov) - delta)
                dsb = dst.astype(BF16)
                dkv_acc[1, :, hc] += _nn(pt.astype(BF16), dov)
                dkv_acc[0, :, hc] += _nn(dsb, q)
                dq_acc[pl.ds(pl.multiple_of(i * t, t), t), hc] += _tn(dsb, k)
                df_acc[hh] -= jnp.sum(dst, axis=-1, keepdims=True)
                dfq_acc[i, hh] += jnp.sum(dst, axis=0, keepdims=True)

        @pl.when(i == nt - 1)
        def _():
            out_ref[0] = dq_acc[pl.ds(pl.multiple_of(j * t, t), t), :]
            out_ref[1] = dkv_acc[0]
            out_ref[2] = dkv_acc[1]
            df_ref[...] = df_acc[...]
            dfq_ref[...] = dfq_acc[j]

    qi = lambda j, i: jnp.maximum(i, j)
    return pl.pallas_call(
        body, name=name, grid=(n_pairs, nt, nt),
        in_specs=[pl.BlockSpec((t, wide), lambda p, j, i: (qi(j, i), p)),
                  pl.BlockSpec((t, wide), lambda p, j, i: (j, n_pairs + p)),
                  pl.BlockSpec((t, wide), lambda p, j, i: (j, 2 * n_pairs + p)),
                  pl.BlockSpec((hs, t, 1), lambda p, j, i: (p, j, 0)),
                  pl.BlockSpec((hs, 1, t), lambda p, j, i: (p, 0, qi(j, i))),
                  pl.BlockSpec((hs, 1, t), lambda p, j, i: (p, 0, qi(j, i))),
                  pl.BlockSpec((t, wide), lambda p, j, i: (qi(j, i), p)),
                  pl.BlockSpec((t, wide), lambda p, j, i: (qi(j, i), p))],
        out_specs=[pl.BlockSpec((3, t, wide), lambda p, j, i: (0, j, p)),
                   pl.BlockSpec((hs, t, 1), lambda p, j, i: (p, j, 0)),
                   pl.BlockSpec((hs, 1, t), lambda p, j, i: (p, 0, j))],
        out_shape=[jax.ShapeDtypeStruct((3, s, d), F32), jax.ShapeDtypeStruct((hs * n_pairs, s, 1), F32),
                   jax.ShapeDtypeStruct((hs * n_pairs, 1, s), F32)],
        scratch_shapes=[pltpu.VMEM((s, wide), F32), pltpu.VMEM((2, t, wide), F32), pltpu.VMEM((hs, t, 1), F32),
                        pltpu.VMEM((nt, hs, 1, t), F32)],
        compiler_params=_params(3),
    )(qkn, qkn, proj, f_col, f_row, lse_row, do, o)


def loss_head(y, target):
    s, d = y.shape
    tm = _row_tile(s)

    def body(y_ref, t_ref, dy_ref, loss_ref):
        @pl.when(pl.program_id(0) == 0)
        def _():
            loss_ref[...] = jnp.zeros_like(loss_ref)

        diff = y_ref[...] - t_ref[...]
        dy_ref[...] = diff * (1.0 / d)
        loss_ref[...] += 0.5 * jnp.sum(jnp.mean(diff * diff, axis=-1, keepdims=True), axis=0, keepdims=True)

    row = pl.BlockSpec((tm, d), lambda i: (i, 0))
    return pl.pallas_call(
        body, name="loss_head", grid=(s // tm,),
        in_specs=[row, row],
        out_specs=[row, pl.BlockSpec((8, LANES), lambda i: (0, 0))],
        out_shape=[jax.ShapeDtypeStruct((s, d), F32), jax.ShapeDtypeStruct((8, LANES), F32)],
        compiler_params=_params(1),
    )(y, target)


def ada_mod(c_all, w, b):
    n_layers, d, cols = w.shape

    def body(c_ref, w_ref, b_ref, o_ref):
        cv = c_ref[...]
        o_ref[...] = _nn(cv * _sigmoid(cv), w_ref[...]) + b_ref[...]

    return pl.pallas_call(
        body, name="ada_mod", grid=(n_layers,),
        in_specs=[pl.BlockSpec((N_DEV, d), lambda l: (0, 0)), pl.BlockSpec((None, d, cols), lambda l: (l, 0, 0)),
                  pl.BlockSpec((None, 1, cols), lambda l: (l, 0, 0))],
        out_specs=pl.BlockSpec((None, N_DEV, cols), lambda l: (l, 0, 0)),
        out_shape=jax.ShapeDtypeStruct((n_layers, N_DEV, cols), F32),
        compiler_params=_params(1),
    )(c_all, w, b)


def ada_grad(c_t, dmod):
    d = c_t.shape[0]
    n_layers, _, cols = dmod.shape
    tn = cols // 2

    def body(c_ref, dm_ref, o_ref):
        cv = c_ref[...]
        o_ref[...] = _nn(cv * _sigmoid(cv), dm_ref[...])

    return pl.pallas_call(
        body, name="ada_grad", grid=(n_layers, 2),
        in_specs=[pl.BlockSpec((d, N_DEV), lambda l, n: (0, 0)), pl.BlockSpec((None, N_DEV, tn), lambda l, n: (l, 0, n))],
        out_specs=pl.BlockSpec((None, d, tn), lambda l, n: (l, 0, n)),
        out_shape=jax.ShapeDtypeStruct((n_layers, d, cols), F32),
        compiler_params=_params(2),
    )(c_t, dmod)


def sum_devices(v):
    def body(v_ref, o_ref):
        acc = v_ref[0]
        for k in range(1, N_DEV):
            acc = acc + v_ref[k]
        o_ref[...] = acc

    return pl.pallas_call(body, name="sum_devices", out_shape=jax.ShapeDtypeStruct(v.shape[1:], F32))(v)


def sum_slots(r, own):
    _, rows, cols = r.shape
    tm = 256 if rows % 256 == 0 else rows

    def body(r_ref, own_ref, o_ref):
        o_ref[...] = ((own_ref[...].astype(F32) + r_ref[0].astype(F32)) + r_ref[1].astype(F32)) + r_ref[2].astype(F32)

    return pl.pallas_call(
        body, name="sum_slots", grid=(rows // tm,),
        in_specs=[pl.BlockSpec((N_CHIP - 1, tm, cols), lambda i: (0, i, 0)), pl.BlockSpec((tm, cols), lambda i: (i, 0))],
        out_specs=pl.BlockSpec((tm, cols), lambda i: (i, 0)),
        out_shape=jax.ShapeDtypeStruct((rows, cols), F32),
        compiler_params=_params(1),
    )(r, own)


def adamw(w, m, v, g, g2=None):
    rows, cols = w.shape
    tm = 256 if rows % 256 == 0 else rows
    two = g2 is not None
    c1 = 1.0 - ADAM_B1 ** ADAM_STEP
    c2 = 1.0 - ADAM_B2 ** ADAM_STEP

    def body(w_ref, m_ref, v_ref, g_ref, *rest):
        if two:
            g2_ref, go_ref, d_ref, mo_ref, vo_ref = rest
            gv = g_ref[...] + g2_ref[...]
        else:
            go_ref, d_ref, mo_ref, vo_ref = rest
            gv = g_ref[...]
        mn = ADAM_B1 * m_ref[...] + (1.0 - ADAM_B1) * gv
        vn = ADAM_B2 * v_ref[...] + (1.0 - ADAM_B2) * (gv * gv)
        go_ref[...] = gv
        mo_ref[...] = mn
        vo_ref[...] = vn
        d_ref[...] = -ADAM_LR * ((mn / c1) / (jnp.sqrt(vn / c2) + ADAM_EPS) + ADAM_WD * w_ref[...])

    blk = pl.BlockSpec((tm, cols), lambda i: (i, 0))
    ops = [w, m, v, g] + ([g2] if two else [])
    return pl.pallas_call(
        body, name="adamw", grid=(rows // tm,),
        in_specs=[blk] * len(ops), out_specs=[blk] * 4,
        out_shape=[jax.ShapeDtypeStruct((rows, cols), F32)] * 4,
        compiler_params=_params(1),
    )(*ops)


def _pad_rows(flat):
    n = flat.shape[0]
    rows = -(-n // LANES)
    return jnp.pad(flat, (0, rows * LANES - n)).reshape(rows, LANES)


def _pad_rows8(flat):
    rows = _pad_rows(flat)
    return jnp.pad(rows, ((0, -rows.shape[0] % 8), (0, 0)))


def _col_tiles(n):
    return next(k for k in range(1, n // LANES + 1) if n % (k * LANES) == 0 and n // k <= 1536)


def kernel(x, c, ada_w, ada_b, norm_g, ffn_w13, ffn_w2, rel_bias, swa_w_in, swa_w_out, swa_q_g, swa_k_g, swa_sink, fox_w_in, fox_w_out, fox_b_f, fox_q_g, fox_k_g, loss_target, m_ada_w, m_ada_b, m_norm_g, m_ffn_w13, m_ffn_w2, m_rel_bias, m_swa_w_in, m_swa_w_out, m_swa_q_g, m_swa_k_g, m_swa_sink, m_fox_w_in, m_fox_w_out, m_fox_b_f, m_fox_q_g, m_fox_k_g, v_ada_w, v_ada_b, v_norm_g, v_ffn_w13, v_ffn_w2, v_rel_bias, v_swa_w_in, v_swa_w_out, v_swa_q_g, v_swa_k_g, v_swa_sink, v_fox_w_in, v_fox_w_out, v_fox_b_f, v_fox_q_g, v_fox_k_g):
    ix, iy, ic = lax.axis_index("x"), lax.axis_index("y"), lax.axis_index("c")
    chip = 2 * ix + iy
    dev = 2 * chip + ic
    s, d = x.shape[1:]
    n_layers = ada_w.shape[0]
    n_a, n_b = swa_w_in.shape[0], fox_w_in.shape[0]
    hq = d // HEAD_DIM
    hkv = hq // GROUP
    kw = hkv * HEAD_DIM
    c13 = ffn_w13.shape[-1]
    f = 2 * c13
    r2 = ffn_w2.shape[2]
    cq = norm_g.shape[-1]
    a_in = d + 2 * kw
    fx = fox_w_in.shape[-1]
    b_in = N_CHIP * fx
    b_pad = 3 * d + LANES
    x0 = x[0]

    hello = _pad_rows8(jnp.concatenate([c.reshape(-1), norm_g.reshape(-1)]))
    hello_all = all_gather_rows(hello, "gather_c_norm").reshape(N_DEV, -1)
    c_all = hello_all[:, :d]
    ng = hello_all[::2, d:d + n_layers * 3 * cq].reshape(N_CHIP, n_layers, 3, cq)
    norm_full = jnp.moveaxis(ng, 0, 2).reshape(n_layers, 3, d)

    half_cols = ada_w.shape[-1] // 2
    w_half = lax.dynamic_slice_in_dim(ada_w, ic * half_cols, half_cols, axis=2)
    b_half = lax.dynamic_slice_in_dim(ada_b, dev * half_cols, half_cols, axis=1)[:, None, :]
    mod_part = ada_mod(c_all, w_half, b_half)
    mod_all = all_gather_rows(mod_part.reshape(n_layers * N_DEV, half_cols), "gather_mod")
    mod_all = mod_all.reshape(N_DEV, n_layers, N_DEV, half_cols)
    mod_mine = lax.dynamic_index_in_dim(mod_all, dev, axis=2, keepdims=False)
    mod_mine = jnp.moveaxis(mod_mine, 0, 1).reshape(n_layers, 3, 3, d)
    mv = jnp.concatenate([norm_full[:, :, None, :], mod_mine, jnp.zeros((n_layers, 3, 4, d), F32)], axis=2)

    n_groups = n_layers // 2
    kinds = [("col", c13), ("row", r2), ("col", swa_w_in.shape[-1]), ("slot",), ("slot",), ("slot",)]

    def group_buffers(g, after):
        raw = [ffn_w13[2 * g:2 * g + 2], ffn_w2[2 * g:2 * g + 2], swa_w_in[g], swa_w_out[g], fox_w_in[g], fox_w_out[g]]
        return [cast_and_place(r, k, chip, after, f"place_weights_{g}_{t}") for t, (r, k) in enumerate(zip(raw, kinds))]

    def group_weights(fulls):
        w13_g, w2_g, wa_in_g, wa_out_g, wb_slots, wb_out_g = fulls
        wb_in_g = jnp.pad(jnp.concatenate([wb_slots[b] for b in range(N_CHIP)], axis=-1), ((0, 0), (0, b_pad - b_in)))
        return dict(w13=w13_g, w2=w2_g, wa_in=wa_in_g, wa_out=wa_out_g.reshape(d, d), wb_in=wb_in_g,
                    wb_out=wb_out_g.reshape(d, d))

    weights = [group_weights(gather_weights(group_buffers(0, mv), kinds))]
    in_flight = []
    for g in range(1, n_groups):
        in_flight.append(exchange_start([], group_buffers(g, weights[0]["w13"]), kinds, True, f"gather_start_{g}"))
    for *_, token in in_flight:
        mv = mv + token[0, 0]

    bucket = jnp.asarray(_rel_bucket_table())
    bias = rel_bias_table(rel_bias, bucket)
    tm, tw = _row_tile(s), _wide_tile(s)
    n13 = 2 * f // c13
    na_t, nb_t = _col_tiles(a_in), _col_tiles(3 * d)
    wa_t, wb_t = a_in // na_t, 3 * d // nb_t
    gate_blk = 3 * d // LANES

    def ffn_forward(xv, l, half, sub):
        wg, li = weights[l // 2], l % 2
        a, h = modmm(xv, mv, (l, sub), wg["w13"], (None, None, d, 2 * f), (li, half, 0, 0), 2 * f, c13, BF16,
                     f"ffn_up_{l}_{half}", True)
        xo, y, u = resmm(xv, mv, (l, sub), 0.5, a, wg["w2"], (None, None, f, d), (li, half, 0, 0), f, c13,
                         f"ffn_down_{l}_{half}", True)
        return xo, dict(x=xv, h=h, a=a, u=u, y=y)

    saved = []
    xv = x0
    for l in range(n_layers):
        j = l // 2
        if l % 2 == 0 and j >= 1:
            send_sems, recv_sems, shards, fulls, _ = in_flight[j - 1]
            weights.append(group_weights(exchange_wait(send_sems, recv_sems, shards, fulls, xv, kinds, True,
                                                       f"gather_wait_{j}")[1]))
        wg = weights[j]
        xv, s0 = ffn_forward(xv, l, 0, 0)
        if l % 2 == 0:
            proj, h = modmm(xv, mv, (l, 1), wg["wa_in"], (d, a_in), (0, 0), a_in, wa_t, BF16, f"swa_in_{j}", True)
            gains = jnp.concatenate([jnp.tile(swa_q_g[j] * HEAD_DIM ** -0.5, hq), jnp.tile(swa_k_g[j], hkv)])[None, :]
            qkn = qknorm_fwd(proj, gains, kw, f"swa_qknorm_{j}")
            sink = swa_sink[j][None, :]
            o, lse = swa_fwd(qkn, proj, bias, sink, d, f"swa_attn_{j}")
            s1 = dict(x=xv, h=h, proj=proj, gains=gains, qkn=qkn, sink=sink, o=o, lse=lse)
            w_out = wg["wa_out"]
        else:
            proj, h = modmm(xv, mv, (l, 1), wg["wb_in"], (d, 3 * d), (0, 0), 3 * d, wb_t, BF16, f"fox_in_{j}", True)
            fl, _ = modmm(xv, mv, (l, 1), wg["wb_in"], (d, LANES), (0, gate_blk), LANES, LANES, F32,
                          f"fox_gate_in_{j}", False)
            b_f = jnp.pad(fox_b_f[j], (0, LANES - hq))[None, :]
            fcum = fox_gates(fl, b_f, f"fox_gates_{j}")
            f_t = fcum[:, :hq].T
            f_col, f_row = f_t[:, :, None], f_t[:, None, :]
            gains = jnp.concatenate([jnp.tile(fox_q_g[j] * HEAD_DIM ** -0.5, hq), jnp.tile(fox_k_g[j], hq)])[None, :]
            qkn = qknorm_fwd(proj, gains, d, f"fox_qknorm_{j}")
            o, o32, lse = fox_fwd(qkn, proj, f_col, f_row, d, f"fox_attn_{j}")
            s1 = dict(x=xv, h=h, proj=proj, gains=gains, qkn=qkn, fl=fl, b_f=b_f, f_col=f_col, f_row=f_row, o=o, o32=o32,
                      lse=lse)
            w_out = wg["wb_out"]
        xv, y = resmm(xv, mv, (l, 1), 1.0, o, w_out, (d, d), (0, 0), d, d, f"mixer_out_{l}", False)
        s1["y"] = y
        xv, s2 = ffn_forward(xv, l, 1, 2)
        saved.append((s0, s1, s2))

    dxv, loss_part = loss_head(xv, loss_target[0])
    loss = lax.psum(loss_part[0, 0], ("x", "y", "c"))

    grads = [dict(w13=lax.empty((2, 2, d, 2 * f), BF16), w2=lax.empty((2, 2, f, d), BF16),
                  wa_in=lax.empty((d, a_in), BF16), wa_out=lax.empty((d, d), BF16),
                  wb_in=lax.empty((d, b_pad), BF16), wb_out=lax.empty((d, d), BF16)) for _ in range(n_groups)]
    dmod = [[None] * 3 for _ in range(n_layers)]
    dnorm = [[None] * 3 for _ in range(n_layers)]
    dqk_gain = {}
    dsink, db_f, dbias_tabs = {}, {}, []

    def ffn_backward(dxo, sv, l, half, sub):
        wg, gg, li = weights[l // 2], grads[l // 2], l % 2
        dy, da, dgate = resmm_bwd(dxo, sv["y"], mv, (l, sub), 0.5, wg["w2"], (None, None, f, d), (li, half, 0, 0), f, c13,
                                  f"ffn_down_bwd_{l}_{half}", a=sv["a"])
        gg["w2"] = weight_grad(sv["u"], pl.BlockSpec((tw, c13), lambda m, n, k: (k, m)), dy,
                               pl.BlockSpec((tw, d), lambda m, n, k: (k, 0)), (f // c13, 1), s, c13, d, gg["w2"],
                               pl.BlockSpec((None, None, c13, d), lambda m, n, k: (li, half, m, 0)),
                               f"ffn_w2_grad_{l}_{half}")
        gg["w13"] = weight_grad(sv["h"], pl.BlockSpec((tw, d), lambda m, n, k: (k, 0)), da,
                                pl.BlockSpec((tw, c13), lambda m, n, k: (k, n)), (1, n13), s, d, c13,
                                gg["w13"], pl.BlockSpec((None, None, d, c13), lambda m, n, k: (li, half, 0, n)),
                                f"ffn_w13_grad_{l}_{half}")
        dx, red = modmm_bwd(da, wg["w13"], (None, None, d, 2 * f), (li, half, 0, 0), 2 * f, c13, sv["x"], dxo, mv,
                            (l, sub), f"ffn_up_bwd_{l}_{half}")
        dmod[l][sub] = (red[1], red[2], dgate[0])
        dnorm[l][sub] = red[0]
        return dx

    def group_slabs(gg):
        gb_slots = jnp.stack([gg["wb_in"][:, b * fx:(b + 1) * fx] for b in range(N_CHIP)])
        return [gg["w13"], gg["w2"], gg["wa_in"], gg["wa_out"].reshape(N_CHIP, d // N_CHIP, d), gb_slots,
                gg["wb_out"].reshape(N_CHIP, d // N_CHIP, d)]

    shard_shapes = [(2,) + ffn_w13.shape[1:], (2,) + ffn_w2.shape[1:], swa_w_in.shape[1:], swa_w_out.shape[1:],
                    fox_w_in.shape[1:], fox_w_out.shape[1:]]
    scatter_in_flight = {}

    for l in reversed(range(n_layers)):
        j = l // 2
        wg, gg = weights[j], grads[j]
        s0, s1, s2 = saved[l]
        dxv = ffn_backward(dxv, s2, l, 1, 2)
        is_a = l % 2 == 0
        w_out = wg["wa_out"] if is_a else wg["wb_out"]
        dy, do, dgate = resmm_bwd(dxv, s1["y"], mv, (l, 1), 1.0, w_out, (d, d), (0, 0), d, d, f"mixer_out_bwd_{l}")
        out_key = "wa_out" if is_a else "wb_out"
        gg[out_key] = weight_grad(s1["o"], pl.BlockSpec((tw, d), lambda m, n, k: (k, 0)), dy,
                                  pl.BlockSpec((tw, d), lambda m, n, k: (k, 0)), (1, 1), s, d, d, gg[out_key],
                                  pl.BlockSpec((d, d), lambda m, n, k: (0, 0)), f"mixer_out_grad_{l}")
        if is_a:
            d_qkv, dbias_tab, dsk = swa_bwd(s1["qkn"], s1["proj"], bias, s1["sink"], do, s1["o"], s1["lse"], d,
                                            f"swa_attn_bwd_{j}")
            dbias_tabs.append(dbias_tab)
            dsink[j] = dsk[0, :hq]
            dproj, dgain = qknorm_bwd(s1["proj"], s1["gains"], d_qkv, pl.BlockSpec((tm, kw), lambda c_, i: (i, c_)),
                                      a_in, kw, f"swa_qknorm_bwd_{j}")
            gg["wa_in"] = weight_grad(s1["h"], pl.BlockSpec((tw, d), lambda m, n, k: (k, 0)), dproj,
                                      pl.BlockSpec((tw, wa_t), lambda m, n, k: (k, n)), (1, na_t), s, d, wa_t, gg["wa_in"],
                                      pl.BlockSpec((d, wa_t), lambda m, n, k: (0, n)), f"swa_in_grad_{j}")
            dxv, red = modmm_bwd(dproj, wg["wa_in"], (d, a_in), (0, 0), a_in, wa_t, s1["x"], dxv, mv, (l, 1),
                                 f"swa_in_bwd_{j}")
            dqk_gain[("a", j)] = (dgain[0, :d].reshape(hq, HEAD_DIM).sum(0) * HEAD_DIM ** -0.5,
                                  dgain[0, d:d + kw].reshape(hkv, HEAD_DIM).sum(0))
        else:
            lse_row = s1["lse"].reshape(hq, 1, s)
            d_qkv, df_col, dfq_row = fox_bwd(s1["qkn"], s1["proj"], s1["f_col"], s1["f_row"], lse_row, do, s1["o32"], d,
                                             f"fox_attn_bwd_{j}")
            lanes_of_heads = lambda a: jnp.pad(a.T, ((0, 0), (0, LANES - hq)))
            dfl, dbf = fox_gates_bwd(s1["fl"], s1["b_f"], lanes_of_heads(dfq_row[:, 0, :]), lanes_of_heads(df_col[:, :, 0]),
                                     f"fox_gates_bwd_{j}")
            db_f[j] = dbf[0, :hq]
            dproj, dgain = qknorm_bwd(s1["proj"], s1["gains"], d_qkv, pl.BlockSpec((None, tm, d), lambda c_, i: (c_, i, 0)),
                                      3 * d, d, f"fox_qknorm_bwd_{j}")
            gg["wb_in"] = weight_grad(s1["h"], pl.BlockSpec((tw, d), lambda m, n, k: (k, 0)), dproj,
                                      pl.BlockSpec((tw, wb_t), lambda m, n, k: (k, n)), (1, nb_t), s, d, wb_t, gg["wb_in"],
                                      pl.BlockSpec((d, wb_t), lambda m, n, k: (0, n)), f"fox_in_grad_{j}")
            gg["wb_in"] = weight_grad(s1["h"], pl.BlockSpec((tw, d), lambda m, n, k: (k, 0)), dfl,
                                      pl.BlockSpec((tw, LANES), lambda m, n, k: (k, 0)), (1, 1), s, d, LANES, gg["wb_in"],
                                      pl.BlockSpec((d, LANES), lambda m, n, k: (0, gate_blk)), f"fox_gate_in_grad_{j}")
            dxv, red = modmm_bwd(dproj, wg["wb_in"], (d, 3 * d), (0, 0), 3 * d, wb_t, s1["x"], dxv, mv, (l, 1),
                                 f"fox_in_bwd_{j}", more=(dfl, (d, LANES), (0, gate_blk)))
            dqk_gain[("b", j)] = (dgain[0, :d].reshape(hq, HEAD_DIM).sum(0) * HEAD_DIM ** -0.5,
                                  dgain[0, d:2 * d].reshape(hq, HEAD_DIM).sum(0))
        dmod[l][1] = (red[1], red[2], dgate[0])
        dnorm[l][1] = red[0]
        dxv = ffn_backward(dxv, s0, l, 0, 0)
        if l % 2 == 0 and j >= 1:
            slabs = group_slabs(gg)
            lands = [lax.empty((N_CHIP - 1,) + tuple(shp), BF16) for shp in shard_shapes]
            scatter_in_flight[j] = exchange_start(slabs, lands, kinds, False, f"scatter_start_{j}")
            mv = mv + scatter_in_flight[j][-1][0, 0]
    grad_x = dxv[None]

    drel = rel_bias_grad(jnp.stack(dbias_tabs), bucket)
    dmod_flat = jnp.stack([jnp.stack([jnp.stack(dmod[l][sub]) for sub in range(3)]) for l in range(n_layers)]).reshape(-1)
    dnorm_flat = jnp.stack([jnp.stack(dnorm[l]) for l in range(n_layers)]).reshape(-1)
    pieces = [dmod_flat, dnorm_flat,
              jnp.stack([dqk_gain[("a", j)][0] for j in range(n_a)]).reshape(-1),
              jnp.stack([dqk_gain[("a", j)][1] for j in range(n_a)]).reshape(-1),
              jnp.stack([dqk_gain[("b", j)][0] for j in range(n_b)]).reshape(-1),
              jnp.stack([dqk_gain[("b", j)][1] for j in range(n_b)]).reshape(-1),
              jnp.stack([dsink[j] for j in range(n_a)]).reshape(-1),
              jnp.stack([db_f[j] for j in range(n_b)]).reshape(-1),
              drel.reshape(-1)]
    rows = [_pad_rows(p) for p in pieces]
    starts = np.cumsum([0] + [r.shape[0] for r in rows])
    total = -(-int(starts[-1]) // 8) * 8
    small = jnp.pad(jnp.concatenate(rows), ((0, total - int(starts[-1])), (0, 0)))
    small_all = all_gather_rows(small, "gather_small_grads").reshape(N_DEV, total, LANES)
    small_sum = sum_devices(small_all)

    def piece(k, shape):
        n = int(np.prod(shape))
        return small_sum[int(starts[k]):int(starts[k + 1])].reshape(-1)[:n].reshape(shape)

    g_ada_b = piece(0, (n_layers, 9 * d))
    g_norm = lax.dynamic_slice_in_dim(piece(1, (n_layers, 3, d)), chip * cq, cq, axis=2)
    g_swa_q, g_swa_k = piece(2, (n_a, HEAD_DIM)), piece(3, (n_a, HEAD_DIM))
    g_fox_q, g_fox_k = piece(4, (n_b, HEAD_DIM)), piece(5, (n_b, HEAD_DIM))
    g_sink, g_bf, g_rel = piece(6, (n_a, hq)), piece(7, (n_b, hq)), piece(8, (REL_BUCKETS, hq))

    dmod_all = small_all[:, :int(starts[1])].reshape(N_DEV, -1)[:, :n_layers * 9 * d].reshape(N_DEV, n_layers, 9 * d)
    ada_cols = ada_w.shape[-1]
    dmod_mine = lax.dynamic_slice_in_dim(jnp.moveaxis(dmod_all, 0, 1), chip * ada_cols, ada_cols, axis=2)
    g_ada_w = ada_grad(c_all.T, dmod_mine)

    sources = {0: group_slabs(grads[0])}
    landed = {0: scatter_grads(sources[0], kinds, shard_shapes)}
    for g, (send_sems, recv_sems, slabs, lands, _) in scatter_in_flight.items():
        sources[g], landed[g] = exchange_wait(send_sems, recv_sems, slabs, lands, landed[0][0], kinds, False,
                                              f"scatter_wait_{g}")

    def group_sum(g, t):
        cols = shard_shapes[t][-1]
        own = own_slab(sources[g][t], kinds[t], chip, shard_shapes[t])
        return sum_slots(landed[g][t].reshape(N_CHIP - 1, -1, cols), own.reshape(-1, cols))

    parts = [jnp.concatenate([group_sum(g, t) for g in range(n_groups)]) for t in range(len(kinds))]
    others = swap_with_sibling(parts)

    def update(w, m, v, g, g2=None):
        w2d = w.reshape(-1, w.shape[-1])
        outs = adamw(w2d, m.reshape(w2d.shape), v.reshape(w2d.shape), g.reshape(w2d.shape) if g2 is None else g, g2)
        return [t.reshape(w.shape) for t in outs]

    big = [(ffn_w13, m_ffn_w13, v_ffn_w13), (ffn_w2, m_ffn_w2, v_ffn_w2), (swa_w_in, m_swa_w_in, v_swa_w_in),
           (swa_w_out, m_swa_w_out, v_swa_w_out), (fox_w_in, m_fox_w_in, v_fox_w_in), (fox_w_out, m_fox_w_out, v_fox_w_out)]
    big_out = [update(w, m, v, p, q) for (w, m, v), p, q in zip(big, parts, others)]
    r_ada_w = update(ada_w, m_ada_w, v_ada_w, g_ada_w)
    r_ada_b = update(ada_b, m_ada_b, v_ada_b, g_ada_b)
    r_norm = update(norm_g, m_norm_g, v_norm_g, g_norm)
    r_rel = update(rel_bias, m_rel_bias, v_rel_bias, g_rel)
    r_swa_q = update(swa_q_g, m_swa_q_g, v_swa_q_g, g_swa_q)
    r_swa_k = update(swa_k_g, m_swa_k_g, v_swa_k_g, g_swa_k)
    r_sink = update(swa_sink, m_swa_sink, v_swa_sink, g_sink)
    r_bf = update(fox_b_f, m_fox_b_f, v_fox_b_f, g_bf)
    r_fox_q = update(fox_q_g, m_fox_q_g, v_fox_q_g, g_fox_q)
    r_fox_k = update(fox_k_g, m_fox_k_g, v_fox_k_g, g_fox_k)
    per_weight = [r_ada_w, r_ada_b, r_norm, big_out[0], big_out[1], r_rel, big_out[2], big_out[3], r_swa_q, r_swa_k,
                  r_sink, big_out[4], big_out[5], r_bf, r_fox_q, r_fox_k]
    return (loss, grad_x, *[r[0] for r in per_weight], *[r[1] for r in per_weight],
            *[r[2] for r in per_weight], *[r[3] for r in per_weight])
```

```python
import math

import numpy as np
import jax
import jax.numpy as jnp
from jax import lax
from jax.experimental import pallas as pl
from jax.experimental.pallas import tpu as pltpu

F32 = jnp.float32
BF16 = jnp.bfloat16
HEAD_DIM = 64
GROUP = 4
FOX_HEADS = 8
BLOCK = 128
REL_BUCKETS = 32
REL_MAX_DIST = 128
EPS = 1e-6
NEG = -1e30
N_CHIP = 4
N_DEV = 8
LANES = 128
VMEM_LIMIT = 52 * 1024 * 1024
ADAM_LR, ADAM_B1, ADAM_B2, ADAM_EPS, ADAM_WD, ADAM_STEP = 0.001, 0.9, 0.999, 1e-08, 0.01, 10
MESH = pl.DeviceIdType.MESH
ANY = pl.BlockSpec(memory_space=pl.ANY)


def _params(n_axes):
    return pltpu.CompilerParams(dimension_semantics=("arbitrary",) * n_axes, vmem_limit_bytes=VMEM_LIMIT)


def _nn(a, b):
    return jnp.dot(a, b, preferred_element_type=F32)


def _nt(a, b):
    return lax.dot_general(a, b, (((1,), (1,)), ((), ())), preferred_element_type=F32)


def _tn(a, b):
    return lax.dot_general(a, b, (((0,), (0,)), ((), ())), preferred_element_type=F32)


def _sigmoid(z):
    return 1.0 / (1.0 + jnp.exp(-z))


def _sigmoid_fast(z):
    return pl.reciprocal(1.0 + jnp.exp(-z), approx=True)


def _row_tile(s):
    return 512 if s >= 2048 else s // 2


def _wide_tile(s):
    return 1024 if s >= 2048 else s // 2


def _attn_tile(s):
    return 512 if s >= 2048 else s // 4


def _position():
    x, y, c = lax.axis_index("x"), lax.axis_index("y"), lax.axis_index("c")
    chips = [(1 - x, y), (x, 1 - y), (1 - x, 1 - y)]
    return x, y, c, chips


def all_gather_rows(v, name):
    m_per, n = v.shape

    def body(x_ref, out_ref, send_sems, recv_sems, local_sem):
        x, y, c, chips = _position()
        me, sibling = (x, y, c), (x, y, 1 - c)

        def rows(px, py, pc):
            return out_ref.at[pl.ds((4 * px + 2 * py + pc) * m_per, m_per), :]

        def copy(k, block, to, src=None):
            return pltpu.make_async_remote_copy(
                src_ref=rows(*block) if src is None else src, dst_ref=rows(*block),
                send_sem=send_sems.at[k], recv_sem=recv_sems.at[k], device_id=to, device_id_type=MESH)

        mine = pltpu.make_async_copy(x_ref, rows(*me), local_sem)
        mine.start()
        first = [copy(0, me, sibling, src=x_ref)]
        first += [copy(1 + j, me, (*chip, c), src=x_ref) for j, chip in enumerate(chips)]
        for cp in first:
            cp.start()
        passed = [copy(4 + j, (*chip, c), sibling) for j, chip in enumerate(chips)]
        for j, chip in enumerate(chips):
            copy(1 + j, (*chip, c), me).wait_recv()
            passed[j].start()
        copy(0, sibling, me).wait_recv()
        for j, chip in enumerate(chips):
            copy(4 + j, (*chip, 1 - c), me).wait_recv()
        for cp in first + passed:
            cp.wait_send()
        mine.wait()

    return pl.pallas_call(
        body, name=name,
        out_shape=jax.ShapeDtypeStruct((N_DEV * m_per, n), v.dtype),
        in_specs=[pl.BlockSpec(memory_space=pltpu.VMEM)],
        out_specs=pl.BlockSpec(memory_space=pltpu.VMEM),
        scratch_shapes=[pltpu.SemaphoreType.DMA((7,)), pltpu.SemaphoreType.DMA((7,)), pltpu.SemaphoreType.DMA],
    )(v)


def _slab(full_ref, kind, b, lead):
    how = kind[0]
    if how == "slot":
        return full_ref.at[b, lead]
    if how == "col":
        w = kind[1]
        idx = (lead,) + (slice(None),) * (len(full_ref.shape) - 2) + (pl.ds(pl.multiple_of(b * w, LANES), w),)
        return full_ref.at[idx]
    h = kind[1]
    idx = (lead,) + (slice(None),) * (len(full_ref.shape) - 3) + (pl.ds(pl.multiple_of(b * h, 8), h), slice(None))
    return full_ref.at[idx]


def _full_shape(shard_shape, kind):
    if kind[0] == "slot":
        return (N_CHIP,) + tuple(shard_shape)
    if kind[0] == "col":
        return tuple(shard_shape[:-1]) + (N_CHIP * shard_shape[-1],)
    return tuple(shard_shape[:-2]) + (N_CHIP * shard_shape[-2], shard_shape[-1])


def _slab_start(shape, kind, b):
    zeros = [0] * len(shape)
    if kind[0] == "slot":
        return [b] + zeros[1:]
    if kind[0] == "col":
        return zeros[:-1] + [b * kind[1]]
    return zeros[:-2] + [b * kind[1], 0]


def cast_and_place(shard, kind, chip, after, name):
    rows, cols = shard.shape[-2:]
    lead = int(np.prod(shard.shape[:-2]))
    if kind[0] == "col":
        full3, where = (lead, rows, N_CHIP * cols), lambda p, b: (p, 0, b[0])
    elif kind[0] == "row":
        full3, where = (lead, N_CHIP * rows, cols), lambda p, b: (p, b[0], 0)
    else:
        full3, where = (N_CHIP * lead, rows, cols), lambda p, b: (b[0] * lead + p, 0, 0)

    def body(b_ref, s_ref, after_ref, o_ref):
        o_ref[...] = s_ref[...].astype(BF16)

    full = pl.pallas_call(
        body, name=name,
        grid_spec=pltpu.PrefetchScalarGridSpec(
            num_scalar_prefetch=1, grid=(lead,),
            in_specs=[pl.BlockSpec((1, rows, cols), lambda p, b: (p, 0, 0)), ANY],
            out_specs=pl.BlockSpec((1, rows, cols), where)),
        out_shape=jax.ShapeDtypeStruct(full3, BF16),
        compiler_params=_params(1),
    )(jnp.reshape(chip, (1,)).astype(jnp.int32), shard.reshape(lead, rows, cols), after)
    return full.reshape(_full_shape(shard.shape, kind))


def own_slab(full, kind, b, shard_shape):
    sizes = (1,) + tuple(shard_shape) if kind[0] == "slot" else tuple(shard_shape)
    return lax.dynamic_slice(full, _slab_start(full.shape, kind, b), sizes).reshape(shard_shape)


def gather_weights(fulls, kinds):
    n = len(fulls)

    def body(*refs):
        outs = refs[n:2 * n]
        send_sems, recv_sems = refs[2 * n:]
        x, y, c, chips = _position()
        b_me = 2 * x + y
        sibling = (x, y, 1 - c)
        sends = []

        def halves(t):
            lead = outs[t].shape[1] if kinds[t][0] == "slot" else outs[t].shape[0]
            return pl.ds(c * (lead // 2), lead // 2), pl.ds((1 - c) * (lead // 2), lead // 2)

        for t in range(n):
            mine, _ = halves(t)
            own = _slab(outs[t], kinds[t], b_me, mine)
            for j, chip in enumerate(chips):
                cp = pltpu.make_async_remote_copy(
                    src_ref=own, dst_ref=own,
                    send_sem=send_sems.at[6 * t + j], recv_sem=recv_sems.at[6 * t + j],
                    device_id=(*chip, c), device_id_type=MESH)
                cp.start()
                sends.append(cp)
        for t in range(n):
            mine, _ = halves(t)
            for j, chip in enumerate(chips):
                landed = _slab(outs[t], kinds[t], 2 * chip[0] + chip[1], mine)
                pltpu.make_async_remote_copy(
                    src_ref=landed, dst_ref=landed, send_sem=send_sems.at[6 * t + j], recv_sem=recv_sems.at[6 * t + j],
                    device_id=(*chip, c), device_id_type=MESH).wait_recv()
                cp = pltpu.make_async_remote_copy(
                    src_ref=landed, dst_ref=landed, send_sem=send_sems.at[6 * t + 3 + j],
                    recv_sem=recv_sems.at[6 * t + 3 + j], device_id=sibling, device_id_type=MESH)
                cp.start()
                sends.append(cp)
        for t in range(n):
            _, theirs = halves(t)
            for j, chip in enumerate(chips):
                landed = _slab(outs[t], kinds[t], 2 * chip[0] + chip[1], theirs)
                pltpu.make_async_remote_copy(
                    src_ref=landed, dst_ref=landed, send_sem=send_sems.at[6 * t + 3 + j],
                    recv_sem=recv_sems.at[6 * t + 3 + j], device_id=sibling, device_id_type=MESH).wait_recv()
        for cp in sends:
            cp.wait_send()

    return pl.pallas_call(
        body, name="gather_weights",
        out_shape=[jax.ShapeDtypeStruct(v.shape, v.dtype) for v in fulls],
        in_specs=[ANY] * n, out_specs=[ANY] * n,
        input_output_aliases={t: t for t in range(n)},
        scratch_shapes=[pltpu.SemaphoreType.DMA((6 * n,)), pltpu.SemaphoreType.DMA((6 * n,))],
    )(*fulls)


def scatter_grads(grads, kinds, shard_shapes):
    n = len(grads)

    def body(*refs):
        ins, outs = refs[:n], refs[n:2 * n]
        send_sems, recv_sems = refs[2 * n:]
        x, y, c, chips = _position()
        sends = []
        for t in range(n):
            whole = _whole(ins[t], kinds[t])
            for j, chip in enumerate(chips):
                cp = pltpu.make_async_remote_copy(
                    src_ref=_slab(ins[t], kinds[t], 2 * chip[0] + chip[1], whole), dst_ref=outs[t].at[j],
                    send_sem=send_sems.at[3 * t + j], recv_sem=recv_sems.at[3 * t + j],
                    device_id=(*chip, c), device_id_type=MESH)
                cp.start()
                sends.append(cp)
        for t in range(n):
            for j, chip in enumerate(chips):
                pltpu.make_async_remote_copy(
                    src_ref=outs[t].at[j], dst_ref=outs[t].at[j], send_sem=send_sems.at[3 * t + j],
                    recv_sem=recv_sems.at[3 * t + j], device_id=(*chip, c), device_id_type=MESH).wait_recv()
        for cp in sends:
            cp.wait_send()

    return pl.pallas_call(
        body, name="scatter_grads",
        out_shape=[jax.ShapeDtypeStruct((N_CHIP - 1,) + tuple(s), g.dtype) for g, s in zip(grads, shard_shapes)],
        in_specs=[ANY] * n, out_specs=[ANY] * n,
        scratch_shapes=[pltpu.SemaphoreType.DMA((3 * n,)), pltpu.SemaphoreType.DMA((3 * n,))],
    )(*grads)


def _whole(ref, kind):
    return pl.ds(0, ref.shape[1] if kind[0] == "slot" else ref.shape[0])


def _exchange_copies(srcs, lands, kinds, gather, send_sems, recv_sems):
    x, y, c, chips = _position()
    b_me = 2 * x + y
    out = []
    for t in range(len(lands)):
        for j, chip in enumerate(chips):
            b_j = 2 * chip[0] + chip[1]
            if gather:
                src = sent_to = _slab(lands[t], kinds[t], b_me, _whole(lands[t], kinds[t]))
                arrives = _slab(lands[t], kinds[t], b_j, _whole(lands[t], kinds[t]))
            else:
                src = _slab(srcs[t], kinds[t], b_j, _whole(srcs[t], kinds[t]))
                sent_to = arrives = lands[t].at[j]
            k = 3 * t + j
            send = pltpu.make_async_remote_copy(src_ref=src, dst_ref=sent_to, send_sem=send_sems.at[k],
                                                recv_sem=recv_sems.at[k], device_id=(*chip, c), device_id_type=MESH)
            recv = pltpu.make_async_remote_copy(src_ref=src, dst_ref=arrives, send_sem=send_sems.at[k],
                                                recv_sem=recv_sems.at[k], device_id=(*chip, c), device_id_type=MESH)
            out.append((send, recv))
    return out


def exchange_start(srcs, lands, kinds, gather, name):
    ns, nl = len(srcs), len(lands)
    hbm = pl.BlockSpec(memory_space=pltpu.HBM)

    def body(*refs):
        ins, lnd = refs[:ns], refs[ns:ns + nl]
        send_sems, recv_sems = refs[ns + nl], refs[ns + nl + 1]
        token = refs[-1]
        for send, _ in _exchange_copies(ins, lnd, kinds, gather, send_sems, recv_sems):
            send.start()
        token[...] = jnp.zeros_like(token)

    ops = [pltpu.with_memory_space_constraint(v, pltpu.HBM) for v in (*srcs, *lands)]
    res = pl.pallas_call(
        body, name=name,
        out_shape=(pltpu.SemaphoreType.DMA((3 * nl,)), pltpu.SemaphoreType.DMA((3 * nl,)),
                   *[pltpu.HBM(v.shape, v.dtype) for v in ops], jax.ShapeDtypeStruct((8, LANES), F32)),
        in_specs=[hbm] * (ns + nl),
        out_specs=(pl.BlockSpec(memory_space=pltpu.SEMAPHORE), pl.BlockSpec(memory_space=pltpu.SEMAPHORE),
                   *[hbm] * (ns + nl), pl.BlockSpec(memory_space=pltpu.VMEM)),
        input_output_aliases={t: 2 + t for t in range(ns + nl)},
        compiler_params=pltpu.CompilerParams(has_side_effects=pltpu.SideEffectType.DATAFLOW_SIDE_EFFECTING),
    )(*ops)
    return res[0], res[1], list(res[2:2 + ns]), list(res[2 + ns:2 + ns + nl]), res[-1]


def exchange_wait(send_sems, recv_sems, srcs, lands, after, kinds, gather, name):
    ns, nl = len(srcs), len(lands)
    hbm = pl.BlockSpec(memory_space=pltpu.HBM)

    def body(*refs):
        ins, lnd = refs[:ns], refs[ns:ns + nl]
        ssem, rsem = refs[ns + nl], refs[ns + nl + 1]
        for send, recv in _exchange_copies(ins, lnd, kinds, gather, ssem, rsem):
            send.wait_send()
            recv.wait_recv()

    res = pl.pallas_call(
        body, name=name,
        out_shape=tuple(pltpu.HBM(v.shape, v.dtype) for v in (*srcs, *lands)),
        in_specs=[hbm] * (ns + nl) + [pl.BlockSpec(memory_space=pltpu.SEMAPHORE)] * 2 + [ANY],
        out_specs=tuple([hbm] * (ns + nl)),
        input_output_aliases={t: t for t in range(ns + nl)},
        compiler_params=pltpu.CompilerParams(has_side_effects=pltpu.SideEffectType.DATAFLOW_SIDE_EFFECTING),
    )(*srcs, *lands, send_sems, recv_sems, after)
    return list(res[:ns]), list(res[ns:])


def swap_with_sibling(parts):
    n = len(parts)

    def body(*refs):
        ins, outs = refs[:n], refs[n:2 * n]
        send_sems, recv_sems = refs[2 * n:]
        x, y, c, _ = _position()
        cps = []
        for t in range(n):
            cp = pltpu.make_async_remote_copy(
                src_ref=ins[t], dst_ref=outs[t], send_sem=send_sems.at[t], recv_sem=recv_sems.at[t],
                device_id=(x, y, 1 - c), device_id_type=MESH)
            cp.start()
            cps.append(cp)
        for cp in cps:
            cp.wait_recv()
        for cp in cps:
            cp.wait_send()

    return pl.pallas_call(
        body, name="swap_with_sibling",
        out_shape=[jax.ShapeDtypeStruct(p.shape, p.dtype) for p in parts],
        in_specs=[ANY] * n, out_specs=[ANY] * n,
        scratch_shapes=[pltpu.SemaphoreType.DMA((n,)), pltpu.SemaphoreType.DMA((n,))],
    )(*parts)


def _modulated(xv, mv_ref):
    g, shift, scale = mv_ref[0:1, :], mv_ref[1:2, :], mv_ref[2:3, :]
    r = lax.rsqrt(jnp.mean(xv * xv, axis=-1, keepdims=True) + EPS)
    xhat = xv * r
    xn = xhat * g
    return xn * (1.0 + scale) + shift, xhat, xn, r, g, scale


def _resident(block_shape, index):
    return pl.BlockSpec(block_shape, lambda i: index, pipeline_mode=pl.Buffered(1))


def modmm(x, mv, mv_idx, w, w_block, w_index, n_cols, chunk, out_dtype, name, want_h):
    s, d = x.shape
    tm = _row_tile(s)

    def body(x_ref, mv_ref, w_ref, out_ref, *rest):
        h = _modulated(x_ref[...], mv_ref)[0].astype(BF16)
        if want_h:
            rest[0][...] = h
        for n in range(n_cols // chunk):
            cols = slice(n * chunk, (n + 1) * chunk)
            out_ref[:, cols] = _nn(h, w_ref[:, cols]).astype(out_dtype)

    row = pl.BlockSpec((tm, d), lambda i: (i, 0))
    out_shapes = [jax.ShapeDtypeStruct((s, n_cols), out_dtype)]
    out_specs = [pl.BlockSpec((tm, n_cols), lambda i: (i, 0))]
    if want_h:
        out_shapes.append(jax.ShapeDtypeStruct((s, d), BF16))
        out_specs.append(row)
    res = pl.pallas_call(
        body, name=name, grid=(s // tm,),
        in_specs=[row, pl.BlockSpec((None, None, 8, d), lambda i: (*mv_idx, 0, 0)), _resident(w_block, w_index)],
        out_specs=out_specs, out_shape=out_shapes,
        compiler_params=_params(1),
    )(x, mv, w)
    return res if want_h else (res[0], None)


def resmm(x, mv, mv_idx, coef, lhs, w, w_block, w_index, kdim, chunk, name, ffn):
    s, d = x.shape
    tm = _row_tile(s)
    lhs_cols = 2 * kdim if ffn else kdim

    def body(x_ref, mv_ref, lhs_ref, w_ref, xo_ref, y_ref, *rest):
        y = jnp.zeros((tm, d), F32)
        for k in range(kdim // chunk):
            rows = slice(k * chunk, (k + 1) * chunk)
            if ffn:
                ag = lhs_ref[:, rows].astype(F32)
                au = lhs_ref[:, kdim + k * chunk:kdim + (k + 1) * chunk].astype(F32)
                left = (ag * _sigmoid_fast(ag) * au).astype(BF16)
                rest[0][:, rows] = left
            else:
                left = lhs_ref[:, rows]
            y = y + _nn(left, w_ref[rows, :])
        y_ref[...] = y.astype(BF16)
        xo_ref[...] = x_ref[...] + (coef * mv_ref[3:4, :]) * y

    row = pl.BlockSpec((tm, d), lambda i: (i, 0))
    out_shapes = [jax.ShapeDtypeStruct((s, d), F32), jax.ShapeDtypeStruct((s, d), BF16)]
    out_specs = [row, row]
    if ffn:
        out_shapes.append(jax.ShapeDtypeStruct((s, kdim), BF16))
        out_specs.append(pl.BlockSpec((tm, kdim), lambda i: (i, 0)))
    return pl.pallas_call(
        body, name=name, grid=(s // tm,),
        in_specs=[row, pl.BlockSpec((None, None, 8, d), lambda i: (*mv_idx, 0, 0)),
                  pl.BlockSpec((tm, lhs_cols), lambda i: (i, 0)), _resident(w_block, w_index)],
        out_specs=out_specs, out_shape=out_shapes,
        compiler_params=_params(1),
    )(x, mv, lhs, w)


def resmm_bwd(dxo, y, mv, mv_idx, coef, w, w_block, w_index, kdim, chunk, name, a=None):
    s, d = dxo.shape
    ffn = a is not None
    tm = _row_tile(s) // 2 if ffn else _row_tile(s)
    dl_cols = 2 * kdim if ffn else kdim

    def body(dxo_ref, y_ref, mv_ref, w_ref, *rest):
        if ffn:
            a_ref, dy_ref, dl_ref, dgate_ref = rest
        else:
            dy_ref, dl_ref, dgate_ref = rest

        @pl.when(pl.program_id(0) == 0)
        def _():
            dgate_ref[...] = jnp.zeros_like(dgate_ref)

        dxv = dxo_ref[...]
        dy = ((coef * mv_ref[3:4, :]) * dxv).astype(BF16)
        dy_ref[...] = dy
        dgate_ref[0:1, :] += jnp.sum(coef * dxv * y_ref[...].astype(F32), axis=0, keepdims=True)
        for k in range(kdim // chunk):
            rows = slice(k * chunk, (k + 1) * chunk)
            dl = _nt(dy, w_ref[rows, :])
            if ffn:
                ups = slice(kdim + k * chunk, kdim + (k + 1) * chunk)
                ag = a_ref[:, rows].astype(F32)
                au = a_ref[:, ups].astype(F32)
                sg = _sigmoid_fast(ag)
                dl_ref[:, rows] = (dl * au * (sg * (1.0 + ag * (1.0 - sg)))).astype(BF16)
                dl_ref[:, ups] = (dl * (ag * sg)).astype(BF16)
            else:
                dl_ref[:, rows] = dl.astype(BF16)

    row = pl.BlockSpec((tm, d), lambda i: (i, 0))
    wide = pl.BlockSpec((tm, dl_cols), lambda i: (i, 0))
    in_specs = [row, row, pl.BlockSpec((None, None, 8, d), lambda i: (*mv_idx, 0, 0)), _resident(w_block, w_index)]
    ops = [dxo, y, mv, w]
    if ffn:
        in_specs.append(wide)
        ops.append(a)
    return pl.pallas_call(
        body, name=name, grid=(s // tm,),
        in_specs=in_specs,
        out_specs=[row, wide, pl.BlockSpec((8, d), lambda i: (0, 0))],
        out_shape=[jax.ShapeDtypeStruct((s, d), BF16), jax.ShapeDtypeStruct((s, dl_cols), BF16),
                   jax.ShapeDtypeStruct((8, d), F32)],
        compiler_params=_params(1),
    )(*ops)


def modmm_bwd(dl, w, w_block, w_index, n_cols, chunk, x, dxo, mv, mv_idx, name, more=None):
    s, d = x.shape
    tm = _row_tile(s)

    def body(dl_ref, w_ref, x_ref, dxo_ref, mv_ref, *rest):
        if more is not None:
            dl2_ref, w2_ref, dx_ref, red_ref = rest
            dh = _nt(dl2_ref[...], w2_ref[...])
        else:
            dx_ref, red_ref = rest
            dh = jnp.zeros((tm, d), F32)

        @pl.when(pl.program_id(0) == 0)
        def _():
            red_ref[...] = jnp.zeros_like(red_ref)

        for n in range(n_cols // chunk):
            cols = slice(n * chunk, (n + 1) * chunk)
            dh = dh + _nt(dl_ref[:, cols], w_ref[:, cols])
        _, xhat, xn, r, g, scale = _modulated(x_ref[...], mv_ref)
        dxn = dh * (1.0 + scale)
        red_ref[0:1, :] += jnp.sum(dxn * xhat, axis=0, keepdims=True)
        red_ref[1:2, :] += jnp.sum(dh, axis=0, keepdims=True)
        red_ref[2:3, :] += jnp.sum(dh * xn, axis=0, keepdims=True)
        gd = dxn * g
        dx_ref[...] = dxo_ref[...] + r * (gd - xhat * jnp.mean(gd * xhat, axis=-1, keepdims=True))

    row = pl.BlockSpec((tm, d), lambda i: (i, 0))
    in_specs = [pl.BlockSpec((tm, n_cols), lambda i: (i, 0)), _resident(w_block, w_index), row, row,
                pl.BlockSpec((None, None, 8, d), lambda i: (*mv_idx, 0, 0))]
    ops = [dl, w, x, dxo, mv]
    if more is not None:
        in_specs += [pl.BlockSpec((tm, more[0].shape[1]), lambda i: (i, 0)), _resident(more[1], more[2])]
        ops += [more[0], w]
    return pl.pallas_call(
        body, name=name, grid=(s // tm,),
        in_specs=in_specs,
        out_specs=[row, pl.BlockSpec((8, d), lambda i: (0, 0))],
        out_shape=[jax.ShapeDtypeStruct((s, d), F32), jax.ShapeDtypeStruct((8, d), F32)],
        compiler_params=_params(1),
    )(*ops)


def weight_grad(a, a_spec, b, b_spec, grid_mn, s, bm, bn, dest, out_spec, name):
    tk = _wide_tile(s)
    k_tiles = s // tk

    def body(a_ref, b_ref, dest_ref, out_ref, acc):
        k = pl.program_id(2)

        @pl.when(k == 0)
        def _():
            acc[...] = jnp.zeros_like(acc)

        acc[...] += _tn(a_ref[...], b_ref[...])

        @pl.when(k == k_tiles - 1)
        def _():
            out_ref[...] = acc[...].astype(out_ref.dtype)

    return pl.pallas_call(
        body, name=name, grid=(*grid_mn, k_tiles),
        in_specs=[a_spec, b_spec, ANY], out_specs=out_spec,
        out_shape=jax.ShapeDtypeStruct(dest.shape, dest.dtype),
        input_output_aliases={2: 0},
        scratch_shapes=[pltpu.VMEM((bm, bn), F32)],
        compiler_params=_params(3),
    )(a, b, dest)


def _head_mean(v):
    lane = lax.broadcasted_iota(jnp.int32, v.shape, 1)
    lo = jnp.sum(jnp.where(lane < HEAD_DIM, v, 0.0), axis=-1, keepdims=True)
    hi = jnp.sum(v, axis=-1, keepdims=True) - lo
    return jnp.where(lane < HEAD_DIM, lo, hi) * (1.0 / HEAD_DIM)


def qknorm_fwd(proj, gains, width, name):
    s = proj.shape[0]
    nqk = gains.shape[1]
    tm = _row_tile(s)

    def body(p_ref, g_ref, o_ref):
        for cc in range(width // LANES):
            sl = slice(cc * LANES, (cc + 1) * LANES)
            xv = p_ref[:, sl].astype(F32)
            r = lax.rsqrt(_head_mean(xv * xv) + EPS)
            o_ref[:, sl] = (xv * r * g_ref[:, sl]).astype(BF16)

    blk = pl.BlockSpec((tm, width), lambda i, c: (i, c))
    return pl.pallas_call(
        body, name=name, grid=(s // tm, nqk // width),
        in_specs=[blk, pl.BlockSpec((1, width), lambda i, c: (0, c))],
        out_specs=blk, out_shape=jax.ShapeDtypeStruct((s, nqk), BF16),
        compiler_params=_params(2),
    )(proj, gains)


def qknorm_bwd(proj, gains, d, d_spec, n_cols, width, name):
    s = proj.shape[0]
    nqk = gains.shape[1] // width
    n_blocks = n_cols // width
    tm = _row_tile(s)

    def body(p_ref, g_ref, d_ref, o_ref, dg_ref):
        c, i = pl.program_id(0), pl.program_id(1)

        @pl.when(i == 0)
        def _():
            dg_ref[...] = jnp.zeros_like(dg_ref)

        @pl.when(c < nqk)
        def _():
            for cc in range(width // LANES):
                sl = slice(cc * LANES, (cc + 1) * LANES)
                xv = p_ref[:, sl].astype(F32)
                r = lax.rsqrt(_head_mean(xv * xv) + EPS)
                xhat = xv * r
                dv = d_ref[:, sl]
                gd = dv * g_ref[:, sl]
                o_ref[:, sl] = (r * (gd - xhat * _head_mean(gd * xhat))).astype(BF16)
                dg_ref[0:1, sl] += jnp.sum(dv * xhat, axis=0, keepdims=True)

        @pl.when(c >= nqk)
        def _():
            o_ref[...] = d_ref[...].astype(BF16)

    return pl.pallas_call(
        body, name=name, grid=(n_blocks, s // tm),
        in_specs=[pl.BlockSpec((tm, width), lambda c, i: (i, c)),
                  pl.BlockSpec((1, width), lambda c, i: (0, jnp.minimum(c, nqk - 1))), d_spec],
        out_specs=[pl.BlockSpec((tm, width), lambda c, i: (i, c)), pl.BlockSpec((8, width), lambda c, i: (0, c))],
        out_shape=[jax.ShapeDtypeStruct((s, n_cols), BF16), jax.ShapeDtypeStruct((8, n_cols), F32)],
        compiler_params=_params(2),
    )(proj, gains, d)


def _swa_mask(first):
    qi = lax.broadcasted_iota(jnp.int32, (BLOCK, 2 * BLOCK), 0) + BLOCK
    kj = lax.broadcasted_iota(jnp.int32, (BLOCK, 2 * BLOCK), 1)
    dist = qi - kj
    return (dist >= 0) & (dist < BLOCK) & ((kj >= BLOCK) | jnp.logical_not(first))


def swa_fwd(qkn, proj, bias, sink, d, name):
    s = qkn.shape[0]
    hq = d // HEAD_DIM
    hkv = hq // GROUP
    kw = hkv * HEAD_DIM
    nblk = s // BLOCK
    kcol = d // kw

    def body(q_ref, kc_ref, kp_ref, vc_ref, vp_ref, bias_ref, sink_ref, o_ref, lse_ref):
        mask = _swa_mask(pl.program_id(0) == 0)
        lse_ref[...] = jnp.zeros_like(lse_ref)
        for kvh in range(hkv):
            cols = slice(kvh * HEAD_DIM, (kvh + 1) * HEAD_DIM)
            k2 = jnp.concatenate([kp_ref[:, cols], kc_ref[:, cols]], axis=0)
            v2 = jnp.concatenate([vp_ref[:, cols], vc_ref[:, cols]], axis=0)
            for g in range(GROUP):
                h = kvh * GROUP + g
                hc = slice(h * HEAD_DIM, (h + 1) * HEAD_DIM)
                sc = jnp.where(mask, _nt(q_ref[:, hc], k2) + bias_ref[h], NEG)
                sk = sink_ref[0, h]
                m = jnp.maximum(jnp.max(sc, axis=-1, keepdims=True), sk)
                p = jnp.exp(sc - m)
                denom = jnp.sum(p, axis=-1, keepdims=True) + jnp.exp(sk - m)
                o_ref[:, hc] = (_nn(p.astype(BF16), v2) / denom).astype(BF16)
                lse_ref[:, h:h + 1] = m + jnp.log(denom)

    prev = lambda i: jnp.maximum(i - 1, 0)
    return pl.pallas_call(
        body, name=name, grid=(nblk,),
        in_specs=[pl.BlockSpec((BLOCK, d), lambda i: (i, 0)),
                  pl.BlockSpec((BLOCK, kw), lambda i: (i, kcol)),
                  pl.BlockSpec((BLOCK, kw), lambda i: (prev(i), kcol)),
                  pl.BlockSpec((BLOCK, kw), lambda i: (i, kcol + 1)),
                  pl.BlockSpec((BLOCK, kw), lambda i: (prev(i), kcol + 1)),
                  pl.BlockSpec((hq, BLOCK, 2 * BLOCK), lambda i: (0, 0, 0)),
                  pl.BlockSpec(memory_space=pltpu.SMEM)],
        out_specs=[pl.BlockSpec((BLOCK, d), lambda i: (i, 0)), pl.BlockSpec((BLOCK, LANES), lambda i: (i, 0))],
        out_shape=[jax.ShapeDtypeStruct((s, d), BF16), jax.ShapeDtypeStruct((s, LANES), F32)],
        compiler_params=_params(1),
    )(qkn, qkn, qkn, proj, proj, bias, sink)


def swa_bwd(qkn, proj, bias, sink, do, o, lse, d, name):
    s = qkn.shape[0]
    hq = d // HEAD_DIM
    hkv = hq // GROUP
    kw = hkv * HEAD_DIM
    nblk = s // BLOCK
    kcol = d // kw
    wide = d + 2 * kw

    def body(q_ref, kc_ref, kp_ref, vc_ref, vp_ref, bias_ref, sink_ref, do_ref, o_ref, lse_ref,
             out_ref, dbias_ref, dsink_ref, carry, fresh):
        i = pl.program_id(0)

        @pl.when(i == 0)
        def _():
            dbias_ref[...] = jnp.zeros_like(dbias_ref)
            dsink_ref[...] = jnp.zeros_like(dsink_ref)
            carry[...] = jnp.zeros_like(carry)

        @pl.when(i == nblk)
        def _():
            fresh[...] = jnp.zeros_like(fresh)

        @pl.when(i < nblk)
        def _():
            mask = _swa_mask(i == 0)
            for kvh in range(hkv):
                cols = slice(kvh * HEAD_DIM, (kvh + 1) * HEAD_DIM)
                k2 = jnp.concatenate([kp_ref[:, cols], kc_ref[:, cols]], axis=0)
                v2 = jnp.concatenate([vp_ref[:, cols], vc_ref[:, cols]], axis=0)
                dk2 = jnp.zeros((2 * BLOCK, HEAD_DIM), F32)
                dv2 = jnp.zeros((2 * BLOCK, HEAD_DIM), F32)
                for g in range(GROUP):
                    h = kvh * GROUP + g
                    hc = slice(h * HEAD_DIM, (h + 1) * HEAD_DIM)
                    q = q_ref[:, hc]
                    dov = do_ref[:, hc]
                    lse_h = lse_ref[:, h:h + 1]
                    sc = jnp.where(mask, _nt(q, k2) + bias_ref[h], NEG)
                    p = jnp.exp(sc - lse_h)
                    delta = jnp.sum(dov.astype(F32) * o_ref[:, hc].astype(F32), axis=-1, keepdims=True)
                    ds = p * (_nt(dov, v2) - delta)
                    dbias_ref[h] += ds
                    dsink_ref[0:1, h:h + 1] += jnp.sum(-jnp.exp(sink_ref[0, h] - lse_h) * delta, axis=0, keepdims=True)
                    dsb = ds.astype(BF16)
                    fresh[0, :, hc] = _nn(dsb, k2)
                    dk2 += _tn(dsb, q)
                    dv2 += _tn(p.astype(BF16), dov)
                kc_cols = slice(d + kvh * HEAD_DIM, d + (kvh + 1) * HEAD_DIM)
                vc_cols = slice(d + kw + kvh * HEAD_DIM, d + kw + (kvh + 1) * HEAD_DIM)
                fresh[0, :, kc_cols] = dk2[BLOCK:]
                fresh[0, :, vc_cols] = dv2[BLOCK:]
                fresh[1, :, kc_cols] = dk2[:BLOCK]
                fresh[1, :, vc_cols] = dv2[:BLOCK]

        lane = lax.broadcasted_iota(jnp.int32, (BLOCK, wide), 1)
        out_ref[...] = carry[...] + jnp.where(lane >= d, fresh[1], 0.0)

        @pl.when(i < nblk)
        def _():
            carry[...] = fresh[0]

    cur = lambda i: jnp.minimum(i, nblk - 1)
    prev = lambda i: jnp.maximum(jnp.minimum(i, nblk - 1) - 1, 0)
    return pl.pallas_call(
        body, name=name, grid=(nblk + 1,),
        in_specs=[pl.BlockSpec((BLOCK, d), lambda i: (cur(i), 0)),
                  pl.BlockSpec((BLOCK, kw), lambda i: (cur(i), kcol)),
                  pl.BlockSpec((BLOCK, kw), lambda i: (prev(i), kcol)),
                  pl.BlockSpec((BLOCK, kw), lambda i: (cur(i), kcol + 1)),
                  pl.BlockSpec((BLOCK, kw), lambda i: (prev(i), kcol + 1)),
                  pl.BlockSpec((hq, BLOCK, 2 * BLOCK), lambda i: (0, 0, 0)),
                  pl.BlockSpec(memory_space=pltpu.SMEM),
                  pl.BlockSpec((BLOCK, d), lambda i: (cur(i), 0)),
                  pl.BlockSpec((BLOCK, d), lambda i: (cur(i), 0)),
                  pl.BlockSpec((BLOCK, LANES), lambda i: (cur(i), 0))],
        out_specs=[pl.BlockSpec((BLOCK, wide), lambda i: (jnp.maximum(i - 1, 0), 0)),
                   pl.BlockSpec((hq, BLOCK, 2 * BLOCK), lambda i: (0, 0, 0)),
                   pl.BlockSpec((8, LANES), lambda i: (0, 0))],
        out_shape=[jax.ShapeDtypeStruct((s, wide), F32), jax.ShapeDtypeStruct((hq, BLOCK, 2 * BLOCK), F32),
                   jax.ShapeDtypeStruct((8, LANES), F32)],
        scratch_shapes=[pltpu.VMEM((BLOCK, wide), F32), pltpu.VMEM((2, BLOCK, wide), F32)],
        compiler_params=_params(1),
    )(qkn, qkn, qkn, proj, proj, bias, sink, do, o, lse)


def _rel_bucket_table():
    qi = np.arange(BLOCK)[:, None] + BLOCK
    kj = np.arange(2 * BLOCK)[None, :]
    n = np.maximum(qi - kj, 0)
    max_exact = REL_BUCKETS // 2
    nf = np.maximum(n, 1).astype(np.float32)
    large = max_exact + (np.log(nf / max_exact) / math.log(REL_MAX_DIST / max_exact)
                         * (REL_BUCKETS - max_exact)).astype(np.int32)
    large = np.minimum(large, REL_BUCKETS - 1)
    return np.where(n < max_exact, n, large).astype(np.int32)


def rel_bias_table(rel_bias, bucket):
    hq = rel_bias.shape[1]

    def body(rb_ref, bucket_ref, out_ref):
        tbl = bucket_ref[...]

        def per_head(h, carry):
            def per_bucket(b, acc):
                return jnp.where(tbl == b, rb_ref[b, h], acc)

            out_ref[h] = lax.fori_loop(0, REL_BUCKETS, per_bucket, jnp.zeros(tbl.shape, F32))
            return carry

        lax.fori_loop(0, hq, per_head, 0)

    return pl.pallas_call(
        body, name="rel_bias_table",
        in_specs=[pl.BlockSpec(memory_space=pltpu.SMEM), pl.BlockSpec(memory_space=pltpu.VMEM)],
        out_specs=pl.BlockSpec(memory_space=pltpu.VMEM),
        out_shape=jax.ShapeDtypeStruct((hq,) + tuple(bucket.shape), F32),
    )(rel_bias, bucket)


def rel_bias_grad(dbias, bucket):
    n_layers, hq = dbias.shape[:2]

    def body(db_ref, bucket_ref, out_ref):
        tbl = bucket_ref[...]

        def per_head(h, carry):
            dsum = db_ref[0, h]
            for a in range(1, n_layers):
                dsum = dsum + db_ref[a, h]

            def per_bucket(b, carry2):
                out_ref[b, h] = jnp.sum(jnp.where(tbl == b, dsum, 0.0))
                return carry2

            return lax.fori_loop(0, REL_BUCKETS, per_bucket, carry)

        lax.fori_loop(0, hq, per_head, 0)

    return pl.pallas_call(
        body, name="rel_bias_grad",
        in_specs=[pl.BlockSpec(memory_space=pltpu.VMEM), pl.BlockSpec(memory_space=pltpu.VMEM)],
        out_specs=pl.BlockSpec(memory_space=pltpu.SMEM),
        out_shape=jax.ShapeDtypeStruct((REL_BUCKETS, hq), F32),
    )(dbias, bucket)


def _split3(v):
    hi = v.astype(BF16)
    r1 = v - hi.astype(F32)
    mid = r1.astype(BF16)
    lo = (r1 - mid.astype(F32)).astype(BF16)
    return hi, mid, lo


def _tri_sum(tri, v):
    hi, mid, lo = _split3(v)
    return _nn(tri, hi) + _nn(tri, mid) + _nn(tri, lo)


def fox_gates(fl, b_f, name):
    s = fl.shape[0]
    t = _row_tile(s)

    def body(fl_ref, b_ref, f_ref, carry):
        @pl.when(pl.program_id(0) == 0)
        def _():
            carry[...] = jnp.zeros_like(carry)

        z = fl_ref[...] + b_ref[...]
        logf = jnp.minimum(z, 0.0) - jnp.log(1.0 + jnp.exp(-jnp.abs(z)))
        r = lax.broadcasted_iota(jnp.int32, (t, t), 0)
        cidx = lax.broadcasted_iota(jnp.int32, (t, t), 1)
        tri = jnp.where(cidx <= r, 1.0, 0.0).astype(BF16)
        f = _tri_sum(tri, logf) + carry[0:1, :]
        f_ref[...] = f
        carry[0:1, :] = f_ref[t - 1:t, :]

    blk = pl.BlockSpec((t, LANES), lambda i: (i, 0))
    return pl.pallas_call(
        body, name=name, grid=(s // t,),
        in_specs=[blk, pl.BlockSpec((1, LANES), lambda i: (0, 0))],
        out_specs=blk, out_shape=jax.ShapeDtypeStruct((s, LANES), F32),
        scratch_shapes=[pltpu.VMEM((8, LANES), F32)],
        compiler_params=_params(1),
    )(fl, b_f)


def fox_gates_bwd(fl, b_f, df_query, df_key, name):
    s = fl.shape[0]
    t = _row_tile(s)
    nb = s // t

    def body(fl_ref, b_ref, dfq_ref, dfk_ref, dfl_ref, db_ref, carry):
        @pl.when(pl.program_id(0) == 0)
        def _():
            carry[...] = jnp.zeros_like(carry)
            db_ref[...] = jnp.zeros_like(db_ref)

        dfv = dfq_ref[...] + dfk_ref[...]
        r = lax.broadcasted_iota(jnp.int32, (t, t), 0)
        cidx = lax.broadcasted_iota(jnp.int32, (t, t), 1)
        tri = jnp.where(cidx >= r, 1.0, 0.0).astype(BF16)
        dlog = _tri_sum(tri, dfv) + carry[0:1, :]
        carry[0:1, :] += jnp.sum(dfv, axis=0, keepdims=True)
        z = fl_ref[...] + b_ref[...]
        dz = dlog * (1.0 - _sigmoid(z))
        dfl_ref[...] = dz.astype(BF16)
        db_ref[0:1, :] += jnp.sum(dz, axis=0, keepdims=True)

    rev = pl.BlockSpec((t, LANES), lambda i: (nb - 1 - i, 0))
    return pl.pallas_call(
        body, name=name, grid=(nb,),
        in_specs=[rev, pl.BlockSpec((1, LANES), lambda i: (0, 0)), rev, rev],
        out_specs=[rev, pl.BlockSpec((8, LANES), lambda i: (0, 0))],
        out_shape=[jax.ShapeDtypeStruct((s, LANES), BF16), jax.ShapeDtypeStruct((8, LANES), F32)],
        scratch_shapes=[pltpu.VMEM((8, LANES), F32)],
        compiler_params=_params(1),
    )(fl, b_f, df_query, df_key)


def fox_fwd(qkn, proj, f_col, f_row, d, name):
    s = qkn.shape[0]
    t = _attn_tile(s)
    hs, wide = FOX_HEADS, FOX_HEADS * HEAD_DIM
    n_pairs = d // wide
    nt = s // t

    def body(q_ref, k_ref, v_ref, fq_ref, fk_ref, o_ref, o32_ref, lse_ref, m_scr, l_scr, acc):
        i, j = pl.program_id(1), pl.program_id(2)

        @pl.when(j == 0)
        def _():
            m_scr[...] = jnp.full_like(m_scr, NEG)
            l_scr[...] = jnp.zeros_like(l_scr)
            acc[...] = jnp.zeros_like(acc)

        @pl.when(j <= i)
        def _():
            krow = lax.broadcasted_iota(jnp.int32, (t, t), 0)
            qcol = lax.broadcasted_iota(jnp.int32, (t, t), 1)
            visible = (krow <= qcol) | (j < i)
            for hh in range(hs):
                hc = slice(hh * HEAD_DIM, (hh + 1) * HEAD_DIM)
                st = _nt(k_ref[:, hc], q_ref[:, hc]) + fq_ref[hh] - fk_ref[hh]
                st = jnp.where(visible, st, NEG)
                m_prev = m_scr[hh]
                m_new = jnp.maximum(m_prev, jnp.max(st, axis=0, keepdims=True))
                alpha = jnp.exp(m_prev - m_new)
                pt = jnp.exp(st - m_new)
                l_scr[hh] = alpha * l_scr[hh] + jnp.sum(pt, axis=0, keepdims=True)
                acc[hc, :] = alpha * acc[hc, :] + _tn(v_ref[:, hc], pt.astype(BF16))
                m_scr[hh] = m_new

        @pl.when(j == i)
        def _():
            l_full = jnp.concatenate([jnp.broadcast_to(l_scr[hh], (HEAD_DIM, t)) for hh in range(hs)], axis=0)
            ov = (acc[...] / l_full).T
            o_ref[...] = ov.astype(BF16)
            o32_ref[...] = ov
            lse_ref[...] = m_scr[...] + jnp.log(l_scr[...])

    kv = lambda j, i: jnp.minimum(j, i)
    return pl.pallas_call(
        body, name=name, grid=(n_pairs, nt, nt),
        in_specs=[pl.BlockSpec((t, wide), lambda p, i, j: (i, p)),
                  pl.BlockSpec((t, wide), lambda p, i, j: (kv(j, i), n_pairs + p)),
                  pl.BlockSpec((t, wide), lambda p, i, j: (kv(j, i), 2 * n_pairs + p)),
                  pl.BlockSpec((hs, 1, t), lambda p, i, j: (p, 0, i)),
                  pl.BlockSpec((hs, t, 1), lambda p, i, j: (p, kv(j, i), 0))],
        out_specs=[pl.BlockSpec((t, wide), lambda p, i, j: (i, p)),
                   pl.BlockSpec((t, wide), lambda p, i, j: (i, p)),
                   pl.BlockSpec((hs, 1, t), lambda p, i, j: (p, 0, i))],
        out_shape=[jax.ShapeDtypeStruct((s, d), BF16), jax.ShapeDtypeStruct((s, d), F32),
                   jax.ShapeDtypeStruct((hs * n_pairs, 1, s), F32)],
        scratch_shapes=[pltpu.VMEM((hs, 1, t), F32), pltpu.VMEM((hs, 1, t), F32), pltpu.VMEM((wide, t), F32)],
        compiler_params=_params(3),
    )(qkn, qkn, proj, f_row, f_col)


def fox_bwd(qkn, proj, f_col, f_row, lse_row, do, o, d, name):
    s = qkn.shape[0]
    t = _attn_tile(s)
    hs, wide = FOX_HEADS, FOX_HEADS * HEAD_DIM
    n_pairs = d // wide
    nt = s // t

    def body(q_ref, k_ref, v_ref, fk_ref, fq_ref, lse_ref, do_ref, o_ref, out_ref, df_ref, dfq_ref,
             dq_acc, dkv_acc, df_acc, dfq_acc):
        j, i = pl.program_id(1), pl.program_id(2)

        @pl.when((j == 0) & (i == 0))
        def _():
            dq_acc[...] = jnp.zeros_like(dq_acc)
            dfq_acc[...] = jnp.zeros_like(dfq_acc)

        @pl.when(i == 0)
        def _():
            dkv_acc[...] = jnp.zeros_like(dkv_acc)
            df_acc[...] = jnp.zeros_like(df_acc)

        @pl.when(i >= j)
        def _():
            krow = lax.broadcasted_iota(jnp.int32, (t, t), 0)
            qcol = lax.broadcasted_iota(jnp.int32, (t, t), 1)
            visible = (krow <= qcol) | (i > j)
            ones = jnp.ones((8, HEAD_DIM), BF16)
            for hh in range(hs):
                hc = slice(hh * HEAD_DIM, (hh + 1) * HEAD_DIM)
                q, k, v, dov = q_ref[:, hc], k_ref[:, hc], v_ref[:, hc], do_ref[:, hc]
                st = _nt(k, q) + fq_ref[hh] - fk_ref[hh]
                pt = jnp.exp(jnp.where(visible, st, NEG) - lse_ref[hh])
                hi, mid, lo = _split3(dov.astype(F32) * o_ref[:, hc])
                delta = jnp.max(_nt(ones, hi) + _nt(ones, mid) + _nt(ones, lo), axis=0, keepdims=True)
                dst = pt * (_nt(v, dov) - delta)
                dsb = dst.astype(BF16)
                dkv_acc[1, :, hc] += _nn(pt.astype(BF16), dov)
                dkv_acc[0, :, hc] += _nn(dsb, q)
                dq_acc[pl.ds(pl.multiple_of(i * t, t), t), hc] += _tn(dsb, k)
                df_acc[hh] -= jnp.sum(dst, axis=-1, keepdims=True)
                dfq_acc[i, hh] += jnp.sum(dst, axis=0, keepdims=True)

        @pl.when(i == nt - 1)
        def _():
            out_ref[0] = dq_acc[pl.ds(pl.multiple_of(j * t, t), t), :]
            out_ref[1] = dkv_acc[0]
            out_ref[2] = dkv_acc[1]
            df_ref[...] = df_acc[...]
            dfq_ref[...] = dfq_acc[j]

    qi = lambda j, i: jnp.maximum(i, j)
    return pl.pallas_call(
        body, name=name, grid=(n_pairs, nt, nt),
        in_specs=[pl.BlockSpec((t, wide), lambda p, j, i: (qi(j, i), p)),
                  pl.BlockSpec((t, wide), lambda p, j, i: (j, n_pairs + p)),
                  pl.BlockSpec((t, wide), lambda p, j, i: (j, 2 * n_pairs + p)),
                  pl.BlockSpec((hs, t, 1), lambda p, j, i: (p, j, 0)),
                  pl.BlockSpec((hs, 1, t), lambda p, j, i: (p, 0, qi(j, i))),
                  pl.BlockSpec((hs, 1, t), lambda p, j, i: (p, 0, qi(j, i))),
                  pl.BlockSpec((t, wide), lambda p, j, i: (qi(j, i), p)),
                  pl.BlockSpec((t, wide), lambda p, j, i: (qi(j, i), p))],
        out_specs=[pl.BlockSpec((3, t, wide), lambda p, j, i: (0, j, p)),
                   pl.BlockSpec((hs, t, 1), lambda p, j, i: (p, j, 0)),
                   pl.BlockSpec((hs, 1, t), lambda p, j, i: (p, 0, j))],
        out_shape=[jax.ShapeDtypeStruct((3, s, d), F32), jax.ShapeDtypeStruct((hs * n_pairs, s, 1), F32),
                   jax.ShapeDtypeStruct((hs * n_pairs, 1, s), F32)],
        scratch_shapes=[pltpu.VMEM((s, wide), F32), pltpu.VMEM((2, t, wide), F32), pltpu.VMEM((hs, t, 1), F32),
                        pltpu.VMEM((nt, hs, 1, t), F32)],
        compiler_params=_params(3),
    )(qkn, qkn, proj, f_col, f_row, lse_row, do, o)


def loss_head(y, target):
    s, d = y.shape
    tm = _row_tile(s)

    def body(y_ref, t_ref, dy_ref, loss_ref):
        @pl.when(pl.program_id(0) == 0)
        def _():
            loss_ref[...] = jnp.zeros_like(loss_ref)

        diff = y_ref[...] - t_ref[...]
        dy_ref[...] = diff * (1.0 / d)
        loss_ref[...] += 0.5 * jnp.sum(jnp.mean(diff * diff, axis=-1, keepdims=True), axis=0, keepdims=True)

    row = pl.BlockSpec((tm, d), lambda i: (i, 0))
    return pl.pallas_call(
        body, name="loss_head", grid=(s // tm,),
        in_specs=[row, row],
        out_specs=[row, pl.BlockSpec((8, LANES), lambda i: (0, 0))],
        out_shape=[jax.ShapeDtypeStruct((s, d), F32), jax.ShapeDtypeStruct((8, LANES), F32)],
        compiler_params=_params(1),
    )(y, target)


def ada_mod(c_all, w, b):
    n_layers, d, cols = w.shape

    def body(c_ref, w_ref, b_ref, o_ref):
        cv = c_ref[...]
        o_ref[...] = _nn(cv * _sigmoid(cv), w_ref[...]) + b_ref[...]

    return pl.pallas_call(
        body, name="ada_mod", grid=(n_layers,),
        in_specs=[pl.BlockSpec((N_DEV, d), lambda l: (0, 0)), pl.BlockSpec((None, d, cols), lambda l: (l, 0, 0)),
                  pl.BlockSpec((None, 1, cols), lambda l: (l, 0, 0))],
        out_specs=pl.BlockSpec((None, N_DEV, cols), lambda l: (l, 0, 0)),
        out_shape=jax.ShapeDtypeStruct((n_layers, N_DEV, cols), F32),
        compiler_params=_params(1),
    )(c_all, w, b)


def ada_grad(c_t, dmod):
    d = c_t.shape[0]
    n_layers, _, cols = dmod.shape
    tn = cols // 2

    def body(c_ref, dm_ref, o_ref):
        cv = c_ref[...]
        o_ref[...] = _nn(cv * _sigmoid(cv), dm_ref[...])

    return pl.pallas_call(
        body, name="ada_grad", grid=(n_layers, 2),
        in_specs=[pl.BlockSpec((d, N_DEV), lambda l, n: (0, 0)), pl.BlockSpec((None, N_DEV, tn), lambda l, n: (l, 0, n))],
        out_specs=pl.BlockSpec((None, d, tn), lambda l, n: (l, 0, n)),
        out_shape=jax.ShapeDtypeStruct((n_layers, d, cols), F32),
        compiler_params=_params(2),
    )(c_t, dmod)


def sum_devices(v):
    def body(v_ref, o_ref):
        acc = v_ref[0]
        for k in range(1, N_DEV):
            acc = acc + v_ref[k]
        o_ref[...] = acc

    return pl.pallas_call(body, name="sum_devices", out_shape=jax.ShapeDtypeStruct(v.shape[1:], F32))(v)


def sum_slots(r, own):
    _, rows, cols = r.shape
    tm = 256 if rows % 256 == 0 else rows

    def body(r_ref, own_ref, o_ref):
        o_ref[...] = ((own_ref[...].astype(F32) + r_ref[0].astype(F32)) + r_ref[1].astype(F32)) + r_ref[2].astype(F32)

    return pl.pallas_call(
        body, name="sum_slots", grid=(rows // tm,),
        in_specs=[pl.BlockSpec((N_CHIP - 1, tm, cols), lambda i: (0, i, 0)), pl.BlockSpec((tm, cols), lambda i: (i, 0))],
        out_specs=pl.BlockSpec((tm, cols), lambda i: (i, 0)),
        out_shape=jax.ShapeDtypeStruct((rows, cols), F32),
        compiler_params=_params(1),
    )(r, own)


def adamw(w, m, v, g, g2=None):
    rows, cols = w.shape
    tm = 256 if rows % 256 == 0 else rows
    two = g2 is not None
    c1 = 1.0 - ADAM_B1 ** ADAM_STEP
    c2 = 1.0 - ADAM_B2 ** ADAM_STEP

    def body(w_ref, m_ref, v_ref, g_ref, *rest):
        if two:
            g2_ref, go_ref, d_ref, mo_ref, vo_ref = rest
            gv = g_ref[...] + g2_ref[...]
        else:
            go_ref, d_ref, mo_ref, vo_ref = rest
            gv = g_ref[...]
        mn = ADAM_B1 * m_ref[...] + (1.0 - ADAM_B1) * gv
        vn = ADAM_B2 * v_ref[...] + (1.0 - ADAM_B2) * (gv * gv)
        go_ref[...] = gv
        mo_ref[...] = mn
        vo_ref[...] = vn
        d_ref[...] = -ADAM_LR * ((mn / c1) / (jnp.sqrt(vn / c2) + ADAM_EPS) + ADAM_WD * w_ref[...])

    blk = pl.BlockSpec((tm, cols), lambda i: (i, 0))
    ops = [w, m, v, g] + ([g2] if two else [])
    return pl.pallas_call(
        body, name="adamw", grid=(rows // tm,),
        in_specs=[blk] * len(ops), out_specs=[blk] * 4,
        out_shape=[jax.ShapeDtypeStruct((rows, cols), F32)] * 4,
        compiler_params=_params(1),
    )(*ops)


def _pad_rows(flat):
    n = flat.shape[0]
    rows = -(-n // LANES)
    return jnp.pad(flat, (0, rows * LANES - n)).reshape(rows, LANES)


def _pad_rows8(flat):
    rows = _pad_rows(flat)
    return jnp.pad(rows, ((0, -rows.shape[0] % 8), (0, 0)))


def _col_tiles(n):
    return next(k for k in range(1, n // LANES + 1) if n % (k * LANES) == 0 and n // k <= 1536)


def kernel(x, c, ada_w, ada_b, norm_g, ffn_w13, ffn_w2, rel_bias, swa_w_in, swa_w_out, swa_q_g, swa_k_g, swa_sink, fox_w_in, fox_w_out, fox_b_f, fox_q_g, fox_k_g, loss_target, m_ada_w, m_ada_b, m_norm_g, m_ffn_w13, m_ffn_w2, m_rel_bias, m_swa_w_in, m_swa_w_out, m_swa_q_g, m_swa_k_g, m_swa_sink, m_fox_w_in, m_fox_w_out, m_fox_b_f, m_fox_q_g, m_fox_k_g, v_ada_w, v_ada_b, v_norm_g, v_ffn_w13, v_ffn_w2, v_rel_bias, v_swa_w_in, v_swa_w_out, v_swa_q_g, v_swa_k_g, v_swa_sink, v_fox_w_in, v_fox_w_out, v_fox_b_f, v_fox_q_g, v_fox_k_g):
    ix, iy, ic = lax.axis_index("x"), lax.axis_index("y"), lax.axis_index("c")
    chip = 2 * ix + iy
    dev = 2 * chip + ic
    s, d = x.shape[1:]
    n_layers = ada_w.shape[0]
    n_a, n_b = swa_w_in.shape[0], fox_w_in.shape[0]
    hq = d // HEAD_DIM
    hkv = hq // GROUP
    kw = hkv * HEAD_DIM
    c13 = ffn_w13.shape[-1]
    f = 2 * c13
    r2 = ffn_w2.shape[2]
    cq = norm_g.shape[-1]
    a_in = d + 2 * kw
    fx = fox_w_in.shape[-1]
    b_in = N_CHIP * fx
    b_pad = 3 * d + LANES
    x0 = x[0]

    hello = _pad_rows8(jnp.concatenate([c.reshape(-1), norm_g.reshape(-1)]))
    hello_all = all_gather_rows(hello, "gather_c_norm").reshape(N_DEV, -1)
    c_all = hello_all[:, :d]
    ng = hello_all[::2, d:d + n_layers * 3 * cq].reshape(N_CHIP, n_layers, 3, cq)
    norm_full = jnp.moveaxis(ng, 0, 2).reshape(n_layers, 3, d)

    half_cols = ada_w.shape[-1] // 2
    w_half = lax.dynamic_slice_in_dim(ada_w, ic * half_cols, half_cols, axis=2)
    b_half = lax.dynamic_slice_in_dim(ada_b, dev * half_cols, half_cols, axis=1)[:, None, :]
    mod_part = ada_mod(c_all, w_half, b_half)
    mod_all = all_gather_rows(mod_part.reshape(n_layers * N_DEV, half_cols), "gather_mod")
    mod_all = mod_all.reshape(N_DEV, n_layers, N_DEV, half_cols)
    mod_mine = lax.dynamic_index_in_dim(mod_all, dev, axis=2, keepdims=False)
    mod_mine = jnp.moveaxis(mod_mine, 0, 1).reshape(n_layers, 3, 3, d)
    mv = jnp.concatenate([norm_full[:, :, None, :], mod_mine, jnp.zeros((n_layers, 3, 4, d), F32)], axis=2)

    def kinds_of(l):
        mixer_in = ("col", swa_w_in.shape[-1]) if l % 2 == 0 else ("slot",)
        return [("col", c13), ("row", r2), mixer_in, ("slot",)]

    def layer_buffers(l, after):
        raw = [ffn_w13[l], ffn_w2[l]] + ([swa_w_in[l // 2], swa_w_out[l // 2]] if l % 2 == 0 else
                                         [fox_w_in[l // 2], fox_w_out[l // 2]])
        return [cast_and_place(r, k, chip, after, f"place_weights_{l}_{t}") for t, (r, k) in enumerate(zip(raw, kinds_of(l)))]

    def layer_weights(l, fulls):
        w13_l, w2_l, w_in, w_out = fulls
        if l % 2 == 1:
            w_in = jnp.pad(jnp.concatenate([w_in[b] for b in range(N_CHIP)], axis=-1), ((0, 0), (0, b_pad - b_in)))
        return dict(w13=w13_l, w2=w2_l, w_in=w_in, w_out=w_out.reshape(d, d))

    weights = [layer_weights(0, gather_weights(layer_buffers(0, mv), kinds_of(0)))]
    in_flight = []
    for l in range(1, n_layers):
        in_flight.append(exchange_start([], layer_buffers(l, weights[0]["w13"]), kinds_of(l), True, f"gather_start_{l}"))
    for *_, token in in_flight:
        mv = mv + token[0, 0]

    bucket = jnp.asarray(_rel_bucket_table())
    bias = rel_bias_table(rel_bias, bucket)
    tm, tw = _row_tile(s), _wide_tile(s)
    n13 = 2 * f // c13
    na_t, nb_t = _col_tiles(a_in), _col_tiles(3 * d)
    wa_t, wb_t = a_in // na_t, 3 * d // nb_t
    gate_blk = 3 * d // LANES

    def ffn_forward(xv, l, half, sub):
        wg = weights[l]
        a, h = modmm(xv, mv, (l, sub), wg["w13"], (None, d, 2 * f), (half, 0, 0), 2 * f, c13, BF16,
                     f"ffn_up_{l}_{half}", True)
        xo, y, u = resmm(xv, mv, (l, sub), 0.5, a, wg["w2"], (None, f, d), (half, 0, 0), f, c13,
                         f"ffn_down_{l}_{half}", True)
        return xo, dict(x=xv, h=h, a=a, u=u, y=y)

    saved = []
    xv = x0
    for l in range(n_layers):
        j = l // 2
        if l >= 1:
            send_sems, recv_sems, shards, fulls, _ = in_flight[l - 1]
            weights.append(layer_weights(l, exchange_wait(send_sems, recv_sems, shards, fulls, xv, kinds_of(l), True,
                                                          f"gather_wait_{l}")[1]))
        wg = weights[l]
        xv, s0 = ffn_forward(xv, l, 0, 0)
        if l % 2 == 0:
            proj, h = modmm(xv, mv, (l, 1), wg["w_in"], (d, a_in), (0, 0), a_in, wa_t, BF16, f"swa_in_{j}", True)
            gains = jnp.concatenate([jnp.tile(swa_q_g[j] * HEAD_DIM ** -0.5, hq), jnp.tile(swa_k_g[j], hkv)])[None, :]
            qkn = qknorm_fwd(proj, gains, kw, f"swa_qknorm_{j}")
            sink = swa_sink[j][None, :]
            o, lse = swa_fwd(qkn, proj, bias, sink, d, f"swa_attn_{j}")
            s1 = dict(x=xv, h=h, proj=proj, gains=gains, qkn=qkn, sink=sink, o=o, lse=lse)
        else:
            proj, h = modmm(xv, mv, (l, 1), wg["w_in"], (d, 3 * d), (0, 0), 3 * d, wb_t, BF16, f"fox_in_{j}", True)
            fl, _ = modmm(xv, mv, (l, 1), wg["w_in"], (d, LANES), (0, gate_blk), LANES, LANES, F32,
                          f"fox_gate_in_{j}", False)
            b_f = jnp.pad(fox_b_f[j], (0, LANES - hq))[None, :]
            fcum = fox_gates(fl, b_f, f"fox_gates_{j}")
            f_t = fcum[:, :hq].T
            f_col, f_row = f_t[:, :, None], f_t[:, None, :]
            gains = jnp.concatenate([jnp.tile(fox_q_g[j] * HEAD_DIM ** -0.5, hq), jnp.tile(fox_k_g[j], hq)])[None, :]
            qkn = qknorm_fwd(proj, gains, d, f"fox_qknorm_{j}")
            o, o32, lse = fox_fwd(qkn, proj, f_col, f_row, d, f"fox_attn_{j}")
            s1 = dict(x=xv, h=h, proj=proj, gains=gains, qkn=qkn, fl=fl, b_f=b_f, f_col=f_col, f_row=f_row, o=o, o32=o32,
                      lse=lse)
        xv, y = resmm(xv, mv, (l, 1), 1.0, o, wg["w_out"], (d, d), (0, 0), d, d, f"mixer_out_{l}", False)
        s1["y"] = y
        xv, s2 = ffn_forward(xv, l, 1, 2)
        saved.append((s0, s1, s2))

    dxv, loss_part = loss_head(xv, loss_target[0])
    loss = lax.psum(loss_part[0, 0], ("x", "y", "c"))

    grads = [dict(w13=lax.empty((2, d, 2 * f), BF16), w2=lax.empty((2, f, d), BF16),
                  w_in=lax.empty((d, a_in if l % 2 == 0 else b_pad), BF16), w_out=lax.empty((d, d), BF16))
             for l in range(n_layers)]
    dmod = [[None] * 3 for _ in range(n_layers)]
    dnorm = [[None] * 3 for _ in range(n_layers)]
    dqk_gain = {}
    dsink, db_f, dbias_tabs = {}, {}, []

    def ffn_backward(dxo, sv, l, half, sub):
        wg, gg = weights[l], grads[l]
        dy, da, dgate = resmm_bwd(dxo, sv["y"], mv, (l, sub), 0.5, wg["w2"], (None, f, d), (half, 0, 0), f, c13,
                                  f"ffn_down_bwd_{l}_{half}", a=sv["a"])
        gg["w2"] = weight_grad(sv["u"], pl.BlockSpec((tw, c13), lambda m, n, k: (k, m)), dy,
                               pl.BlockSpec((tw, d), lambda m, n, k: (k, 0)), (f // c13, 1), s, c13, d, gg["w2"],
                               pl.BlockSpec((None, c13, d), lambda m, n, k: (half, m, 0)), f"ffn_w2_grad_{l}_{half}")
        gg["w13"] = weight_grad(sv["h"], pl.BlockSpec((tw, d), lambda m, n, k: (k, 0)), da,
                                pl.BlockSpec((tw, c13), lambda m, n, k: (k, n)), (1, n13), s, d, c13,
                                gg["w13"], pl.BlockSpec((None, d, c13), lambda m, n, k: (half, 0, n)),
                                f"ffn_w13_grad_{l}_{half}")
        dx, red = modmm_bwd(da, wg["w13"], (None, d, 2 * f), (half, 0, 0), 2 * f, c13, sv["x"], dxo, mv,
                            (l, sub), f"ffn_up_bwd_{l}_{half}")
        dmod[l][sub] = (red[1], red[2], dgate[0])
        dnorm[l][sub] = red[0]
        return dx

    def layer_slabs(l):
        gg = grads[l]
        g_in = gg["w_in"] if l % 2 == 0 else jnp.stack([gg["w_in"][:, b * fx:(b + 1) * fx] for b in range(N_CHIP)])
        return [gg["w13"], gg["w2"], g_in, gg["w_out"].reshape(N_CHIP, d // N_CHIP, d)]

    def shard_shapes_of(l):
        mixer = (swa_w_in, swa_w_out) if l % 2 == 0 else (fox_w_in, fox_w_out)
        return [ffn_w13.shape[1:], ffn_w2.shape[1:], mixer[0].shape[1:], mixer[1].shape[1:]]

    scatter_in_flight = {}

    for l in reversed(range(n_layers)):
        j = l // 2
        wg, gg = weights[l], grads[l]
        s0, s1, s2 = saved[l]
        dxv = ffn_backward(dxv, s2, l, 1, 2)
        is_a = l % 2 == 0
        w_out = wg["w_out"]
        dy, do, dgate = resmm_bwd(dxv, s1["y"], mv, (l, 1), 1.0, w_out, (d, d), (0, 0), d, d, f"mixer_out_bwd_{l}")
        gg["w_out"] = weight_grad(s1["o"], pl.BlockSpec((tw, d), lambda m, n, k: (k, 0)), dy,
                                  pl.BlockSpec((tw, d), lambda m, n, k: (k, 0)), (1, 1), s, d, d, gg["w_out"],
                                  pl.BlockSpec((d, d), lambda m, n, k: (0, 0)), f"mixer_out_grad_{l}")
        if is_a:
            d_qkv, dbias_tab, dsk = swa_bwd(s1["qkn"], s1["proj"], bias, s1["sink"], do, s1["o"], s1["lse"], d,
                                            f"swa_attn_bwd_{j}")
            dbias_tabs.append(dbias_tab)
            dsink[j] = dsk[0, :hq]
            dproj, dgain = qknorm_bwd(s1["proj"], s1["gains"], d_qkv, pl.BlockSpec((tm, kw), lambda c_, i: (i, c_)),
                                      a_in, kw, f"swa_qknorm_bwd_{j}")
            gg["w_in"] = weight_grad(s1["h"], pl.BlockSpec((tw, d), lambda m, n, k: (k, 0)), dproj,
                                     pl.BlockSpec((tw, wa_t), lambda m, n, k: (k, n)), (1, na_t), s, d, wa_t, gg["w_in"],
                                     pl.BlockSpec((d, wa_t), lambda m, n, k: (0, n)), f"swa_in_grad_{j}")
            dxv, red = modmm_bwd(dproj, wg["w_in"], (d, a_in), (0, 0), a_in, wa_t, s1["x"], dxv, mv, (l, 1),
                                 f"swa_in_bwd_{j}")
            dqk_gain[("a", j)] = (dgain[0, :d].reshape(hq, HEAD_DIM).sum(0) * HEAD_DIM ** -0.5,
                                  dgain[0, d:d + kw].reshape(hkv, HEAD_DIM).sum(0))
        else:
            lse_row = s1["lse"].reshape(hq, 1, s)
            d_qkv, df_col, dfq_row = fox_bwd(s1["qkn"], s1["proj"], s1["f_col"], s1["f_row"], lse_row, do, s1["o32"], d,
                                             f"fox_attn_bwd_{j}")
            lanes_of_heads = lambda a: jnp.pad(a.T, ((0, 0), (0, LANES - hq)))
            dfl, dbf = fox_gates_bwd(s1["fl"], s1["b_f"], lanes_of_heads(dfq_row[:, 0, :]), lanes_of_heads(df_col[:, :, 0]),
                                     f"fox_gates_bwd_{j}")
            db_f[j] = dbf[0, :hq]
            dproj, dgain = qknorm_bwd(s1["proj"], s1["gains"], d_qkv, pl.BlockSpec((None, tm, d), lambda c_, i: (c_, i, 0)),
                                      3 * d, d, f"fox_qknorm_bwd_{j}")
            gg["w_in"] = weight_grad(s1["h"], pl.BlockSpec((tw, d), lambda m, n, k: (k, 0)), dproj,
                                     pl.BlockSpec((tw, wb_t), lambda m, n, k: (k, n)), (1, nb_t), s, d, wb_t, gg["w_in"],
                                     pl.BlockSpec((d, wb_t), lambda m, n, k: (0, n)), f"fox_in_grad_{j}")
            gg["w_in"] = weight_grad(s1["h"], pl.BlockSpec((tw, d), lambda m, n, k: (k, 0)), dfl,
                                     pl.BlockSpec((tw, LANES), lambda m, n, k: (k, 0)), (1, 1), s, d, LANES, gg["w_in"],
                                     pl.BlockSpec((d, LANES), lambda m, n, k: (0, gate_blk)), f"fox_gate_in_grad_{j}")
            dxv, red = modmm_bwd(dproj, wg["w_in"], (d, 3 * d), (0, 0), 3 * d, wb_t, s1["x"], dxv, mv, (l, 1),
                                 f"fox_in_bwd_{j}", more=(dfl, (d, LANES), (0, gate_blk)))
            dqk_gain[("b", j)] = (dgain[0, :d].reshape(hq, HEAD_DIM).sum(0) * HEAD_DIM ** -0.5,
                                  dgain[0, d:2 * d].reshape(hq, HEAD_DIM).sum(0))
        dmod[l][1] = (red[1], red[2], dgate[0])
        dnorm[l][1] = red[0]
        dxv = ffn_backward(dxv, s0, l, 0, 0)
        if l >= 1:
            lands = [lax.empty((N_CHIP - 1,) + tuple(shp), BF16) for shp in shard_shapes_of(l)]
            scatter_in_flight[l] = exchange_start(layer_slabs(l), lands, kinds_of(l), False, f"scatter_start_{l}")
            mv = mv + scatter_in_flight[l][-1][0, 0]
    grad_x = dxv[None]

    drel = rel_bias_grad(jnp.stack(dbias_tabs), bucket)
    dmod_flat = jnp.stack([jnp.stack([jnp.stack(dmod[l][sub]) for sub in range(3)]) for l in range(n_layers)]).reshape(-1)
    dnorm_flat = jnp.stack([jnp.stack(dnorm[l]) for l in range(n_layers)]).reshape(-1)
    pieces = [dmod_flat, dnorm_flat,
              jnp.stack([dqk_gain[("a", j)][0] for j in range(n_a)]).reshape(-1),
              jnp.stack([dqk_gain[("a", j)][1] for j in range(n_a)]).reshape(-1),
              jnp.stack([dqk_gain[("b", j)][0] for j in range(n_b)]).reshape(-1),
              jnp.stack([dqk_gain[("b", j)][1] for j in range(n_b)]).reshape(-1),
              jnp.stack([dsink[j] for j in range(n_a)]).reshape(-1),
              jnp.stack([db_f[j] for j in range(n_b)]).reshape(-1),
              drel.reshape(-1)]
    rows = [_pad_rows(p) for p in pieces]
    starts = np.cumsum([0] + [r.shape[0] for r in rows])
    total = -(-int(starts[-1]) // 8) * 8
    small = jnp.pad(jnp.concatenate(rows), ((0, total - int(starts[-1])), (0, 0)))
    small_all = all_gather_rows(small, "gather_small_grads").reshape(N_DEV, total, LANES)
    small_sum = sum_devices(small_all)

    def piece(k, shape):
        n = int(np.prod(shape))
        return small_sum[int(starts[k]):int(starts[k + 1])].reshape(-1)[:n].reshape(shape)

    g_ada_b = piece(0, (n_layers, 9 * d))
    g_norm = lax.dynamic_slice_in_dim(piece(1, (n_layers, 3, d)), chip * cq, cq, axis=2)
    g_swa_q, g_swa_k = piece(2, (n_a, HEAD_DIM)), piece(3, (n_a, HEAD_DIM))
    g_fox_q, g_fox_k = piece(4, (n_b, HEAD_DIM)), piece(5, (n_b, HEAD_DIM))
    g_sink, g_bf, g_rel = piece(6, (n_a, hq)), piece(7, (n_b, hq)), piece(8, (REL_BUCKETS, hq))

    dmod_all = small_all[:, :int(starts[1])].reshape(N_DEV, -1)[:, :n_layers * 9 * d].reshape(N_DEV, n_layers, 9 * d)
    ada_cols = ada_w.shape[-1]
    dmod_mine = lax.dynamic_slice_in_dim(jnp.moveaxis(dmod_all, 0, 1), chip * ada_cols, ada_cols, axis=2)
    g_ada_w = ada_grad(c_all.T, dmod_mine)

    sources = {0: layer_slabs(0)}
    landed = {0: scatter_grads(sources[0], kinds_of(0), shard_shapes_of(0))}
    for l, (send_sems, recv_sems, slabs, lands, _) in scatter_in_flight.items():
        sources[l], landed[l] = exchange_wait(send_sems, recv_sems, slabs, lands, landed[0][0], kinds_of(l), False,
                                              f"scatter_wait_{l}")

    def layer_sum(l, t):
        shape = shard_shapes_of(l)[t]
        own = own_slab(sources[l][t], kinds_of(l)[t], chip, shape)
        return sum_slots(landed[l][t].reshape(N_CHIP - 1, -1, shape[-1]), own.reshape(-1, shape[-1]))

    every, even, odd = range(n_layers), range(0, n_layers, 2), range(1, n_layers, 2)
    parts = [jnp.concatenate([layer_sum(l, t) for l in layers])
             for layers, t in [(every, 0), (every, 1), (even, 2), (even, 3), (odd, 2), (odd, 3)]]
    others = swap_with_sibling(parts)

    def update(w, m, v, g, g2=None):
        w2d = w.reshape(-1, w.shape[-1])
        outs = adamw(w2d, m.reshape(w2d.shape), v.reshape(w2d.shape), g.reshape(w2d.shape) if g2 is None else g, g2)
        return [t.reshape(w.shape) for t in outs]

    big = [(ffn_w13, m_ffn_w13, v_ffn_w13), (ffn_w2, m_ffn_w2, v_ffn_w2), (swa_w_in, m_swa_w_in, v_swa_w_in),
           (swa_w_out, m_swa_w_out, v_swa_w_out), (fox_w_in, m_fox_w_in, v_fox_w_in), (fox_w_out, m_fox_w_out, v_fox_w_out)]
    big_out = [update(w, m, v, p, q) for (w, m, v), p, q in zip(big, parts, others)]
    r_ada_w = update(ada_w, m_ada_w, v_ada_w, g_ada_w)
    r_ada_b = update(ada_b, m_ada_b, v_ada_b, g_ada_b)
    r_norm = update(norm_g, m_norm_g, v_norm_g, g_norm)
    r_rel = update(rel_bias, m_rel_bias, v_rel_bias, g_rel)
    r_swa_q = update(swa_q_g, m_swa_q_g, v_swa_q_g, g_swa_q)
    r_swa_k = update(swa_k_g, m_swa_k_g, v_swa_k_g, g_swa_k)
    r_sink = update(swa_sink, m_swa_sink, v_swa_sink, g_sink)
    r_bf = update(fox_b_f, m_fox_b_f, v_fox_b_f, g_bf)
    r_fox_q = update(fox_q_g, m_fox_q_g, v_fox_q_g, g_fox_q)
    r_fox_k = update(fox_k_g, m_fox_k_g, v_fox_k_g, g_fox_k)
    per_weight = [r_ada_w, r_ada_b, r_norm, big_out[0], big_out[1], r_rel, big_out[2], big_out[3], r_swa_q, r_swa_k,
                  r_sink, big_out[4], big_out[5], r_bf, r_fox_q, r_fox_k]
    return (loss, grad_x, *[r[0] for r in per_weight], *[r[1] for r in per_weight],
            *[r[2] for r in per_weight], *[r[3] for r in per_weight])
```

```python
import math

import numpy as np
import jax
import jax.numpy as jnp
from jax import lax
from jax.experimental import pallas as pl
from jax.experimental.pallas import tpu as pltpu

F32 = jnp.float32
BF16 = jnp.bfloat16
HEAD_DIM = 64
GROUP = 4
FOX_HEADS = 8
BLOCK = 128
REL_BUCKETS = 32
REL_MAX_DIST = 128
EPS = 1e-6
NEG = -1e30
N_CHIP = 4
N_DEV = 8
LANES = 128
VMEM_LIMIT = 52 * 1024 * 1024
ADAM_LR, ADAM_B1, ADAM_B2, ADAM_EPS, ADAM_WD, ADAM_STEP = 0.001, 0.9, 0.999, 1e-08, 0.01, 10
MESH = pl.DeviceIdType.MESH
ANY = pl.BlockSpec(memory_space=pl.ANY)


def _params(n_axes):
    return pltpu.CompilerParams(dimension_semantics=("arbitrary",) * n_axes, vmem_limit_bytes=VMEM_LIMIT)


def _nn(a, b):
    return jnp.dot(a, b, preferred_element_type=F32)


def _nt(a, b):
    return lax.dot_general(a, b, (((1,), (1,)), ((), ())), preferred_element_type=F32)


def _tn(a, b):
    return lax.dot_general(a, b, (((0,), (0,)), ((), ())), preferred_element_type=F32)


def _sigmoid(z):
    return 1.0 / (1.0 + jnp.exp(-z))


def _sigmoid_fast(z):
    return pl.reciprocal(1.0 + jnp.exp(-z), approx=True)


def _row_tile(s):
    return 512 if s >= 2048 else s // 2


def _wide_tile(s):
    return 1024 if s >= 2048 else s // 2


def _attn_tile(s):
    return 512 if s >= 2048 else s // 4


def _position():
    x, y, c = lax.axis_index("x"), lax.axis_index("y"), lax.axis_index("c")
    chips = [(1 - x, y), (x, 1 - y), (1 - x, 1 - y)]
    return x, y, c, chips


def all_gather_rows(v, name):
    m_per, n = v.shape

    def body(x_ref, out_ref, send_sems, recv_sems, local_sem):
        x, y, c, chips = _position()
        me, sibling = (x, y, c), (x, y, 1 - c)

        def rows(px, py, pc):
            return out_ref.at[pl.ds((4 * px + 2 * py + pc) * m_per, m_per), :]

        def copy(k, block, to, src=None):
            return pltpu.make_async_remote_copy(
                src_ref=rows(*block) if src is None else src, dst_ref=rows(*block),
                send_sem=send_sems.at[k], recv_sem=recv_sems.at[k], device_id=to, device_id_type=MESH)

        mine = pltpu.make_async_copy(x_ref, rows(*me), local_sem)
        mine.start()
        first = [copy(0, me, sibling, src=x_ref)]
        first += [copy(1 + j, me, (*chip, c), src=x_ref) for j, chip in enumerate(chips)]
        for cp in first:
            cp.start()
        passed = [copy(4 + j, (*chip, c), sibling) for j, chip in enumerate(chips)]
        for j, chip in enumerate(chips):
            copy(1 + j, (*chip, c), me).wait_recv()
            passed[j].start()
        copy(0, sibling, me).wait_recv()
        for j, chip in enumerate(chips):
            copy(4 + j, (*chip, 1 - c), me).wait_recv()
        for cp in first + passed:
            cp.wait_send()
        mine.wait()

    return pl.pallas_call(
        body, name=name,
        out_shape=jax.ShapeDtypeStruct((N_DEV * m_per, n), v.dtype),
        in_specs=[pl.BlockSpec(memory_space=pltpu.VMEM)],
        out_specs=pl.BlockSpec(memory_space=pltpu.VMEM),
        scratch_shapes=[pltpu.SemaphoreType.DMA((7,)), pltpu.SemaphoreType.DMA((7,)), pltpu.SemaphoreType.DMA],
    )(v)


def _slab(full_ref, kind, b, lead):
    how = kind[0]
    if how == "slot":
        return full_ref.at[b, lead]
    if how == "col":
        w = kind[1]
        idx = (lead,) + (slice(None),) * (len(full_ref.shape) - 2) + (pl.ds(pl.multiple_of(b * w, LANES), w),)
        return full_ref.at[idx]
    h = kind[1]
    idx = (lead,) + (slice(None),) * (len(full_ref.shape) - 3) + (pl.ds(pl.multiple_of(b * h, 8), h), slice(None))
    return full_ref.at[idx]


def _full_shape(shard_shape, kind):
    if kind[0] == "slot":
        return (N_CHIP,) + tuple(shard_shape)
    if kind[0] == "col":
        return tuple(shard_shape[:-1]) + (N_CHIP * shard_shape[-1],)
    return tuple(shard_shape[:-2]) + (N_CHIP * shard_shape[-2], shard_shape[-1])


def _slab_start(shape, kind, b):
    zeros = [0] * len(shape)
    if kind[0] == "slot":
        return [b] + zeros[1:]
    if kind[0] == "col":
        return zeros[:-1] + [b * kind[1]]
    return zeros[:-2] + [b * kind[1], 0]


def cast_and_place(shard, kind, chip, after, name):
    rows, cols = shard.shape[-2:]
    lead = int(np.prod(shard.shape[:-2]))
    if kind[0] == "col":
        full3, where = (lead, rows, N_CHIP * cols), lambda p, b: (p, 0, b[0])
    elif kind[0] == "row":
        full3, where = (lead, N_CHIP * rows, cols), lambda p, b: (p, b[0], 0)
    else:
        full3, where = (N_CHIP * lead, rows, cols), lambda p, b: (b[0] * lead + p, 0, 0)

    def body(b_ref, s_ref, after_ref, o_ref):
        o_ref[...] = s_ref[...].astype(BF16)

    full = pl.pallas_call(
        body, name=name,
        grid_spec=pltpu.PrefetchScalarGridSpec(
            num_scalar_prefetch=1, grid=(lead,),
            in_specs=[pl.BlockSpec((1, rows, cols), lambda p, b: (p, 0, 0)), ANY],
            out_specs=pl.BlockSpec((1, rows, cols), where)),
        out_shape=jax.ShapeDtypeStruct(full3, BF16),
        compiler_params=_params(1),
    )(jnp.reshape(chip, (1,)).astype(jnp.int32), shard.reshape(lead, rows, cols), after)
    return full.reshape(_full_shape(shard.shape, kind))


def own_slab(full, kind, b, shard_shape):
    sizes = (1,) + tuple(shard_shape) if kind[0] == "slot" else tuple(shard_shape)
    return lax.dynamic_slice(full, _slab_start(full.shape, kind, b), sizes).reshape(shard_shape)


def gather_weights(fulls, kinds):
    n = len(fulls)

    def body(*refs):
        outs = refs[n:2 * n]
        send_sems, recv_sems = refs[2 * n:]
        x, y, c, chips = _position()
        b_me = 2 * x + y
        sibling = (x, y, 1 - c)
        sends = []

        def halves(t):
            lead = outs[t].shape[1] if kinds[t][0] == "slot" else outs[t].shape[0]
            return pl.ds(c * (lead // 2), lead // 2), pl.ds((1 - c) * (lead // 2), lead // 2)

        for t in range(n):
            mine, _ = halves(t)
            own = _slab(outs[t], kinds[t], b_me, mine)
            for j, chip in enumerate(chips):
                cp = pltpu.make_async_remote_copy(
                    src_ref=own, dst_ref=own,
                    send_sem=send_sems.at[6 * t + j], recv_sem=recv_sems.at[6 * t + j],
                    device_id=(*chip, c), device_id_type=MESH)
                cp.start()
                sends.append(cp)
        for t in range(n):
            mine, _ = halves(t)
            for j, chip in enumerate(chips):
                landed = _slab(outs[t], kinds[t], 2 * chip[0] + chip[1], mine)
                pltpu.make_async_remote_copy(
                    src_ref=landed, dst_ref=landed, send_sem=send_sems.at[6 * t + j], recv_sem=recv_sems.at[6 * t + j],
                    device_id=(*chip, c), device_id_type=MESH).wait_recv()
                cp = pltpu.make_async_remote_copy(
                    src_ref=landed, dst_ref=landed, send_sem=send_sems.at[6 * t + 3 + j],
                    recv_sem=recv_sems.at[6 * t + 3 + j], device_id=sibling, device_id_type=MESH)
                cp.start()
                sends.append(cp)
        for t in range(n):
            _, theirs = halves(t)
            for j, chip in enumerate(chips):
                landed = _slab(outs[t], kinds[t], 2 * chip[0] + chip[1], theirs)
                pltpu.make_async_remote_copy(
                    src_ref=landed, dst_ref=landed, send_sem=send_sems.at[6 * t + 3 + j],
                    recv_sem=recv_sems.at[6 * t + 3 + j], device_id=sibling, device_id_type=MESH).wait_recv()
        for cp in sends:
            cp.wait_send()

    return pl.pallas_call(
        body, name="gather_weights",
        out_shape=[jax.ShapeDtypeStruct(v.shape, v.dtype) for v in fulls],
        in_specs=[ANY] * n, out_specs=[ANY] * n,
        input_output_aliases={t: t for t in range(n)},
        scratch_shapes=[pltpu.SemaphoreType.DMA((6 * n,)), pltpu.SemaphoreType.DMA((6 * n,))],
    )(*fulls)


def scatter_grads(grads, kinds, shard_shapes):
    n = len(grads)

    def body(*refs):
        ins, outs = refs[:n], refs[n:2 * n]
        send_sems, recv_sems = refs[2 * n:]
        x, y, c, chips = _position()
        sends = []
        for t in range(n):
            whole = _whole(ins[t], kinds[t])
            for j, chip in enumerate(chips):
                cp = pltpu.make_async_remote_copy(
                    src_ref=_slab(ins[t], kinds[t], 2 * chip[0] + chip[1], whole), dst_ref=outs[t].at[j],
                    send_sem=send_sems.at[3 * t + j], recv_sem=recv_sems.at[3 * t + j],
                    device_id=(*chip, c), device_id_type=MESH)
                cp.start()
                sends.append(cp)
        for t in range(n):
            for j, chip in enumerate(chips):
                pltpu.make_async_remote_copy(
                    src_ref=outs[t].at[j], dst_ref=outs[t].at[j], send_sem=send_sems.at[3 * t + j],
                    recv_sem=recv_sems.at[3 * t + j], device_id=(*chip, c), device_id_type=MESH).wait_recv()
        for cp in sends:
            cp.wait_send()

    return pl.pallas_call(
        body, name="scatter_grads",
        out_shape=[jax.ShapeDtypeStruct((N_CHIP - 1,) + tuple(s), g.dtype) for g, s in zip(grads, shard_shapes)],
        in_specs=[ANY] * n, out_specs=[ANY] * n,
        scratch_shapes=[pltpu.SemaphoreType.DMA((3 * n,)), pltpu.SemaphoreType.DMA((3 * n,))],
    )(*grads)


def _whole(ref, kind):
    return pl.ds(0, ref.shape[1] if kind[0] == "slot" else ref.shape[0])


def _exchange_copies(srcs, lands, kinds, gather, send_sems, recv_sems):
    x, y, c, chips = _position()
    b_me = 2 * x + y
    out = []
    for t in range(len(lands)):
        for j, chip in enumerate(chips):
            b_j = 2 * chip[0] + chip[1]
            if gather:
                src = sent_to = _slab(lands[t], kinds[t], b_me, _whole(lands[t], kinds[t]))
                arrives = _slab(lands[t], kinds[t], b_j, _whole(lands[t], kinds[t]))
            else:
                src = _slab(srcs[t], kinds[t], b_j, _whole(srcs[t], kinds[t]))
                sent_to = arrives = lands[t].at[j]
            k = 3 * t + j
            send = pltpu.make_async_remote_copy(src_ref=src, dst_ref=sent_to, send_sem=send_sems.at[k],
                                                recv_sem=recv_sems.at[k], device_id=(*chip, c), device_id_type=MESH)
            recv = pltpu.make_async_remote_copy(src_ref=src, dst_ref=arrives, send_sem=send_sems.at[k],
                                                recv_sem=recv_sems.at[k], device_id=(*chip, c), device_id_type=MESH)
            out.append((send, recv))
    return out


def exchange_start(srcs, lands, kinds, gather, name):
    ns, nl = len(srcs), len(lands)
    hbm = pl.BlockSpec(memory_space=pltpu.HBM)

    def body(*refs):
        ins, lnd = refs[:ns], refs[ns:ns + nl]
        send_sems, recv_sems = refs[ns + nl], refs[ns + nl + 1]
        token = refs[-1]
        for send, _ in _exchange_copies(ins, lnd, kinds, gather, send_sems, recv_sems):
            send.start()
        token[...] = jnp.zeros_like(token)

    ops = [pltpu.with_memory_space_constraint(v, pltpu.HBM) for v in (*srcs, *lands)]
    res = pl.pallas_call(
        body, name=name,
        out_shape=(pltpu.SemaphoreType.DMA((3 * nl,)), pltpu.SemaphoreType.DMA((3 * nl,)),
                   *[pltpu.HBM(v.shape, v.dtype) for v in ops], jax.ShapeDtypeStruct((8, LANES), F32)),
        in_specs=[hbm] * (ns + nl),
        out_specs=(pl.BlockSpec(memory_space=pltpu.SEMAPHORE), pl.BlockSpec(memory_space=pltpu.SEMAPHORE),
                   *[hbm] * (ns + nl), pl.BlockSpec(memory_space=pltpu.VMEM)),
        input_output_aliases={t: 2 + t for t in range(ns + nl)},
        compiler_params=pltpu.CompilerParams(has_side_effects=pltpu.SideEffectType.DATAFLOW_SIDE_EFFECTING),
    )(*ops)
    return res[0], res[1], list(res[2:2 + ns]), list(res[2 + ns:2 + ns + nl]), res[-1]


def exchange_wait(send_sems, recv_sems, srcs, lands, after, kinds, gather, name):
    ns, nl = len(srcs), len(lands)
    hbm = pl.BlockSpec(memory_space=pltpu.HBM)

    def body(*refs):
        ins, lnd = refs[:ns], refs[ns:ns + nl]
        ssem, rsem = refs[ns + nl], refs[ns + nl + 1]
        for send, recv in _exchange_copies(ins, lnd, kinds, gather, ssem, rsem):
            send.wait_send()
            recv.wait_recv()

    res = pl.pallas_call(
        body, name=name,
        out_shape=tuple(pltpu.HBM(v.shape, v.dtype) for v in (*srcs, *lands)),
        in_specs=[hbm] * (ns + nl) + [pl.BlockSpec(memory_space=pltpu.SEMAPHORE)] * 2 + [ANY],
        out_specs=tuple([hbm] * (ns + nl)),
        input_output_aliases={t: t for t in range(ns + nl)},
        compiler_params=pltpu.CompilerParams(has_side_effects=pltpu.SideEffectType.DATAFLOW_SIDE_EFFECTING),
    )(*srcs, *lands, send_sems, recv_sems, after)
    return list(res[:ns]), list(res[ns:])


def swap_with_sibling(parts):
    n = len(parts)

    def body(*refs):
        ins, outs = refs[:n], refs[n:2 * n]
        send_sems, recv_sems = refs[2 * n:]
        x, y, c, _ = _position()
        cps = []
        for t in range(n):
            cp = pltpu.make_async_remote_copy(
                src_ref=ins[t], dst_ref=outs[t], send_sem=send_sems.at[t], recv_sem=recv_sems.at[t],
                device_id=(x, y, 1 - c), device_id_type=MESH)
            cp.start()
            cps.append(cp)
        for cp in cps:
            cp.wait_recv()
        for cp in cps:
            cp.wait_send()

    return pl.pallas_call(
        body, name="swap_with_sibling",
        out_shape=[jax.ShapeDtypeStruct(p.shape, p.dtype) for p in parts],
        in_specs=[ANY] * n, out_specs=[ANY] * n,
        scratch_shapes=[pltpu.SemaphoreType.DMA((n,)), pltpu.SemaphoreType.DMA((n,))],
    )(*parts)


def _modulated(xv, mv_ref):
    g, shift, scale = mv_ref[0:1, :], mv_ref[1:2, :], mv_ref[2:3, :]
    r = lax.rsqrt(jnp.mean(xv * xv, axis=-1, keepdims=True) + EPS)
    xhat = xv * r
    xn = xhat * g
    return xn * (1.0 + scale) + shift, xhat, xn, r, g, scale


def _resident(block_shape, index):
    return pl.BlockSpec(block_shape, lambda i: index, pipeline_mode=pl.Buffered(1))


def modmm(x, mv, mv_idx, w, w_block, w_index, n_cols, chunk, out_dtype, name, want_h):
    s, d = x.shape
    tm = _row_tile(s)

    def body(x_ref, mv_ref, w_ref, out_ref, *rest):
        h = _modulated(x_ref[...], mv_ref)[0].astype(BF16)
        if want_h:
            rest[0][...] = h
        for n in range(n_cols // chunk):
            cols = slice(n * chunk, (n + 1) * chunk)
            out_ref[:, cols] = _nn(h, w_ref[:, cols]).astype(out_dtype)

    row = pl.BlockSpec((tm, d), lambda i: (i, 0))
    out_shapes = [jax.ShapeDtypeStruct((s, n_cols), out_dtype)]
    out_specs = [pl.BlockSpec((tm, n_cols), lambda i: (i, 0))]
    if want_h:
        out_shapes.append(jax.ShapeDtypeStruct((s, d), BF16))
        out_specs.append(row)
    res = pl.pallas_call(
        body, name=name, grid=(s // tm,),
        in_specs=[row, pl.BlockSpec((None, None, 8, d), lambda i: (*mv_idx, 0, 0)), _resident(w_block, w_index)],
        out_specs=out_specs, out_shape=out_shapes,
        compiler_params=_params(1),
    )(x, mv, w)
    return res if want_h else (res[0], None)


def resmm(x, mv, mv_idx, coef, lhs, w, w_block, w_index, kdim, chunk, name, ffn):
    s, d = x.shape
    tm = _row_tile(s)
    lhs_cols = 2 * kdim if ffn else kdim

    def body(x_ref, mv_ref, lhs_ref, w_ref, xo_ref, y_ref, *rest):
        y = jnp.zeros((tm, d), F32)
        for k in range(kdim // chunk):
            rows = slice(k * chunk, (k + 1) * chunk)
            if ffn:
                ag = lhs_ref[:, rows].astype(F32)
                au = lhs_ref[:, kdim + k * chunk:kdim + (k + 1) * chunk].astype(F32)
                left = (ag * _sigmoid_fast(ag) * au).astype(BF16)
                rest[0][:, rows] = left
            else:
                left = lhs_ref[:, rows]
            y = y + _nn(left, w_ref[rows, :])
        y_ref[...] = y.astype(BF16)
        xo_ref[...] = x_ref[...] + (coef * mv_ref[3:4, :]) * y

    row = pl.BlockSpec((tm, d), lambda i: (i, 0))
    out_shapes = [jax.ShapeDtypeStruct((s, d), F32), jax.ShapeDtypeStruct((s, d), BF16)]
    out_specs = [row, row]
    if ffn:
        out_shapes.append(jax.ShapeDtypeStruct((s, kdim), BF16))
        out_specs.append(pl.BlockSpec((tm, kdim), lambda i: (i, 0)))
    return pl.pallas_call(
        body, name=name, grid=(s // tm,),
        in_specs=[row, pl.BlockSpec((None, None, 8, d), lambda i: (*mv_idx, 0, 0)),
                  pl.BlockSpec((tm, lhs_cols), lambda i: (i, 0)), _resident(w_block, w_index)],
        out_specs=out_specs, out_shape=out_shapes,
        compiler_params=_params(1),
    )(x, mv, lhs, w)


def resmm_bwd(dxo, y, mv, mv_idx, coef, w, w_block, w_index, kdim, chunk, name, a=None):
    s, d = dxo.shape
    ffn = a is not None
    tm = _row_tile(s) // 2 if ffn else _row_tile(s)
    dl_cols = 2 * kdim if ffn else kdim

    def body(dxo_ref, y_ref, mv_ref, w_ref, *rest):
        if ffn:
            a_ref, dy_ref, dl_ref, dgate_ref = rest
        else:
            dy_ref, dl_ref, dgate_ref = rest

        @pl.when(pl.program_id(0) == 0)
        def _():
            dgate_ref[...] = jnp.zeros_like(dgate_ref)

        dxv = dxo_ref[...]
        dy = ((coef * mv_ref[3:4, :]) * dxv).astype(BF16)
        dy_ref[...] = dy
        dgate_ref[0:1, :] += jnp.sum(coef * dxv * y_ref[...].astype(F32), axis=0, keepdims=True)
        for k in range(kdim // chunk):
            rows = slice(k * chunk, (k + 1) * chunk)
            dl = _nt(dy, w_ref[rows, :])
            if ffn:
                ups = slice(kdim + k * chunk, kdim + (k + 1) * chunk)
                ag = a_ref[:, rows].astype(F32)
                au = a_ref[:, ups].astype(F32)
                sg = _sigmoid_fast(ag)
                dl_ref[:, rows] = (dl * au * (sg * (1.0 + ag * (1.0 - sg)))).astype(BF16)
                dl_ref[:, ups] = (dl * (ag * sg)).astype(BF16)
            else:
                dl_ref[:, rows] = dl.astype(BF16)

    row = pl.BlockSpec((tm, d), lambda i: (i, 0))
    wide = pl.BlockSpec((tm, dl_cols), lambda i: (i, 0))
    in_specs = [row, row, pl.BlockSpec((None, None, 8, d), lambda i: (*mv_idx, 0, 0)), _resident(w_block, w_index)]
    ops = [dxo, y, mv, w]
    if ffn:
        in_specs.append(wide)
        ops.append(a)
    return pl.pallas_call(
        body, name=name, grid=(s // tm,),
        in_specs=in_specs,
        out_specs=[row, wide, pl.BlockSpec((8, d), lambda i: (0, 0))],
        out_shape=[jax.ShapeDtypeStruct((s, d), BF16), jax.ShapeDtypeStruct((s, dl_cols), BF16),
                   jax.ShapeDtypeStruct((8, d), F32)],
        compiler_params=_params(1),
    )(*ops)


def modmm_bwd(dl, w, w_block, w_index, n_cols, chunk, x, dxo, mv, mv_idx, name, more=None):
    s, d = x.shape
    tm = _row_tile(s)

    def body(dl_ref, w_ref, x_ref, dxo_ref, mv_ref, *rest):
        if more is not None:
            dl2_ref, w2_ref, dx_ref, red_ref = rest
            dh = _nt(dl2_ref[...], w2_ref[...])
        else:
            dx_ref, red_ref = rest
            dh = jnp.zeros((tm, d), F32)

        @pl.when(pl.program_id(0) == 0)
        def _():
            red_ref[...] = jnp.zeros_like(red_ref)

        for n in range(n_cols // chunk):
            cols = slice(n * chunk, (n + 1) * chunk)
            dh = dh + _nt(dl_ref[:, cols], w_ref[:, cols])
        _, xhat, xn, r, g, scale = _modulated(x_ref[...], mv_ref)
        dxn = dh * (1.0 + scale)
        red_ref[0:1, :] += jnp.sum(dxn * xhat, axis=0, keepdims=True)
        red_ref[1:2, :] += jnp.sum(dh, axis=0, keepdims=True)
        red_ref[2:3, :] += jnp.sum(dh * xn, axis=0, keepdims=True)
        gd = dxn * g
        dx_ref[...] = dxo_ref[...] + r * (gd - xhat * jnp.mean(gd * xhat, axis=-1, keepdims=True))

    row = pl.BlockSpec((tm, d), lambda i: (i, 0))
    in_specs = [pl.BlockSpec((tm, n_cols), lambda i: (i, 0)), _resident(w_block, w_index), row, row,
                pl.BlockSpec((None, None, 8, d), lambda i: (*mv_idx, 0, 0))]
    ops = [dl, w, x, dxo, mv]
    if more is not None:
        in_specs += [pl.BlockSpec((tm, more[0].shape[1]), lambda i: (i, 0)), _resident(more[1], more[2])]
        ops += [more[0], w]
    return pl.pallas_call(
        body, name=name, grid=(s // tm,),
        in_specs=in_specs,
        out_specs=[row, pl.BlockSpec((8, d), lambda i: (0, 0))],
        out_shape=[jax.ShapeDtypeStruct((s, d), F32), jax.ShapeDtypeStruct((8, d), F32)],
        compiler_params=_params(1),
    )(*ops)


def weight_grad(a, a_spec, b, b_spec, grid_mn, s, bm, bn, dest, out_spec, name):
    tk = _wide_tile(s)
    k_tiles = s // tk

    def body(a_ref, b_ref, dest_ref, out_ref, acc):
        k = pl.program_id(2)

        @pl.when(k == 0)
        def _():
            acc[...] = jnp.zeros_like(acc)

        acc[...] += _tn(a_ref[...], b_ref[...])

        @pl.when(k == k_tiles - 1)
        def _():
            out_ref[...] = acc[...].astype(out_ref.dtype)

    return pl.pallas_call(
        body, name=name, grid=(*grid_mn, k_tiles),
        in_specs=[a_spec, b_spec, ANY], out_specs=out_spec,
        out_shape=jax.ShapeDtypeStruct(dest.shape, dest.dtype),
        input_output_aliases={2: 0},
        scratch_shapes=[pltpu.VMEM((bm, bn), F32)],
        compiler_params=_params(3),
    )(a, b, dest)


def _head_mean(v):
    lane = lax.broadcasted_iota(jnp.int32, v.shape, 1)
    lo = jnp.sum(jnp.where(lane < HEAD_DIM, v, 0.0), axis=-1, keepdims=True)
    hi = jnp.sum(v, axis=-1, keepdims=True) - lo
    return jnp.where(lane < HEAD_DIM, lo, hi) * (1.0 / HEAD_DIM)


def qknorm_fwd(proj, gains, width, name):
    s = proj.shape[0]
    nqk = gains.shape[1]
    tm = _row_tile(s)

    def body(p_ref, g_ref, o_ref):
        for cc in range(width // LANES):
            sl = slice(cc * LANES, (cc + 1) * LANES)
            xv = p_ref[:, sl].astype(F32)
            r = lax.rsqrt(_head_mean(xv * xv) + EPS)
            o_ref[:, sl] = (xv * r * g_ref[:, sl]).astype(BF16)

    blk = pl.BlockSpec((tm, width), lambda i, c: (i, c))
    return pl.pallas_call(
        body, name=name, grid=(s // tm, nqk // width),
        in_specs=[blk, pl.BlockSpec((1, width), lambda i, c: (0, c))],
        out_specs=blk, out_shape=jax.ShapeDtypeStruct((s, nqk), BF16),
        compiler_params=_params(2),
    )(proj, gains)


def qknorm_bwd(proj, gains, d, d_spec, n_cols, width, name):
    s = proj.shape[0]
    nqk = gains.shape[1] // width
    n_blocks = n_cols // width
    tm = _row_tile(s)

    def body(p_ref, g_ref, d_ref, o_ref, dg_ref):
        c, i = pl.program_id(0), pl.program_id(1)

        @pl.when(i == 0)
        def _():
            dg_ref[...] = jnp.zeros_like(dg_ref)

        @pl.when(c < nqk)
        def _():
            for cc in range(width // LANES):
                sl = slice(cc * LANES, (cc + 1) * LANES)
                xv = p_ref[:, sl].astype(F32)
                r = lax.rsqrt(_head_mean(xv * xv) + EPS)
                xhat = xv * r
                dv = d_ref[:, sl]
                gd = dv * g_ref[:, sl]
                o_ref[:, sl] = (r * (gd - xhat * _head_mean(gd * xhat))).astype(BF16)
                dg_ref[0:1, sl] += jnp.sum(dv * xhat, axis=0, keepdims=True)

        @pl.when(c >= nqk)
        def _():
            o_ref[...] = d_ref[...].astype(BF16)

    return pl.pallas_call(
        body, name=name, grid=(n_blocks, s // tm),
        in_specs=[pl.BlockSpec((tm, width), lambda c, i: (i, c)),
                  pl.BlockSpec((1, width), lambda c, i: (0, jnp.minimum(c, nqk - 1))), d_spec],
        out_specs=[pl.BlockSpec((tm, width), lambda c, i: (i, c)), pl.BlockSpec((8, width), lambda c, i: (0, c))],
        out_shape=[jax.ShapeDtypeStruct((s, n_cols), BF16), jax.ShapeDtypeStruct((8, n_cols), F32)],
        compiler_params=_params(2),
    )(proj, gains, d)


def _swa_mask(first, copies):
    qi = (lax.broadcasted_iota(jnp.int32, (copies * BLOCK, 2 * BLOCK), 0) & (BLOCK - 1)) + BLOCK
    kj = lax.broadcasted_iota(jnp.int32, (copies * BLOCK, 2 * BLOCK), 1)
    dist = qi - kj
    return (dist >= 0) & (dist < BLOCK) & ((kj >= BLOCK) | jnp.logical_not(first))


def swa_fwd(qkn, proj, bias, sink, d, name):
    s = qkn.shape[0]
    hq = d // HEAD_DIM
    hkv = hq // GROUP
    kw = hkv * HEAD_DIM
    nblk = s // BLOCK
    kcol = d // kw

    def body(q_ref, kc_ref, kp_ref, vc_ref, vp_ref, bias_ref, sink_ref, o_ref, lse_ref):
        mask = _swa_mask(pl.program_id(0) == 0, GROUP)
        lse_ref[...] = jnp.zeros_like(lse_ref)
        for kvh in range(hkv):
            cols = slice(kvh * HEAD_DIM, (kvh + 1) * HEAD_DIM)
            heads = range(kvh * GROUP, (kvh + 1) * GROUP)
            k2 = jnp.concatenate([kp_ref[:, cols], kc_ref[:, cols]], axis=0)
            v2 = jnp.concatenate([vp_ref[:, cols], vc_ref[:, cols]], axis=0)
            qs = jnp.concatenate([q_ref[:, h * HEAD_DIM:(h + 1) * HEAD_DIM] for h in heads], axis=0)
            bias_s = bias_ref[kvh * GROUP:(kvh + 1) * GROUP].reshape(GROUP * BLOCK, 2 * BLOCK)
            sk = jnp.concatenate([jnp.full((BLOCK, 1), sink_ref[0, h], F32) for h in heads], axis=0)
            sc = jnp.where(mask, _nt(qs, k2) + bias_s, NEG)
            m = jnp.maximum(jnp.max(sc, axis=-1, keepdims=True), sk)
            p = jnp.exp(sc - m)
            denom = jnp.sum(p, axis=-1, keepdims=True) + jnp.exp(sk - m)
            ov = (_nn(p.astype(BF16), v2) / denom).astype(BF16)
            lse_s = m + jnp.log(denom)
            for g, h in enumerate(heads):
                rows = slice(g * BLOCK, (g + 1) * BLOCK)
                o_ref[:, h * HEAD_DIM:(h + 1) * HEAD_DIM] = ov[rows]
                lse_ref[:, h:h + 1] = lse_s[rows]

    prev = lambda i: jnp.maximum(i - 1, 0)
    return pl.pallas_call(
        body, name=name, grid=(nblk,),
        in_specs=[pl.BlockSpec((BLOCK, d), lambda i: (i, 0)),
                  pl.BlockSpec((BLOCK, kw), lambda i: (i, kcol)),
                  pl.BlockSpec((BLOCK, kw), lambda i: (prev(i), kcol)),
                  pl.BlockSpec((BLOCK, kw), lambda i: (i, kcol + 1)),
                  pl.BlockSpec((BLOCK, kw), lambda i: (prev(i), kcol + 1)),
                  pl.BlockSpec((hq, BLOCK, 2 * BLOCK), lambda i: (0, 0, 0)),
                  pl.BlockSpec(memory_space=pltpu.SMEM)],
        out_specs=[pl.BlockSpec((BLOCK, d), lambda i: (i, 0)), pl.BlockSpec((BLOCK, LANES), lambda i: (i, 0))],
        out_shape=[jax.ShapeDtypeStruct((s, d), BF16), jax.ShapeDtypeStruct((s, LANES), F32)],
        compiler_params=_params(1),
    )(qkn, qkn, qkn, proj, proj, bias, sink)


def swa_bwd(qkn, proj, bias, sink, do, o, lse, d, name):
    s = qkn.shape[0]
    hq = d // HEAD_DIM
    hkv = hq // GROUP
    kw = hkv * HEAD_DIM
    nblk = s // BLOCK
    kcol = d // kw
    wide = d + 2 * kw

    def body(q_ref, kc_ref, kp_ref, vc_ref, vp_ref, bias_ref, sink_ref, do_ref, o_ref, lse_ref,
             out_ref, dbias_ref, dsink_ref, carry, fresh):
        i = pl.program_id(0)

        @pl.when(i == 0)
        def _():
            dbias_ref[...] = jnp.zeros_like(dbias_ref)
            dsink_ref[...] = jnp.zeros_like(dsink_ref)
            carry[...] = jnp.zeros_like(carry)

        @pl.when(i == nblk)
        def _():
            fresh[...] = jnp.zeros_like(fresh)

        @pl.when(i < nblk)
        def _():
            mask = _swa_mask(i == 0, GROUP)
            for kvh in range(hkv):
                cols = slice(kvh * HEAD_DIM, (kvh + 1) * HEAD_DIM)
                heads = range(kvh * GROUP, (kvh + 1) * GROUP)
                stack = lambda ref: jnp.concatenate([ref[:, h * HEAD_DIM:(h + 1) * HEAD_DIM] for h in heads], axis=0)
                k2 = jnp.concatenate([kp_ref[:, cols], kc_ref[:, cols]], axis=0)
                v2 = jnp.concatenate([vp_ref[:, cols], vc_ref[:, cols]], axis=0)
                qs, dos, os_ = stack(q_ref), stack(do_ref), stack(o_ref)
                bias_s = bias_ref[kvh * GROUP:(kvh + 1) * GROUP].reshape(GROUP * BLOCK, 2 * BLOCK)
                sk = jnp.concatenate([jnp.full((BLOCK, 1), sink_ref[0, h], F32) for h in heads], axis=0)
                lse_s = jnp.concatenate([lse_ref[:, h:h + 1] for h in heads], axis=0)
                sc = jnp.where(mask, _nt(qs, k2) + bias_s, NEG)
                p = jnp.exp(sc - lse_s)
                delta = jnp.sum(dos.astype(F32) * os_.astype(F32), axis=-1, keepdims=True)
                ds = p * (_nt(dos, v2) - delta)
                dbias_ref[kvh * GROUP:(kvh + 1) * GROUP] += ds.reshape(GROUP, BLOCK, 2 * BLOCK)
                to_sink = -jnp.exp(sk - lse_s) * delta
                dsb = ds.astype(BF16)
                dqs = _nn(dsb, k2)
                for g, h in enumerate(heads):
                    rows = slice(g * BLOCK, (g + 1) * BLOCK)
                    fresh[0, :, h * HEAD_DIM:(h + 1) * HEAD_DIM] = dqs[rows]
                    dsink_ref[0:1, h:h + 1] += jnp.sum(to_sink[rows], axis=0, keepdims=True)
                dk2 = _tn(dsb, qs)
                dv2 = _tn(p.astype(BF16), dos)
                kc_cols = slice(d + kvh * HEAD_DIM, d + (kvh + 1) * HEAD_DIM)
                vc_cols = slice(d + kw + kvh * HEAD_DIM, d + kw + (kvh + 1) * HEAD_DIM)
                fresh[0, :, kc_cols] = dk2[BLOCK:]
                fresh[0, :, vc_cols] = dv2[BLOCK:]
                fresh[1, :, kc_cols] = dk2[:BLOCK]
                fresh[1, :, vc_cols] = dv2[:BLOCK]

        lane = lax.broadcasted_iota(jnp.int32, (BLOCK, wide), 1)
        out_ref[...] = carry[...] + jnp.where(lane >= d, fresh[1], 0.0)

        @pl.when(i < nblk)
        def _():
            carry[...] = fresh[0]

    cur = lambda i: jnp.minimum(i, nblk - 1)
    prev = lambda i: jnp.maximum(jnp.minimum(i, nblk - 1) - 1, 0)
    return pl.pallas_call(
        body, name=name, grid=(nblk + 1,),
        in_specs=[pl.BlockSpec((BLOCK, d), lambda i: (cur(i), 0)),
                  pl.BlockSpec((BLOCK, kw), lambda i: (cur(i), kcol)),
                  pl.BlockSpec((BLOCK, kw), lambda i: (prev(i), kcol)),
                  pl.BlockSpec((BLOCK, kw), lambda i: (cur(i), kcol + 1)),
                  pl.BlockSpec((BLOCK, kw), lambda i: (prev(i), kcol + 1)),
                  pl.BlockSpec((hq, BLOCK, 2 * BLOCK), lambda i: (0, 0, 0)),
                  pl.BlockSpec(memory_space=pltpu.SMEM),
                  pl.BlockSpec((BLOCK, d), lambda i: (cur(i), 0)),
                  pl.BlockSpec((BLOCK, d), lambda i: (cur(i), 0)),
                  pl.BlockSpec((BLOCK, LANES), lambda i: (cur(i), 0))],
        out_specs=[pl.BlockSpec((BLOCK, wide), lambda i: (jnp.maximum(i - 1, 0), 0)),
                   pl.BlockSpec((hq, BLOCK, 2 * BLOCK), lambda i: (0, 0, 0)),
                   pl.BlockSpec((8, LANES), lambda i: (0, 0))],
        out_shape=[jax.ShapeDtypeStruct((s, wide), F32), jax.ShapeDtypeStruct((hq, BLOCK, 2 * BLOCK), F32),
                   jax.ShapeDtypeStruct((8, LANES), F32)],
        scratch_shapes=[pltpu.VMEM((BLOCK, wide), F32), pltpu.VMEM((2, BLOCK, wide), F32)],
        compiler_params=_params(1),
    )(qkn, qkn, qkn, proj, proj, bias, sink, do, o, lse)


def _rel_bucket_table():
    qi = np.arange(BLOCK)[:, None] + BLOCK
    kj = np.arange(2 * BLOCK)[None, :]
    n = np.maximum(qi - kj, 0)
    max_exact = REL_BUCKETS // 2
    nf = np.maximum(n, 1).astype(np.float32)
    large = max_exact + (np.log(nf / max_exact) / math.log(REL_MAX_DIST / max_exact)
                         * (REL_BUCKETS - max_exact)).astype(np.int32)
    large = np.minimum(large, REL_BUCKETS - 1)
    return np.where(n < max_exact, n, large).astype(np.int32)


def rel_bias_table(rel_bias, bucket):
    hq = rel_bias.shape[1]

    def body(rb_ref, bucket_ref, out_ref):
        tbl = bucket_ref[...]

        def per_head(h, carry):
            def per_bucket(b, acc):
                return jnp.where(tbl == b, rb_ref[b, h], acc)

            out_ref[h] = lax.fori_loop(0, REL_BUCKETS, per_bucket, jnp.zeros(tbl.shape, F32))
            return carry

        lax.fori_loop(0, hq, per_head, 0)

    return pl.pallas_call(
        body, name="rel_bias_table",
        in_specs=[pl.BlockSpec(memory_space=pltpu.SMEM), pl.BlockSpec(memory_space=pltpu.VMEM)],
        out_specs=pl.BlockSpec(memory_space=pltpu.VMEM),
        out_shape=jax.ShapeDtypeStruct((hq,) + tuple(bucket.shape), F32),
    )(rel_bias, bucket)


def rel_bias_grad(dbias, bucket):
    n_layers, hq = dbias.shape[:2]

    def body(db_ref, bucket_ref, out_ref):
        tbl = bucket_ref[...]

        def per_head(h, carry):
            dsum = db_ref[0, h]
            for a in range(1, n_layers):
                dsum = dsum + db_ref[a, h]

            def per_bucket(b, carry2):
                out_ref[b, h] = jnp.sum(jnp.where(tbl == b, dsum, 0.0))
                return carry2

            return lax.fori_loop(0, REL_BUCKETS, per_bucket, carry)

        lax.fori_loop(0, hq, per_head, 0)

    return pl.pallas_call(
        body, name="rel_bias_grad",
        in_specs=[pl.BlockSpec(memory_space=pltpu.VMEM), pl.BlockSpec(memory_space=pltpu.VMEM)],
        out_specs=pl.BlockSpec(memory_space=pltpu.SMEM),
        out_shape=jax.ShapeDtypeStruct((REL_BUCKETS, hq), F32),
    )(dbias, bucket)


def _split3(v):
    hi = v.astype(BF16)
    r1 = v - hi.astype(F32)
    mid = r1.astype(BF16)
    lo = (r1 - mid.astype(F32)).astype(BF16)
    return hi, mid, lo


def _tri_sum(tri, v):
    hi, mid, lo = _split3(v)
    return _nn(tri, hi) + _nn(tri, mid) + _nn(tri, lo)


def fox_gates(fl, b_f, name):
    s = fl.shape[0]
    t = _row_tile(s)

    def body(fl_ref, b_ref, f_ref, carry):
        @pl.when(pl.program_id(0) == 0)
        def _():
            carry[...] = jnp.zeros_like(carry)

        z = fl_ref[...] + b_ref[...]
        logf = jnp.minimum(z, 0.0) - jnp.log(1.0 + jnp.exp(-jnp.abs(z)))
        r = lax.broadcasted_iota(jnp.int32, (t, t), 0)
        cidx = lax.broadcasted_iota(jnp.int32, (t, t), 1)
        tri = jnp.where(cidx <= r, 1.0, 0.0).astype(BF16)
        f = _tri_sum(tri, logf) + carry[0:1, :]
        f_ref[...] = f
        carry[0:1, :] = f_ref[t - 1:t, :]

    blk = pl.BlockSpec((t, LANES), lambda i: (i, 0))
    return pl.pallas_call(
        body, name=name, grid=(s // t,),
        in_specs=[blk, pl.BlockSpec((1, LANES), lambda i: (0, 0))],
        out_specs=blk, out_shape=jax.ShapeDtypeStruct((s, LANES), F32),
        scratch_shapes=[pltpu.VMEM((8, LANES), F32)],
        compiler_params=_params(1),
    )(fl, b_f)


def fox_gates_bwd(fl, b_f, df_query, df_key, name):
    s = fl.shape[0]
    t = _row_tile(s)
    nb = s // t

    def body(fl_ref, b_ref, dfq_ref, dfk_ref, dfl_ref, db_ref, carry):
        @pl.when(pl.program_id(0) == 0)
        def _():
            carry[...] = jnp.zeros_like(carry)
            db_ref[...] = jnp.zeros_like(db_ref)

        dfv = dfq_ref[...] + dfk_ref[...]
        r = lax.broadcasted_iota(jnp.int32, (t, t), 0)
        cidx = lax.broadcasted_iota(jnp.int32, (t, t), 1)
        tri = jnp.where(cidx >= r, 1.0, 0.0).astype(BF16)
        dlog = _tri_sum(tri, dfv) + carry[0:1, :]
        carry[0:1, :] += jnp.sum(dfv, axis=0, keepdims=True)
        z = fl_ref[...] + b_ref[...]
        dz = dlog * (1.0 - _sigmoid(z))
        dfl_ref[...] = dz.astype(BF16)
        db_ref[0:1, :] += jnp.sum(dz, axis=0, keepdims=True)

    rev = pl.BlockSpec((t, LANES), lambda i: (nb - 1 - i, 0))
    return pl.pallas_call(
        body, name=name, grid=(nb,),
        in_specs=[rev, pl.BlockSpec((1, LANES), lambda i: (0, 0)), rev, rev],
        out_specs=[rev, pl.BlockSpec((8, LANES), lambda i: (0, 0))],
        out_shape=[jax.ShapeDtypeStruct((s, LANES), BF16), jax.ShapeDtypeStruct((8, LANES), F32)],
        scratch_shapes=[pltpu.VMEM((8, LANES), F32)],
        compiler_params=_params(1),
    )(fl, b_f, df_query, df_key)


def fox_fwd(qkn, proj, f_col, f_row, d, name):
    s = qkn.shape[0]
    t = _attn_tile(s)
    hs, wide = FOX_HEADS, FOX_HEADS * HEAD_DIM
    n_pairs = d // wide
    nt = s // t

    def body(q_ref, k_ref, v_ref, fq_ref, fk_ref, o_ref, o32_ref, lse_ref, m_scr, l_scr, acc):
        i, j = pl.program_id(1), pl.program_id(2)

        @pl.when(j == 0)
        def _():
            m_scr[...] = jnp.full_like(m_scr, NEG)
            l_scr[...] = jnp.zeros_like(l_scr)
            acc[...] = jnp.zeros_like(acc)

        @pl.when(j <= i)
        def _():
            krow = lax.broadcasted_iota(jnp.int32, (t, t), 0)
            qcol = lax.broadcasted_iota(jnp.int32, (t, t), 1)
            visible = (krow <= qcol) | (j < i)
            for hh in range(hs):
                hc = slice(hh * HEAD_DIM, (hh + 1) * HEAD_DIM)
                st = _nt(k_ref[:, hc], q_ref[:, hc]) + fq_ref[hh] - fk_ref[hh]
                st = jnp.where(visible, st, NEG)
                m_prev = m_scr[hh]
                m_new = jnp.maximum(m_prev, jnp.max(st, axis=0, keepdims=True))
                alpha = jnp.exp(m_prev - m_new)
                pt = jnp.exp(st - m_new)
                l_scr[hh] = alpha * l_scr[hh] + jnp.sum(pt, axis=0, keepdims=True)
                acc[hc, :] = alpha * acc[hc, :] + _tn(v_ref[:, hc], pt.astype(BF16))
                m_scr[hh] = m_new

        @pl.when(j == i)
        def _():
            l_full = jnp.concatenate([jnp.broadcast_to(l_scr[hh], (HEAD_DIM, t)) for hh in range(hs)], axis=0)
            ov = (acc[...] / l_full).T
            o_ref[...] = ov.astype(BF16)
            o32_ref[...] = ov
            lse_ref[...] = m_scr[...] + jnp.log(l_scr[...])

    kv = lambda j, i: jnp.minimum(j, i)
    return pl.pallas_call(
        body, name=name, grid=(n_pairs, nt, nt),
        in_specs=[pl.BlockSpec((t, wide), lambda p, i, j: (i, p)),
                  pl.BlockSpec((t, wide), lambda p, i, j: (kv(j, i), n_pairs + p)),
                  pl.BlockSpec((t, wide), lambda p, i, j: (kv(j, i), 2 * n_pairs + p)),
                  pl.BlockSpec((hs, 1, t), lambda p, i, j: (p, 0, i)),
                  pl.BlockSpec((hs, t, 1), lambda p, i, j: (p, kv(j, i), 0))],
        out_specs=[pl.BlockSpec((t, wide), lambda p, i, j: (i, p)),
                   pl.BlockSpec((t, wide), lambda p, i, j: (i, p)),
                   pl.BlockSpec((hs, 1, t), lambda p, i, j: (p, 0, i))],
        out_shape=[jax.ShapeDtypeStruct((s, d), BF16), jax.ShapeDtypeStruct((s, d), F32),
                   jax.ShapeDtypeStruct((hs * n_pairs, 1, s), F32)],
        scratch_shapes=[pltpu.VMEM((hs, 1, t), F32), pltpu.VMEM((hs, 1, t), F32), pltpu.VMEM((wide, t), F32)],
        compiler_params=_params(3),
    )(qkn, qkn, proj, f_row, f_col)


def fox_bwd(qkn, proj, f_col, f_row, lse_row, do, o, d, name):
    s = qkn.shape[0]
    t = _attn_tile(s)
    hs, wide = FOX_HEADS, FOX_HEADS * HEAD_DIM
    n_pairs = d // wide
    nt = s // t

    def body(q_ref, k_ref, v_ref, fk_ref, fq_ref, lse_ref, do_ref, o_ref, out_ref, df_ref, dfq_ref,
             dq_acc, dkv_acc, df_acc, dfq_acc):
        j, i = pl.program_id(1), pl.program_id(2)

        @pl.when((j == 0) & (i == 0))
        def _():
            dq_acc[...] = jnp.zeros_like(dq_acc)
            dfq_acc[...] = jnp.zeros_like(dfq_acc)

        @pl.when(i == 0)
        def _():
            dkv_acc[...] = jnp.zeros_like(dkv_acc)
            df_acc[...] = jnp.zeros_like(df_acc)

        @pl.when(i >= j)
        def _():
            krow = lax.broadcasted_iota(jnp.int32, (t, t), 0)
            qcol = lax.broadcasted_iota(jnp.int32, (t, t), 1)
            visible = (krow <= qcol) | (i > j)
            ones = jnp.ones((8, HEAD_DIM), BF16)
            for hh in range(hs):
                hc = slice(hh * HEAD_DIM, (hh + 1) * HEAD_DIM)
                q, k, v, dov = q_ref[:, hc], k_ref[:, hc], v_ref[:, hc], do_ref[:, hc]
                st = _nt(k, q) + fq_ref[hh] - fk_ref[hh]
                pt = jnp.exp(jnp.where(visible, st, NEG) - lse_ref[hh])
                hi, mid, lo = _split3(dov.astype(F32) * o_ref[:, hc])
                delta = jnp.max(_nt(ones, hi) + _nt(ones, mid) + _nt(ones, lo), axis=0, keepdims=True)
                dst = pt * (_nt(v, dov) - delta)
                dsb = dst.astype(BF16)
                dkv_acc[1, :, hc] += _nn(pt.astype(BF16), dov)
                dkv_acc[0, :, hc] += _nn(dsb, q)
                dq_acc[pl.ds(pl.multiple_of(i * t, t), t), hc] += _tn(dsb, k)
                df_acc[hh] -= jnp.sum(dst, axis=-1, keepdims=True)
                dfq_acc[i, hh] += jnp.sum(dst, axis=0, keepdims=True)

        @pl.when(i == nt - 1)
        def _():
            out_ref[0] = dq_acc[pl.ds(pl.multiple_of(j * t, t), t), :]
            out_ref[1] = dkv_acc[0]
            out_ref[2] = dkv_acc[1]
            df_ref[...] = df_acc[...]
            dfq_ref[...] = dfq_acc[j]

    qi = lambda j, i: jnp.maximum(i, j)
    return pl.pallas_call(
        body, name=name, grid=(n_pairs, nt, nt),
        in_specs=[pl.BlockSpec((t, wide), lambda p, j, i: (qi(j, i), p)),
                  pl.BlockSpec((t, wide), lambda p, j, i: (j, n_pairs + p)),
                  pl.BlockSpec((t, wide), lambda p, j, i: (j, 2 * n_pairs + p)),
                  pl.BlockSpec((hs, t, 1), lambda p, j, i: (p, j, 0)),
                  pl.BlockSpec((hs, 1, t), lambda p, j, i: (p, 0, qi(j, i))),
                  pl.BlockSpec((hs, 1, t), lambda p, j, i: (p, 0, qi(j, i))),
                  pl.BlockSpec((t, wide), lambda p, j, i: (qi(j, i), p)),
                  pl.BlockSpec((t, wide), lambda p, j, i: (qi(j, i), p))],
        out_specs=[pl.BlockSpec((3, t, wide), lambda p, j, i: (0, j, p)),
                   pl.BlockSpec((hs, t, 1), lambda p, j, i: (p, j, 0)),
                   pl.BlockSpec((hs, 1, t), lambda p, j, i: (p, 0, j))],
        out_shape=[jax.ShapeDtypeStruct((3, s, d), F32), jax.ShapeDtypeStruct((hs * n_pairs, s, 1), F32),
                   jax.ShapeDtypeStruct((hs * n_pairs, 1, s), F32)],
        scratch_shapes=[pltpu.VMEM((s, wide), F32), pltpu.VMEM((2, t, wide), F32), pltpu.VMEM((hs, t, 1), F32),
                        pltpu.VMEM((nt, hs, 1, t), F32)],
        compiler_params=_params(3),
    )(qkn, qkn, proj, f_col, f_row, lse_row, do, o)


def loss_head(y, target):
    s, d = y.shape
    tm = _row_tile(s)

    def body(y_ref, t_ref, dy_ref, loss_ref):
        @pl.when(pl.program_id(0) == 0)
        def _():
            loss_ref[...] = jnp.zeros_like(loss_ref)

        diff = y_ref[...] - t_ref[...]
        dy_ref[...] = diff * (1.0 / d)
        loss_ref[...] += 0.5 * jnp.sum(jnp.mean(diff * diff, axis=-1, keepdims=True), axis=0, keepdims=True)

    row = pl.BlockSpec((tm, d), lambda i: (i, 0))
    return pl.pallas_call(
        body, name="loss_head", grid=(s // tm,),
        in_specs=[row, row],
        out_specs=[row, pl.BlockSpec((8, LANES), lambda i: (0, 0))],
        out_shape=[jax.ShapeDtypeStruct((s, d), F32), jax.ShapeDtypeStruct((8, LANES), F32)],
        compiler_params=_params(1),
    )(y, target)


def ada_mod(c_all, w, b):
    n_layers, d, cols = w.shape

    def body(c_ref, w_ref, b_ref, o_ref):
        cv = c_ref[...]
        o_ref[...] = _nn(cv * _sigmoid(cv), w_ref[...]) + b_ref[...]

    return pl.pallas_call(
        body, name="ada_mod", grid=(n_layers,),
        in_specs=[pl.BlockSpec((N_DEV, d), lambda l: (0, 0)), pl.BlockSpec((None, d, cols), lambda l: (l, 0, 0)),
                  pl.BlockSpec((None, 1, cols), lambda l: (l, 0, 0))],
        out_specs=pl.BlockSpec((None, N_DEV, cols), lambda l: (l, 0, 0)),
        out_shape=jax.ShapeDtypeStruct((n_layers, N_DEV, cols), F32),
        compiler_params=_params(1),
    )(c_all, w, b)


def ada_grad(c_t, dmod):
    d = c_t.shape[0]
    n_layers, _, cols = dmod.shape
    tn = cols // 2

    def body(c_ref, dm_ref, o_ref):
        cv = c_ref[...]
        o_ref[...] = _nn(cv * _sigmoid(cv), dm_ref[...])

    return pl.pallas_call(
        body, name="ada_grad", grid=(n_layers, 2),
        in_specs=[pl.BlockSpec((d, N_DEV), lambda l, n: (0, 0)), pl.BlockSpec((None, N_DEV, tn), lambda l, n: (l, 0, n))],
        out_specs=pl.BlockSpec((None, d, tn), lambda l, n: (l, 0, n)),
        out_shape=jax.ShapeDtypeStruct((n_layers, d, cols), F32),
        compiler_params=_params(2),
    )(c_t, dmod)


def sum_devices(v):
    def body(v_ref, o_ref):
        acc = v_ref[0]
        for k in range(1, N_DEV):
            acc = acc + v_ref[k]
        o_ref[...] = acc

    return pl.pallas_call(body, name="sum_devices", out_shape=jax.ShapeDtypeStruct(v.shape[1:], F32))(v)


def sum_slots(r, own):
    _, rows, cols = r.shape
    tm = 256 if rows % 256 == 0 else rows

    def body(r_ref, own_ref, o_ref):
        o_ref[...] = ((own_ref[...].astype(F32) + r_ref[0].astype(F32)) + r_ref[1].astype(F32)) + r_ref[2].astype(F32)

    return pl.pallas_call(
        body, name="sum_slots", grid=(rows // tm,),
        in_specs=[pl.BlockSpec((N_CHIP - 1, tm, cols), lambda i: (0, i, 0)), pl.BlockSpec((tm, cols), lambda i: (i, 0))],
        out_specs=pl.BlockSpec((tm, cols), lambda i: (i, 0)),
        out_shape=jax.ShapeDtypeStruct((rows, cols), F32),
        compiler_params=_params(1),
    )(r, own)


def adamw(w, m, v, g, g2=None):
    rows, cols = w.shape
    tm = 256 if rows % 256 == 0 else rows
    two = g2 is not None
    c1 = 1.0 - ADAM_B1 ** ADAM_STEP
    c2 = 1.0 - ADAM_B2 ** ADAM_STEP

    def body(w_ref, m_ref, v_ref, g_ref, *rest):
        if two:
            g2_ref, go_ref, d_ref, mo_ref, vo_ref = rest
            gv = g_ref[...] + g2_ref[...]
        else:
            go_ref, d_ref, mo_ref, vo_ref = rest
            gv = g_ref[...]
        mn = ADAM_B1 * m_ref[...] + (1.0 - ADAM_B1) * gv
        vn = ADAM_B2 * v_ref[...] + (1.0 - ADAM_B2) * (gv * gv)
        go_ref[...] = gv
        mo_ref[...] = mn
        vo_ref[...] = vn
        d_ref[...] = -ADAM_LR * ((mn / c1) / (jnp.sqrt(vn / c2) + ADAM_EPS) + ADAM_WD * w_ref[...])

    blk = pl.BlockSpec((tm, cols), lambda i: (i, 0))
    ops = [w, m, v, g] + ([g2] if two else [])
    return pl.pallas_call(
        body, name="adamw", grid=(rows // tm,),
        in_specs=[blk] * len(ops), out_specs=[blk] * 4,
        out_shape=[jax.ShapeDtypeStruct((rows, cols), F32)] * 4,
        compiler_params=_params(1),
    )(*ops)


def _pad_rows(flat):
    n = flat.shape[0]
    rows = -(-n // LANES)
    return jnp.pad(flat, (0, rows * LANES - n)).reshape(rows, LANES)


def _pad_rows8(flat):
    rows = _pad_rows(flat)
    return jnp.pad(rows, ((0, -rows.shape[0] % 8), (0, 0)))


def _col_tiles(n):
    return next(k for k in range(1, n // LANES + 1) if n % (k * LANES) == 0 and n // k <= 1536)


def kernel(x, c, ada_w, ada_b, norm_g, ffn_w13, ffn_w2, rel_bias, swa_w_in, swa_w_out, swa_q_g, swa_k_g, swa_sink, fox_w_in, fox_w_out, fox_b_f, fox_q_g, fox_k_g, loss_target, m_ada_w, m_ada_b, m_norm_g, m_ffn_w13, m_ffn_w2, m_rel_bias, m_swa_w_in, m_swa_w_out, m_swa_q_g, m_swa_k_g, m_swa_sink, m_fox_w_in, m_fox_w_out, m_fox_b_f, m_fox_q_g, m_fox_k_g, v_ada_w, v_ada_b, v_norm_g, v_ffn_w13, v_ffn_w2, v_rel_bias, v_swa_w_in, v_swa_w_out, v_swa_q_g, v_swa_k_g, v_swa_sink, v_fox_w_in, v_fox_w_out, v_fox_b_f, v_fox_q_g, v_fox_k_g):
    ix, iy, ic = lax.axis_index("x"), lax.axis_index("y"), lax.axis_index("c")
    chip = 2 * ix + iy
    dev = 2 * chip + ic
    s, d = x.shape[1:]
    n_layers = ada_w.shape[0]
    n_a, n_b = swa_w_in.shape[0], fox_w_in.shape[0]
    hq = d // HEAD_DIM
    hkv = hq // GROUP
    kw = hkv * HEAD_DIM
    c13 = ffn_w13.shape[-1]
    f = 2 * c13
    r2 = ffn_w2.shape[2]
    cq = norm_g.shape[-1]
    a_in = d + 2 * kw
    fx = fox_w_in.shape[-1]
    b_in = N_CHIP * fx
    b_pad = 3 * d + LANES
    x0 = x[0]

    hello = _pad_rows8(jnp.concatenate([c.reshape(-1), norm_g.reshape(-1)]))
    hello_all = all_gather_rows(hello, "gather_c_norm").reshape(N_DEV, -1)
    c_all = hello_all[:, :d]
    ng = hello_all[::2, d:d + n_layers * 3 * cq].reshape(N_CHIP, n_layers, 3, cq)
    norm_full = jnp.moveaxis(ng, 0, 2).reshape(n_layers, 3, d)

    half_cols = ada_w.shape[-1] // 2
    w_half = lax.dynamic_slice_in_dim(ada_w, ic * half_cols, half_cols, axis=2)
    b_half = lax.dynamic_slice_in_dim(ada_b, dev * half_cols, half_cols, axis=1)[:, None, :]
    mod_part = ada_mod(c_all, w_half, b_half)
    mod_all = all_gather_rows(mod_part.reshape(n_layers * N_DEV, half_cols), "gather_mod")
    mod_all = mod_all.reshape(N_DEV, n_layers, N_DEV, half_cols)
    mod_mine = lax.dynamic_index_in_dim(mod_all, dev, axis=2, keepdims=False)
    mod_mine = jnp.moveaxis(mod_mine, 0, 1).reshape(n_layers, 3, 3, d)
    mv = jnp.concatenate([norm_full[:, :, None, :], mod_mine, jnp.zeros((n_layers, 3, 4, d), F32)], axis=2)

    def kinds_of(l):
        mixer_in = ("col", swa_w_in.shape[-1]) if l % 2 == 0 else ("slot",)
        return [("col", c13), ("row", r2), mixer_in, ("slot",)]

    def layer_buffers(l, after):
        raw = [ffn_w13[l], ffn_w2[l]] + ([swa_w_in[l // 2], swa_w_out[l // 2]] if l % 2 == 0 else
                                         [fox_w_in[l // 2], fox_w_out[l // 2]])
        return [cast_and_place(r, k, chip, after, f"place_weights_{l}_{t}") for t, (r, k) in enumerate(zip(raw, kinds_of(l)))]

    def layer_weights(l, fulls):
        w13_l, w2_l, w_in, w_out = fulls
        if l % 2 == 1:
            w_in = jnp.pad(jnp.concatenate([w_in[b] for b in range(N_CHIP)], axis=-1), ((0, 0), (0, b_pad - b_in)))
        return dict(w13=w13_l, w2=w2_l, w_in=w_in, w_out=w_out.reshape(d, d))

    weights = [layer_weights(0, gather_weights(layer_buffers(0, mv), kinds_of(0)))]
    in_flight = []
    for l in range(1, n_layers):
        in_flight.append(exchange_start([], layer_buffers(l, weights[0]["w13"]), kinds_of(l), True, f"gather_start_{l}"))
    for *_, token in in_flight:
        mv = mv + token[0, 0]

    bucket = jnp.asarray(_rel_bucket_table())
    bias = rel_bias_table(rel_bias, bucket)
    tm, tw = _row_tile(s), _wide_tile(s)
    n13 = 2 * f // c13
    na_t, nb_t = _col_tiles(a_in), _col_tiles(3 * d)
    wa_t, wb_t = a_in // na_t, 3 * d // nb_t
    gate_blk = 3 * d // LANES

    def ffn_forward(xv, l, half, sub):
        wg = weights[l]
        a, h = modmm(xv, mv, (l, sub), wg["w13"], (None, d, 2 * f), (half, 0, 0), 2 * f, c13, BF16,
                     f"ffn_up_{l}_{half}", True)
        xo, y, u = resmm(xv, mv, (l, sub), 0.5, a, wg["w2"], (None, f, d), (half, 0, 0), f, c13,
                         f"ffn_down_{l}_{half}", True)
        return xo, dict(x=xv, h=h, a=a, u=u, y=y)

    saved = []
    xv = x0
    for l in range(n_layers):
        j = l // 2
        if l >= 1:
            send_sems, recv_sems, shards, fulls, _ = in_flight[l - 1]
            weights.append(layer_weights(l, exchange_wait(send_sems, recv_sems, shards, fulls, xv, kinds_of(l), True,
                                                          f"gather_wait_{l}")[1]))
        wg = weights[l]
        xv, s0 = ffn_forward(xv, l, 0, 0)
        if l % 2 == 0:
            proj, h = modmm(xv, mv, (l, 1), wg["w_in"], (d, a_in), (0, 0), a_in, wa_t, BF16, f"swa_in_{j}", True)
            gains = jnp.concatenate([jnp.tile(swa_q_g[j] * HEAD_DIM ** -0.5, hq), jnp.tile(swa_k_g[j], hkv)])[None, :]
            qkn = qknorm_fwd(proj, gains, kw, f"swa_qknorm_{j}")
            sink = swa_sink[j][None, :]
            o, lse = swa_fwd(qkn, proj, bias, sink, d, f"swa_attn_{j}")
            s1 = dict(x=xv, h=h, proj=proj, gains=gains, qkn=qkn, sink=sink, o=o, lse=lse)
        else:
            proj, h = modmm(xv, mv, (l, 1), wg["w_in"], (d, 3 * d), (0, 0), 3 * d, wb_t, BF16, f"fox_in_{j}", True)
            fl, _ = modmm(xv, mv, (l, 1), wg["w_in"], (d, LANES), (0, gate_blk), LANES, LANES, F32,
                          f"fox_gate_in_{j}", False)
            b_f = jnp.pad(fox_b_f[j], (0, LANES - hq))[None, :]
            fcum = fox_gates(fl, b_f, f"fox_gates_{j}")
            f_t = fcum[:, :hq].T
            f_col, f_row = f_t[:, :, None], f_t[:, None, :]
            gains = jnp.concatenate([jnp.tile(fox_q_g[j] * HEAD_DIM ** -0.5, hq), jnp.tile(fox_k_g[j], hq)])[None, :]
            qkn = qknorm_fwd(proj, gains, d, f"fox_qknorm_{j}")
            o, o32, lse = fox_fwd(qkn, proj, f_col, f_row, d, f"fox_attn_{j}")
            s1 = dict(x=xv, h=h, proj=proj, gains=gains, qkn=qkn, fl=fl, b_f=b_f, f_col=f_col, f_row=f_row, o=o, o32=o32,
                      lse=lse)
        xv, y = resmm(xv, mv, (l, 1), 1.0, o, wg["w_out"], (d, d), (0, 0), d, d, f"mixer_out_{l}", False)
        s1["y"] = y
        xv, s2 = ffn_forward(xv, l, 1, 2)
        saved.append((s0, s1, s2))

    dxv, loss_part = loss_head(xv, loss_target[0])
    loss = lax.psum(loss_part[0, 0], ("x", "y", "c"))

    grads = [dict(w13=lax.empty((2, d, 2 * f), BF16), w2=lax.empty((2, f, d), BF16),
                  w_in=lax.empty((d, a_in if l % 2 == 0 else b_pad), BF16), w_out=lax.empty((d, d), BF16))
             for l in range(n_layers)]
    dmod = [[None] * 3 for _ in range(n_layers)]
    dnorm = [[None] * 3 for _ in range(n_layers)]
    dqk_gain = {}
    dsink, db_f, dbias_tabs = {}, {}, []

    def ffn_backward(dxo, sv, l, half, sub):
        wg, gg = weights[l], grads[l]
        dy, da, dgate = resmm_bwd(dxo, sv["y"], mv, (l, sub), 0.5, wg["w2"], (None, f, d), (half, 0, 0), f, c13,
                                  f"ffn_down_bwd_{l}_{half}", a=sv["a"])
        gg["w2"] = weight_grad(sv["u"], pl.BlockSpec((tw, c13), lambda m, n, k: (k, m)), dy,
                               pl.BlockSpec((tw, d), lambda m, n, k: (k, 0)), (f // c13, 1), s, c13, d, gg["w2"],
                               pl.BlockSpec((None, c13, d), lambda m, n, k: (half, m, 0)), f"ffn_w2_grad_{l}_{half}")
        gg["w13"] = weight_grad(sv["h"], pl.BlockSpec((tw, d), lambda m, n, k: (k, 0)), da,
                                pl.BlockSpec((tw, c13), lambda m, n, k: (k, n)), (1, n13), s, d, c13,
                                gg["w13"], pl.BlockSpec((None, d, c13), lambda m, n, k: (half, 0, n)),
                                f"ffn_w13_grad_{l}_{half}")
        dx, red = modmm_bwd(da, wg["w13"], (None, d, 2 * f), (half, 0, 0), 2 * f, c13, sv["x"], dxo, mv,
                            (l, sub), f"ffn_up_bwd_{l}_{half}")
        dmod[l][sub] = (red[1], red[2], dgate[0])
        dnorm[l][sub] = red[0]
        return dx

    def layer_slabs(l):
        gg = grads[l]
        g_in = gg["w_in"] if l % 2 == 0 else jnp.stack([gg["w_in"][:, b * fx:(b + 1) * fx] for b in range(N_CHIP)])
        return [gg["w13"], gg["w2"], g_in, gg["w_out"].reshape(N_CHIP, d // N_CHIP, d)]

    def shard_shapes_of(l):
        mixer = (swa_w_in, swa_w_out) if l % 2 == 0 else (fox_w_in, fox_w_out)
        return [ffn_w13.shape[1:], ffn_w2.shape[1:], mixer[0].shape[1:], mixer[1].shape[1:]]

    scatter_in_flight = {}

    for l in reversed(range(n_layers)):
        j = l // 2
        wg, gg = weights[l], grads[l]
        s0, s1, s2 = saved[l]
        dxv = ffn_backward(dxv, s2, l, 1, 2)
        is_a = l % 2 == 0
        w_out = wg["w_out"]
        dy, do, dgate = resmm_bwd(dxv, s1["y"], mv, (l, 1), 1.0, w_out, (d, d), (0, 0), d, d, f"mixer_out_bwd_{l}")
        gg["w_out"] = weight_grad(s1["o"], pl.BlockSpec((tw, d), lambda m, n, k: (k, 0)), dy,
                                  pl.BlockSpec((tw, d), lambda m, n, k: (k, 0)), (1, 1), s, d, d, gg["w_out"],
                                  pl.BlockSpec((d, d), lambda m, n, k: (0, 0)), f"mixer_out_grad_{l}")
        if is_a:
            d_qkv, dbias_tab, dsk = swa_bwd(s1["qkn"], s1["proj"], bias, s1["sink"], do, s1["o"], s1["lse"], d,
                                            f"swa_attn_bwd_{j}")
            dbias_tabs.append(dbias_tab)
            dsink[j] = dsk[0, :hq]
            dproj, dgain = qknorm_bwd(s1["proj"], s1["gains"], d_qkv, pl.BlockSpec((tm, kw), lambda c_, i: (i, c_)),
                                      a_in, kw, f"swa_qknorm_bwd_{j}")
            gg["w_in"] = weight_grad(s1["h"], pl.BlockSpec((tw, d), lambda m, n, k: (k, 0)), dproj,
                                     pl.BlockSpec((tw, wa_t), lambda m, n, k: (k, n)), (1, na_t), s, d, wa_t, gg["w_in"],
                                     pl.BlockSpec((d, wa_t), lambda m, n, k: (0, n)), f"swa_in_grad_{j}")
            dxv, red = modmm_bwd(dproj, wg["w_in"], (d, a_in), (0, 0), a_in, wa_t, s1["x"], dxv, mv, (l, 1),
                                 f"swa_in_bwd_{j}")
            dqk_gain[("a", j)] = (dgain[0, :d].reshape(hq, HEAD_DIM).sum(0) * HEAD_DIM ** -0.5,
                                  dgain[0, d:d + kw].reshape(hkv, HEAD_DIM).sum(0))
        else:
            lse_row = s1["lse"].reshape(hq, 1, s)
            d_qkv, df_col, dfq_row = fox_bwd(s1["qkn"], s1["proj"], s1["f_col"], s1["f_row"], lse_row, do, s1["o32"], d,
                                             f"fox_attn_bwd_{j}")
            lanes_of_heads = lambda a: jnp.pad(a.T, ((0, 0), (0, LANES - hq)))
            dfl, dbf = fox_gates_bwd(s1["fl"], s1["b_f"], lanes_of_heads(dfq_row[:, 0, :]), lanes_of_heads(df_col[:, :, 0]),
                                     f"fox_gates_bwd_{j}")
            db_f[j] = dbf[0, :hq]
            dproj, dgain = qknorm_bwd(s1["proj"], s1["gains"], d_qkv, pl.BlockSpec((None, tm, d), lambda c_, i: (c_, i, 0)),
                                      3 * d, d, f"fox_qknorm_bwd_{j}")
            gg["w_in"] = weight_grad(s1["h"], pl.BlockSpec((tw, d), lambda m, n, k: (k, 0)), dproj,
                                     pl.BlockSpec((tw, wb_t), lambda m, n, k: (k, n)), (1, nb_t), s, d, wb_t, gg["w_in"],
                                     pl.BlockSpec((d, wb_t), lambda m, n, k: (0, n)), f"fox_in_grad_{j}")
            gg["w_in"] = weight_grad(s1["h"], pl.BlockSpec((tw, d), lambda m, n, k: (k, 0)), dfl,
                                     pl.BlockSpec((tw, LANES), lambda m, n, k: (k, 0)), (1, 1), s, d, LANES, gg["w_in"],
                                     pl.BlockSpec((d, LANES), lambda m, n, k: (0, gate_blk)), f"fox_gate_in_grad_{j}")
            dxv, red = modmm_bwd(dproj, wg["w_in"], (d, 3 * d), (0, 0), 3 * d, wb_t, s1["x"], dxv, mv, (l, 1),
                                 f"fox_in_bwd_{j}", more=(dfl, (d, LANES), (0, gate_blk)))
            dqk_gain[("b", j)] = (dgain[0, :d].reshape(hq, HEAD_DIM).sum(0) * HEAD_DIM ** -0.5,
                                  dgain[0, d:2 * d].reshape(hq, HEAD_DIM).sum(0))
        dmod[l][1] = (red[1], red[2], dgate[0])
        dnorm[l][1] = red[0]
        dxv = ffn_backward(dxv, s0, l, 0, 0)
        if l >= 1:
            lands = [lax.empty((N_CHIP - 1,) + tuple(shp), BF16) for shp in shard_shapes_of(l)]
            scatter_in_flight[l] = exchange_start(layer_slabs(l), lands, kinds_of(l), False, f"scatter_start_{l}")
            mv = mv + scatter_in_flight[l][-1][0, 0]
    grad_x = dxv[None]

    drel = rel_bias_grad(jnp.stack(dbias_tabs), bucket)
    dmod_flat = jnp.stack([jnp.stack([jnp.stack(dmod[l][sub]) for sub in range(3)]) for l in range(n_layers)]).reshape(-1)
    dnorm_flat = jnp.stack([jnp.stack(dnorm[l]) for l in range(n_layers)]).reshape(-1)
    pieces = [dmod_flat, dnorm_flat,
              jnp.stack([dqk_gain[("a", j)][0] for j in range(n_a)]).reshape(-1),
              jnp.stack([dqk_gain[("a", j)][1] for j in range(n_a)]).reshape(-1),
              jnp.stack([dqk_gain[("b", j)][0] for j in range(n_b)]).reshape(-1),
              jnp.stack([dqk_gain[("b", j)][1] for j in range(n_b)]).reshape(-1),
              jnp.stack([dsink[j] for j in range(n_a)]).reshape(-1),
              jnp.stack([db_f[j] for j in range(n_b)]).reshape(-1),
              drel.reshape(-1)]
    rows = [_pad_rows(p) for p in pieces]
    starts = np.cumsum([0] + [r.shape[0] for r in rows])
    total = -(-int(starts[-1]) // 8) * 8
    small = jnp.pad(jnp.concatenate(rows), ((0, total - int(starts[-1])), (0, 0)))
    small_all = all_gather_rows(small, "gather_small_grads").reshape(N_DEV, total, LANES)
    small_sum = sum_devices(small_all)

    def piece(k, shape):
        n = int(np.prod(shape))
        return small_sum[int(starts[k]):int(starts[k + 1])].reshape(-1)[:n].reshape(shape)

    g_ada_b = piece(0, (n_layers, 9 * d))
    g_norm = lax.dynamic_slice_in_dim(piece(1, (n_layers, 3, d)), chip * cq, cq, axis=2)
    g_swa_q, g_swa_k = piece(2, (n_a, HEAD_DIM)), piece(3, (n_a, HEAD_DIM))
    g_fox_q, g_fox_k = piece(4, (n_b, HEAD_DIM)), piece(5, (n_b, HEAD_DIM))
    g_sink, g_bf, g_rel = piece(6, (n_a, hq)), piece(7, (n_b, hq)), piece(8, (REL_BUCKETS, hq))

    dmod_all = small_all[:, :int(starts[1])].reshape(N_DEV, -1)[:, :n_layers * 9 * d].reshape(N_DEV, n_layers, 9 * d)
    ada_cols = ada_w.shape[-1]
    dmod_mine = lax.dynamic_slice_in_dim(jnp.moveaxis(dmod_all, 0, 1), chip * ada_cols, ada_cols, axis=2)
    g_ada_w = ada_grad(c_all.T, dmod_mine)

    sources = {0: layer_slabs(0)}
    landed = {0: scatter_grads(sources[0], kinds_of(0), shard_shapes_of(0))}
    for l, (send_sems, recv_sems, slabs, lands, _) in scatter_in_flight.items():
        sources[l], landed[l] = exchange_wait(send_sems, recv_sems, slabs, lands, landed[0][0], kinds_of(l), False,
                                              f"scatter_wait_{l}")

    def layer_sum(l, t):
        shape = shard_shapes_of(l)[t]
        own = own_slab(sources[l][t], kinds_of(l)[t], chip, shape)
        return sum_slots(landed[l][t].reshape(N_CHIP - 1, -1, shape[-1]), own.reshape(-1, shape[-1]))

    every, even, odd = range(n_layers), range(0, n_layers, 2), range(1, n_layers, 2)
    parts = [jnp.concatenate([layer_sum(l, t) for l in layers])
             for layers, t in [(every, 0), (every, 1), (even, 2), (even, 3), (odd, 2), (odd, 3)]]
    others = swap_with_sibling(parts)

    def update(w, m, v, g, g2=None):
        w2d = w.reshape(-1, w.shape[-1])
        outs = adamw(w2d, m.reshape(w2d.shape), v.reshape(w2d.shape), g.reshape(w2d.shape) if g2 is None else g, g2)
        return [t.reshape(w.shape) for t in outs]

    big = [(ffn_w13, m_ffn_w13, v_ffn_w13), (ffn_w2, m_ffn_w2, v_ffn_w2), (swa_w_in, m_swa_w_in, v_swa_w_in),
           (swa_w_out, m_swa_w_out, v_swa_w_out), (fox_w_in, m_fox_w_in, v_fox_w_in), (fox_w_out, m_fox_w_out, v_fox_w_out)]
    big_out = [update(w, m, v, p, q) for (w, m, v), p, q in zip(big, parts, others)]
    r_ada_w = update(ada_w, m_ada_w, v_ada_w, g_ada_w)
    r_ada_b = update(ada_b, m_ada_b, v_ada_b, g_ada_b)
    r_norm = update(norm_g, m_norm_g, v_norm_g, g_norm)
    r_rel = update(rel_bias, m_rel_bias, v_rel_bias, g_rel)
    r_swa_q = update(swa_q_g, m_swa_q_g, v_swa_q_g, g_swa_q)
    r_swa_k = update(swa_k_g, m_swa_k_g, v_swa_k_g, g_swa_k)
    r_sink = update(swa_sink, m_swa_sink, v_swa_sink, g_sink)
    r_bf = update(fox_b_f, m_fox_b_f, v_fox_b_f, g_bf)
    r_fox_q = update(fox_q_g, m_fox_q_g, v_fox_q_g, g_fox_q)
    r_fox_k = update(fox_k_g, m_fox_k_g, v_fox_k_g, g_fox_k)
    per_weight = [r_ada_w, r_ada_b, r_norm, big_out[0], big_out[1], r_rel, big_out[2], big_out[3], r_swa_q, r_swa_k,
                  r_sink, big_out[4], big_out[5], r_bf, r_fox_q, r_fox_k]
    return (loss, grad_x, *[r[0] for r in per_weight], *[r[1] for r in per_weight],
            *[r[2] for r in per_weight], *[r[3] for r in per_weight])
```

```python
import math

import numpy as np
import jax
import jax.numpy as jnp
from jax import lax
from jax.experimental import pallas as pl
from jax.experimental.pallas import tpu as pltpu

F32 = jnp.float32
BF16 = jnp.bfloat16
HEAD_DIM = 64
GROUP = 4
FOX_HEADS = 8
FOX_HEADS_FWD = 16
BLOCK = 128
REL_BUCKETS = 32
REL_MAX_DIST = 128
EPS = 1e-6
NEG = -1e30
N_CHIP = 4
N_DEV = 8
LANES = 128
VMEM_LIMIT = 52 * 1024 * 1024
ADAM_LR, ADAM_B1, ADAM_B2, ADAM_EPS, ADAM_WD, ADAM_STEP = 0.001, 0.9, 0.999, 1e-08, 0.01, 10
MESH = pl.DeviceIdType.MESH
ANY = pl.BlockSpec(memory_space=pl.ANY)


def _params(n_axes):
    return pltpu.CompilerParams(dimension_semantics=("arbitrary",) * n_axes, vmem_limit_bytes=VMEM_LIMIT)


def _nn(a, b):
    return jnp.dot(a, b, preferred_element_type=F32)


def _nt(a, b):
    return lax.dot_general(a, b, (((1,), (1,)), ((), ())), preferred_element_type=F32)


def _tn(a, b):
    return lax.dot_general(a, b, (((0,), (0,)), ((), ())), preferred_element_type=F32)


def _sigmoid(z):
    return 1.0 / (1.0 + jnp.exp(-z))


def _sigmoid_fast(z):
    return pl.reciprocal(1.0 + jnp.exp(-z), approx=True)


def _row_tile(s):
    return 512 if s >= 2048 else s // 2


def _wide_tile(s):
    return 1024 if s >= 2048 else s // 2


def _attn_tile(s):
    return 512 if s >= 2048 else s // 4


def _position():
    x, y, c = lax.axis_index("x"), lax.axis_index("y"), lax.axis_index("c")
    chips = [(1 - x, y), (x, 1 - y), (1 - x, 1 - y)]
    return x, y, c, chips


def all_gather_rows(v, name):
    m_per, n = v.shape

    def body(x_ref, out_ref, send_sems, recv_sems, local_sem):
        x, y, c, chips = _position()
        me, sibling = (x, y, c), (x, y, 1 - c)

        def rows(px, py, pc):
            return out_ref.at[pl.ds((4 * px + 2 * py + pc) * m_per, m_per), :]

        def copy(k, block, to, src=None):
            return pltpu.make_async_remote_copy(
                src_ref=rows(*block) if src is None else src, dst_ref=rows(*block),
                send_sem=send_sems.at[k], recv_sem=recv_sems.at[k], device_id=to, device_id_type=MESH)

        mine = pltpu.make_async_copy(x_ref, rows(*me), local_sem)
        mine.start()
        first = [copy(0, me, sibling, src=x_ref)]
        first += [copy(1 + j, me, (*chip, c), src=x_ref) for j, chip in enumerate(chips)]
        for cp in first:
            cp.start()
        passed = [copy(4 + j, (*chip, c), sibling) for j, chip in enumerate(chips)]
        for j, chip in enumerate(chips):
            copy(1 + j, (*chip, c), me).wait_recv()
            passed[j].start()
        copy(0, sibling, me).wait_recv()
        for j, chip in enumerate(chips):
            copy(4 + j, (*chip, 1 - c), me).wait_recv()
        for cp in first + passed:
            cp.wait_send()
        mine.wait()

    return pl.pallas_call(
        body, name=name,
        out_shape=jax.ShapeDtypeStruct((N_DEV * m_per, n), v.dtype),
        in_specs=[pl.BlockSpec(memory_space=pltpu.VMEM)],
        out_specs=pl.BlockSpec(memory_space=pltpu.VMEM),
        scratch_shapes=[pltpu.SemaphoreType.DMA((7,)), pltpu.SemaphoreType.DMA((7,)), pltpu.SemaphoreType.DMA],
    )(v)


def _slab(full_ref, kind, b, lead):
    how = kind[0]
    if how == "slot":
        return full_ref.at[b, lead]
    if how == "col":
        w = kind[1]
        idx = (lead,) + (slice(None),) * (len(full_ref.shape) - 2) + (pl.ds(pl.multiple_of(b * w, LANES), w),)
        return full_ref.at[idx]
    h = kind[1]
    idx = (lead,) + (slice(None),) * (len(full_ref.shape) - 3) + (pl.ds(pl.multiple_of(b * h, 8), h), slice(None))
    return full_ref.at[idx]


def _full_shape(shard_shape, kind):
    if kind[0] == "slot":
        return (N_CHIP,) + tuple(shard_shape)
    if kind[0] == "col":
        return tuple(shard_shape[:-1]) + (N_CHIP * shard_shape[-1],)
    return tuple(shard_shape[:-2]) + (N_CHIP * shard_shape[-2], shard_shape[-1])


def _slab_start(shape, kind, b):
    zeros = [0] * len(shape)
    if kind[0] == "slot":
        return [b] + zeros[1:]
    if kind[0] == "col":
        return zeros[:-1] + [b * kind[1]]
    return zeros[:-2] + [b * kind[1], 0]


def cast_and_place(shard, kind, chip, after, name):
    rows, cols = shard.shape[-2:]
    lead = int(np.prod(shard.shape[:-2]))
    if kind[0] == "col":
        full3, where = (lead, rows, N_CHIP * cols), lambda p, b: (p, 0, b[0])
    elif kind[0] == "row":
        full3, where = (lead, N_CHIP * rows, cols), lambda p, b: (p, b[0], 0)
    else:
        full3, where = (N_CHIP * lead, rows, cols), lambda p, b: (b[0] * lead + p, 0, 0)

    def body(b_ref, s_ref, after_ref, o_ref):
        o_ref[...] = s_ref[...].astype(BF16)

    full = pl.pallas_call(
        body, name=name,
        grid_spec=pltpu.PrefetchScalarGridSpec(
            num_scalar_prefetch=1, grid=(lead,),
            in_specs=[pl.BlockSpec((1, rows, cols), lambda p, b: (p, 0, 0)), ANY],
            out_specs=pl.BlockSpec((1, rows, cols), where)),
        out_shape=jax.ShapeDtypeStruct(full3, BF16),
        compiler_params=_params(1),
    )(jnp.reshape(chip, (1,)).astype(jnp.int32), shard.reshape(lead, rows, cols), after)
    return full.reshape(_full_shape(shard.shape, kind))


def own_slab(full, kind, b, shard_shape):
    sizes = (1,) + tuple(shard_shape) if kind[0] == "slot" else tuple(shard_shape)
    return lax.dynamic_slice(full, _slab_start(full.shape, kind, b), sizes).reshape(shard_shape)


def gather_weights(fulls, kinds):
    n = len(fulls)

    def body(*refs):
        outs = refs[n:2 * n]
        send_sems, recv_sems = refs[2 * n:]
        x, y, c, chips = _position()
        b_me = 2 * x + y
        sibling = (x, y, 1 - c)
        sends = []

        def halves(t):
            lead = outs[t].shape[1] if kinds[t][0] == "slot" else outs[t].shape[0]
            return pl.ds(c * (lead // 2), lead // 2), pl.ds((1 - c) * (lead // 2), lead // 2)

        for t in range(n):
            mine, _ = halves(t)
            own = _slab(outs[t], kinds[t], b_me, mine)
            for j, chip in enumerate(chips):
                cp = pltpu.make_async_remote_copy(
                    src_ref=own, dst_ref=own,
                    send_sem=send_sems.at[6 * t + j], recv_sem=recv_sems.at[6 * t + j],
                    device_id=(*chip, c), device_id_type=MESH)
                cp.start()
                sends.append(cp)
        for t in range(n):
            mine, _ = halves(t)
            for j, chip in enumerate(chips):
                landed = _slab(outs[t], kinds[t], 2 * chip[0] + chip[1], mine)
                pltpu.make_async_remote_copy(
                    src_ref=landed, dst_ref=landed, send_sem=send_sems.at[6 * t + j], recv_sem=recv_sems.at[6 * t + j],
                    device_id=(*chip, c), device_id_type=MESH).wait_recv()
                cp = pltpu.make_async_remote_copy(
                    src_ref=landed, dst_ref=landed, send_sem=send_sems.at[6 * t + 3 + j],
                    recv_sem=recv_sems.at[6 * t + 3 + j], device_id=sibling, device_id_type=MESH)
                cp.start()
                sends.append(cp)
        for t in range(n):
            _, theirs = halves(t)
            for j, chip in enumerate(chips):
                landed = _slab(outs[t], kinds[t], 2 * chip[0] + chip[1], theirs)
                pltpu.make_async_remote_copy(
                    src_ref=landed, dst_ref=landed, send_sem=send_sems.at[6 * t + 3 + j],
                    recv_sem=recv_sems.at[6 * t + 3 + j], device_id=sibling, device_id_type=MESH).wait_recv()
        for cp in sends:
            cp.wait_send()

    return pl.pallas_call(
        body, name="gather_weights",
        out_shape=[jax.ShapeDtypeStruct(v.shape, v.dtype) for v in fulls],
        in_specs=[ANY] * n, out_specs=[ANY] * n,
        input_output_aliases={t: t for t in range(n)},
        scratch_shapes=[pltpu.SemaphoreType.DMA((6 * n,)), pltpu.SemaphoreType.DMA((6 * n,))],
    )(*fulls)


def scatter_grads(grads, kinds, shard_shapes):
    n = len(grads)

    def body(*refs):
        ins, outs = refs[:n], refs[n:2 * n]
        send_sems, recv_sems = refs[2 * n:]
        x, y, c, chips = _position()
        sends = []
        for t in range(n):
            whole = _whole(ins[t], kinds[t])
            for j, chip in enumerate(chips):
                cp = pltpu.make_async_remote_copy(
                    src_ref=_slab(ins[t], kinds[t], 2 * chip[0] + chip[1], whole), dst_ref=outs[t].at[j],
                    send_sem=send_sems.at[3 * t + j], recv_sem=recv_sems.at[3 * t + j],
                    device_id=(*chip, c), device_id_type=MESH)
                cp.start()
                sends.append(cp)
        for t in range(n):
            for j, chip in enumerate(chips):
                pltpu.make_async_remote_copy(
                    src_ref=outs[t].at[j], dst_ref=outs[t].at[j], send_sem=send_sems.at[3 * t + j],
                    recv_sem=recv_sems.at[3 * t + j], device_id=(*chip, c), device_id_type=MESH).wait_recv()
        for cp in sends:
            cp.wait_send()

    return pl.pallas_call(
        body, name="scatter_grads",
        out_shape=[jax.ShapeDtypeStruct((N_CHIP - 1,) + tuple(s), g.dtype) for g, s in zip(grads, shard_shapes)],
        in_specs=[ANY] * n, out_specs=[ANY] * n,
        scratch_shapes=[pltpu.SemaphoreType.DMA((3 * n,)), pltpu.SemaphoreType.DMA((3 * n,))],
    )(*grads)


def _whole(ref, kind):
    return pl.ds(0, ref.shape[1] if kind[0] == "slot" else ref.shape[0])


def _exchange_copies(srcs, lands, kinds, gather, send_sems, recv_sems):
    x, y, c, chips = _position()
    b_me = 2 * x + y
    out = []
    for t in range(len(lands)):
        for j, chip in enumerate(chips):
            b_j = 2 * chip[0] + chip[1]
            if gather:
                src = sent_to = _slab(lands[t], kinds[t], b_me, _whole(lands[t], kinds[t]))
                arrives = _slab(lands[t], kinds[t], b_j, _whole(lands[t], kinds[t]))
            else:
                src = _slab(srcs[t], kinds[t], b_j, _whole(srcs[t], kinds[t]))
                sent_to = arrives = lands[t].at[j]
            k = 3 * t + j
            send = pltpu.make_async_remote_copy(src_ref=src, dst_ref=sent_to, send_sem=send_sems.at[k],
                                                recv_sem=recv_sems.at[k], device_id=(*chip, c), device_id_type=MESH)
            recv = pltpu.make_async_remote_copy(src_ref=src, dst_ref=arrives, send_sem=send_sems.at[k],
                                                recv_sem=recv_sems.at[k], device_id=(*chip, c), device_id_type=MESH)
            out.append((send, recv))
    return out


def exchange_start(srcs, lands, kinds, gather, name):
    ns, nl = len(srcs), len(lands)
    hbm = pl.BlockSpec(memory_space=pltpu.HBM)

    def body(*refs):
        ins, lnd = refs[:ns], refs[ns:ns + nl]
        send_sems, recv_sems = refs[ns + nl], refs[ns + nl + 1]
        token = refs[-1]
        for send, _ in _exchange_copies(ins, lnd, kinds, gather, send_sems, recv_sems):
            send.start()
        token[...] = jnp.zeros_like(token)

    ops = [pltpu.with_memory_space_constraint(v, pltpu.HBM) for v in (*srcs, *lands)]
    res = pl.pallas_call(
        body, name=name,
        out_shape=(pltpu.SemaphoreType.DMA((3 * nl,)), pltpu.SemaphoreType.DMA((3 * nl,)),
                   *[pltpu.HBM(v.shape, v.dtype) for v in ops], jax.ShapeDtypeStruct((8, LANES), F32)),
        in_specs=[hbm] * (ns + nl),
        out_specs=(pl.BlockSpec(memory_space=pltpu.SEMAPHORE), pl.BlockSpec(memory_space=pltpu.SEMAPHORE),
                   *[hbm] * (ns + nl), pl.BlockSpec(memory_space=pltpu.VMEM)),
        input_output_aliases={t: 2 + t for t in range(ns + nl)},
        compiler_params=pltpu.CompilerParams(has_side_effects=pltpu.SideEffectType.DATAFLOW_SIDE_EFFECTING),
    )(*ops)
    return res[0], res[1], list(res[2:2 + ns]), list(res[2 + ns:2 + ns + nl]), res[-1]


def exchange_wait(send_sems, recv_sems, srcs, lands, after, kinds, gather, name):
    ns, nl = len(srcs), len(lands)
    hbm = pl.BlockSpec(memory_space=pltpu.HBM)

    def body(*refs):
        ins, lnd = refs[:ns], refs[ns:ns + nl]
        ssem, rsem = refs[ns + nl], refs[ns + nl + 1]
        for send, recv in _exchange_copies(ins, lnd, kinds, gather, ssem, rsem):
            send.wait_send()
            recv.wait_recv()

    res = pl.pallas_call(
        body, name=name,
        out_shape=tuple(pltpu.HBM(v.shape, v.dtype) for v in (*srcs, *lands)),
        in_specs=[hbm] * (ns + nl) + [pl.BlockSpec(memory_space=pltpu.SEMAPHORE)] * 2 + [ANY],
        out_specs=tuple([hbm] * (ns + nl)),
        input_output_aliases={t: t for t in range(ns + nl)},
        compiler_params=pltpu.CompilerParams(has_side_effects=pltpu.SideEffectType.DATAFLOW_SIDE_EFFECTING),
    )(*srcs, *lands, send_sems, recv_sems, after)
    return list(res[:ns]), list(res[ns:])


def swap_with_sibling(parts):
    n = len(parts)

    def body(*refs):
        ins, outs = refs[:n], refs[n:2 * n]
        send_sems, recv_sems = refs[2 * n:]
        x, y, c, _ = _position()
        cps = []
        for t in range(n):
            cp = pltpu.make_async_remote_copy(
                src_ref=ins[t], dst_ref=outs[t], send_sem=send_sems.at[t], recv_sem=recv_sems.at[t],
                device_id=(x, y, 1 - c), device_id_type=MESH)
            cp.start()
            cps.append(cp)
        for cp in cps:
            cp.wait_recv()
        for cp in cps:
            cp.wait_send()

    return pl.pallas_call(
        body, name="swap_with_sibling",
        out_shape=[jax.ShapeDtypeStruct(p.shape, p.dtype) for p in parts],
        in_specs=[ANY] * n, out_specs=[ANY] * n,
        scratch_shapes=[pltpu.SemaphoreType.DMA((n,)), pltpu.SemaphoreType.DMA((n,))],
    )(*parts)


def _modulated(xv, mv_ref):
    g, shift, scale = mv_ref[0:1, :], mv_ref[1:2, :], mv_ref[2:3, :]
    r = lax.rsqrt(jnp.mean(xv * xv, axis=-1, keepdims=True) + EPS)
    xhat = xv * r
    xn = xhat * g
    return xn * (1.0 + scale) + shift, xhat, xn, r, g, scale


def _resident(block_shape, index):
    return pl.BlockSpec(block_shape, lambda i: index, pipeline_mode=pl.Buffered(1))


def modmm(x, mv, mv_idx, w, w_block, w_index, n_cols, chunk, out_dtype, name, want_h):
    s, d = x.shape
    tm = _row_tile(s)

    def body(x_ref, mv_ref, w_ref, out_ref, *rest):
        h = _modulated(x_ref[...], mv_ref)[0].astype(BF16)
        if want_h:
            rest[0][...] = h
        for n in range(n_cols // chunk):
            cols = slice(n * chunk, (n + 1) * chunk)
            out_ref[:, cols] = _nn(h, w_ref[:, cols]).astype(out_dtype)

    row = pl.BlockSpec((tm, d), lambda i: (i, 0))
    out_shapes = [jax.ShapeDtypeStruct((s, n_cols), out_dtype)]
    out_specs = [pl.BlockSpec((tm, n_cols), lambda i: (i, 0))]
    if want_h:
        out_shapes.append(jax.ShapeDtypeStruct((s, d), BF16))
        out_specs.append(row)
    res = pl.pallas_call(
        body, name=name, grid=(s // tm,),
        in_specs=[row, pl.BlockSpec((None, None, 8, d), lambda i: (*mv_idx, 0, 0)), _resident(w_block, w_index)],
        out_specs=out_specs, out_shape=out_shapes,
        compiler_params=_params(1),
    )(x, mv, w)
    return res if want_h else (res[0], None)


def resmm(x, mv, mv_idx, coef, lhs, w, w_block, w_index, kdim, chunk, name, ffn):
    s, d = x.shape
    tm = _row_tile(s)
    lhs_cols = 2 * kdim if ffn else kdim

    def body(x_ref, mv_ref, lhs_ref, w_ref, xo_ref, y_ref, *rest):
        y = jnp.zeros((tm, d), F32)
        for k in range(kdim // chunk):
            rows = slice(k * chunk, (k + 1) * chunk)
            if ffn:
                ag = lhs_ref[:, rows].astype(F32)
                au = lhs_ref[:, kdim + k * chunk:kdim + (k + 1) * chunk].astype(F32)
                left = (ag * _sigmoid_fast(ag) * au).astype(BF16)
                rest[0][:, rows] = left
            else:
                left = lhs_ref[:, rows]
            y = y + _nn(left, w_ref[rows, :])
        y_ref[...] = y.astype(BF16)
        xo_ref[...] = x_ref[...] + (coef * mv_ref[3:4, :]) * y

    row = pl.BlockSpec((tm, d), lambda i: (i, 0))
    out_shapes = [jax.ShapeDtypeStruct((s, d), F32), jax.ShapeDtypeStruct((s, d), BF16)]
    out_specs = [row, row]
    if ffn:
        out_shapes.append(jax.ShapeDtypeStruct((s, kdim), BF16))
        out_specs.append(pl.BlockSpec((tm, kdim), lambda i: (i, 0)))
    return pl.pallas_call(
        body, name=name, grid=(s // tm,),
        in_specs=[row, pl.BlockSpec((None, None, 8, d), lambda i: (*mv_idx, 0, 0)),
                  pl.BlockSpec((tm, lhs_cols), lambda i: (i, 0)), _resident(w_block, w_index)],
        out_specs=out_specs, out_shape=out_shapes,
        compiler_params=_params(1),
    )(x, mv, lhs, w)


def resmm_bwd(dxo, y, mv, mv_idx, coef, w, w_block, w_index, kdim, chunk, name, a=None):
    s, d = dxo.shape
    ffn = a is not None
    tm = _row_tile(s) // 2 if ffn else _row_tile(s)
    dl_cols = 2 * kdim if ffn else kdim

    def body(dxo_ref, y_ref, mv_ref, w_ref, *rest):
        if ffn:
            a_ref, dy_ref, dl_ref, dgate_ref = rest
        else:
            dy_ref, dl_ref, dgate_ref = rest

        @pl.when(pl.program_id(0) == 0)
        def _():
            dgate_ref[...] = jnp.zeros_like(dgate_ref)

        dxv = dxo_ref[...]
        dy = ((coef * mv_ref[3:4, :]) * dxv).astype(BF16)
        dy_ref[...] = dy
        dgate_ref[0:1, :] += jnp.sum(coef * dxv * y_ref[...].astype(F32), axis=0, keepdims=True)
        for k in range(kdim // chunk):
            rows = slice(k * chunk, (k + 1) * chunk)
            dl = _nt(dy, w_ref[rows, :])
            if ffn:
                ups = slice(kdim + k * chunk, kdim + (k + 1) * chunk)
                ag = a_ref[:, rows].astype(F32)
                au = a_ref[:, ups].astype(F32)
                sg = _sigmoid_fast(ag)
                dl_ref[:, rows] = (dl * au * (sg * (1.0 + ag * (1.0 - sg)))).astype(BF16)
                dl_ref[:, ups] = (dl * (ag * sg)).astype(BF16)
            else:
                dl_ref[:, rows] = dl.astype(BF16)

    row = pl.BlockSpec((tm, d), lambda i: (i, 0))
    wide = pl.BlockSpec((tm, dl_cols), lambda i: (i, 0))
    in_specs = [row, row, pl.BlockSpec((None, None, 8, d), lambda i: (*mv_idx, 0, 0)), _resident(w_block, w_index)]
    ops = [dxo, y, mv, w]
    if ffn:
        in_specs.append(wide)
        ops.append(a)
    return pl.pallas_call(
        body, name=name, grid=(s // tm,),
        in_specs=in_specs,
        out_specs=[row, wide, pl.BlockSpec((8, d), lambda i: (0, 0))],
        out_shape=[jax.ShapeDtypeStruct((s, d), BF16), jax.ShapeDtypeStruct((s, dl_cols), BF16),
                   jax.ShapeDtypeStruct((8, d), F32)],
        compiler_params=_params(1),
    )(*ops)


def modmm_bwd(dl, w, w_block, w_index, n_cols, chunk, x, dxo, mv, mv_idx, name, more=None):
    s, d = x.shape
    tm = _row_tile(s)

    def body(dl_ref, w_ref, x_ref, dxo_ref, mv_ref, *rest):
        if more is not None:
            dl2_ref, w2_ref, dx_ref, red_ref = rest
            dh = _nt(dl2_ref[...], w2_ref[...])
        else:
            dx_ref, red_ref = rest
            dh = jnp.zeros((tm, d), F32)

        @pl.when(pl.program_id(0) == 0)
        def _():
            red_ref[...] = jnp.zeros_like(red_ref)

        for n in range(n_cols // chunk):
            cols = slice(n * chunk, (n + 1) * chunk)
            dh = dh + _nt(dl_ref[:, cols], w_ref[:, cols])
        _, xhat, xn, r, g, scale = _modulated(x_ref[...], mv_ref)
        dxn = dh * (1.0 + scale)
        red_ref[0:1, :] += jnp.sum(dxn * xhat, axis=0, keepdims=True)
        red_ref[1:2, :] += jnp.sum(dh, axis=0, keepdims=True)
        red_ref[2:3, :] += jnp.sum(dh * xn, axis=0, keepdims=True)
        gd = dxn * g
        dx_ref[...] = dxo_ref[...] + r * (gd - xhat * jnp.mean(gd * xhat, axis=-1, keepdims=True))

    row = pl.BlockSpec((tm, d), lambda i: (i, 0))
    in_specs = [pl.BlockSpec((tm, n_cols), lambda i: (i, 0)), _resident(w_block, w_index), row, row,
                pl.BlockSpec((None, None, 8, d), lambda i: (*mv_idx, 0, 0))]
    ops = [dl, w, x, dxo, mv]
    if more is not None:
        in_specs += [pl.BlockSpec((tm, more[0].shape[1]), lambda i: (i, 0)), _resident(more[1], more[2])]
        ops += [more[0], w]
    return pl.pallas_call(
        body, name=name, grid=(s // tm,),
        in_specs=in_specs,
        out_specs=[row, pl.BlockSpec((8, d), lambda i: (0, 0))],
        out_shape=[jax.ShapeDtypeStruct((s, d), F32), jax.ShapeDtypeStruct((8, d), F32)],
        compiler_params=_params(1),
    )(*ops)


def weight_grad(a, a_spec, b, b_spec, grid_mn, s, bm, bn, dest, out_spec, name):
    tk = _wide_tile(s)
    k_tiles = s // tk

    def body(a_ref, b_ref, dest_ref, out_ref, acc):
        k = pl.program_id(2)

        @pl.when(k == 0)
        def _():
            acc[...] = jnp.zeros_like(acc)

        acc[...] += _tn(a_ref[...], b_ref[...])

        @pl.when(k == k_tiles - 1)
        def _():
            out_ref[...] = acc[...].astype(out_ref.dtype)

    return pl.pallas_call(
        body, name=name, grid=(*grid_mn, k_tiles),
        in_specs=[a_spec, b_spec, ANY], out_specs=out_spec,
        out_shape=jax.ShapeDtypeStruct(dest.shape, dest.dtype),
        input_output_aliases={2: 0},
        scratch_shapes=[pltpu.VMEM((bm, bn), F32)],
        compiler_params=_params(3),
    )(a, b, dest)


def _head_mean(v):
    lane = lax.broadcasted_iota(jnp.int32, v.shape, 1)
    lo = jnp.sum(jnp.where(lane < HEAD_DIM, v, 0.0), axis=-1, keepdims=True)
    hi = jnp.sum(v, axis=-1, keepdims=True) - lo
    return jnp.where(lane < HEAD_DIM, lo, hi) * (1.0 / HEAD_DIM)


def qknorm_fwd(proj, gains, width, name):
    s = proj.shape[0]
    nqk = gains.shape[1]
    tm = _row_tile(s)

    def body(p_ref, g_ref, o_ref):
        for cc in range(width // LANES):
            sl = slice(cc * LANES, (cc + 1) * LANES)
            xv = p_ref[:, sl].astype(F32)
            r = lax.rsqrt(_head_mean(xv * xv) + EPS)
            o_ref[:, sl] = (xv * r * g_ref[:, sl]).astype(BF16)

    blk = pl.BlockSpec((tm, width), lambda i, c: (i, c))
    return pl.pallas_call(
        body, name=name, grid=(s // tm, nqk // width),
        in_specs=[blk, pl.BlockSpec((1, width), lambda i, c: (0, c))],
        out_specs=blk, out_shape=jax.ShapeDtypeStruct((s, nqk), BF16),
        compiler_params=_params(2),
    )(proj, gains)


def qknorm_bwd(proj, gains, d, d_spec, n_cols, width, name):
    s = proj.shape[0]
    nqk = gains.shape[1] // width
    n_blocks = n_cols // width
    tm = _row_tile(s)

    def body(p_ref, g_ref, d_ref, o_ref, dg_ref):
        c, i = pl.program_id(0), pl.program_id(1)

        @pl.when(i == 0)
        def _():
            dg_ref[...] = jnp.zeros_like(dg_ref)

        @pl.when(c < nqk)
        def _():
            for cc in range(width // LANES):
                sl = slice(cc * LANES, (cc + 1) * LANES)
                xv = p_ref[:, sl].astype(F32)
                r = lax.rsqrt(_head_mean(xv * xv) + EPS)
                xhat = xv * r
                dv = d_ref[:, sl]
                gd = dv * g_ref[:, sl]
                o_ref[:, sl] = (r * (gd - xhat * _head_mean(gd * xhat))).astype(BF16)
                dg_ref[0:1, sl] += jnp.sum(dv * xhat, axis=0, keepdims=True)

        @pl.when(c >= nqk)
        def _():
            o_ref[...] = d_ref[...].astype(BF16)

    return pl.pallas_call(
        body, name=name, grid=(n_blocks, s // tm),
        in_specs=[pl.BlockSpec((tm, width), lambda c, i: (i, c)),
                  pl.BlockSpec((1, width), lambda c, i: (0, jnp.minimum(c, nqk - 1))), d_spec],
        out_specs=[pl.BlockSpec((tm, width), lambda c, i: (i, c)), pl.BlockSpec((8, width), lambda c, i: (0, c))],
        out_shape=[jax.ShapeDtypeStruct((s, n_cols), BF16), jax.ShapeDtypeStruct((8, n_cols), F32)],
        compiler_params=_params(2),
    )(proj, gains, d)


def _swa_mask(first, copies):
    qi = (lax.broadcasted_iota(jnp.int32, (copies * BLOCK, 2 * BLOCK), 0) & (BLOCK - 1)) + BLOCK
    kj = lax.broadcasted_iota(jnp.int32, (copies * BLOCK, 2 * BLOCK), 1)
    dist = qi - kj
    return (dist >= 0) & (dist < BLOCK) & ((kj >= BLOCK) | jnp.logical_not(first))


def swa_fwd(qkn, proj, bias, sink, d, name):
    s = qkn.shape[0]
    hq = d // HEAD_DIM
    hkv = hq // GROUP
    kw = hkv * HEAD_DIM
    nblk = s // BLOCK
    kcol = d // kw

    def body(q_ref, kc_ref, kp_ref, vc_ref, vp_ref, bias_ref, sink_ref, o_ref, lse_ref):
        mask = _swa_mask(pl.program_id(0) == 0, 1)
        lse_ref[...] = jnp.zeros_like(lse_ref)
        for kvh in range(hkv):
            cols = slice(kvh * HEAD_DIM, (kvh + 1) * HEAD_DIM)
            k2 = jnp.concatenate([kp_ref[:, cols], kc_ref[:, cols]], axis=0)
            v2 = jnp.concatenate([vp_ref[:, cols], vc_ref[:, cols]], axis=0)
            for g in range(GROUP):
                h = kvh * GROUP + g
                hc = slice(h * HEAD_DIM, (h + 1) * HEAD_DIM)
                sc = jnp.where(mask, _nt(q_ref[:, hc], k2) + bias_ref[h], NEG)
                sk = sink_ref[0, h]
                m = jnp.maximum(jnp.max(sc, axis=-1, keepdims=True), sk)
                p = jnp.exp(sc - m)
                denom = jnp.sum(p, axis=-1, keepdims=True) + jnp.exp(sk - m)
                o_ref[:, hc] = (_nn(p.astype(BF16), v2) / denom).astype(BF16)
                lse_ref[:, h:h + 1] = m + jnp.log(denom)

    prev = lambda i: jnp.maximum(i - 1, 0)
    return pl.pallas_call(
        body, name=name, grid=(nblk,),
        in_specs=[pl.BlockSpec((BLOCK, d), lambda i: (i, 0)),
                  pl.BlockSpec((BLOCK, kw), lambda i: (i, kcol)),
                  pl.BlockSpec((BLOCK, kw), lambda i: (prev(i), kcol)),
                  pl.BlockSpec((BLOCK, kw), lambda i: (i, kcol + 1)),
                  pl.BlockSpec((BLOCK, kw), lambda i: (prev(i), kcol + 1)),
                  pl.BlockSpec((hq, BLOCK, 2 * BLOCK), lambda i: (0, 0, 0)),
                  pl.BlockSpec(memory_space=pltpu.SMEM)],
        out_specs=[pl.BlockSpec((BLOCK, d), lambda i: (i, 0)), pl.BlockSpec((BLOCK, LANES), lambda i: (i, 0))],
        out_shape=[jax.ShapeDtypeStruct((s, d), BF16), jax.ShapeDtypeStruct((s, LANES), F32)],
        compiler_params=_params(1),
    )(qkn, qkn, qkn, proj, proj, bias, sink)


def swa_bwd(qkn, proj, bias, sink, do, o, lse, d, name):
    s = qkn.shape[0]
    hq = d // HEAD_DIM
    hkv = hq // GROUP
    kw = hkv * HEAD_DIM
    nblk = s // BLOCK
    kcol = d // kw
    wide = d + 2 * kw

    def body(q_ref, kc_ref, kp_ref, vc_ref, vp_ref, bias_ref, sink_ref, do_ref, o_ref, lse_ref,
             out_ref, dbias_ref, dsink_ref, carry, fresh):
        i = pl.program_id(0)

        @pl.when(i == 0)
        def _():
            dbias_ref[...] = jnp.zeros_like(dbias_ref)
            dsink_ref[...] = jnp.zeros_like(dsink_ref)
            carry[...] = jnp.zeros_like(carry)

        @pl.when(i == nblk)
        def _():
            fresh[...] = jnp.zeros_like(fresh)

        @pl.when(i < nblk)
        def _():
            mask = _swa_mask(i == 0, GROUP)
            for kvh in range(hkv):
                cols = slice(kvh * HEAD_DIM, (kvh + 1) * HEAD_DIM)
                heads = range(kvh * GROUP, (kvh + 1) * GROUP)
                stack = lambda ref: jnp.concatenate([ref[:, h * HEAD_DIM:(h + 1) * HEAD_DIM] for h in heads], axis=0)
                k2 = jnp.concatenate([kp_ref[:, cols], kc_ref[:, cols]], axis=0)
                v2 = jnp.concatenate([vp_ref[:, cols], vc_ref[:, cols]], axis=0)
                qs, dos, os_ = stack(q_ref), stack(do_ref), stack(o_ref)
                bias_s = bias_ref[kvh * GROUP:(kvh + 1) * GROUP].reshape(GROUP * BLOCK, 2 * BLOCK)
                sk = jnp.concatenate([jnp.full((BLOCK, 1), sink_ref[0, h], F32) for h in heads], axis=0)
                lse_s = jnp.concatenate([lse_ref[:, h:h + 1] for h in heads], axis=0)
                sc = jnp.where(mask, _nt(qs, k2) + bias_s, NEG)
                p = jnp.exp(sc - lse_s)
                delta = jnp.sum(dos.astype(F32) * os_.astype(F32), axis=-1, keepdims=True)
                ds = p * (_nt(dos, v2) - delta)
                dbias_ref[kvh * GROUP:(kvh + 1) * GROUP] += ds.reshape(GROUP, BLOCK, 2 * BLOCK)
                to_sink = -jnp.exp(sk - lse_s) * delta
                dsb = ds.astype(BF16)
                dqs = _nn(dsb, k2)
                for g, h in enumerate(heads):
                    rows = slice(g * BLOCK, (g + 1) * BLOCK)
                    fresh[0, :, h * HEAD_DIM:(h + 1) * HEAD_DIM] = dqs[rows]
                    dsink_ref[0:1, h:h + 1] += jnp.sum(to_sink[rows], axis=0, keepdims=True)
                dk2 = _tn(dsb, qs)
                dv2 = _tn(p.astype(BF16), dos)
                kc_cols = slice(d + kvh * HEAD_DIM, d + (kvh + 1) * HEAD_DIM)
                vc_cols = slice(d + kw + kvh * HEAD_DIM, d + kw + (kvh + 1) * HEAD_DIM)
                fresh[0, :, kc_cols] = dk2[BLOCK:]
                fresh[0, :, vc_cols] = dv2[BLOCK:]
                fresh[1, :, kc_cols] = dk2[:BLOCK]
                fresh[1, :, vc_cols] = dv2[:BLOCK]

        lane = lax.broadcasted_iota(jnp.int32, (BLOCK, wide), 1)
        out_ref[...] = carry[...] + jnp.where(lane >= d, fresh[1], 0.0)

        @pl.when(i < nblk)
        def _():
            carry[...] = fresh[0]

    cur = lambda i: jnp.minimum(i, nblk - 1)
    prev = lambda i: jnp.maximum(jnp.minimum(i, nblk - 1) - 1, 0)
    return pl.pallas_call(
        body, name=name, grid=(nblk + 1,),
        in_specs=[pl.BlockSpec((BLOCK, d), lambda i: (cur(i), 0)),
                  pl.BlockSpec((BLOCK, kw), lambda i: (cur(i), kcol)),
                  pl.BlockSpec((BLOCK, kw), lambda i: (prev(i), kcol)),
                  pl.BlockSpec((BLOCK, kw), lambda i: (cur(i), kcol + 1)),
                  pl.BlockSpec((BLOCK, kw), lambda i: (prev(i), kcol + 1)),
                  pl.BlockSpec((hq, BLOCK, 2 * BLOCK), lambda i: (0, 0, 0)),
                  pl.BlockSpec(memory_space=pltpu.SMEM),
                  pl.BlockSpec((BLOCK, d), lambda i: (cur(i), 0)),
                  pl.BlockSpec((BLOCK, d), lambda i: (cur(i), 0)),
                  pl.BlockSpec((BLOCK, LANES), lambda i: (cur(i), 0))],
        out_specs=[pl.BlockSpec((BLOCK, wide), lambda i: (jnp.maximum(i - 1, 0), 0)),
                   pl.BlockSpec((hq, BLOCK, 2 * BLOCK), lambda i: (0, 0, 0)),
                   pl.BlockSpec((8, LANES), lambda i: (0, 0))],
        out_shape=[jax.ShapeDtypeStruct((s, wide), F32), jax.ShapeDtypeStruct((hq, BLOCK, 2 * BLOCK), F32),
                   jax.ShapeDtypeStruct((8, LANES), F32)],
        scratch_shapes=[pltpu.VMEM((BLOCK, wide), F32), pltpu.VMEM((2, BLOCK, wide), F32)],
        compiler_params=_params(1),
    )(qkn, qkn, qkn, proj, proj, bias, sink, do, o, lse)


def _rel_bucket_table():
    qi = np.arange(BLOCK)[:, None] + BLOCK
    kj = np.arange(2 * BLOCK)[None, :]
    n = np.maximum(qi - kj, 0)
    max_exact = REL_BUCKETS // 2
    nf = np.maximum(n, 1).astype(np.float32)
    large = max_exact + (np.log(nf / max_exact) / math.log(REL_MAX_DIST / max_exact)
                         * (REL_BUCKETS - max_exact)).astype(np.int32)
    large = np.minimum(large, REL_BUCKETS - 1)
    return np.where(n < max_exact, n, large).astype(np.int32)


def rel_bias_table(rel_bias, bucket):
    hq = rel_bias.shape[1]

    def body(rb_ref, bucket_ref, out_ref):
        tbl = bucket_ref[...]

        def per_head(h, carry):
            def per_bucket(b, acc):
                return jnp.where(tbl == b, rb_ref[b, h], acc)

            out_ref[h] = lax.fori_loop(0, REL_BUCKETS, per_bucket, jnp.zeros(tbl.shape, F32))
            return carry

        lax.fori_loop(0, hq, per_head, 0)

    return pl.pallas_call(
        body, name="rel_bias_table",
        in_specs=[pl.BlockSpec(memory_space=pltpu.SMEM), pl.BlockSpec(memory_space=pltpu.VMEM)],
        out_specs=pl.BlockSpec(memory_space=pltpu.VMEM),
        out_shape=jax.ShapeDtypeStruct((hq,) + tuple(bucket.shape), F32),
    )(rel_bias, bucket)


def rel_bias_grad(dbias, bucket):
    n_layers, hq = dbias.shape[:2]

    def body(db_ref, bucket_ref, out_ref):
        tbl = bucket_ref[...]

        def per_head(h, carry):
            dsum = db_ref[0, h]
            for a in range(1, n_layers):
                dsum = dsum + db_ref[a, h]

            def per_bucket(b, carry2):
                out_ref[b, h] = jnp.sum(jnp.where(tbl == b, dsum, 0.0))
                return carry2

            return lax.fori_loop(0, REL_BUCKETS, per_bucket, carry)

        lax.fori_loop(0, hq, per_head, 0)

    return pl.pallas_call(
        body, name="rel_bias_grad",
        in_specs=[pl.BlockSpec(memory_space=pltpu.VMEM), pl.BlockSpec(memory_space=pltpu.VMEM)],
        out_specs=pl.BlockSpec(memory_space=pltpu.SMEM),
        out_shape=jax.ShapeDtypeStruct((REL_BUCKETS, hq), F32),
    )(dbias, bucket)


def _split3(v):
    hi = v.astype(BF16)
    r1 = v - hi.astype(F32)
    mid = r1.astype(BF16)
    lo = (r1 - mid.astype(F32)).astype(BF16)
    return hi, mid, lo


def _tri_sum(tri, v):
    hi, mid, lo = _split3(v)
    return _nn(tri, hi) + _nn(tri, mid) + _nn(tri, lo)


def fox_gates(fl, b_f, name):
    s = fl.shape[0]
    t = _row_tile(s)

    def body(fl_ref, b_ref, f_ref, carry):
        @pl.when(pl.program_id(0) == 0)
        def _():
            carry[...] = jnp.zeros_like(carry)

        z = fl_ref[...] + b_ref[...]
        logf = jnp.minimum(z, 0.0) - jnp.log(1.0 + jnp.exp(-jnp.abs(z)))
        r = lax.broadcasted_iota(jnp.int32, (t, t), 0)
        cidx = lax.broadcasted_iota(jnp.int32, (t, t), 1)
        tri = jnp.where(cidx <= r, 1.0, 0.0).astype(BF16)
        f = _tri_sum(tri, logf) + carry[0:1, :]
        f_ref[...] = f
        carry[0:1, :] = f_ref[t - 1:t, :]

    blk = pl.BlockSpec((t, LANES), lambda i: (i, 0))
    return pl.pallas_call(
        body, name=name, grid=(s // t,),
        in_specs=[blk, pl.BlockSpec((1, LANES), lambda i: (0, 0))],
        out_specs=blk, out_shape=jax.ShapeDtypeStruct((s, LANES), F32),
        scratch_shapes=[pltpu.VMEM((8, LANES), F32)],
        compiler_params=_params(1),
    )(fl, b_f)


def fox_gates_bwd(fl, b_f, df_query, df_key, name):
    s = fl.shape[0]
    t = _row_tile(s)
    nb = s // t

    def body(fl_ref, b_ref, dfq_ref, dfk_ref, dfl_ref, db_ref, carry):
        @pl.when(pl.program_id(0) == 0)
        def _():
            carry[...] = jnp.zeros_like(carry)
            db_ref[...] = jnp.zeros_like(db_ref)

        dfv = dfq_ref[...] + dfk_ref[...]
        r = lax.broadcasted_iota(jnp.int32, (t, t), 0)
        cidx = lax.broadcasted_iota(jnp.int32, (t, t), 1)
        tri = jnp.where(cidx >= r, 1.0, 0.0).astype(BF16)
        dlog = _tri_sum(tri, dfv) + carry[0:1, :]
        carry[0:1, :] += jnp.sum(dfv, axis=0, keepdims=True)
        z = fl_ref[...] + b_ref[...]
        dz = dlog * (1.0 - _sigmoid(z))
        dfl_ref[...] = dz.astype(BF16)
        db_ref[0:1, :] += jnp.sum(dz, axis=0, keepdims=True)

    rev = pl.BlockSpec((t, LANES), lambda i: (nb - 1 - i, 0))
    return pl.pallas_call(
        body, name=name, grid=(nb,),
        in_specs=[rev, pl.BlockSpec((1, LANES), lambda i: (0, 0)), rev, rev],
        out_specs=[rev, pl.BlockSpec((8, LANES), lambda i: (0, 0))],
        out_shape=[jax.ShapeDtypeStruct((s, LANES), BF16), jax.ShapeDtypeStruct((8, LANES), F32)],
        scratch_shapes=[pltpu.VMEM((8, LANES), F32)],
        compiler_params=_params(1),
    )(fl, b_f, df_query, df_key)


def fox_fwd(qkn, proj, f_col, f_row, d, name):
    s = qkn.shape[0]
    t = _attn_tile(s)
    hs = min(FOX_HEADS_FWD, d // HEAD_DIM)
    wide = hs * HEAD_DIM
    n_pairs = d // wide
    nt = s // t

    def body(q_ref, k_ref, v_ref, fq_ref, fk_ref, o_ref, o32_ref, lse_ref, m_scr, l_scr, acc):
        i, j = pl.program_id(1), pl.program_id(2)

        @pl.when(j == 0)
        def _():
            m_scr[...] = jnp.full_like(m_scr, NEG)
            l_scr[...] = jnp.zeros_like(l_scr)
            acc[...] = jnp.zeros_like(acc)

        @pl.when(j <= i)
        def _():
            krow = lax.broadcasted_iota(jnp.int32, (t, t), 0)
            qcol = lax.broadcasted_iota(jnp.int32, (t, t), 1)
            visible = (krow <= qcol) | (j < i)
            for hh in range(hs):
                hc = slice(hh * HEAD_DIM, (hh + 1) * HEAD_DIM)
                st = _nt(k_ref[:, hc], q_ref[:, hc]) + fq_ref[hh] - fk_ref[hh]
                st = jnp.where(visible, st, NEG)
                m_prev = m_scr[hh]
                m_new = jnp.maximum(m_prev, jnp.max(st, axis=0, keepdims=True))
                alpha = jnp.exp(m_prev - m_new)
                pt = jnp.exp(st - m_new)
                l_scr[hh] = alpha * l_scr[hh] + jnp.sum(pt, axis=0, keepdims=True)
                acc[hc, :] = alpha * acc[hc, :] + _tn(v_ref[:, hc], pt.astype(BF16))
                m_scr[hh] = m_new

        @pl.when(j == i)
        def _():
            l_full = jnp.concatenate([jnp.broadcast_to(l_scr[hh], (HEAD_DIM, t)) for hh in range(hs)], axis=0)
            ov = (acc[...] / l_full).T
            o_ref[...] = ov.astype(BF16)
            o32_ref[...] = ov
            lse_ref[...] = m_scr[...] + jnp.log(l_scr[...])

    kv = lambda j, i: jnp.minimum(j, i)
    return pl.pallas_call(
        body, name=name, grid=(n_pairs, nt, nt),
        in_specs=[pl.BlockSpec((t, wide), lambda p, i, j: (i, p)),
                  pl.BlockSpec((t, wide), lambda p, i, j: (kv(j, i), n_pairs + p)),
                  pl.BlockSpec((t, wide), lambda p, i, j: (kv(j, i), 2 * n_pairs + p)),
                  pl.BlockSpec((hs, 1, t), lambda p, i, j: (p, 0, i)),
                  pl.BlockSpec((hs, t, 1), lambda p, i, j: (p, kv(j, i), 0))],
        out_specs=[pl.BlockSpec((t, wide), lambda p, i, j: (i, p)),
                   pl.BlockSpec((t, wide), lambda p, i, j: (i, p)),
                   pl.BlockSpec((hs, 1, t), lambda p, i, j: (p, 0, i))],
        out_shape=[jax.ShapeDtypeStruct((s, d), BF16), jax.ShapeDtypeStruct((s, d), F32),
                   jax.ShapeDtypeStruct((hs * n_pairs, 1, s), F32)],
        scratch_shapes=[pltpu.VMEM((hs, 1, t), F32), pltpu.VMEM((hs, 1, t), F32), pltpu.VMEM((wide, t), F32)],
        compiler_params=_params(3),
    )(qkn, qkn, proj, f_row, f_col)


def fox_bwd(qkn, proj, f_col, f_row, lse_row, do, o, d, name):
    s = qkn.shape[0]
    t = _attn_tile(s)
    hs, wide = FOX_HEADS, FOX_HEADS * HEAD_DIM
    n_pairs = d // wide
    nt = s // t

    def body(q_ref, k_ref, v_ref, fk_ref, fq_ref, lse_ref, do_ref, o_ref, out_ref, df_ref, dfq_ref,
             dq_acc, dkv_acc, df_acc, dfq_acc):
        j, i = pl.program_id(1), pl.program_id(2)

        @pl.when((j == 0) & (i == 0))
        def _():
            dq_acc[...] = jnp.zeros_like(dq_acc)
            dfq_acc[...] = jnp.zeros_like(dfq_acc)

        @pl.when(i == 0)
        def _():
            dkv_acc[...] = jnp.zeros_like(dkv_acc)
            df_acc[...] = jnp.zeros_like(df_acc)

        @pl.when(i >= j)
        def _():
            krow = lax.broadcasted_iota(jnp.int32, (t, t), 0)
            qcol = lax.broadcasted_iota(jnp.int32, (t, t), 1)
            visible = (krow <= qcol) | (i > j)
            ones = jnp.ones((8, HEAD_DIM), BF16)
            for hh in range(hs):
                hc = slice(hh * HEAD_DIM, (hh + 1) * HEAD_DIM)
                q, k, v, dov = q_ref[:, hc], k_ref[:, hc], v_ref[:, hc], do_ref[:, hc]
                st = _nt(k, q) + fq_ref[hh] - fk_ref[hh]
                pt = jnp.exp(jnp.where(visible, st, NEG) - lse_ref[hh])
                hi, mid, lo = _split3(dov.astype(F32) * o_ref[:, hc])
                delta = jnp.max(_nt(ones, hi) + _nt(ones, mid) + _nt(ones, lo), axis=0, keepdims=True)
                dst = pt * (_nt(v, dov) - delta)
                dsb = dst.astype(BF16)
                dkv_acc[1, :, hc] += _nn(pt.astype(BF16), dov)
                dkv_acc[0, :, hc] += _nn(dsb, q)
                dq_acc[pl.ds(pl.multiple_of(i * t, t), t), hc] += _tn(dsb, k)
                df_acc[hh] -= jnp.sum(dst, axis=-1, keepdims=True)
                dfq_acc[i, hh] += jnp.sum(dst, axis=0, keepdims=True)

        @pl.when(i == nt - 1)
        def _():
            out_ref[0] = dq_acc[pl.ds(pl.multiple_of(j * t, t), t), :]
            out_ref[1] = dkv_acc[0]
            out_ref[2] = dkv_acc[1]
            df_ref[...] = df_acc[...]
            dfq_ref[...] = dfq_acc[j]

    qi = lambda j, i: jnp.maximum(i, j)
    return pl.pallas_call(
        body, name=name, grid=(n_pairs, nt, nt),
        in_specs=[pl.BlockSpec((t, wide), lambda p, j, i: (qi(j, i), p)),
                  pl.BlockSpec((t, wide), lambda p, j, i: (j, n_pairs + p)),
                  pl.BlockSpec((t, wide), lambda p, j, i: (j, 2 * n_pairs + p)),
                  pl.BlockSpec((hs, t, 1), lambda p, j, i: (p, j, 0)),
                  pl.BlockSpec((hs, 1, t), lambda p, j, i: (p, 0, qi(j, i))),
                  pl.BlockSpec((hs, 1, t), lambda p, j, i: (p, 0, qi(j, i))),
                  pl.BlockSpec((t, wide), lambda p, j, i: (qi(j, i), p)),
                  pl.BlockSpec((t, wide), lambda p, j, i: (qi(j, i), p))],
        out_specs=[pl.BlockSpec((3, t, wide), lambda p, j, i: (0, j, p)),
                   pl.BlockSpec((hs, t, 1), lambda p, j, i: (p, j, 0)),
                   pl.BlockSpec((hs, 1, t), lambda p, j, i: (p, 0, j))],
        out_shape=[jax.ShapeDtypeStruct((3, s, d), F32), jax.ShapeDtypeStruct((hs * n_pairs, s, 1), F32),
                   jax.ShapeDtypeStruct((hs * n_pairs, 1, s), F32)],
        scratch_shapes=[pltpu.VMEM((s, wide), F32), pltpu.VMEM((2, t, wide), F32), pltpu.VMEM((hs, t, 1), F32),
                        pltpu.VMEM((nt, hs, 1, t), F32)],
        compiler_params=_params(3),
    )(qkn, qkn, proj, f_col, f_row, lse_row, do, o)


def loss_head(y, target):
    s, d = y.shape
    tm = _row_tile(s)

    def body(y_ref, t_ref, dy_ref, loss_ref):
        @pl.when(pl.program_id(0) == 0)
        def _():
            loss_ref[...] = jnp.zeros_like(loss_ref)

        diff = y_ref[...] - t_ref[...]
        dy_ref[...] = diff * (1.0 / d)
        loss_ref[...] += 0.5 * jnp.sum(jnp.mean(diff * diff, axis=-1, keepdims=True), axis=0, keepdims=True)

    row = pl.BlockSpec((tm, d), lambda i: (i, 0))
    return pl.pallas_call(
        body, name="loss_head", grid=(s // tm,),
        in_specs=[row, row],
        out_specs=[row, pl.BlockSpec((8, LANES), lambda i: (0, 0))],
        out_shape=[jax.ShapeDtypeStruct((s, d), F32), jax.ShapeDtypeStruct((8, LANES), F32)],
        compiler_params=_params(1),
    )(y, target)


def ada_mod(c_all, w, b):
    n_layers, d, cols = w.shape

    def body(c_ref, w_ref, b_ref, o_ref):
        cv = c_ref[...]
        o_ref[...] = _nn(cv * _sigmoid(cv), w_ref[...]) + b_ref[...]

    return pl.pallas_call(
        body, name="ada_mod", grid=(n_layers,),
        in_specs=[pl.BlockSpec((N_DEV, d), lambda l: (0, 0)), pl.BlockSpec((None, d, cols), lambda l: (l, 0, 0)),
                  pl.BlockSpec((None, 1, cols), lambda l: (l, 0, 0))],
        out_specs=pl.BlockSpec((None, N_DEV, cols), lambda l: (l, 0, 0)),
        out_shape=jax.ShapeDtypeStruct((n_layers, N_DEV, cols), F32),
        compiler_params=_params(1),
    )(c_all, w, b)


def ada_grad(c_t, dmod):
    d = c_t.shape[0]
    n_layers, _, cols = dmod.shape
    tn = cols // 2

    def body(c_ref, dm_ref, o_ref):
        cv = c_ref[...]
        o_ref[...] = _nn(cv * _sigmoid(cv), dm_ref[...])

    return pl.pallas_call(
        body, name="ada_grad", grid=(n_layers, 2),
        in_specs=[pl.BlockSpec((d, N_DEV), lambda l, n: (0, 0)), pl.BlockSpec((None, N_DEV, tn), lambda l, n: (l, 0, n))],
        out_specs=pl.BlockSpec((None, d, tn), lambda l, n: (l, 0, n)),
        out_shape=jax.ShapeDtypeStruct((n_layers, d, cols), F32),
        compiler_params=_params(2),
    )(c_t, dmod)


def sum_devices(v):
    def body(v_ref, o_ref):
        acc = v_ref[0]
        for k in range(1, N_DEV):
            acc = acc + v_ref[k]
        o_ref[...] = acc

    return pl.pallas_call(body, name="sum_devices", out_shape=jax.ShapeDtypeStruct(v.shape[1:], F32))(v)


def sum_slots(r, own):
    _, rows, cols = r.shape
    tm = 256 if rows % 256 == 0 else rows

    def body(r_ref, own_ref, o_ref):
        o_ref[...] = ((own_ref[...].astype(F32) + r_ref[0].astype(F32)) + r_ref[1].astype(F32)) + r_ref[2].astype(F32)

    return pl.pallas_call(
        body, name="sum_slots", grid=(rows // tm,),
        in_specs=[pl.BlockSpec((N_CHIP - 1, tm, cols), lambda i: (0, i, 0)), pl.BlockSpec((tm, cols), lambda i: (i, 0))],
        out_specs=pl.BlockSpec((tm, cols), lambda i: (i, 0)),
        out_shape=jax.ShapeDtypeStruct((rows, cols), F32),
        compiler_params=_params(1),
    )(r, own)


def adamw(w, m, v, g, g2=None):
    rows, cols = w.shape
    tm = 256 if rows % 256 == 0 else rows
    two = g2 is not None
    c1 = 1.0 - ADAM_B1 ** ADAM_STEP
    c2 = 1.0 - ADAM_B2 ** ADAM_STEP

    def body(w_ref, m_ref, v_ref, g_ref, *rest):
        if two:
            g2_ref, go_ref, d_ref, mo_ref, vo_ref = rest
            gv = g_ref[...] + g2_ref[...]
        else:
            go_ref, d_ref, mo_ref, vo_ref = rest
            gv = g_ref[...]
        mn = ADAM_B1 * m_ref[...] + (1.0 - ADAM_B1) * gv
        vn = ADAM_B2 * v_ref[...] + (1.0 - ADAM_B2) * (gv * gv)
        go_ref[...] = gv
        mo_ref[...] = mn
        vo_ref[...] = vn
        d_ref[...] = -ADAM_LR * ((mn / c1) / (jnp.sqrt(vn / c2) + ADAM_EPS) + ADAM_WD * w_ref[...])

    blk = pl.BlockSpec((tm, cols), lambda i: (i, 0))
    ops = [w, m, v, g] + ([g2] if two else [])
    return pl.pallas_call(
        body, name="adamw", grid=(rows // tm,),
        in_specs=[blk] * len(ops), out_specs=[blk] * 4,
        out_shape=[jax.ShapeDtypeStruct((rows, cols), F32)] * 4,
        compiler_params=_params(1),
    )(*ops)


def _pad_rows(flat):
    n = flat.shape[0]
    rows = -(-n // LANES)
    return jnp.pad(flat, (0, rows * LANES - n)).reshape(rows, LANES)


def _pad_rows8(flat):
    rows = _pad_rows(flat)
    return jnp.pad(rows, ((0, -rows.shape[0] % 8), (0, 0)))


def _col_tiles(n):
    return next(k for k in range(1, n // LANES + 1) if n % (k * LANES) == 0 and n // k <= 1536)


def kernel(x, c, ada_w, ada_b, norm_g, ffn_w13, ffn_w2, rel_bias, swa_w_in, swa_w_out, swa_q_g, swa_k_g, swa_sink, fox_w_in, fox_w_out, fox_b_f, fox_q_g, fox_k_g, loss_target, m_ada_w, m_ada_b, m_norm_g, m_ffn_w13, m_ffn_w2, m_rel_bias, m_swa_w_in, m_swa_w_out, m_swa_q_g, m_swa_k_g, m_swa_sink, m_fox_w_in, m_fox_w_out, m_fox_b_f, m_fox_q_g, m_fox_k_g, v_ada_w, v_ada_b, v_norm_g, v_ffn_w13, v_ffn_w2, v_rel_bias, v_swa_w_in, v_swa_w_out, v_swa_q_g, v_swa_k_g, v_swa_sink, v_fox_w_in, v_fox_w_out, v_fox_b_f, v_fox_q_g, v_fox_k_g):
    ix, iy, ic = lax.axis_index("x"), lax.axis_index("y"), lax.axis_index("c")
    chip = 2 * ix + iy
    dev = 2 * chip + ic
    s, d = x.shape[1:]
    n_layers = ada_w.shape[0]
    n_a, n_b = swa_w_in.shape[0], fox_w_in.shape[0]
    hq = d // HEAD_DIM
    hkv = hq // GROUP
    kw = hkv * HEAD_DIM
    c13 = ffn_w13.shape[-1]
    f = 2 * c13
    r2 = ffn_w2.shape[2]
    cq = norm_g.shape[-1]
    a_in = d + 2 * kw
    fx = fox_w_in.shape[-1]
    b_in = N_CHIP * fx
    b_pad = 3 * d + LANES
    x0 = x[0]

    hello = _pad_rows8(jnp.concatenate([c.reshape(-1), norm_g.reshape(-1)]))
    hello_all = all_gather_rows(hello, "gather_c_norm").reshape(N_DEV, -1)
    c_all = hello_all[:, :d]
    ng = hello_all[::2, d:d + n_layers * 3 * cq].reshape(N_CHIP, n_layers, 3, cq)
    norm_full = jnp.moveaxis(ng, 0, 2).reshape(n_layers, 3, d)

    half_cols = ada_w.shape[-1] // 2
    w_half = lax.dynamic_slice_in_dim(ada_w, ic * half_cols, half_cols, axis=2)
    b_half = lax.dynamic_slice_in_dim(ada_b, dev * half_cols, half_cols, axis=1)[:, None, :]
    mod_part = ada_mod(c_all, w_half, b_half)
    mod_all = all_gather_rows(mod_part.reshape(n_layers * N_DEV, half_cols), "gather_mod")
    mod_all = mod_all.reshape(N_DEV, n_layers, N_DEV, half_cols)
    mod_mine = lax.dynamic_index_in_dim(mod_all, dev, axis=2, keepdims=False)
    mod_mine = jnp.moveaxis(mod_mine, 0, 1).reshape(n_layers, 3, 3, d)
    mv = jnp.concatenate([norm_full[:, :, None, :], mod_mine, jnp.zeros((n_layers, 3, 4, d), F32)], axis=2)

    def kinds_of(l):
        mixer_in = ("col", swa_w_in.shape[-1]) if l % 2 == 0 else ("slot",)
        return [("col", c13), ("row", r2), mixer_in, ("slot",)]

    def layer_buffers(l, after):
        raw = [ffn_w13[l], ffn_w2[l]] + ([swa_w_in[l // 2], swa_w_out[l // 2]] if l % 2 == 0 else
                                         [fox_w_in[l // 2], fox_w_out[l // 2]])
        return [cast_and_place(r, k, chip, after, f"place_weights_{l}_{t}") for t, (r, k) in enumerate(zip(raw, kinds_of(l)))]

    def layer_weights(l, fulls):
        w13_l, w2_l, w_in, w_out = fulls
        if l % 2 == 1:
            w_in = jnp.pad(jnp.concatenate([w_in[b] for b in range(N_CHIP)], axis=-1), ((0, 0), (0, b_pad - b_in)))
        return dict(w13=w13_l, w2=w2_l, w_in=w_in, w_out=w_out.reshape(d, d))

    weights = [layer_weights(0, gather_weights(layer_buffers(0, mv), kinds_of(0)))]
    in_flight = []
    for l in range(1, n_layers):
        in_flight.append(exchange_start([], layer_buffers(l, weights[0]["w13"]), kinds_of(l), True, f"gather_start_{l}"))
    for *_, token in in_flight:
        mv = mv + token[0, 0]

    bucket = jnp.asarray(_rel_bucket_table())
    bias = rel_bias_table(rel_bias, bucket)
    tm, tw = _row_tile(s), _wide_tile(s)
    n13 = 2 * f // c13
    na_t, nb_t = _col_tiles(a_in), _col_tiles(3 * d)
    wa_t, wb_t = a_in // na_t, 3 * d // nb_t
    gate_blk = 3 * d // LANES

    def ffn_forward(xv, l, half, sub):
        wg = weights[l]
        a, h = modmm(xv, mv, (l, sub), wg["w13"], (None, d, 2 * f), (half, 0, 0), 2 * f, c13, BF16,
                     f"ffn_up_{l}_{half}", True)
        xo, y, u = resmm(xv, mv, (l, sub), 0.5, a, wg["w2"], (None, f, d), (half, 0, 0), f, c13,
                         f"ffn_down_{l}_{half}", True)
        return xo, dict(x=xv, h=h, a=a, u=u, y=y)

    saved = []
    xv = x0
    for l in range(n_layers):
        j = l // 2
        if l >= 1:
            send_sems, recv_sems, shards, fulls, _ = in_flight[l - 1]
            weights.append(layer_weights(l, exchange_wait(send_sems, recv_sems, shards, fulls, xv, kinds_of(l), True,
                                                          f"gather_wait_{l}")[1]))
        wg = weights[l]
        xv, s0 = ffn_forward(xv, l, 0, 0)
        if l % 2 == 0:
            proj, h = modmm(xv, mv, (l, 1), wg["w_in"], (d, a_in), (0, 0), a_in, wa_t, BF16, f"swa_in_{j}", True)
            gains = jnp.concatenate([jnp.tile(swa_q_g[j] * HEAD_DIM ** -0.5, hq), jnp.tile(swa_k_g[j], hkv)])[None, :]
            qkn = qknorm_fwd(proj, gains, kw, f"swa_qknorm_{j}")
            sink = swa_sink[j][None, :]
            o, lse = swa_fwd(qkn, proj, bias, sink, d, f"swa_attn_{j}")
            s1 = dict(x=xv, h=h, proj=proj, gains=gains, qkn=qkn, sink=sink, o=o, lse=lse)
        else:
            proj, h = modmm(xv, mv, (l, 1), wg["w_in"], (d, 3 * d), (0, 0), 3 * d, wb_t, BF16, f"fox_in_{j}", True)
            fl, _ = modmm(xv, mv, (l, 1), wg["w_in"], (d, LANES), (0, gate_blk), LANES, LANES, F32,
                          f"fox_gate_in_{j}", False)
            b_f = jnp.pad(fox_b_f[j], (0, LANES - hq))[None, :]
            fcum = fox_gates(fl, b_f, f"fox_gates_{j}")
            f_t = fcum[:, :hq].T
            f_col, f_row = f_t[:, :, None], f_t[:, None, :]
            gains = jnp.concatenate([jnp.tile(fox_q_g[j] * HEAD_DIM ** -0.5, hq), jnp.tile(fox_k_g[j], hq)])[None, :]
            qkn = qknorm_fwd(proj, gains, d, f"fox_qknorm_{j}")
            o, o32, lse = fox_fwd(qkn, proj, f_col, f_row, d, f"fox_attn_{j}")
            s1 = dict(x=xv, h=h, proj=proj, gains=gains, qkn=qkn, fl=fl, b_f=b_f, f_col=f_col, f_row=f_row, o=o, o32=o32,
                      lse=lse)
        xv, y = resmm(xv, mv, (l, 1), 1.0, o, wg["w_out"], (d, d), (0, 0), d, d, f"mixer_out_{l}", False)
        s1["y"] = y
        xv, s2 = ffn_forward(xv, l, 1, 2)
        saved.append((s0, s1, s2))

    dxv, loss_part = loss_head(xv, loss_target[0])
    loss = lax.psum(loss_part[0, 0], ("x", "y", "c"))

    grads = [dict(w13=lax.empty((2, d, 2 * f), BF16), w2=lax.empty((2, f, d), BF16),
                  w_in=lax.empty((d, a_in if l % 2 == 0 else b_pad), BF16), w_out=lax.empty((d, d), BF16))
             for l in range(n_layers)]
    dmod = [[None] * 3 for _ in range(n_layers)]
    dnorm = [[None] * 3 for _ in range(n_layers)]
    dqk_gain = {}
    dsink, db_f, dbias_tabs = {}, {}, []

    def ffn_backward(dxo, sv, l, half, sub):
        wg, gg = weights[l], grads[l]
        dy, da, dgate = resmm_bwd(dxo, sv["y"], mv, (l, sub), 0.5, wg["w2"], (None, f, d), (half, 0, 0), f, c13,
                                  f"ffn_down_bwd_{l}_{half}", a=sv["a"])
        gg["w2"] = weight_grad(sv["u"], pl.BlockSpec((tw, c13), lambda m, n, k: (k, m)), dy,
                               pl.BlockSpec((tw, d), lambda m, n, k: (k, 0)), (f // c13, 1), s, c13, d, gg["w2"],
                               pl.BlockSpec((None, c13, d), lambda m, n, k: (half, m, 0)), f"ffn_w2_grad_{l}_{half}")
        gg["w13"] = weight_grad(sv["h"], pl.BlockSpec((tw, d), lambda m, n, k: (k, 0)), da,
                                pl.BlockSpec((tw, c13), lambda m, n, k: (k, n)), (1, n13), s, d, c13,
                                gg["w13"], pl.BlockSpec((None, d, c13), lambda m, n, k: (half, 0, n)),
                                f"ffn_w13_grad_{l}_{half}")
        dx, red = modmm_bwd(da, wg["w13"], (None, d, 2 * f), (half, 0, 0), 2 * f, c13, sv["x"], dxo, mv,
                            (l, sub), f"ffn_up_bwd_{l}_{half}")
        dmod[l][sub] = (red[1], red[2], dgate[0])
        dnorm[l][sub] = red[0]
        return dx

    def layer_slabs(l):
        gg = grads[l]
        g_in = gg["w_in"] if l % 2 == 0 else jnp.stack([gg["w_in"][:, b * fx:(b + 1) * fx] for b in range(N_CHIP)])
        return [gg["w13"], gg["w2"], g_in, gg["w_out"].reshape(N_CHIP, d // N_CHIP, d)]

    def shard_shapes_of(l):
        mixer = (swa_w_in, swa_w_out) if l % 2 == 0 else (fox_w_in, fox_w_out)
        return [ffn_w13.shape[1:], ffn_w2.shape[1:], mixer[0].shape[1:], mixer[1].shape[1:]]

    scatter_in_flight = {}

    for l in reversed(range(n_layers)):
        j = l // 2
        wg, gg = weights[l], grads[l]
        s0, s1, s2 = saved[l]
        dxv = ffn_backward(dxv, s2, l, 1, 2)
        is_a = l % 2 == 0
        w_out = wg["w_out"]
        dy, do, dgate = resmm_bwd(dxv, s1["y"], mv, (l, 1), 1.0, w_out, (d, d), (0, 0), d, d, f"mixer_out_bwd_{l}")
        gg["w_out"] = weight_grad(s1["o"], pl.BlockSpec((tw, d), lambda m, n, k: (k, 0)), dy,
                                  pl.BlockSpec((tw, d), lambda m, n, k: (k, 0)), (1, 1), s, d, d, gg["w_out"],
                                  pl.BlockSpec((d, d), lambda m, n, k: (0, 0)), f"mixer_out_grad_{l}")
        if is_a:
            d_qkv, dbias_tab, dsk = swa_bwd(s1["qkn"], s1["proj"], bias, s1["sink"], do, s1["o"], s1["lse"], d,
                                            f"swa_attn_bwd_{j}")
            dbias_tabs.append(dbias_tab)
            dsink[j] = dsk[0, :hq]
            dproj, dgain = qknorm_bwd(s1["proj"], s1["gains"], d_qkv, pl.BlockSpec((tm, kw), lambda c_, i: (i, c_)),
                                      a_in, kw, f"swa_qknorm_bwd_{j}")
            gg["w_in"] = weight_grad(s1["h"], pl.BlockSpec((tw, d), lambda m, n, k: (k, 0)), dproj,
                                     pl.BlockSpec((tw, wa_t), lambda m, n, k: (k, n)), (1, na_t), s, d, wa_t, gg["w_in"],
                                     pl.BlockSpec((d, wa_t), lambda m, n, k: (0, n)), f"swa_in_grad_{j}")
            dxv, red = modmm_bwd(dproj, wg["w_in"], (d, a_in), (0, 0), a_in, wa_t, s1["x"], dxv, mv, (l, 1),
                                 f"swa_in_bwd_{j}")
            dqk_gain[("a", j)] = (dgain[0, :d].reshape(hq, HEAD_DIM).sum(0) * HEAD_DIM ** -0.5,
                                  dgain[0, d:d + kw].reshape(hkv, HEAD_DIM).sum(0))
        else:
            lse_row = s1["lse"].reshape(hq, 1, s)
            d_qkv, df_col, dfq_row = fox_bwd(s1["qkn"], s1["proj"], s1["f_col"], s1["f_row"], lse_row, do, s1["o32"], d,
                                             f"fox_attn_bwd_{j}")
            lanes_of_heads = lambda a: jnp.pad(a.T, ((0, 0), (0, LANES - hq)))
            dfl, dbf = fox_gates_bwd(s1["fl"], s1["b_f"], lanes_of_heads(dfq_row[:, 0, :]), lanes_of_heads(df_col[:, :, 0]),
                                     f"fox_gates_bwd_{j}")
            db_f[j] = dbf[0, :hq]
            dproj, dgain = qknorm_bwd(s1["proj"], s1["gains"], d_qkv, pl.BlockSpec((None, tm, d), lambda c_, i: (c_, i, 0)),
                                      3 * d, d, f"fox_qknorm_bwd_{j}")
            gg["w_in"] = weight_grad(s1["h"], pl.BlockSpec((tw, d), lambda m, n, k: (k, 0)), dproj,
                                     pl.BlockSpec((tw, wb_t), lambda m, n, k: (k, n)), (1, nb_t), s, d, wb_t, gg["w_in"],
                                     pl.BlockSpec((d, wb_t), lambda m, n, k: (0, n)), f"fox_in_grad_{j}")
            gg["w_in"] = weight_grad(s1["h"], pl.BlockSpec((tw, d), lambda m, n, k: (k, 0)), dfl,
                                     pl.BlockSpec((tw, LANES), lambda m, n, k: (k, 0)), (1, 1), s, d, LANES, gg["w_in"],
                                     pl.BlockSpec((d, LANES), lambda m, n, k: (0, gate_blk)), f"fox_gate_in_grad_{j}")
            dxv, red = modmm_bwd(dproj, wg["w_in"], (d, 3 * d), (0, 0), 3 * d, wb_t, s1["x"], dxv, mv, (l, 1),
                                 f"fox_in_bwd_{j}", more=(dfl, (d, LANES), (0, gate_blk)))
            dqk_gain[("b", j)] = (dgain[0, :d].reshape(hq, HEAD_DIM).sum(0) * HEAD_DIM ** -0.5,
                                  dgain[0, d:2 * d].reshape(hq, HEAD_DIM).sum(0))
        dmod[l][1] = (red[1], red[2], dgate[0])
        dnorm[l][1] = red[0]
        dxv = ffn_backward(dxv, s0, l, 0, 0)
        if l >= 1:
            lands = [lax.empty((N_CHIP - 1,) + tuple(shp), BF16) for shp in shard_shapes_of(l)]
            scatter_in_flight[l] = exchange_start(layer_slabs(l), lands, kinds_of(l), False, f"scatter_start_{l}")
            mv = mv + scatter_in_flight[l][-1][0, 0]
    grad_x = dxv[None]

    drel = rel_bias_grad(jnp.stack(dbias_tabs), bucket)
    dmod_flat = jnp.stack([jnp.stack([jnp.stack(dmod[l][sub]) for sub in range(3)]) for l in range(n_layers)]).reshape(-1)
    dnorm_flat = jnp.stack([jnp.stack(dnorm[l]) for l in range(n_layers)]).reshape(-1)
    pieces = [dmod_flat, dnorm_flat,
              jnp.stack([dqk_gain[("a", j)][0] for j in range(n_a)]).reshape(-1),
              jnp.stack([dqk_gain[("a", j)][1] for j in range(n_a)]).reshape(-1),
              jnp.stack([dqk_gain[("b", j)][0] for j in range(n_b)]).reshape(-1),
              jnp.stack([dqk_gain[("b", j)][1] for j in range(n_b)]).reshape(-1),
              jnp.stack([dsink[j] for j in range(n_a)]).reshape(-1),
              jnp.stack([db_f[j] for j in range(n_b)]).reshape(-1),
              drel.reshape(-1)]
    rows = [_pad_rows(p) for p in pieces]
    starts = np.cumsum([0] + [r.shape[0] for r in rows])
    total = -(-int(starts[-1]) // 8) * 8
    small = jnp.pad(jnp.concatenate(rows), ((0, total - int(starts[-1])), (0, 0)))
    small_all = all_gather_rows(small, "gather_small_grads").reshape(N_DEV, total, LANES)
    small_sum = sum_devices(small_all)

    def piece(k, shape):
        n = int(np.prod(shape))
        return small_sum[int(starts[k]):int(starts[k + 1])].reshape(-1)[:n].reshape(shape)

    g_ada_b = piece(0, (n_layers, 9 * d))
    g_norm = lax.dynamic_slice_in_dim(piece(1, (n_layers, 3, d)), chip * cq, cq, axis=2)
    g_swa_q, g_swa_k = piece(2, (n_a, HEAD_DIM)), piece(3, (n_a, HEAD_DIM))
    g_fox_q, g_fox_k = piece(4, (n_b, HEAD_DIM)), piece(5, (n_b, HEAD_DIM))
    g_sink, g_bf, g_rel = piece(6, (n_a, hq)), piece(7, (n_b, hq)), piece(8, (REL_BUCKETS, hq))

    dmod_all = small_all[:, :int(starts[1])].reshape(N_DEV, -1)[:, :n_layers * 9 * d].reshape(N_DEV, n_layers, 9 * d)
    ada_cols = ada_w.shape[-1]
    dmod_mine = lax.dynamic_slice_in_dim(jnp.moveaxis(dmod_all, 0, 1), chip * ada_cols, ada_cols, axis=2)
    g_ada_w = ada_grad(c_all.T, dmod_mine)

    sources = {0: layer_slabs(0)}
    landed = {0: scatter_grads(sources[0], kinds_of(0), shard_shapes_of(0))}
    for l, (send_sems, recv_sems, slabs, lands, _) in scatter_in_flight.items():
        sources[l], landed[l] = exchange_wait(send_sems, recv_sems, slabs, lands, landed[0][0], kinds_of(l), False,
                                              f"scatter_wait_{l}")

    def layer_sum(l, t):
        shape = shard_shapes_of(l)[t]
        own = own_slab(sources[l][t], kinds_of(l)[t], chip, shape)
        return sum_slots(landed[l][t].reshape(N_CHIP - 1, -1, shape[-1]), own.reshape(-1, shape[-1]))

    every, even, odd = range(n_layers), range(0, n_layers, 2), range(1, n_layers, 2)
    parts = [jnp.concatenate([layer_sum(l, t) for l in layers])
             for layers, t in [(every, 0), (every, 1), (even, 2), (even, 3), (odd, 2), (odd, 3)]]
    others = swap_with_sibling(parts)

    def update(w, m, v, g, g2=None):
        w2d = w.reshape(-1, w.shape[-1])
        outs = adamw(w2d, m.reshape(w2d.shape), v.reshape(w2d.shape), g.reshape(w2d.shape) if g2 is None else g, g2)
        return [t.reshape(w.shape) for t in outs]

    big = [(ffn_w13, m_ffn_w13, v_ffn_w13), (ffn_w2, m_ffn_w2, v_ffn_w2), (swa_w_in, m_swa_w_in, v_swa_w_in),
           (swa_w_out, m_swa_w_out, v_swa_w_out), (fox_w_in, m_fox_w_in, v_fox_w_in), (fox_w_out, m_fox_w_out, v_fox_w_out)]
    big_out = [update(w, m, v, p, q) for (w, m, v), p, q in zip(big, parts, others)]
    r_ada_w = update(ada_w, m_ada_w, v_ada_w, g_ada_w)
    r_ada_b = update(ada_b, m_ada_b, v_ada_b, g_ada_b)
    r_norm = update(norm_g, m_norm_g, v_norm_g, g_norm)
    r_rel = update(rel_bias, m_rel_bias, v_rel_bias, g_rel)
    r_swa_q = update(swa_q_g, m_swa_q_g, v_swa_q_g, g_swa_q)
    r_swa_k = update(swa_k_g, m_swa_k_g, v_swa_k_g, g_swa_k)
    r_sink = update(swa_sink, m_swa_sink, v_swa_sink, g_sink)
    r_bf = update(fox_b_f, m_fox_b_f, v_fox_b_f, g_bf)
    r_fox_q = update(fox_q_g, m_fox_q_g, v_fox_q_g, g_fox_q)
    r_fox_k = update(fox_k_g, m_fox_k_g, v_fox_k_g, g_fox_k)
    per_weight = [r_ada_w, r_ada_b, r_norm, big_out[0], big_out[1], r_rel, big_out[2], big_out[3], r_swa_q, r_swa_k,
                  r_sink, big_out[4], big_out[5], r_bf, r_fox_q, r_fox_k]
    return (loss, grad_x, *[r[0] for r in per_weight], *[r[1] for r in per_weight],
            *[r[2] for r in per_weight], *[r[3] for r in per_weight])
```

```python
import math

import numpy as np
import jax
import jax.numpy as jnp
from jax import lax
from jax.experimental import pallas as pl
from jax.experimental.pallas import tpu as pltpu

F32 = jnp.float32
BF16 = jnp.bfloat16
HEAD_DIM = 64
GROUP = 4
FOX_HEADS = 8
FOX_HEADS_FWD = 16
BLOCK = 128
REL_BUCKETS = 32
REL_MAX_DIST = 128
EPS = 1e-6
NEG = -1e30
N_CHIP = 4
N_DEV = 8
LANES = 128
VMEM_LIMIT = 52 * 1024 * 1024
ADAM_LR, ADAM_B1, ADAM_B2, ADAM_EPS, ADAM_WD, ADAM_STEP = 0.001, 0.9, 0.999, 1e-08, 0.01, 10
MESH = pl.DeviceIdType.MESH
ANY = pl.BlockSpec(memory_space=pl.ANY)


def _params(n_axes):
    return pltpu.CompilerParams(dimension_semantics=("arbitrary",) * n_axes, vmem_limit_bytes=VMEM_LIMIT)


def _nn(a, b):
    return jnp.dot(a, b, preferred_element_type=F32)


def _nt(a, b):
    return lax.dot_general(a, b, (((1,), (1,)), ((), ())), preferred_element_type=F32)


def _tn(a, b):
    return lax.dot_general(a, b, (((0,), (0,)), ((), ())), preferred_element_type=F32)


def _sigmoid(z):
    return 1.0 / (1.0 + jnp.exp(-z))


def _sigmoid_fast(z):
    return pl.reciprocal(1.0 + jnp.exp(-z), approx=True)


def _row_tile(s):
    return 512 if s >= 2048 else s // 2


def _wide_tile(s):
    return 1024 if s >= 2048 else s // 2


def _attn_tile(s):
    return 512 if s >= 2048 else s // 4


def _position():
    x, y, c = lax.axis_index("x"), lax.axis_index("y"), lax.axis_index("c")
    chips = [(1 - x, y), (x, 1 - y), (1 - x, 1 - y)]
    return x, y, c, chips


def all_gather_rows(v, name):
    m_per, n = v.shape

    def body(x_ref, out_ref, send_sems, recv_sems, local_sem):
        x, y, c, chips = _position()
        me, sibling = (x, y, c), (x, y, 1 - c)

        def rows(px, py, pc):
            return out_ref.at[pl.ds((4 * px + 2 * py + pc) * m_per, m_per), :]

        def copy(k, block, to, src=None):
            return pltpu.make_async_remote_copy(
                src_ref=rows(*block) if src is None else src, dst_ref=rows(*block),
                send_sem=send_sems.at[k], recv_sem=recv_sems.at[k], device_id=to, device_id_type=MESH)

        mine = pltpu.make_async_copy(x_ref, rows(*me), local_sem)
        mine.start()
        first = [copy(0, me, sibling, src=x_ref)]
        first += [copy(1 + j, me, (*chip, c), src=x_ref) for j, chip in enumerate(chips)]
        for cp in first:
            cp.start()
        passed = [copy(4 + j, (*chip, c), sibling) for j, chip in enumerate(chips)]
        for j, chip in enumerate(chips):
            copy(1 + j, (*chip, c), me).wait_recv()
            passed[j].start()
        copy(0, sibling, me).wait_recv()
        for j, chip in enumerate(chips):
            copy(4 + j, (*chip, 1 - c), me).wait_recv()
        for cp in first + passed:
            cp.wait_send()
        mine.wait()

    return pl.pallas_call(
        body, name=name,
        out_shape=jax.ShapeDtypeStruct((N_DEV * m_per, n), v.dtype),
        in_specs=[pl.BlockSpec(memory_space=pltpu.VMEM)],
        out_specs=pl.BlockSpec(memory_space=pltpu.VMEM),
        scratch_shapes=[pltpu.SemaphoreType.DMA((7,)), pltpu.SemaphoreType.DMA((7,)), pltpu.SemaphoreType.DMA],
    )(v)


def _slab(full_ref, kind, b, lead):
    how = kind[0]
    if how == "slot":
        return full_ref.at[b, lead]
    if how == "col":
        w = kind[1]
        idx = (lead,) + (slice(None),) * (len(full_ref.shape) - 2) + (pl.ds(pl.multiple_of(b * w, LANES), w),)
        return full_ref.at[idx]
    h = kind[1]
    idx = (lead,) + (slice(None),) * (len(full_ref.shape) - 3) + (pl.ds(pl.multiple_of(b * h, 8), h), slice(None))
    return full_ref.at[idx]


def _full_shape(shard_shape, kind):
    if kind[0] == "slot":
        return (N_CHIP,) + tuple(shard_shape)
    if kind[0] == "col":
        return tuple(shard_shape[:-1]) + (N_CHIP * shard_shape[-1],)
    return tuple(shard_shape[:-2]) + (N_CHIP * shard_shape[-2], shard_shape[-1])


def _slab_start(shape, kind, b):
    zeros = [0] * len(shape)
    if kind[0] == "slot":
        return [b] + zeros[1:]
    if kind[0] == "col":
        return zeros[:-1] + [b * kind[1]]
    return zeros[:-2] + [b * kind[1], 0]


def cast_and_place(shard, kind, chip, after, name):
    rows, cols = shard.shape[-2:]
    lead = int(np.prod(shard.shape[:-2]))
    if kind[0] == "col":
        full3, where = (lead, rows, N_CHIP * cols), lambda p, b: (p, 0, b[0])
    elif kind[0] == "row":
        full3, where = (lead, N_CHIP * rows, cols), lambda p, b: (p, b[0], 0)
    else:
        full3, where = (N_CHIP * lead, rows, cols), lambda p, b: (b[0] * lead + p, 0, 0)

    def body(b_ref, s_ref, after_ref, o_ref):
        o_ref[...] = s_ref[...].astype(BF16)

    full = pl.pallas_call(
        body, name=name,
        grid_spec=pltpu.PrefetchScalarGridSpec(
            num_scalar_prefetch=1, grid=(lead,),
            in_specs=[pl.BlockSpec((1, rows, cols), lambda p, b: (p, 0, 0)), ANY],
            out_specs=pl.BlockSpec((1, rows, cols), where)),
        out_shape=jax.ShapeDtypeStruct(full3, BF16),
        compiler_params=_params(1),
    )(jnp.reshape(chip, (1,)).astype(jnp.int32), shard.reshape(lead, rows, cols), after)
    return full.reshape(_full_shape(shard.shape, kind))


def tie(v, after, name):
    def body(v_ref, after_ref, out_ref):
        pass

    return pl.pallas_call(
        body, name=name, out_shape=jax.ShapeDtypeStruct(v.shape, v.dtype),
        in_specs=[ANY, ANY], out_specs=ANY, input_output_aliases={0: 0},
    )(v, after)


def own_slab(full, kind, b, shard_shape):
    sizes = (1,) + tuple(shard_shape) if kind[0] == "slot" else tuple(shard_shape)
    return lax.dynamic_slice(full, _slab_start(full.shape, kind, b), sizes).reshape(shard_shape)


def gather_weights(fulls, kinds):
    n = len(fulls)

    def body(*refs):
        outs = refs[n:2 * n]
        send_sems, recv_sems = refs[2 * n:]
        x, y, c, chips = _position()
        b_me = 2 * x + y
        sibling = (x, y, 1 - c)
        sends = []

        def halves(t):
            lead = outs[t].shape[1] if kinds[t][0] == "slot" else outs[t].shape[0]
            return pl.ds(c * (lead // 2), lead // 2), pl.ds((1 - c) * (lead // 2), lead // 2)

        for t in range(n):
            mine, _ = halves(t)
            own = _slab(outs[t], kinds[t], b_me, mine)
            for j, chip in enumerate(chips):
                cp = pltpu.make_async_remote_copy(
                    src_ref=own, dst_ref=own,
                    send_sem=send_sems.at[6 * t + j], recv_sem=recv_sems.at[6 * t + j],
                    device_id=(*chip, c), device_id_type=MESH)
                cp.start()
                sends.append(cp)
        for t in range(n):
            mine, _ = halves(t)
            for j, chip in enumerate(chips):
                landed = _slab(outs[t], kinds[t], 2 * chip[0] + chip[1], mine)
                pltpu.make_async_remote_copy(
                    src_ref=landed, dst_ref=landed, send_sem=send_sems.at[6 * t + j], recv_sem=recv_sems.at[6 * t + j],
                    device_id=(*chip, c), device_id_type=MESH).wait_recv()
                cp = pltpu.make_async_remote_copy(
                    src_ref=landed, dst_ref=landed, send_sem=send_sems.at[6 * t + 3 + j],
                    recv_sem=recv_sems.at[6 * t + 3 + j], device_id=sibling, device_id_type=MESH)
                cp.start()
                sends.append(cp)
        for t in range(n):
            _, theirs = halves(t)
            for j, chip in enumerate(chips):
                landed = _slab(outs[t], kinds[t], 2 * chip[0] + chip[1], theirs)
                pltpu.make_async_remote_copy(
                    src_ref=landed, dst_ref=landed, send_sem=send_sems.at[6 * t + 3 + j],
                    recv_sem=recv_sems.at[6 * t + 3 + j], device_id=sibling, device_id_type=MESH).wait_recv()
        for cp in sends:
            cp.wait_send()

    return pl.pallas_call(
        body, name="gather_weights",
        out_shape=[jax.ShapeDtypeStruct(v.shape, v.dtype) for v in fulls],
        in_specs=[ANY] * n, out_specs=[ANY] * n,
        input_output_aliases={t: t for t in range(n)},
        scratch_shapes=[pltpu.SemaphoreType.DMA((6 * n,)), pltpu.SemaphoreType.DMA((6 * n,))],
    )(*fulls)


def scatter_grads(grads, kinds, shard_shapes):
    n = len(grads)

    def body(*refs):
        ins, outs = refs[:n], refs[n:2 * n]
        send_sems, recv_sems = refs[2 * n:]
        x, y, c, chips = _position()
        sends = []
        for t in range(n):
            whole = _whole(ins[t], kinds[t])
            for j, chip in enumerate(chips):
                cp = pltpu.make_async_remote_copy(
                    src_ref=_slab(ins[t], kinds[t], 2 * chip[0] + chip[1], whole), dst_ref=outs[t].at[j],
                    send_sem=send_sems.at[3 * t + j], recv_sem=recv_sems.at[3 * t + j],
                    device_id=(*chip, c), device_id_type=MESH)
                cp.start()
                sends.append(cp)
        for t in range(n):
            for j, chip in enumerate(chips):
                pltpu.make_async_remote_copy(
                    src_ref=outs[t].at[j], dst_ref=outs[t].at[j], send_sem=send_sems.at[3 * t + j],
                    recv_sem=recv_sems.at[3 * t + j], device_id=(*chip, c), device_id_type=MESH).wait_recv()
        for cp in sends:
            cp.wait_send()

    return pl.pallas_call(
        body, name="scatter_grads",
        out_shape=[jax.ShapeDtypeStruct((N_CHIP - 1,) + tuple(s), g.dtype) for g, s in zip(grads, shard_shapes)],
        in_specs=[ANY] * n, out_specs=[ANY] * n,
        scratch_shapes=[pltpu.SemaphoreType.DMA((3 * n,)), pltpu.SemaphoreType.DMA((3 * n,))],
    )(*grads)


def _whole(ref, kind):
    return pl.ds(0, ref.shape[1] if kind[0] == "slot" else ref.shape[0])


def _exchange_copies(srcs, lands, kinds, gather, send_sems, recv_sems):
    x, y, c, chips = _position()
    b_me = 2 * x + y
    out = []
    for t in range(len(lands)):
        for j, chip in enumerate(chips):
            b_j = 2 * chip[0] + chip[1]
            if gather:
                src = sent_to = _slab(lands[t], kinds[t], b_me, _whole(lands[t], kinds[t]))
                arrives = _slab(lands[t], kinds[t], b_j, _whole(lands[t], kinds[t]))
            else:
                src = _slab(srcs[t], kinds[t], b_j, _whole(srcs[t], kinds[t]))
                sent_to = arrives = lands[t].at[j]
            k = 3 * t + j
            send = pltpu.make_async_remote_copy(src_ref=src, dst_ref=sent_to, send_sem=send_sems.at[k],
                                                recv_sem=recv_sems.at[k], device_id=(*chip, c), device_id_type=MESH)
            recv = pltpu.make_async_remote_copy(src_ref=src, dst_ref=arrives, send_sem=send_sems.at[k],
                                                recv_sem=recv_sems.at[k], device_id=(*chip, c), device_id_type=MESH)
            out.append((send, recv))
    return out


def exchange_start(srcs, lands, kinds, gather, name):
    ns, nl = len(srcs), len(lands)
    hbm = pl.BlockSpec(memory_space=pltpu.HBM)

    def body(*refs):
        ins, lnd = refs[:ns], refs[ns:ns + nl]
        send_sems, recv_sems = refs[ns + nl], refs[ns + nl + 1]
        token = refs[-1]
        for send, _ in _exchange_copies(ins, lnd, kinds, gather, send_sems, recv_sems):
            send.start()
        token[...] = jnp.zeros_like(token)

    ops = [pltpu.with_memory_space_constraint(v, pltpu.HBM) for v in (*srcs, *lands)]
    res = pl.pallas_call(
        body, name=name,
        out_shape=(pltpu.SemaphoreType.DMA((3 * nl,)), pltpu.SemaphoreType.DMA((3 * nl,)),
                   *[pltpu.HBM(v.shape, v.dtype) for v in ops], jax.ShapeDtypeStruct((8, LANES), F32)),
        in_specs=[hbm] * (ns + nl),
        out_specs=(pl.BlockSpec(memory_space=pltpu.SEMAPHORE), pl.BlockSpec(memory_space=pltpu.SEMAPHORE),
                   *[hbm] * (ns + nl), pl.BlockSpec(memory_space=pltpu.VMEM)),
        input_output_aliases={t: 2 + t for t in range(ns + nl)},
        compiler_params=pltpu.CompilerParams(has_side_effects=pltpu.SideEffectType.DATAFLOW_SIDE_EFFECTING),
    )(*ops)
    return res[0], res[1], list(res[2:2 + ns]), list(res[2 + ns:2 + ns + nl]), res[-1]


def exchange_wait(send_sems, recv_sems, srcs, lands, after, kinds, gather, name):
    ns, nl = len(srcs), len(lands)
    hbm = pl.BlockSpec(memory_space=pltpu.HBM)

    def body(*refs):
        ins, lnd = refs[:ns], refs[ns:ns + nl]
        ssem, rsem = refs[ns + nl], refs[ns + nl + 1]
        for send, recv in _exchange_copies(ins, lnd, kinds, gather, ssem, rsem):
            send.wait_send()
            recv.wait_recv()

    res = pl.pallas_call(
        body, name=name,
        out_shape=tuple(pltpu.HBM(v.shape, v.dtype) for v in (*srcs, *lands)),
        in_specs=[hbm] * (ns + nl) + [pl.BlockSpec(memory_space=pltpu.SEMAPHORE)] * 2 + [ANY],
        out_specs=tuple([hbm] * (ns + nl)),
        input_output_aliases={t: t for t in range(ns + nl)},
        compiler_params=pltpu.CompilerParams(has_side_effects=pltpu.SideEffectType.DATAFLOW_SIDE_EFFECTING),
    )(*srcs, *lands, send_sems, recv_sems, after)
    return list(res[:ns]), list(res[ns:])


def swap_with_sibling(parts):
    n = len(parts)

    def body(*refs):
        ins, outs = refs[:n], refs[n:2 * n]
        send_sems, recv_sems = refs[2 * n:]
        x, y, c, _ = _position()
        cps = []
        for t in range(n):
            cp = pltpu.make_async_remote_copy(
                src_ref=ins[t], dst_ref=outs[t], send_sem=send_sems.at[t], recv_sem=recv_sems.at[t],
                device_id=(x, y, 1 - c), device_id_type=MESH)
            cp.start()
            cps.append(cp)
        for cp in cps:
            cp.wait_recv()
        for cp in cps:
            cp.wait_send()

    return pl.pallas_call(
        body, name="swap_with_sibling",
        out_shape=[jax.ShapeDtypeStruct(p.shape, p.dtype) for p in parts],
        in_specs=[ANY] * n, out_specs=[ANY] * n,
        scratch_shapes=[pltpu.SemaphoreType.DMA((n,)), pltpu.SemaphoreType.DMA((n,))],
    )(*parts)


def _modulated(xv, mv_ref):
    g, shift, scale = mv_ref[0:1, :], mv_ref[1:2, :], mv_ref[2:3, :]
    r = lax.rsqrt(jnp.mean(xv * xv, axis=-1, keepdims=True) + EPS)
    xhat = xv * r
    xn = xhat * g
    return xn * (1.0 + scale) + shift, xhat, xn, r, g, scale


def _resident(block_shape, index):
    return pl.BlockSpec(block_shape, lambda i: index, pipeline_mode=pl.Buffered(1))


def modmm(x, mv, mv_idx, w, w_block, w_index, n_cols, chunk, out_dtype, name, want_h):
    s, d = x.shape
    tm = _row_tile(s)

    def body(x_ref, mv_ref, w_ref, out_ref, *rest):
        h = _modulated(x_ref[...], mv_ref)[0].astype(BF16)
        if want_h:
            rest[0][...] = h
        for n in range(n_cols // chunk):
            cols = slice(n * chunk, (n + 1) * chunk)
            out_ref[:, cols] = _nn(h, w_ref[:, cols]).astype(out_dtype)

    row = pl.BlockSpec((tm, d), lambda i: (i, 0))
    out_shapes = [jax.ShapeDtypeStruct((s, n_cols), out_dtype)]
    out_specs = [pl.BlockSpec((tm, n_cols), lambda i: (i, 0))]
    if want_h:
        out_shapes.append(jax.ShapeDtypeStruct((s, d), BF16))
        out_specs.append(row)
    res = pl.pallas_call(
        body, name=name, grid=(s // tm,),
        in_specs=[row, pl.BlockSpec((None, None, 8, d), lambda i: (*mv_idx, 0, 0)), _resident(w_block, w_index)],
        out_specs=out_specs, out_shape=out_shapes,
        compiler_params=_params(1),
    )(x, mv, w)
    return res if want_h else (res[0], None)


def resmm(x, mv, mv_idx, coef, lhs, w, w_block, w_index, kdim, chunk, name, ffn):
    s, d = x.shape
    tm = _row_tile(s)
    lhs_cols = 2 * kdim if ffn else kdim

    def body(x_ref, mv_ref, lhs_ref, w_ref, xo_ref, y_ref, *rest):
        y = jnp.zeros((tm, d), F32)
        for k in range(kdim // chunk):
            rows = slice(k * chunk, (k + 1) * chunk)
            if ffn:
                ag = lhs_ref[:, rows].astype(F32)
                au = lhs_ref[:, kdim + k * chunk:kdim + (k + 1) * chunk].astype(F32)
                left = (ag * _sigmoid_fast(ag) * au).astype(BF16)
                rest[0][:, rows] = left
            else:
                left = lhs_ref[:, rows]
            y = y + _nn(left, w_ref[rows, :])
        y_ref[...] = y.astype(BF16)
        xo_ref[...] = x_ref[...] + (coef * mv_ref[3:4, :]) * y

    row = pl.BlockSpec((tm, d), lambda i: (i, 0))
    out_shapes = [jax.ShapeDtypeStruct((s, d), F32), jax.ShapeDtypeStruct((s, d), BF16)]
    out_specs = [row, row]
    if ffn:
        out_shapes.append(jax.ShapeDtypeStruct((s, kdim), BF16))
        out_specs.append(pl.BlockSpec((tm, kdim), lambda i: (i, 0)))
    return pl.pallas_call(
        body, name=name, grid=(s // tm,),
        in_specs=[row, pl.BlockSpec((None, None, 8, d), lambda i: (*mv_idx, 0, 0)),
                  pl.BlockSpec((tm, lhs_cols), lambda i: (i, 0)), _resident(w_block, w_index)],
        out_specs=out_specs, out_shape=out_shapes,
        compiler_params=_params(1),
    )(x, mv, lhs, w)


def resmm_bwd(dxo, y, mv, mv_idx, coef, w, w_block, w_index, kdim, chunk, name, a=None):
    s, d = dxo.shape
    ffn = a is not None
    tm = _row_tile(s) // 2 if ffn else _row_tile(s)
    dl_cols = 2 * kdim if ffn else kdim

    def body(dxo_ref, y_ref, mv_ref, w_ref, *rest):
        if ffn:
            a_ref, dy_ref, dl_ref, dgate_ref = rest
        else:
            dy_ref, dl_ref, dgate_ref = rest

        @pl.when(pl.program_id(0) == 0)
        def _():
            dgate_ref[...] = jnp.zeros_like(dgate_ref)

        dxv = dxo_ref[...]
        dy = ((coef * mv_ref[3:4, :]) * dxv).astype(BF16)
        dy_ref[...] = dy
        dgate_ref[0:1, :] += jnp.sum(coef * dxv * y_ref[...].astype(F32), axis=0, keepdims=True)
        for k in range(kdim // chunk):
            rows = slice(k * chunk, (k + 1) * chunk)
            dl = _nt(dy, w_ref[rows, :])
            if ffn:
                ups = slice(kdim + k * chunk, kdim + (k + 1) * chunk)
                ag = a_ref[:, rows].astype(F32)
                au = a_ref[:, ups].astype(F32)
                sg = _sigmoid_fast(ag)
                dl_ref[:, rows] = (dl * au * (sg * (1.0 + ag * (1.0 - sg)))).astype(BF16)
                dl_ref[:, ups] = (dl * (ag * sg)).astype(BF16)
            else:
                dl_ref[:, rows] = dl.astype(BF16)

    row = pl.BlockSpec((tm, d), lambda i: (i, 0))
    wide = pl.BlockSpec((tm, dl_cols), lambda i: (i, 0))
    in_specs = [row, row, pl.BlockSpec((None, None, 8, d), lambda i: (*mv_idx, 0, 0)), _resident(w_block, w_index)]
    ops = [dxo, y, mv, w]
    if ffn:
        in_specs.append(wide)
        ops.append(a)
    return pl.pallas_call(
        body, name=name, grid=(s // tm,),
        in_specs=in_specs,
        out_specs=[row, wide, pl.BlockSpec((8, d), lambda i: (0, 0))],
        out_shape=[jax.ShapeDtypeStruct((s, d), BF16), jax.ShapeDtypeStruct((s, dl_cols), BF16),
                   jax.ShapeDtypeStruct((8, d), F32)],
        compiler_params=_params(1),
    )(*ops)


def modmm_bwd(dl, w, w_block, w_index, n_cols, chunk, x, dxo, mv, mv_idx, name, more=None):
    s, d = x.shape
    tm = _row_tile(s)

    def body(dl_ref, w_ref, x_ref, dxo_ref, mv_ref, *rest):
        if more is not None:
            dl2_ref, w2_ref, dx_ref, red_ref = rest
            dh = _nt(dl2_ref[...], w2_ref[...])
        else:
            dx_ref, red_ref = rest
            dh = jnp.zeros((tm, d), F32)

        @pl.when(pl.program_id(0) == 0)
        def _():
            red_ref[...] = jnp.zeros_like(red_ref)

        for n in range(n_cols // chunk):
            cols = slice(n * chunk, (n + 1) * chunk)
            dh = dh + _nt(dl_ref[:, cols], w_ref[:, cols])
        _, xhat, xn, r, g, scale = _modulated(x_ref[...], mv_ref)
        dxn = dh * (1.0 + scale)
        red_ref[0:1, :] += jnp.sum(dxn * xhat, axis=0, keepdims=True)
        red_ref[1:2, :] += jnp.sum(dh, axis=0, keepdims=True)
        red_ref[2:3, :] += jnp.sum(dh * xn, axis=0, keepdims=True)
        gd = dxn * g
        dx_ref[...] = dxo_ref[...] + r * (gd - xhat * jnp.mean(gd * xhat, axis=-1, keepdims=True))

    row = pl.BlockSpec((tm, d), lambda i: (i, 0))
    in_specs = [pl.BlockSpec((tm, n_cols), lambda i: (i, 0)), _resident(w_block, w_index), row, row,
                pl.BlockSpec((None, None, 8, d), lambda i: (*mv_idx, 0, 0))]
    ops = [dl, w, x, dxo, mv]
    if more is not None:
        in_specs += [pl.BlockSpec((tm, more[0].shape[1]), lambda i: (i, 0)), _resident(more[1], more[2])]
        ops += [more[0], w]
    return pl.pallas_call(
        body, name=name, grid=(s // tm,),
        in_specs=in_specs,
        out_specs=[row, pl.BlockSpec((8, d), lambda i: (0, 0))],
        out_shape=[jax.ShapeDtypeStruct((s, d), F32), jax.ShapeDtypeStruct((8, d), F32)],
        compiler_params=_params(1),
    )(*ops)


def weight_grad(a, a_spec, b, b_spec, grid_mn, s, bm, bn, dest, out_spec, name):
    tk = _wide_tile(s)
    k_tiles = s // tk

    def body(a_ref, b_ref, dest_ref, out_ref, acc):
        k = pl.program_id(2)

        @pl.when(k == 0)
        def _():
            acc[...] = jnp.zeros_like(acc)

        acc[...] += _tn(a_ref[...], b_ref[...])

        @pl.when(k == k_tiles - 1)
        def _():
            out_ref[...] = acc[...].astype(out_ref.dtype)

    return pl.pallas_call(
        body, name=name, grid=(*grid_mn, k_tiles),
        in_specs=[a_spec, b_spec, ANY], out_specs=out_spec,
        out_shape=jax.ShapeDtypeStruct(dest.shape, dest.dtype),
        input_output_aliases={2: 0},
        scratch_shapes=[pltpu.VMEM((bm, bn), F32)],
        compiler_params=_params(3),
    )(a, b, dest)


def _head_mean(v):
    lane = lax.broadcasted_iota(jnp.int32, v.shape, 1)
    lo = jnp.sum(jnp.where(lane < HEAD_DIM, v, 0.0), axis=-1, keepdims=True)
    hi = jnp.sum(v, axis=-1, keepdims=True) - lo
    return jnp.where(lane < HEAD_DIM, lo, hi) * (1.0 / HEAD_DIM)


def qknorm_fwd(proj, gains, width, name):
    s = proj.shape[0]
    nqk = gains.shape[1]
    tm = _row_tile(s)

    def body(p_ref, g_ref, o_ref):
        for cc in range(width // LANES):
            sl = slice(cc * LANES, (cc + 1) * LANES)
            xv = p_ref[:, sl].astype(F32)
            r = lax.rsqrt(_head_mean(xv * xv) + EPS)
            o_ref[:, sl] = (xv * r * g_ref[:, sl]).astype(BF16)

    blk = pl.BlockSpec((tm, width), lambda i, c: (i, c))
    return pl.pallas_call(
        body, name=name, grid=(s // tm, nqk // width),
        in_specs=[blk, pl.BlockSpec((1, width), lambda i, c: (0, c))],
        out_specs=blk, out_shape=jax.ShapeDtypeStruct((s, nqk), BF16),
        compiler_params=_params(2),
    )(proj, gains)


def qknorm_bwd(proj, gains, d, d_spec, n_cols, width, name):
    s = proj.shape[0]
    nqk = gains.shape[1] // width
    n_blocks = n_cols // width
    tm = _row_tile(s)

    def body(p_ref, g_ref, d_ref, o_ref, dg_ref):
        c, i = pl.program_id(0), pl.program_id(1)

        @pl.when(i == 0)
        def _():
            dg_ref[...] = jnp.zeros_like(dg_ref)

        @pl.when(c < nqk)
        def _():
            for cc in range(width // LANES):
                sl = slice(cc * LANES, (cc + 1) * LANES)
                xv = p_ref[:, sl].astype(F32)
                r = lax.rsqrt(_head_mean(xv * xv) + EPS)
                xhat = xv * r
                dv = d_ref[:, sl]
                gd = dv * g_ref[:, sl]
                o_ref[:, sl] = (r * (gd - xhat * _head_mean(gd * xhat))).astype(BF16)
                dg_ref[0:1, sl] += jnp.sum(dv * xhat, axis=0, keepdims=True)

        @pl.when(c >= nqk)
        def _():
            o_ref[...] = d_ref[...].astype(BF16)

    return pl.pallas_call(
        body, name=name, grid=(n_blocks, s // tm),
        in_specs=[pl.BlockSpec((tm, width), lambda c, i: (i, c)),
                  pl.BlockSpec((1, width), lambda c, i: (0, jnp.minimum(c, nqk - 1))), d_spec],
        out_specs=[pl.BlockSpec((tm, width), lambda c, i: (i, c)), pl.BlockSpec((8, width), lambda c, i: (0, c))],
        out_shape=[jax.ShapeDtypeStruct((s, n_cols), BF16), jax.ShapeDtypeStruct((8, n_cols), F32)],
        compiler_params=_params(2),
    )(proj, gains, d)


def _swa_mask(first, copies):
    qi = (lax.broadcasted_iota(jnp.int32, (copies * BLOCK, 2 * BLOCK), 0) & (BLOCK - 1)) + BLOCK
    kj = lax.broadcasted_iota(jnp.int32, (copies * BLOCK, 2 * BLOCK), 1)
    dist = qi - kj
    return (dist >= 0) & (dist < BLOCK) & ((kj >= BLOCK) | jnp.logical_not(first))


def swa_fwd(qkn, proj, bias, sink, d, name):
    s = qkn.shape[0]
    hq = d // HEAD_DIM
    hkv = hq // GROUP
    kw = hkv * HEAD_DIM
    nblk = s // BLOCK
    kcol = d // kw

    def body(q_ref, kc_ref, kp_ref, vc_ref, vp_ref, bias_ref, sink_ref, o_ref, lse_ref):
        mask = _swa_mask(pl.program_id(0) == 0, 1)
        lse_ref[...] = jnp.zeros_like(lse_ref)
        for kvh in range(hkv):
            cols = slice(kvh * HEAD_DIM, (kvh + 1) * HEAD_DIM)
            k2 = jnp.concatenate([kp_ref[:, cols], kc_ref[:, cols]], axis=0)
            v2 = jnp.concatenate([vp_ref[:, cols], vc_ref[:, cols]], axis=0)
            for g in range(GROUP):
                h = kvh * GROUP + g
                hc = slice(h * HEAD_DIM, (h + 1) * HEAD_DIM)
                sc = jnp.where(mask, _nt(q_ref[:, hc], k2) + bias_ref[h], NEG)
                sk = sink_ref[0, h]
                m = jnp.maximum(jnp.max(sc, axis=-1, keepdims=True), sk)
                p = jnp.exp(sc - m)
                denom = jnp.sum(p, axis=-1, keepdims=True) + jnp.exp(sk - m)
                o_ref[:, hc] = (_nn(p.astype(BF16), v2) / denom).astype(BF16)
                lse_ref[:, h:h + 1] = m + jnp.log(denom)

    prev = lambda i: jnp.maximum(i - 1, 0)
    return pl.pallas_call(
        body, name=name, grid=(nblk,),
        in_specs=[pl.BlockSpec((BLOCK, d), lambda i: (i, 0)),
                  pl.BlockSpec((BLOCK, kw), lambda i: (i, kcol)),
                  pl.BlockSpec((BLOCK, kw), lambda i: (prev(i), kcol)),
                  pl.BlockSpec((BLOCK, kw), lambda i: (i, kcol + 1)),
                  pl.BlockSpec((BLOCK, kw), lambda i: (prev(i), kcol + 1)),
                  pl.BlockSpec((hq, BLOCK, 2 * BLOCK), lambda i: (0, 0, 0)),
                  pl.BlockSpec(memory_space=pltpu.SMEM)],
        out_specs=[pl.BlockSpec((BLOCK, d), lambda i: (i, 0)), pl.BlockSpec((BLOCK, LANES), lambda i: (i, 0))],
        out_shape=[jax.ShapeDtypeStruct((s, d), BF16), jax.ShapeDtypeStruct((s, LANES), F32)],
        compiler_params=_params(1),
    )(qkn, qkn, qkn, proj, proj, bias, sink)


def swa_bwd(qkn, proj, bias, sink, do, o, lse, d, name):
    s = qkn.shape[0]
    hq = d // HEAD_DIM
    hkv = hq // GROUP
    kw = hkv * HEAD_DIM
    nblk = s // BLOCK
    kcol = d // kw
    wide = d + 2 * kw

    def body(q_ref, kc_ref, kp_ref, vc_ref, vp_ref, bias_ref, sink_ref, do_ref, o_ref, lse_ref,
             out_ref, dbias_ref, dsink_ref, carry, fresh):
        i = pl.program_id(0)

        @pl.when(i == 0)
        def _():
            dbias_ref[...] = jnp.zeros_like(dbias_ref)
            dsink_ref[...] = jnp.zeros_like(dsink_ref)
            carry[...] = jnp.zeros_like(carry)

        @pl.when(i == nblk)
        def _():
            fresh[...] = jnp.zeros_like(fresh)

        @pl.when(i < nblk)
        def _():
            mask = _swa_mask(i == 0, GROUP)
            for kvh in range(hkv):
                cols = slice(kvh * HEAD_DIM, (kvh + 1) * HEAD_DIM)
                heads = range(kvh * GROUP, (kvh + 1) * GROUP)
                stack = lambda ref: jnp.concatenate([ref[:, h * HEAD_DIM:(h + 1) * HEAD_DIM] for h in heads], axis=0)
                k2 = jnp.concatenate([kp_ref[:, cols], kc_ref[:, cols]], axis=0)
                v2 = jnp.concatenate([vp_ref[:, cols], vc_ref[:, cols]], axis=0)
                qs, dos, os_ = stack(q_ref), stack(do_ref), stack(o_ref)
                bias_s = bias_ref[kvh * GROUP:(kvh + 1) * GROUP].reshape(GROUP * BLOCK, 2 * BLOCK)
                sk = jnp.concatenate([jnp.full((BLOCK, 1), sink_ref[0, h], F32) for h in heads], axis=0)
                lse_s = jnp.concatenate([lse_ref[:, h:h + 1] for h in heads], axis=0)
                sc = jnp.where(mask, _nt(qs, k2) + bias_s, NEG)
                p = jnp.exp(sc - lse_s)
                delta = jnp.sum(dos.astype(F32) * os_.astype(F32), axis=-1, keepdims=True)
                ds = p * (_nt(dos, v2) - delta)
                dbias_ref[kvh * GROUP:(kvh + 1) * GROUP] += ds.reshape(GROUP, BLOCK, 2 * BLOCK)
                to_sink = -jnp.exp(sk - lse_s) * delta
                dsb = ds.astype(BF16)
                dqs = _nn(dsb, k2)
                for g, h in enumerate(heads):
                    rows = slice(g * BLOCK, (g + 1) * BLOCK)
                    fresh[0, :, h * HEAD_DIM:(h + 1) * HEAD_DIM] = dqs[rows]
                    dsink_ref[0:1, h:h + 1] += jnp.sum(to_sink[rows], axis=0, keepdims=True)
                dk2 = _tn(dsb, qs)
                dv2 = _tn(p.astype(BF16), dos)
                kc_cols = slice(d + kvh * HEAD_DIM, d + (kvh + 1) * HEAD_DIM)
                vc_cols = slice(d + kw + kvh * HEAD_DIM, d + kw + (kvh + 1) * HEAD_DIM)
                fresh[0, :, kc_cols] = dk2[BLOCK:]
                fresh[0, :, vc_cols] = dv2[BLOCK:]
                fresh[1, :, kc_cols] = dk2[:BLOCK]
                fresh[1, :, vc_cols] = dv2[:BLOCK]

        lane = lax.broadcasted_iota(jnp.int32, (BLOCK, wide), 1)
        out_ref[...] = carry[...] + jnp.where(lane >= d, fresh[1], 0.0)

        @pl.when(i < nblk)
        def _():
            carry[...] = fresh[0]

    cur = lambda i: jnp.minimum(i, nblk - 1)
    prev = lambda i: jnp.maximum(jnp.minimum(i, nblk - 1) - 1, 0)
    return pl.pallas_call(
        body, name=name, grid=(nblk + 1,),
        in_specs=[pl.BlockSpec((BLOCK, d), lambda i: (cur(i), 0)),
                  pl.BlockSpec((BLOCK, kw), lambda i: (cur(i), kcol)),
                  pl.BlockSpec((BLOCK, kw), lambda i: (prev(i), kcol)),
                  pl.BlockSpec((BLOCK, kw), lambda i: (cur(i), kcol + 1)),
                  pl.BlockSpec((BLOCK, kw), lambda i: (prev(i), kcol + 1)),
                  pl.BlockSpec((hq, BLOCK, 2 * BLOCK), lambda i: (0, 0, 0)),
                  pl.BlockSpec(memory_space=pltpu.SMEM),
                  pl.BlockSpec((BLOCK, d), lambda i: (cur(i), 0)),
                  pl.BlockSpec((BLOCK, d), lambda i: (cur(i), 0)),
                  pl.BlockSpec((BLOCK, LANES), lambda i: (cur(i), 0))],
        out_specs=[pl.BlockSpec((BLOCK, wide), lambda i: (jnp.maximum(i - 1, 0), 0)),
                   pl.BlockSpec((hq, BLOCK, 2 * BLOCK), lambda i: (0, 0, 0)),
                   pl.BlockSpec((8, LANES), lambda i: (0, 0))],
        out_shape=[jax.ShapeDtypeStruct((s, wide), F32), jax.ShapeDtypeStruct((hq, BLOCK, 2 * BLOCK), F32),
                   jax.ShapeDtypeStruct((8, LANES), F32)],
        scratch_shapes=[pltpu.VMEM((BLOCK, wide), F32), pltpu.VMEM((2, BLOCK, wide), F32)],
        compiler_params=_params(1),
    )(qkn, qkn, qkn, proj, proj, bias, sink, do, o, lse)


def _rel_bucket_table():
    qi = np.arange(BLOCK)[:, None] + BLOCK
    kj = np.arange(2 * BLOCK)[None, :]
    n = np.maximum(qi - kj, 0)
    max_exact = REL_BUCKETS // 2
    nf = np.maximum(n, 1).astype(np.float32)
    large = max_exact + (np.log(nf / max_exact) / math.log(REL_MAX_DIST / max_exact)
                         * (REL_BUCKETS - max_exact)).astype(np.int32)
    large = np.minimum(large, REL_BUCKETS - 1)
    return np.where(n < max_exact, n, large).astype(np.int32)


def rel_bias_table(rel_bias, bucket):
    hq = rel_bias.shape[1]

    def body(rb_ref, bucket_ref, out_ref):
        tbl = bucket_ref[...]

        def per_head(h, carry):
            def per_bucket(b, acc):
                return jnp.where(tbl == b, rb_ref[b, h], acc)

            out_ref[h] = lax.fori_loop(0, REL_BUCKETS, per_bucket, jnp.zeros(tbl.shape, F32))
            return carry

        lax.fori_loop(0, hq, per_head, 0)

    return pl.pallas_call(
        body, name="rel_bias_table",
        in_specs=[pl.BlockSpec(memory_space=pltpu.SMEM), pl.BlockSpec(memory_space=pltpu.VMEM)],
        out_specs=pl.BlockSpec(memory_space=pltpu.VMEM),
        out_shape=jax.ShapeDtypeStruct((hq,) + tuple(bucket.shape), F32),
    )(rel_bias, bucket)


def rel_bias_grad(dbias, bucket):
    n_layers, hq = dbias.shape[:2]

    def body(db_ref, bucket_ref, out_ref):
        tbl = bucket_ref[...]

        def per_head(h, carry):
            dsum = db_ref[0, h]
            for a in range(1, n_layers):
                dsum = dsum + db_ref[a, h]

            def per_bucket(b, carry2):
                out_ref[b, h] = jnp.sum(jnp.where(tbl == b, dsum, 0.0))
                return carry2

            return lax.fori_loop(0, REL_BUCKETS, per_bucket, carry)

        lax.fori_loop(0, hq, per_head, 0)

    return pl.pallas_call(
        body, name="rel_bias_grad",
        in_specs=[pl.BlockSpec(memory_space=pltpu.VMEM), pl.BlockSpec(memory_space=pltpu.VMEM)],
        out_specs=pl.BlockSpec(memory_space=pltpu.SMEM),
        out_shape=jax.ShapeDtypeStruct((REL_BUCKETS, hq), F32),
    )(dbias, bucket)


def _split3(v):
    hi = v.astype(BF16)
    r1 = v - hi.astype(F32)
    mid = r1.astype(BF16)
    lo = (r1 - mid.astype(F32)).astype(BF16)
    return hi, mid, lo


def _tri_sum(tri, v):
    hi, mid, lo = _split3(v)
    return _nn(tri, hi) + _nn(tri, mid) + _nn(tri, lo)


def fox_gates(fl, b_f, name):
    s = fl.shape[0]
    t = _row_tile(s)

    def body(fl_ref, b_ref, f_ref, carry):
        @pl.when(pl.program_id(0) == 0)
        def _():
            carry[...] = jnp.zeros_like(carry)

        z = fl_ref[...] + b_ref[...]
        logf = jnp.minimum(z, 0.0) - jnp.log(1.0 + jnp.exp(-jnp.abs(z)))
        r = lax.broadcasted_iota(jnp.int32, (t, t), 0)
        cidx = lax.broadcasted_iota(jnp.int32, (t, t), 1)
        tri = jnp.where(cidx <= r, 1.0, 0.0).astype(BF16)
        f = _tri_sum(tri, logf) + carry[0:1, :]
        f_ref[...] = f
        carry[0:1, :] = f_ref[t - 1:t, :]

    blk = pl.BlockSpec((t, LANES), lambda i: (i, 0))
    return pl.pallas_call(
        body, name=name, grid=(s // t,),
        in_specs=[blk, pl.BlockSpec((1, LANES), lambda i: (0, 0))],
        out_specs=blk, out_shape=jax.ShapeDtypeStruct((s, LANES), F32),
        scratch_shapes=[pltpu.VMEM((8, LANES), F32)],
        compiler_params=_params(1),
    )(fl, b_f)


def fox_gates_bwd(fl, b_f, df_query, df_key, name):
    s = fl.shape[0]
    t = _row_tile(s)
    nb = s // t

    def body(fl_ref, b_ref, dfq_ref, dfk_ref, dfl_ref, db_ref, carry):
        @pl.when(pl.program_id(0) == 0)
        def _():
            carry[...] = jnp.zeros_like(carry)
            db_ref[...] = jnp.zeros_like(db_ref)

        dfv = dfq_ref[...] + dfk_ref[...]
        r = lax.broadcasted_iota(jnp.int32, (t, t), 0)
        cidx = lax.broadcasted_iota(jnp.int32, (t, t), 1)
        tri = jnp.where(cidx >= r, 1.0, 0.0).astype(BF16)
        dlog = _tri_sum(tri, dfv) + carry[0:1, :]
        carry[0:1, :] += jnp.sum(dfv, axis=0, keepdims=True)
        z = fl_ref[...] + b_ref[...]
        dz = dlog * (1.0 - _sigmoid(z))
        dfl_ref[...] = dz.astype(BF16)
        db_ref[0:1, :] += jnp.sum(dz, axis=0, keepdims=True)

    rev = pl.BlockSpec((t, LANES), lambda i: (nb - 1 - i, 0))
    return pl.pallas_call(
        body, name=name, grid=(nb,),
        in_specs=[rev, pl.BlockSpec((1, LANES), lambda i: (0, 0)), rev, rev],
        out_specs=[rev, pl.BlockSpec((8, LANES), lambda i: (0, 0))],
        out_shape=[jax.ShapeDtypeStruct((s, LANES), BF16), jax.ShapeDtypeStruct((8, LANES), F32)],
        scratch_shapes=[pltpu.VMEM((8, LANES), F32)],
        compiler_params=_params(1),
    )(fl, b_f, df_query, df_key)


def fox_fwd(qkn, proj, f_col, f_row, d, name):
    s = qkn.shape[0]
    t = _attn_tile(s)
    hs = min(FOX_HEADS_FWD, d // HEAD_DIM)
    wide = hs * HEAD_DIM
    n_pairs = d // wide
    nt = s // t

    def body(q_ref, k_ref, v_ref, fq_ref, fk_ref, o_ref, o32_ref, lse_ref, m_scr, l_scr, acc):
        i, j = pl.program_id(1), pl.program_id(2)

        @pl.when(j == 0)
        def _():
            m_scr[...] = jnp.full_like(m_scr, NEG)
            l_scr[...] = jnp.zeros_like(l_scr)
            acc[...] = jnp.zeros_like(acc)

        @pl.when(j <= i)
        def _():
            krow = lax.broadcasted_iota(jnp.int32, (t, t), 0)
            qcol = lax.broadcasted_iota(jnp.int32, (t, t), 1)
            visible = (krow <= qcol) | (j < i)
            for hh in range(hs):
                hc = slice(hh * HEAD_DIM, (hh + 1) * HEAD_DIM)
                st = _nt(k_ref[:, hc], q_ref[:, hc]) + fq_ref[hh] - fk_ref[hh]
                st = jnp.where(visible, st, NEG)
                m_prev = m_scr[hh]
                m_new = jnp.maximum(m_prev, jnp.max(st, axis=0, keepdims=True))
                alpha = jnp.exp(m_prev - m_new)
                pt = jnp.exp(st - m_new)
                l_scr[hh] = alpha * l_scr[hh] + jnp.sum(pt, axis=0, keepdims=True)
                acc[hc, :] = alpha * acc[hc, :] + _tn(v_ref[:, hc], pt.astype(BF16))
                m_scr[hh] = m_new

        @pl.when(j == i)
        def _():
            l_full = jnp.concatenate([jnp.broadcast_to(l_scr[hh], (HEAD_DIM, t)) for hh in range(hs)], axis=0)
            ov = (acc[...] / l_full).T
            o_ref[...] = ov.astype(BF16)
            o32_ref[...] = ov
            lse_ref[...] = m_scr[...] + jnp.log(l_scr[...])

    kv = lambda j, i: jnp.minimum(j, i)
    return pl.pallas_call(
        body, name=name, grid=(n_pairs, nt, nt),
        in_specs=[pl.BlockSpec((t, wide), lambda p, i, j: (i, p)),
                  pl.BlockSpec((t, wide), lambda p, i, j: (kv(j, i), n_pairs + p)),
                  pl.BlockSpec((t, wide), lambda p, i, j: (kv(j, i), 2 * n_pairs + p)),
                  pl.BlockSpec((hs, 1, t), lambda p, i, j: (p, 0, i)),
                  pl.BlockSpec((hs, t, 1), lambda p, i, j: (p, kv(j, i), 0))],
        out_specs=[pl.BlockSpec((t, wide), lambda p, i, j: (i, p)),
                   pl.BlockSpec((t, wide), lambda p, i, j: (i, p)),
                   pl.BlockSpec((hs, 1, t), lambda p, i, j: (p, 0, i))],
        out_shape=[jax.ShapeDtypeStruct((s, d), BF16), jax.ShapeDtypeStruct((s, d), F32),
                   jax.ShapeDtypeStruct((hs * n_pairs, 1, s), F32)],
        scratch_shapes=[pltpu.VMEM((hs, 1, t), F32), pltpu.VMEM((hs, 1, t), F32), pltpu.VMEM((wide, t), F32)],
        compiler_params=_params(3),
    )(qkn, qkn, proj, f_row, f_col)


def fox_bwd(qkn, proj, f_col, f_row, lse_row, do, o, d, name):
    s = qkn.shape[0]
    t = _attn_tile(s)
    hs, wide = FOX_HEADS, FOX_HEADS * HEAD_DIM
    n_pairs = d // wide
    nt = s // t

    def body(q_ref, k_ref, v_ref, fk_ref, fq_ref, lse_ref, do_ref, o_ref, out_ref, df_ref, dfq_ref,
             dq_acc, dkv_acc, df_acc, dfq_acc):
        j, i = pl.program_id(1), pl.program_id(2)

        @pl.when((j == 0) & (i == 0))
        def _():
            dq_acc[...] = jnp.zeros_like(dq_acc)
            dfq_acc[...] = jnp.zeros_like(dfq_acc)

        @pl.when(i == 0)
        def _():
            dkv_acc[...] = jnp.zeros_like(dkv_acc)
            df_acc[...] = jnp.zeros_like(df_acc)

        @pl.when(i >= j)
        def _():
            krow = lax.broadcasted_iota(jnp.int32, (t, t), 0)
            qcol = lax.broadcasted_iota(jnp.int32, (t, t), 1)
            visible = (krow <= qcol) | (i > j)
            ones = jnp.ones((8, HEAD_DIM), BF16)
            for hh in range(hs):
                hc = slice(hh * HEAD_DIM, (hh + 1) * HEAD_DIM)
                q, k, v, dov = q_ref[:, hc], k_ref[:, hc], v_ref[:, hc], do_ref[:, hc]
                st = _nt(k, q) + fq_ref[hh] - fk_ref[hh]
                pt = jnp.exp(jnp.where(visible, st, NEG) - lse_ref[hh])
                hi, mid, lo = _split3(dov.astype(F32) * o_ref[:, hc])
                delta = jnp.max(_nt(ones, hi) + _nt(ones, mid) + _nt(ones, lo), axis=0, keepdims=True)
                dst = pt * (_nt(v, dov) - delta)
                dsb = dst.astype(BF16)
                dkv_acc[1, :, hc] += _nn(pt.astype(BF16), dov)
                dkv_acc[0, :, hc] += _nn(dsb, q)
                dq_acc[pl.ds(pl.multiple_of(i * t, t), t), hc] += _tn(dsb, k)
                df_acc[hh] -= jnp.sum(dst, axis=-1, keepdims=True)
                dfq_acc[i, hh] += jnp.sum(dst, axis=0, keepdims=True)

        @pl.when(i == nt - 1)
        def _():
            out_ref[0] = dq_acc[pl.ds(pl.multiple_of(j * t, t), t), :]
            out_ref[1] = dkv_acc[0]
            out_ref[2] = dkv_acc[1]
            df_ref[...] = df_acc[...]
            dfq_ref[...] = dfq_acc[j]

    qi = lambda j, i: jnp.maximum(i, j)
    return pl.pallas_call(
        body, name=name, grid=(n_pairs, nt, nt),
        in_specs=[pl.BlockSpec((t, wide), lambda p, j, i: (qi(j, i), p)),
                  pl.BlockSpec((t, wide), lambda p, j, i: (j, n_pairs + p)),
                  pl.BlockSpec((t, wide), lambda p, j, i: (j, 2 * n_pairs + p)),
                  pl.BlockSpec((hs, t, 1), lambda p, j, i: (p, j, 0)),
                  pl.BlockSpec((hs, 1, t), lambda p, j, i: (p, 0, qi(j, i))),
                  pl.BlockSpec((hs, 1, t), lambda p, j, i: (p, 0, qi(j, i))),
                  pl.BlockSpec((t, wide), lambda p, j, i: (qi(j, i), p)),
                  pl.BlockSpec((t, wide), lambda p, j, i: (qi(j, i), p))],
        out_specs=[pl.BlockSpec((3, t, wide), lambda p, j, i: (0, j, p)),
                   pl.BlockSpec((hs, t, 1), lambda p, j, i: (p, j, 0)),
                   pl.BlockSpec((hs, 1, t), lambda p, j, i: (p, 0, j))],
        out_shape=[jax.ShapeDtypeStruct((3, s, d), F32), jax.ShapeDtypeStruct((hs * n_pairs, s, 1), F32),
                   jax.ShapeDtypeStruct((hs * n_pairs, 1, s), F32)],
        scratch_shapes=[pltpu.VMEM((s, wide), F32), pltpu.VMEM((2, t, wide), F32), pltpu.VMEM((hs, t, 1), F32),
                        pltpu.VMEM((nt, hs, 1, t), F32)],
        compiler_params=_params(3),
    )(qkn, qkn, proj, f_col, f_row, lse_row, do, o)


def loss_head(y, target):
    s, d = y.shape
    tm = _row_tile(s)

    def body(y_ref, t_ref, dy_ref, loss_ref):
        @pl.when(pl.program_id(0) == 0)
        def _():
            loss_ref[...] = jnp.zeros_like(loss_ref)

        diff = y_ref[...] - t_ref[...]
        dy_ref[...] = diff * (1.0 / d)
        loss_ref[...] += 0.5 * jnp.sum(jnp.mean(diff * diff, axis=-1, keepdims=True), axis=0, keepdims=True)

    row = pl.BlockSpec((tm, d), lambda i: (i, 0))
    return pl.pallas_call(
        body, name="loss_head", grid=(s // tm,),
        in_specs=[row, row],
        out_specs=[row, pl.BlockSpec((8, LANES), lambda i: (0, 0))],
        out_shape=[jax.ShapeDtypeStruct((s, d), F32), jax.ShapeDtypeStruct((8, LANES), F32)],
        compiler_params=_params(1),
    )(y, target)


def ada_mod(c_all, w, b):
    n_layers, d, cols = w.shape

    def body(c_ref, w_ref, b_ref, o_ref):
        cv = c_ref[...]
        o_ref[...] = _nn(cv * _sigmoid(cv), w_ref[...]) + b_ref[...]

    return pl.pallas_call(
        body, name="ada_mod", grid=(n_layers,),
        in_specs=[pl.BlockSpec((N_DEV, d), lambda l: (0, 0)), pl.BlockSpec((None, d, cols), lambda l: (l, 0, 0)),
                  pl.BlockSpec((None, 1, cols), lambda l: (l, 0, 0))],
        out_specs=pl.BlockSpec((None, N_DEV, cols), lambda l: (l, 0, 0)),
        out_shape=jax.ShapeDtypeStruct((n_layers, N_DEV, cols), F32),
        compiler_params=_params(1),
    )(c_all, w, b)


def ada_grad(c_t, dmod):
    d = c_t.shape[0]
    n_layers, _, cols = dmod.shape
    tn = cols // 2

    def body(c_ref, dm_ref, o_ref):
        cv = c_ref[...]
        o_ref[...] = _nn(cv * _sigmoid(cv), dm_ref[...])

    return pl.pallas_call(
        body, name="ada_grad", grid=(n_layers, 2),
        in_specs=[pl.BlockSpec((d, N_DEV), lambda l, n: (0, 0)), pl.BlockSpec((None, N_DEV, tn), lambda l, n: (l, 0, n))],
        out_specs=pl.BlockSpec((None, d, tn), lambda l, n: (l, 0, n)),
        out_shape=jax.ShapeDtypeStruct((n_layers, d, cols), F32),
        compiler_params=_params(2),
    )(c_t, dmod)


def sum_devices(v):
    def body(v_ref, o_ref):
        acc = v_ref[0]
        for k in range(1, N_DEV):
            acc = acc + v_ref[k]
        o_ref[...] = acc

    return pl.pallas_call(body, name="sum_devices", out_shape=jax.ShapeDtypeStruct(v.shape[1:], F32))(v)


def sum_slots(r, own):
    _, rows, cols = r.shape
    tm = 256 if rows % 256 == 0 else rows

    def body(r_ref, own_ref, o_ref):
        o_ref[...] = ((own_ref[...].astype(F32) + r_ref[0].astype(F32)) + r_ref[1].astype(F32)) + r_ref[2].astype(F32)

    return pl.pallas_call(
        body, name="sum_slots", grid=(rows // tm,),
        in_specs=[pl.BlockSpec((N_CHIP - 1, tm, cols), lambda i: (0, i, 0)), pl.BlockSpec((tm, cols), lambda i: (i, 0))],
        out_specs=pl.BlockSpec((tm, cols), lambda i: (i, 0)),
        out_shape=jax.ShapeDtypeStruct((rows, cols), F32),
        compiler_params=_params(1),
    )(r, own)


def adamw(w, m, v, g, g2=None):
    rows, cols = w.shape
    tm = 256 if rows % 256 == 0 else rows
    two = g2 is not None
    c1 = 1.0 - ADAM_B1 ** ADAM_STEP
    c2 = 1.0 - ADAM_B2 ** ADAM_STEP

    def body(w_ref, m_ref, v_ref, g_ref, *rest):
        if two:
            g2_ref, go_ref, d_ref, mo_ref, vo_ref = rest
            gv = g_ref[...] + g2_ref[...]
        else:
            go_ref, d_ref, mo_ref, vo_ref = rest
            gv = g_ref[...]
        mn = ADAM_B1 * m_ref[...] + (1.0 - ADAM_B1) * gv
        vn = ADAM_B2 * v_ref[...] + (1.0 - ADAM_B2) * (gv * gv)
        go_ref[...] = gv
        mo_ref[...] = mn
        vo_ref[...] = vn
        d_ref[...] = -ADAM_LR * ((mn / c1) / (jnp.sqrt(vn / c2) + ADAM_EPS) + ADAM_WD * w_ref[...])

    blk = pl.BlockSpec((tm, cols), lambda i: (i, 0))
    ops = [w, m, v, g] + ([g2] if two else [])
    return pl.pallas_call(
        body, name="adamw", grid=(rows // tm,),
        in_specs=[blk] * len(ops), out_specs=[blk] * 4,
        out_shape=[jax.ShapeDtypeStruct((rows, cols), F32)] * 4,
        compiler_params=_params(1),
    )(*ops)


def _pad_rows(flat):
    n = flat.shape[0]
    rows = -(-n // LANES)
    return jnp.pad(flat, (0, rows * LANES - n)).reshape(rows, LANES)


def _pad_rows8(flat):
    rows = _pad_rows(flat)
    return jnp.pad(rows, ((0, -rows.shape[0] % 8), (0, 0)))


def _col_tiles(n):
    return next(k for k in range(1, n // LANES + 1) if n % (k * LANES) == 0 and n // k <= 1536)


def kernel(x, c, ada_w, ada_b, norm_g, ffn_w13, ffn_w2, rel_bias, swa_w_in, swa_w_out, swa_q_g, swa_k_g, swa_sink, fox_w_in, fox_w_out, fox_b_f, fox_q_g, fox_k_g, loss_target, m_ada_w, m_ada_b, m_norm_g, m_ffn_w13, m_ffn_w2, m_rel_bias, m_swa_w_in, m_swa_w_out, m_swa_q_g, m_swa_k_g, m_swa_sink, m_fox_w_in, m_fox_w_out, m_fox_b_f, m_fox_q_g, m_fox_k_g, v_ada_w, v_ada_b, v_norm_g, v_ffn_w13, v_ffn_w2, v_rel_bias, v_swa_w_in, v_swa_w_out, v_swa_q_g, v_swa_k_g, v_swa_sink, v_fox_w_in, v_fox_w_out, v_fox_b_f, v_fox_q_g, v_fox_k_g):
    ix, iy, ic = lax.axis_index("x"), lax.axis_index("y"), lax.axis_index("c")
    chip = 2 * ix + iy
    dev = 2 * chip + ic
    s, d = x.shape[1:]
    n_layers = ada_w.shape[0]
    n_a, n_b = swa_w_in.shape[0], fox_w_in.shape[0]
    hq = d // HEAD_DIM
    hkv = hq // GROUP
    kw = hkv * HEAD_DIM
    c13 = ffn_w13.shape[-1]
    f = 2 * c13
    r2 = ffn_w2.shape[2]
    cq = norm_g.shape[-1]
    a_in = d + 2 * kw
    fx = fox_w_in.shape[-1]
    b_in = N_CHIP * fx
    b_pad = 3 * d + LANES
    x0 = x[0]

    hello = _pad_rows8(jnp.concatenate([c.reshape(-1), norm_g.reshape(-1)]))
    hello_all = all_gather_rows(hello, "gather_c_norm").reshape(N_DEV, -1)
    c_all = hello_all[:, :d]
    ng = hello_all[::2, d:d + n_layers * 3 * cq].reshape(N_CHIP, n_layers, 3, cq)
    norm_full = jnp.moveaxis(ng, 0, 2).reshape(n_layers, 3, d)

    half_cols = ada_w.shape[-1] // 2
    w_half = lax.dynamic_slice_in_dim(ada_w, ic * half_cols, half_cols, axis=2)
    b_half = lax.dynamic_slice_in_dim(ada_b, dev * half_cols, half_cols, axis=1)[:, None, :]
    mod_part = ada_mod(c_all, w_half, b_half)
    mod_all = all_gather_rows(mod_part.reshape(n_layers * N_DEV, half_cols), "gather_mod")
    mod_all = mod_all.reshape(N_DEV, n_layers, N_DEV, half_cols)
    mod_mine = lax.dynamic_index_in_dim(mod_all, dev, axis=2, keepdims=False)
    mod_mine = jnp.moveaxis(mod_mine, 0, 1).reshape(n_layers, 3, 3, d)
    mv = jnp.concatenate([norm_full[:, :, None, :], mod_mine, jnp.zeros((n_layers, 3, 4, d), F32)], axis=2)

    def kinds_of(l):
        mixer_in = ("col", swa_w_in.shape[-1]) if l % 2 == 0 else ("slot",)
        return [("col", c13), ("row", r2), mixer_in, ("slot",)]

    def layer_buffers(l, after):
        raw = [ffn_w13[l], ffn_w2[l]] + ([swa_w_in[l // 2], swa_w_out[l // 2]] if l % 2 == 0 else
                                         [fox_w_in[l // 2], fox_w_out[l // 2]])
        return [cast_and_place(r, k, chip, after, f"place_weights_{l}_{t}") for t, (r, k) in enumerate(zip(raw, kinds_of(l)))]

    def layer_weights(l, fulls):
        w13_l, w2_l, w_in, w_out = fulls
        if l % 2 == 1:
            w_in = jnp.pad(jnp.concatenate([w_in[b] for b in range(N_CHIP)], axis=-1), ((0, 0), (0, b_pad - b_in)))
        return dict(w13=w13_l, w2=w2_l, w_in=w_in, w_out=w_out.reshape(d, d))

    weights = [layer_weights(0, gather_weights(layer_buffers(0, mv), kinds_of(0)))]
    in_flight = []
    for l in range(1, n_layers):
        in_flight.append(exchange_start([], layer_buffers(l, weights[0]["w13"]), kinds_of(l), True, f"gather_start_{l}"))
    for *_, token in in_flight:
        mv = mv + token[0, 0]

    bucket = jnp.asarray(_rel_bucket_table())
    bias = rel_bias_table(rel_bias, bucket)
    tm, tw = _row_tile(s), _wide_tile(s)
    n13 = 2 * f // c13
    na_t, nb_t = _col_tiles(a_in), _col_tiles(3 * d)
    wa_t, wb_t = a_in // na_t, 3 * d // nb_t
    gate_blk = 3 * d // LANES

    def ffn_forward(xv, l, half, sub):
        wg = weights[l]
        a, h = modmm(xv, mv, (l, sub), wg["w13"], (None, d, 2 * f), (half, 0, 0), 2 * f, c13, BF16,
                     f"ffn_up_{l}_{half}", True)
        xo, y, u = resmm(xv, mv, (l, sub), 0.5, a, wg["w2"], (None, f, d), (half, 0, 0), f, c13,
                         f"ffn_down_{l}_{half}", True)
        return xo, dict(x=xv, h=h, a=a, u=u, y=y)

    saved = []
    xv = x0
    for l in range(n_layers):
        j = l // 2
        if l >= 1:
            send_sems, recv_sems, shards, fulls, _ = in_flight[l - 1]
            weights.append(layer_weights(l, exchange_wait(send_sems, recv_sems, shards, fulls, xv, kinds_of(l), True,
                                                          f"gather_wait_{l}")[1]))
        wg = weights[l]
        xv, s0 = ffn_forward(xv, l, 0, 0)
        if l % 2 == 0:
            proj, h = modmm(xv, mv, (l, 1), wg["w_in"], (d, a_in), (0, 0), a_in, wa_t, BF16, f"swa_in_{j}", True)
            gains = jnp.concatenate([jnp.tile(swa_q_g[j] * HEAD_DIM ** -0.5, hq), jnp.tile(swa_k_g[j], hkv)])[None, :]
            qkn = qknorm_fwd(proj, gains, kw, f"swa_qknorm_{j}")
            sink = swa_sink[j][None, :]
            o, lse = swa_fwd(qkn, proj, bias, sink, d, f"swa_attn_{j}")
            s1 = dict(x=xv, h=h, proj=proj, gains=gains, qkn=qkn, sink=sink, o=o, lse=lse)
        else:
            proj, h = modmm(xv, mv, (l, 1), wg["w_in"], (d, 3 * d), (0, 0), 3 * d, wb_t, BF16, f"fox_in_{j}", True)
            fl, _ = modmm(xv, mv, (l, 1), wg["w_in"], (d, LANES), (0, gate_blk), LANES, LANES, F32,
                          f"fox_gate_in_{j}", False)
            b_f = jnp.pad(fox_b_f[j], (0, LANES - hq))[None, :]
            fcum = fox_gates(fl, b_f, f"fox_gates_{j}")
            f_t = fcum[:, :hq].T
            f_col, f_row = f_t[:, :, None], f_t[:, None, :]
            gains = jnp.concatenate([jnp.tile(fox_q_g[j] * HEAD_DIM ** -0.5, hq), jnp.tile(fox_k_g[j], hq)])[None, :]
            qkn = qknorm_fwd(proj, gains, d, f"fox_qknorm_{j}")
            o, o32, lse = fox_fwd(qkn, proj, f_col, f_row, d, f"fox_attn_{j}")
            s1 = dict(x=xv, h=h, proj=proj, gains=gains, qkn=qkn, fl=fl, b_f=b_f, f_col=f_col, f_row=f_row, o=o, o32=o32,
                      lse=lse)
        xv, y = resmm(xv, mv, (l, 1), 1.0, o, wg["w_out"], (d, d), (0, 0), d, d, f"mixer_out_{l}", False)
        s1["y"] = y
        xv, s2 = ffn_forward(xv, l, 1, 2)
        saved.append((s0, s1, s2))

    dxv, loss_part = loss_head(xv, loss_target[0])
    loss = lax.psum(loss_part[0, 0], ("x", "y", "c"))

    grads = [dict(w13=lax.empty((2, d, 2 * f), BF16), w2=lax.empty((2, f, d), BF16),
                  w_in=lax.empty((d, a_in if l % 2 == 0 else b_pad), BF16), w_out=lax.empty((d, d), BF16))
             for l in range(n_layers)]
    dmod = [[None] * 3 for _ in range(n_layers)]
    dnorm = [[None] * 3 for _ in range(n_layers)]
    dqk_gain = {}
    dsink, db_f, dbias_tabs = {}, {}, []

    def ffn_backward(dxo, sv, l, half, sub):
        wg, gg = weights[l], grads[l]
        dy, da, dgate = resmm_bwd(dxo, sv["y"], mv, (l, sub), 0.5, wg["w2"], (None, f, d), (half, 0, 0), f, c13,
                                  f"ffn_down_bwd_{l}_{half}", a=sv["a"])
        gg["w2"] = weight_grad(sv["u"], pl.BlockSpec((tw, c13), lambda m, n, k: (k, m)), dy,
                               pl.BlockSpec((tw, d), lambda m, n, k: (k, 0)), (f // c13, 1), s, c13, d, gg["w2"],
                               pl.BlockSpec((None, c13, d), lambda m, n, k: (half, m, 0)), f"ffn_w2_grad_{l}_{half}")
        gg["w13"] = weight_grad(sv["h"], pl.BlockSpec((tw, d), lambda m, n, k: (k, 0)), da,
                                pl.BlockSpec((tw, c13), lambda m, n, k: (k, n)), (1, n13), s, d, c13,
                                gg["w13"], pl.BlockSpec((None, d, c13), lambda m, n, k: (half, 0, n)),
                                f"ffn_w13_grad_{l}_{half}")
        dx, red = modmm_bwd(da, wg["w13"], (None, d, 2 * f), (half, 0, 0), 2 * f, c13, sv["x"], dxo, mv,
                            (l, sub), f"ffn_up_bwd_{l}_{half}")
        dmod[l][sub] = (red[1], red[2], dgate[0])
        dnorm[l][sub] = red[0]
        return dx

    def layer_slabs(l):
        gg = grads[l]
        g_in = gg["w_in"] if l % 2 == 0 else jnp.stack([gg["w_in"][:, b * fx:(b + 1) * fx] for b in range(N_CHIP)])
        return [gg["w13"], gg["w2"], g_in, gg["w_out"].reshape(N_CHIP, d // N_CHIP, d)]

    def shard_shapes_of(l):
        mixer = (swa_w_in, swa_w_out) if l % 2 == 0 else (fox_w_in, fox_w_out)
        return [ffn_w13.shape[1:], ffn_w2.shape[1:], mixer[0].shape[1:], mixer[1].shape[1:]]

    scatter_in_flight = {}

    for l in reversed(range(n_layers)):
        j = l // 2
        wg, gg = weights[l], grads[l]
        s0, s1, s2 = saved[l]
        dxv = ffn_backward(dxv, s2, l, 1, 2)
        is_a = l % 2 == 0
        w_out = wg["w_out"]
        dy, do, dgate = resmm_bwd(dxv, s1["y"], mv, (l, 1), 1.0, w_out, (d, d), (0, 0), d, d, f"mixer_out_bwd_{l}")
        gg["w_out"] = weight_grad(s1["o"], pl.BlockSpec((tw, d), lambda m, n, k: (k, 0)), dy,
                                  pl.BlockSpec((tw, d), lambda m, n, k: (k, 0)), (1, 1), s, d, d, gg["w_out"],
                                  pl.BlockSpec((d, d), lambda m, n, k: (0, 0)), f"mixer_out_grad_{l}")
        if is_a:
            d_qkv, dbias_tab, dsk = swa_bwd(s1["qkn"], s1["proj"], bias, s1["sink"], do, s1["o"], s1["lse"], d,
                                            f"swa_attn_bwd_{j}")
            dbias_tabs.append(dbias_tab)
            dsink[j] = dsk[0, :hq]
            dproj, dgain = qknorm_bwd(s1["proj"], s1["gains"], d_qkv, pl.BlockSpec((tm, kw), lambda c_, i: (i, c_)),
                                      a_in, kw, f"swa_qknorm_bwd_{j}")
            gg["w_in"] = weight_grad(s1["h"], pl.BlockSpec((tw, d), lambda m, n, k: (k, 0)), dproj,
                                     pl.BlockSpec((tw, wa_t), lambda m, n, k: (k, n)), (1, na_t), s, d, wa_t, gg["w_in"],
                                     pl.BlockSpec((d, wa_t), lambda m, n, k: (0, n)), f"swa_in_grad_{j}")
            dxv, red = modmm_bwd(dproj, wg["w_in"], (d, a_in), (0, 0), a_in, wa_t, s1["x"], dxv, mv, (l, 1),
                                 f"swa_in_bwd_{j}")
            dqk_gain[("a", j)] = (dgain[0, :d].reshape(hq, HEAD_DIM).sum(0) * HEAD_DIM ** -0.5,
                                  dgain[0, d:d + kw].reshape(hkv, HEAD_DIM).sum(0))
        else:
            lse_row = s1["lse"].reshape(hq, 1, s)
            d_qkv, df_col, dfq_row = fox_bwd(s1["qkn"], s1["proj"], s1["f_col"], s1["f_row"], lse_row, do, s1["o32"], d,
                                             f"fox_attn_bwd_{j}")
            lanes_of_heads = lambda a: jnp.pad(a.T, ((0, 0), (0, LANES - hq)))
            dfl, dbf = fox_gates_bwd(s1["fl"], s1["b_f"], lanes_of_heads(dfq_row[:, 0, :]), lanes_of_heads(df_col[:, :, 0]),
                                     f"fox_gates_bwd_{j}")
            db_f[j] = dbf[0, :hq]
            dproj, dgain = qknorm_bwd(s1["proj"], s1["gains"], d_qkv, pl.BlockSpec((None, tm, d), lambda c_, i: (c_, i, 0)),
                                      3 * d, d, f"fox_qknorm_bwd_{j}")
            gg["w_in"] = weight_grad(s1["h"], pl.BlockSpec((tw, d), lambda m, n, k: (k, 0)), dproj,
                                     pl.BlockSpec((tw, wb_t), lambda m, n, k: (k, n)), (1, nb_t), s, d, wb_t, gg["w_in"],
                                     pl.BlockSpec((d, wb_t), lambda m, n, k: (0, n)), f"fox_in_grad_{j}")
            gg["w_in"] = weight_grad(s1["h"], pl.BlockSpec((tw, d), lambda m, n, k: (k, 0)), dfl,
                                     pl.BlockSpec((tw, LANES), lambda m, n, k: (k, 0)), (1, 1), s, d, LANES, gg["w_in"],
                                     pl.BlockSpec((d, LANES), lambda m, n, k: (0, gate_blk)), f"fox_gate_in_grad_{j}")
            dxv, red = modmm_bwd(dproj, wg["w_in"], (d, 3 * d), (0, 0), 3 * d, wb_t, s1["x"], dxv, mv, (l, 1),
                                 f"fox_in_bwd_{j}", more=(dfl, (d, LANES), (0, gate_blk)))
            dqk_gain[("b", j)] = (dgain[0, :d].reshape(hq, HEAD_DIM).sum(0) * HEAD_DIM ** -0.5,
                                  dgain[0, d:2 * d].reshape(hq, HEAD_DIM).sum(0))
        dmod[l][1] = (red[1], red[2], dgate[0])
        dnorm[l][1] = red[0]
        dxv = ffn_backward(dxv, s0, l, 0, 0)
        if l >= 1:
            lands = [lax.empty((N_CHIP - 1,) + tuple(shp), BF16) for shp in shard_shapes_of(l)]
            scatter_in_flight[l] = exchange_start(layer_slabs(l), lands, kinds_of(l), False, f"scatter_start_{l}")
            mv = mv + scatter_in_flight[l][-1][0, 0]
    grad_x = dxv[None]

    drel = rel_bias_grad(jnp.stack(dbias_tabs), bucket)
    dmod_flat = jnp.stack([jnp.stack([jnp.stack(dmod[l][sub]) for sub in range(3)]) for l in range(n_layers)]).reshape(-1)
    dnorm_flat = jnp.stack([jnp.stack(dnorm[l]) for l in range(n_layers)]).reshape(-1)
    pieces = [dmod_flat, dnorm_flat,
              jnp.stack([dqk_gain[("a", j)][0] for j in range(n_a)]).reshape(-1),
              jnp.stack([dqk_gain[("a", j)][1] for j in range(n_a)]).reshape(-1),
              jnp.stack([dqk_gain[("b", j)][0] for j in range(n_b)]).reshape(-1),
              jnp.stack([dqk_gain[("b", j)][1] for j in range(n_b)]).reshape(-1),
              jnp.stack([dsink[j] for j in range(n_a)]).reshape(-1),
              jnp.stack([db_f[j] for j in range(n_b)]).reshape(-1),
              drel.reshape(-1)]
    rows = [_pad_rows(p) for p in pieces]
    starts = np.cumsum([0] + [r.shape[0] for r in rows])
    total = -(-int(starts[-1]) // 8) * 8
    small = jnp.pad(jnp.concatenate(rows), ((0, total - int(starts[-1])), (0, 0)))
    small_all = all_gather_rows(small, "gather_small_grads").reshape(N_DEV, total, LANES)
    small_sum = sum_devices(small_all)

    def piece(k, shape):
        n = int(np.prod(shape))
        return small_sum[int(starts[k]):int(starts[k + 1])].reshape(-1)[:n].reshape(shape)

    g_ada_b = piece(0, (n_layers, 9 * d))
    g_norm = lax.dynamic_slice_in_dim(piece(1, (n_layers, 3, d)), chip * cq, cq, axis=2)
    g_swa_q, g_swa_k = piece(2, (n_a, HEAD_DIM)), piece(3, (n_a, HEAD_DIM))
    g_fox_q, g_fox_k = piece(4, (n_b, HEAD_DIM)), piece(5, (n_b, HEAD_DIM))
    g_sink, g_bf, g_rel = piece(6, (n_a, hq)), piece(7, (n_b, hq)), piece(8, (REL_BUCKETS, hq))

    dmod_all = small_all[:, :int(starts[1])].reshape(N_DEV, -1)[:, :n_layers * 9 * d].reshape(N_DEV, n_layers, 9 * d)
    ada_cols = ada_w.shape[-1]
    dmod_mine = lax.dynamic_slice_in_dim(jnp.moveaxis(dmod_all, 0, 1), chip * ada_cols, ada_cols, axis=2)
    g_ada_w = ada_grad(c_all.T, dmod_mine)

    def update(w, m, v, g, g2=None):
        w2d = w.reshape(-1, w.shape[-1])
        outs = adamw(w2d, m.reshape(w2d.shape), v.reshape(w2d.shape), g.reshape(w2d.shape) if g2 is None else g, g2)
        return [t.reshape(w.shape) for t in outs]

    slabs0 = layer_slabs(0)
    slabs0[0] = tie(slabs0[0], small_all, "scatter_behind_small_gather")
    lands0 = [lax.empty((N_CHIP - 1,) + tuple(shp), BF16) for shp in shard_shapes_of(0)]
    scatter_in_flight[0] = exchange_start(slabs0, lands0, kinds_of(0), False, "scatter_start_0")

    r_ada_w = update(ada_w, m_ada_w, v_ada_w, g_ada_w)
    r_ada_b = update(ada_b, m_ada_b, v_ada_b, g_ada_b)
    r_norm = update(norm_g, m_norm_g, v_norm_g, g_norm)
    r_rel = update(rel_bias, m_rel_bias, v_rel_bias, g_rel)
    r_swa_q = update(swa_q_g, m_swa_q_g, v_swa_q_g, g_swa_q)
    r_swa_k = update(swa_k_g, m_swa_k_g, v_swa_k_g, g_swa_k)
    r_sink = update(swa_sink, m_swa_sink, v_swa_sink, g_sink)
    r_bf = update(fox_b_f, m_fox_b_f, v_fox_b_f, g_bf)
    r_fox_q = update(fox_q_g, m_fox_q_g, v_fox_q_g, g_fox_q)
    r_fox_k = update(fox_k_g, m_fox_k_g, v_fox_k_g, g_fox_k)

    sources, landed, sums = {}, {}, {}

    def collect(l, after):
        send_sems, recv_sems, slabs, lands, _ = scatter_in_flight[l]
        sources[l], landed[l] = exchange_wait(send_sems, recv_sems, slabs, lands, after, kinds_of(l), False,
                                              f"scatter_wait_{l}")
        for t, shape in enumerate(shard_shapes_of(l)):
            own = own_slab(sources[l][t], kinds_of(l)[t], chip, shape)
            sums[l, t] = sum_slots(landed[l][t].reshape(N_CHIP - 1, -1, shape[-1]), own.reshape(-1, shape[-1]))

    for l in reversed(range(1, n_layers)):
        collect(l, r_ada_w[0])
    collect(0, sums[1, 0])
    every, even, odd = range(n_layers), range(0, n_layers, 2), range(1, n_layers, 2)
    parts = [jnp.concatenate([sums[l, t] for l in layers])
             for layers, t in [(every, 0), (every, 1), (even, 2), (even, 3), (odd, 2), (odd, 3)]]
    others = swap_with_sibling(parts)
    big = [(ffn_w13, m_ffn_w13, v_ffn_w13), (ffn_w2, m_ffn_w2, v_ffn_w2), (swa_w_in, m_swa_w_in, v_swa_w_in),
           (swa_w_out, m_swa_w_out, v_swa_w_out), (fox_w_in, m_fox_w_in, v_fox_w_in), (fox_w_out, m_fox_w_out, v_fox_w_out)]
    big_out = [update(w, m, v, p, q) for (w, m, v), p, q in zip(big, parts, others)]
    per_weight = [r_ada_w, r_ada_b, r_norm, big_out[0], big_out[1], r_rel, big_out[2], big_out[3], r_swa_q, r_swa_k,
                  r_sink, big_out[4], big_out[5], r_bf, r_fox_q, r_fox_k]
    return (loss, grad_x, *[r[0] for r in per_weight], *[r[1] for r in per_weight],
            *[r[2] for r in per_weight], *[r[3] for r in per_weight])
```

```python
import math

import numpy as np
import jax
import jax.numpy as jnp
from jax import lax
from jax.experimental import pallas as pl
from jax.experimental.pallas import tpu as pltpu

F32 = jnp.float32
BF16 = jnp.bfloat16
HEAD_DIM = 64
GROUP = 4
FOX_HEADS = 8
FOX_HEADS_FWD = 16
BLOCK = 128
REL_BUCKETS = 32
REL_MAX_DIST = 128
EPS = 1e-6
NEG = -1e30
N_CHIP = 4
N_DEV = 8
LANES = 128
VMEM_LIMIT = 52 * 1024 * 1024
ADAM_LR, ADAM_B1, ADAM_B2, ADAM_EPS, ADAM_WD, ADAM_STEP = 0.001, 0.9, 0.999, 1e-08, 0.01, 10
MESH = pl.DeviceIdType.MESH
ANY = pl.BlockSpec(memory_space=pl.ANY)


def _params(n_axes):
    return pltpu.CompilerParams(dimension_semantics=("arbitrary",) * n_axes, vmem_limit_bytes=VMEM_LIMIT)


def _nn(a, b):
    return jnp.dot(a, b, preferred_element_type=F32)


def _nt(a, b):
    return lax.dot_general(a, b, (((1,), (1,)), ((), ())), preferred_element_type=F32)


def _tn(a, b):
    return lax.dot_general(a, b, (((0,), (0,)), ((), ())), preferred_element_type=F32)


def _sigmoid(z):
    return 1.0 / (1.0 + jnp.exp(-z))


def _sigmoid_fast(z):
    return pl.reciprocal(1.0 + jnp.exp(-z), approx=True)


def _row_tile(s):
    return 512 if s >= 2048 else s // 2


def _wide_tile(s):
    return 1024 if s >= 2048 else s // 2


def _attn_tile(s):
    return 512 if s >= 2048 else s // 4


def _position():
    x, y, c = lax.axis_index("x"), lax.axis_index("y"), lax.axis_index("c")
    chips = [(1 - x, y), (x, 1 - y), (1 - x, 1 - y)]
    return x, y, c, chips


def all_gather_rows(v, name):
    m_per, n = v.shape

    def body(x_ref, out_ref, send_sems, recv_sems, local_sem):
        x, y, c, chips = _position()
        me, sibling = (x, y, c), (x, y, 1 - c)

        def rows(px, py, pc):
            return out_ref.at[pl.ds((4 * px + 2 * py + pc) * m_per, m_per), :]

        def copy(k, block, to, src=None):
            return pltpu.make_async_remote_copy(
                src_ref=rows(*block) if src is None else src, dst_ref=rows(*block),
                send_sem=send_sems.at[k], recv_sem=recv_sems.at[k], device_id=to, device_id_type=MESH)

        mine = pltpu.make_async_copy(x_ref, rows(*me), local_sem)
        mine.start()
        first = [copy(0, me, sibling, src=x_ref)]
        first += [copy(1 + j, me, (*chip, c), src=x_ref) for j, chip in enumerate(chips)]
        for cp in first:
            cp.start()
        passed = [copy(4 + j, (*chip, c), sibling) for j, chip in enumerate(chips)]
        for j, chip in enumerate(chips):
            copy(1 + j, (*chip, c), me).wait_recv()
            passed[j].start()
        copy(0, sibling, me).wait_recv()
        for j, chip in enumerate(chips):
            copy(4 + j, (*chip, 1 - c), me).wait_recv()
        for cp in first + passed:
            cp.wait_send()
        mine.wait()

    return pl.pallas_call(
        body, name=name,
        out_shape=jax.ShapeDtypeStruct((N_DEV * m_per, n), v.dtype),
        in_specs=[pl.BlockSpec(memory_space=pltpu.VMEM)],
        out_specs=pl.BlockSpec(memory_space=pltpu.VMEM),
        scratch_shapes=[pltpu.SemaphoreType.DMA((7,)), pltpu.SemaphoreType.DMA((7,)), pltpu.SemaphoreType.DMA],
    )(v)


def _slab(full_ref, kind, b, lead):
    how = kind[0]
    if how == "slot":
        return full_ref.at[b, lead]
    if how == "col":
        w = kind[1]
        idx = (lead,) + (slice(None),) * (len(full_ref.shape) - 2) + (pl.ds(pl.multiple_of(b * w, LANES), w),)
        return full_ref.at[idx]
    h = kind[1]
    idx = (lead,) + (slice(None),) * (len(full_ref.shape) - 3) + (pl.ds(pl.multiple_of(b * h, 8), h), slice(None))
    return full_ref.at[idx]


def _full_shape(shard_shape, kind):
    if kind[0] == "slot":
        return (N_CHIP,) + tuple(shard_shape)
    if kind[0] == "col":
        return tuple(shard_shape[:-1]) + (N_CHIP * shard_shape[-1],)
    return tuple(shard_shape[:-2]) + (N_CHIP * shard_shape[-2], shard_shape[-1])


def _slab_start(shape, kind, b):
    zeros = [0] * len(shape)
    if kind[0] == "slot":
        return [b] + zeros[1:]
    if kind[0] == "col":
        return zeros[:-1] + [b * kind[1]]
    return zeros[:-2] + [b * kind[1], 0]


def cast_and_place(shard, kind, chip, after, name):
    rows, cols = shard.shape[-2:]
    lead = int(np.prod(shard.shape[:-2]))
    if kind[0] == "col":
        full3, where = (lead, rows, N_CHIP * cols), lambda p, b: (p, 0, b[0])
    elif kind[0] == "row":
        full3, where = (lead, N_CHIP * rows, cols), lambda p, b: (p, b[0], 0)
    else:
        full3, where = (N_CHIP * lead, rows, cols), lambda p, b: (b[0] * lead + p, 0, 0)

    def body(b_ref, s_ref, after_ref, o_ref):
        o_ref[...] = s_ref[...].astype(BF16)

    full = pl.pallas_call(
        body, name=name,
        grid_spec=pltpu.PrefetchScalarGridSpec(
            num_scalar_prefetch=1, grid=(lead,),
            in_specs=[pl.BlockSpec((1, rows, cols), lambda p, b: (p, 0, 0)), ANY],
            out_specs=pl.BlockSpec((1, rows, cols), where)),
        out_shape=jax.ShapeDtypeStruct(full3, BF16),
        compiler_params=_params(1),
    )(jnp.reshape(chip, (1,)).astype(jnp.int32), shard.reshape(lead, rows, cols), after)
    return full.reshape(_full_shape(shard.shape, kind))


def own_slab(full, kind, b, shard_shape):
    sizes = (1,) + tuple(shard_shape) if kind[0] == "slot" else tuple(shard_shape)
    return lax.dynamic_slice(full, _slab_start(full.shape, kind, b), sizes).reshape(shard_shape)


def gather_weights(fulls, kinds):
    n = len(fulls)

    def body(*refs):
        outs = refs[n:2 * n]
        send_sems, recv_sems = refs[2 * n:]
        x, y, c, chips = _position()
        b_me = 2 * x + y
        sibling = (x, y, 1 - c)
        sends = []

        def halves(t):
            lead = outs[t].shape[1] if kinds[t][0] == "slot" else outs[t].shape[0]
            return pl.ds(c * (lead // 2), lead // 2), pl.ds((1 - c) * (lead // 2), lead // 2)

        for t in range(n):
            mine, _ = halves(t)
            own = _slab(outs[t], kinds[t], b_me, mine)
            for j, chip in enumerate(chips):
                cp = pltpu.make_async_remote_copy(
                    src_ref=own, dst_ref=own,
                    send_sem=send_sems.at[6 * t + j], recv_sem=recv_sems.at[6 * t + j],
                    device_id=(*chip, c), device_id_type=MESH)
                cp.start()
                sends.append(cp)
        for t in range(n):
            mine, _ = halves(t)
            for j, chip in enumerate(chips):
                landed = _slab(outs[t], kinds[t], 2 * chip[0] + chip[1], mine)
                pltpu.make_async_remote_copy(
                    src_ref=landed, dst_ref=landed, send_sem=send_sems.at[6 * t + j], recv_sem=recv_sems.at[6 * t + j],
                    device_id=(*chip, c), device_id_type=MESH).wait_recv()
                cp = pltpu.make_async_remote_copy(
                    src_ref=landed, dst_ref=landed, send_sem=send_sems.at[6 * t + 3 + j],
                    recv_sem=recv_sems.at[6 * t + 3 + j], device_id=sibling, device_id_type=MESH)
                cp.start()
                sends.append(cp)
        for t in range(n):
            _, theirs = halves(t)
            for j, chip in enumerate(chips):
                landed = _slab(outs[t], kinds[t], 2 * chip[0] + chip[1], theirs)
                pltpu.make_async_remote_copy(
                    src_ref=landed, dst_ref=landed, send_sem=send_sems.at[6 * t + 3 + j],
                    recv_sem=recv_sems.at[6 * t + 3 + j], device_id=sibling, device_id_type=MESH).wait_recv()
        for cp in sends:
            cp.wait_send()

    return pl.pallas_call(
        body, name="gather_weights",
        out_shape=[jax.ShapeDtypeStruct(v.shape, v.dtype) for v in fulls],
        in_specs=[ANY] * n, out_specs=[ANY] * n,
        input_output_aliases={t: t for t in range(n)},
        scratch_shapes=[pltpu.SemaphoreType.DMA((6 * n,)), pltpu.SemaphoreType.DMA((6 * n,))],
    )(*fulls)


def scatter_grads(grads, kinds, shard_shapes):
    n = len(grads)

    def body(*refs):
        ins, outs = refs[:n], refs[n:2 * n]
        send_sems, recv_sems = refs[2 * n:]
        x, y, c, chips = _position()
        sends = []
        for t in range(n):
            whole = _whole(ins[t], kinds[t])
            for j, chip in enumerate(chips):
                cp = pltpu.make_async_remote_copy(
                    src_ref=_slab(ins[t], kinds[t], 2 * chip[0] + chip[1], whole), dst_ref=outs[t].at[j],
                    send_sem=send_sems.at[3 * t + j], recv_sem=recv_sems.at[3 * t + j],
                    device_id=(*chip, c), device_id_type=MESH)
                cp.start()
                sends.append(cp)
        for t in range(n):
            for j, chip in enumerate(chips):
                pltpu.make_async_remote_copy(
                    src_ref=outs[t].at[j], dst_ref=outs[t].at[j], send_sem=send_sems.at[3 * t + j],
                    recv_sem=recv_sems.at[3 * t + j], device_id=(*chip, c), device_id_type=MESH).wait_recv()
        for cp in sends:
            cp.wait_send()

    return pl.pallas_call(
        body, name="scatter_grads",
        out_shape=[jax.ShapeDtypeStruct((N_CHIP - 1,) + tuple(s), g.dtype) for g, s in zip(grads, shard_shapes)],
        in_specs=[ANY] * n, out_specs=[ANY] * n,
        scratch_shapes=[pltpu.SemaphoreType.DMA((3 * n,)), pltpu.SemaphoreType.DMA((3 * n,))],
    )(*grads)


def _whole(ref, kind):
    return pl.ds(0, ref.shape[1] if kind[0] == "slot" else ref.shape[0])


def _exchange_copies(srcs, lands, kinds, gather, send_sems, recv_sems):
    x, y, c, chips = _position()
    b_me = 2 * x + y
    out = []
    for t in range(len(lands)):
        for j, chip in enumerate(chips):
            b_j = 2 * chip[0] + chip[1]
            if gather:
                src = sent_to = _slab(lands[t], kinds[t], b_me, _whole(lands[t], kinds[t]))
                arrives = _slab(lands[t], kinds[t], b_j, _whole(lands[t], kinds[t]))
            else:
                src = _slab(srcs[t], kinds[t], b_j, _whole(srcs[t], kinds[t]))
                sent_to = arrives = lands[t].at[j]
            k = 3 * t + j
            send = pltpu.make_async_remote_copy(src_ref=src, dst_ref=sent_to, send_sem=send_sems.at[k],
                                                recv_sem=recv_sems.at[k], device_id=(*chip, c), device_id_type=MESH)
            recv = pltpu.make_async_remote_copy(src_ref=src, dst_ref=arrives, send_sem=send_sems.at[k],
                                                recv_sem=recv_sems.at[k], device_id=(*chip, c), device_id_type=MESH)
            out.append((send, recv))
    return out


def exchange_start(srcs, lands, kinds, gather, name):
    ns, nl = len(srcs), len(lands)
    hbm = pl.BlockSpec(memory_space=pltpu.HBM)

    def body(*refs):
        ins, lnd = refs[:ns], refs[ns:ns + nl]
        send_sems, recv_sems = refs[ns + nl], refs[ns + nl + 1]
        token = refs[-1]
        for send, _ in _exchange_copies(ins, lnd, kinds, gather, send_sems, recv_sems):
            send.start()
        token[...] = jnp.zeros_like(token)

    ops = [pltpu.with_memory_space_constraint(v, pltpu.HBM) for v in (*srcs, *lands)]
    res = pl.pallas_call(
        body, name=name,
        out_shape=(pltpu.SemaphoreType.DMA((3 * nl,)), pltpu.SemaphoreType.DMA((3 * nl,)),
                   *[pltpu.HBM(v.shape, v.dtype) for v in ops], jax.ShapeDtypeStruct((8, LANES), F32)),
        in_specs=[hbm] * (ns + nl),
        out_specs=(pl.BlockSpec(memory_space=pltpu.SEMAPHORE), pl.BlockSpec(memory_space=pltpu.SEMAPHORE),
                   *[hbm] * (ns + nl), pl.BlockSpec(memory_space=pltpu.VMEM)),
        input_output_aliases={t: 2 + t for t in range(ns + nl)},
        compiler_params=pltpu.CompilerParams(has_side_effects=pltpu.SideEffectType.DATAFLOW_SIDE_EFFECTING),
    )(*ops)
    return res[0], res[1], list(res[2:2 + ns]), list(res[2 + ns:2 + ns + nl]), res[-1]


def exchange_wait(send_sems, recv_sems, srcs, lands, after, kinds, gather, name):
    ns, nl = len(srcs), len(lands)
    hbm = pl.BlockSpec(memory_space=pltpu.HBM)

    def body(*refs):
        ins, lnd = refs[:ns], refs[ns:ns + nl]
        ssem, rsem = refs[ns + nl], refs[ns + nl + 1]
        for send, recv in _exchange_copies(ins, lnd, kinds, gather, ssem, rsem):
            send.wait_send()
            recv.wait_recv()

    res = pl.pallas_call(
        body, name=name,
        out_shape=tuple(pltpu.HBM(v.shape, v.dtype) for v in (*srcs, *lands)),
        in_specs=[hbm] * (ns + nl) + [pl.BlockSpec(memory_space=pltpu.SEMAPHORE)] * 2 + [ANY],
        out_specs=tuple([hbm] * (ns + nl)),
        input_output_aliases={t: t for t in range(ns + nl)},
        compiler_params=pltpu.CompilerParams(has_side_effects=pltpu.SideEffectType.DATAFLOW_SIDE_EFFECTING),
    )(*srcs, *lands, send_sems, recv_sems, after)
    return list(res[:ns]), list(res[ns:])


def swap_with_sibling(parts):
    n = len(parts)

    def body(*refs):
        ins, outs = refs[:n], refs[n:2 * n]
        send_sems, recv_sems = refs[2 * n:]
        x, y, c, _ = _position()
        cps = []
        for t in range(n):
            cp = pltpu.make_async_remote_copy(
                src_ref=ins[t], dst_ref=outs[t], send_sem=send_sems.at[t], recv_sem=recv_sems.at[t],
                device_id=(x, y, 1 - c), device_id_type=MESH)
            cp.start()
            cps.append(cp)
        for cp in cps:
            cp.wait_recv()
        for cp in cps:
            cp.wait_send()

    return pl.pallas_call(
        body, name="swap_with_sibling",
        out_shape=[jax.ShapeDtypeStruct(p.shape, p.dtype) for p in parts],
        in_specs=[ANY] * n, out_specs=[ANY] * n,
        scratch_shapes=[pltpu.SemaphoreType.DMA((n,)), pltpu.SemaphoreType.DMA((n,))],
    )(*parts)


def _modulated(xv, mv_ref):
    g, shift, scale = mv_ref[0:1, :], mv_ref[1:2, :], mv_ref[2:3, :]
    r = lax.rsqrt(jnp.mean(xv * xv, axis=-1, keepdims=True) + EPS)
    xhat = xv * r
    xn = xhat * g
    return xn * (1.0 + scale) + shift, xhat, xn, r, g, scale


def _resident(block_shape, index):
    return pl.BlockSpec(block_shape, lambda i: index, pipeline_mode=pl.Buffered(1))


def modmm(x, mv, mv_idx, w, w_block, w_index, n_cols, chunk, out_dtype, name, want_h):
    s, d = x.shape
    tm = _row_tile(s)

    def body(x_ref, mv_ref, w_ref, out_ref, *rest):
        h = _modulated(x_ref[...], mv_ref)[0].astype(BF16)
        if want_h:
            rest[0][...] = h
        for n in range(n_cols // chunk):
            cols = slice(n * chunk, (n + 1) * chunk)
            out_ref[:, cols] = _nn(h, w_ref[:, cols]).astype(out_dtype)

    row = pl.BlockSpec((tm, d), lambda i: (i, 0))
    out_shapes = [jax.ShapeDtypeStruct((s, n_cols), out_dtype)]
    out_specs = [pl.BlockSpec((tm, n_cols), lambda i: (i, 0))]
    if want_h:
        out_shapes.append(jax.ShapeDtypeStruct((s, d), BF16))
        out_specs.append(row)
    res = pl.pallas_call(
        body, name=name, grid=(s // tm,),
        in_specs=[row, pl.BlockSpec((None, None, 8, d), lambda i: (*mv_idx, 0, 0)), _resident(w_block, w_index)],
        out_specs=out_specs, out_shape=out_shapes,
        compiler_params=_params(1),
    )(x, mv, w)
    return res if want_h else (res[0], None)


def resmm(x, mv, mv_idx, coef, lhs, w, w_block, w_index, kdim, chunk, name, ffn):
    s, d = x.shape
    tm = _row_tile(s)
    lhs_cols = 2 * kdim if ffn else kdim

    def body(x_ref, mv_ref, lhs_ref, w_ref, xo_ref, y_ref, *rest):
        y = jnp.zeros((tm, d), F32)
        for k in range(kdim // chunk):
            rows = slice(k * chunk, (k + 1) * chunk)
            if ffn:
                ag = lhs_ref[:, rows].astype(F32)
                au = lhs_ref[:, kdim + k * chunk:kdim + (k + 1) * chunk].astype(F32)
                left = (ag * _sigmoid_fast(ag) * au).astype(BF16)
                rest[0][:, rows] = left
            else:
                left = lhs_ref[:, rows]
            y = y + _nn(left, w_ref[rows, :])
        y_ref[...] = y.astype(BF16)
        xo_ref[...] = x_ref[...] + (coef * mv_ref[3:4, :]) * y

    row = pl.BlockSpec((tm, d), lambda i: (i, 0))
    out_shapes = [jax.ShapeDtypeStruct((s, d), F32), jax.ShapeDtypeStruct((s, d), BF16)]
    out_specs = [row, row]
    if ffn:
        out_shapes.append(jax.ShapeDtypeStruct((s, kdim), BF16))
        out_specs.append(pl.BlockSpec((tm, kdim), lambda i: (i, 0)))
    return pl.pallas_call(
        body, name=name, grid=(s // tm,),
        in_specs=[row, pl.BlockSpec((None, None, 8, d), lambda i: (*mv_idx, 0, 0)),
                  pl.BlockSpec((tm, lhs_cols), lambda i: (i, 0)), _resident(w_block, w_index)],
        out_specs=out_specs, out_shape=out_shapes,
        compiler_params=_params(1),
    )(x, mv, lhs, w)


def resmm_bwd(dxo, y, mv, mv_idx, coef, w, w_block, w_index, kdim, chunk, name, a=None):
    s, d = dxo.shape
    ffn = a is not None
    tm = _row_tile(s)
    dl_cols = 2 * kdim if ffn else kdim

    def body(dxo_ref, y_ref, mv_ref, w_ref, *rest):
        if ffn:
            a_ref, dy_ref, dl_ref, dgate_ref = rest
        else:
            dy_ref, dl_ref, dgate_ref = rest

        @pl.when(pl.program_id(0) == 0)
        def _():
            dgate_ref[...] = jnp.zeros_like(dgate_ref)

        dxv = dxo_ref[...]
        dy = ((coef * mv_ref[3:4, :]) * dxv).astype(BF16)
        dy_ref[...] = dy
        dgate_ref[0:1, :] += jnp.sum(coef * dxv * y_ref[...].astype(F32), axis=0, keepdims=True)
        for k in range(kdim // chunk):
            rows = slice(k * chunk, (k + 1) * chunk)
            dl = _nt(dy, w_ref[rows, :])
            if ffn:
                ups = slice(kdim + k * chunk, kdim + (k + 1) * chunk)
                ag = a_ref[:, rows].astype(F32)
                au = a_ref[:, ups].astype(F32)
                sg = _sigmoid_fast(ag)
                dl_ref[:, rows] = (dl * au * (sg * (1.0 + ag * (1.0 - sg)))).astype(BF16)
                dl_ref[:, ups] = (dl * (ag * sg)).astype(BF16)
            else:
                dl_ref[:, rows] = dl.astype(BF16)

    row = pl.BlockSpec((tm, d), lambda i: (i, 0))
    wide = pl.BlockSpec((tm, dl_cols), lambda i: (i, 0))
    in_specs = [row, row, pl.BlockSpec((None, None, 8, d), lambda i: (*mv_idx, 0, 0)), _resident(w_block, w_index)]
    ops = [dxo, y, mv, w]
    if ffn:
        in_specs.append(wide)
        ops.append(a)
    return pl.pallas_call(
        body, name=name, grid=(s // tm,),
        in_specs=in_specs,
        out_specs=[row, wide, pl.BlockSpec((8, d), lambda i: (0, 0))],
        out_shape=[jax.ShapeDtypeStruct((s, d), BF16), jax.ShapeDtypeStruct((s, dl_cols), BF16),
                   jax.ShapeDtypeStruct((8, d), F32)],
        compiler_params=_params(1),
    )(*ops)


def modmm_bwd(dl, w, w_block, w_index, n_cols, chunk, x, dxo, mv, mv_idx, name, more=None):
    s, d = x.shape
    tm = _row_tile(s)

    def body(dl_ref, w_ref, x_ref, dxo_ref, mv_ref, *rest):
        if more is not None:
            dl2_ref, w2_ref, dx_ref, red_ref = rest
            dh = _nt(dl2_ref[...], w2_ref[...])
        else:
            dx_ref, red_ref = rest
            dh = jnp.zeros((tm, d), F32)

        @pl.when(pl.program_id(0) == 0)
        def _():
            red_ref[...] = jnp.zeros_like(red_ref)

        for n in range(n_cols // chunk):
            cols = slice(n * chunk, (n + 1) * chunk)
            dh = dh + _nt(dl_ref[:, cols], w_ref[:, cols])
        _, xhat, xn, r, g, scale = _modulated(x_ref[...], mv_ref)
        dxn = dh * (1.0 + scale)
        red_ref[0:1, :] += jnp.sum(dxn * xhat, axis=0, keepdims=True)
        red_ref[1:2, :] += jnp.sum(dh, axis=0, keepdims=True)
        red_ref[2:3, :] += jnp.sum(dh * xn, axis=0, keepdims=True)
        gd = dxn * g
        dx_ref[...] = dxo_ref[...] + r * (gd - xhat * jnp.mean(gd * xhat, axis=-1, keepdims=True))

    row = pl.BlockSpec((tm, d), lambda i: (i, 0))
    in_specs = [pl.BlockSpec((tm, n_cols), lambda i: (i, 0)), _resident(w_block, w_index), row, row,
                pl.BlockSpec((None, None, 8, d), lambda i: (*mv_idx, 0, 0))]
    ops = [dl, w, x, dxo, mv]
    if more is not None:
        in_specs += [pl.BlockSpec((tm, more[0].shape[1]), lambda i: (i, 0)), _resident(more[1], more[2])]
        ops += [more[0], w]
    return pl.pallas_call(
        body, name=name, grid=(s // tm,),
        in_specs=in_specs,
        out_specs=[row, pl.BlockSpec((8, d), lambda i: (0, 0))],
        out_shape=[jax.ShapeDtypeStruct((s, d), F32), jax.ShapeDtypeStruct((8, d), F32)],
        compiler_params=_params(1),
    )(*ops)


def weight_grad(a, a_spec, b, b_spec, grid_mn, s, bm, bn, dest, out_spec, name):
    tk = _wide_tile(s)
    k_tiles = s // tk

    def body(a_ref, b_ref, dest_ref, out_ref, acc):
        k = pl.program_id(2)

        @pl.when(k == 0)
        def _():
            acc[...] = jnp.zeros_like(acc)

        acc[...] += _tn(a_ref[...], b_ref[...])

        @pl.when(k == k_tiles - 1)
        def _():
            out_ref[...] = acc[...].astype(out_ref.dtype)

    return pl.pallas_call(
        body, name=name, grid=(*grid_mn, k_tiles),
        in_specs=[a_spec, b_spec, ANY], out_specs=out_spec,
        out_shape=jax.ShapeDtypeStruct(dest.shape, dest.dtype),
        input_output_aliases={2: 0},
        scratch_shapes=[pltpu.VMEM((bm, bn), F32)],
        compiler_params=_params(3),
    )(a, b, dest)


def _head_mean(v):
    lane = lax.broadcasted_iota(jnp.int32, v.shape, 1)
    lo = jnp.sum(jnp.where(lane < HEAD_DIM, v, 0.0), axis=-1, keepdims=True)
    hi = jnp.sum(v, axis=-1, keepdims=True) - lo
    return jnp.where(lane < HEAD_DIM, lo, hi) * (1.0 / HEAD_DIM)


def qknorm_fwd(proj, gains, width, name):
    s = proj.shape[0]
    nqk = gains.shape[1]
    tm = _row_tile(s)

    def body(p_ref, g_ref, o_ref):
        for cc in range(width // LANES):
            sl = slice(cc * LANES, (cc + 1) * LANES)
            xv = p_ref[:, sl].astype(F32)
            r = lax.rsqrt(_head_mean(xv * xv) + EPS)
            o_ref[:, sl] = (xv * r * g_ref[:, sl]).astype(BF16)

    blk = pl.BlockSpec((tm, width), lambda i, c: (i, c))
    return pl.pallas_call(
        body, name=name, grid=(s // tm, nqk // width),
        in_specs=[blk, pl.BlockSpec((1, width), lambda i, c: (0, c))],
        out_specs=blk, out_shape=jax.ShapeDtypeStruct((s, nqk), BF16),
        compiler_params=_params(2),
    )(proj, gains)


def qknorm_bwd(proj, gains, d, d_spec, n_cols, width, name):
    s = proj.shape[0]
    nqk = gains.shape[1] // width
    n_blocks = n_cols // width
    tm = _row_tile(s)

    def body(p_ref, g_ref, d_ref, o_ref, dg_ref):
        c, i = pl.program_id(0), pl.program_id(1)

        @pl.when(i == 0)
        def _():
            dg_ref[...] = jnp.zeros_like(dg_ref)

        @pl.when(c < nqk)
        def _():
            for cc in range(width // LANES):
                sl = slice(cc * LANES, (cc + 1) * LANES)
                xv = p_ref[:, sl].astype(F32)
                r = lax.rsqrt(_head_mean(xv * xv) + EPS)
                xhat = xv * r
                dv = d_ref[:, sl]
                gd = dv * g_ref[:, sl]
                o_ref[:, sl] = (r * (gd - xhat * _head_mean(gd * xhat))).astype(BF16)
                dg_ref[0:1, sl] += jnp.sum(dv * xhat, axis=0, keepdims=True)

        @pl.when(c >= nqk)
        def _():
            o_ref[...] = d_ref[...].astype(BF16)

    return pl.pallas_call(
        body, name=name, grid=(n_blocks, s // tm),
        in_specs=[pl.BlockSpec((tm, width), lambda c, i: (i, c)),
                  pl.BlockSpec((1, width), lambda c, i: (0, jnp.minimum(c, nqk - 1))), d_spec],
        out_specs=[pl.BlockSpec((tm, width), lambda c, i: (i, c)), pl.BlockSpec((8, width), lambda c, i: (0, c))],
        out_shape=[jax.ShapeDtypeStruct((s, n_cols), BF16), jax.ShapeDtypeStruct((8, n_cols), F32)],
        compiler_params=_params(2),
    )(proj, gains, d)


def _swa_mask(first, copies):
    qi = (lax.broadcasted_iota(jnp.int32, (copies * BLOCK, 2 * BLOCK), 0) & (BLOCK - 1)) + BLOCK
    kj = lax.broadcasted_iota(jnp.int32, (copies * BLOCK, 2 * BLOCK), 1)
    dist = qi - kj
    return (dist >= 0) & (dist < BLOCK) & ((kj >= BLOCK) | jnp.logical_not(first))


def swa_fwd(qkn, proj, bias, sink, d, name):
    s = qkn.shape[0]
    hq = d // HEAD_DIM
    hkv = hq // GROUP
    kw = hkv * HEAD_DIM
    nblk = s // BLOCK
    kcol = d // kw

    def body(q_ref, kc_ref, kp_ref, vc_ref, vp_ref, bias_ref, sink_ref, o_ref, lse_ref):
        mask = _swa_mask(pl.program_id(0) == 0, 1)
        lse_ref[...] = jnp.zeros_like(lse_ref)
        for kvh in range(hkv):
            cols = slice(kvh * HEAD_DIM, (kvh + 1) * HEAD_DIM)
            k2 = jnp.concatenate([kp_ref[:, cols], kc_ref[:, cols]], axis=0)
            v2 = jnp.concatenate([vp_ref[:, cols], vc_ref[:, cols]], axis=0)
            for g in range(GROUP):
                h = kvh * GROUP + g
                hc = slice(h * HEAD_DIM, (h + 1) * HEAD_DIM)
                sc = jnp.where(mask, _nt(q_ref[:, hc], k2) + bias_ref[h], NEG)
                sk = sink_ref[0, h]
                m = jnp.maximum(jnp.max(sc, axis=-1, keepdims=True), sk)
                p = jnp.exp(sc - m)
                denom = jnp.sum(p, axis=-1, keepdims=True) + jnp.exp(sk - m)
                o_ref[:, hc] = (_nn(p.astype(BF16), v2) / denom).astype(BF16)
                lse_ref[:, h:h + 1] = m + jnp.log(denom)

    prev = lambda i: jnp.maximum(i - 1, 0)
    return pl.pallas_call(
        body, name=name, grid=(nblk,),
        in_specs=[pl.BlockSpec((BLOCK, d), lambda i: (i, 0)),
                  pl.BlockSpec((BLOCK, kw), lambda i: (i, kcol)),
                  pl.BlockSpec((BLOCK, kw), lambda i: (prev(i), kcol)),
                  pl.BlockSpec((BLOCK, kw), lambda i: (i, kcol + 1)),
                  pl.BlockSpec((BLOCK, kw), lambda i: (prev(i), kcol + 1)),
                  pl.BlockSpec((hq, BLOCK, 2 * BLOCK), lambda i: (0, 0, 0)),
                  pl.BlockSpec(memory_space=pltpu.SMEM)],
        out_specs=[pl.BlockSpec((BLOCK, d), lambda i: (i, 0)), pl.BlockSpec((BLOCK, LANES), lambda i: (i, 0))],
        out_shape=[jax.ShapeDtypeStruct((s, d), BF16), jax.ShapeDtypeStruct((s, LANES), F32)],
        compiler_params=_params(1),
    )(qkn, qkn, qkn, proj, proj, bias, sink)


def swa_bwd(qkn, proj, bias, sink, do, o, lse, d, name):
    s = qkn.shape[0]
    hq = d // HEAD_DIM
    hkv = hq // GROUP
    kw = hkv * HEAD_DIM
    nblk = s // BLOCK
    kcol = d // kw
    wide = d + 2 * kw

    def body(q_ref, kc_ref, kp_ref, vc_ref, vp_ref, bias_ref, sink_ref, do_ref, o_ref, lse_ref,
             out_ref, dbias_ref, dsink_ref, carry, fresh):
        i = pl.program_id(0)

        @pl.when(i == 0)
        def _():
            dbias_ref[...] = jnp.zeros_like(dbias_ref)
            dsink_ref[...] = jnp.zeros_like(dsink_ref)
            carry[...] = jnp.zeros_like(carry)

        @pl.when(i == nblk)
        def _():
            fresh[...] = jnp.zeros_like(fresh)

        @pl.when(i < nblk)
        def _():
            mask = _swa_mask(i == 0, GROUP)
            for kvh in range(hkv):
                cols = slice(kvh * HEAD_DIM, (kvh + 1) * HEAD_DIM)
                heads = range(kvh * GROUP, (kvh + 1) * GROUP)
                stack = lambda ref: jnp.concatenate([ref[:, h * HEAD_DIM:(h + 1) * HEAD_DIM] for h in heads], axis=0)
                k2 = jnp.concatenate([kp_ref[:, cols], kc_ref[:, cols]], axis=0)
                v2 = jnp.concatenate([vp_ref[:, cols], vc_ref[:, cols]], axis=0)
                qs, dos, os_ = stack(q_ref), stack(do_ref), stack(o_ref)
                bias_s = bias_ref[kvh * GROUP:(kvh + 1) * GROUP].reshape(GROUP * BLOCK, 2 * BLOCK)
                sk = jnp.concatenate([jnp.full((BLOCK, 1), sink_ref[0, h], F32) for h in heads], axis=0)
                lse_s = jnp.concatenate([lse_ref[:, h:h + 1] for h in heads], axis=0)
                sc = jnp.where(mask, _nt(qs, k2) + bias_s, NEG)
                p = jnp.exp(sc - lse_s)
                delta = jnp.sum(dos.astype(F32) * os_.astype(F32), axis=-1, keepdims=True)
                ds = p * (_nt(dos, v2) - delta)
                dbias_ref[kvh * GROUP:(kvh + 1) * GROUP] += ds.reshape(GROUP, BLOCK, 2 * BLOCK)
                to_sink = -jnp.exp(sk - lse_s) * delta
                dsb = ds.astype(BF16)
                dqs = _nn(dsb, k2)
                for g, h in enumerate(heads):
                    rows = slice(g * BLOCK, (g + 1) * BLOCK)
                    fresh[0, :, h * HEAD_DIM:(h + 1) * HEAD_DIM] = dqs[rows]
                    dsink_ref[0:1, h:h + 1] += jnp.sum(to_sink[rows], axis=0, keepdims=True)
                dk2 = _tn(dsb, qs)
                dv2 = _tn(p.astype(BF16), dos)
                kc_cols = slice(d + kvh * HEAD_DIM, d + (kvh + 1) * HEAD_DIM)
                vc_cols = slice(d + kw + kvh * HEAD_DIM, d + kw + (kvh + 1) * HEAD_DIM)
                fresh[0, :, kc_cols] = dk2[BLOCK:]
                fresh[0, :, vc_cols] = dv2[BLOCK:]
                fresh[1, :, kc_cols] = dk2[:BLOCK]
                fresh[1, :, vc_cols] = dv2[:BLOCK]

        lane = lax.broadcasted_iota(jnp.int32, (BLOCK, wide), 1)
        out_ref[...] = carry[...] + jnp.where(lane >= d, fresh[1], 0.0)

        @pl.when(i < nblk)
        def _():
            carry[...] = fresh[0]

    cur = lambda i: jnp.minimum(i, nblk - 1)
    prev = lambda i: jnp.maximum(jnp.minimum(i, nblk - 1) - 1, 0)
    return pl.pallas_call(
        body, name=name, grid=(nblk + 1,),
        in_specs=[pl.BlockSpec((BLOCK, d), lambda i: (cur(i), 0)),
                  pl.BlockSpec((BLOCK, kw), lambda i: (cur(i), kcol)),
                  pl.BlockSpec((BLOCK, kw), lambda i: (prev(i), kcol)),
                  pl.BlockSpec((BLOCK, kw), lambda i: (cur(i), kcol + 1)),
                  pl.BlockSpec((BLOCK, kw), lambda i: (prev(i), kcol + 1)),
                  pl.BlockSpec((hq, BLOCK, 2 * BLOCK), lambda i: (0, 0, 0)),
                  pl.BlockSpec(memory_space=pltpu.SMEM),
                  pl.BlockSpec((BLOCK, d), lambda i: (cur(i), 0)),
                  pl.BlockSpec((BLOCK, d), lambda i: (cur(i), 0)),
                  pl.BlockSpec((BLOCK, LANES), lambda i: (cur(i), 0))],
        out_specs=[pl.BlockSpec((BLOCK, wide), lambda i: (jnp.maximum(i - 1, 0), 0)),
                   pl.BlockSpec((hq, BLOCK, 2 * BLOCK), lambda i: (0, 0, 0)),
                   pl.BlockSpec((8, LANES), lambda i: (0, 0))],
        out_shape=[jax.ShapeDtypeStruct((s, wide), F32), jax.ShapeDtypeStruct((hq, BLOCK, 2 * BLOCK), F32),
                   jax.ShapeDtypeStruct((8, LANES), F32)],
        scratch_shapes=[pltpu.VMEM((BLOCK, wide), F32), pltpu.VMEM((2, BLOCK, wide), F32)],
        compiler_params=_params(1),
    )(qkn, qkn, qkn, proj, proj, bias, sink, do, o, lse)


def _rel_bucket_table():
    qi = np.arange(BLOCK)[:, None] + BLOCK
    kj = np.arange(2 * BLOCK)[None, :]
    n = np.maximum(qi - kj, 0)
    max_exact = REL_BUCKETS // 2
    nf = np.maximum(n, 1).astype(np.float32)
    large = max_exact + (np.log(nf / max_exact) / math.log(REL_MAX_DIST / max_exact)
                         * (REL_BUCKETS - max_exact)).astype(np.int32)
    large = np.minimum(large, REL_BUCKETS - 1)
    return np.where(n < max_exact, n, large).astype(np.int32)


def rel_bias_table(rel_bias, bucket):
    hq = rel_bias.shape[1]

    def body(rb_ref, bucket_ref, out_ref):
        tbl = bucket_ref[...]

        def per_head(h, carry):
            def per_bucket(b, acc):
                return jnp.where(tbl == b, rb_ref[b, h], acc)

            out_ref[h] = lax.fori_loop(0, REL_BUCKETS, per_bucket, jnp.zeros(tbl.shape, F32))
            return carry

        lax.fori_loop(0, hq, per_head, 0)

    return pl.pallas_call(
        body, name="rel_bias_table",
        in_specs=[pl.BlockSpec(memory_space=pltpu.SMEM), pl.BlockSpec(memory_space=pltpu.VMEM)],
        out_specs=pl.BlockSpec(memory_space=pltpu.VMEM),
        out_shape=jax.ShapeDtypeStruct((hq,) + tuple(bucket.shape), F32),
    )(rel_bias, bucket)


def rel_bias_grad(dbias, bucket):
    n_layers, hq = dbias.shape[:2]

    def body(db_ref, bucket_ref, out_ref):
        tbl = bucket_ref[...]

        def per_head(h, carry):
            dsum = db_ref[0, h]
            for a in range(1, n_layers):
                dsum = dsum + db_ref[a, h]

            def per_bucket(b, carry2):
                out_ref[b, h] = jnp.sum(jnp.where(tbl == b, dsum, 0.0))
                return carry2

            return lax.fori_loop(0, REL_BUCKETS, per_bucket, carry)

        lax.fori_loop(0, hq, per_head, 0)

    return pl.pallas_call(
        body, name="rel_bias_grad",
        in_specs=[pl.BlockSpec(memory_space=pltpu.VMEM), pl.BlockSpec(memory_space=pltpu.VMEM)],
        out_specs=pl.BlockSpec(memory_space=pltpu.SMEM),
        out_shape=jax.ShapeDtypeStruct((REL_BUCKETS, hq), F32),
    )(dbias, bucket)


def _split3(v):
    hi = v.astype(BF16)
    r1 = v - hi.astype(F32)
    mid = r1.astype(BF16)
    lo = (r1 - mid.astype(F32)).astype(BF16)
    return hi, mid, lo


def _tri_sum(tri, v):
    hi, mid, lo = _split3(v)
    return _nn(tri, hi) + _nn(tri, mid) + _nn(tri, lo)


def fox_gates(fl, b_f, name):
    s = fl.shape[0]
    t = _row_tile(s)

    def body(fl_ref, b_ref, f_ref, carry):
        @pl.when(pl.program_id(0) == 0)
        def _():
            carry[...] = jnp.zeros_like(carry)

        z = fl_ref[...] + b_ref[...]
        logf = jnp.minimum(z, 0.0) - jnp.log(1.0 + jnp.exp(-jnp.abs(z)))
        r = lax.broadcasted_iota(jnp.int32, (t, t), 0)
        cidx = lax.broadcasted_iota(jnp.int32, (t, t), 1)
        tri = jnp.where(cidx <= r, 1.0, 0.0).astype(BF16)
        f = _tri_sum(tri, logf) + carry[0:1, :]
        f_ref[...] = f
        carry[0:1, :] = f_ref[t - 1:t, :]

    blk = pl.BlockSpec((t, LANES), lambda i: (i, 0))
    return pl.pallas_call(
        body, name=name, grid=(s // t,),
        in_specs=[blk, pl.BlockSpec((1, LANES), lambda i: (0, 0))],
        out_specs=blk, out_shape=jax.ShapeDtypeStruct((s, LANES), F32),
        scratch_shapes=[pltpu.VMEM((8, LANES), F32)],
        compiler_params=_params(1),
    )(fl, b_f)


def fox_gates_bwd(fl, b_f, df_query, df_key, name):
    s = fl.shape[0]
    t = _row_tile(s)
    nb = s // t

    def body(fl_ref, b_ref, dfq_ref, dfk_ref, dfl_ref, db_ref, carry):
        @pl.when(pl.program_id(0) == 0)
        def _():
            carry[...] = jnp.zeros_like(carry)
            db_ref[...] = jnp.zeros_like(db_ref)

        dfv = dfq_ref[...] + dfk_ref[...]
        r = lax.broadcasted_iota(jnp.int32, (t, t), 0)
        cidx = lax.broadcasted_iota(jnp.int32, (t, t), 1)
        tri = jnp.where(cidx >= r, 1.0, 0.0).astype(BF16)
        dlog = _tri_sum(tri, dfv) + carry[0:1, :]
        carry[0:1, :] += jnp.sum(dfv, axis=0, keepdims=True)
        z = fl_ref[...] + b_ref[...]
        dz = dlog * (1.0 - _sigmoid(z))
        dfl_ref[...] = dz.astype(BF16)
        db_ref[0:1, :] += jnp.sum(dz, axis=0, keepdims=True)

    rev = pl.BlockSpec((t, LANES), lambda i: (nb - 1 - i, 0))
    return pl.pallas_call(
        body, name=name, grid=(nb,),
        in_specs=[rev, pl.BlockSpec((1, LANES), lambda i: (0, 0)), rev, rev],
        out_specs=[rev, pl.BlockSpec((8, LANES), lambda i: (0, 0))],
        out_shape=[jax.ShapeDtypeStruct((s, LANES), BF16), jax.ShapeDtypeStruct((8, LANES), F32)],
        scratch_shapes=[pltpu.VMEM((8, LANES), F32)],
        compiler_params=_params(1),
    )(fl, b_f, df_query, df_key)


def fox_fwd(qkn, proj, f_col, f_row, d, name):
    s = qkn.shape[0]
    t = _attn_tile(s)
    hs = min(FOX_HEADS_FWD, d // HEAD_DIM)
    wide = hs * HEAD_DIM
    n_pairs = d // wide
    nt = s // t

    def body(q_ref, k_ref, v_ref, fq_ref, fk_ref, o_ref, o32_ref, lse_ref, m_scr, l_scr, acc):
        i, j = pl.program_id(1), pl.program_id(2)

        @pl.when(j == 0)
        def _():
            m_scr[...] = jnp.full_like(m_scr, NEG)
            l_scr[...] = jnp.zeros_like(l_scr)
            acc[...] = jnp.zeros_like(acc)

        @pl.when(j <= i)
        def _():
            krow = lax.broadcasted_iota(jnp.int32, (t, t), 0)
            qcol = lax.broadcasted_iota(jnp.int32, (t, t), 1)
            visible = (krow <= qcol) | (j < i)
            for hh in range(hs):
                hc = slice(hh * HEAD_DIM, (hh + 1) * HEAD_DIM)
                st = _nt(k_ref[:, hc], q_ref[:, hc]) + fq_ref[hh] - fk_ref[hh]
                st = jnp.where(visible, st, NEG)
                m_prev = m_scr[hh]
                m_new = jnp.maximum(m_prev, jnp.max(st, axis=0, keepdims=True))
                alpha = jnp.exp(m_prev - m_new)
                pt = jnp.exp(st - m_new)
                l_scr[hh] = alpha * l_scr[hh] + jnp.sum(pt, axis=0, keepdims=True)
                acc[hc, :] = alpha * acc[hc, :] + _tn(v_ref[:, hc], pt.astype(BF16))
                m_scr[hh] = m_new

        @pl.when(j == i)
        def _():
            l_full = jnp.concatenate([jnp.broadcast_to(l_scr[hh], (HEAD_DIM, t)) for hh in range(hs)], axis=0)
            ov = (acc[...] / l_full).T
            o_ref[...] = ov.astype(BF16)
            o32_ref[...] = ov
            lse_ref[...] = m_scr[...] + jnp.log(l_scr[...])

    kv = lambda j, i: jnp.minimum(j, i)
    return pl.pallas_call(
        body, name=name, grid=(n_pairs, nt, nt),
        in_specs=[pl.BlockSpec((t, wide), lambda p, i, j: (i, p)),
                  pl.BlockSpec((t, wide), lambda p, i, j: (kv(j, i), n_pairs + p)),
                  pl.BlockSpec((t, wide), lambda p, i, j: (kv(j, i), 2 * n_pairs + p)),
                  pl.BlockSpec((hs, 1, t), lambda p, i, j: (p, 0, i)),
                  pl.BlockSpec((hs, t, 1), lambda p, i, j: (p, kv(j, i), 0))],
        out_specs=[pl.BlockSpec((t, wide), lambda p, i, j: (i, p)),
                   pl.BlockSpec((t, wide), lambda p, i, j: (i, p)),
                   pl.BlockSpec((hs, 1, t), lambda p, i, j: (p, 0, i))],
        out_shape=[jax.ShapeDtypeStruct((s, d), BF16), jax.ShapeDtypeStruct((s, d), F32),
                   jax.ShapeDtypeStruct((hs * n_pairs, 1, s), F32)],
        scratch_shapes=[pltpu.VMEM((hs, 1, t), F32), pltpu.VMEM((hs, 1, t), F32), pltpu.VMEM((wide, t), F32)],
        compiler_params=_params(3),
    )(qkn, qkn, proj, f_row, f_col)


def fox_bwd(qkn, proj, f_col, f_row, lse_row, do, o, d, name):
    s = qkn.shape[0]
    t = _attn_tile(s)
    hs, wide = FOX_HEADS, FOX_HEADS * HEAD_DIM
    n_pairs = d // wide
    nt = s // t

    def body(q_ref, k_ref, v_ref, fk_ref, fq_ref, lse_ref, do_ref, o_ref, out_ref, df_ref, dfq_ref,
             dq_acc, dkv_acc, df_acc, dfq_acc):
        j, i = pl.program_id(1), pl.program_id(2)

        @pl.when((j == 0) & (i == 0))
        def _():
            dq_acc[...] = jnp.zeros_like(dq_acc)
            dfq_acc[...] = jnp.zeros_like(dfq_acc)

        @pl.when(i == 0)
        def _():
            dkv_acc[...] = jnp.zeros_like(dkv_acc)
            df_acc[...] = jnp.zeros_like(df_acc)

        @pl.when(i >= j)
        def _():
            krow = lax.broadcasted_iota(jnp.int32, (t, t), 0)
            qcol = lax.broadcasted_iota(jnp.int32, (t, t), 1)
            visible = (krow <= qcol) | (i > j)
            ones = jnp.ones((8, HEAD_DIM), BF16)
            for hh in range(hs):
                hc = slice(hh * HEAD_DIM, (hh + 1) * HEAD_DIM)
                q, k, v, dov = q_ref[:, hc], k_ref[:, hc], v_ref[:, hc], do_ref[:, hc]
                st = _nt(k, q) + fq_ref[hh] - fk_ref[hh]
                pt = jnp.exp(jnp.where(visible, st, NEG) - lse_ref[hh])
                hi, mid, lo = _split3(dov.astype(F32) * o_ref[:, hc])
                delta = jnp.max(_nt(ones, hi) + _nt(ones, mid) + _nt(ones, lo), axis=0, keepdims=True)
                dst = pt * (_nt(v, dov) - delta)
                dsb = dst.astype(BF16)
                dkv_acc[1, :, hc] += _nn(pt.astype(BF16), dov)
                dkv_acc[0, :, hc] += _nn(dsb, q)
                dq_acc[pl.ds(pl.multiple_of(i * t, t), t), hc] += _tn(dsb, k)
                df_acc[hh] -= jnp.sum(dst, axis=-1, keepdims=True)
                dfq_acc[i, hh] += jnp.sum(dst, axis=0, keepdims=True)

        @pl.when(i == nt - 1)
        def _():
            out_ref[0] = dq_acc[pl.ds(pl.multiple_of(j * t, t), t), :]
            out_ref[1] = dkv_acc[0]
            out_ref[2] = dkv_acc[1]
            df_ref[...] = df_acc[...]
            dfq_ref[...] = dfq_acc[j]

    qi = lambda j, i: jnp.maximum(i, j)
    return pl.pallas_call(
        body, name=name, grid=(n_pairs, nt, nt),
        in_specs=[pl.BlockSpec((t, wide), lambda p, j, i: (qi(j, i), p)),
                  pl.BlockSpec((t, wide), lambda p, j, i: (j, n_pairs + p)),
                  pl.BlockSpec((t, wide), lambda p, j, i: (j, 2 * n_pairs + p)),
                  pl.BlockSpec((hs, t, 1), lambda p, j, i: (p, j, 0)),
                  pl.BlockSpec((hs, 1, t), lambda p, j, i: (p, 0, qi(j, i))),
                  pl.BlockSpec((hs, 1, t), lambda p, j, i: (p, 0, qi(j, i))),
                  pl.BlockSpec((t, wide), lambda p, j, i: (qi(j, i), p)),
                  pl.BlockSpec((t, wide), lambda p, j, i: (qi(j, i), p))],
        out_specs=[pl.BlockSpec((3, t, wide), lambda p, j, i: (0, j, p)),
                   pl.BlockSpec((hs, t, 1), lambda p, j, i: (p, j, 0)),
                   pl.BlockSpec((hs, 1, t), lambda p, j, i: (p, 0, j))],
        out_shape=[jax.ShapeDtypeStruct((3, s, d), F32), jax.ShapeDtypeStruct((hs * n_pairs, s, 1), F32),
                   jax.ShapeDtypeStruct((hs * n_pairs, 1, s), F32)],
        scratch_shapes=[pltpu.VMEM((s, wide), F32), pltpu.VMEM((2, t, wide), F32), pltpu.VMEM((hs, t, 1), F32),
                        pltpu.VMEM((nt, hs, 1, t), F32)],
        compiler_params=_params(3),
    )(qkn, qkn, proj, f_col, f_row, lse_row, do, o)


def loss_head(y, target):
    s, d = y.shape
    tm = _row_tile(s)

    def body(y_ref, t_ref, dy_ref, loss_ref):
        @pl.when(pl.program_id(0) == 0)
        def _():
            loss_ref[...] = jnp.zeros_like(loss_ref)

        diff = y_ref[...] - t_ref[...]
        dy_ref[...] = diff * (1.0 / d)
        loss_ref[...] += 0.5 * jnp.sum(jnp.mean(diff * diff, axis=-1, keepdims=True), axis=0, keepdims=True)

    row = pl.BlockSpec((tm, d), lambda i: (i, 0))
    return pl.pallas_call(
        body, name="loss_head", grid=(s // tm,),
        in_specs=[row, row],
        out_specs=[row, pl.BlockSpec((8, LANES), lambda i: (0, 0))],
        out_shape=[jax.ShapeDtypeStruct((s, d), F32), jax.ShapeDtypeStruct((8, LANES), F32)],
        compiler_params=_params(1),
    )(y, target)


def ada_mod(c_all, w, b):
    n_layers, d, cols = w.shape

    def body(c_ref, w_ref, b_ref, o_ref):
        cv = c_ref[...]
        o_ref[...] = _nn(cv * _sigmoid(cv), w_ref[...]) + b_ref[...]

    return pl.pallas_call(
        body, name="ada_mod", grid=(n_layers,),
        in_specs=[pl.BlockSpec((N_DEV, d), lambda l: (0, 0)), pl.BlockSpec((None, d, cols), lambda l: (l, 0, 0)),
                  pl.BlockSpec((None, 1, cols), lambda l: (l, 0, 0))],
        out_specs=pl.BlockSpec((None, N_DEV, cols), lambda l: (l, 0, 0)),
        out_shape=jax.ShapeDtypeStruct((n_layers, N_DEV, cols), F32),
        compiler_params=_params(1),
    )(c_all, w, b)


def ada_grad(c_t, dmod):
    d = c_t.shape[0]
    n_layers, _, cols = dmod.shape
    tn = cols // 2

    def body(c_ref, dm_ref, o_ref):
        cv = c_ref[...]
        o_ref[...] = _nn(cv * _sigmoid(cv), dm_ref[...])

    return pl.pallas_call(
        body, name="ada_grad", grid=(n_layers, 2),
        in_specs=[pl.BlockSpec((d, N_DEV), lambda l, n: (0, 0)), pl.BlockSpec((None, N_DEV, tn), lambda l, n: (l, 0, n))],
        out_specs=pl.BlockSpec((None, d, tn), lambda l, n: (l, 0, n)),
        out_shape=jax.ShapeDtypeStruct((n_layers, d, cols), F32),
        compiler_params=_params(2),
    )(c_t, dmod)


def sum_devices(v):
    def body(v_ref, o_ref):
        acc = v_ref[0]
        for k in range(1, N_DEV):
            acc = acc + v_ref[k]
        o_ref[...] = acc

    return pl.pallas_call(body, name="sum_devices", out_shape=jax.ShapeDtypeStruct(v.shape[1:], F32))(v)


def sum_slots(r, own):
    _, rows, cols = r.shape
    tm = 256 if rows % 256 == 0 else rows

    def body(r_ref, own_ref, o_ref):
        o_ref[...] = ((own_ref[...].astype(F32) + r_ref[0].astype(F32)) + r_ref[1].astype(F32)) + r_ref[2].astype(F32)

    return pl.pallas_call(
        body, name="sum_slots", grid=(rows // tm,),
        in_specs=[pl.BlockSpec((N_CHIP - 1, tm, cols), lambda i: (0, i, 0)), pl.BlockSpec((tm, cols), lambda i: (i, 0))],
        out_specs=pl.BlockSpec((tm, cols), lambda i: (i, 0)),
        out_shape=jax.ShapeDtypeStruct((rows, cols), F32),
        compiler_params=_params(1),
    )(r, own)


def adamw(w, m, v, g, g2=None):
    rows, cols = w.shape
    tm = 256 if rows % 256 == 0 else rows
    two = g2 is not None
    c1 = 1.0 - ADAM_B1 ** ADAM_STEP
    c2 = 1.0 - ADAM_B2 ** ADAM_STEP

    def body(w_ref, m_ref, v_ref, g_ref, *rest):
        if two:
            g2_ref, go_ref, d_ref, mo_ref, vo_ref = rest
            gv = g_ref[...] + g2_ref[...]
        else:
            go_ref, d_ref, mo_ref, vo_ref = rest
            gv = g_ref[...]
        mn = ADAM_B1 * m_ref[...] + (1.0 - ADAM_B1) * gv
        vn = ADAM_B2 * v_ref[...] + (1.0 - ADAM_B2) * (gv * gv)
        go_ref[...] = gv
        mo_ref[...] = mn
        vo_ref[...] = vn
        d_ref[...] = -ADAM_LR * ((mn / c1) / (jnp.sqrt(vn / c2) + ADAM_EPS) + ADAM_WD * w_ref[...])

    blk = pl.BlockSpec((tm, cols), lambda i: (i, 0))
    ops = [w, m, v, g] + ([g2] if two else [])
    return pl.pallas_call(
        body, name="adamw", grid=(rows // tm,),
        in_specs=[blk] * len(ops), out_specs=[blk] * 4,
        out_shape=[jax.ShapeDtypeStruct((rows, cols), F32)] * 4,
        compiler_params=_params(1),
    )(*ops)


def _pad_rows(flat):
    n = flat.shape[0]
    rows = -(-n // LANES)
    return jnp.pad(flat, (0, rows * LANES - n)).reshape(rows, LANES)


def _pad_rows8(flat):
    rows = _pad_rows(flat)
    return jnp.pad(rows, ((0, -rows.shape[0] % 8), (0, 0)))


def _col_tiles(n):
    return next(k for k in range(1, n // LANES + 1) if n % (k * LANES) == 0 and n // k <= 1536)


def kernel(x, c, ada_w, ada_b, norm_g, ffn_w13, ffn_w2, rel_bias, swa_w_in, swa_w_out, swa_q_g, swa_k_g, swa_sink, fox_w_in, fox_w_out, fox_b_f, fox_q_g, fox_k_g, loss_target, m_ada_w, m_ada_b, m_norm_g, m_ffn_w13, m_ffn_w2, m_rel_bias, m_swa_w_in, m_swa_w_out, m_swa_q_g, m_swa_k_g, m_swa_sink, m_fox_w_in, m_fox_w_out, m_fox_b_f, m_fox_q_g, m_fox_k_g, v_ada_w, v_ada_b, v_norm_g, v_ffn_w13, v_ffn_w2, v_rel_bias, v_swa_w_in, v_swa_w_out, v_swa_q_g, v_swa_k_g, v_swa_sink, v_fox_w_in, v_fox_w_out, v_fox_b_f, v_fox_q_g, v_fox_k_g):
    ix, iy, ic = lax.axis_index("x"), lax.axis_index("y"), lax.axis_index("c")
    chip = 2 * ix + iy
    dev = 2 * chip + ic
    s, d = x.shape[1:]
    n_layers = ada_w.shape[0]
    n_a, n_b = swa_w_in.shape[0], fox_w_in.shape[0]
    hq = d // HEAD_DIM
    hkv = hq // GROUP
    kw = hkv * HEAD_DIM
    c13 = ffn_w13.shape[-1]
    f = 2 * c13
    r2 = ffn_w2.shape[2]
    cq = norm_g.shape[-1]
    a_in = d + 2 * kw
    fx = fox_w_in.shape[-1]
    b_in = N_CHIP * fx
    b_pad = 3 * d + LANES
    x0 = x[0]

    hello = _pad_rows8(jnp.concatenate([c.reshape(-1), norm_g.reshape(-1)]))
    hello_all = all_gather_rows(hello, "gather_c_norm").reshape(N_DEV, -1)
    c_all = hello_all[:, :d]
    ng = hello_all[::2, d:d + n_layers * 3 * cq].reshape(N_CHIP, n_layers, 3, cq)
    norm_full = jnp.moveaxis(ng, 0, 2).reshape(n_layers, 3, d)

    half_cols = ada_w.shape[-1] // 2
    w_half = lax.dynamic_slice_in_dim(ada_w, ic * half_cols, half_cols, axis=2)
    b_half = lax.dynamic_slice_in_dim(ada_b, dev * half_cols, half_cols, axis=1)[:, None, :]
    mod_part = ada_mod(c_all, w_half, b_half)
    mod_all = all_gather_rows(mod_part.reshape(n_layers * N_DEV, half_cols), "gather_mod")
    mod_all = mod_all.reshape(N_DEV, n_layers, N_DEV, half_cols)
    mod_mine = lax.dynamic_index_in_dim(mod_all, dev, axis=2, keepdims=False)
    mod_mine = jnp.moveaxis(mod_mine, 0, 1).reshape(n_layers, 3, 3, d)
    mv = jnp.concatenate([norm_full[:, :, None, :], mod_mine, jnp.zeros((n_layers, 3, 4, d), F32)], axis=2)

    def kinds_of(l):
        mixer_in = ("col", swa_w_in.shape[-1]) if l % 2 == 0 else ("slot",)
        return [("col", c13), ("row", r2), mixer_in, ("slot",)]

    def layer_buffers(l, after):
        raw = [ffn_w13[l], ffn_w2[l]] + ([swa_w_in[l // 2], swa_w_out[l // 2]] if l % 2 == 0 else
                                         [fox_w_in[l // 2], fox_w_out[l // 2]])
        return [cast_and_place(r, k, chip, after, f"place_weights_{l}_{t}") for t, (r, k) in enumerate(zip(raw, kinds_of(l)))]

    def layer_weights(l, fulls):
        w13_l, w2_l, w_in, w_out = fulls
        if l % 2 == 1:
            w_in = jnp.pad(jnp.concatenate([w_in[b] for b in range(N_CHIP)], axis=-1), ((0, 0), (0, b_pad - b_in)))
        return dict(w13=w13_l, w2=w2_l, w_in=w_in, w_out=w_out.reshape(d, d))

    weights = [layer_weights(0, gather_weights(layer_buffers(0, mv), kinds_of(0)))]
    in_flight = []
    for l in range(1, n_layers):
        in_flight.append(exchange_start([], layer_buffers(l, weights[0]["w13"]), kinds_of(l), True, f"gather_start_{l}"))
    for *_, token in in_flight:
        mv = mv + token[0, 0]

    bucket = jnp.asarray(_rel_bucket_table())
    bias = rel_bias_table(rel_bias, bucket)
    tm, tw = _row_tile(s), _wide_tile(s)
    n13 = 2 * f // c13
    na_t, nb_t = _col_tiles(a_in), _col_tiles(3 * d)
    wa_t, wb_t = a_in // na_t, 3 * d // nb_t
    gate_blk = 3 * d // LANES

    def ffn_forward(xv, l, half, sub):
        wg = weights[l]
        a, h = modmm(xv, mv, (l, sub), wg["w13"], (None, d, 2 * f), (half, 0, 0), 2 * f, c13, BF16,
                     f"ffn_up_{l}_{half}", True)
        xo, y, u = resmm(xv, mv, (l, sub), 0.5, a, wg["w2"], (None, f, d), (half, 0, 0), f, c13,
                         f"ffn_down_{l}_{half}", True)
        return xo, dict(x=xv, h=h, a=a, u=u, y=y)

    saved = []
    xv = x0
    for l in range(n_layers):
        j = l // 2
        if l >= 1:
            send_sems, recv_sems, shards, fulls, _ = in_flight[l - 1]
            weights.append(layer_weights(l, exchange_wait(send_sems, recv_sems, shards, fulls, xv, kinds_of(l), True,
                                                          f"gather_wait_{l}")[1]))
        wg = weights[l]
        xv, s0 = ffn_forward(xv, l, 0, 0)
        if l % 2 == 0:
            proj, h = modmm(xv, mv, (l, 1), wg["w_in"], (d, a_in), (0, 0), a_in, wa_t, BF16, f"swa_in_{j}", True)
            gains = jnp.concatenate([jnp.tile(swa_q_g[j] * HEAD_DIM ** -0.5, hq), jnp.tile(swa_k_g[j], hkv)])[None, :]
            qkn = qknorm_fwd(proj, gains, kw, f"swa_qknorm_{j}")
            sink = swa_sink[j][None, :]
            o, lse = swa_fwd(qkn, proj, bias, sink, d, f"swa_attn_{j}")
            s1 = dict(x=xv, h=h, proj=proj, gains=gains, qkn=qkn, sink=sink, o=o, lse=lse)
        else:
            proj, h = modmm(xv, mv, (l, 1), wg["w_in"], (d, 3 * d), (0, 0), 3 * d, wb_t, BF16, f"fox_in_{j}", True)
            fl, _ = modmm(xv, mv, (l, 1), wg["w_in"], (d, LANES), (0, gate_blk), LANES, LANES, F32,
                          f"fox_gate_in_{j}", False)
            b_f = jnp.pad(fox_b_f[j], (0, LANES - hq))[None, :]
            fcum = fox_gates(fl, b_f, f"fox_gates_{j}")
            f_t = fcum[:, :hq].T
            f_col, f_row = f_t[:, :, None], f_t[:, None, :]
            gains = jnp.concatenate([jnp.tile(fox_q_g[j] * HEAD_DIM ** -0.5, hq), jnp.tile(fox_k_g[j], hq)])[None, :]
            qkn = qknorm_fwd(proj, gains, d, f"fox_qknorm_{j}")
            o, o32, lse = fox_fwd(qkn, proj, f_col, f_row, d, f"fox_attn_{j}")
            s1 = dict(x=xv, h=h, proj=proj, gains=gains, qkn=qkn, fl=fl, b_f=b_f, f_col=f_col, f_row=f_row, o=o, o32=o32,
                      lse=lse)
        xv, y = resmm(xv, mv, (l, 1), 1.0, o, wg["w_out"], (d, d), (0, 0), d, d, f"mixer_out_{l}", False)
        s1["y"] = y
        xv, s2 = ffn_forward(xv, l, 1, 2)
        saved.append((s0, s1, s2))

    dxv, loss_part = loss_head(xv, loss_target[0])
    loss = lax.psum(loss_part[0, 0], ("x", "y", "c"))

    grads = [dict(w13=lax.empty((2, d, 2 * f), BF16), w2=lax.empty((2, f, d), BF16),
                  w_in=lax.empty((d, a_in if l % 2 == 0 else b_pad), BF16), w_out=lax.empty((d, d), BF16))
             for l in range(n_layers)]
    dmod = [[None] * 3 for _ in range(n_layers)]
    dnorm = [[None] * 3 for _ in range(n_layers)]
    dqk_gain = {}
    dsink, db_f, dbias_tabs = {}, {}, []

    def ffn_backward(dxo, sv, l, half, sub):
        wg, gg = weights[l], grads[l]
        dy, da, dgate = resmm_bwd(dxo, sv["y"], mv, (l, sub), 0.5, wg["w2"], (None, f, d), (half, 0, 0), f, c13,
                                  f"ffn_down_bwd_{l}_{half}", a=sv["a"])
        gg["w2"] = weight_grad(sv["u"], pl.BlockSpec((tw, c13), lambda m, n, k: (k, m)), dy,
                               pl.BlockSpec((tw, d), lambda m, n, k: (k, 0)), (f // c13, 1), s, c13, d, gg["w2"],
                               pl.BlockSpec((None, c13, d), lambda m, n, k: (half, m, 0)), f"ffn_w2_grad_{l}_{half}")
        gg["w13"] = weight_grad(sv["h"], pl.BlockSpec((tw, d), lambda m, n, k: (k, 0)), da,
                                pl.BlockSpec((tw, c13), lambda m, n, k: (k, n)), (1, n13), s, d, c13,
                                gg["w13"], pl.BlockSpec((None, d, c13), lambda m, n, k: (half, 0, n)),
                                f"ffn_w13_grad_{l}_{half}")
        dx, red = modmm_bwd(da, wg["w13"], (None, d, 2 * f), (half, 0, 0), 2 * f, c13, sv["x"], dxo, mv,
                            (l, sub), f"ffn_up_bwd_{l}_{half}")
        dmod[l][sub] = (red[1], red[2], dgate[0])
        dnorm[l][sub] = red[0]
        return dx

    def layer_slabs(l):
        gg = grads[l]
        g_in = gg["w_in"] if l % 2 == 0 else jnp.stack([gg["w_in"][:, b * fx:(b + 1) * fx] for b in range(N_CHIP)])
        return [gg["w13"], gg["w2"], g_in, gg["w_out"].reshape(N_CHIP, d // N_CHIP, d)]

    def shard_shapes_of(l):
        mixer = (swa_w_in, swa_w_out) if l % 2 == 0 else (fox_w_in, fox_w_out)
        return [ffn_w13.shape[1:], ffn_w2.shape[1:], mixer[0].shape[1:], mixer[1].shape[1:]]

    scatter_in_flight = {}

    for l in reversed(range(n_layers)):
        j = l // 2
        wg, gg = weights[l], grads[l]
        s0, s1, s2 = saved[l]
        dxv = ffn_backward(dxv, s2, l, 1, 2)
        is_a = l % 2 == 0
        w_out = wg["w_out"]
        dy, do, dgate = resmm_bwd(dxv, s1["y"], mv, (l, 1), 1.0, w_out, (d, d), (0, 0), d, d, f"mixer_out_bwd_{l}")
        gg["w_out"] = weight_grad(s1["o"], pl.BlockSpec((tw, d), lambda m, n, k: (k, 0)), dy,
                                  pl.BlockSpec((tw, d), lambda m, n, k: (k, 0)), (1, 1), s, d, d, gg["w_out"],
                                  pl.BlockSpec((d, d), lambda m, n, k: (0, 0)), f"mixer_out_grad_{l}")
        if is_a:
            d_qkv, dbias_tab, dsk = swa_bwd(s1["qkn"], s1["proj"], bias, s1["sink"], do, s1["o"], s1["lse"], d,
                                            f"swa_attn_bwd_{j}")
            dbias_tabs.append(dbias_tab)
            dsink[j] = dsk[0, :hq]
            dproj, dgain = qknorm_bwd(s1["proj"], s1["gains"], d_qkv, pl.BlockSpec((tm, kw), lambda c_, i: (i, c_)),
                                      a_in, kw, f"swa_qknorm_bwd_{j}")
            gg["w_in"] = weight_grad(s1["h"], pl.BlockSpec((tw, d), lambda m, n, k: (k, 0)), dproj,
                                     pl.BlockSpec((tw, wa_t), lambda m, n, k: (k, n)), (1, na_t), s, d, wa_t, gg["w_in"],
                                     pl.BlockSpec((d, wa_t), lambda m, n, k: (0, n)), f"swa_in_grad_{j}")
            dxv, red = modmm_bwd(dproj, wg["w_in"], (d, a_in), (0, 0), a_in, wa_t, s1["x"], dxv, mv, (l, 1),
                                 f"swa_in_bwd_{j}")
            dqk_gain[("a", j)] = (dgain[0, :d].reshape(hq, HEAD_DIM).sum(0) * HEAD_DIM ** -0.5,
                                  dgain[0, d:d + kw].reshape(hkv, HEAD_DIM).sum(0))
        else:
            lse_row = s1["lse"].reshape(hq, 1, s)
            d_qkv, df_col, dfq_row = fox_bwd(s1["qkn"], s1["proj"], s1["f_col"], s1["f_row"], lse_row, do, s1["o32"], d,
                                             f"fox_attn_bwd_{j}")
            lanes_of_heads = lambda a: jnp.pad(a.T, ((0, 0), (0, LANES - hq)))
            dfl, dbf = fox_gates_bwd(s1["fl"], s1["b_f"], lanes_of_heads(dfq_row[:, 0, :]), lanes_of_heads(df_col[:, :, 0]),
                                     f"fox_gates_bwd_{j}")
            db_f[j] = dbf[0, :hq]
            dproj, dgain = qknorm_bwd(s1["proj"], s1["gains"], d_qkv, pl.BlockSpec((None, tm, d), lambda c_, i: (c_, i, 0)),
                                      3 * d, d, f"fox_qknorm_bwd_{j}")
            gg["w_in"] = weight_grad(s1["h"], pl.BlockSpec((tw, d), lambda m, n, k: (k, 0)), dproj,
                                     pl.BlockSpec((tw, wb_t), lambda m, n, k: (k, n)), (1, nb_t), s, d, wb_t, gg["w_in"],
                                     pl.BlockSpec((d, wb_t), lambda m, n, k: (0, n)), f"fox_in_grad_{j}")
            gg["w_in"] = weight_grad(s1["h"], pl.BlockSpec((tw, d), lambda m, n, k: (k, 0)), dfl,
                                     pl.BlockSpec((tw, LANES), lambda m, n, k: (k, 0)), (1, 1), s, d, LANES, gg["w_in"],
                                     pl.BlockSpec((d, LANES), lambda m, n, k: (0, gate_blk)), f"fox_gate_in_grad_{j}")
            dxv, red = modmm_bwd(dproj, wg["w_in"], (d, 3 * d), (0, 0), 3 * d, wb_t, s1["x"], dxv, mv, (l, 1),
                                 f"fox_in_bwd_{j}", more=(dfl, (d, LANES), (0, gate_blk)))
            dqk_gain[("b", j)] = (dgain[0, :d].reshape(hq, HEAD_DIM).sum(0) * HEAD_DIM ** -0.5,
                                  dgain[0, d:2 * d].reshape(hq, HEAD_DIM).sum(0))
        dmod[l][1] = (red[1], red[2], dgate[0])
        dnorm[l][1] = red[0]
        dxv = ffn_backward(dxv, s0, l, 0, 0)
        if l >= 1:
            lands = [lax.empty((N_CHIP - 1,) + tuple(shp), BF16) for shp in shard_shapes_of(l)]
            scatter_in_flight[l] = exchange_start(layer_slabs(l), lands, kinds_of(l), False, f"scatter_start_{l}")
            mv = mv + scatter_in_flight[l][-1][0, 0]
    grad_x = dxv[None]

    drel = rel_bias_grad(jnp.stack(dbias_tabs), bucket)
    dmod_flat = jnp.stack([jnp.stack([jnp.stack(dmod[l][sub]) for sub in range(3)]) for l in range(n_layers)]).reshape(-1)
    dnorm_flat = jnp.stack([jnp.stack(dnorm[l]) for l in range(n_layers)]).reshape(-1)
    pieces = [dmod_flat, dnorm_flat,
              jnp.stack([dqk_gain[("a", j)][0] for j in range(n_a)]).reshape(-1),
              jnp.stack([dqk_gain[("a", j)][1] for j in range(n_a)]).reshape(-1),
              jnp.stack([dqk_gain[("b", j)][0] for j in range(n_b)]).reshape(-1),
              jnp.stack([dqk_gain[("b", j)][1] for j in range(n_b)]).reshape(-1),
              jnp.stack([dsink[j] for j in range(n_a)]).reshape(-1),
              jnp.stack([db_f[j] for j in range(n_b)]).reshape(-1),
              drel.reshape(-1)]
    rows = [_pad_rows(p) for p in pieces]
    starts = np.cumsum([0] + [r.shape[0] for r in rows])
    total = -(-int(starts[-1]) // 8) * 8
    small = jnp.pad(jnp.concatenate(rows), ((0, total - int(starts[-1])), (0, 0)))
    small_all = all_gather_rows(small, "gather_small_grads").reshape(N_DEV, total, LANES)
    small_sum = sum_devices(small_all)

    def piece(k, shape):
        n = int(np.prod(shape))
        return small_sum[int(starts[k]):int(starts[k + 1])].reshape(-1)[:n].reshape(shape)

    g_ada_b = piece(0, (n_layers, 9 * d))
    g_norm = lax.dynamic_slice_in_dim(piece(1, (n_layers, 3, d)), chip * cq, cq, axis=2)
    g_swa_q, g_swa_k = piece(2, (n_a, HEAD_DIM)), piece(3, (n_a, HEAD_DIM))
    g_fox_q, g_fox_k = piece(4, (n_b, HEAD_DIM)), piece(5, (n_b, HEAD_DIM))
    g_sink, g_bf, g_rel = piece(6, (n_a, hq)), piece(7, (n_b, hq)), piece(8, (REL_BUCKETS, hq))

    dmod_all = small_all[:, :int(starts[1])].reshape(N_DEV, -1)[:, :n_layers * 9 * d].reshape(N_DEV, n_layers, 9 * d)
    ada_cols = ada_w.shape[-1]
    dmod_mine = lax.dynamic_slice_in_dim(jnp.moveaxis(dmod_all, 0, 1), chip * ada_cols, ada_cols, axis=2)
    g_ada_w = ada_grad(c_all.T, dmod_mine)

    sources = {0: layer_slabs(0)}
    landed = {0: scatter_grads(sources[0], kinds_of(0), shard_shapes_of(0))}
    for l, (send_sems, recv_sems, slabs, lands, _) in scatter_in_flight.items():
        sources[l], landed[l] = exchange_wait(send_sems, recv_sems, slabs, lands, landed[0][0], kinds_of(l), False,
                                              f"scatter_wait_{l}")

    def layer_sum(l, t):
        shape = shard_shapes_of(l)[t]
        own = own_slab(sources[l][t], kinds_of(l)[t], chip, shape)
        return sum_slots(landed[l][t].reshape(N_CHIP - 1, -1, shape[-1]), own.reshape(-1, shape[-1]))

    every, even, odd = range(n_layers), range(0, n_layers, 2), range(1, n_layers, 2)
    parts = [jnp.concatenate([layer_sum(l, t) for l in layers])
             for layers, t in [(every, 0), (every, 1), (even, 2), (even, 3), (odd, 2), (odd, 3)]]
    others = swap_with_sibling(parts)

    def update(w, m, v, g, g2=None):
        w2d = w.reshape(-1, w.shape[-1])
        outs = adamw(w2d, m.reshape(w2d.shape), v.reshape(w2d.shape), g.reshape(w2d.shape) if g2 is None else g, g2)
        return [t.reshape(w.shape) for t in outs]

    big = [(ffn_w13, m_ffn_w13, v_ffn_w13), (ffn_w2, m_ffn_w2, v_ffn_w2), (swa_w_in, m_swa_w_in, v_swa_w_in),
           (swa_w_out, m_swa_w_out, v_swa_w_out), (fox_w_in, m_fox_w_in, v_fox_w_in), (fox_w_out, m_fox_w_out, v_fox_w_out)]
    big_out = [update(w, m, v, p, q) for (w, m, v), p, q in zip(big, parts, others)]
    r_ada_w = update(ada_w, m_ada_w, v_ada_w, g_ada_w)
    r_ada_b = update(ada_b, m_ada_b, v_ada_b, g_ada_b)
    r_norm = update(norm_g, m_norm_g, v_norm_g, g_norm)
    r_rel = update(rel_bias, m_rel_bias, v_rel_bias, g_rel)
    r_swa_q = update(swa_q_g, m_swa_q_g, v_swa_q_g, g_swa_q)
    r_swa_k = update(swa_k_g, m_swa_k_g, v_swa_k_g, g_swa_k)
    r_sink = update(swa_sink, m_swa_sink, v_swa_sink, g_sink)
    r_bf = update(fox_b_f, m_fox_b_f, v_fox_b_f, g_bf)
    r_fox_q = update(fox_q_g, m_fox_q_g, v_fox_q_g, g_fox_q)
    r_fox_k = update(fox_k_g, m_fox_k_g, v_fox_k_g, g_fox_k)
    per_weight = [r_ada_w, r_ada_b, r_norm, big_out[0], big_out[1], r_rel, big_out[2], big_out[3], r_swa_q, r_swa_k,
                  r_sink, big_out[4], big_out[5], r_bf, r_fox_q, r_fox_k]
    return (loss, grad_x, *[r[0] for r in per_weight], *[r[1] for r in per_weight],
            *[r[2] for r in per_weight], *[r[3] for r in per_weight])
```
